```python
import math
import jax, jax.numpy as jnp
from jax import lax
import numpy as np

D_MODEL = 1024
BATCH = 8
SEQ = 8192
DEPTH = 2

CHUNK = 64
D_FF = 2816
BRANCH_W = 512
N_BRANCH = 3
CONV_W = 3
H_RET = 4
DK_RET = BRANCH_W // H_RET
DV_RET = BRANCH_W // H_RET
H_ATT = 8
DH_ATT = BRANCH_W // H_ATT
N_PREV_CHUNKS = 8
BAND = (N_PREV_CHUNKS + 1) * CHUNK
REL_CLIP = 128
N_REL = 2 * REL_CLIP + 1
IN_COLS = 3 * BRANCH_W + 4 * BRANCH_W + 3 * BRANCH_W
EPS = 1e-6
NEG_INF = -1e30
ROPE_BASE = 10000.0

kernel_name = "hybrid_gated_conv_retention_chunkattn_macaron"


def _rmsnorm(x, w):
    xf = x.astype(jnp.float32)
    xf = xf * lax.rsqrt(jnp.mean(xf * xf, axis=-1, keepdims=True) + EPS)
    return (xf * w.astype(jnp.float32)).astype(x.dtype)


def _swiglu(h, w_gate, w_up, w_down):
    return (jax.nn.silu(h @ w_gate) * (h @ w_up)) @ w_down


def _short_gated_conv(u, b_gate, c_gate, conv_w):
    z = c_gate * u
    zp = jnp.pad(z, ((0, 0), (CONV_W - 1, 0), (0, 0)))
    s = z.shape[1]
    conv = sum(conv_w[j] * zp[:, j:j + s] for j in range(CONV_W))
    return b_gate * conv


def _rotary(x, cos, sin):
    half = x.shape[-1] // 2
    x1, x2 = x[..., :half], x[..., half:]
    c = cos[None, :, None, :]
    s_ = sin[None, :, None, :]
    return jnp.concatenate([x1 * c - x2 * s_, x1 * s_ + x2 * c], axis=-1).astype(x.dtype)


def _retention(q, k, v, g):
    bsz, s = q.shape[:2]
    nc = s // CHUNK
    log_gamma = jnp.log1p(-jnp.exp2(-5.0 - jnp.arange(H_RET, dtype=jnp.float32)))
    pos = jnp.arange(CHUNK, dtype=jnp.float32)
    d_intra = jnp.exp(log_gamma[:, None, None] * jnp.abs(pos[:, None] - pos[None, :]))
    q_decay = jnp.exp(log_gamma[:, None] * (pos + 1.0))
    k_decay = jnp.exp(log_gamma[:, None] * (CHUNK - 1.0 - pos))
    chunk_decay = jnp.exp(log_gamma * CHUNK)

    def to_chunks(t):
        return t.astype(jnp.float32).reshape(bsz, nc, CHUNK, H_RET, -1).transpose(1, 0, 3, 2, 4)

    qc = to_chunks(q) * (DK_RET ** -0.5)
    kc, vc = to_chunks(k), to_chunks(v)

    def step(state, qkv):
        qb, kb, vb = qkv
        inner = jnp.einsum('bhnk,bhmk->bhnm', qb, kb) * d_intra[None]
        o = jnp.einsum('bhnm,bhmv->bhnv', inner, vb) \
            + jnp.einsum('bhnk,bhkv->bhnv', qb * q_decay[None, :, :, None], state)
        state = state * chunk_decay[None, :, None, None] \
            + jnp.einsum('bhmk,bhmv->bhkv', kb * k_decay[None, :, :, None], vb)
        return state, o

    s0 = jnp.zeros((bsz, H_RET, DK_RET, DV_RET), jnp.float32)
    _, o = lax.scan(step, s0, (qc, kc, vc))
    o = o.transpose(1, 0, 3, 2, 4).reshape(bsz, s, H_RET, DV_RET)
    o = o * lax.rsqrt(jnp.mean(o * o, axis=-1, keepdims=True) + EPS)
    o = o.reshape(bsz, s, BRANCH_W).astype(g.dtype)
    return jax.nn.silu(g) * o


def _chunk_band_attention(q, k, v, rel_bias):
    bsz, s = q.shape[:2]
    nc = s // CHUNK
    pad = N_PREV_CHUNKS * CHUNK
    qc = q.reshape(bsz, nc, CHUNK, H_ATT, DH_ATT).transpose(1, 0, 3, 2, 4)
    kp = jnp.pad(k, ((0, 0), (pad, 0), (0, 0), (0, 0))).transpose(0, 2, 1, 3)
    vp = jnp.pad(v, ((0, 0), (pad, 0), (0, 0), (0, 0))).transpose(0, 2, 1, 3)
    n = jnp.arange(CHUNK)
    m = jnp.arange(BAND)
    dist = (pad + n)[:, None] - m[None, :]
    idx = jnp.clip(dist, -REL_CLIP, REL_CLIP) + REL_CLIP
    bias = rel_bias[:, idx].astype(jnp.float32)
    scale = DH_ATT ** -0.5

    def one_chunk(args):
        c, q_blk = args
        kb = lax.dynamic_slice_in_dim(kp, c * CHUNK, BAND, axis=2)
        vb = lax.dynamic_slice_in_dim(vp, c * CHUNK, BAND, axis=2)
        sc = jnp.einsum('bhnd,bhmd->bhnm', q_blk, kb).astype(jnp.float32) * scale + bias[None]
        valid = m >= (N_PREV_CHUNKS - c) * CHUNK
        sc = jnp.where(valid[None, None, None, :], sc, NEG_INF)
        p = jax.nn.softmax(sc, axis=-1).astype(vb.dtype)
        return jnp.einsum('bhnm,bhmd->bhnd', p, vb)

    out = lax.map(one_chunk, (jnp.arange(nc), qc))
    return out.transpose(1, 0, 3, 2, 4).reshape(bsz, s, BRANCH_W)


def _fwd_setup_inputs(seed: int = 0) -> dict:
    key = jax.random.key(seed)
    ks = jax.random.split(key, 20)
    f32 = jnp.float32

    def w(k, shape, fan_in):
        return jax.random.normal(k, shape, f32) * (fan_in ** -0.5)

    def gain(k, shape):
        return 1.0 + 0.05 * jax.random.normal(k, shape, f32)

    return {
        "x": jax.random.normal(ks[0], (BATCH, SEQ, D_MODEL), f32),
        "ffn1_norm": gain(ks[1], (DEPTH, D_MODEL)),
        "ffn1_w_gate": w(ks[2], (DEPTH, D_MODEL, D_FF), D_MODEL),
        "ffn1_w_up": w(ks[3], (DEPTH, D_MODEL, D_FF), D_MODEL),
        "ffn1_w_down": w(ks[4], (DEPTH, D_FF, D_MODEL), D_FF),
        "mix_norm": gain(ks[5], (DEPTH, D_MODEL)),
        "w_in": w(ks[6], (DEPTH, D_MODEL, IN_COLS), D_MODEL),
        "conv_w": w(ks[7], (DEPTH, CONV_W, BRANCH_W), CONV_W),
        "rel_bias": 0.5 * jax.random.normal(ks[8], (DEPTH, H_ATT, N_REL), f32),
        "w_branch": w(ks[9], (DEPTH, N_BRANCH, BRANCH_W, D_MODEL), BRANCH_W),
        "w_merge_gate": w(ks[10], (DEPTH, N_BRANCH, D_MODEL, D_MODEL), D_MODEL),
        "w_out": w(ks[11], (DEPTH, D_MODEL, D_MODEL), D_MODEL),
        "ffn2_norm": gain(ks[12], (DEPTH, D_MODEL)),
        "ffn2_w_gate": w(ks[13], (DEPTH, D_MODEL, D_FF), D_MODEL),
        "ffn2_w_up": w(ks[14], (DEPTH, D_MODEL, D_FF), D_MODEL),
        "ffn2_w_down": w(ks[15], (DEPTH, D_FF, D_MODEL), D_FF),
        "final_norm": gain(ks[16], (D_MODEL,)),
    }


def _fwd_reference(x, ffn1_norm, ffn1_w_gate, ffn1_w_up, ffn1_w_down, mix_norm, w_in, conv_w,
              rel_bias, w_branch, w_merge_gate, w_out, ffn2_norm, ffn2_w_gate, ffn2_w_up,
              ffn2_w_down, final_norm):
    bsz, s, _ = x.shape
    inv_freq = ROPE_BASE ** (-jnp.linspace(0.0, 1.0, DK_RET // 2, dtype=jnp.float32))
    ang = jnp.arange(s, dtype=jnp.float32)[:, None] * inv_freq[None, :]
    cos, sin = jnp.cos(ang), jnp.sin(ang)
    split_pts = [BRANCH_W * i for i in range(1, IN_COLS // BRANCH_W)]

    for l in range(DEPTH):
        x = x + 0.5 * _swiglu(_rmsnorm(x, ffn1_norm[l]), ffn1_w_gate[l], ffn1_w_up[l], ffn1_w_down[l])

        h = _rmsnorm(x, mix_norm[l])
        cols = jnp.split(h @ w_in[l], split_pts, axis=-1)
        cu, cb, cc, rq, rk, rv, rg, aq, ak, av = cols

        y_conv = _short_gated_conv(cu, cb, cc, conv_w[l])

        rq = _rotary(rq.reshape(bsz, s, H_RET, DK_RET), cos, sin)
        rk = _rotary(rk.reshape(bsz, s, H_RET, DK_RET), cos, sin)
        y_ret = _retention(rq, rk, rv.reshape(bsz, s, H_RET, DV_RET), rg)

        y_att = _chunk_band_attention(aq.reshape(bsz, s, H_ATT, DH_ATT),
                                      ak.reshape(bsz, s, H_ATT, DH_ATT),
                                      av.reshape(bsz, s, H_ATT, DH_ATT), rel_bias[l])

        merged = sum(jax.nn.sigmoid(h @ w_merge_gate[l, i]) * (y @ w_branch[l, i])
                     for i, y in enumerate((y_conv, y_ret, y_att)))
        x = x + merged @ w_out[l]

        x = x + 0.5 * _swiglu(_rmsnorm(x, ffn2_norm[l]), ffn2_w_gate[l], ffn2_w_up[l], ffn2_w_down[l])

    return _rmsnorm(x, final_norm)


import jax as _jax
import jax.numpy as _jnp

TWIN_FORMAT = 'train_step'
FWD_PARAMS = ['x', 'ffn1_norm', 'ffn1_w_gate', 'ffn1_w_up', 'ffn1_w_down', 'mix_norm', 'w_in', 'conv_w', 'rel_bias', 'w_branch', 'w_merge_gate', 'w_out', 'ffn2_norm', 'ffn2_w_gate', 'ffn2_w_up', 'ffn2_w_down', 'final_norm']
TWIN_WEIGHTS = ['ffn1_norm', 'ffn1_w_gate', 'ffn1_w_up', 'ffn1_w_down', 'mix_norm', 'w_in', 'conv_w', 'rel_bias', 'w_branch', 'w_merge_gate', 'w_out', 'ffn2_norm', 'ffn2_w_gate', 'ffn2_w_up', 'ffn2_w_down', 'final_norm']
TWIN_DIFF_INPUT = 'x'
TWIN_INPUTS = ['x', 'ffn1_norm', 'ffn1_w_gate', 'ffn1_w_up', 'ffn1_w_down', 'mix_norm', 'w_in', 'conv_w', 'rel_bias', 'w_branch', 'w_merge_gate', 'w_out', 'ffn2_norm', 'ffn2_w_gate', 'ffn2_w_up', 'ffn2_w_down', 'final_norm', 'loss_target', 'm_ffn1_norm', 'm_ffn1_w_gate', 'm_ffn1_w_up', 'm_ffn1_w_down', 'm_mix_norm', 'm_w_in', 'm_conv_w', 'm_rel_bias', 'm_w_branch', 'm_w_merge_gate', 'm_w_out', 'm_ffn2_norm', 'm_ffn2_w_gate', 'm_ffn2_w_up', 'm_ffn2_w_down', 'm_final_norm', 'v_ffn1_norm', 'v_ffn1_w_gate', 'v_ffn1_w_up', 'v_ffn1_w_down', 'v_mix_norm', 'v_w_in', 'v_conv_w', 'v_rel_bias', 'v_w_branch', 'v_w_merge_gate', 'v_w_out', 'v_ffn2_norm', 'v_ffn2_w_gate', 'v_ffn2_w_up', 'v_ffn2_w_down', 'v_final_norm']
TWIN_OUTPUTS = ['loss', 'grad_x', 'grad_ffn1_norm', 'grad_ffn1_w_gate', 'grad_ffn1_w_up', 'grad_ffn1_w_down', 'grad_mix_norm', 'grad_w_in', 'grad_conv_w', 'grad_rel_bias', 'grad_w_branch', 'grad_w_merge_gate', 'grad_w_out', 'grad_ffn2_norm', 'grad_ffn2_w_gate', 'grad_ffn2_w_up', 'grad_ffn2_w_down', 'grad_final_norm', 'delta_ffn1_norm', 'delta_ffn1_w_gate', 'delta_ffn1_w_up', 'delta_ffn1_w_down', 'delta_mix_norm', 'delta_w_in', 'delta_conv_w', 'delta_rel_bias', 'delta_w_branch', 'delta_w_merge_gate', 'delta_w_out', 'delta_ffn2_norm', 'delta_ffn2_w_gate', 'delta_ffn2_w_up', 'delta_ffn2_w_down', 'delta_final_norm', 'new_m_ffn1_norm', 'new_m_ffn1_w_gate', 'new_m_ffn1_w_up', 'new_m_ffn1_w_down', 'new_m_mix_norm', 'new_m_w_in', 'new_m_conv_w', 'new_m_rel_bias', 'new_m_w_branch', 'new_m_w_merge_gate', 'new_m_w_out', 'new_m_ffn2_norm', 'new_m_ffn2_w_gate', 'new_m_ffn2_w_up', 'new_m_ffn2_w_down', 'new_m_final_norm', 'new_v_ffn1_norm', 'new_v_ffn1_w_gate', 'new_v_ffn1_w_up', 'new_v_ffn1_w_down', 'new_v_mix_norm', 'new_v_w_in', 'new_v_conv_w', 'new_v_rel_bias', 'new_v_w_branch', 'new_v_w_merge_gate', 'new_v_w_out', 'new_v_ffn2_norm', 'new_v_ffn2_w_gate', 'new_v_ffn2_w_up', 'new_v_ffn2_w_down', 'new_v_final_norm']
TWIN_LEAF_KINDS = {'loss': 'loss', 'grad_x': 'grad_x', 'grad_ffn1_norm': 'grad_w', 'grad_ffn1_w_gate': 'grad_w', 'grad_ffn1_w_up': 'grad_w', 'grad_ffn1_w_down': 'grad_w', 'grad_mix_norm': 'grad_w', 'grad_w_in': 'grad_w', 'grad_conv_w': 'grad_w', 'grad_rel_bias': 'grad_w', 'grad_w_branch': 'grad_w', 'grad_w_merge_gate': 'grad_w', 'grad_w_out': 'grad_w', 'grad_ffn2_norm': 'grad_w', 'grad_ffn2_w_gate': 'grad_w', 'grad_ffn2_w_up': 'grad_w', 'grad_ffn2_w_down': 'grad_w', 'grad_final_norm': 'grad_w', 'delta_ffn1_norm': 'delta_w', 'delta_ffn1_w_gate': 'delta_w', 'delta_ffn1_w_up': 'delta_w', 'delta_ffn1_w_down': 'delta_w', 'delta_mix_norm': 'delta_w', 'delta_w_in': 'delta_w', 'delta_conv_w': 'delta_w', 'delta_rel_bias': 'delta_w', 'delta_w_branch': 'delta_w', 'delta_w_merge_gate': 'delta_w', 'delta_w_out': 'delta_w', 'delta_ffn2_norm': 'delta_w', 'delta_ffn2_w_gate': 'delta_w', 'delta_ffn2_w_up': 'delta_w', 'delta_ffn2_w_down': 'delta_w', 'delta_final_norm': 'delta_w', 'new_m_ffn1_norm': 'new_m', 'new_m_ffn1_w_gate': 'new_m', 'new_m_ffn1_w_up': 'new_m', 'new_m_ffn1_w_down': 'new_m', 'new_m_mix_norm': 'new_m', 'new_m_w_in': 'new_m', 'new_m_conv_w': 'new_m', 'new_m_rel_bias': 'new_m', 'new_m_w_branch': 'new_m', 'new_m_w_merge_gate': 'new_m', 'new_m_w_out': 'new_m', 'new_m_ffn2_norm': 'new_m', 'new_m_ffn2_w_gate': 'new_m', 'new_m_ffn2_w_up': 'new_m', 'new_m_ffn2_w_down': 'new_m', 'new_m_final_norm': 'new_m', 'new_v_ffn1_norm': 'new_v', 'new_v_ffn1_w_gate': 'new_v', 'new_v_ffn1_w_up': 'new_v', 'new_v_ffn1_w_down': 'new_v', 'new_v_mix_norm': 'new_v', 'new_v_w_in': 'new_v', 'new_v_conv_w': 'new_v', 'new_v_rel_bias': 'new_v', 'new_v_w_branch': 'new_v', 'new_v_w_merge_gate': 'new_v', 'new_v_w_out': 'new_v', 'new_v_ffn2_norm': 'new_v', 'new_v_ffn2_w_gate': 'new_v', 'new_v_ffn2_w_up': 'new_v', 'new_v_ffn2_w_down': 'new_v', 'new_v_final_norm': 'new_v'}


def _forward(args):
    return _fwd_reference(*[args[k] for k in FWD_PARAMS])


def _output_shape():
    out = _jax.eval_shape(lambda: _forward(_fwd_setup_inputs(0)))
    return out.shape, out.dtype

N_MICROBATCH = 1
ADAM_LR = 0.001
ADAM_B1 = 0.9
ADAM_B2 = 0.999
ADAM_EPS = 1e-08
ADAM_WD = 0.01
ADAM_STEP = 10
PER_EXAMPLE_BATCH_AXIS = {'x': 0, 'loss_target': 0}
SHARED_INPUTS = []
_WEIGHT_DTYPES = {'ffn1_norm': _jnp.float32, 'ffn1_w_gate': _jnp.float32, 'ffn1_w_up': _jnp.float32, 'ffn1_w_down': _jnp.float32, 'mix_norm': _jnp.float32, 'w_in': _jnp.float32, 'conv_w': _jnp.float32, 'rel_bias': _jnp.float32, 'w_branch': _jnp.float32, 'w_merge_gate': _jnp.float32, 'w_out': _jnp.float32, 'ffn2_norm': _jnp.float32, 'ffn2_w_gate': _jnp.float32, 'ffn2_w_up': _jnp.float32, 'ffn2_w_down': _jnp.float32, 'final_norm': _jnp.float32}
MOMENT_SCALE = {'ffn1_norm': 1.461140e-01, 'ffn1_w_gate': 6.242929e-02, 'ffn1_w_up': 6.031747e-02, 'ffn1_w_down': 1.002012e-01, 'mix_norm': 2.759599e-01, 'w_in': 1.181921e-01, 'conv_w': 1.833057e-01, 'rel_bias': 1.261872e-02, 'w_branch': 8.404826e-02, 'w_merge_gate': 3.283174e-02, 'w_out': 1.455899e-01, 'ffn2_norm': 9.809366e-02, 'ffn2_w_gate': 4.074384e-02, 'ffn2_w_up': 3.963922e-02, 'ffn2_w_down': 6.572631e-02, 'final_norm': 6.395245e+01}


def _to_microbatches(a, axis):
    t = _jnp.moveaxis(a, axis, 0)
    t = t.reshape((N_MICROBATCH, t.shape[0] // N_MICROBATCH) + t.shape[1:])
    return _jnp.moveaxis(t, 1, axis + 1)


def setup_inputs(seed: int = 0) -> dict:
    inp = _fwd_setup_inputs(seed)
    key = _jax.random.fold_in(_jax.random.key(seed), 7919)
    shape, _ = _output_shape()
    out = dict(inp)
    out["loss_target"] = _jax.random.normal(_jax.random.fold_in(key, 0), shape, _jnp.float32)
    for i, name in enumerate(TWIN_WEIGHTS):
        w = inp[name].astype(_jnp.float32)
        if MOMENT_SCALE is None:
            s = _jnp.sqrt(_jnp.mean(_jnp.square(w)) + 1e-30)
        else:
            s = MOMENT_SCALE[name]
        km, kv = _jax.random.split(_jax.random.fold_in(key, i + 1))
        out[name] = w
        out["m_" + name] = s * _jax.random.normal(km, w.shape, _jnp.float32)
        out["v_" + name] = (s * s) * _jax.random.uniform(kv, w.shape, _jnp.float32, 0.5, 1.5)
    if N_MICROBATCH > 1:
        for name, axis in PER_EXAMPLE_BATCH_AXIS.items():
            out[name] = _to_microbatches(out[name], axis)
    return {'x': out['x'], 'ffn1_norm': out['ffn1_norm'], 'ffn1_w_gate': out['ffn1_w_gate'], 'ffn1_w_up': out['ffn1_w_up'], 'ffn1_w_down': out['ffn1_w_down'], 'mix_norm': out['mix_norm'], 'w_in': out['w_in'], 'conv_w': out['conv_w'], 'rel_bias': out['rel_bias'], 'w_branch': out['w_branch'], 'w_merge_gate': out['w_merge_gate'], 'w_out': out['w_out'], 'ffn2_norm': out['ffn2_norm'], 'ffn2_w_gate': out['ffn2_w_gate'], 'ffn2_w_up': out['ffn2_w_up'], 'ffn2_w_down': out['ffn2_w_down'], 'final_norm': out['final_norm'], 'loss_target': out['loss_target'], 'm_ffn1_norm': out['m_ffn1_norm'], 'm_ffn1_w_gate': out['m_ffn1_w_gate'], 'm_ffn1_w_up': out['m_ffn1_w_up'], 'm_ffn1_w_down': out['m_ffn1_w_down'], 'm_mix_norm': out['m_mix_norm'], 'm_w_in': out['m_w_in'], 'm_conv_w': out['m_conv_w'], 'm_rel_bias': out['m_rel_bias'], 'm_w_branch': out['m_w_branch'], 'm_w_merge_gate': out['m_w_merge_gate'], 'm_w_out': out['m_w_out'], 'm_ffn2_norm': out['m_ffn2_norm'], 'm_ffn2_w_gate': out['m_ffn2_w_gate'], 'm_ffn2_w_up': out['m_ffn2_w_up'], 'm_ffn2_w_down': out['m_ffn2_w_down'], 'm_final_norm': out['m_final_norm'], 'v_ffn1_norm': out['v_ffn1_norm'], 'v_ffn1_w_gate': out['v_ffn1_w_gate'], 'v_ffn1_w_up': out['v_ffn1_w_up'], 'v_ffn1_w_down': out['v_ffn1_w_down'], 'v_mix_norm': out['v_mix_norm'], 'v_w_in': out['v_w_in'], 'v_conv_w': out['v_conv_w'], 'v_rel_bias': out['v_rel_bias'], 'v_w_branch': out['v_w_branch'], 'v_w_merge_gate': out['v_w_merge_gate'], 'v_w_out': out['v_w_out'], 'v_ffn2_norm': out['v_ffn2_norm'], 'v_ffn2_w_gate': out['v_ffn2_w_gate'], 'v_ffn2_w_up': out['v_ffn2_w_up'], 'v_ffn2_w_down': out['v_ffn2_w_down'], 'v_final_norm': out['v_final_norm']}


def _loss(weights, diff, rest, loss_target):
    with _jax.named_scope("forward"):
        args = {**rest, TWIN_DIFF_INPUT: diff, **{k: w.astype(_WEIGHT_DTYPES[k]) for k, w in weights.items()}}
        y = _forward(args)
    with _jax.named_scope("loss_head"):
        err = _jnp.square(y.astype(_jnp.float32) - loss_target)
        return 0.5 * _jnp.sum(_jnp.mean(err, axis=-1)) if err.ndim else 0.5 * err


def _adamw(w, g, m, v):
    m = ADAM_B1 * m + (1.0 - ADAM_B1) * g
    v = ADAM_B2 * v + (1.0 - ADAM_B2) * _jnp.square(g)
    m_hat = m / (1.0 - ADAM_B1 ** ADAM_STEP)
    v_hat = v / (1.0 - ADAM_B2 ** ADAM_STEP)
    delta = -ADAM_LR * (m_hat / (_jnp.sqrt(v_hat) + ADAM_EPS) + ADAM_WD * w)
    return delta, m, v


def reference(x, ffn1_norm, ffn1_w_gate, ffn1_w_up, ffn1_w_down, mix_norm, w_in, conv_w, rel_bias, w_branch, w_merge_gate, w_out, ffn2_norm, ffn2_w_gate, ffn2_w_up, ffn2_w_down, final_norm, loss_target, m_ffn1_norm, m_ffn1_w_gate, m_ffn1_w_up, m_ffn1_w_down, m_mix_norm, m_w_in, m_conv_w, m_rel_bias, m_w_branch, m_w_merge_gate, m_w_out, m_ffn2_norm, m_ffn2_w_gate, m_ffn2_w_up, m_ffn2_w_down, m_final_norm, v_ffn1_norm, v_ffn1_w_gate, v_ffn1_w_up, v_ffn1_w_down, v_mix_norm, v_w_in, v_conv_w, v_rel_bias, v_w_branch, v_w_merge_gate, v_w_out, v_ffn2_norm, v_ffn2_w_gate, v_ffn2_w_up, v_ffn2_w_down, v_final_norm):
    given = dict(x=x, ffn1_norm=ffn1_norm, ffn1_w_gate=ffn1_w_gate, ffn1_w_up=ffn1_w_up, ffn1_w_down=ffn1_w_down, mix_norm=mix_norm, w_in=w_in, conv_w=conv_w, rel_bias=rel_bias, w_branch=w_branch, w_merge_gate=w_merge_gate, w_out=w_out, ffn2_norm=ffn2_norm, ffn2_w_gate=ffn2_w_gate, ffn2_w_up=ffn2_w_up, ffn2_w_down=ffn2_w_down, final_norm=final_norm, loss_target=loss_target, m_ffn1_norm=m_ffn1_norm, m_ffn1_w_gate=m_ffn1_w_gate, m_ffn1_w_up=m_ffn1_w_up, m_ffn1_w_down=m_ffn1_w_down, m_mix_norm=m_mix_norm, m_w_in=m_w_in, m_conv_w=m_conv_w, m_rel_bias=m_rel_bias, m_w_branch=m_w_branch, m_w_merge_gate=m_w_merge_gate, m_w_out=m_w_out, m_ffn2_norm=m_ffn2_norm, m_ffn2_w_gate=m_ffn2_w_gate, m_ffn2_w_up=m_ffn2_w_up, m_ffn2_w_down=m_ffn2_w_down, m_final_norm=m_final_norm, v_ffn1_norm=v_ffn1_norm, v_ffn1_w_gate=v_ffn1_w_gate, v_ffn1_w_up=v_ffn1_w_up, v_ffn1_w_down=v_ffn1_w_down, v_mix_norm=v_mix_norm, v_w_in=v_w_in, v_conv_w=v_conv_w, v_rel_bias=v_rel_bias, v_w_branch=v_w_branch, v_w_merge_gate=v_w_merge_gate, v_w_out=v_w_out, v_ffn2_norm=v_ffn2_norm, v_ffn2_w_gate=v_ffn2_w_gate, v_ffn2_w_up=v_ffn2_w_up, v_ffn2_w_down=v_ffn2_w_down, v_final_norm=v_final_norm)
    weights = {n: given[n] for n in TWIN_WEIGHTS}
    shared = {n: given[n] for n in SHARED_INPUTS}
    per_example = {n: given[n] for n in ['x']}
    grad_fn = _jax.value_and_grad(_loss, argnums=(0, 1))

    def one_microbatch(ex, loss_target):
        ex = dict(ex)
        diff = ex.pop(TWIN_DIFF_INPUT)
        return grad_fn(weights, diff, {**shared, **ex}, loss_target)

    if N_MICROBATCH == 1:
        loss, (grad_w, grad_x) = one_microbatch(per_example, given["loss_target"])
    else:
        def body(carry, xs):
            loss_sum, grad_sum = carry
            l_k, (gw_k, gx_k) = one_microbatch(xs[0], xs[1])
            with _jax.named_scope("update"):
                return (loss_sum + l_k, _jax.tree.map(_jnp.add, grad_sum, gw_k)), gx_k

        init = (_jnp.zeros((), _jnp.float32), _jax.tree.map(_jnp.zeros_like, weights))
        (loss, grad_w), grad_x = _jax.lax.scan(body, init, (per_example, given["loss_target"]))
    with _jax.named_scope("update"):
        delta_w, new_m, new_v = {}, {}, {}
        for n in TWIN_WEIGHTS:
            delta_w[n], new_m[n], new_v[n] = _adamw(weights[n], grad_w[n], given["m_" + n], given["v_" + n])
    return (loss, grad_x, *[grad_w[n] for n in TWIN_WEIGHTS], *[delta_w[n] for n in TWIN_WEIGHTS],
            *[new_m[n] for n in TWIN_WEIGHTS], *[new_v[n] for n in TWIN_WEIGHTS])
```

```python
import functools
import math

import jax
import jax.numpy as jnp
from jax import lax
from jax.experimental import pallas as pl
from jax.experimental.pallas import tpu as pltpu

F32 = jnp.float32
BF16 = jnp.bfloat16

N_DEV = 8
EPS = 1e-6
CHUNK = 64
BRANCH_W = 512
N_SLICES = 10
H_RET = 4
DK_RET = 128
H_ATT = 8
DH_ATT = 64
N_PREV_CHUNKS = 8
REL_CLIP = 128
N_REL = 2 * REL_CLIP + 1
NEG_INF = -1e30
ROPE_BASE = 10000.0
ATT_QB = 256
ATT_LOOKBACK = N_PREV_CHUNKS * CHUNK
ATT_WIN = ATT_LOOKBACK + ATT_QB
RET_TB = 512
CONV_HALO = 16

ADAM_LR = 0.001
ADAM_B1 = 0.9
ADAM_B2 = 0.999
ADAM_EPS = 1e-08
ADAM_WD = 0.01
ADAM_STEP = 10

VMEM_LIMIT_BYTES = 56 * 1024 * 1024
MESH = pl.DeviceIdType.MESH


def _cparams(*sem):
    return pltpu.CompilerParams(dimension_semantics=sem, vmem_limit_bytes=VMEM_LIMIT_BYTES)


def _dot(a, b):
    return lax.dot_general(a, b, (((1,), (0,)), ((), ())), preferred_element_type=F32)


def _dot_nt(a, b):
    return lax.dot_general(a, b, (((1,), (1,)), ((), ())), preferred_element_type=F32)


def _dot_tn(a, b):
    return lax.dot_general(a, b, (((0,), (0,)), ((), ())), preferred_element_type=F32)


def _bf(v):
    return v.astype(BF16)


def _sigmoid(v):
    return 1.0 / (1.0 + jnp.exp(-v))


def _tile(n, want):
    if n <= want:
        return n
    for t in range(want - want % 128, 0, -128):
        if n % t == 0:
            return t
    t = want
    while n % t:
        t //= 2
    return t


def _matmul(a, b, *, dims, grid, a_spec, b_spec, o_spec, out_shape, acc_shape, name, scale=1.0,
            res=None, res_spec=None):
    nk = grid[3]
    has_res = res is not None

    def body(*refs):
        if has_res:
            a_ref, b_ref, r_ref, o_ref = refs[:4]
        else:
            a_ref, b_ref, o_ref = refs[:3]
            r_ref = None

        def finish(acc):
            if scale != 1.0:
                acc = acc * scale
            if r_ref is not None:
                acc = acc + r_ref[...].astype(F32)
            o_ref[...] = acc.astype(o_ref.dtype)

        part = lax.dot_general(_bf(a_ref[...]), _bf(b_ref[...]), (dims, ((), ())), preferred_element_type=F32)
        if nk == 1:
            finish(part)
        else:
            acc_ref = refs[-1]
            k = pl.program_id(3)

            @pl.when(k == 0)
            def _():
                acc_ref[...] = part

            @pl.when(k > 0)
            def _():
                acc_ref[...] += part

            @pl.when(k == nk - 1)
            def _():
                finish(acc_ref[...])

    in_specs = [a_spec, b_spec] + ([res_spec] if has_res else [])
    args = (a, b) + ((res,) if has_res else ())
    return pl.pallas_call(
        body, name=name, grid=grid, in_specs=in_specs, out_specs=o_spec, out_shape=out_shape,
        scratch_shapes=[] if nk == 1 else [pltpu.VMEM(acc_shape, F32)],
        compiler_params=_cparams("parallel", "parallel", "parallel", "arbitrary"),
    )(*args)


NT = ((1,), (1,))
NN = ((1,), (0,))
TN = ((0,), (0,))


def _mm_rows(a, b, *, dims, out_dtype, name, res=None, scale=1.0, tm=1024, tn=1024, tk=1024):
    m, kdim = a.shape
    n = b.shape[1] if dims == NN else b.shape[0]
    tm, tn, tk = _tile(m, tm), _tile(n, tn), _tile(kdim, tk)
    grid = (1, m // tm, n // tn, kdim // tk)
    a_spec = pl.BlockSpec((tm, tk), lambda s, i, j, k: (i, k))
    if dims == NN:
        b_spec = pl.BlockSpec((tk, tn), lambda s, i, j, k: (k, j))
    else:
        b_spec = pl.BlockSpec((tn, tk), lambda s, i, j, k: (j, k))
    o_spec = pl.BlockSpec((tm, tn), lambda s, i, j, k: (i, j))
    return _matmul(a, b, dims=dims, grid=grid, a_spec=a_spec, b_spec=b_spec, o_spec=o_spec,
                   out_shape=jax.ShapeDtypeStruct((m, n), out_dtype), acc_shape=(tm, tn), name=name,
                   res=res, res_spec=o_spec if res is not None else None, scale=scale)


def _mm_to_slices(a, b_t, *, name, tm=1024):
    m, kdim = a.shape
    n = b_t.shape[0]
    tm = _tile(m, tm)
    w = BRANCH_W
    grid = (1, m // tm, n // w, 1)
    return _matmul(a, b_t, dims=NT, grid=grid,
                   a_spec=pl.BlockSpec((tm, kdim), lambda s, i, j, k: (i, 0)),
                   b_spec=pl.BlockSpec((w, kdim), lambda s, i, j, k: (j, 0)),
                   o_spec=pl.BlockSpec((None, tm, w), lambda s, i, j, k: (j, i, 0)),
                   out_shape=jax.ShapeDtypeStruct((n // w, m, w), BF16), acc_shape=(tm, w), name=name)


def _mm_from_slices(a3, b, row0, res, *, name, tm=1024):
    s_n, m, w = a3.shape
    n = b.shape[1]
    tm = _tile(m, tm)
    off = row0 // w
    grid = (1, m // tm, 1, s_n)
    o_spec = pl.BlockSpec((tm, n), lambda s, i, j, k: (i, 0))
    return _matmul(a3, b, dims=NN, grid=grid,
                   a_spec=pl.BlockSpec((None, tm, w), lambda s, i, j, k: (k, i, 0)),
                   b_spec=pl.BlockSpec((w, n), lambda s, i, j, k: (k + off, 0)),
                   o_spec=o_spec, out_shape=jax.ShapeDtypeStruct((m, n), F32), acc_shape=(tm, n), name=name,
                   res=res, res_spec=o_spec)


def _mm_tn(a, b, *, name, tm=1408, tn=1024, tk=512):
    t, m = a.shape
    n = b.shape[1]
    tm, tn, tk = _tile(m, tm), _tile(n, tn), _tile(t, tk)
    grid = (1, m // tm, n // tn, t // tk)
    return _matmul(a, b, dims=TN, grid=grid,
                   a_spec=pl.BlockSpec((tk, tm), lambda s, i, j, k: (k, i)),
                   b_spec=pl.BlockSpec((tk, tn), lambda s, i, j, k: (k, j)),
                   o_spec=pl.BlockSpec((tm, tn), lambda s, i, j, k: (i, j)),
                   out_shape=jax.ShapeDtypeStruct((m, n), F32), acc_shape=(tm, tn), name=name)


def _mm_tn_slices(a3, b, *, name, tn=1024, tk=512):
    s_n, t, w = a3.shape
    n = b.shape[1]
    tn, tk = _tile(n, tn), _tile(t, tk)
    grid = (1, s_n, n // tn, t // tk)
    return _matmul(a3, b, dims=TN, grid=grid,
                   a_spec=pl.BlockSpec((None, tk, w), lambda s, i, j, k: (i, k, 0)),
                   b_spec=pl.BlockSpec((tk, tn), lambda s, i, j, k: (k, j)),
                   o_spec=pl.BlockSpec((w, tn), lambda s, i, j, k: (i, j)),
                   out_shape=jax.ShapeDtypeStruct((s_n * w, n), F32), acc_shape=(w, tn), name=name)


def _mm_tn_batch(a, b3, *, name, a_batched, tm=1024, tn=1024, tk=512):
    s_n, t, n = b3.shape
    m = a.shape[-1]
    tm, tn, tk = _tile(m, tm), _tile(n, tn), _tile(t, tk)
    grid = (s_n, m // tm, n // tn, t // tk)
    if a_batched:
        a_spec = pl.BlockSpec((None, tk, tm), lambda s, i, j, k: (s, k, i))
    else:
        a_spec = pl.BlockSpec((tk, tm), lambda s, i, j, k: (k, i))
    return _matmul(a, b3, dims=TN, grid=grid, a_spec=a_spec,
                   b_spec=pl.BlockSpec((None, tk, tn), lambda s, i, j, k: (s, k, j)),
                   o_spec=pl.BlockSpec((None, tm, tn), lambda s, i, j, k: (s, i, j)),
                   out_shape=jax.ShapeDtypeStruct((s_n, m, n), F32), acc_shape=(tm, tn), name=name)


def _norm_parts(xf):
    r = lax.rsqrt(jnp.mean(xf * xf, axis=-1, keepdims=True) + EPS)
    return xf * r, r


def _norm_bwd(dh, xhat, r, w):
    dxhat = dh * w
    dx = r * (dxhat - xhat * jnp.mean(dxhat * xhat, axis=-1, keepdims=True))
    return dx, jnp.sum(dh * xhat, axis=0, keepdims=True)


def _rmsnorm_fwd(x, w, *, name, tm=1024):
    t, d = x.shape
    tm = _tile(t, tm)

    def body(x_ref, w_ref, h_ref):
        xhat, _ = _norm_parts(x_ref[...])
        h_ref[...] = _bf(xhat * w_ref[...])

    return pl.pallas_call(
        body, name=name, grid=(t // tm,),
        in_specs=[pl.BlockSpec((tm, d), lambda i: (i, 0)), pl.BlockSpec((1, d), lambda i: (0, 0))],
        out_specs=pl.BlockSpec((tm, d), lambda i: (i, 0)),
        out_shape=jax.ShapeDtypeStruct((t, d), BF16), compiler_params=_cparams("parallel"),
    )(x, w)


def _rmsnorm_bwd(dh, x, w, dres, *, name, tm=1024):
    t, d = x.shape
    tm = _tile(t, tm)

    def body(dh_ref, x_ref, w_ref, dres_ref, dx_ref, dw_ref):
        xhat, r = _norm_parts(x_ref[...])
        dx, dw = _norm_bwd(dh_ref[...], xhat, r, w_ref[...])
        dx_ref[...] = dres_ref[...] + dx

        @pl.when(pl.program_id(0) == 0)
        def _():
            dw_ref[...] = dw

        @pl.when(pl.program_id(0) > 0)
        def _():
            dw_ref[...] += dw

    row = pl.BlockSpec((tm, d), lambda i: (i, 0))
    vec = pl.BlockSpec((1, d), lambda i: (0, 0))
    return pl.pallas_call(
        body, name=name, grid=(t // tm,), in_specs=[row, row, vec, row], out_specs=[row, vec],
        out_shape=[jax.ShapeDtypeStruct((t, d), F32), jax.ShapeDtypeStruct((1, d), F32)],
        compiler_params=_cparams("arbitrary"),
    )(dh, x, w, dres)


def _loss_fwd_bwd(x, w, target, *, name, tm=1024):
    t, d = x.shape
    tm = _tile(t, tm)

    def body(x_ref, w_ref, t_ref, dx_ref, dw_ref, loss_ref):
        xhat, r = _norm_parts(x_ref[...])
        wv = w_ref[...]
        err = xhat * wv - t_ref[...]
        dx, dw = _norm_bwd(err * (1.0 / d), xhat, r, wv)
        dx_ref[...] = dx
        part = jnp.full((1, 128), 0.5 / d, F32) * jnp.sum(err * err)

        @pl.when(pl.program_id(0) == 0)
        def _():
            dw_ref[...] = dw
            loss_ref[...] = part

        @pl.when(pl.program_id(0) > 0)
        def _():
            dw_ref[...] += dw
            loss_ref[...] += part

    row = pl.BlockSpec((tm, d), lambda i: (i, 0))
    vec = pl.BlockSpec((1, d), lambda i: (0, 0))
    return pl.pallas_call(
        body, name=name, grid=(t // tm,), in_specs=[row, vec, row],
        out_specs=[row, vec, pl.BlockSpec((1, 128), lambda i: (0, 0))],
        out_shape=[jax.ShapeDtypeStruct((t, d), F32), jax.ShapeDtypeStruct((1, d), F32),
                   jax.ShapeDtypeStruct((1, 128), F32)],
        compiler_params=_cparams("arbitrary"),
    )(x, w, target)


def _ffn_tiles(t, f):
    tf = f
    for cand in (1408, 1024, 512, 256, 128):
        if f % cand == 0:
            tf = cand
            break
    return _tile(t, 512), tf


def _ffn_fwd(x, nw, wg_t, wu_t, wd, *, name):
    t, d = x.shape
    f = wd.shape[0]
    tm, tf = _ffn_tiles(t, f)
    nf = f // tf

    def body(x_ref, nw_ref, wg_ref, wu_ref, wd_ref, xo_ref, h_ref, g_ref, u_ref, a_ref, hs_ref, acc_ref):
        j = pl.program_id(1)

        @pl.when(j == 0)
        def _():
            xhat, _ = _norm_parts(x_ref[...])
            hb = _bf(xhat * nw_ref[...])
            hs_ref[...] = hb
            h_ref[...] = hb

        hb = hs_ref[...]
        g = _dot_nt(hb, wg_ref[...])
        u = _dot_nt(hb, wu_ref[...])
        a = _bf(g * _sigmoid(g) * u)
        g_ref[...] = _bf(g)
        u_ref[...] = _bf(u)
        a_ref[...] = a
        part = _dot(a, wd_ref[...])

        @pl.when(j == 0)
        def _():
            acc_ref[...] = part

        @pl.when(j > 0)
        def _():
            acc_ref[...] += part

        @pl.when(j == nf - 1)
        def _():
            xo_ref[...] = x_ref[...] + 0.5 * acc_ref[...]

    row = pl.BlockSpec((tm, d), lambda i, j: (i, 0))
    wspec = pl.BlockSpec((tf, d), lambda i, j: (j, 0))
    hid = pl.BlockSpec((tm, tf), lambda i, j: (i, j))
    return pl.pallas_call(
        body, name=name, grid=(t // tm, nf),
        in_specs=[row, pl.BlockSpec((1, d), lambda i, j: (0, 0)), wspec, wspec, wspec],
        out_specs=[row, row, hid, hid, hid],
        out_shape=[jax.ShapeDtypeStruct((t, d), F32), jax.ShapeDtypeStruct((t, d), BF16)]
        + [jax.ShapeDtypeStruct((t, f), BF16)] * 3,
        scratch_shapes=[pltpu.VMEM((tm, d), BF16), pltpu.VMEM((tm, d), F32)],
        compiler_params=_cparams("parallel", "arbitrary"),
    )(x, nw, wg_t, wu_t, wd)


def _ffn_bwd(dxo, x, nw, g, u, wg_t, wu_t, wd, *, name):
    t, d = x.shape
    f = wd.shape[0]
    tm, tf = _ffn_tiles(t, f)
    nf = f // tf

    def body(dxo_ref, x_ref, nw_ref, g_ref, u_ref, wg_ref, wu_ref, wd_ref,
             dx_ref, dy_ref, dg_ref, du_ref, dnw_ref, dys_ref, acc_ref):
        i, j = pl.program_id(0), pl.program_id(1)

        @pl.when(j == 0)
        def _():
            dyb = _bf(0.5 * dxo_ref[...])
            dys_ref[...] = dyb
            dy_ref[...] = dyb

        da = _dot_nt(dys_ref[...], wd_ref[...])
        gv = g_ref[...].astype(F32)
        uv = u_ref[...].astype(F32)
        s = _sigmoid(gv)
        dg = _bf(da * uv * (s * (1.0 + gv * (1.0 - s))))
        du = _bf(da * (gv * s))
        dg_ref[...] = dg
        du_ref[...] = du
        part = _dot(dg, wg_ref[...]) + _dot(du, wu_ref[...])

        @pl.when(j == 0)
        def _():
            acc_ref[...] = part

        @pl.when(j > 0)
        def _():
            acc_ref[...] += part

        @pl.when(j == nf - 1)
        def _():
            xhat, r = _norm_parts(x_ref[...])
            dx, dw = _norm_bwd(acc_ref[...], xhat, r, nw_ref[...])
            dx_ref[...] = dxo_ref[...] + dx

            @pl.when(i == 0)
            def _():
                dnw_ref[...] = dw

            @pl.when(i > 0)
            def _():
                dnw_ref[...] += dw

    row = pl.BlockSpec((tm, d), lambda i, j: (i, 0))
    vec = pl.BlockSpec((1, d), lambda i, j: (0, 0))
    wspec = pl.BlockSpec((tf, d), lambda i, j: (j, 0))
    hid = pl.BlockSpec((tm, tf), lambda i, j: (i, j))
    return pl.pallas_call(
        body, name=name, grid=(t // tm, nf),
        in_specs=[row, row, vec, hid, hid, wspec, wspec, wspec],
        out_specs=[row, row, hid, hid, vec],
        out_shape=[jax.ShapeDtypeStruct((t, d), F32), jax.ShapeDtypeStruct((t, d), BF16),
                   jax.ShapeDtypeStruct((t, f), BF16), jax.ShapeDtypeStruct((t, f), BF16),
                   jax.ShapeDtypeStruct((1, d), F32)],
        scratch_shapes=[pltpu.VMEM((tm, d), BF16), pltpu.VMEM((tm, d), F32)],
        compiler_params=_cparams("arbitrary", "arbitrary"),
    )(dxo, x, nw, g, u, wg_t, wu_t, wd)


def _shift_down(prev, cur, n):
    ext = jnp.concatenate([prev, cur], axis=0)
    return pltpu.roll(ext, n, axis=0)[prev.shape[0]:]


def _shift_up(cur, nxt, n):
    ext = jnp.concatenate([cur, nxt], axis=0)
    return pltpu.roll(ext, ext.shape[0] - n, axis=0)[:cur.shape[0]]


def _conv_specs(t, tb):
    hb = tb // CONV_HALO
    last = t // CONV_HALO - 1

    def tile(s):
        return pl.BlockSpec((None, tb, 128), lambda c, i: (s, i, c))

    def prev(s):
        return pl.BlockSpec((None, CONV_HALO, 128), lambda c, i: (s, jnp.maximum(i * hb - 1, 0), c))

    def nxt(s):
        return pl.BlockSpec((None, CONV_HALO, 128), lambda c, i: (s, jnp.minimum((i + 1) * hb, last), c))

    return tile, prev, nxt


def _conv_fwd(cols3, conv_w, *, name, tb=1024):
    _, t, bw = cols3.shape
    tb = _tile(t, tb)
    tile, prev, _ = _conv_specs(t, tb)

    def body(u_ref, b_ref, c_ref, up_ref, cp_ref, w_ref, y_ref):
        first = pl.program_id(1) == 0
        z = c_ref[...].astype(F32) * u_ref[...].astype(F32)
        zp = jnp.where(first, 0.0, cp_ref[...].astype(F32) * up_ref[...].astype(F32))
        conv = w_ref[0:1, :] * _shift_down(zp, z, 2) + w_ref[1:2, :] * _shift_down(zp, z, 1) + w_ref[2:3, :] * z
        y_ref[...] = _bf(b_ref[...].astype(F32) * conv)

    return pl.pallas_call(
        body, name=name, grid=(bw // 128, t // tb),
        in_specs=[tile(0), tile(1), tile(2), prev(0), prev(2), pl.BlockSpec((3, 128), lambda c, i: (0, c))],
        out_specs=pl.BlockSpec((tb, 128), lambda c, i: (i, c)),
        out_shape=jax.ShapeDtypeStruct((t, bw), BF16), compiler_params=_cparams("parallel", "parallel"),
    )(cols3, cols3, cols3, cols3, cols3, conv_w)


def _conv_bwd(cols3, conv_w, dy, *, name, tb=1024):
    _, t, bw = cols3.shape
    tb = _tile(t, tb)
    nt = t // tb
    tile, prev, nxt = _conv_specs(t, tb)
    hb = tb // CONV_HALO
    last = t // CONV_HALO - 1

    def body(u_ref, b_ref, c_ref, up_ref, cp_ref, bn_ref, dy_ref, dyn_ref, w_ref, d3_ref, dw_ref):
        i = pl.program_id(1)
        uv, bv, cv = u_ref[...].astype(F32), b_ref[...].astype(F32), c_ref[...].astype(F32)
        dyv = dy_ref[...].astype(F32)
        z = cv * uv
        zp = jnp.where(i == 0, 0.0, cp_ref[...].astype(F32) * up_ref[...].astype(F32))
        z1, z2 = _shift_down(zp, z, 1), _shift_down(zp, z, 2)
        w0, w1, w2 = w_ref[0:1, :], w_ref[1:2, :], w_ref[2:3, :]
        conv = w0 * z2 + w1 * z1 + w2 * z
        dconv = dyv * bv
        dconv_n = jnp.where(i == nt - 1, 0.0, dyn_ref[...].astype(F32) * bn_ref[...].astype(F32))
        dz = w2 * dconv + w1 * _shift_up(dconv, dconv_n, 1) + w0 * _shift_up(dconv, dconv_n, 2)
        d3_ref[0] = _bf(dz * cv)
        d3_ref[1] = _bf(dyv * conv)
        d3_ref[2] = _bf(dz * uv)
        dws = [jnp.sum(dconv * zz, axis=0, keepdims=True) for zz in (z2, z1, z)]

        @pl.when(i == 0)
        def _():
            for j in range(3):
                dw_ref[j:j + 1, :] = dws[j]

        @pl.when(i > 0)
        def _():
            for j in range(3):
                dw_ref[j:j + 1, :] += dws[j]

    dy_tile = pl.BlockSpec((tb, 128), lambda c, i: (i, c))
    dy_next = pl.BlockSpec((CONV_HALO, 128), lambda c, i: (jnp.minimum((i + 1) * hb, last), c))
    wspec = pl.BlockSpec((3, 128), lambda c, i: (0, c))
    return pl.pallas_call(
        body, name=name, grid=(bw // 128, nt),
        in_specs=[tile(0), tile(1), tile(2), prev(0), prev(2), nxt(1), dy_tile, dy_next, wspec],
        out_specs=[pl.BlockSpec((3, tb, 128), lambda c, i: (0, i, c)), wspec],
        out_shape=[jax.ShapeDtypeStruct((3, t, bw), BF16), jax.ShapeDtypeStruct((3, bw), F32)],
        compiler_params=_cparams("parallel", "arbitrary"),
    )(cols3, cols3, cols3, cols3, cols3, cols3, dy, dy, conv_w)


def _ret_consts():
    log_gamma = jnp.log1p(-jnp.exp2(-5.0 - jnp.arange(H_RET, dtype=F32)))
    pos = jnp.arange(CHUNK, dtype=F32)
    d_intra = jnp.exp(log_gamma[:, None, None] * jnp.abs(pos[:, None] - pos[None, :]))
    q_decay = jnp.exp(log_gamma[:, None] * (pos + 1.0))
    k_decay = jnp.exp(log_gamma[:, None] * (CHUNK - 1.0 - pos))
    chunk_decay = jnp.exp(log_gamma * CHUNK)
    wide = (H_RET, CHUNK, DK_RET)
    return (d_intra, jnp.broadcast_to(q_decay[:, :, None], wide), jnp.broadcast_to(k_decay[:, :, None], wide),
            jnp.broadcast_to(chunk_decay[:, None, None], (H_RET, 1, DK_RET)))


def _rope_tables(t):
    inv_freq = ROPE_BASE ** (-jnp.linspace(0.0, 1.0, DK_RET // 2, dtype=F32))
    ang = jnp.arange(t, dtype=F32)[:, None] * inv_freq[None, :]
    cos, sin = jnp.cos(ang), jnp.sin(ang)
    return jnp.concatenate([cos, cos], axis=1), jnp.concatenate([-sin, sin], axis=1)


def _rope(v, cc, ss):
    return v * cc + pltpu.roll(v, DK_RET // 2, axis=1) * ss


def _ret_in_specs(tb, blk):
    def col(s):
        return pl.BlockSpec((None, tb, DK_RET), lambda h, i: (s, blk(i), h))

    tab = pl.BlockSpec((tb, DK_RET), lambda h, i: (blk(i), 0))
    return ([col(3), col(4), col(5), col(6), tab, tab,
             pl.BlockSpec((None, CHUNK, CHUNK), lambda h, i: (h, 0, 0)),
             pl.BlockSpec((None, CHUNK, DK_RET), lambda h, i: (h, 0, 0)),
             pl.BlockSpec((None, CHUNK, DK_RET), lambda h, i: (h, 0, 0)),
             pl.BlockSpec((None, 1, DK_RET), lambda h, i: (h, 0, 0))])


def _ret_fwd(cols3, tables, consts, *, name):
    _, t, bw = cols3.shape
    tb = _tile(t, RET_TB)
    ncb = tb // CHUNK
    scale = DK_RET ** -0.5

    def body(q_ref, k_ref, v_ref, g_ref, cc_ref, ss_ref, di_ref, qd_ref, kd_ref, cd_ref, y_ref, st_ref, state):
        @pl.when(pl.program_id(1) == 0)
        def _():
            state[...] = jnp.zeros_like(state)

        di, qd, kd, cd = di_ref[...], qd_ref[...], kd_ref[...], cd_ref[...]
        for c in range(ncb):
            rows = pl.ds(c * CHUNK, CHUNK)
            cc, ss = cc_ref[rows, :], ss_ref[rows, :]
            qs = _rope(q_ref[rows, :].astype(F32), cc, ss) * scale
            kr = _rope(k_ref[rows, :].astype(F32), cc, ss)
            vb = v_ref[rows, :]
            gv = g_ref[rows, :].astype(F32)
            s_in = state[...]
            st_ref[c] = s_in
            inner = _dot_nt(_bf(qs), _bf(kr)) * di
            o = _dot(_bf(inner), vb) + _dot(_bf(qs * qd), _bf(s_in))
            state[...] = s_in * cd + _dot_tn(_bf(kr * kd), vb)
            on = o * lax.rsqrt(jnp.mean(o * o, axis=-1, keepdims=True) + EPS)
            y_ref[rows, :] = _bf(gv * _sigmoid(gv) * on)

    return pl.pallas_call(
        body, name=name, grid=(H_RET, t // tb),
        in_specs=_ret_in_specs(tb, lambda i: i),
        out_specs=[pl.BlockSpec((tb, DK_RET), lambda h, i: (i, h)),
                   pl.BlockSpec((None, ncb, DK_RET, DK_RET), lambda h, i: (h, i, 0, 0))],
        out_shape=[jax.ShapeDtypeStruct((t, bw), BF16),
                   jax.ShapeDtypeStruct((H_RET, t // CHUNK, DK_RET, DK_RET), F32)],
        scratch_shapes=[pltpu.VMEM((DK_RET, DK_RET), F32)],
        compiler_params=_cparams("parallel", "arbitrary"),
    )(cols3, cols3, cols3, cols3, tables[0], tables[1], *consts)


def _ret_bwd(cols3, tables, consts, states, dy, *, name):
    _, t, bw = cols3.shape
    tb = _tile(t, RET_TB)
    ncb = tb // CHUNK
    nb = t // tb
    scale = DK_RET ** -0.5

    def body(q_ref, k_ref, v_ref, g_ref, cc_ref, ss_ref, di_ref, qd_ref, kd_ref, cd_ref, st_ref, dy_ref,
             d4_ref, dstate):
        @pl.when(pl.program_id(1) == 0)
        def _():
            dstate[...] = jnp.zeros_like(dstate)

        di, qd, kd, cd = di_ref[...], qd_ref[...], kd_ref[...], cd_ref[...]
        for c in reversed(range(ncb)):
            rows = pl.ds(c * CHUNK, CHUNK)
            cc, ss = cc_ref[rows, :], ss_ref[rows, :]
            qs = _rope(q_ref[rows, :].astype(F32), cc, ss) * scale
            kr = _rope(k_ref[rows, :].astype(F32), cc, ss)
            vb = v_ref[rows, :]
            gv = g_ref[rows, :].astype(F32)
            dyv = dy_ref[rows, :].astype(F32)
            s_in = _bf(st_ref[c])
            qsb, krb = _bf(qs), _bf(kr)
            qdb, kdb = _bf(qs * qd), _bf(kr * kd)
            inner = _dot_nt(qsb, krb) * di
            innerb = _bf(inner)
            o = _dot(innerb, vb) + _dot(qdb, s_in)
            r = lax.rsqrt(jnp.mean(o * o, axis=-1, keepdims=True) + EPS)
            on = o * r
            sg = _sigmoid(gv)
            d4_ref[3, rows, :] = _bf(dyv * on * (sg * (1.0 + gv * (1.0 - sg))))
            don = dyv * (gv * sg)
            dob = _bf(r * (don - on * jnp.mean(don * on, axis=-1, keepdims=True)))
            dst = dstate[...]
            dstb = _bf(dst)
            dinner = _bf(_dot_nt(dob, vb) * di)
            dv = _dot_tn(innerb, dob) + _dot(kdb, dstb)
            dqs = _dot(dinner, krb) + _dot_nt(dob, s_in) * qd
            dkr = _dot_tn(dinner, qsb) + _dot_nt(vb, dstb) * kd
            dstate[...] = dst * cd + _dot_tn(qdb, dob)
            d4_ref[0, rows, :] = _bf(_rope_bwd(dqs * scale, cc, ss))
            d4_ref[1, rows, :] = _bf(_rope_bwd(dkr, cc, ss))
            d4_ref[2, rows, :] = _bf(dv)

    rev = lambda i: nb - 1 - i
    return pl.pallas_call(
        body, name=name, grid=(H_RET, nb),
        in_specs=_ret_in_specs(tb, rev)
        + [pl.BlockSpec((None, ncb, DK_RET, DK_RET), lambda h, i: (h, rev(i), 0, 0)),
           pl.BlockSpec((tb, DK_RET), lambda h, i: (rev(i), h))],
        out_specs=pl.BlockSpec((4, tb, DK_RET), lambda h, i: (0, rev(i), h)),
        out_shape=jax.ShapeDtypeStruct((4, t, bw), BF16),
        scratch_shapes=[pltpu.VMEM((DK_RET, DK_RET), F32)],
        compiler_params=_cparams("parallel", "arbitrary"),
    )(cols3, cols3, cols3, cols3, tables[0], tables[1], *consts, states, dy)


def _rope_bwd(dv, cc, ss):
    return dv * cc + pltpu.roll(dv * ss, DK_RET // 2, axis=1)


def _att_bias_index():
    n = jnp.arange(ATT_QB)[:, None]
    m = jnp.arange(ATT_WIN)[None, :]
    off = jnp.arange(3)[:, None, None] * ATT_QB
    return jnp.clip(off + n - m, -REL_CLIP, REL_CLIP) + REL_CLIP


def _att_window(i):
    return pl.multiple_of(jnp.maximum(i - ATT_LOOKBACK // ATT_QB, 0) * ATT_QB, ATT_QB)


def _att_mask(i, ws):
    qchunk = (i * ATT_QB + lax.broadcasted_iota(jnp.int32, (ATT_QB, ATT_WIN), 0)) // CHUNK
    kchunk = (ws + lax.broadcasted_iota(jnp.int32, (ATT_QB, ATT_WIN), 1)) // CHUNK
    return (kchunk <= qchunk) & (kchunk >= qchunk - N_PREV_CHUNKS)


def _att_fwd(cols3, bias3, *, name):
    _, t, bw = cols3.shape
    assert t % ATT_QB == 0 and t >= ATT_WIN
    scale = DH_ATT ** -0.5
    nvar = ATT_LOOKBACK // ATT_QB

    def body(q_ref, k_ref, v_ref, b_ref, y_ref, lse_ref):
        i = pl.program_id(1)
        ws = _att_window(i)
        valid = _att_mask(i, ws)
        q = q_ref[...].astype(F32)
        kw = k_ref[pl.ds(ws, ATT_WIN), :]
        vw = v_ref[pl.ds(ws, ATT_WIN), :]
        lane_head = lax.broadcasted_iota(jnp.int32, (ATT_QB, 128), 1) // DH_ATT
        out = jnp.zeros((ATT_QB, 128), F32)
        lse = jnp.zeros((ATT_QB, 128), F32)
        for hh in range(2):
            mine = lane_head == hh
            s = _dot_nt(_bf(jnp.where(mine, q, 0.0)), kw) * scale + b_ref[hh]
            s = jnp.where(valid, s, NEG_INF)
            mx = jnp.max(s, axis=-1, keepdims=True)
            p = jnp.exp(s - mx)
            l = jnp.sum(p, axis=-1, keepdims=True)
            out = jnp.where(mine, _dot(_bf(p), vw) / l, out)
            lse = jnp.where(mine, mx + jnp.log(l), lse)
        y_ref[...] = _bf(out)
        lse_ref[...] = lse

    kv = lambda s: pl.BlockSpec((None, t, 128), lambda hp, i: (s, 0, hp))
    return pl.pallas_call(
        body, name=name, grid=(H_ATT // 2, t // ATT_QB),
        in_specs=[pl.BlockSpec((None, ATT_QB, 128), lambda hp, i: (7, i, hp)), kv(8), kv(9),
                  pl.BlockSpec((None, 2, ATT_QB, ATT_WIN), lambda hp, i: (jnp.minimum(i, nvar), hp, 0, 0))],
        out_specs=[pl.BlockSpec((ATT_QB, 128), lambda hp, i: (i, hp)),
                   pl.BlockSpec((None, ATT_QB, 128), lambda hp, i: (hp, i, 0))],
        out_shape=[jax.ShapeDtypeStruct((t, bw), BF16), jax.ShapeDtypeStruct((H_ATT // 2, t, 128), F32)],
        compiler_params=_cparams("parallel", "arbitrary"),
    )(cols3, cols3, cols3, bias3)


def _att_bwd(cols3, bias3, y, lse, dy, *, name):
    _, t, bw = cols3.shape
    nq = t // ATT_QB
    scale = DH_ATT ** -0.5
    nvar = ATT_LOOKBACK // ATT_QB

    def body(q_ref, k_ref, v_ref, b_ref, y_ref, lse_ref, dy_ref, d3_ref, db_ref, dk_acc, dv_acc):
        i = pl.program_id(1)

        @pl.when(i == 0)
        def _():
            dk_acc[...] = jnp.zeros_like(dk_acc)
            dv_acc[...] = jnp.zeros_like(dv_acc)

        ws = _att_window(i)
        valid = _att_mask(i, ws)
        q = q_ref[...].astype(F32)
        kw = k_ref[pl.ds(ws, ATT_WIN), :]
        vw = v_ref[pl.ds(ws, ATT_WIN), :]
        do = dy_ref[...].astype(F32)
        dof = do * y_ref[...].astype(F32)
        lsev = lse_ref[...]
        lane_head = lax.broadcasted_iota(jnp.int32, (ATT_QB, 128), 1) // DH_ATT
        dq = jnp.zeros((ATT_QB, 128), F32)
        dk = jnp.zeros((ATT_WIN, 128), F32)
        dv = jnp.zeros((ATT_WIN, 128), F32)
        first = i <= nvar
        for hh in range(2):
            mine = lane_head == hh
            qh = _bf(jnp.where(mine, q, 0.0))
            doh = _bf(jnp.where(mine, do, 0.0))
            s = _dot_nt(qh, kw) * scale + b_ref[hh]
            lse_h = jnp.max(jnp.where(mine, lsev, NEG_INF), axis=-1, keepdims=True)
            p = jnp.where(valid, jnp.exp(s - lse_h), 0.0)
            delta = jnp.sum(jnp.where(mine, dof, 0.0), axis=-1, keepdims=True)
            ds = p * (_dot_nt(doh, vw) - delta)

            @pl.when(first)
            def _():
                db_ref[hh] = ds

            @pl.when(jnp.logical_not(first))
            def _():
                db_ref[hh] += ds

            dsb = _bf(ds * scale)
            dq = jnp.where(mine, _dot(dsb, kw), dq)
            dk = dk + _dot_tn(dsb, qh)
            dv = dv + _dot_tn(_bf(p), doh)
        d3_ref[0, pl.ds(pl.multiple_of(i * ATT_QB, ATT_QB), ATT_QB), :] = _bf(dq)
        dk_acc[pl.ds(ws, ATT_WIN), :] += dk
        dv_acc[pl.ds(ws, ATT_WIN), :] += dv

        @pl.when(i == nq - 1)
        def _():
            d3_ref[1] = _bf(dk_acc[...])
            d3_ref[2] = _bf(dv_acc[...])

    kv = lambda s: pl.BlockSpec((None, t, 128), lambda hp, i: (s, 0, hp))
    qrow = pl.BlockSpec((ATT_QB, 128), lambda hp, i: (i, hp))
    btile = pl.BlockSpec((None, 2, ATT_QB, ATT_WIN), lambda hp, i: (jnp.minimum(i, nvar), hp, 0, 0))
    return pl.pallas_call(
        body, name=name, grid=(H_ATT // 2, nq),
        in_specs=[pl.BlockSpec((None, ATT_QB, 128), lambda hp, i: (7, i, hp)), kv(8), kv(9), btile, qrow,
                  pl.BlockSpec((None, ATT_QB, 128), lambda hp, i: (hp, i, 0)), qrow],
        out_specs=[pl.BlockSpec((3, t, 128), lambda hp, i: (0, 0, hp)), btile],
        out_shape=[jax.ShapeDtypeStruct((3, t, bw), BF16),
                   jax.ShapeDtypeStruct((nvar + 1, H_ATT, ATT_QB, ATT_WIN), F32)],
        scratch_shapes=[pltpu.VMEM((t, 128), F32), pltpu.VMEM((t, 128), F32)],
        compiler_params=_cparams("parallel", "arbitrary"),
    )(cols3, cols3, cols3, bias3, y, lse, dy)


def _rel_bias_grad(dbias3, idx3):
    nv, nh, qb, win = dbias3.shape
    width = win + qb
    flipped = jnp.pad(dbias3[..., ::-1], ((0, 0), (0, 0), (0, 0), (0, qb + 1)))
    skew = flipped.reshape(nv, nh, qb * (width + 1))[..., :qb * width].reshape(nv, nh, qb, width)
    diag = jnp.sum(skew, axis=2)
    dist = jnp.arange(nv)[:, None] * qb + jnp.arange(width)[None, :] - (win - 1)
    col = jnp.clip(dist, -REL_CLIP, REL_CLIP) + REL_CLIP
    onehot = (col[:, :, None] == jnp.arange(N_REL)[None, None, :]).astype(F32)
    return jnp.einsum('vhj,vjr->hr', diag, onehot, precision=lax.Precision.HIGHEST)


def _merge_fwd(h, ys, wmg, wb_t, *, name, tm=512):
    t, d = h.shape
    bw = ys[0].shape[1]
    tm = _tile(t, tm)

    def body(h_ref, y0_ref, y1_ref, y2_ref, wg_ref, wb_ref, m_ref, s_ref, p_ref):
        hv = h_ref[...]
        total = jnp.zeros((tm, d), F32)
        for b, y_ref in enumerate((y0_ref, y1_ref, y2_ref)):
            s = _sigmoid(_dot(hv, wg_ref[b]))
            p = _dot_nt(y_ref[...], wb_ref[b])
            s_ref[b] = _bf(s)
            p_ref[b] = _bf(p)
            total = total + s * p
        m_ref[...] = _bf(total)

    row = pl.BlockSpec((tm, d), lambda i: (i, 0))
    yrow = pl.BlockSpec((tm, bw), lambda i: (i, 0))
    three = pl.BlockSpec((3, tm, d), lambda i: (0, i, 0))
    return pl.pallas_call(
        body, name=name, grid=(t // tm,),
        in_specs=[row, yrow, yrow, yrow, pl.BlockSpec((3, d, d), lambda i: (0, 0, 0)),
                  pl.BlockSpec((3, d, bw), lambda i: (0, 0, 0))],
        out_specs=[row, three, three],
        out_shape=[jax.ShapeDtypeStruct((t, d), BF16), jax.ShapeDtypeStruct((3, t, d), BF16),
                   jax.ShapeDtypeStruct((3, t, d), BF16)],
        compiler_params=_cparams("parallel"),
    )(h, *ys, wmg, wb_t)


def _merge_bwd(dm, s3, p3, wmg, wb_t, *, name, tm=512):
    _, t, d = s3.shape
    bw = wb_t.shape[2]
    tm = _tile(t, tm)

    def body(dm_ref, s_ref, p_ref, wg_ref, wb_ref, dgp_ref, dp_ref, dy_ref, dh_ref):
        dmv = dm_ref[...].astype(F32)
        dh = jnp.zeros((tm, d), F32)
        for b in range(3):
            s = s_ref[b].astype(F32)
            dgp = _bf(dmv * p_ref[b].astype(F32) * s * (1.0 - s))
            dp = _bf(dmv * s)
            dgp_ref[b] = dgp
            dp_ref[b] = dp
            dy_ref[b] = _bf(_dot(dp, wb_ref[b]))
            dh = dh + _dot_nt(dgp, wg_ref[b])
        dh_ref[...] = dh

    row = pl.BlockSpec((tm, d), lambda i: (i, 0))
    three = pl.BlockSpec((3, tm, d), lambda i: (0, i, 0))
    return pl.pallas_call(
        body, name=name, grid=(t // tm,),
        in_specs=[row, three, three, pl.BlockSpec((3, d, d), lambda i: (0, 0, 0)),
                  pl.BlockSpec((3, d, bw), lambda i: (0, 0, 0))],
        out_specs=[three, three, pl.BlockSpec((3, tm, bw), lambda i: (0, i, 0)), row],
        out_shape=[jax.ShapeDtypeStruct((3, t, d), BF16), jax.ShapeDtypeStruct((3, t, d), BF16),
                   jax.ShapeDtypeStruct((3, t, bw), BF16), jax.ShapeDtypeStruct((t, d), F32)],
        compiler_params=_cparams("parallel"),
    )(dm, s3, p3, wmg, wb_t)


def _layer_fwd(x, p, aux, tag):
    x1, h1, g1, u1, a1 = _ffn_fwd(x, p["n1"], p["wg1"], p["wu1"], p["wd1"], name=f"ffn1_fwd_{tag}")
    h2 = _rmsnorm_fwd(x1, p["nmix"], name=f"mixnorm_fwd_{tag}")
    cols3 = _mm_to_slices(h2, p["win"], name=f"inproj_fwd_{tag}")
    bias3 = p["rel_bias"][:, aux["idx3"]].transpose(1, 0, 2, 3)
    y_conv = _conv_fwd(cols3, p["conv_w"], name=f"conv_fwd_{tag}")
    y_ret, states = _ret_fwd(cols3, aux["rope"], aux["ret"], name=f"ret_fwd_{tag}")
    y_att, lse = _att_fwd(cols3, bias3, name=f"att_fwd_{tag}")
    merged, s3, p3 = _merge_fwd(h2, (y_conv, y_ret, y_att), p["wmg"], p["wb"], name=f"merge_fwd_{tag}")
    x2 = _mm_rows(merged, p["wout"], dims=NN, out_dtype=F32, res=x1, name=f"outproj_fwd_{tag}")
    x3, h3, g2, u2, a2 = _ffn_fwd(x2, p["n2"], p["wg2"], p["wu2"], p["wd2"], name=f"ffn2_fwd_{tag}")
    saved = dict(x0=x, h1=h1, g1=g1, u1=u1, a1=a1, x1=x1, h2=h2, cols3=cols3, bias3=bias3, y_conv=y_conv,
                 y_ret=y_ret, states=states, y_att=y_att, lse=lse, merged=merged, s3=s3, p3=p3, x2=x2, h3=h3,
                 g2=g2, u2=u2, a2=a2)
    return x3, saved


def _ffn_grads(dxo, x, h, g, u, a, nw, wg, wu, wd, tag):
    dx, dyb, dg, du, dnw = _ffn_bwd(dxo, x, nw, g, u, wg, wu, wd, name=f"{tag}_bwd")
    return dx, dnw, _mm_tn(dg, h, name=f"{tag}_dwg"), _mm_tn(du, h, name=f"{tag}_dwu"), _mm_tn(a, dyb, name=f"{tag}_dwd")


def _layer_bwd(dx3, p, s, aux, tag):
    grads = {}
    dx2, grads["n2"], grads["wg2"], grads["wu2"], grads["wd2"] = _ffn_grads(
        dx3, s["x2"], s["h3"], s["g2"], s["u2"], s["a2"], p["n2"], p["wg2"], p["wu2"], p["wd2"], f"ffn2_{tag}")

    dm = _mm_rows(dx2, p["wout"], dims=NT, out_dtype=BF16, name=f"outproj_dm_{tag}")
    grads["wout"] = _mm_tn(s["merged"], dx2, name=f"outproj_dw_{tag}")
    dgp3, dp3, dy3, dh2 = _merge_bwd(dm, s["s3"], s["p3"], p["wmg"], p["wb"], name=f"merge_bwd_{tag}")
    grads["wmg"] = _mm_tn_batch(s["h2"], dgp3, a_batched=False, name=f"merge_dwg_{tag}")
    y3 = jnp.stack([s["y_conv"], s["y_ret"], s["y_att"]])
    grads["wb"] = _mm_tn_batch(dp3, y3, a_batched=True, name=f"merge_dwb_{tag}")

    dconv3, grads["conv_w"] = _conv_bwd(s["cols3"], p["conv_w"], dy3[0], name=f"conv_bwd_{tag}")
    dret4 = _ret_bwd(s["cols3"], aux["rope"], aux["ret"], s["states"], dy3[1], name=f"ret_bwd_{tag}")
    datt3, dbias3 = _att_bwd(s["cols3"], s["bias3"], s["y_att"], s["lse"], dy3[2], name=f"att_bwd_{tag}")
    grads["rel_bias"] = _rel_bias_grad(dbias3, aux["idx3"])

    w = BRANCH_W
    dh2 = _mm_from_slices(dconv3, p["win"], 0, dh2, name=f"inproj_dh_conv_{tag}")
    dh2 = _mm_from_slices(dret4, p["win"], 3 * w, dh2, name=f"inproj_dh_ret_{tag}")
    dh2 = _mm_from_slices(datt3, p["win"], 7 * w, dh2, name=f"inproj_dh_att_{tag}")
    grads["win"] = jnp.concatenate([
        _mm_tn_slices(dconv3, s["h2"], name=f"inproj_dw_conv_{tag}"),
        _mm_tn_slices(dret4, s["h2"], name=f"inproj_dw_ret_{tag}"),
        _mm_tn_slices(datt3, s["h2"], name=f"inproj_dw_att_{tag}")], axis=0)
    dx1, grads["nmix"] = _rmsnorm_bwd(dh2, s["x1"], p["nmix"], dx2, name=f"mixnorm_bwd_{tag}")

    dx0, grads["n1"], grads["wg1"], grads["wu1"], grads["wd1"] = _ffn_grads(
        dx1, s["x0"], s["h1"], s["g1"], s["u1"], s["a1"], p["n1"], p["wg1"], p["wu1"], p["wd1"], f"ffn1_{tag}")
    return dx0, grads


def _device_step(x, target, layers, final_norm):
    t = x.shape[0]
    aux = dict(idx3=_att_bias_index(), rope=_rope_tables(t), ret=_ret_consts())
    saved = []
    for l, p in enumerate(layers):
        x, s = _layer_fwd(x, p, aux, f"l{l}")
        saved.append(s)
    dx, dfinal, loss_row = _loss_fwd_bwd(x, final_norm, target, name="loss_fwd_bwd")
    grads = [None] * len(layers)
    for l in reversed(range(len(layers))):
        dx, grads[l] = _layer_bwd(dx, layers[l], saved[l], aux, f"l{l}")
    return loss_row, dx, grads, dfinal


def _my_position():
    x, y, c = lax.axis_index("x"), lax.axis_index("y"), lax.axis_index("c")
    return x, y, c, 4 * x + 2 * y + c


def _peer(x, y, c, k):
    px = 1 - x if k & 4 else x
    py = 1 - y if k & 2 else y
    pc = 1 - c if k & 1 else c
    return (px, py, pc), 4 * px + 2 * py + pc


def _all_gather(shard, *, name):
    def body(x_ref, o_ref, send_sems, recv_sems, local_sem):
        x, y, c, me = _my_position()
        mine = pltpu.make_async_copy(x_ref, o_ref.at[me], local_sem)
        mine.start()
        sends = []
        for k in range(1, N_DEV):
            peer, _ = _peer(x, y, c, k)
            cp = pltpu.make_async_remote_copy(src_ref=x_ref, dst_ref=o_ref.at[me], send_sem=send_sems.at[k - 1],
                                              recv_sem=recv_sems.at[k - 1], device_id=peer, device_id_type=MESH)
            cp.start()
            sends.append(cp)
        for k in range(1, N_DEV):
            peer, peer_id = _peer(x, y, c, k)
            pltpu.make_async_remote_copy(src_ref=x_ref, dst_ref=o_ref.at[peer_id], send_sem=send_sems.at[k - 1],
                                         recv_sem=recv_sems.at[k - 1], device_id=peer, device_id_type=MESH).wait_recv()
        for cp in sends:
            cp.wait_send()
        mine.wait()

    return pl.pallas_call(
        body, name=name, in_specs=[pl.BlockSpec(memory_space=pl.ANY)], out_specs=pl.BlockSpec(memory_space=pl.ANY),
        out_shape=jax.ShapeDtypeStruct((N_DEV,) + shard.shape, shard.dtype),
        scratch_shapes=[pltpu.SemaphoreType.DMA((N_DEV - 1,)), pltpu.SemaphoreType.DMA((N_DEV - 1,)),
                        pltpu.SemaphoreType.DMA],
    )(shard)


def _scatter_partials(full, *, name):
    def body(g_ref, o_ref, send_sems, recv_sems, local_sem):
        x, y, c, me = _my_position()
        mine = pltpu.make_async_copy(g_ref.at[me], o_ref.at[me], local_sem)
        mine.start()
        sends = []
        for k in range(1, N_DEV):
            peer, peer_id = _peer(x, y, c, k)
            cp = pltpu.make_async_remote_copy(src_ref=g_ref.at[peer_id], dst_ref=o_ref.at[me],
                                              send_sem=send_sems.at[k - 1], recv_sem=recv_sems.at[k - 1],
                                              device_id=peer, device_id_type=MESH)
            cp.start()
            sends.append(cp)
        for k in range(1, N_DEV):
            peer, peer_id = _peer(x, y, c, k)
            pltpu.make_async_remote_copy(src_ref=g_ref.at[me], dst_ref=o_ref.at[peer_id],
                                         send_sem=send_sems.at[k - 1], recv_sem=recv_sems.at[k - 1],
                                         device_id=peer, device_id_type=MESH).wait_recv()
        for cp in sends:
            cp.wait_send()
        mine.wait()

    return pl.pallas_call(
        body, name=name, in_specs=[pl.BlockSpec(memory_space=pl.ANY)], out_specs=pl.BlockSpec(memory_space=pl.ANY),
        out_shape=jax.ShapeDtypeStruct(full.shape, full.dtype),
        scratch_shapes=[pltpu.SemaphoreType.DMA((N_DEV - 1,)), pltpu.SemaphoreType.DMA((N_DEV - 1,)),
                        pltpu.SemaphoreType.DMA],
    )(full)


def _sum_slots(parts, *, name, tr=512):
    n, r, cdim = parts.shape
    tr = _tile(r, tr)

    def body(p_ref, o_ref):
        acc = p_ref[0].astype(F32)
        for s in range(1, n):
            acc = acc + p_ref[s].astype(F32)
        o_ref[...] = acc

    return pl.pallas_call(
        body, name=name, grid=(r // tr,), in_specs=[pl.BlockSpec((n, tr, cdim), lambda i: (0, i, 0))],
        out_specs=pl.BlockSpec((tr, cdim), lambda i: (i, 0)), out_shape=jax.ShapeDtypeStruct((r, cdim), F32),
        compiler_params=_cparams("parallel"),
    )(parts)


def _all_reduce_small(v, *, name):
    r = v.shape[0]

    def body(x_ref, o_ref, slots, send_sems, recv_sems):
        x, y, c, me = _my_position()
        slots[me] = x_ref[...]
        sends = []
        for k in range(1, N_DEV):
            peer, _ = _peer(x, y, c, k)
            cp = pltpu.make_async_remote_copy(src_ref=x_ref, dst_ref=slots.at[me], send_sem=send_sems.at[k - 1],
                                              recv_sem=recv_sems.at[k - 1], device_id=peer, device_id_type=MESH)
            cp.start()
            sends.append(cp)
        for k in range(1, N_DEV):
            peer, peer_id = _peer(x, y, c, k)
            pltpu.make_async_remote_copy(src_ref=x_ref, dst_ref=slots.at[peer_id], send_sem=send_sems.at[k - 1],
                                         recv_sem=recv_sems.at[k - 1], device_id=peer, device_id_type=MESH).wait_recv()
        for cp in sends:
            cp.wait_send()
        acc = slots[0]
        for s in range(1, N_DEV):
            acc = acc + slots[s]
        o_ref[...] = acc

    return pl.pallas_call(
        body, name=name, in_specs=[pl.BlockSpec(memory_space=pltpu.VMEM)],
        out_specs=pl.BlockSpec(memory_space=pltpu.VMEM), out_shape=jax.ShapeDtypeStruct((r, 128), F32),
        scratch_shapes=[pltpu.VMEM((N_DEV, r, 128), F32), pltpu.SemaphoreType.DMA((N_DEV - 1,)),
                        pltpu.SemaphoreType.DMA((N_DEV - 1,))],
    )(v)


def _adamw(w, g, m, v, *, name, tr=256):
    shape = w.shape
    cdim = shape[-1]
    w2, g2, m2, v2 = (a.reshape(-1, cdim) for a in (w, g, m, v))
    r = w2.shape[0]
    tr = _tile(r, tr) if r % 8 == 0 else r
    c1 = 1.0 - ADAM_B1 ** ADAM_STEP
    c2 = 1.0 - ADAM_B2 ** ADAM_STEP

    def body(w_ref, g_ref, m_ref, v_ref, d_ref, mo_ref, vo_ref):
        gv = g_ref[...]
        mn = ADAM_B1 * m_ref[...] + (1.0 - ADAM_B1) * gv
        vn = ADAM_B2 * v_ref[...] + (1.0 - ADAM_B2) * (gv * gv)
        d_ref[...] = -ADAM_LR * ((mn / c1) / (jnp.sqrt(vn / c2) + ADAM_EPS) + ADAM_WD * w_ref[...])
        mo_ref[...] = mn
        vo_ref[...] = vn

    spec = pl.BlockSpec((tr, cdim), lambda i: (i, 0))
    outs = pl.pallas_call(
        body, name=name, grid=(r // tr,), in_specs=[spec] * 4, out_specs=[spec] * 3,
        out_shape=[jax.ShapeDtypeStruct((r, cdim), F32)] * 3, compiler_params=_cparams("parallel"),
    )(w2, g2, m2, v2)
    return tuple(o.reshape(shape) for o in outs)


BIG = ("wg1", "wu1", "wd1", "win", "wb", "wmg", "wout", "wg2", "wu2", "wd2")


def _to_rows(name, w, d):
    if name in ("wg1", "wu1", "wg2", "wu2", "win"):
        return w.T
    if name == "wb":
        return w.transpose(0, 2, 1).reshape(-1, d)
    if name == "wmg":
        return w.reshape(-1, d)
    return w


def _from_rows(name, rows, d):
    if name in ("wg1", "wu1", "wg2", "wu2", "win"):
        return rows.T
    if name == "wb":
        return rows.reshape(3, -1, BRANCH_W).transpose(0, 2, 1)
    if name == "wmg":
        return rows.reshape(3, -1, d)
    return rows


def _full_from_gathered(name, g, d):
    if name == "wb":
        return g.reshape(N_DEV, 3, -1, BRANCH_W).transpose(1, 0, 2, 3).reshape(3, d, BRANCH_W)
    if name == "wmg":
        return g.reshape(N_DEV, 3, -1, d).transpose(1, 0, 2, 3).reshape(3, d, d)
    return g.reshape(-1, d)


def _gathered_from_full(name, full, d):
    if name == "wb":
        return full.reshape(3, N_DEV, -1, BRANCH_W).transpose(1, 0, 2, 3).reshape(N_DEV, -1, d)
    if name == "wmg":
        return full.reshape(3, N_DEV, -1, d).transpose(1, 0, 2, 3).reshape(N_DEV, -1, d)
    return full.reshape(N_DEV, -1, d)


def kernel(x, ffn1_norm, ffn1_w_gate, ffn1_w_up, ffn1_w_down, mix_norm, w_in, conv_w, rel_bias, w_branch, w_merge_gate, w_out, ffn2_norm, ffn2_w_gate, ffn2_w_up, ffn2_w_down, final_norm, loss_target, m_ffn1_norm, m_ffn1_w_gate, m_ffn1_w_up, m_ffn1_w_down, m_mix_norm, m_w_in, m_conv_w, m_rel_bias, m_w_branch, m_w_merge_gate, m_w_out, m_ffn2_norm, m_ffn2_w_gate, m_ffn2_w_up, m_ffn2_w_down, m_final_norm, v_ffn1_norm, v_ffn1_w_gate, v_ffn1_w_up, v_ffn1_w_down, v_mix_norm, v_w_in, v_conv_w, v_rel_bias, v_w_branch, v_w_merge_gate, v_w_out, v_ffn2_norm, v_ffn2_w_gate, v_ffn2_w_up, v_ffn2_w_down, v_final_norm):
    names = ["ffn1_norm", "ffn1_w_gate", "ffn1_w_up", "ffn1_w_down", "mix_norm", "w_in", "conv_w", "rel_bias",
             "w_branch", "w_merge_gate", "w_out", "ffn2_norm", "ffn2_w_gate", "ffn2_w_up", "ffn2_w_down", "final_norm"]
    weights = dict(zip(names, (ffn1_norm, ffn1_w_gate, ffn1_w_up, ffn1_w_down, mix_norm, w_in, conv_w, rel_bias,
                               w_branch, w_merge_gate, w_out, ffn2_norm, ffn2_w_gate, ffn2_w_up, ffn2_w_down,
                               final_norm)))
    m_in = dict(zip(names, (m_ffn1_norm, m_ffn1_w_gate, m_ffn1_w_up, m_ffn1_w_down, m_mix_norm, m_w_in, m_conv_w,
                            m_rel_bias, m_w_branch, m_w_merge_gate, m_w_out, m_ffn2_norm, m_ffn2_w_gate,
                            m_ffn2_w_up, m_ffn2_w_down, m_final_norm)))
    v_in = dict(zip(names, (v_ffn1_norm, v_ffn1_w_gate, v_ffn1_w_up, v_ffn1_w_down, v_mix_norm, v_w_in, v_conv_w,
                            v_rel_bias, v_w_branch, v_w_merge_gate, v_w_out, v_ffn2_norm, v_ffn2_w_gate,
                            v_ffn2_w_up, v_ffn2_w_down, v_final_norm)))
    big_of = dict(wg1="ffn1_w_gate", wu1="ffn1_w_up", wd1="ffn1_w_down", win="w_in", wb="w_branch",
                  wmg="w_merge_gate", wout="w_out", wg2="ffn2_w_gate", wu2="ffn2_w_up", wd2="ffn2_w_down")
    depth = ffn1_norm.shape[0]
    d = x.shape[-1]
    xs = x.reshape(-1, d)
    target = loss_target.reshape(-1, d)
    _, _, _, me = _my_position()

    blocks, layout, row = [], [], 0
    for l in range(depth):
        for nm in BIG:
            rows = _to_rows(nm, weights[big_of[nm]][l], d)
            blocks.append(rows.astype(BF16))
            layout.append((l, nm, row, rows.shape[0]))
            row += rows.shape[0]
    gathered = _all_gather(jnp.concatenate(blocks, axis=0), name="gather_weights")
    conv_cols = conv_w.shape[-1]
    conv_full = _all_gather(conv_w.reshape(depth * 3, conv_cols), name="gather_conv_w")
    conv_full = conv_full.reshape(N_DEV, depth, 3, conv_cols).transpose(1, 2, 0, 3).reshape(depth, 3, -1)
    layers = [dict(n1=ffn1_norm[l][None], nmix=mix_norm[l][None], n2=ffn2_norm[l][None], conv_w=conv_full[l],
                   rel_bias=rel_bias[l]) for l in range(depth)]
    for l, nm, r0, nr in layout:
        layers[l][nm] = _full_from_gathered(nm, gathered[:, r0:r0 + nr], d)

    loss_row, dx, grads, dfinal = _device_step(xs, target, layers, final_norm[None])

    full = jnp.concatenate([_gathered_from_full(nm, grads[l][nm], d).astype(BF16) for l, nm, _, _ in layout], axis=1)
    shard_grad = _sum_slots(_scatter_partials(full, name="scatter_grads"), name="sum_grads")
    big_grads = {nm: [None] * depth for nm in BIG}
    for l, nm, r0, nr in layout:
        big_grads[nm][l] = _from_rows(nm, shard_grad[r0:r0 + nr], d)

    small = {"ffn1_norm": jnp.stack([grads[l]["n1"][0] for l in range(depth)]),
             "mix_norm": jnp.stack([grads[l]["nmix"][0] for l in range(depth)]),
             "ffn2_norm": jnp.stack([grads[l]["n2"][0] for l in range(depth)]),
             "final_norm": dfinal[0],
             "rel_bias": jnp.stack([grads[l]["rel_bias"] for l in range(depth)]),
             "conv_w": jnp.stack([grads[l]["conv_w"] for l in range(depth)]),
             "loss": loss_row[0, :1]}
    order = list(small)
    flat = jnp.concatenate([small[k].reshape(-1) for k in order])
    pad = (-flat.shape[0]) % 1024
    summed = _all_reduce_small(jnp.pad(flat, (0, pad)).reshape(-1, 128), name="reduce_small").reshape(-1)
    pos = 0
    for k in order:
        n = small[k].size
        small[k] = summed[pos:pos + n].reshape(small[k].shape)
        pos += n
    small["conv_w"] = lax.dynamic_slice_in_dim(small["conv_w"], me * conv_cols, conv_cols, axis=2)

    grad_w = {big_of[nm]: jnp.stack(big_grads[nm]) for nm in BIG}
    grad_w.update({k: small[k] for k in order if k != "loss"})
    delta, new_m, new_v = {}, {}, {}
    for nm in names:
        delta[nm], new_m[nm], new_v[nm] = _adamw(weights[nm], grad_w[nm], m_in[nm], v_in[nm], name=f"adamw_{nm}")
    return (small["loss"].reshape(()), dx.reshape(x.shape), *[grad_w[n] for n in names], *[delta[n] for n in names],
            *[new_m[n] for n in names], *[new_v[n] for n in names])
```

```python
import functools
import math

import jax
import jax.numpy as jnp
from jax import lax
from jax.experimental import pallas as pl
from jax.experimental.pallas import tpu as pltpu

F32 = jnp.float32
BF16 = jnp.bfloat16

N_DEV = 8
EPS = 1e-6
CHUNK = 64
BRANCH_W = 512
N_SLICES = 10
H_RET = 4
DK_RET = 128
H_ATT = 8
DH_ATT = 64
N_PREV_CHUNKS = 8
REL_CLIP = 128
N_REL = 2 * REL_CLIP + 1
NEG_INF = -1e30
ROPE_BASE = 10000.0
ATT_QB = 256
ATT_LOOKBACK = N_PREV_CHUNKS * CHUNK
ATT_WIN = ATT_LOOKBACK + ATT_QB
RET_TB = 512
CONV_HALO = 16

ADAM_LR = 0.001
ADAM_B1 = 0.9
ADAM_B2 = 0.999
ADAM_EPS = 1e-08
ADAM_WD = 0.01
ADAM_STEP = 10

VMEM_LIMIT_BYTES = 56 * 1024 * 1024
MESH = pl.DeviceIdType.MESH


def _cparams(*sem):
    return pltpu.CompilerParams(dimension_semantics=sem, vmem_limit_bytes=VMEM_LIMIT_BYTES)


def _dot(a, b):
    return lax.dot_general(a, b, (((1,), (0,)), ((), ())), preferred_element_type=F32)


def _dot_nt(a, b):
    return lax.dot_general(a, b, (((1,), (1,)), ((), ())), preferred_element_type=F32)


def _dot_tn(a, b):
    return lax.dot_general(a, b, (((0,), (0,)), ((), ())), preferred_element_type=F32)


def _bf(v):
    return v.astype(BF16)


def _sigmoid(v):
    return 1.0 / (1.0 + jnp.exp(-v))


def _tile(n, want):
    if n <= want:
        return n
    for t in range(want - want % 128, 0, -128):
        if n % t == 0:
            return t
    t = want
    while n % t:
        t //= 2
    return t


def _matmul(a, b, *, dims, grid, a_spec, b_spec, o_spec, out_shape, acc_shape, name, scale=1.0,
            res=None, res_spec=None):
    nk = grid[3]
    has_res = res is not None

    def body(*refs):
        if has_res:
            a_ref, b_ref, r_ref, o_ref = refs[:4]
        else:
            a_ref, b_ref, o_ref = refs[:3]
            r_ref = None

        def finish(acc):
            if scale != 1.0:
                acc = acc * scale
            if r_ref is not None:
                acc = acc + r_ref[...].astype(F32)
            o_ref[...] = acc.astype(o_ref.dtype)

        part = lax.dot_general(_bf(a_ref[...]), _bf(b_ref[...]), (dims, ((), ())), preferred_element_type=F32)
        if nk == 1:
            finish(part)
        else:
            acc_ref = refs[-1]
            k = pl.program_id(3)

            @pl.when(k == 0)
            def _():
                acc_ref[...] = part

            @pl.when(k > 0)
            def _():
                acc_ref[...] += part

            @pl.when(k == nk - 1)
            def _():
                finish(acc_ref[...])

    in_specs = [a_spec, b_spec] + ([res_spec] if has_res else [])
    args = (a, b) + ((res,) if has_res else ())
    return pl.pallas_call(
        body, name=name, grid=grid, in_specs=in_specs, out_specs=o_spec, out_shape=out_shape,
        scratch_shapes=[] if nk == 1 else [pltpu.VMEM(acc_shape, F32)],
        compiler_params=_cparams("parallel", "parallel", "parallel", "arbitrary"),
    )(*args)


NT = ((1,), (1,))
NN = ((1,), (0,))
TN = ((0,), (0,))


def _mm_rows(a, b, *, dims, out_dtype, name, res=None, scale=1.0, tm=1024, tn=1024, tk=1024):
    m, kdim = a.shape
    n = b.shape[1] if dims == NN else b.shape[0]
    tm, tn, tk = _tile(m, tm), _tile(n, tn), _tile(kdim, tk)
    grid = (1, m // tm, n // tn, kdim // tk)
    a_spec = pl.BlockSpec((tm, tk), lambda s, i, j, k: (i, k))
    if dims == NN:
        b_spec = pl.BlockSpec((tk, tn), lambda s, i, j, k: (k, j))
    else:
        b_spec = pl.BlockSpec((tn, tk), lambda s, i, j, k: (j, k))
    o_spec = pl.BlockSpec((tm, tn), lambda s, i, j, k: (i, j))
    return _matmul(a, b, dims=dims, grid=grid, a_spec=a_spec, b_spec=b_spec, o_spec=o_spec,
                   out_shape=jax.ShapeDtypeStruct((m, n), out_dtype), acc_shape=(tm, tn), name=name,
                   res=res, res_spec=o_spec if res is not None else None, scale=scale)


def _mm_to_slices(a, b_t, *, name, tm=1024):
    m, kdim = a.shape
    n = b_t.shape[0]
    tm = _tile(m, tm)
    w = BRANCH_W
    grid = (1, m // tm, n // w, 1)
    return _matmul(a, b_t, dims=NT, grid=grid,
                   a_spec=pl.BlockSpec((tm, kdim), lambda s, i, j, k: (i, 0)),
                   b_spec=pl.BlockSpec((w, kdim), lambda s, i, j, k: (j, 0)),
                   o_spec=pl.BlockSpec((None, tm, w), lambda s, i, j, k: (j, i, 0)),
                   out_shape=jax.ShapeDtypeStruct((n // w, m, w), BF16), acc_shape=(tm, w), name=name)


def _mm_from_slices(a3, b, row0, res, *, name, tm=1024):
    s_n, m, w = a3.shape
    n = b.shape[1]
    tm = _tile(m, tm)
    off = row0 // w
    grid = (1, m // tm, 1, s_n)
    o_spec = pl.BlockSpec((tm, n), lambda s, i, j, k: (i, 0))
    return _matmul(a3, b, dims=NN, grid=grid,
                   a_spec=pl.BlockSpec((None, tm, w), lambda s, i, j, k: (k, i, 0)),
                   b_spec=pl.BlockSpec((w, n), lambda s, i, j, k: (k + off, 0)),
                   o_spec=o_spec, out_shape=jax.ShapeDtypeStruct((m, n), F32), acc_shape=(tm, n), name=name,
                   res=res, res_spec=o_spec)


def _mm_tn(a, b, *, name, tm=1408, tn=1024, tk=512):
    t, m = a.shape
    n = b.shape[1]
    tm, tn, tk = _tile(m, tm), _tile(n, tn), _tile(t, tk)
    grid = (1, m // tm, n // tn, t // tk)
    return _matmul(a, b, dims=TN, grid=grid,
                   a_spec=pl.BlockSpec((tk, tm), lambda s, i, j, k: (k, i)),
                   b_spec=pl.BlockSpec((tk, tn), lambda s, i, j, k: (k, j)),
                   o_spec=pl.BlockSpec((tm, tn), lambda s, i, j, k: (i, j)),
                   out_shape=jax.ShapeDtypeStruct((m, n), F32), acc_shape=(tm, tn), name=name)


def _mm_tn_slices(a3, b, *, name, tn=1024, tk=512):
    s_n, t, w = a3.shape
    n = b.shape[1]
    tn, tk = _tile(n, tn), _tile(t, tk)
    grid = (1, s_n, n // tn, t // tk)
    return _matmul(a3, b, dims=TN, grid=grid,
                   a_spec=pl.BlockSpec((None, tk, w), lambda s, i, j, k: (i, k, 0)),
                   b_spec=pl.BlockSpec((tk, tn), lambda s, i, j, k: (k, j)),
                   o_spec=pl.BlockSpec((w, tn), lambda s, i, j, k: (i, j)),
                   out_shape=jax.ShapeDtypeStruct((s_n * w, n), F32), acc_shape=(w, tn), name=name)


def _mm_tn_batch(a, b3, *, name, a_batched, tm=1024, tn=1024, tk=512):
    s_n, t, n = b3.shape
    m = a.shape[-1]
    tm, tn, tk = _tile(m, tm), _tile(n, tn), _tile(t, tk)
    grid = (s_n, m // tm, n // tn, t // tk)
    if a_batched:
        a_spec = pl.BlockSpec((None, tk, tm), lambda s, i, j, k: (s, k, i))
    else:
        a_spec = pl.BlockSpec((tk, tm), lambda s, i, j, k: (k, i))
    return _matmul(a, b3, dims=TN, grid=grid, a_spec=a_spec,
                   b_spec=pl.BlockSpec((None, tk, tn), lambda s, i, j, k: (s, k, j)),
                   o_spec=pl.BlockSpec((None, tm, tn), lambda s, i, j, k: (s, i, j)),
                   out_shape=jax.ShapeDtypeStruct((s_n, m, n), F32), acc_shape=(tm, tn), name=name)


def _norm_parts(xf):
    r = lax.rsqrt(jnp.mean(xf * xf, axis=-1, keepdims=True) + EPS)
    return xf * r, r


def _norm_bwd(dh, xhat, r, w):
    dxhat = dh * w
    dx = r * (dxhat - xhat * jnp.mean(dxhat * xhat, axis=-1, keepdims=True))
    return dx, jnp.sum(dh * xhat, axis=0, keepdims=True)


def _rmsnorm_fwd(x, w, *, name, tm=1024):
    t, d = x.shape
    tm = _tile(t, tm)

    def body(x_ref, w_ref, h_ref):
        xhat, _ = _norm_parts(x_ref[...])
        h_ref[...] = _bf(xhat * w_ref[...])

    return pl.pallas_call(
        body, name=name, grid=(t // tm,),
        in_specs=[pl.BlockSpec((tm, d), lambda i: (i, 0)), pl.BlockSpec((1, d), lambda i: (0, 0))],
        out_specs=pl.BlockSpec((tm, d), lambda i: (i, 0)),
        out_shape=jax.ShapeDtypeStruct((t, d), BF16), compiler_params=_cparams("parallel"),
    )(x, w)


def _rmsnorm_bwd(dh, x, w, dres, *, name, tm=1024):
    t, d = x.shape
    tm = _tile(t, tm)

    def body(dh_ref, x_ref, w_ref, dres_ref, dx_ref, dw_ref):
        xhat, r = _norm_parts(x_ref[...])
        dx, dw = _norm_bwd(dh_ref[...], xhat, r, w_ref[...])
        dx_ref[...] = dres_ref[...] + dx

        @pl.when(pl.program_id(0) == 0)
        def _():
            dw_ref[...] = dw

        @pl.when(pl.program_id(0) > 0)
        def _():
            dw_ref[...] += dw

    row = pl.BlockSpec((tm, d), lambda i: (i, 0))
    vec = pl.BlockSpec((1, d), lambda i: (0, 0))
    return pl.pallas_call(
        body, name=name, grid=(t // tm,), in_specs=[row, row, vec, row], out_specs=[row, vec],
        out_shape=[jax.ShapeDtypeStruct((t, d), F32), jax.ShapeDtypeStruct((1, d), F32)],
        compiler_params=_cparams("arbitrary"),
    )(dh, x, w, dres)


def _loss_fwd_bwd(x, w, target, *, name, tm=1024):
    t, d = x.shape
    tm = _tile(t, tm)

    def body(x_ref, w_ref, t_ref, dx_ref, dw_ref, loss_ref):
        xhat, r = _norm_parts(x_ref[...])
        wv = w_ref[...]
        err = xhat * wv - t_ref[...]
        dx, dw = _norm_bwd(err * (1.0 / d), xhat, r, wv)
        dx_ref[...] = dx
        part = jnp.full((1, 128), 0.5 / d, F32) * jnp.sum(err * err)

        @pl.when(pl.program_id(0) == 0)
        def _():
            dw_ref[...] = dw
            loss_ref[...] = part

        @pl.when(pl.program_id(0) > 0)
        def _():
            dw_ref[...] += dw
            loss_ref[...] += part

    row = pl.BlockSpec((tm, d), lambda i: (i, 0))
    vec = pl.BlockSpec((1, d), lambda i: (0, 0))
    return pl.pallas_call(
        body, name=name, grid=(t // tm,), in_specs=[row, vec, row],
        out_specs=[row, vec, pl.BlockSpec((1, 128), lambda i: (0, 0))],
        out_shape=[jax.ShapeDtypeStruct((t, d), F32), jax.ShapeDtypeStruct((1, d), F32),
                   jax.ShapeDtypeStruct((1, 128), F32)],
        compiler_params=_cparams("arbitrary"),
    )(x, w, target)


def _ffn_tiles(t, f):
    tf = f
    for cand in (1408, 1024, 512, 256, 128):
        if f % cand == 0:
            tf = cand
            break
    return _tile(t, 512), tf


def _ffn_fwd(x, nw, wg_t, wu_t, wd, *, name):
    t, d = x.shape
    f = wd.shape[0]
    tm, tf = _ffn_tiles(t, f)
    nf = f // tf

    def body(x_ref, nw_ref, wg_ref, wu_ref, wd_ref, xo_ref, h_ref, g_ref, u_ref, a_ref, hs_ref, acc_ref):
        j = pl.program_id(1)

        @pl.when(j == 0)
        def _():
            xhat, _ = _norm_parts(x_ref[...])
            hb = _bf(xhat * nw_ref[...])
            hs_ref[...] = hb
            h_ref[...] = hb

        hb = hs_ref[...]
        g = _dot_nt(hb, wg_ref[...])
        u = _dot_nt(hb, wu_ref[...])
        a = _bf(g * _sigmoid(g) * u)
        g_ref[...] = _bf(g)
        u_ref[...] = _bf(u)
        a_ref[...] = a
        part = _dot(a, wd_ref[...])

        @pl.when(j == 0)
        def _():
            acc_ref[...] = part

        @pl.when(j > 0)
        def _():
            acc_ref[...] += part

        @pl.when(j == nf - 1)
        def _():
            xo_ref[...] = x_ref[...] + 0.5 * acc_ref[...]

    row = pl.BlockSpec((tm, d), lambda i, j: (i, 0))
    wspec = pl.BlockSpec((tf, d), lambda i, j: (j, 0))
    hid = pl.BlockSpec((tm, tf), lambda i, j: (i, j))
    return pl.pallas_call(
        body, name=name, grid=(t // tm, nf),
        in_specs=[row, pl.BlockSpec((1, d), lambda i, j: (0, 0)), wspec, wspec, wspec],
        out_specs=[row, row, hid, hid, hid],
        out_shape=[jax.ShapeDtypeStruct((t, d), F32), jax.ShapeDtypeStruct((t, d), BF16)]
        + [jax.ShapeDtypeStruct((t, f), BF16)] * 3,
        scratch_shapes=[pltpu.VMEM((tm, d), BF16), pltpu.VMEM((tm, d), F32)],
        compiler_params=_cparams("parallel", "arbitrary"),
    )(x, nw, wg_t, wu_t, wd)


def _ffn_bwd(dxo, x, nw, g, u, wg_t, wu_t, wd, *, name):
    t, d = x.shape
    f = wd.shape[0]
    tm, tf = _ffn_tiles(t, f)
    nf = f // tf

    def body(dxo_ref, x_ref, nw_ref, g_ref, u_ref, wg_ref, wu_ref, wd_ref,
             dx_ref, dy_ref, dg_ref, du_ref, dnw_ref, dys_ref, acc_ref):
        i, j = pl.program_id(0), pl.program_id(1)

        @pl.when(j == 0)
        def _():
            dyb = _bf(0.5 * dxo_ref[...])
            dys_ref[...] = dyb
            dy_ref[...] = dyb

        da = _dot_nt(dys_ref[...], wd_ref[...])
        gv = g_ref[...].astype(F32)
        uv = u_ref[...].astype(F32)
        s = _sigmoid(gv)
        dg = _bf(da * uv * (s * (1.0 + gv * (1.0 - s))))
        du = _bf(da * (gv * s))
        dg_ref[...] = dg
        du_ref[...] = du
        part = _dot(dg, wg_ref[...]) + _dot(du, wu_ref[...])

        @pl.when(j == 0)
        def _():
            acc_ref[...] = part

        @pl.when(j > 0)
        def _():
            acc_ref[...] += part

        @pl.when(j == nf - 1)
        def _():
            xhat, r = _norm_parts(x_ref[...])
            dx, dw = _norm_bwd(acc_ref[...], xhat, r, nw_ref[...])
            dx_ref[...] = dxo_ref[...] + dx

            @pl.when(i == 0)
            def _():
                dnw_ref[...] = dw

            @pl.when(i > 0)
            def _():
                dnw_ref[...] += dw

    row = pl.BlockSpec((tm, d), lambda i, j: (i, 0))
    vec = pl.BlockSpec((1, d), lambda i, j: (0, 0))
    wspec = pl.BlockSpec((tf, d), lambda i, j: (j, 0))
    hid = pl.BlockSpec((tm, tf), lambda i, j: (i, j))
    return pl.pallas_call(
        body, name=name, grid=(t // tm, nf),
        in_specs=[row, row, vec, hid, hid, wspec, wspec, wspec],
        out_specs=[row, row, hid, hid, vec],
        out_shape=[jax.ShapeDtypeStruct((t, d), F32), jax.ShapeDtypeStruct((t, d), BF16),
                   jax.ShapeDtypeStruct((t, f), BF16), jax.ShapeDtypeStruct((t, f), BF16),
                   jax.ShapeDtypeStruct((1, d), F32)],
        scratch_shapes=[pltpu.VMEM((tm, d), BF16), pltpu.VMEM((tm, d), F32)],
        compiler_params=_cparams("arbitrary", "arbitrary"),
    )(dxo, x, nw, g, u, wg_t, wu_t, wd)


def _shift_down(prev, cur, n):
    ext = jnp.concatenate([prev, cur], axis=0)
    return pltpu.roll(ext, n, axis=0)[prev.shape[0]:]


def _shift_up(cur, nxt, n):
    ext = jnp.concatenate([cur, nxt], axis=0)
    return pltpu.roll(ext, ext.shape[0] - n, axis=0)[:cur.shape[0]]


def _conv_specs(t, tb):
    hb = tb // CONV_HALO
    last = t // CONV_HALO - 1

    def tile(s):
        return pl.BlockSpec((None, tb, 128), lambda c, i: (s, i, c))

    def prev(s):
        return pl.BlockSpec((None, CONV_HALO, 128), lambda c, i: (s, jnp.maximum(i * hb - 1, 0), c))

    def nxt(s):
        return pl.BlockSpec((None, CONV_HALO, 128), lambda c, i: (s, jnp.minimum((i + 1) * hb, last), c))

    return tile, prev, nxt


def _conv_fwd(cols3, conv_w, *, name, tb=1024):
    _, t, bw = cols3.shape
    tb = _tile(t, tb)
    tile, prev, _ = _conv_specs(t, tb)

    def body(u_ref, b_ref, c_ref, up_ref, cp_ref, w_ref, y_ref):
        first = pl.program_id(1) == 0
        z = c_ref[...].astype(F32) * u_ref[...].astype(F32)
        zp = jnp.where(first, 0.0, cp_ref[...].astype(F32) * up_ref[...].astype(F32))
        conv = w_ref[0:1, :] * _shift_down(zp, z, 2) + w_ref[1:2, :] * _shift_down(zp, z, 1) + w_ref[2:3, :] * z
        y_ref[...] = _bf(b_ref[...].astype(F32) * conv)

    return pl.pallas_call(
        body, name=name, grid=(bw // 128, t // tb),
        in_specs=[tile(0), tile(1), tile(2), prev(0), prev(2), pl.BlockSpec((3, 128), lambda c, i: (0, c))],
        out_specs=pl.BlockSpec((tb, 128), lambda c, i: (i, c)),
        out_shape=jax.ShapeDtypeStruct((t, bw), BF16), compiler_params=_cparams("parallel", "parallel"),
    )(cols3, cols3, cols3, cols3, cols3, conv_w)


def _conv_bwd(cols3, conv_w, dy, *, name, tb=1024):
    _, t, bw = cols3.shape
    tb = _tile(t, tb)
    nt = t // tb
    tile, prev, nxt = _conv_specs(t, tb)
    hb = tb // CONV_HALO
    last = t // CONV_HALO - 1

    def body(u_ref, b_ref, c_ref, up_ref, cp_ref, bn_ref, dy_ref, dyn_ref, w_ref, d3_ref, dw_ref):
        i = pl.program_id(1)
        uv, bv, cv = u_ref[...].astype(F32), b_ref[...].astype(F32), c_ref[...].astype(F32)
        dyv = dy_ref[...].astype(F32)
        z = cv * uv
        zp = jnp.where(i == 0, 0.0, cp_ref[...].astype(F32) * up_ref[...].astype(F32))
        z1, z2 = _shift_down(zp, z, 1), _shift_down(zp, z, 2)
        w0, w1, w2 = w_ref[0:1, :], w_ref[1:2, :], w_ref[2:3, :]
        conv = w0 * z2 + w1 * z1 + w2 * z
        dconv = dyv * bv
        dconv_n = jnp.where(i == nt - 1, 0.0, dyn_ref[...].astype(F32) * bn_ref[...].astype(F32))
        dz = w2 * dconv + w1 * _shift_up(dconv, dconv_n, 1) + w0 * _shift_up(dconv, dconv_n, 2)
        d3_ref[0] = _bf(dz * cv)
        d3_ref[1] = _bf(dyv * conv)
        d3_ref[2] = _bf(dz * uv)
        dws = [jnp.sum(dconv * zz, axis=0, keepdims=True) for zz in (z2, z1, z)]

        @pl.when(i == 0)
        def _():
            for j in range(3):
                dw_ref[j:j + 1, :] = dws[j]

        @pl.when(i > 0)
        def _():
            for j in range(3):
                dw_ref[j:j + 1, :] += dws[j]

    dy_tile = pl.BlockSpec((tb, 128), lambda c, i: (i, c))
    dy_next = pl.BlockSpec((CONV_HALO, 128), lambda c, i: (jnp.minimum((i + 1) * hb, last), c))
    wspec = pl.BlockSpec((3, 128), lambda c, i: (0, c))
    return pl.pallas_call(
        body, name=name, grid=(bw // 128, nt),
        in_specs=[tile(0), tile(1), tile(2), prev(0), prev(2), nxt(1), dy_tile, dy_next, wspec],
        out_specs=[pl.BlockSpec((3, tb, 128), lambda c, i: (0, i, c)), wspec],
        out_shape=[jax.ShapeDtypeStruct((3, t, bw), BF16), jax.ShapeDtypeStruct((3, bw), F32)],
        compiler_params=_cparams("parallel", "arbitrary"),
    )(cols3, cols3, cols3, cols3, cols3, cols3, dy, dy, conv_w)


def _ret_consts():
    log_gamma = jnp.log1p(-jnp.exp2(-5.0 - jnp.arange(H_RET, dtype=F32)))
    pos = jnp.arange(CHUNK, dtype=F32)
    d_intra = jnp.exp(log_gamma[:, None, None] * jnp.abs(pos[:, None] - pos[None, :]))
    q_decay = jnp.exp(log_gamma[:, None] * (pos + 1.0))
    k_decay = jnp.exp(log_gamma[:, None] * (CHUNK - 1.0 - pos))
    chunk_decay = jnp.exp(log_gamma * CHUNK)
    wide = (H_RET, CHUNK, DK_RET)
    return (d_intra, jnp.broadcast_to(q_decay[:, :, None], wide), jnp.broadcast_to(k_decay[:, :, None], wide),
            jnp.broadcast_to(chunk_decay[:, None, None], (H_RET, 1, DK_RET)))


def _rope_tables(t):
    inv_freq = ROPE_BASE ** (-jnp.linspace(0.0, 1.0, DK_RET // 2, dtype=F32))
    ang = jnp.arange(t, dtype=F32)[:, None] * inv_freq[None, :]
    cos, sin = jnp.cos(ang), jnp.sin(ang)
    return jnp.concatenate([cos, cos], axis=1), jnp.concatenate([-sin, sin], axis=1)


def _rope(v, cc, ss):
    return v * cc + pltpu.roll(v, DK_RET // 2, axis=1) * ss


def _ret_in_specs(tb, blk):
    def col(s):
        return pl.BlockSpec((None, tb, DK_RET), lambda h, i: (s, blk(i), h))

    tab = pl.BlockSpec((tb, DK_RET), lambda h, i: (blk(i), 0))
    return ([col(3), col(4), col(5), col(6), tab, tab,
             pl.BlockSpec((None, CHUNK, CHUNK), lambda h, i: (h, 0, 0)),
             pl.BlockSpec((None, CHUNK, DK_RET), lambda h, i: (h, 0, 0)),
             pl.BlockSpec((None, CHUNK, DK_RET), lambda h, i: (h, 0, 0)),
             pl.BlockSpec((None, 1, DK_RET), lambda h, i: (h, 0, 0))])


def _ret_fwd(cols3, tables, consts, *, name):
    _, t, bw = cols3.shape
    tb = _tile(t, RET_TB)
    ncb = tb // CHUNK
    scale = DK_RET ** -0.5

    def body(q_ref, k_ref, v_ref, g_ref, cc_ref, ss_ref, di_ref, qd_ref, kd_ref, cd_ref, y_ref, st_ref, state):
        @pl.when(pl.program_id(1) == 0)
        def _():
            state[...] = jnp.zeros_like(state)

        di, qd, kd, cd = di_ref[...], qd_ref[...], kd_ref[...], cd_ref[...]
        for c in range(ncb):
            rows = pl.ds(c * CHUNK, CHUNK)
            cc, ss = cc_ref[rows, :], ss_ref[rows, :]
            qs = _rope(q_ref[rows, :].astype(F32), cc, ss) * scale
            kr = _rope(k_ref[rows, :].astype(F32), cc, ss)
            vb = v_ref[rows, :]
            gv = g_ref[rows, :].astype(F32)
            s_in = state[...]
            st_ref[c] = s_in
            inner = _dot_nt(_bf(qs), _bf(kr)) * di
            o = _dot(_bf(inner), vb) + _dot(_bf(qs * qd), _bf(s_in))
            state[...] = s_in * cd + _dot_tn(_bf(kr * kd), vb)
            on = o * lax.rsqrt(jnp.mean(o * o, axis=-1, keepdims=True) + EPS)
            y_ref[rows, :] = _bf(gv * _sigmoid(gv) * on)

    return pl.pallas_call(
        body, name=name, grid=(H_RET, t // tb),
        in_specs=_ret_in_specs(tb, lambda i: i),
        out_specs=[pl.BlockSpec((tb, DK_RET), lambda h, i: (i, h)),
                   pl.BlockSpec((None, ncb, DK_RET, DK_RET), lambda h, i: (h, i, 0, 0))],
        out_shape=[jax.ShapeDtypeStruct((t, bw), BF16),
                   jax.ShapeDtypeStruct((H_RET, t // CHUNK, DK_RET, DK_RET), F32)],
        scratch_shapes=[pltpu.VMEM((DK_RET, DK_RET), F32)],
        compiler_params=_cparams("parallel", "arbitrary"),
    )(cols3, cols3, cols3, cols3, tables[0], tables[1], *consts)


def _ret_bwd(cols3, tables, consts, states, dy, *, name):
    _, t, bw = cols3.shape
    tb = _tile(t, RET_TB)
    ncb = tb // CHUNK
    nb = t // tb
    scale = DK_RET ** -0.5

    def body(q_ref, k_ref, v_ref, g_ref, cc_ref, ss_ref, di_ref, qd_ref, kd_ref, cd_ref, st_ref, dy_ref,
             d4_ref, dstate):
        @pl.when(pl.program_id(1) == 0)
        def _():
            dstate[...] = jnp.zeros_like(dstate)

        di, qd, kd, cd = di_ref[...], qd_ref[...], kd_ref[...], cd_ref[...]
        for c in reversed(range(ncb)):
            rows = pl.ds(c * CHUNK, CHUNK)
            cc, ss = cc_ref[rows, :], ss_ref[rows, :]
            qs = _rope(q_ref[rows, :].astype(F32), cc, ss) * scale
            kr = _rope(k_ref[rows, :].astype(F32), cc, ss)
            vb = v_ref[rows, :]
            gv = g_ref[rows, :].astype(F32)
            dyv = dy_ref[rows, :].astype(F32)
            s_in = _bf(st_ref[c])
            qsb, krb = _bf(qs), _bf(kr)
            qdb, kdb = _bf(qs * qd), _bf(kr * kd)
            inner = _dot_nt(qsb, krb) * di
            innerb = _bf(inner)
            o = _dot(innerb, vb) + _dot(qdb, s_in)
            r = lax.rsqrt(jnp.mean(o * o, axis=-1, keepdims=True) + EPS)
            on = o * r
            sg = _sigmoid(gv)
            d4_ref[3, rows, :] = _bf(dyv * on * (sg * (1.0 + gv * (1.0 - sg))))
            don = dyv * (gv * sg)
            dob = _bf(r * (don - on * jnp.mean(don * on, axis=-1, keepdims=True)))
            dst = dstate[...]
            dstb = _bf(dst)
            dinner = _bf(_dot_nt(dob, vb) * di)
            dv = _dot_tn(innerb, dob) + _dot(kdb, dstb)
            dqs = _dot(dinner, krb) + _dot_nt(dob, s_in) * qd
            dkr = _dot_tn(dinner, qsb) + _dot_nt(vb, dstb) * kd
            dstate[...] = dst * cd + _dot_tn(qdb, dob)
            d4_ref[0, rows, :] = _bf(_rope_bwd(dqs * scale, cc, ss))
            d4_ref[1, rows, :] = _bf(_rope_bwd(dkr, cc, ss))
            d4_ref[2, rows, :] = _bf(dv)

    rev = lambda i: nb - 1 - i
    return pl.pallas_call(
        body, name=name, grid=(H_RET, nb),
        in_specs=_ret_in_specs(tb, rev)
        + [pl.BlockSpec((None, ncb, DK_RET, DK_RET), lambda h, i: (h, rev(i), 0, 0)),
           pl.BlockSpec((tb, DK_RET), lambda h, i: (rev(i), h))],
        out_specs=pl.BlockSpec((4, tb, DK_RET), lambda h, i: (0, rev(i), h)),
        out_shape=jax.ShapeDtypeStruct((4, t, bw), BF16),
        scratch_shapes=[pltpu.VMEM((DK_RET, DK_RET), F32)],
        compiler_params=_cparams("parallel", "arbitrary"),
    )(cols3, cols3, cols3, cols3, tables[0], tables[1], *consts, states, dy)


def _rope_bwd(dv, cc, ss):
    return dv * cc + pltpu.roll(dv * ss, DK_RET // 2, axis=1)


def _att_window(i):
    return pl.multiple_of(jnp.maximum(i - ATT_LOOKBACK // ATT_QB, 0) * ATT_QB, ATT_QB)


def _att_mask(i, ws):
    qchunk = (i * ATT_QB + lax.broadcasted_iota(jnp.int32, (ATT_QB, ATT_WIN), 0)) // CHUNK
    kchunk = (ws + lax.broadcasted_iota(jnp.int32, (ATT_QB, ATT_WIN), 1)) // CHUNK
    return (kchunk <= qchunk) & (kchunk >= qchunk - N_PREV_CHUNKS)


def _att_fwd(cols3, bias3, *, name):
    _, t, bw = cols3.shape
    assert t % ATT_QB == 0 and t >= ATT_WIN
    scale = DH_ATT ** -0.5
    nvar = ATT_LOOKBACK // ATT_QB

    def body(q_ref, k_ref, v_ref, b_ref, y_ref, lse_ref):
        i = pl.program_id(1)
        ws = _att_window(i)
        valid = _att_mask(i, ws)
        q = q_ref[...].astype(F32)
        kw = k_ref[pl.ds(ws, ATT_WIN), :]
        vw = v_ref[pl.ds(ws, ATT_WIN), :]
        lane_head = lax.broadcasted_iota(jnp.int32, (ATT_QB, 128), 1) // DH_ATT
        out = jnp.zeros((ATT_QB, 128), F32)
        lse = jnp.zeros((ATT_QB, 128), F32)
        for hh in range(2):
            mine = lane_head == hh
            s = _dot_nt(_bf(jnp.where(mine, q, 0.0)), kw) * scale + b_ref[hh]
            s = jnp.where(valid, s, NEG_INF)
            mx = jnp.max(s, axis=-1, keepdims=True)
            p = jnp.exp(s - mx)
            l = jnp.sum(p, axis=-1, keepdims=True)
            out = jnp.where(mine, _dot(_bf(p), vw) / l, out)
            lse = jnp.where(mine, mx + jnp.log(l), lse)
        y_ref[...] = _bf(out)
        lse_ref[...] = lse

    kv = lambda s: pl.BlockSpec((None, t, 128), lambda hp, i: (s, 0, hp))
    return pl.pallas_call(
        body, name=name, grid=(H_ATT // 2, t // ATT_QB),
        in_specs=[pl.BlockSpec((None, ATT_QB, 128), lambda hp, i: (7, i, hp)), kv(8), kv(9),
                  pl.BlockSpec((None, 2, ATT_QB, ATT_WIN), lambda hp, i: (jnp.minimum(i, nvar), hp, 0, 0))],
        out_specs=[pl.BlockSpec((ATT_QB, 128), lambda hp, i: (i, hp)),
                   pl.BlockSpec((None, ATT_QB, 128), lambda hp, i: (hp, i, 0))],
        out_shape=[jax.ShapeDtypeStruct((t, bw), BF16), jax.ShapeDtypeStruct((H_ATT // 2, t, 128), F32)],
        compiler_params=_cparams("parallel", "arbitrary"),
    )(cols3, cols3, cols3, bias3)


def _att_bwd(cols3, bias3, y, lse, dy, *, name):
    _, t, bw = cols3.shape
    nq = t // ATT_QB
    scale = DH_ATT ** -0.5
    nvar = ATT_LOOKBACK // ATT_QB

    def body(q_ref, k_ref, v_ref, b_ref, y_ref, lse_ref, dy_ref, d3_ref, db_ref, dk_acc, dv_acc):
        i = pl.program_id(1)

        @pl.when(i == 0)
        def _():
            dk_acc[...] = jnp.zeros_like(dk_acc)
            dv_acc[...] = jnp.zeros_like(dv_acc)

        ws = _att_window(i)
        valid = _att_mask(i, ws)
        q = q_ref[...].astype(F32)
        kw = k_ref[pl.ds(ws, ATT_WIN), :]
        vw = v_ref[pl.ds(ws, ATT_WIN), :]
        do = dy_ref[...].astype(F32)
        dof = do * y_ref[...].astype(F32)
        lsev = lse_ref[...]
        lane_head = lax.broadcasted_iota(jnp.int32, (ATT_QB, 128), 1) // DH_ATT
        dq = jnp.zeros((ATT_QB, 128), F32)
        dk = jnp.zeros((ATT_WIN, 128), F32)
        dv = jnp.zeros((ATT_WIN, 128), F32)
        first = i <= nvar
        for hh in range(2):
            mine = lane_head == hh
            qh = _bf(jnp.where(mine, q, 0.0))
            doh = _bf(jnp.where(mine, do, 0.0))
            s = _dot_nt(qh, kw) * scale + b_ref[hh]
            lse_h = jnp.max(jnp.where(mine, lsev, NEG_INF), axis=-1, keepdims=True)
            p = jnp.where(valid, jnp.exp(s - lse_h), 0.0)
            delta = jnp.sum(jnp.where(mine, dof, 0.0), axis=-1, keepdims=True)
            ds = p * (_dot_nt(doh, vw) - delta)

            @pl.when(first)
            def _():
                db_ref[hh] = ds

            @pl.when(jnp.logical_not(first))
            def _():
                db_ref[hh] += ds

            dsb = _bf(ds * scale)
            dq = jnp.where(mine, _dot(dsb, kw), dq)
            dk = dk + _dot_tn(dsb, qh)
            dv = dv + _dot_tn(_bf(p), doh)
        d3_ref[0, pl.ds(pl.multiple_of(i * ATT_QB, ATT_QB), ATT_QB), :] = _bf(dq)
        dk_acc[pl.ds(ws, ATT_WIN), :] += dk
        dv_acc[pl.ds(ws, ATT_WIN), :] += dv

        @pl.when(i == nq - 1)
        def _():
            d3_ref[1] = _bf(dk_acc[...])
            d3_ref[2] = _bf(dv_acc[...])

    kv = lambda s: pl.BlockSpec((None, t, 128), lambda hp, i: (s, 0, hp))
    qrow = pl.BlockSpec((ATT_QB, 128), lambda hp, i: (i, hp))
    btile = pl.BlockSpec((None, 2, ATT_QB, ATT_WIN), lambda hp, i: (jnp.minimum(i, nvar), hp, 0, 0))
    return pl.pallas_call(
        body, name=name, grid=(H_ATT // 2, nq),
        in_specs=[pl.BlockSpec((None, ATT_QB, 128), lambda hp, i: (7, i, hp)), kv(8), kv(9), btile, qrow,
                  pl.BlockSpec((None, ATT_QB, 128), lambda hp, i: (hp, i, 0)), qrow],
        out_specs=[pl.BlockSpec((3, t, 128), lambda hp, i: (0, 0, hp)), btile],
        out_shape=[jax.ShapeDtypeStruct((3, t, bw), BF16),
                   jax.ShapeDtypeStruct((nvar + 1, H_ATT, ATT_QB, ATT_WIN), F32)],
        scratch_shapes=[pltpu.VMEM((t, 128), F32), pltpu.VMEM((t, 128), F32)],
        compiler_params=_cparams("parallel", "arbitrary"),
    )(cols3, cols3, cols3, bias3, y, lse, dy)


SKEW_W = ATT_WIN + ATT_QB
REL_PAD = 384


def _rel_onehot(v):
    r = lax.broadcasted_iota(jnp.int32, (REL_PAD, SKEW_W), 0)
    j = lax.broadcasted_iota(jnp.int32, (REL_PAD, SKEW_W), 1)
    dist = jnp.where(j < ATT_WIN, v * ATT_QB - j, v * ATT_QB + SKEW_W - j)
    col = jnp.clip(dist, -REL_CLIP, REL_CLIP) + REL_CLIP
    return _bf(jnp.where(col == r, 1.0, 0.0))


def _split3(v):
    hi = _bf(v)
    rest = v - hi.astype(F32)
    mid = _bf(rest)
    return hi, mid, _bf(rest - mid.astype(F32))


def _skew_rows(a, forward):
    row = lax.broadcasted_iota(jnp.int32, a.shape, 0)
    for b in range(ATT_QB.bit_length() - 1):
        shift = (1 << b) if forward else SKEW_W - (1 << b)
        a = jnp.where(((row >> b) & 1) == 1, pltpu.roll(a, shift, axis=1), a)
    return a


def _bias_tiles(rel_bias, *, name):
    nvar = ATT_LOOKBACK // ATT_QB + 1
    rel = jnp.pad(rel_bias, ((0, 0), (0, REL_PAD - N_REL)))

    def body(rel_ref, o_ref, ext_ref):
        onehot = _rel_onehot(pl.program_id(0))
        ext_ref[...] = sum(_dot(part, onehot) for part in _split3(rel_ref[...]))
        for h in range(H_ATT):
            tile = _skew_rows(jnp.broadcast_to(ext_ref[h:h + 1, :], (ATT_QB, SKEW_W)), True)
            o_ref[h] = tile[:, :ATT_WIN]

    return pl.pallas_call(
        body, name=name, grid=(nvar,), in_specs=[pl.BlockSpec((H_ATT, REL_PAD), lambda v: (0, 0))],
        out_specs=pl.BlockSpec((None, H_ATT, ATT_QB, ATT_WIN), lambda v: (v, 0, 0, 0)),
        out_shape=jax.ShapeDtypeStruct((nvar, H_ATT, ATT_QB, ATT_WIN), F32),
        scratch_shapes=[pltpu.VMEM((H_ATT, SKEW_W), F32)], compiler_params=_cparams("parallel"),
    )(rel)


def _rel_bias_grad(dbias3, *, name):
    nvar = dbias3.shape[0]

    def body(db_ref, o_ref, diag_ref):
        v = pl.program_id(0)
        for h in range(H_ATT):
            tile = jnp.concatenate([db_ref[h], jnp.zeros((ATT_QB, ATT_QB), F32)], axis=1)
            diag_ref[h:h + 1, :] = jnp.sum(_skew_rows(tile, False), axis=0, keepdims=True)
        onehot = _rel_onehot(v)
        part = sum(_dot_nt(p, onehot) for p in _split3(diag_ref[...]))

        @pl.when(v == 0)
        def _():
            o_ref[...] = part

        @pl.when(v > 0)
        def _():
            o_ref[...] += part

    out = pl.pallas_call(
        body, name=name, grid=(nvar,),
        in_specs=[pl.BlockSpec((None, H_ATT, ATT_QB, ATT_WIN), lambda v: (v, 0, 0, 0))],
        out_specs=pl.BlockSpec((H_ATT, REL_PAD), lambda v: (0, 0)),
        out_shape=jax.ShapeDtypeStruct((H_ATT, REL_PAD), F32),
        scratch_shapes=[pltpu.VMEM((H_ATT, SKEW_W), F32)], compiler_params=_cparams("arbitrary"),
    )(dbias3)
    return out[:, :N_REL]


def _merge_fwd(h, ys, wmg, wb_t, *, name, tm=512):
    t, d = h.shape
    bw = ys[0].shape[1]
    tm = _tile(t, tm)

    def body(h_ref, y0_ref, y1_ref, y2_ref, wg_ref, wb_ref, m_ref, s_ref, p_ref):
        hv = h_ref[...]
        total = jnp.zeros((tm, d), F32)
        for b, y_ref in enumerate((y0_ref, y1_ref, y2_ref)):
            s = _sigmoid(_dot(hv, wg_ref[b]))
            p = _dot_nt(y_ref[...], wb_ref[b])
            s_ref[b] = _bf(s)
            p_ref[b] = _bf(p)
            total = total + s * p
        m_ref[...] = _bf(total)

    row = pl.BlockSpec((tm, d), lambda i: (i, 0))
    yrow = pl.BlockSpec((tm, bw), lambda i: (i, 0))
    three = pl.BlockSpec((3, tm, d), lambda i: (0, i, 0))
    return pl.pallas_call(
        body, name=name, grid=(t // tm,),
        in_specs=[row, yrow, yrow, yrow, pl.BlockSpec((3, d, d), lambda i: (0, 0, 0)),
                  pl.BlockSpec((3, d, bw), lambda i: (0, 0, 0))],
        out_specs=[row, three, three],
        out_shape=[jax.ShapeDtypeStruct((t, d), BF16), jax.ShapeDtypeStruct((3, t, d), BF16),
                   jax.ShapeDtypeStruct((3, t, d), BF16)],
        compiler_params=_cparams("parallel"),
    )(h, *ys, wmg, wb_t)


def _merge_bwd(dm, s3, p3, wmg, wb_t, *, name, tm=512):
    _, t, d = s3.shape
    bw = wb_t.shape[2]
    tm = _tile(t, tm)

    def body(dm_ref, s_ref, p_ref, wg_ref, wb_ref, dgp_ref, dp_ref, dy_ref, dh_ref):
        dmv = dm_ref[...].astype(F32)
        dh = jnp.zeros((tm, d), F32)
        for b in range(3):
            s = s_ref[b].astype(F32)
            dgp = _bf(dmv * p_ref[b].astype(F32) * s * (1.0 - s))
            dp = _bf(dmv * s)
            dgp_ref[b] = dgp
            dp_ref[b] = dp
            dy_ref[b] = _bf(_dot(dp, wb_ref[b]))
            dh = dh + _dot_nt(dgp, wg_ref[b])
        dh_ref[...] = dh

    row = pl.BlockSpec((tm, d), lambda i: (i, 0))
    three = pl.BlockSpec((3, tm, d), lambda i: (0, i, 0))
    return pl.pallas_call(
        body, name=name, grid=(t // tm,),
        in_specs=[row, three, three, pl.BlockSpec((3, d, d), lambda i: (0, 0, 0)),
                  pl.BlockSpec((3, d, bw), lambda i: (0, 0, 0))],
        out_specs=[three, three, pl.BlockSpec((3, tm, bw), lambda i: (0, i, 0)), row],
        out_shape=[jax.ShapeDtypeStruct((3, t, d), BF16), jax.ShapeDtypeStruct((3, t, d), BF16),
                   jax.ShapeDtypeStruct((3, t, bw), BF16), jax.ShapeDtypeStruct((t, d), F32)],
        compiler_params=_cparams("parallel"),
    )(dm, s3, p3, wmg, wb_t)


def _layer_fwd(x, p, aux, tag):
    x1, h1, g1, u1, a1 = _ffn_fwd(x, p["n1"], p["wg1"], p["wu1"], p["wd1"], name=f"ffn1_fwd_{tag}")
    h2 = _rmsnorm_fwd(x1, p["nmix"], name=f"mixnorm_fwd_{tag}")
    cols3 = _mm_to_slices(h2, p["win"], name=f"inproj_fwd_{tag}")
    bias3 = _bias_tiles(p["rel_bias"], name=f"bias_tiles_{tag}")
    y_conv = _conv_fwd(cols3, p["conv_w"], name=f"conv_fwd_{tag}")
    y_ret, states = _ret_fwd(cols3, aux["rope"], aux["ret"], name=f"ret_fwd_{tag}")
    y_att, lse = _att_fwd(cols3, bias3, name=f"att_fwd_{tag}")
    merged, s3, p3 = _merge_fwd(h2, (y_conv, y_ret, y_att), p["wmg"], p["wb"], name=f"merge_fwd_{tag}")
    x2 = _mm_rows(merged, p["wout"], dims=NN, out_dtype=F32, res=x1, name=f"outproj_fwd_{tag}")
    x3, h3, g2, u2, a2 = _ffn_fwd(x2, p["n2"], p["wg2"], p["wu2"], p["wd2"], name=f"ffn2_fwd_{tag}")
    saved = dict(x0=x, h1=h1, g1=g1, u1=u1, a1=a1, x1=x1, h2=h2, cols3=cols3, bias3=bias3, y_conv=y_conv,
                 y_ret=y_ret, states=states, y_att=y_att, lse=lse, merged=merged, s3=s3, p3=p3, x2=x2, h3=h3,
                 g2=g2, u2=u2, a2=a2)
    return x3, saved


def _ffn_grads(dxo, x, h, g, u, a, nw, wg, wu, wd, tag):
    dx, dyb, dg, du, dnw = _ffn_bwd(dxo, x, nw, g, u, wg, wu, wd, name=f"{tag}_bwd")
    return dx, dnw, _mm_tn(dg, h, name=f"{tag}_dwg"), _mm_tn(du, h, name=f"{tag}_dwu"), _mm_tn(a, dyb, name=f"{tag}_dwd")


def _layer_bwd(dx3, p, s, aux, tag):
    grads = {}
    dx2, grads["n2"], grads["wg2"], grads["wu2"], grads["wd2"] = _ffn_grads(
        dx3, s["x2"], s["h3"], s["g2"], s["u2"], s["a2"], p["n2"], p["wg2"], p["wu2"], p["wd2"], f"ffn2_{tag}")

    dm = _mm_rows(dx2, p["wout"], dims=NT, out_dtype=BF16, name=f"outproj_dm_{tag}")
    grads["wout"] = _mm_tn(s["merged"], dx2, name=f"outproj_dw_{tag}")
    dgp3, dp3, dy3, dh2 = _merge_bwd(dm, s["s3"], s["p3"], p["wmg"], p["wb"], name=f"merge_bwd_{tag}")
    grads["wmg"] = _mm_tn_batch(s["h2"], dgp3, a_batched=False, name=f"merge_dwg_{tag}")
    y3 = jnp.stack([s["y_conv"], s["y_ret"], s["y_att"]])
    grads["wb"] = _mm_tn_batch(dp3, y3, a_batched=True, name=f"merge_dwb_{tag}")

    dconv3, grads["conv_w"] = _conv_bwd(s["cols3"], p["conv_w"], dy3[0], name=f"conv_bwd_{tag}")
    dret4 = _ret_bwd(s["cols3"], aux["rope"], aux["ret"], s["states"], dy3[1], name=f"ret_bwd_{tag}")
    datt3, dbias3 = _att_bwd(s["cols3"], s["bias3"], s["y_att"], s["lse"], dy3[2], name=f"att_bwd_{tag}")
    grads["rel_bias"] = _rel_bias_grad(dbias3, name=f"bias_grad_{tag}")

    w = BRANCH_W
    dh2 = _mm_from_slices(dconv3, p["win"], 0, dh2, name=f"inproj_dh_conv_{tag}")
    dh2 = _mm_from_slices(dret4, p["win"], 3 * w, dh2, name=f"inproj_dh_ret_{tag}")
    dh2 = _mm_from_slices(datt3, p["win"], 7 * w, dh2, name=f"inproj_dh_att_{tag}")
    grads["win"] = jnp.concatenate([
        _mm_tn_slices(dconv3, s["h2"], name=f"inproj_dw_conv_{tag}"),
        _mm_tn_slices(dret4, s["h2"], name=f"inproj_dw_ret_{tag}"),
        _mm_tn_slices(datt3, s["h2"], name=f"inproj_dw_att_{tag}")], axis=0)
    dx1, grads["nmix"] = _rmsnorm_bwd(dh2, s["x1"], p["nmix"], dx2, name=f"mixnorm_bwd_{tag}")

    dx0, grads["n1"], grads["wg1"], grads["wu1"], grads["wd1"] = _ffn_grads(
        dx1, s["x0"], s["h1"], s["g1"], s["u1"], s["a1"], p["n1"], p["wg1"], p["wu1"], p["wd1"], f"ffn1_{tag}")
    return dx0, grads


def _device_step(x, target, layers, final_norm):
    t = x.shape[0]
    aux = dict(rope=_rope_tables(t), ret=_ret_consts())
    saved = []
    for l, p in enumerate(layers):
        x, s = _layer_fwd(x, p, aux, f"l{l}")
        saved.append(s)
    dx, dfinal, loss_row = _loss_fwd_bwd(x, final_norm, target, name="loss_fwd_bwd")
    grads = [None] * len(layers)
    for l in reversed(range(len(layers))):
        dx, grads[l] = _layer_bwd(dx, layers[l], saved[l], aux, f"l{l}")
    return loss_row, dx, grads, dfinal


def _my_position():
    x, y, c = lax.axis_index("x"), lax.axis_index("y"), lax.axis_index("c")
    return x, y, c, 4 * x + 2 * y + c


def _peer(x, y, c, k):
    px = 1 - x if k & 4 else x
    py = 1 - y if k & 2 else y
    pc = 1 - c if k & 1 else c
    return (px, py, pc), 4 * px + 2 * py + pc


def _all_gather(shard, *, name):
    def body(x_ref, o_ref, send_sems, recv_sems, local_sem):
        x, y, c, me = _my_position()
        mine = pltpu.make_async_copy(x_ref, o_ref.at[me], local_sem)
        mine.start()
        sends = []
        for k in range(1, N_DEV):
            peer, _ = _peer(x, y, c, k)
            cp = pltpu.make_async_remote_copy(src_ref=x_ref, dst_ref=o_ref.at[me], send_sem=send_sems.at[k - 1],
                                              recv_sem=recv_sems.at[k - 1], device_id=peer, device_id_type=MESH)
            cp.start()
            sends.append(cp)
        for k in range(1, N_DEV):
            peer, peer_id = _peer(x, y, c, k)
            pltpu.make_async_remote_copy(src_ref=x_ref, dst_ref=o_ref.at[peer_id], send_sem=send_sems.at[k - 1],
                                         recv_sem=recv_sems.at[k - 1], device_id=peer, device_id_type=MESH).wait_recv()
        for cp in sends:
            cp.wait_send()
        mine.wait()

    return pl.pallas_call(
        body, name=name, in_specs=[pl.BlockSpec(memory_space=pl.ANY)], out_specs=pl.BlockSpec(memory_space=pl.ANY),
        out_shape=jax.ShapeDtypeStruct((N_DEV,) + shard.shape, shard.dtype),
        scratch_shapes=[pltpu.SemaphoreType.DMA((N_DEV - 1,)), pltpu.SemaphoreType.DMA((N_DEV - 1,)),
                        pltpu.SemaphoreType.DMA],
    )(shard)


def _scatter_partials(full, *, name):
    def body(g_ref, o_ref, send_sems, recv_sems, local_sem):
        x, y, c, me = _my_position()
        mine = pltpu.make_async_copy(g_ref.at[me], o_ref.at[me], local_sem)
        mine.start()
        sends = []
        for k in range(1, N_DEV):
            peer, peer_id = _peer(x, y, c, k)
            cp = pltpu.make_async_remote_copy(src_ref=g_ref.at[peer_id], dst_ref=o_ref.at[me],
                                              send_sem=send_sems.at[k - 1], recv_sem=recv_sems.at[k - 1],
                                              device_id=peer, device_id_type=MESH)
            cp.start()
            sends.append(cp)
        for k in range(1, N_DEV):
            peer, peer_id = _peer(x, y, c, k)
            pltpu.make_async_remote_copy(src_ref=g_ref.at[me], dst_ref=o_ref.at[peer_id],
                                         send_sem=send_sems.at[k - 1], recv_sem=recv_sems.at[k - 1],
                                         device_id=peer, device_id_type=MESH).wait_recv()
        for cp in sends:
            cp.wait_send()
        mine.wait()

    return pl.pallas_call(
        body, name=name, in_specs=[pl.BlockSpec(memory_space=pl.ANY)], out_specs=pl.BlockSpec(memory_space=pl.ANY),
        out_shape=jax.ShapeDtypeStruct(full.shape, full.dtype),
        scratch_shapes=[pltpu.SemaphoreType.DMA((N_DEV - 1,)), pltpu.SemaphoreType.DMA((N_DEV - 1,)),
                        pltpu.SemaphoreType.DMA],
    )(full)


def _sum_slots(parts, *, name, tr=512):
    n, r, cdim = parts.shape
    tr = _tile(r, tr)

    def body(p_ref, o_ref):
        acc = p_ref[0].astype(F32)
        for s in range(1, n):
            acc = acc + p_ref[s].astype(F32)
        o_ref[...] = acc

    return pl.pallas_call(
        body, name=name, grid=(r // tr,), in_specs=[pl.BlockSpec((n, tr, cdim), lambda i: (0, i, 0))],
        out_specs=pl.BlockSpec((tr, cdim), lambda i: (i, 0)), out_shape=jax.ShapeDtypeStruct((r, cdim), F32),
        compiler_params=_cparams("parallel"),
    )(parts)


def _all_reduce_small(v, *, name):
    r = v.shape[0]

    def body(x_ref, o_ref, slots, send_sems, recv_sems):
        x, y, c, me = _my_position()
        slots[me] = x_ref[...]
        sends = []
        for k in range(1, N_DEV):
            peer, _ = _peer(x, y, c, k)
            cp = pltpu.make_async_remote_copy(src_ref=x_ref, dst_ref=slots.at[me], send_sem=send_sems.at[k - 1],
                                              recv_sem=recv_sems.at[k - 1], device_id=peer, device_id_type=MESH)
            cp.start()
            sends.append(cp)
        for k in range(1, N_DEV):
            peer, peer_id = _peer(x, y, c, k)
            pltpu.make_async_remote_copy(src_ref=x_ref, dst_ref=slots.at[peer_id], send_sem=send_sems.at[k - 1],
                                         recv_sem=recv_sems.at[k - 1], device_id=peer, device_id_type=MESH).wait_recv()
        for cp in sends:
            cp.wait_send()
        acc = slots[0]
        for s in range(1, N_DEV):
            acc = acc + slots[s]
        o_ref[...] = acc

    return pl.pallas_call(
        body, name=name, in_specs=[pl.BlockSpec(memory_space=pltpu.VMEM)],
        out_specs=pl.BlockSpec(memory_space=pltpu.VMEM), out_shape=jax.ShapeDtypeStruct((r, 128), F32),
        scratch_shapes=[pltpu.VMEM((N_DEV, r, 128), F32), pltpu.SemaphoreType.DMA((N_DEV - 1,)),
                        pltpu.SemaphoreType.DMA((N_DEV - 1,))],
    )(v)


def _adamw(w, g, m, v, *, name, tr=256):
    shape = w.shape
    cdim = shape[-1]
    w2, g2, m2, v2 = (a.reshape(-1, cdim) for a in (w, g, m, v))
    r = w2.shape[0]
    tr = _tile(r, tr) if r % 8 == 0 else r
    c1 = 1.0 - ADAM_B1 ** ADAM_STEP
    c2 = 1.0 - ADAM_B2 ** ADAM_STEP

    def body(w_ref, g_ref, m_ref, v_ref, d_ref, mo_ref, vo_ref):
        gv = g_ref[...]
        mn = ADAM_B1 * m_ref[...] + (1.0 - ADAM_B1) * gv
        vn = ADAM_B2 * v_ref[...] + (1.0 - ADAM_B2) * (gv * gv)
        d_ref[...] = -ADAM_LR * ((mn / c1) / (jnp.sqrt(vn / c2) + ADAM_EPS) + ADAM_WD * w_ref[...])
        mo_ref[...] = mn
        vo_ref[...] = vn

    spec = pl.BlockSpec((tr, cdim), lambda i: (i, 0))
    outs = pl.pallas_call(
        body, name=name, grid=(r // tr,), in_specs=[spec] * 4, out_specs=[spec] * 3,
        out_shape=[jax.ShapeDtypeStruct((r, cdim), F32)] * 3, compiler_params=_cparams("parallel"),
    )(w2, g2, m2, v2)
    return tuple(o.reshape(shape) for o in outs)


BIG = ("wg1", "wu1", "wd1", "win", "wb", "wmg", "wout", "wg2", "wu2", "wd2")


def _to_rows(name, w, d):
    if name in ("wg1", "wu1", "wg2", "wu2", "win"):
        return w.T
    if name == "wb":
        return w.transpose(0, 2, 1).reshape(-1, d)
    if name == "wmg":
        return w.reshape(-1, d)
    return w


def _from_rows(name, rows, d):
    if name in ("wg1", "wu1", "wg2", "wu2", "win"):
        return rows.T
    if name == "wb":
        return rows.reshape(3, -1, BRANCH_W).transpose(0, 2, 1)
    if name == "wmg":
        return rows.reshape(3, -1, d)
    return rows


def _full_from_gathered(name, g, d):
    if name == "wb":
        return g.reshape(N_DEV, 3, -1, BRANCH_W).transpose(1, 0, 2, 3).reshape(3, d, BRANCH_W)
    if name == "wmg":
        return g.reshape(N_DEV, 3, -1, d).transpose(1, 0, 2, 3).reshape(3, d, d)
    return g.reshape(-1, d)


def _gathered_from_full(name, full, d):
    if name == "wb":
        return full.reshape(3, N_DEV, -1, BRANCH_W).transpose(1, 0, 2, 3).reshape(N_DEV, -1, d)
    if name == "wmg":
        return full.reshape(3, N_DEV, -1, d).transpose(1, 0, 2, 3).reshape(N_DEV, -1, d)
    return full.reshape(N_DEV, -1, d)


def kernel(x, ffn1_norm, ffn1_w_gate, ffn1_w_up, ffn1_w_down, mix_norm, w_in, conv_w, rel_bias, w_branch, w_merge_gate, w_out, ffn2_norm, ffn2_w_gate, ffn2_w_up, ffn2_w_down, final_norm, loss_target, m_ffn1_norm, m_ffn1_w_gate, m_ffn1_w_up, m_ffn1_w_down, m_mix_norm, m_w_in, m_conv_w, m_rel_bias, m_w_branch, m_w_merge_gate, m_w_out, m_ffn2_norm, m_ffn2_w_gate, m_ffn2_w_up, m_ffn2_w_down, m_final_norm, v_ffn1_norm, v_ffn1_w_gate, v_ffn1_w_up, v_ffn1_w_down, v_mix_norm, v_w_in, v_conv_w, v_rel_bias, v_w_branch, v_w_merge_gate, v_w_out, v_ffn2_norm, v_ffn2_w_gate, v_ffn2_w_up, v_ffn2_w_down, v_final_norm):
    names = ["ffn1_norm", "ffn1_w_gate", "ffn1_w_up", "ffn1_w_down", "mix_norm", "w_in", "conv_w", "rel_bias",
             "w_branch", "w_merge_gate", "w_out", "ffn2_norm", "ffn2_w_gate", "ffn2_w_up", "ffn2_w_down", "final_norm"]
    weights = dict(zip(names, (ffn1_norm, ffn1_w_gate, ffn1_w_up, ffn1_w_down, mix_norm, w_in, conv_w, rel_bias,
                               w_branch, w_merge_gate, w_out, ffn2_norm, ffn2_w_gate, ffn2_w_up, ffn2_w_down,
                               final_norm)))
    m_in = dict(zip(names, (m_ffn1_norm, m_ffn1_w_gate, m_ffn1_w_up, m_ffn1_w_down, m_mix_norm, m_w_in, m_conv_w,
                            m_rel_bias, m_w_branch, m_w_merge_gate, m_w_out, m_ffn2_norm, m_ffn2_w_gate,
                            m_ffn2_w_up, m_ffn2_w_down, m_final_norm)))
    v_in = dict(zip(names, (v_ffn1_norm, v_ffn1_w_gate, v_ffn1_w_up, v_ffn1_w_down, v_mix_norm, v_w_in, v_conv_w,
                            v_rel_bias, v_w_branch, v_w_merge_gate, v_w_out, v_ffn2_norm, v_ffn2_w_gate,
                            v_ffn2_w_up, v_ffn2_w_down, v_final_norm)))
    big_of = dict(wg1="ffn1_w_gate", wu1="ffn1_w_up", wd1="ffn1_w_down", win="w_in", wb="w_branch",
                  wmg="w_merge_gate", wout="w_out", wg2="ffn2_w_gate", wu2="ffn2_w_up", wd2="ffn2_w_down")
    depth = ffn1_norm.shape[0]
    d = x.shape[-1]
    xs = x.reshape(-1, d)
    target = loss_target.reshape(-1, d)
    _, _, _, me = _my_position()

    blocks, layout, row = [], [], 0
    for l in range(depth):
        for nm in BIG:
            rows = _to_rows(nm, weights[big_of[nm]][l], d)
            blocks.append(rows.astype(BF16))
            layout.append((l, nm, row, rows.shape[0]))
            row += rows.shape[0]
    gathered = _all_gather(jnp.concatenate(blocks, axis=0), name="gather_weights")
    conv_cols = conv_w.shape[-1]
    conv_full = _all_gather(conv_w.reshape(depth * 3, conv_cols), name="gather_conv_w")
    conv_full = conv_full.reshape(N_DEV, depth, 3, conv_cols).transpose(1, 2, 0, 3).reshape(depth, 3, -1)
    layers = [dict(n1=ffn1_norm[l][None], nmix=mix_norm[l][None], n2=ffn2_norm[l][None], conv_w=conv_full[l],
                   rel_bias=rel_bias[l]) for l in range(depth)]
    for l, nm, r0, nr in layout:
        layers[l][nm] = _full_from_gathered(nm, gathered[:, r0:r0 + nr], d)

    loss_row, dx, grads, dfinal = _device_step(xs, target, layers, final_norm[None])

    full = jnp.concatenate([_gathered_from_full(nm, grads[l][nm], d).astype(BF16) for l, nm, _, _ in layout], axis=1)
    shard_grad = _sum_slots(_scatter_partials(full, name="scatter_grads"), name="sum_grads")
    big_grads = {nm: [None] * depth for nm in BIG}
    for l, nm, r0, nr in layout:
        big_grads[nm][l] = _from_rows(nm, shard_grad[r0:r0 + nr], d)

    small = {"ffn1_norm": jnp.stack([grads[l]["n1"][0] for l in range(depth)]),
             "mix_norm": jnp.stack([grads[l]["nmix"][0] for l in range(depth)]),
             "ffn2_norm": jnp.stack([grads[l]["n2"][0] for l in range(depth)]),
             "final_norm": dfinal[0],
             "rel_bias": jnp.stack([grads[l]["rel_bias"] for l in range(depth)]),
             "conv_w": jnp.stack([grads[l]["conv_w"] for l in range(depth)]),
             "loss": loss_row[0, :1]}
    order = list(small)
    flat = jnp.concatenate([small[k].reshape(-1) for k in order])
    pad = (-flat.shape[0]) % 1024
    summed = _all_reduce_small(jnp.pad(flat, (0, pad)).reshape(-1, 128), name="reduce_small").reshape(-1)
    pos = 0
    for k in order:
        n = small[k].size
        small[k] = summed[pos:pos + n].reshape(small[k].shape)
        pos += n
    small["conv_w"] = lax.dynamic_slice_in_dim(small["conv_w"], me * conv_cols, conv_cols, axis=2)

    grad_w = {big_of[nm]: jnp.stack(big_grads[nm]) for nm in BIG}
    grad_w.update({k: small[k] for k in order if k != "loss"})
    delta, new_m, new_v = {}, {}, {}
    for nm in names:
        delta[nm], new_m[nm], new_v[nm] = _adamw(weights[nm], grad_w[nm], m_in[nm], v_in[nm], name=f"adamw_{nm}")
    return (small["loss"].reshape(()), dx.reshape(x.shape), *[grad_w[n] for n in names], *[delta[n] for n in names],
            *[new_m[n] for n in names], *[new_v[n] for n in names])
```

```python
import functools
import math

import jax
import jax.numpy as jnp
from jax import lax
from jax.experimental import pallas as pl
from jax.experimental.pallas import tpu as pltpu

F32 = jnp.float32
BF16 = jnp.bfloat16

N_DEV = 8
EPS = 1e-6
CHUNK = 64
BRANCH_W = 512
N_SLICES = 10
H_RET = 4
DK_RET = 128
H_ATT = 8
DH_ATT = 64
N_PREV_CHUNKS = 8
REL_CLIP = 128
N_REL = 2 * REL_CLIP + 1
NEG_INF = -1e30
ROPE_BASE = 10000.0
ATT_QB = 256
ATT_LOOKBACK = N_PREV_CHUNKS * CHUNK
ATT_WIN = ATT_LOOKBACK + ATT_QB
RET_TB = 512
CONV_HALO = 16

ADAM_LR = 0.001
ADAM_B1 = 0.9
ADAM_B2 = 0.999
ADAM_EPS = 1e-08
ADAM_WD = 0.01
ADAM_STEP = 10

VMEM_LIMIT_BYTES = 56 * 1024 * 1024
MESH = pl.DeviceIdType.MESH


def _cparams(*sem):
    return pltpu.CompilerParams(dimension_semantics=sem, vmem_limit_bytes=VMEM_LIMIT_BYTES)


def _dot(a, b):
    return lax.dot_general(a, b, (((1,), (0,)), ((), ())), preferred_element_type=F32)


def _dot_nt(a, b):
    return lax.dot_general(a, b, (((1,), (1,)), ((), ())), preferred_element_type=F32)


def _dot_tn(a, b):
    return lax.dot_general(a, b, (((0,), (0,)), ((), ())), preferred_element_type=F32)


def _bf(v):
    return v.astype(BF16)


def _sigmoid(v):
    return 1.0 / (1.0 + jnp.exp(-v))


def _tile(n, want):
    if n <= want:
        return n
    for t in range(want - want % 128, 0, -128):
        if n % t == 0:
            return t
    t = want
    while n % t:
        t //= 2
    return t


def _my_position():
    x, y, c = lax.axis_index("x"), lax.axis_index("y"), lax.axis_index("c")
    return x, y, c, 4 * x + 2 * y + c


def _peer(x, y, c, k):
    px = 1 - x if k & 4 else x
    py = 1 - y if k & 2 else y
    pc = 1 - c if k & 1 else c
    return (px, py, pc), 4 * px + 2 * py + pc


def _exchange_sems():
    return [pltpu.SemaphoreType.DMA((N_DEV - 1,)), pltpu.SemaphoreType.DMA((N_DEV - 1,)), pltpu.SemaphoreType.DMA]


def _exchange_copy(kind, src_ref, dst_ref, sems, k, pos, incoming):
    x, y, c, me = pos
    peer, peer_id = _peer(x, y, c, k)
    if kind == "gather":
        src = src_ref
    else:
        src = src_ref.at[me if incoming else peer_id]
    return pltpu.make_async_remote_copy(src_ref=src, dst_ref=dst_ref.at[peer_id if incoming else me],
                                        send_sem=sems[0].at[k - 1], recv_sem=sems[1].at[k - 1],
                                        device_id=peer, device_id_type=MESH)


def _exchange_local(kind, src_ref, dst_ref, sems, me):
    return pltpu.make_async_copy(src_ref if kind == "gather" else src_ref.at[me], dst_ref.at[me], sems[2])


def _exchange_start(kind, src_ref, dst_ref, sems):
    pos = _my_position()
    _exchange_local(kind, src_ref, dst_ref, sems, pos[3]).start()
    for k in range(1, N_DEV):
        _exchange_copy(kind, src_ref, dst_ref, sems, k, pos, False).start()


def _exchange_wait(kind, src_ref, dst_ref, sems):
    pos = _my_position()
    for k in range(1, N_DEV):
        _exchange_copy(kind, src_ref, dst_ref, sems, k, pos, True).wait_recv()
    for k in range(1, N_DEV):
        _exchange_copy(kind, src_ref, dst_ref, sems, k, pos, False).wait_send()
    _exchange_local(kind, src_ref, dst_ref, sems, pos[3]).wait()


def _exchange_shape(kind, src):
    return jax.ShapeDtypeStruct((N_DEV,) + src.shape if kind == "gather" else src.shape, src.dtype)


def _exchange(kind, src, *, name):
    def body(src_ref, dst_ref, *sems):
        _exchange_start(kind, src_ref, dst_ref, sems)
        _exchange_wait(kind, src_ref, dst_ref, sems)

    return pl.pallas_call(
        body, name=name, in_specs=[pl.BlockSpec(memory_space=pl.ANY)], out_specs=pl.BlockSpec(memory_space=pl.ANY),
        out_shape=_exchange_shape(kind, src), scratch_shapes=_exchange_sems(),
    )(src)


def _pcall(body, args, *, name, grid, in_specs, out_specs, out_shape, scratch_shapes, sem, exchange=None):
    if exchange is None:
        outs = pl.pallas_call(body, name=name, grid=grid, in_specs=in_specs, out_specs=out_specs, out_shape=out_shape,
                              scratch_shapes=scratch_shapes, compiler_params=_cparams(*sem))(*args)
        return outs, None
    kind, src = exchange
    n_in, n_out, n_scr = len(in_specs), len(out_specs), len(scratch_shapes)
    hbm = pl.BlockSpec(memory_space=pl.ANY)

    def carrier(*refs):
        ins, src_ref = refs[:n_in], refs[n_in]
        outs, dst_ref = refs[n_in + 1:n_in + 1 + n_out], refs[n_in + 1 + n_out]
        scr = refs[n_in + 2 + n_out:n_in + 2 + n_out + n_scr]
        sems = refs[n_in + 2 + n_out + n_scr:]
        ids = [pl.program_id(a) for a in range(len(grid))]
        first = functools.reduce(jnp.logical_and, [i == 0 for i in ids])
        last = functools.reduce(jnp.logical_and, [i == g - 1 for i, g in zip(ids, grid)])

        @pl.when(first)
        def _():
            _exchange_start(kind, src_ref, dst_ref, sems)

        body(*ins, *outs, *scr)

        @pl.when(last)
        def _():
            _exchange_wait(kind, src_ref, dst_ref, sems)

    outs = pl.pallas_call(
        carrier, name=name, grid=grid, in_specs=list(in_specs) + [hbm], out_specs=list(out_specs) + [hbm],
        out_shape=list(out_shape) + [_exchange_shape(kind, src)],
        scratch_shapes=list(scratch_shapes) + _exchange_sems(),
        compiler_params=_cparams(*(["arbitrary"] * len(grid))))(*args, src)
    return outs[:-1], outs[-1]


def _matmul(a, b, *, dims, grid, a_spec, b_spec, o_spec, out_shape, acc_shape, name, scale=1.0,
            res=None, res_spec=None):
    nk = grid[3]
    has_res = res is not None

    def body(*refs):
        if has_res:
            a_ref, b_ref, r_ref, o_ref = refs[:4]
        else:
            a_ref, b_ref, o_ref = refs[:3]
            r_ref = None

        def finish(acc):
            if scale != 1.0:
                acc = acc * scale
            if r_ref is not None:
                acc = acc + r_ref[...].astype(F32)
            o_ref[...] = acc.astype(o_ref.dtype)

        part = lax.dot_general(_bf(a_ref[...]), _bf(b_ref[...]), (dims, ((), ())), preferred_element_type=F32)
        if nk == 1:
            finish(part)
        else:
            acc_ref = refs[-1]
            k = pl.program_id(3)

            @pl.when(k == 0)
            def _():
                acc_ref[...] = part

            @pl.when(k > 0)
            def _():
                acc_ref[...] += part

            @pl.when(k == nk - 1)
            def _():
                finish(acc_ref[...])

    in_specs = [a_spec, b_spec] + ([res_spec] if has_res else [])
    args = (a, b) + ((res,) if has_res else ())
    return pl.pallas_call(
        body, name=name, grid=grid, in_specs=in_specs, out_specs=o_spec, out_shape=out_shape,
        scratch_shapes=[] if nk == 1 else [pltpu.VMEM(acc_shape, F32)],
        compiler_params=_cparams("parallel", "parallel", "parallel", "arbitrary"),
    )(*args)


NT = ((1,), (1,))
NN = ((1,), (0,))
TN = ((0,), (0,))


def _mm_rows(a, b, *, dims, out_dtype, name, res=None, scale=1.0, tm=1024, tn=1024, tk=1024):
    m, kdim = a.shape
    n = b.shape[1] if dims == NN else b.shape[0]
    tm, tn, tk = _tile(m, tm), _tile(n, tn), _tile(kdim, tk)
    grid = (1, m // tm, n // tn, kdim // tk)
    a_spec = pl.BlockSpec((tm, tk), lambda s, i, j, k: (i, k))
    if dims == NN:
        b_spec = pl.BlockSpec((tk, tn), lambda s, i, j, k: (k, j))
    else:
        b_spec = pl.BlockSpec((tn, tk), lambda s, i, j, k: (j, k))
    o_spec = pl.BlockSpec((tm, tn), lambda s, i, j, k: (i, j))
    return _matmul(a, b, dims=dims, grid=grid, a_spec=a_spec, b_spec=b_spec, o_spec=o_spec,
                   out_shape=jax.ShapeDtypeStruct((m, n), out_dtype), acc_shape=(tm, tn), name=name,
                   res=res, res_spec=o_spec if res is not None else None, scale=scale)


def _mm_to_slices(a, b_t, *, name, tm=1024):
    m, kdim = a.shape
    n = b_t.shape[0]
    tm = _tile(m, tm)
    w = BRANCH_W
    grid = (1, m // tm, n // w, 1)
    return _matmul(a, b_t, dims=NT, grid=grid,
                   a_spec=pl.BlockSpec((tm, kdim), lambda s, i, j, k: (i, 0)),
                   b_spec=pl.BlockSpec((w, kdim), lambda s, i, j, k: (j, 0)),
                   o_spec=pl.BlockSpec((None, tm, w), lambda s, i, j, k: (j, i, 0)),
                   out_shape=jax.ShapeDtypeStruct((n // w, m, w), BF16), acc_shape=(tm, w), name=name)


def _mm_from_slices(a3, b, row0, res, *, name, tm=1024):
    s_n, m, w = a3.shape
    n = b.shape[1]
    tm = _tile(m, tm)
    off = row0 // w
    grid = (1, m // tm, 1, s_n)
    o_spec = pl.BlockSpec((tm, n), lambda s, i, j, k: (i, 0))
    return _matmul(a3, b, dims=NN, grid=grid,
                   a_spec=pl.BlockSpec((None, tm, w), lambda s, i, j, k: (k, i, 0)),
                   b_spec=pl.BlockSpec((w, n), lambda s, i, j, k: (k + off, 0)),
                   o_spec=o_spec, out_shape=jax.ShapeDtypeStruct((m, n), F32), acc_shape=(tm, n), name=name,
                   res=res, res_spec=o_spec)


def _mm_tn(a, b, *, name, tm=1408, tn=1024, tk=512):
    t, m = a.shape
    n = b.shape[1]
    tm, tn, tk = _tile(m, tm), _tile(n, tn), _tile(t, tk)
    grid = (1, m // tm, n // tn, t // tk)
    return _matmul(a, b, dims=TN, grid=grid,
                   a_spec=pl.BlockSpec((tk, tm), lambda s, i, j, k: (k, i)),
                   b_spec=pl.BlockSpec((tk, tn), lambda s, i, j, k: (k, j)),
                   o_spec=pl.BlockSpec((tm, tn), lambda s, i, j, k: (i, j)),
                   out_shape=jax.ShapeDtypeStruct((m, n), F32), acc_shape=(tm, tn), name=name)


def _mm_tn_slices(a3, b, *, name, tn=1024, tk=512):
    s_n, t, w = a3.shape
    n = b.shape[1]
    tn, tk = _tile(n, tn), _tile(t, tk)
    grid = (1, s_n, n // tn, t // tk)
    return _matmul(a3, b, dims=TN, grid=grid,
                   a_spec=pl.BlockSpec((None, tk, w), lambda s, i, j, k: (i, k, 0)),
                   b_spec=pl.BlockSpec((tk, tn), lambda s, i, j, k: (k, j)),
                   o_spec=pl.BlockSpec((w, tn), lambda s, i, j, k: (i, j)),
                   out_shape=jax.ShapeDtypeStruct((s_n * w, n), F32), acc_shape=(w, tn), name=name)


def _mm_tn_batch(a, b3, *, name, a_batched, tm=1024, tn=1024, tk=512):
    s_n, t, n = b3.shape
    m = a.shape[-1]
    tm, tn, tk = _tile(m, tm), _tile(n, tn), _tile(t, tk)
    grid = (s_n, m // tm, n // tn, t // tk)
    if a_batched:
        a_spec = pl.BlockSpec((None, tk, tm), lambda s, i, j, k: (s, k, i))
    else:
        a_spec = pl.BlockSpec((tk, tm), lambda s, i, j, k: (k, i))
    return _matmul(a, b3, dims=TN, grid=grid, a_spec=a_spec,
                   b_spec=pl.BlockSpec((None, tk, tn), lambda s, i, j, k: (s, k, j)),
                   o_spec=pl.BlockSpec((None, tm, tn), lambda s, i, j, k: (s, i, j)),
                   out_shape=jax.ShapeDtypeStruct((s_n, m, n), F32), acc_shape=(tm, tn), name=name)


def _norm_parts(xf):
    r = lax.rsqrt(jnp.mean(xf * xf, axis=-1, keepdims=True) + EPS)
    return xf * r, r


def _norm_bwd(dh, xhat, r, w):
    dxhat = dh * w
    dx = r * (dxhat - xhat * jnp.mean(dxhat * xhat, axis=-1, keepdims=True))
    return dx, jnp.sum(dh * xhat, axis=0, keepdims=True)


def _rmsnorm_fwd(x, w, *, name, tm=1024):
    t, d = x.shape
    tm = _tile(t, tm)

    def body(x_ref, w_ref, h_ref):
        xhat, _ = _norm_parts(x_ref[...])
        h_ref[...] = _bf(xhat * w_ref[...])

    return pl.pallas_call(
        body, name=name, grid=(t // tm,),
        in_specs=[pl.BlockSpec((tm, d), lambda i: (i, 0)), pl.BlockSpec((1, d), lambda i: (0, 0))],
        out_specs=pl.BlockSpec((tm, d), lambda i: (i, 0)),
        out_shape=jax.ShapeDtypeStruct((t, d), BF16), compiler_params=_cparams("parallel"),
    )(x, w)


def _rmsnorm_bwd(dh, x, w, dres, *, name, tm=1024):
    t, d = x.shape
    tm = _tile(t, tm)

    def body(dh_ref, x_ref, w_ref, dres_ref, dx_ref, dw_ref):
        xhat, r = _norm_parts(x_ref[...])
        dx, dw = _norm_bwd(dh_ref[...], xhat, r, w_ref[...])
        dx_ref[...] = dres_ref[...] + dx

        @pl.when(pl.program_id(0) == 0)
        def _():
            dw_ref[...] = dw

        @pl.when(pl.program_id(0) > 0)
        def _():
            dw_ref[...] += dw

    row = pl.BlockSpec((tm, d), lambda i: (i, 0))
    vec = pl.BlockSpec((1, d), lambda i: (0, 0))
    return pl.pallas_call(
        body, name=name, grid=(t // tm,), in_specs=[row, row, vec, row], out_specs=[row, vec],
        out_shape=[jax.ShapeDtypeStruct((t, d), F32), jax.ShapeDtypeStruct((1, d), F32)],
        compiler_params=_cparams("arbitrary"),
    )(dh, x, w, dres)


def _loss_fwd_bwd(x, w, target, *, name, tm=1024):
    t, d = x.shape
    tm = _tile(t, tm)

    def body(x_ref, w_ref, t_ref, dx_ref, dw_ref, loss_ref):
        xhat, r = _norm_parts(x_ref[...])
        wv = w_ref[...]
        err = xhat * wv - t_ref[...]
        dx, dw = _norm_bwd(err * (1.0 / d), xhat, r, wv)
        dx_ref[...] = dx
        part = jnp.full((1, 128), 0.5 / d, F32) * jnp.sum(err * err)

        @pl.when(pl.program_id(0) == 0)
        def _():
            dw_ref[...] = dw
            loss_ref[...] = part

        @pl.when(pl.program_id(0) > 0)
        def _():
            dw_ref[...] += dw
            loss_ref[...] += part

    row = pl.BlockSpec((tm, d), lambda i: (i, 0))
    vec = pl.BlockSpec((1, d), lambda i: (0, 0))
    return pl.pallas_call(
        body, name=name, grid=(t // tm,), in_specs=[row, vec, row],
        out_specs=[row, vec, pl.BlockSpec((1, 128), lambda i: (0, 0))],
        out_shape=[jax.ShapeDtypeStruct((t, d), F32), jax.ShapeDtypeStruct((1, d), F32),
                   jax.ShapeDtypeStruct((1, 128), F32)],
        compiler_params=_cparams("arbitrary"),
    )(x, w, target)


def _ffn_tiles(t, f):
    tf = f
    for cand in (1408, 1024, 512, 256, 128):
        if f % cand == 0:
            tf = cand
            break
    return _tile(t, 512), tf


def _ffn_fwd(x, nw, wg_t, wu_t, wd, *, name, exchange=None):
    t, d = x.shape
    f = wd.shape[0]
    tm, tf = _ffn_tiles(t, f)
    nf = f // tf

    def body(x_ref, nw_ref, wg_ref, wu_ref, wd_ref, xo_ref, h_ref, g_ref, u_ref, a_ref, hs_ref, acc_ref):
        j = pl.program_id(1)

        @pl.when(j == 0)
        def _():
            xhat, _ = _norm_parts(x_ref[...])
            hb = _bf(xhat * nw_ref[...])
            hs_ref[...] = hb
            h_ref[...] = hb

        hb = hs_ref[...]
        g = _dot_nt(hb, wg_ref[...])
        u = _dot_nt(hb, wu_ref[...])
        a = _bf(g * _sigmoid(g) * u)
        g_ref[...] = _bf(g)
        u_ref[...] = _bf(u)
        a_ref[...] = a
        part = _dot(a, wd_ref[...])

        @pl.when(j == 0)
        def _():
            acc_ref[...] = part

        @pl.when(j > 0)
        def _():
            acc_ref[...] += part

        @pl.when(j == nf - 1)
        def _():
            xo_ref[...] = x_ref[...] + 0.5 * acc_ref[...]

    row = pl.BlockSpec((tm, d), lambda i, j: (i, 0))
    wspec = pl.BlockSpec((tf, d), lambda i, j: (j, 0))
    hid = pl.BlockSpec((tm, tf), lambda i, j: (i, j))
    return _pcall(
        body, (x, nw, wg_t, wu_t, wd), name=name, grid=(t // tm, nf),
        in_specs=[row, pl.BlockSpec((1, d), lambda i, j: (0, 0)), wspec, wspec, wspec],
        out_specs=[row, row, hid, hid, hid],
        out_shape=[jax.ShapeDtypeStruct((t, d), F32), jax.ShapeDtypeStruct((t, d), BF16)]
        + [jax.ShapeDtypeStruct((t, f), BF16)] * 3,
        scratch_shapes=[pltpu.VMEM((tm, d), BF16), pltpu.VMEM((tm, d), F32)],
        sem=("parallel", "arbitrary"), exchange=exchange)


def _ffn_bwd(dxo, x, nw, g, u, wg_t, wu_t, wd, *, name, exchange=None):
    t, d = x.shape
    f = wd.shape[0]
    tm, tf = _ffn_tiles(t, f)
    nf = f // tf

    def body(dxo_ref, x_ref, nw_ref, g_ref, u_ref, wg_ref, wu_ref, wd_ref,
             dx_ref, dy_ref, dg_ref, du_ref, dnw_ref, dys_ref, acc_ref):
        i, j = pl.program_id(0), pl.program_id(1)

        @pl.when(j == 0)
        def _():
            dyb = _bf(0.5 * dxo_ref[...])
            dys_ref[...] = dyb
            dy_ref[...] = dyb

        da = _dot_nt(dys_ref[...], wd_ref[...])
        gv = g_ref[...].astype(F32)
        uv = u_ref[...].astype(F32)
        s = _sigmoid(gv)
        dg = _bf(da * uv * (s * (1.0 + gv * (1.0 - s))))
        du = _bf(da * (gv * s))
        dg_ref[...] = dg
        du_ref[...] = du
        part = _dot(dg, wg_ref[...]) + _dot(du, wu_ref[...])

        @pl.when(j == 0)
        def _():
            acc_ref[...] = part

        @pl.when(j > 0)
        def _():
            acc_ref[...] += part

        @pl.when(j == nf - 1)
        def _():
            xhat, r = _norm_parts(x_ref[...])
            dx, dw = _norm_bwd(acc_ref[...], xhat, r, nw_ref[...])
            dx_ref[...] = dxo_ref[...] + dx

            @pl.when(i == 0)
            def _():
                dnw_ref[...] = dw

            @pl.when(i > 0)
            def _():
                dnw_ref[...] += dw

    row = pl.BlockSpec((tm, d), lambda i, j: (i, 0))
    vec = pl.BlockSpec((1, d), lambda i, j: (0, 0))
    wspec = pl.BlockSpec((tf, d), lambda i, j: (j, 0))
    hid = pl.BlockSpec((tm, tf), lambda i, j: (i, j))
    return _pcall(
        body, (dxo, x, nw, g, u, wg_t, wu_t, wd), name=name, grid=(t // tm, nf),
        in_specs=[row, row, vec, hid, hid, wspec, wspec, wspec],
        out_specs=[row, row, hid, hid, vec],
        out_shape=[jax.ShapeDtypeStruct((t, d), F32), jax.ShapeDtypeStruct((t, d), BF16),
                   jax.ShapeDtypeStruct((t, f), BF16), jax.ShapeDtypeStruct((t, f), BF16),
                   jax.ShapeDtypeStruct((1, d), F32)],
        scratch_shapes=[pltpu.VMEM((tm, d), BF16), pltpu.VMEM((tm, d), F32)],
        sem=("arbitrary", "arbitrary"), exchange=exchange)


def _shift_down(prev, cur, n):
    ext = jnp.concatenate([prev, cur], axis=0)
    return pltpu.roll(ext, n, axis=0)[prev.shape[0]:]


def _shift_up(cur, nxt, n):
    ext = jnp.concatenate([cur, nxt], axis=0)
    return pltpu.roll(ext, ext.shape[0] - n, axis=0)[:cur.shape[0]]


def _conv_specs(t, tb):
    hb = tb // CONV_HALO
    last = t // CONV_HALO - 1

    def tile(s):
        return pl.BlockSpec((None, tb, 128), lambda c, i: (s, i, c))

    def prev(s):
        return pl.BlockSpec((None, CONV_HALO, 128), lambda c, i: (s, jnp.maximum(i * hb - 1, 0), c))

    def nxt(s):
        return pl.BlockSpec((None, CONV_HALO, 128), lambda c, i: (s, jnp.minimum((i + 1) * hb, last), c))

    return tile, prev, nxt


def _conv_fwd(cols3, conv_w, *, name, tb=1024):
    _, t, bw = cols3.shape
    tb = _tile(t, tb)
    tile, prev, _ = _conv_specs(t, tb)

    def body(u_ref, b_ref, c_ref, up_ref, cp_ref, w_ref, y_ref):
        first = pl.program_id(1) == 0
        z = c_ref[...].astype(F32) * u_ref[...].astype(F32)
        zp = jnp.where(first, 0.0, cp_ref[...].astype(F32) * up_ref[...].astype(F32))
        conv = w_ref[0:1, :] * _shift_down(zp, z, 2) + w_ref[1:2, :] * _shift_down(zp, z, 1) + w_ref[2:3, :] * z
        y_ref[...] = _bf(b_ref[...].astype(F32) * conv)

    return pl.pallas_call(
        body, name=name, grid=(bw // 128, t // tb),
        in_specs=[tile(0), tile(1), tile(2), prev(0), prev(2), pl.BlockSpec((3, 128), lambda c, i: (0, c))],
        out_specs=pl.BlockSpec((tb, 128), lambda c, i: (i, c)),
        out_shape=jax.ShapeDtypeStruct((t, bw), BF16), compiler_params=_cparams("parallel", "parallel"),
    )(cols3, cols3, cols3, cols3, cols3, conv_w)


def _conv_bwd(cols3, conv_w, dy, *, name, tb=1024):
    _, t, bw = cols3.shape
    tb = _tile(t, tb)
    nt = t // tb
    tile, prev, nxt = _conv_specs(t, tb)
    hb = tb // CONV_HALO
    last = t // CONV_HALO - 1

    def body(u_ref, b_ref, c_ref, up_ref, cp_ref, bn_ref, dy_ref, dyn_ref, w_ref, d3_ref, dw_ref):
        i = pl.program_id(1)
        uv, bv, cv = u_ref[...].astype(F32), b_ref[...].astype(F32), c_ref[...].astype(F32)
        dyv = dy_ref[...].astype(F32)
        z = cv * uv
        zp = jnp.where(i == 0, 0.0, cp_ref[...].astype(F32) * up_ref[...].astype(F32))
        z1, z2 = _shift_down(zp, z, 1), _shift_down(zp, z, 2)
        w0, w1, w2 = w_ref[0:1, :], w_ref[1:2, :], w_ref[2:3, :]
        conv = w0 * z2 + w1 * z1 + w2 * z
        dconv = dyv * bv
        dconv_n = jnp.where(i == nt - 1, 0.0, dyn_ref[...].astype(F32) * bn_ref[...].astype(F32))
        dz = w2 * dconv + w1 * _shift_up(dconv, dconv_n, 1) + w0 * _shift_up(dconv, dconv_n, 2)
        d3_ref[0] = _bf(dz * cv)
        d3_ref[1] = _bf(dyv * conv)
        d3_ref[2] = _bf(dz * uv)
        dws = [jnp.sum(dconv * zz, axis=0, keepdims=True) for zz in (z2, z1, z)]

        @pl.when(i == 0)
        def _():
            for j in range(3):
                dw_ref[j:j + 1, :] = dws[j]

        @pl.when(i > 0)
        def _():
            for j in range(3):
                dw_ref[j:j + 1, :] += dws[j]

    dy_tile = pl.BlockSpec((tb, 128), lambda c, i: (i, c))
    dy_next = pl.BlockSpec((CONV_HALO, 128), lambda c, i: (jnp.minimum((i + 1) * hb, last), c))
    wspec = pl.BlockSpec((3, 128), lambda c, i: (0, c))
    return pl.pallas_call(
        body, name=name, grid=(bw // 128, nt),
        in_specs=[tile(0), tile(1), tile(2), prev(0), prev(2), nxt(1), dy_tile, dy_next, wspec],
        out_specs=[pl.BlockSpec((3, tb, 128), lambda c, i: (0, i, c)), wspec],
        out_shape=[jax.ShapeDtypeStruct((3, t, bw), BF16), jax.ShapeDtypeStruct((3, bw), F32)],
        compiler_params=_cparams("parallel", "arbitrary"),
    )(cols3, cols3, cols3, cols3, cols3, cols3, dy, dy, conv_w)


def _ret_consts():
    log_gamma = jnp.log1p(-jnp.exp2(-5.0 - jnp.arange(H_RET, dtype=F32)))
    pos = jnp.arange(CHUNK, dtype=F32)
    d_intra = jnp.exp(log_gamma[:, None, None] * jnp.abs(pos[:, None] - pos[None, :]))
    q_decay = jnp.exp(log_gamma[:, None] * (pos + 1.0))
    k_decay = jnp.exp(log_gamma[:, None] * (CHUNK - 1.0 - pos))
    chunk_decay = jnp.exp(log_gamma * CHUNK)
    wide = (H_RET, CHUNK, DK_RET)
    return (d_intra, jnp.broadcast_to(q_decay[:, :, None], wide), jnp.broadcast_to(k_decay[:, :, None], wide),
            jnp.broadcast_to(chunk_decay[:, None, None], (H_RET, 1, DK_RET)))


def _rope_tables(t):
    inv_freq = ROPE_BASE ** (-jnp.linspace(0.0, 1.0, DK_RET // 2, dtype=F32))
    ang = jnp.arange(t, dtype=F32)[:, None] * inv_freq[None, :]
    cos, sin = jnp.cos(ang), jnp.sin(ang)
    return jnp.concatenate([cos, cos], axis=1), jnp.concatenate([-sin, sin], axis=1)


def _rope(v, cc, ss):
    return v * cc + pltpu.roll(v, DK_RET // 2, axis=1) * ss


def _ret_in_specs(tb, blk):
    def col(s):
        return pl.BlockSpec((None, tb, DK_RET), lambda h, i: (s, blk(i), h))

    tab = pl.BlockSpec((tb, DK_RET), lambda h, i: (blk(i), 0))
    return ([col(3), col(4), col(5), col(6), tab, tab,
             pl.BlockSpec((None, CHUNK, CHUNK), lambda h, i: (h, 0, 0)),
             pl.BlockSpec((None, CHUNK, DK_RET), lambda h, i: (h, 0, 0)),
             pl.BlockSpec((None, CHUNK, DK_RET), lambda h, i: (h, 0, 0)),
             pl.BlockSpec((None, 1, DK_RET), lambda h, i: (h, 0, 0))])


def _ret_fwd(cols3, tables, consts, *, name):
    _, t, bw = cols3.shape
    tb = _tile(t, RET_TB)
    ncb = tb // CHUNK
    scale = DK_RET ** -0.5

    def body(q_ref, k_ref, v_ref, g_ref, cc_ref, ss_ref, di_ref, qd_ref, kd_ref, cd_ref, y_ref, st_ref, state):
        @pl.when(pl.program_id(1) == 0)
        def _():
            state[...] = jnp.zeros_like(state)

        di, qd, kd, cd = di_ref[...], qd_ref[...], kd_ref[...], cd_ref[...]
        for c in range(ncb):
            rows = pl.ds(c * CHUNK, CHUNK)
            cc, ss = cc_ref[rows, :], ss_ref[rows, :]
            qs = _rope(q_ref[rows, :].astype(F32), cc, ss) * scale
            kr = _rope(k_ref[rows, :].astype(F32), cc, ss)
            vb = v_ref[rows, :]
            gv = g_ref[rows, :].astype(F32)
            s_in = state[...]
            st_ref[c] = s_in
            inner = _dot_nt(_bf(qs), _bf(kr)) * di
            o = _dot(_bf(inner), vb) + _dot(_bf(qs * qd), _bf(s_in))
            state[...] = s_in * cd + _dot_tn(_bf(kr * kd), vb)
            on = o * lax.rsqrt(jnp.mean(o * o, axis=-1, keepdims=True) + EPS)
            y_ref[rows, :] = _bf(gv * _sigmoid(gv) * on)

    return pl.pallas_call(
        body, name=name, grid=(H_RET, t // tb),
        in_specs=_ret_in_specs(tb, lambda i: i),
        out_specs=[pl.BlockSpec((tb, DK_RET), lambda h, i: (i, h)),
                   pl.BlockSpec((None, ncb, DK_RET, DK_RET), lambda h, i: (h, i, 0, 0))],
        out_shape=[jax.ShapeDtypeStruct((t, bw), BF16),
                   jax.ShapeDtypeStruct((H_RET, t // CHUNK, DK_RET, DK_RET), F32)],
        scratch_shapes=[pltpu.VMEM((DK_RET, DK_RET), F32)],
        compiler_params=_cparams("parallel", "arbitrary"),
    )(cols3, cols3, cols3, cols3, tables[0], tables[1], *consts)


def _ret_bwd(cols3, tables, consts, states, dy, *, name, exchange=None):
    _, t, bw = cols3.shape
    tb = _tile(t, RET_TB)
    ncb = tb // CHUNK
    nb = t // tb
    scale = DK_RET ** -0.5

    def body(q_ref, k_ref, v_ref, g_ref, cc_ref, ss_ref, di_ref, qd_ref, kd_ref, cd_ref, st_ref, dy_ref,
             d4_ref, dstate):
        @pl.when(pl.program_id(1) == 0)
        def _():
            dstate[...] = jnp.zeros_like(dstate)

        di, qd, kd, cd = di_ref[...], qd_ref[...], kd_ref[...], cd_ref[...]
        for c in reversed(range(ncb)):
            rows = pl.ds(c * CHUNK, CHUNK)
            cc, ss = cc_ref[rows, :], ss_ref[rows, :]
            qs = _rope(q_ref[rows, :].astype(F32), cc, ss) * scale
            kr = _rope(k_ref[rows, :].astype(F32), cc, ss)
            vb = v_ref[rows, :]
            gv = g_ref[rows, :].astype(F32)
            dyv = dy_ref[rows, :].astype(F32)
            s_in = _bf(st_ref[c])
            qsb, krb = _bf(qs), _bf(kr)
            qdb, kdb = _bf(qs * qd), _bf(kr * kd)
            inner = _dot_nt(qsb, krb) * di
            innerb = _bf(inner)
            o = _dot(innerb, vb) + _dot(qdb, s_in)
            r = lax.rsqrt(jnp.mean(o * o, axis=-1, keepdims=True) + EPS)
            on = o * r
            sg = _sigmoid(gv)
            d4_ref[3, rows, :] = _bf(dyv * on * (sg * (1.0 + gv * (1.0 - sg))))
            don = dyv * (gv * sg)
            dob = _bf(r * (don - on * jnp.mean(don * on, axis=-1, keepdims=True)))
            dst = dstate[...]
            dstb = _bf(dst)
            dinner = _bf(_dot_nt(dob, vb) * di)
            dv = _dot_tn(innerb, dob) + _dot(kdb, dstb)
            dqs = _dot(dinner, krb) + _dot_nt(dob, s_in) * qd
            dkr = _dot_tn(dinner, qsb) + _dot_nt(vb, dstb) * kd
            dstate[...] = dst * cd + _dot_tn(qdb, dob)
            d4_ref[0, rows, :] = _bf(_rope_bwd(dqs * scale, cc, ss))
            d4_ref[1, rows, :] = _bf(_rope_bwd(dkr, cc, ss))
            d4_ref[2, rows, :] = _bf(dv)

    rev = lambda i: nb - 1 - i
    outs, moved = _pcall(
        body, (cols3, cols3, cols3, cols3, tables[0], tables[1], *consts, states, dy), name=name, grid=(H_RET, nb),
        in_specs=_ret_in_specs(tb, rev)
        + [pl.BlockSpec((None, ncb, DK_RET, DK_RET), lambda h, i: (h, rev(i), 0, 0)),
           pl.BlockSpec((tb, DK_RET), lambda h, i: (rev(i), h))],
        out_specs=[pl.BlockSpec((4, tb, DK_RET), lambda h, i: (0, rev(i), h))],
        out_shape=[jax.ShapeDtypeStruct((4, t, bw), BF16)],
        scratch_shapes=[pltpu.VMEM((DK_RET, DK_RET), F32)],
        sem=("parallel", "arbitrary"), exchange=exchange)
    return outs[0], moved


def _rope_bwd(dv, cc, ss):
    return dv * cc + pltpu.roll(dv * ss, DK_RET // 2, axis=1)


def _att_window(i):
    return pl.multiple_of(jnp.maximum(i - ATT_LOOKBACK // ATT_QB, 0) * ATT_QB, ATT_QB)


def _att_mask(i, ws):
    qchunk = (i * ATT_QB + lax.broadcasted_iota(jnp.int32, (ATT_QB, ATT_WIN), 0)) // CHUNK
    kchunk = (ws + lax.broadcasted_iota(jnp.int32, (ATT_QB, ATT_WIN), 1)) // CHUNK
    return (kchunk <= qchunk) & (kchunk >= qchunk - N_PREV_CHUNKS)


def _att_fwd(cols3, bias3, *, name, exchange=None):
    _, t, bw = cols3.shape
    assert t % ATT_QB == 0 and t >= ATT_WIN
    scale = DH_ATT ** -0.5
    nvar = ATT_LOOKBACK // ATT_QB

    def body(q_ref, k_ref, v_ref, b_ref, y_ref, lse_ref):
        i = pl.program_id(1)
        ws = _att_window(i)
        valid = _att_mask(i, ws)
        q = q_ref[...].astype(F32)
        kw = k_ref[pl.ds(ws, ATT_WIN), :]
        vw = v_ref[pl.ds(ws, ATT_WIN), :]
        lane_head = lax.broadcasted_iota(jnp.int32, (ATT_QB, 128), 1) // DH_ATT
        out = jnp.zeros((ATT_QB, 128), F32)
        lse = jnp.zeros((ATT_QB, 128), F32)
        for hh in range(2):
            mine = lane_head == hh
            s = _dot_nt(_bf(jnp.where(mine, q, 0.0)), kw) * scale + b_ref[hh]
            s = jnp.where(valid, s, NEG_INF)
            mx = jnp.max(s, axis=-1, keepdims=True)
            p = jnp.exp(s - mx)
            l = jnp.sum(p, axis=-1, keepdims=True)
            out = jnp.where(mine, _dot(_bf(p), vw) / l, out)
            lse = jnp.where(mine, mx + jnp.log(l), lse)
        y_ref[...] = _bf(out)
        lse_ref[...] = lse

    kv = lambda s: pl.BlockSpec((None, t, 128), lambda hp, i: (s, 0, hp))
    return _pcall(
        body, (cols3, cols3, cols3, bias3), name=name, grid=(H_ATT // 2, t // ATT_QB),
        in_specs=[pl.BlockSpec((None, ATT_QB, 128), lambda hp, i: (7, i, hp)), kv(8), kv(9),
                  pl.BlockSpec((None, 2, ATT_QB, ATT_WIN), lambda hp, i: (jnp.minimum(i, nvar), hp, 0, 0))],
        out_specs=[pl.BlockSpec((ATT_QB, 128), lambda hp, i: (i, hp)),
                   pl.BlockSpec((None, ATT_QB, 128), lambda hp, i: (hp, i, 0))],
        out_shape=[jax.ShapeDtypeStruct((t, bw), BF16), jax.ShapeDtypeStruct((H_ATT // 2, t, 128), F32)],
        scratch_shapes=[], sem=("parallel", "arbitrary"), exchange=exchange)


def _att_bwd(cols3, bias3, y, lse, dy, *, name):
    _, t, bw = cols3.shape
    nq = t // ATT_QB
    scale = DH_ATT ** -0.5
    nvar = ATT_LOOKBACK // ATT_QB

    def body(q_ref, k_ref, v_ref, b_ref, y_ref, lse_ref, dy_ref, d3_ref, db_ref, dk_acc, dv_acc):
        i = pl.program_id(1)

        @pl.when(i == 0)
        def _():
            dk_acc[...] = jnp.zeros_like(dk_acc)
            dv_acc[...] = jnp.zeros_like(dv_acc)

        ws = _att_window(i)
        valid = _att_mask(i, ws)
        q = q_ref[...].astype(F32)
        kw = k_ref[pl.ds(ws, ATT_WIN), :]
        vw = v_ref[pl.ds(ws, ATT_WIN), :]
        do = dy_ref[...].astype(F32)
        dof = do * y_ref[...].astype(F32)
        lsev = lse_ref[...]
        lane_head = lax.broadcasted_iota(jnp.int32, (ATT_QB, 128), 1) // DH_ATT
        dq = jnp.zeros((ATT_QB, 128), F32)
        dk = jnp.zeros((ATT_WIN, 128), F32)
        dv = jnp.zeros((ATT_WIN, 128), F32)
        first = i <= nvar
        for hh in range(2):
            mine = lane_head == hh
            qh = _bf(jnp.where(mine, q, 0.0))
            doh = _bf(jnp.where(mine, do, 0.0))
            s = _dot_nt(qh, kw) * scale + b_ref[hh]
            lse_h = jnp.max(jnp.where(mine, lsev, NEG_INF), axis=-1, keepdims=True)
            p = jnp.where(valid, jnp.exp(s - lse_h), 0.0)
            delta = jnp.sum(jnp.where(mine, dof, 0.0), axis=-1, keepdims=True)
            ds = p * (_dot_nt(doh, vw) - delta)

            @pl.when(first)
            def _():
                db_ref[hh] = ds

            @pl.when(jnp.logical_not(first))
            def _():
                db_ref[hh] += ds

            dsb = _bf(ds * scale)
            dq = jnp.where(mine, _dot(dsb, kw), dq)
            dk = dk + _dot_tn(dsb, qh)
            dv = dv + _dot_tn(_bf(p), doh)
        d3_ref[0, pl.ds(pl.multiple_of(i * ATT_QB, ATT_QB), ATT_QB), :] = _bf(dq)
        dk_acc[pl.ds(ws, ATT_WIN), :] += dk
        dv_acc[pl.ds(ws, ATT_WIN), :] += dv

        @pl.when(i == nq - 1)
        def _():
            d3_ref[1] = _bf(dk_acc[...])
            d3_ref[2] = _bf(dv_acc[...])

    kv = lambda s: pl.BlockSpec((None, t, 128), lambda hp, i: (s, 0, hp))
    qrow = pl.BlockSpec((ATT_QB, 128), lambda hp, i: (i, hp))
    btile = pl.BlockSpec((None, 2, ATT_QB, ATT_WIN), lambda hp, i: (jnp.minimum(i, nvar), hp, 0, 0))
    return pl.pallas_call(
        body, name=name, grid=(H_ATT // 2, nq),
        in_specs=[pl.BlockSpec((None, ATT_QB, 128), lambda hp, i: (7, i, hp)), kv(8), kv(9), btile, qrow,
                  pl.BlockSpec((None, ATT_QB, 128), lambda hp, i: (hp, i, 0)), qrow],
        out_specs=[pl.BlockSpec((3, t, 128), lambda hp, i: (0, 0, hp)), btile],
        out_shape=[jax.ShapeDtypeStruct((3, t, bw), BF16),
                   jax.ShapeDtypeStruct((nvar + 1, H_ATT, ATT_QB, ATT_WIN), F32)],
        scratch_shapes=[pltpu.VMEM((t, 128), F32), pltpu.VMEM((t, 128), F32)],
        compiler_params=_cparams("parallel", "arbitrary"),
    )(cols3, cols3, cols3, bias3, y, lse, dy)


SKEW_W = ATT_WIN + ATT_QB
REL_PAD = 384


def _rel_onehot(v):
    r = lax.broadcasted_iota(jnp.int32, (REL_PAD, SKEW_W), 0)
    j = lax.broadcasted_iota(jnp.int32, (REL_PAD, SKEW_W), 1)
    dist = jnp.where(j < ATT_WIN, v * ATT_QB - j, v * ATT_QB + SKEW_W - j)
    col = jnp.clip(dist, -REL_CLIP, REL_CLIP) + REL_CLIP
    return _bf(jnp.where(col == r, 1.0, 0.0))


def _split3(v):
    hi = _bf(v)
    rest = v - hi.astype(F32)
    mid = _bf(rest)
    return hi, mid, _bf(rest - mid.astype(F32))


def _skew_rows(a, forward):
    row = lax.broadcasted_iota(jnp.int32, a.shape, 0)
    for b in range(ATT_QB.bit_length() - 1):
        shift = (1 << b) if forward else SKEW_W - (1 << b)
        a = jnp.where(((row >> b) & 1) == 1, pltpu.roll(a, shift, axis=1), a)
    return a


def _bias_tiles(rel_bias, *, name):
    nvar = ATT_LOOKBACK // ATT_QB + 1
    rel = jnp.pad(rel_bias, ((0, 0), (0, REL_PAD - N_REL)))

    def body(rel_ref, o_ref, ext_ref):
        onehot = _rel_onehot(pl.program_id(0))
        ext_ref[...] = sum(_dot(part, onehot) for part in _split3(rel_ref[...]))
        for h in range(H_ATT):
            tile = _skew_rows(jnp.broadcast_to(ext_ref[h:h + 1, :], (ATT_QB, SKEW_W)), True)
            o_ref[h] = tile[:, :ATT_WIN]

    return pl.pallas_call(
        body, name=name, grid=(nvar,), in_specs=[pl.BlockSpec((H_ATT, REL_PAD), lambda v: (0, 0))],
        out_specs=pl.BlockSpec((None, H_ATT, ATT_QB, ATT_WIN), lambda v: (v, 0, 0, 0)),
        out_shape=jax.ShapeDtypeStruct((nvar, H_ATT, ATT_QB, ATT_WIN), F32),
        scratch_shapes=[pltpu.VMEM((H_ATT, SKEW_W), F32)], compiler_params=_cparams("parallel"),
    )(rel)


def _rel_bias_grad(dbias3, *, name):
    nvar = dbias3.shape[0]

    def body(db_ref, o_ref, diag_ref):
        v = pl.program_id(0)
        for h in range(H_ATT):
            tile = jnp.concatenate([db_ref[h], jnp.zeros((ATT_QB, ATT_QB), F32)], axis=1)
            diag_ref[h:h + 1, :] = jnp.sum(_skew_rows(tile, False), axis=0, keepdims=True)
        onehot = _rel_onehot(v)
        part = sum(_dot_nt(p, onehot) for p in _split3(diag_ref[...]))

        @pl.when(v == 0)
        def _():
            o_ref[...] = part

        @pl.when(v > 0)
        def _():
            o_ref[...] += part

    out = pl.pallas_call(
        body, name=name, grid=(nvar,),
        in_specs=[pl.BlockSpec((None, H_ATT, ATT_QB, ATT_WIN), lambda v: (v, 0, 0, 0))],
        out_specs=pl.BlockSpec((H_ATT, REL_PAD), lambda v: (0, 0)),
        out_shape=jax.ShapeDtypeStruct((H_ATT, REL_PAD), F32),
        scratch_shapes=[pltpu.VMEM((H_ATT, SKEW_W), F32)], compiler_params=_cparams("arbitrary"),
    )(dbias3)
    return out[:, :N_REL]


def _merge_fwd(h, ys, wmg, wb_t, *, name, tm=512):
    t, d = h.shape
    bw = ys[0].shape[1]
    tm = _tile(t, tm)

    def body(h_ref, y0_ref, y1_ref, y2_ref, wg_ref, wb_ref, m_ref, s_ref, p_ref):
        hv = h_ref[...]
        total = jnp.zeros((tm, d), F32)
        for b, y_ref in enumerate((y0_ref, y1_ref, y2_ref)):
            s = _sigmoid(_dot(hv, wg_ref[b]))
            p = _dot_nt(y_ref[...], wb_ref[b])
            s_ref[b] = _bf(s)
            p_ref[b] = _bf(p)
            total = total + s * p
        m_ref[...] = _bf(total)

    row = pl.BlockSpec((tm, d), lambda i: (i, 0))
    yrow = pl.BlockSpec((tm, bw), lambda i: (i, 0))
    three = pl.BlockSpec((3, tm, d), lambda i: (0, i, 0))
    return pl.pallas_call(
        body, name=name, grid=(t // tm,),
        in_specs=[row, yrow, yrow, yrow, pl.BlockSpec((3, d, d), lambda i: (0, 0, 0)),
                  pl.BlockSpec((3, d, bw), lambda i: (0, 0, 0))],
        out_specs=[row, three, three],
        out_shape=[jax.ShapeDtypeStruct((t, d), BF16), jax.ShapeDtypeStruct((3, t, d), BF16),
                   jax.ShapeDtypeStruct((3, t, d), BF16)],
        compiler_params=_cparams("parallel"),
    )(h, *ys, wmg, wb_t)


def _merge_bwd(dm, s3, p3, wmg, wb_t, *, name, tm=512):
    _, t, d = s3.shape
    bw = wb_t.shape[2]
    tm = _tile(t, tm)

    def body(dm_ref, s_ref, p_ref, wg_ref, wb_ref, dgp_ref, dp_ref, dy_ref, dh_ref):
        dmv = dm_ref[...].astype(F32)
        dh = jnp.zeros((tm, d), F32)
        for b in range(3):
            s = s_ref[b].astype(F32)
            dgp = _bf(dmv * p_ref[b].astype(F32) * s * (1.0 - s))
            dp = _bf(dmv * s)
            dgp_ref[b] = dgp
            dp_ref[b] = dp
            dy_ref[b] = _bf(_dot(dp, wb_ref[b]))
            dh = dh + _dot_nt(dgp, wg_ref[b])
        dh_ref[...] = dh

    row = pl.BlockSpec((tm, d), lambda i: (i, 0))
    three = pl.BlockSpec((3, tm, d), lambda i: (0, i, 0))
    return pl.pallas_call(
        body, name=name, grid=(t // tm,),
        in_specs=[row, three, three, pl.BlockSpec((3, d, d), lambda i: (0, 0, 0)),
                  pl.BlockSpec((3, d, bw), lambda i: (0, 0, 0))],
        out_specs=[three, three, pl.BlockSpec((3, tm, bw), lambda i: (0, i, 0)), row],
        out_shape=[jax.ShapeDtypeStruct((3, t, d), BF16), jax.ShapeDtypeStruct((3, t, d), BF16),
                   jax.ShapeDtypeStruct((3, t, bw), BF16), jax.ShapeDtypeStruct((t, d), F32)],
        compiler_params=_cparams("parallel"),
    )(dm, s3, p3, wmg, wb_t)


def _carried(hooks, slot, state=None):
    if not hooks or slot not in hooks:
        return None, lambda buf: None
    make, done = hooks[slot]
    return make(state), done


def _layer_fwd(x, p, aux, tag, hooks=None):
    ex, done = _carried(hooks, "ffn1")
    (x1, h1, g1, u1, a1), buf = _ffn_fwd(x, p["n1"], p["wg1"], p["wu1"], p["wd1"], name=f"ffn1_fwd_{tag}", exchange=ex)
    done(buf)
    h2 = _rmsnorm_fwd(x1, p["nmix"], name=f"mixnorm_fwd_{tag}")
    cols3 = _mm_to_slices(h2, p["win"], name=f"inproj_fwd_{tag}")
    bias3 = _bias_tiles(p["rel_bias"], name=f"bias_tiles_{tag}")
    y_conv = _conv_fwd(cols3, p["conv_w"], name=f"conv_fwd_{tag}")
    y_ret, states = _ret_fwd(cols3, aux["rope"], aux["ret"], name=f"ret_fwd_{tag}")
    ex, done = _carried(hooks, "att")
    (y_att, lse), buf = _att_fwd(cols3, bias3, name=f"att_fwd_{tag}", exchange=ex)
    done(buf)
    merged, s3, p3 = _merge_fwd(h2, (y_conv, y_ret, y_att), p["wmg"], p["wb"], name=f"merge_fwd_{tag}")
    x2 = _mm_rows(merged, p["wout"], dims=NN, out_dtype=F32, res=x1, name=f"outproj_fwd_{tag}")
    ex, done = _carried(hooks, "ffn2")
    (x3, h3, g2, u2, a2), buf = _ffn_fwd(x2, p["n2"], p["wg2"], p["wu2"], p["wd2"], name=f"ffn2_fwd_{tag}", exchange=ex)
    done(buf)
    saved = dict(x0=x, h1=h1, g1=g1, u1=u1, a1=a1, x1=x1, h2=h2, cols3=cols3, bias3=bias3, y_conv=y_conv,
                 y_ret=y_ret, states=states, y_att=y_att, lse=lse, merged=merged, s3=s3, p3=p3, x2=x2, h3=h3,
                 g2=g2, u2=u2, a2=a2)
    return x3, saved


def _ffn_grads(dxo, x, h, g, u, a, nw, wg, wu, wd, tag, exchange):
    (dx, dyb, dg, du, dnw), buf = _ffn_bwd(dxo, x, nw, g, u, wg, wu, wd, name=f"{tag}_bwd", exchange=exchange)
    return (dx, dnw, _mm_tn(dg, h, name=f"{tag}_dwg"), _mm_tn(du, h, name=f"{tag}_dwu"),
            _mm_tn(a, dyb, name=f"{tag}_dwd"), buf)


def _layer_bwd(dx3, p, s, aux, tag, grads, all_grads, hooks=None):
    ex, done = _carried(hooks, "ffn2_bwd", all_grads)
    dx2, grads["n2"], grads["wg2"], grads["wu2"], grads["wd2"], buf = _ffn_grads(
        dx3, s["x2"], s["h3"], s["g2"], s["u2"], s["a2"], p["n2"], p["wg2"], p["wu2"], p["wd2"], f"ffn2_{tag}", ex)
    done(buf)

    dm = _mm_rows(dx2, p["wout"], dims=NT, out_dtype=BF16, name=f"outproj_dm_{tag}")
    grads["wout"] = _mm_tn(s["merged"], dx2, name=f"outproj_dw_{tag}")
    dgp3, dp3, dy3, dh2 = _merge_bwd(dm, s["s3"], s["p3"], p["wmg"], p["wb"], name=f"merge_bwd_{tag}")
    grads["wmg"] = _mm_tn_batch(s["h2"], dgp3, a_batched=False, name=f"merge_dwg_{tag}")
    y3 = jnp.stack([s["y_conv"], s["y_ret"], s["y_att"]])
    grads["wb"] = _mm_tn_batch(dp3, y3, a_batched=True, name=f"merge_dwb_{tag}")

    dconv3, grads["conv_w"] = _conv_bwd(s["cols3"], p["conv_w"], dy3[0], name=f"conv_bwd_{tag}")
    ex, done = _carried(hooks, "ret_bwd", all_grads)
    dret4, buf = _ret_bwd(s["cols3"], aux["rope"], aux["ret"], s["states"], dy3[1], name=f"ret_bwd_{tag}", exchange=ex)
    done(buf)
    datt3, dbias3 = _att_bwd(s["cols3"], s["bias3"], s["y_att"], s["lse"], dy3[2], name=f"att_bwd_{tag}")
    grads["rel_bias"] = _rel_bias_grad(dbias3, name=f"bias_grad_{tag}")

    w = BRANCH_W
    dh2 = _mm_from_slices(dconv3, p["win"], 0, dh2, name=f"inproj_dh_conv_{tag}")
    dh2 = _mm_from_slices(dret4, p["win"], 3 * w, dh2, name=f"inproj_dh_ret_{tag}")
    dh2 = _mm_from_slices(datt3, p["win"], 7 * w, dh2, name=f"inproj_dh_att_{tag}")
    grads["win"] = jnp.concatenate([
        _mm_tn_slices(dconv3, s["h2"], name=f"inproj_dw_conv_{tag}"),
        _mm_tn_slices(dret4, s["h2"], name=f"inproj_dw_ret_{tag}"),
        _mm_tn_slices(datt3, s["h2"], name=f"inproj_dw_att_{tag}")], axis=0)
    dx1, grads["nmix"] = _rmsnorm_bwd(dh2, s["x1"], p["nmix"], dx2, name=f"mixnorm_bwd_{tag}")

    ex, done = _carried(hooks, "ffn1_bwd", all_grads)
    dx0, grads["n1"], grads["wg1"], grads["wu1"], grads["wd1"], buf = _ffn_grads(
        dx1, s["x0"], s["h1"], s["g1"], s["u1"], s["a1"], p["n1"], p["wg1"], p["wu1"], p["wd1"], f"ffn1_{tag}", ex)
    done(buf)
    return dx0


def _device_step(x, target, layers, final_norm, fwd_hooks=None, bwd_hooks=None):
    t = x.shape[0]
    depth = len(layers)
    aux = dict(rope=_rope_tables(t), ret=_ret_consts())
    saved = []
    for l, p in enumerate(layers):
        x, s = _layer_fwd(x, p, aux, f"l{l}", fwd_hooks[l] if fwd_hooks else None)
        saved.append(s)
    dx, dfinal, loss_row = _loss_fwd_bwd(x, final_norm, target, name="loss_fwd_bwd")
    grads = [dict() for _ in range(depth)]
    for l in reversed(range(depth)):
        dx = _layer_bwd(dx, layers[l], saved[l], aux, f"l{l}", grads[l], grads, bwd_hooks[l] if bwd_hooks else None)
    return loss_row, dx, grads, dfinal


def _sum_slots(parts, *, name, tr=512):
    n, r, cdim = parts.shape
    tr = _tile(r, tr)

    def body(p_ref, o_ref):
        acc = p_ref[0].astype(F32)
        for s in range(1, n):
            acc = acc + p_ref[s].astype(F32)
        o_ref[...] = acc

    return pl.pallas_call(
        body, name=name, grid=(r // tr,), in_specs=[pl.BlockSpec((n, tr, cdim), lambda i: (0, i, 0))],
        out_specs=pl.BlockSpec((tr, cdim), lambda i: (i, 0)), out_shape=jax.ShapeDtypeStruct((r, cdim), F32),
        compiler_params=_cparams("parallel"),
    )(parts)


def _all_reduce_small(v, *, name):
    r = v.shape[0]

    def body(x_ref, o_ref, slots, send_sems, recv_sems):
        x, y, c, me = _my_position()
        slots[me] = x_ref[...]
        sends = []
        for k in range(1, N_DEV):
            peer, _ = _peer(x, y, c, k)
            cp = pltpu.make_async_remote_copy(src_ref=x_ref, dst_ref=slots.at[me], send_sem=send_sems.at[k - 1],
                                              recv_sem=recv_sems.at[k - 1], device_id=peer, device_id_type=MESH)
            cp.start()
            sends.append(cp)
        for k in range(1, N_DEV):
            peer, peer_id = _peer(x, y, c, k)
            pltpu.make_async_remote_copy(src_ref=x_ref, dst_ref=slots.at[peer_id], send_sem=send_sems.at[k - 1],
                                         recv_sem=recv_sems.at[k - 1], device_id=peer, device_id_type=MESH).wait_recv()
        for cp in sends:
            cp.wait_send()
        acc = slots[0]
        for s in range(1, N_DEV):
            acc = acc + slots[s]
        o_ref[...] = acc

    return pl.pallas_call(
        body, name=name, in_specs=[pl.BlockSpec(memory_space=pltpu.VMEM)],
        out_specs=pl.BlockSpec(memory_space=pltpu.VMEM), out_shape=jax.ShapeDtypeStruct((r, 128), F32),
        scratch_shapes=[pltpu.VMEM((N_DEV, r, 128), F32), pltpu.SemaphoreType.DMA((N_DEV - 1,)),
                        pltpu.SemaphoreType.DMA((N_DEV - 1,))],
    )(v)


def _adamw(w, g, m, v, *, name, tr=256):
    shape = w.shape
    cdim = shape[-1]
    w2, g2, m2, v2 = (a.reshape(-1, cdim) for a in (w, g, m, v))
    r = w2.shape[0]
    tr = _tile(r, tr) if r % 8 == 0 else r
    c1 = 1.0 - ADAM_B1 ** ADAM_STEP
    c2 = 1.0 - ADAM_B2 ** ADAM_STEP

    def body(w_ref, g_ref, m_ref, v_ref, d_ref, mo_ref, vo_ref):
        gv = g_ref[...]
        mn = ADAM_B1 * m_ref[...] + (1.0 - ADAM_B1) * gv
        vn = ADAM_B2 * v_ref[...] + (1.0 - ADAM_B2) * (gv * gv)
        d_ref[...] = -ADAM_LR * ((mn / c1) / (jnp.sqrt(vn / c2) + ADAM_EPS) + ADAM_WD * w_ref[...])
        mo_ref[...] = mn
        vo_ref[...] = vn

    spec = pl.BlockSpec((tr, cdim), lambda i: (i, 0))
    outs = pl.pallas_call(
        body, name=name, grid=(r // tr,), in_specs=[spec] * 4, out_specs=[spec] * 3,
        out_shape=[jax.ShapeDtypeStruct((r, cdim), F32)] * 3, compiler_params=_cparams("parallel"),
    )(w2, g2, m2, v2)
    return tuple(o.reshape(shape) for o in outs)


GROUPS = (("f1", ("wg1", "wu1", "wd1")), ("mx", ("win", "wb", "wmg", "wout")), ("f2", ("wg2", "wu2", "wd2")))
BIG = tuple(nm for _, members in GROUPS for nm in members)


def _to_rows(name, w, d):
    if name in ("wg1", "wu1", "wg2", "wu2", "win"):
        return w.T
    if name == "wb":
        return w.transpose(0, 2, 1).reshape(-1, d)
    if name == "wmg":
        return w.reshape(-1, d)
    return w


def _from_rows(name, rows, d):
    if name in ("wg1", "wu1", "wg2", "wu2", "win"):
        return rows.T
    if name == "wb":
        return rows.reshape(3, -1, BRANCH_W).transpose(0, 2, 1)
    if name == "wmg":
        return rows.reshape(3, -1, d)
    return rows


def _full_from_gathered(name, g, d):
    if name == "wb":
        return g.reshape(N_DEV, 3, -1, BRANCH_W).transpose(1, 0, 2, 3).reshape(3, d, BRANCH_W)
    if name == "wmg":
        return g.reshape(N_DEV, 3, -1, d).transpose(1, 0, 2, 3).reshape(3, d, d)
    return g.reshape(-1, d)


def _gathered_from_full(name, full, d):
    if name == "wb":
        return full.reshape(3, N_DEV, -1, BRANCH_W).transpose(1, 0, 2, 3).reshape(N_DEV, -1, d)
    if name == "wmg":
        return full.reshape(3, N_DEV, -1, d).transpose(1, 0, 2, 3).reshape(N_DEV, -1, d)
    return full.reshape(N_DEV, -1, d)


def kernel(x, ffn1_norm, ffn1_w_gate, ffn1_w_up, ffn1_w_down, mix_norm, w_in, conv_w, rel_bias, w_branch, w_merge_gate, w_out, ffn2_norm, ffn2_w_gate, ffn2_w_up, ffn2_w_down, final_norm, loss_target, m_ffn1_norm, m_ffn1_w_gate, m_ffn1_w_up, m_ffn1_w_down, m_mix_norm, m_w_in, m_conv_w, m_rel_bias, m_w_branch, m_w_merge_gate, m_w_out, m_ffn2_norm, m_ffn2_w_gate, m_ffn2_w_up, m_ffn2_w_down, m_final_norm, v_ffn1_norm, v_ffn1_w_gate, v_ffn1_w_up, v_ffn1_w_down, v_mix_norm, v_w_in, v_conv_w, v_rel_bias, v_w_branch, v_w_merge_gate, v_w_out, v_ffn2_norm, v_ffn2_w_gate, v_ffn2_w_up, v_ffn2_w_down, v_final_norm):
    names = ["ffn1_norm", "ffn1_w_gate", "ffn1_w_up", "ffn1_w_down", "mix_norm", "w_in", "conv_w", "rel_bias",
             "w_branch", "w_merge_gate", "w_out", "ffn2_norm", "ffn2_w_gate", "ffn2_w_up", "ffn2_w_down", "final_norm"]
    weights = dict(zip(names, (ffn1_norm, ffn1_w_gate, ffn1_w_up, ffn1_w_down, mix_norm, w_in, conv_w, rel_bias,
                               w_branch, w_merge_gate, w_out, ffn2_norm, ffn2_w_gate, ffn2_w_up, ffn2_w_down,
                               final_norm)))
    m_in = dict(zip(names, (m_ffn1_norm, m_ffn1_w_gate, m_ffn1_w_up, m_ffn1_w_down, m_mix_norm, m_w_in, m_conv_w,
                            m_rel_bias, m_w_branch, m_w_merge_gate, m_w_out, m_ffn2_norm, m_ffn2_w_gate,
                            m_ffn2_w_up, m_ffn2_w_down, m_final_norm)))
    v_in = dict(zip(names, (v_ffn1_norm, v_ffn1_w_gate, v_ffn1_w_up, v_ffn1_w_down, v_mix_norm, v_w_in, v_conv_w,
                            v_rel_bias, v_w_branch, v_w_merge_gate, v_w_out, v_ffn2_norm, v_ffn2_w_gate,
                            v_ffn2_w_up, v_ffn2_w_down, v_final_norm)))
    big_of = dict(wg1="ffn1_w_gate", wu1="ffn1_w_up", wd1="ffn1_w_down", win="w_in", wb="w_branch",
                  wmg="w_merge_gate", wout="w_out", wg2="ffn2_w_gate", wu2="ffn2_w_up", wd2="ffn2_w_down")
    depth = ffn1_norm.shape[0]
    d = x.shape[-1]
    xs = x.reshape(-1, d)
    target = loss_target.reshape(-1, d)
    _, _, _, me = _my_position()

    packs, layout = {}, {}
    for l in range(depth):
        for grp, members in GROUPS:
            blocks, row = [], 0
            layout[grp] = []
            for nm in members:
                rows = _to_rows(nm, weights[big_of[nm]][l], d)
                blocks.append(rows.astype(BF16))
                layout[grp].append((nm, row, rows.shape[0]))
                row += rows.shape[0]
            packs[(l, grp)] = jnp.concatenate(blocks, axis=0)

    def unpack(grp, gathered):
        return {nm: _full_from_gathered(nm, gathered[:, r0:r0 + nr], d) for nm, r0, nr in layout[grp]}

    conv_cols = conv_w.shape[-1]
    conv_full = _exchange("gather", conv_w.reshape(depth * 3, conv_cols), name="gather_conv_w")
    conv_full = conv_full.reshape(N_DEV, depth, 3, conv_cols).transpose(1, 2, 0, 3).reshape(depth, 3, -1)
    layers = [dict(n1=ffn1_norm[l][None], nmix=mix_norm[l][None], n2=ffn2_norm[l][None], conv_w=conv_full[l],
                   rel_bias=rel_bias[l]) for l in range(depth)]
    layers[0].update(unpack("f1", _exchange("gather", packs[(0, "f1")], name="gather_l0_f1")))

    def gather_hook(l, grp):
        return (lambda _: ("gather", packs[(l, grp)])), (lambda buf: layers[l].update(unpack(grp, buf)))

    fwd_hooks = []
    for l in range(depth):
        hooks = {"ffn1": gather_hook(l, "mx"), "att": gather_hook(l, "f2")}
        if l + 1 < depth:
            hooks["ffn2"] = gather_hook(l + 1, "f1")
        fwd_hooks.append(hooks)

    received = {}

    def pack_grads(g, grp):
        return jnp.concatenate([_gathered_from_full(nm, g[nm], d).astype(BF16) for nm, _, _ in layout[grp]], axis=1)

    def scatter_hook(l, grp):
        return (lambda all_grads: ("scatter", pack_grads(all_grads[l], grp))), (lambda buf: received.update({(l, grp): buf}))

    bwd_hooks = []
    for l in range(depth):
        hooks = {"ret_bwd": scatter_hook(l, "f2"), "ffn1_bwd": scatter_hook(l, "mx")}
        if l + 1 < depth:
            hooks["ffn2_bwd"] = scatter_hook(l + 1, "f1")
        bwd_hooks.append(hooks)

    loss_row, dx, grads, dfinal = _device_step(xs, target, layers, final_norm[None], fwd_hooks, bwd_hooks)
    received[(0, "f1")] = _exchange("scatter", pack_grads(grads[0], "f1"), name="scatter_l0_f1")

    big_grads = {nm: [None] * depth for nm in BIG}
    for l in range(depth):
        for grp, _ in GROUPS:
            shard_grad = _sum_slots(received[(l, grp)], name=f"sum_grads_l{l}_{grp}")
            for nm, r0, nr in layout[grp]:
                big_grads[nm][l] = _from_rows(nm, shard_grad[r0:r0 + nr], d)

    small = {"ffn1_norm": jnp.stack([grads[l]["n1"][0] for l in range(depth)]),
             "mix_norm": jnp.stack([grads[l]["nmix"][0] for l in range(depth)]),
             "ffn2_norm": jnp.stack([grads[l]["n2"][0] for l in range(depth)]),
             "final_norm": dfinal[0],
             "rel_bias": jnp.stack([grads[l]["rel_bias"] for l in range(depth)]),
             "conv_w": jnp.stack([grads[l]["conv_w"] for l in range(depth)]),
             "loss": loss_row[0, :1]}
    order = list(small)
    flat = jnp.concatenate([small[k].reshape(-1) for k in order])
    pad = (-flat.shape[0]) % 1024
    summed = _all_reduce_small(jnp.pad(flat, (0, pad)).reshape(-1, 128), name="reduce_small").reshape(-1)
    pos = 0
    for k in order:
        n = small[k].size
        small[k] = summed[pos:pos + n].reshape(small[k].shape)
        pos += n
    small["conv_w"] = lax.dynamic_slice_in_dim(small["conv_w"], me * conv_cols, conv_cols, axis=2)

    grad_w = {big_of[nm]: jnp.stack(big_grads[nm]) for nm in BIG}
    grad_w.update({k: small[k] for k in order if k != "loss"})
    delta, new_m, new_v = {}, {}, {}
    for nm in names:
        delta[nm], new_m[nm], new_v[nm] = _adamw(weights[nm], grad_w[nm], m_in[nm], v_in[nm], name=f"adamw_{nm}")
    return (small["loss"].reshape(()), dx.reshape(x.shape), *[grad_w[n] for n in names], *[delta[n] for n in names],
            *[new_m[n] for n in names], *[new_v[n] for n in names])
```

```python
import functools
import math

import jax
import jax.numpy as jnp
from jax import lax
from jax.experimental import pallas as pl
from jax.experimental.pallas import tpu as pltpu

F32 = jnp.float32
BF16 = jnp.bfloat16

N_DEV = 8
EPS = 1e-6
CHUNK = 64
BRANCH_W = 512
N_SLICES = 10
H_RET = 4
DK_RET = 128
H_ATT = 8
DH_ATT = 64
N_PREV_CHUNKS = 8
REL_CLIP = 128
N_REL = 2 * REL_CLIP + 1
NEG_INF = -1e30
ROPE_BASE = 10000.0
ATT_QB = 256
ATT_LOOKBACK = N_PREV_CHUNKS * CHUNK
ATT_WIN = ATT_LOOKBACK + ATT_QB
RET_TB = 256
TOKEN_TK = 2048
CONV_HALO = 16

ADAM_LR = 0.001
ADAM_B1 = 0.9
ADAM_B2 = 0.999
ADAM_EPS = 1e-08
ADAM_WD = 0.01
ADAM_STEP = 10

VMEM_LIMIT_BYTES = 56 * 1024 * 1024
MESH = pl.DeviceIdType.MESH


def _cparams(*sem):
    return pltpu.CompilerParams(dimension_semantics=sem, vmem_limit_bytes=VMEM_LIMIT_BYTES)


def _dot(a, b):
    return lax.dot_general(a, b, (((1,), (0,)), ((), ())), preferred_element_type=F32)


def _dot_nt(a, b):
    return lax.dot_general(a, b, (((1,), (1,)), ((), ())), preferred_element_type=F32)


def _dot_tn(a, b):
    return lax.dot_general(a, b, (((0,), (0,)), ((), ())), preferred_element_type=F32)


def _bf(v):
    return v.astype(BF16)


def _sigmoid(v):
    return 1.0 / (1.0 + jnp.exp(-v))


def _tile(n, want):
    if n <= want:
        return n
    for t in range(want - want % 128, 0, -128):
        if n % t == 0:
            return t
    t = want
    while n % t:
        t //= 2
    return t


def _my_position():
    x, y, c = lax.axis_index("x"), lax.axis_index("y"), lax.axis_index("c")
    return x, y, c, 4 * x + 2 * y + c


def _peer(x, y, c, k):
    px = 1 - x if k & 4 else x
    py = 1 - y if k & 2 else y
    pc = 1 - c if k & 1 else c
    return (px, py, pc), 4 * px + 2 * py + pc


def _exchange_sems():
    return [pltpu.SemaphoreType.DMA((N_DEV - 1,)), pltpu.SemaphoreType.DMA((N_DEV - 1,)), pltpu.SemaphoreType.DMA]


def _exchange_copy(kind, src_ref, dst_ref, sems, k, pos, incoming):
    x, y, c, me = pos
    peer, peer_id = _peer(x, y, c, k)
    if kind == "gather":
        src = src_ref
    else:
        src = src_ref.at[me if incoming else peer_id]
    return pltpu.make_async_remote_copy(src_ref=src, dst_ref=dst_ref.at[peer_id if incoming else me],
                                        send_sem=sems[0].at[k - 1], recv_sem=sems[1].at[k - 1],
                                        device_id=peer, device_id_type=MESH)


def _exchange_local(kind, src_ref, dst_ref, sems, me):
    return pltpu.make_async_copy(src_ref if kind == "gather" else src_ref.at[me], dst_ref.at[me], sems[2])


def _exchange_start(kind, src_ref, dst_ref, sems):
    pos = _my_position()
    _exchange_local(kind, src_ref, dst_ref, sems, pos[3]).start()
    for k in range(1, N_DEV):
        _exchange_copy(kind, src_ref, dst_ref, sems, k, pos, False).start()


def _exchange_wait(kind, src_ref, dst_ref, sems):
    pos = _my_position()
    for k in range(1, N_DEV):
        _exchange_copy(kind, src_ref, dst_ref, sems, k, pos, True).wait_recv()
    for k in range(1, N_DEV):
        _exchange_copy(kind, src_ref, dst_ref, sems, k, pos, False).wait_send()
    _exchange_local(kind, src_ref, dst_ref, sems, pos[3]).wait()


def _exchange_shape(kind, src):
    return jax.ShapeDtypeStruct((N_DEV,) + src.shape if kind == "gather" else src.shape, src.dtype)


def _exchange(kind, src, *, name):
    def body(src_ref, dst_ref, *sems):
        _exchange_start(kind, src_ref, dst_ref, sems)
        _exchange_wait(kind, src_ref, dst_ref, sems)

    return pl.pallas_call(
        body, name=name, in_specs=[pl.BlockSpec(memory_space=pl.ANY)], out_specs=pl.BlockSpec(memory_space=pl.ANY),
        out_shape=_exchange_shape(kind, src), scratch_shapes=_exchange_sems(),
    )(src)


def _pcall(body, args, *, name, grid, in_specs, out_specs, out_shape, scratch_shapes, sem, exchange=None):
    if exchange is None:
        outs = pl.pallas_call(body, name=name, grid=grid, in_specs=in_specs, out_specs=out_specs, out_shape=out_shape,
                              scratch_shapes=scratch_shapes, compiler_params=_cparams(*sem))(*args)
        return outs, None
    kind, src = exchange
    n_in, n_out, n_scr = len(in_specs), len(out_specs), len(scratch_shapes)
    hbm = pl.BlockSpec(memory_space=pl.ANY)

    def carrier(*refs):
        ins, src_ref = refs[:n_in], refs[n_in]
        outs, dst_ref = refs[n_in + 1:n_in + 1 + n_out], refs[n_in + 1 + n_out]
        scr = refs[n_in + 2 + n_out:n_in + 2 + n_out + n_scr]
        sems = refs[n_in + 2 + n_out + n_scr:]
        ids = [pl.program_id(a) for a in range(len(grid))]
        first = functools.reduce(jnp.logical_and, [i == 0 for i in ids])
        last = functools.reduce(jnp.logical_and, [i == g - 1 for i, g in zip(ids, grid)])

        @pl.when(first)
        def _():
            _exchange_start(kind, src_ref, dst_ref, sems)

        body(*ins, *outs, *scr)

        @pl.when(last)
        def _():
            _exchange_wait(kind, src_ref, dst_ref, sems)

    outs = pl.pallas_call(
        carrier, name=name, grid=grid, in_specs=list(in_specs) + [hbm], out_specs=list(out_specs) + [hbm],
        out_shape=list(out_shape) + [_exchange_shape(kind, src)],
        scratch_shapes=list(scratch_shapes) + _exchange_sems(),
        compiler_params=_cparams(*(["arbitrary"] * len(grid))))(*args, src)
    return outs[:-1], outs[-1]


def _matmul(a, b, *, dims, grid, a_spec, b_spec, o_spec, out_shape, acc_shape, name, scale=1.0,
            res=None, res_spec=None, exchange=None):
    nk = grid[3]
    has_res = res is not None

    def body(*refs):
        if has_res:
            a_ref, b_ref, r_ref, o_ref = refs[:4]
        else:
            a_ref, b_ref, o_ref = refs[:3]
            r_ref = None

        def finish(acc):
            if scale != 1.0:
                acc = acc * scale
            if r_ref is not None:
                acc = acc + r_ref[...].astype(F32)
            o_ref[...] = acc.astype(o_ref.dtype)

        part = lax.dot_general(_bf(a_ref[...]), _bf(b_ref[...]), (dims, ((), ())), preferred_element_type=F32)
        if nk == 1:
            finish(part)
        else:
            acc_ref = refs[-1]
            k = pl.program_id(3)

            @pl.when(k == 0)
            def _():
                acc_ref[...] = part

            @pl.when(jnp.logical_and(k > 0, k < nk - 1))
            def _():
                acc_ref[...] += part

            @pl.when(k == nk - 1)
            def _():
                finish(acc_ref[...] + part)

    in_specs = [a_spec, b_spec] + ([res_spec] if has_res else [])
    args = (a, b) + ((res,) if has_res else ())
    outs, moved = _pcall(
        body, args, name=name, grid=grid, in_specs=in_specs, out_specs=[o_spec], out_shape=[out_shape],
        scratch_shapes=[] if nk == 1 else [pltpu.VMEM(acc_shape, F32)],
        sem=("parallel", "parallel", "parallel", "arbitrary"), exchange=exchange)
    return outs[0] if exchange is None else (outs[0], moved)


NT = ((1,), (1,))
NN = ((1,), (0,))
TN = ((0,), (0,))


def _mm_rows(a, b, *, dims, out_dtype, name, res=None, scale=1.0, tm=1024, tn=1024, tk=1024):
    m, kdim = a.shape
    n = b.shape[1] if dims == NN else b.shape[0]
    tm, tn, tk = _tile(m, tm), _tile(n, tn), _tile(kdim, tk)
    grid = (1, m // tm, n // tn, kdim // tk)
    a_spec = pl.BlockSpec((tm, tk), lambda s, i, j, k: (i, k))
    if dims == NN:
        b_spec = pl.BlockSpec((tk, tn), lambda s, i, j, k: (k, j))
    else:
        b_spec = pl.BlockSpec((tn, tk), lambda s, i, j, k: (j, k))
    o_spec = pl.BlockSpec((tm, tn), lambda s, i, j, k: (i, j))
    return _matmul(a, b, dims=dims, grid=grid, a_spec=a_spec, b_spec=b_spec, o_spec=o_spec,
                   out_shape=jax.ShapeDtypeStruct((m, n), out_dtype), acc_shape=(tm, tn), name=name,
                   res=res, res_spec=o_spec if res is not None else None, scale=scale)


def _mm_to_slices(a, b_t, *, name, tm=1024, exchange=None):
    m, kdim = a.shape
    n = b_t.shape[0]
    tm = _tile(m, tm)
    w = BRANCH_W
    grid = (1, m // tm, n // w, 1)
    return _matmul(a, b_t, dims=NT, grid=grid,
                   a_spec=pl.BlockSpec((tm, kdim), lambda s, i, j, k: (i, 0)),
                   b_spec=pl.BlockSpec((w, kdim), lambda s, i, j, k: (j, 0)),
                   o_spec=pl.BlockSpec((None, tm, w), lambda s, i, j, k: (j, i, 0)),
                   out_shape=jax.ShapeDtypeStruct((n // w, m, w), BF16), acc_shape=(tm, w), name=name,
                   exchange=exchange)


def _inproj_dh(parts, w_t, res, *, name, tm=512):
    t, n = res.shape
    w = BRANCH_W
    tm = _tile(t, tm)
    counts = [p.shape[0] for p in parts]
    assert sum(counts) * w == w_t.shape[0]

    def body(*refs):
        w_ref, r_ref, o_ref = refs[len(parts):]
        cols = jnp.concatenate([p_ref[s] for p_ref, cnt in zip(refs, counts) for s in range(cnt)], axis=1)
        o_ref[...] = r_ref[...] + _dot(cols, w_ref[...])

    row = pl.BlockSpec((tm, n), lambda i: (i, 0))
    return pl.pallas_call(
        body, name=name, grid=(t // tm,),
        in_specs=[pl.BlockSpec((cnt, tm, w), lambda i: (0, i, 0)) for cnt in counts]
        + [pl.BlockSpec(w_t.shape, lambda i: (0, 0)), row],
        out_specs=row, out_shape=jax.ShapeDtypeStruct((t, n), F32), compiler_params=_cparams("parallel"),
    )(*parts, w_t, res)


def _mm_tn(a, b, *, name, tm=1408, tn=1024, tk=TOKEN_TK, exchange=None):
    t, m = a.shape
    n = b.shape[1]
    tm, tn, tk = _tile(m, tm), _tile(n, tn), _tile(t, tk)
    grid = (1, m // tm, n // tn, t // tk)
    return _matmul(a, b, dims=TN, grid=grid,
                   a_spec=pl.BlockSpec((tk, tm), lambda s, i, j, k: (k, i)),
                   b_spec=pl.BlockSpec((tk, tn), lambda s, i, j, k: (k, j)),
                   o_spec=pl.BlockSpec((tm, tn), lambda s, i, j, k: (i, j)),
                   out_shape=jax.ShapeDtypeStruct((m, n), F32), acc_shape=(tm, tn), name=name, exchange=exchange)


def _mm_tn_slices(a3, b, *, name, tn=1024, tk=TOKEN_TK):
    s_n, t, w = a3.shape
    n = b.shape[1]
    tn, tk = _tile(n, tn), _tile(t, tk)
    grid = (1, s_n, n // tn, t // tk)
    return _matmul(a3, b, dims=TN, grid=grid,
                   a_spec=pl.BlockSpec((None, tk, w), lambda s, i, j, k: (i, k, 0)),
                   b_spec=pl.BlockSpec((tk, tn), lambda s, i, j, k: (k, j)),
                   o_spec=pl.BlockSpec((w, tn), lambda s, i, j, k: (i, j)),
                   out_shape=jax.ShapeDtypeStruct((s_n * w, n), F32), acc_shape=(w, tn), name=name)


def _mm_tn_batch(a, b3, *, name, a_batched, tm=1024, tn=1024, tk=TOKEN_TK):
    s_n, t, n = b3.shape
    m = a.shape[-1]
    tm, tn, tk = _tile(m, tm), _tile(n, tn), _tile(t, tk)
    grid = (s_n, m // tm, n // tn, t // tk)
    if a_batched:
        a_spec = pl.BlockSpec((None, tk, tm), lambda s, i, j, k: (s, k, i))
    else:
        a_spec = pl.BlockSpec((tk, tm), lambda s, i, j, k: (k, i))
    return _matmul(a, b3, dims=TN, grid=grid, a_spec=a_spec,
                   b_spec=pl.BlockSpec((None, tk, tn), lambda s, i, j, k: (s, k, j)),
                   o_spec=pl.BlockSpec((None, tm, tn), lambda s, i, j, k: (s, i, j)),
                   out_shape=jax.ShapeDtypeStruct((s_n, m, n), F32), acc_shape=(tm, tn), name=name)


def _norm_parts(xf):
    r = lax.rsqrt(jnp.mean(xf * xf, axis=-1, keepdims=True) + EPS)
    return xf * r, r


def _norm_bwd(dh, xhat, r, w):
    dxhat = dh * w
    dx = r * (dxhat - xhat * jnp.mean(dxhat * xhat, axis=-1, keepdims=True))
    return dx, jnp.sum(dh * xhat, axis=0, keepdims=True)


def _rmsnorm_fwd(x, w, *, name, tm=1024):
    t, d = x.shape
    tm = _tile(t, tm)

    def body(x_ref, w_ref, h_ref):
        xhat, _ = _norm_parts(x_ref[...])
        h_ref[...] = _bf(xhat * w_ref[...])

    return pl.pallas_call(
        body, name=name, grid=(t // tm,),
        in_specs=[pl.BlockSpec((tm, d), lambda i: (i, 0)), pl.BlockSpec((1, d), lambda i: (0, 0))],
        out_specs=pl.BlockSpec((tm, d), lambda i: (i, 0)),
        out_shape=jax.ShapeDtypeStruct((t, d), BF16), compiler_params=_cparams("parallel"),
    )(x, w)


def _rmsnorm_bwd(dh, x, w, dres, *, name, tm=1024):
    t, d = x.shape
    tm = _tile(t, tm)

    def body(dh_ref, x_ref, w_ref, dres_ref, dx_ref, dw_ref):
        xhat, r = _norm_parts(x_ref[...])
        dx, dw = _norm_bwd(dh_ref[...], xhat, r, w_ref[...])
        dx_ref[...] = dres_ref[...] + dx

        @pl.when(pl.program_id(0) == 0)
        def _():
            dw_ref[...] = dw

        @pl.when(pl.program_id(0) > 0)
        def _():
            dw_ref[...] += dw

    row = pl.BlockSpec((tm, d), lambda i: (i, 0))
    vec = pl.BlockSpec((1, d), lambda i: (0, 0))
    return pl.pallas_call(
        body, name=name, grid=(t // tm,), in_specs=[row, row, vec, row], out_specs=[row, vec],
        out_shape=[jax.ShapeDtypeStruct((t, d), F32), jax.ShapeDtypeStruct((1, d), F32)],
        compiler_params=_cparams("arbitrary"),
    )(dh, x, w, dres)


def _loss_fwd_bwd(x, w, target, *, name, tm=1024):
    t, d = x.shape
    tm = _tile(t, tm)

    def body(x_ref, w_ref, t_ref, dx_ref, dw_ref, loss_ref):
        xhat, r = _norm_parts(x_ref[...])
        wv = w_ref[...]
        err = xhat * wv - t_ref[...]
        dx, dw = _norm_bwd(err * (1.0 / d), xhat, r, wv)
        dx_ref[...] = dx
        part = jnp.full((1, 128), 0.5 / d, F32) * jnp.sum(err * err)

        @pl.when(pl.program_id(0) == 0)
        def _():
            dw_ref[...] = dw
            loss_ref[...] = part

        @pl.when(pl.program_id(0) > 0)
        def _():
            dw_ref[...] += dw
            loss_ref[...] += part

    row = pl.BlockSpec((tm, d), lambda i: (i, 0))
    vec = pl.BlockSpec((1, d), lambda i: (0, 0))
    return pl.pallas_call(
        body, name=name, grid=(t // tm,), in_specs=[row, vec, row],
        out_specs=[row, vec, pl.BlockSpec((1, 128), lambda i: (0, 0))],
        out_shape=[jax.ShapeDtypeStruct((t, d), F32), jax.ShapeDtypeStruct((1, d), F32),
                   jax.ShapeDtypeStruct((1, 128), F32)],
        compiler_params=_cparams("arbitrary"),
    )(x, w, target)


def _ffn_tiles(t, f):
    tf = f
    for cand in (1408, 1024, 512, 256, 128):
        if f % cand == 0:
            tf = cand
            break
    return _tile(t, 512), tf


def _ffn_fwd(x, nw, wg_t, wu_t, wd, *, name, exchange=None):
    t, d = x.shape
    f = wd.shape[0]
    tm, tf = _ffn_tiles(t, f)
    nf = f // tf

    def body(x_ref, nw_ref, wg_ref, wu_ref, wd_ref, xo_ref, h_ref, g_ref, u_ref, a_ref, hs_ref, acc_ref):
        j = pl.program_id(1)

        @pl.when(j == 0)
        def _():
            xhat, _ = _norm_parts(x_ref[...])
            hb = _bf(xhat * nw_ref[...])
            hs_ref[...] = hb
            h_ref[...] = hb

        hb = hs_ref[...]
        g = _dot_nt(hb, wg_ref[...])
        u = _dot_nt(hb, wu_ref[...])
        a = _bf(g * _sigmoid(g) * u)
        g_ref[...] = _bf(g)
        u_ref[...] = _bf(u)
        a_ref[...] = a
        part = _dot(a, wd_ref[...])

        @pl.when(j == 0)
        def _():
            acc_ref[...] = part

        @pl.when(j > 0)
        def _():
            acc_ref[...] += part

        @pl.when(j == nf - 1)
        def _():
            xo_ref[...] = x_ref[...] + 0.5 * acc_ref[...]

    row = pl.BlockSpec((tm, d), lambda i, j: (i, 0))
    wspec = pl.BlockSpec((tf, d), lambda i, j: (j, 0))
    hid = pl.BlockSpec((tm, tf), lambda i, j: (i, j))
    return _pcall(
        body, (x, nw, wg_t, wu_t, wd), name=name, grid=(t // tm, nf),
        in_specs=[row, pl.BlockSpec((1, d), lambda i, j: (0, 0)), wspec, wspec, wspec],
        out_specs=[row, row, hid, hid, hid],
        out_shape=[jax.ShapeDtypeStruct((t, d), F32), jax.ShapeDtypeStruct((t, d), BF16)]
        + [jax.ShapeDtypeStruct((t, f), BF16)] * 3,
        scratch_shapes=[pltpu.VMEM((tm, d), BF16), pltpu.VMEM((tm, d), F32)],
        sem=("parallel", "arbitrary"), exchange=exchange)


def _ffn_bwd(dxo, x, nw, g, u, wg_t, wu_t, wd, *, name, exchange=None):
    t, d = x.shape
    f = wd.shape[0]
    tm, tf = _ffn_tiles(t, f)
    nf = f // tf

    def body(dxo_ref, x_ref, nw_ref, g_ref, u_ref, wg_ref, wu_ref, wd_ref,
             dx_ref, dy_ref, dg_ref, du_ref, dnw_ref, dys_ref, acc_ref):
        i, j = pl.program_id(0), pl.program_id(1)

        @pl.when(j == 0)
        def _():
            dyb = _bf(0.5 * dxo_ref[...])
            dys_ref[...] = dyb
            dy_ref[...] = dyb

        da = _dot_nt(dys_ref[...], wd_ref[...])
        gv = g_ref[...].astype(F32)
        uv = u_ref[...].astype(F32)
        s = _sigmoid(gv)
        dg = _bf(da * uv * (s * (1.0 + gv * (1.0 - s))))
        du = _bf(da * (gv * s))
        dg_ref[...] = dg
        du_ref[...] = du
        part = _dot(dg, wg_ref[...]) + _dot(du, wu_ref[...])

        @pl.when(j == 0)
        def _():
            acc_ref[...] = part

        @pl.when(j > 0)
        def _():
            acc_ref[...] += part

        @pl.when(j == nf - 1)
        def _():
            xhat, r = _norm_parts(x_ref[...])
            dx, dw = _norm_bwd(acc_ref[...], xhat, r, nw_ref[...])
            dx_ref[...] = dxo_ref[...] + dx

            @pl.when(i == 0)
            def _():
                dnw_ref[...] = dw

            @pl.when(i > 0)
            def _():
                dnw_ref[...] += dw

    row = pl.BlockSpec((tm, d), lambda i, j: (i, 0))
    vec = pl.BlockSpec((1, d), lambda i, j: (0, 0))
    wspec = pl.BlockSpec((tf, d), lambda i, j: (j, 0))
    hid = pl.BlockSpec((tm, tf), lambda i, j: (i, j))
    return _pcall(
        body, (dxo, x, nw, g, u, wg_t, wu_t, wd), name=name, grid=(t // tm, nf),
        in_specs=[row, row, vec, hid, hid, wspec, wspec, wspec],
        out_specs=[row, row, hid, hid, vec],
        out_shape=[jax.ShapeDtypeStruct((t, d), F32), jax.ShapeDtypeStruct((t, d), BF16),
                   jax.ShapeDtypeStruct((t, f), BF16), jax.ShapeDtypeStruct((t, f), BF16),
                   jax.ShapeDtypeStruct((1, d), F32)],
        scratch_shapes=[pltpu.VMEM((tm, d), BF16), pltpu.VMEM((tm, d), F32)],
        sem=("arbitrary", "arbitrary"), exchange=exchange)


def _shift_down(prev, cur, n):
    ext = jnp.concatenate([prev, cur], axis=0)
    return pltpu.roll(ext, n, axis=0)[prev.shape[0]:]


def _shift_up(cur, nxt, n):
    ext = jnp.concatenate([cur, nxt], axis=0)
    return pltpu.roll(ext, ext.shape[0] - n, axis=0)[:cur.shape[0]]


def _conv_specs(t, tb):
    hb = tb // CONV_HALO
    last = t // CONV_HALO - 1

    def tile(s):
        return pl.BlockSpec((None, tb, 128), lambda c, i: (s, i, c))

    def prev(s):
        return pl.BlockSpec((None, CONV_HALO, 128), lambda c, i: (s, jnp.maximum(i * hb - 1, 0), c))

    def nxt(s):
        return pl.BlockSpec((None, CONV_HALO, 128), lambda c, i: (s, jnp.minimum((i + 1) * hb, last), c))

    return tile, prev, nxt


def _conv_fwd(cols3, conv_w, *, name, tb=1024):
    _, t, bw = cols3.shape
    tb = _tile(t, tb)
    tile, prev, _ = _conv_specs(t, tb)

    def body(u_ref, b_ref, c_ref, up_ref, cp_ref, w_ref, y_ref):
        first = pl.program_id(1) == 0
        z = c_ref[...].astype(F32) * u_ref[...].astype(F32)
        zp = jnp.where(first, 0.0, cp_ref[...].astype(F32) * up_ref[...].astype(F32))
        conv = w_ref[0:1, :] * _shift_down(zp, z, 2) + w_ref[1:2, :] * _shift_down(zp, z, 1) + w_ref[2:3, :] * z
        y_ref[...] = _bf(b_ref[...].astype(F32) * conv)

    return pl.pallas_call(
        body, name=name, grid=(bw // 128, t // tb),
        in_specs=[tile(0), tile(1), tile(2), prev(0), prev(2), pl.BlockSpec((3, 128), lambda c, i: (0, c))],
        out_specs=pl.BlockSpec((tb, 128), lambda c, i: (i, c)),
        out_shape=jax.ShapeDtypeStruct((t, bw), BF16), compiler_params=_cparams("parallel", "parallel"),
    )(cols3, cols3, cols3, cols3, cols3, conv_w)


def _conv_bwd(cols3, conv_w, dy, *, name, tb=1024):
    _, t, bw = cols3.shape
    tb = _tile(t, tb)
    nt = t // tb
    tile, prev, nxt = _conv_specs(t, tb)
    hb = tb // CONV_HALO
    last = t // CONV_HALO - 1

    def body(u_ref, b_ref, c_ref, up_ref, cp_ref, bn_ref, dy_ref, dyn_ref, w_ref, d3_ref, dw_ref):
        i = pl.program_id(1)
        uv, bv, cv = u_ref[...].astype(F32), b_ref[...].astype(F32), c_ref[...].astype(F32)
        dyv = dy_ref[...].astype(F32)
        z = cv * uv
        zp = jnp.where(i == 0, 0.0, cp_ref[...].astype(F32) * up_ref[...].astype(F32))
        z1, z2 = _shift_down(zp, z, 1), _shift_down(zp, z, 2)
        w0, w1, w2 = w_ref[0:1, :], w_ref[1:2, :], w_ref[2:3, :]
        conv = w0 * z2 + w1 * z1 + w2 * z
        dconv = dyv * bv
        dconv_n = jnp.where(i == nt - 1, 0.0, dyn_ref[...].astype(F32) * bn_ref[...].astype(F32))
        dz = w2 * dconv + w1 * _shift_up(dconv, dconv_n, 1) + w0 * _shift_up(dconv, dconv_n, 2)
        d3_ref[0] = _bf(dz * cv)
        d3_ref[1] = _bf(dyv * conv)
        d3_ref[2] = _bf(dz * uv)
        dws = [jnp.sum(dconv * zz, axis=0, keepdims=True) for zz in (z2, z1, z)]

        @pl.when(i == 0)
        def _():
            for j in range(3):
                dw_ref[j:j + 1, :] = dws[j]

        @pl.when(i > 0)
        def _():
            for j in range(3):
                dw_ref[j:j + 1, :] += dws[j]

    dy_tile = pl.BlockSpec((tb, 128), lambda c, i: (i, c))
    dy_next = pl.BlockSpec((CONV_HALO, 128), lambda c, i: (jnp.minimum((i + 1) * hb, last), c))
    wspec = pl.BlockSpec((3, 128), lambda c, i: (0, c))
    return pl.pallas_call(
        body, name=name, grid=(bw // 128, nt),
        in_specs=[tile(0), tile(1), tile(2), prev(0), prev(2), nxt(1), dy_tile, dy_next, wspec],
        out_specs=[pl.BlockSpec((3, tb, 128), lambda c, i: (0, i, c)), wspec],
        out_shape=[jax.ShapeDtypeStruct((3, t, bw), BF16), jax.ShapeDtypeStruct((3, bw), F32)],
        compiler_params=_cparams("parallel", "arbitrary"),
    )(cols3, cols3, cols3, cols3, cols3, cols3, dy, dy, conv_w)


def _ret_consts():
    log_gamma = jnp.log1p(-jnp.exp2(-5.0 - jnp.arange(H_RET, dtype=F32)))
    pos = jnp.arange(CHUNK, dtype=F32)
    d_intra = jnp.exp(log_gamma[:, None, None] * jnp.abs(pos[:, None] - pos[None, :]))
    q_decay = jnp.exp(log_gamma[:, None] * (pos + 1.0))
    k_decay = jnp.exp(log_gamma[:, None] * (CHUNK - 1.0 - pos))
    chunk_decay = jnp.exp(log_gamma * CHUNK)
    wide = (H_RET, CHUNK, DK_RET)
    return (d_intra, jnp.broadcast_to(q_decay[:, :, None], wide), jnp.broadcast_to(k_decay[:, :, None], wide),
            jnp.broadcast_to(chunk_decay[:, None, None], (H_RET, 1, DK_RET)))


def _rope_tables(t):
    inv_freq = ROPE_BASE ** (-jnp.linspace(0.0, 1.0, DK_RET // 2, dtype=F32))
    ang = jnp.arange(t, dtype=F32)[:, None] * inv_freq[None, :]
    cos, sin = jnp.cos(ang), jnp.sin(ang)
    return jnp.concatenate([cos, cos], axis=1), jnp.concatenate([-sin, sin], axis=1)


def _rope(v, cc, ss):
    return v * cc + pltpu.roll(v, DK_RET // 2, axis=1) * ss


def _ret_in_specs(tb, blk):
    def col(s):
        return pl.BlockSpec((None, tb, H_RET * DK_RET), lambda i: (s, blk(i), 0))

    tab = pl.BlockSpec((tb, DK_RET), lambda i: (blk(i), 0))
    return ([col(3), col(4), col(5), col(6), tab, tab,
             pl.BlockSpec((H_RET, CHUNK, CHUNK), lambda i: (0, 0, 0)),
             pl.BlockSpec((H_RET, CHUNK, DK_RET), lambda i: (0, 0, 0)),
             pl.BlockSpec((H_RET, CHUNK, DK_RET), lambda i: (0, 0, 0)),
             pl.BlockSpec((H_RET, 1, DK_RET), lambda i: (0, 0, 0))])


def _ret_fwd(cols3, tables, consts, *, name):
    _, t, bw = cols3.shape
    tb = _tile(t, RET_TB)
    ncb = tb // CHUNK
    scale = DK_RET ** -0.5

    def body(q_ref, k_ref, v_ref, g_ref, cc_ref, ss_ref, di_ref, qd_ref, kd_ref, cd_ref, y_ref, st_ref, state):
        @pl.when(pl.program_id(0) == 0)
        def _():
            state[...] = jnp.zeros_like(state)

        for c in range(ncb):
            rows = pl.ds(c * CHUNK, CHUNK)
            cc, ss = cc_ref[rows, :], ss_ref[rows, :]
            for h in range(H_RET):
                lanes = pl.ds(h * DK_RET, DK_RET)
                qs = _rope(q_ref[rows, lanes].astype(F32), cc, ss) * scale
                kr = _rope(k_ref[rows, lanes].astype(F32), cc, ss)
                vb = v_ref[rows, lanes]
                gv = g_ref[rows, lanes].astype(F32)
                s_in = state[h]
                st_ref[h, c] = s_in
                inner = _dot_nt(_bf(qs), _bf(kr)) * di_ref[h]
                o = _dot(_bf(inner), vb) + _dot(_bf(qs * qd_ref[h]), _bf(s_in))
                state[h] = s_in * cd_ref[h] + _dot_tn(_bf(kr * kd_ref[h]), vb)
                on = o * lax.rsqrt(jnp.mean(o * o, axis=-1, keepdims=True) + EPS)
                y_ref[rows, lanes] = _bf(gv * _sigmoid(gv) * on)

    return pl.pallas_call(
        body, name=name, grid=(t // tb,),
        in_specs=_ret_in_specs(tb, lambda i: i),
        out_specs=[pl.BlockSpec((tb, bw), lambda i: (i, 0)),
                   pl.BlockSpec((H_RET, ncb, DK_RET, DK_RET), lambda i: (0, i, 0, 0))],
        out_shape=[jax.ShapeDtypeStruct((t, bw), BF16),
                   jax.ShapeDtypeStruct((H_RET, t // CHUNK, DK_RET, DK_RET), F32)],
        scratch_shapes=[pltpu.VMEM((H_RET, DK_RET, DK_RET), F32)],
        compiler_params=_cparams("arbitrary"),
    )(cols3, cols3, cols3, cols3, tables[0], tables[1], *consts)


def _ret_bwd(cols3, tables, consts, states, dy, *, name, exchange=None):
    _, t, bw = cols3.shape
    tb = _tile(t, RET_TB)
    ncb = tb // CHUNK
    nb = t // tb
    scale = DK_RET ** -0.5

    def body(q_ref, k_ref, v_ref, g_ref, cc_ref, ss_ref, di_ref, qd_ref, kd_ref, cd_ref, st_ref, dy_ref,
             d4_ref, dstate):
        @pl.when(pl.program_id(0) == 0)
        def _():
            dstate[...] = jnp.zeros_like(dstate)

        heads = range(H_RET)
        lanes_of = [pl.ds(h * DK_RET, DK_RET) for h in heads]
        di, qd, kd = [di_ref[h] for h in heads], [qd_ref[h] for h in heads], [kd_ref[h] for h in heads]
        carried = [dstate[h] for h in heads]
        for c in reversed(range(ncb)):
            rows = pl.ds(c * CHUNK, CHUNK)
            cc, ss = cc_ref[rows, :], ss_ref[rows, :]
            vb = [v_ref[rows, ln] for ln in lanes_of]
            gv = [g_ref[rows, ln].astype(F32) for ln in lanes_of]
            dyv = [dy_ref[rows, ln].astype(F32) for ln in lanes_of]
            s_in = [_bf(st_ref[h, c]) for h in heads]
            qs = [_rope(q_ref[rows, ln].astype(F32), cc, ss) * scale for ln in lanes_of]
            kr = [_rope(k_ref[rows, ln].astype(F32), cc, ss) for ln in lanes_of]
            qsb, krb = [_bf(v) for v in qs], [_bf(v) for v in kr]
            qdb = [_bf(qs[h] * qd[h]) for h in heads]
            kdb = [_bf(kr[h] * kd[h]) for h in heads]
            innerb = [_bf(_dot_nt(qsb[h], krb[h]) * di[h]) for h in heads]
            o = [_dot(innerb[h], vb[h]) + _dot(qdb[h], s_in[h]) for h in heads]
            r = [lax.rsqrt(jnp.mean(v * v, axis=-1, keepdims=True) + EPS) for v in o]
            on = [o[h] * r[h] for h in heads]
            sg = [_sigmoid(v) for v in gv]
            dgv = [_bf(dyv[h] * on[h] * (sg[h] * (1.0 + gv[h] * (1.0 - sg[h])))) for h in heads]
            don = [dyv[h] * (gv[h] * sg[h]) for h in heads]
            dob = [_bf(r[h] * (don[h] - on[h] * jnp.mean(don[h] * on[h], axis=-1, keepdims=True))) for h in heads]
            dinner = [_bf(_dot_nt(dob[h], vb[h]) * di[h]) for h in heads]
            dqs = [_dot(dinner[h], krb[h]) + _dot_nt(dob[h], s_in[h]) * qd[h] for h in heads]
            dkr = [_dot_tn(dinner[h], qsb[h]) for h in heads]
            dv = [_dot_tn(innerb[h], dob[h]) for h in heads]
            dnew = [_dot_tn(qdb[h], dob[h]) for h in heads]
            dstb = [_bf(v) for v in carried]
            dv = [dv[h] + _dot(kdb[h], dstb[h]) for h in heads]
            dkr = [dkr[h] + _dot_nt(vb[h], dstb[h]) * kd[h] for h in heads]
            carried = [carried[h] * cd_ref[h] + dnew[h] for h in heads]
            for h, ln in enumerate(lanes_of):
                d4_ref[0, rows, ln] = _bf(_rope_bwd(dqs[h] * scale, cc, ss))
                d4_ref[1, rows, ln] = _bf(_rope_bwd(dkr[h], cc, ss))
                d4_ref[2, rows, ln] = _bf(dv[h])
                d4_ref[3, rows, ln] = dgv[h]
        for h in heads:
            dstate[h] = carried[h]

    rev = lambda i: nb - 1 - i
    outs, moved = _pcall(
        body, (cols3, cols3, cols3, cols3, tables[0], tables[1], *consts, states, dy), name=name, grid=(nb,),
        in_specs=_ret_in_specs(tb, rev)
        + [pl.BlockSpec((H_RET, ncb, DK_RET, DK_RET), lambda i: (0, rev(i), 0, 0)),
           pl.BlockSpec((tb, bw), lambda i: (rev(i), 0))],
        out_specs=[pl.BlockSpec((4, tb, bw), lambda i: (0, rev(i), 0))],
        out_shape=[jax.ShapeDtypeStruct((4, t, bw), BF16)],
        scratch_shapes=[pltpu.VMEM((H_RET, DK_RET, DK_RET), F32)],
        sem=("arbitrary",), exchange=exchange)
    return outs[0], moved


def _rope_bwd(dv, cc, ss):
    return dv * cc + pltpu.roll(dv * ss, DK_RET // 2, axis=1)


def _att_window(i):
    return pl.multiple_of(jnp.maximum(i - ATT_LOOKBACK // ATT_QB, 0) * ATT_QB, ATT_QB)


def _att_mask(i, ws):
    qchunk = (i * ATT_QB + lax.broadcasted_iota(jnp.int32, (ATT_QB, ATT_WIN), 0)) // CHUNK
    kchunk = (ws + lax.broadcasted_iota(jnp.int32, (ATT_QB, ATT_WIN), 1)) // CHUNK
    return (kchunk <= qchunk) & (kchunk >= qchunk - N_PREV_CHUNKS)


def _att_fwd(cols3, bias3, *, name, exchange=None):
    _, t, bw = cols3.shape
    assert t % ATT_QB == 0 and t >= ATT_WIN
    scale = DH_ATT ** -0.5
    nvar = ATT_LOOKBACK // ATT_QB

    def body(q_ref, k_ref, v_ref, b_ref, y_ref, lse_ref):
        i = pl.program_id(1)
        ws = _att_window(i)
        valid = _att_mask(i, ws)
        q = q_ref[...].astype(F32)
        kw = k_ref[pl.ds(ws, ATT_WIN), :]
        vw = v_ref[pl.ds(ws, ATT_WIN), :]
        lane_head = lax.broadcasted_iota(jnp.int32, (ATT_QB, 128), 1) // DH_ATT
        out = jnp.zeros((ATT_QB, 128), F32)
        lse = jnp.zeros((ATT_QB, 128), F32)
        for hh in range(2):
            mine = lane_head == hh
            s = _dot_nt(_bf(jnp.where(mine, q, 0.0)), kw) * scale + b_ref[hh]
            s = jnp.where(valid, s, NEG_INF)
            mx = jnp.max(s, axis=-1, keepdims=True)
            p = jnp.exp(s - mx)
            l = jnp.sum(p, axis=-1, keepdims=True)
            out = jnp.where(mine, _dot(_bf(p), vw) / l, out)
            lse = jnp.where(mine, mx + jnp.log(l), lse)
        y_ref[...] = _bf(out)
        lse_ref[...] = lse

    kv = lambda s: pl.BlockSpec((None, t, 128), lambda hp, i: (s, 0, hp))
    return _pcall(
        body, (cols3, cols3, cols3, bias3), name=name, grid=(H_ATT // 2, t // ATT_QB),
        in_specs=[pl.BlockSpec((None, ATT_QB, 128), lambda hp, i: (7, i, hp)), kv(8), kv(9),
                  pl.BlockSpec((None, 2, ATT_QB, ATT_WIN), lambda hp, i: (jnp.minimum(i, nvar), hp, 0, 0))],
        out_specs=[pl.BlockSpec((ATT_QB, 128), lambda hp, i: (i, hp)),
                   pl.BlockSpec((None, ATT_QB, 128), lambda hp, i: (hp, i, 0))],
        out_shape=[jax.ShapeDtypeStruct((t, bw), BF16), jax.ShapeDtypeStruct((H_ATT // 2, t, 128), F32)],
        scratch_shapes=[], sem=("parallel", "arbitrary"), exchange=exchange)


def _att_bwd(cols3, bias3, y, lse, dy, *, name, exchange=None):
    _, t, bw = cols3.shape
    nq = t // ATT_QB
    scale = DH_ATT ** -0.5
    nvar = ATT_LOOKBACK // ATT_QB

    def body(q_ref, k_ref, v_ref, b_ref, y_ref, lse_ref, dy_ref, d3_ref, db_ref, dk_acc, dv_acc):
        i = pl.program_id(1)

        @pl.when(i == 0)
        def _():
            dk_acc[...] = jnp.zeros_like(dk_acc)
            dv_acc[...] = jnp.zeros_like(dv_acc)

        ws = _att_window(i)
        valid = _att_mask(i, ws)
        q = q_ref[...].astype(F32)
        kw = k_ref[pl.ds(ws, ATT_WIN), :]
        vw = v_ref[pl.ds(ws, ATT_WIN), :]
        do = dy_ref[...].astype(F32)
        dof = do * y_ref[...].astype(F32)
        lsev = lse_ref[...]
        lane_head = lax.broadcasted_iota(jnp.int32, (ATT_QB, 128), 1) // DH_ATT
        dq = jnp.zeros((ATT_QB, 128), F32)
        dk = jnp.zeros((ATT_WIN, 128), F32)
        dv = jnp.zeros((ATT_WIN, 128), F32)
        first = i <= nvar
        for hh in range(2):
            mine = lane_head == hh
            qh = _bf(jnp.where(mine, q, 0.0))
            doh = _bf(jnp.where(mine, do, 0.0))
            s = _dot_nt(qh, kw) * scale + b_ref[hh]
            lse_h = jnp.max(jnp.where(mine, lsev, NEG_INF), axis=-1, keepdims=True)
            p = jnp.where(valid, jnp.exp(s - lse_h), 0.0)
            delta = jnp.sum(jnp.where(mine, dof, 0.0), axis=-1, keepdims=True)
            ds = p * (_dot_nt(doh, vw) - delta)

            @pl.when(first)
            def _():
                db_ref[hh] = ds

            @pl.when(jnp.logical_not(first))
            def _():
                db_ref[hh] += ds

            dsb = _bf(ds * scale)
            dq = jnp.where(mine, _dot(dsb, kw), dq)
            dk = dk + _dot_tn(dsb, qh)
            dv = dv + _dot_tn(_bf(p), doh)
        d3_ref[0, pl.ds(pl.multiple_of(i * ATT_QB, ATT_QB), ATT_QB), :] = _bf(dq)
        dk_acc[pl.ds(ws, ATT_WIN), :] += dk
        dv_acc[pl.ds(ws, ATT_WIN), :] += dv

        @pl.when(i == nq - 1)
        def _():
            d3_ref[1] = _bf(dk_acc[...])
            d3_ref[2] = _bf(dv_acc[...])

    kv = lambda s: pl.BlockSpec((None, t, 128), lambda hp, i: (s, 0, hp))
    qrow = pl.BlockSpec((ATT_QB, 128), lambda hp, i: (i, hp))
    btile = pl.BlockSpec((None, 2, ATT_QB, ATT_WIN), lambda hp, i: (jnp.minimum(i, nvar), hp, 0, 0))
    return _pcall(
        body, (cols3, cols3, cols3, bias3, y, lse, dy), name=name, grid=(H_ATT // 2, nq),
        in_specs=[pl.BlockSpec((None, ATT_QB, 128), lambda hp, i: (7, i, hp)), kv(8), kv(9), btile, qrow,
                  pl.BlockSpec((None, ATT_QB, 128), lambda hp, i: (hp, i, 0)), qrow],
        out_specs=[pl.BlockSpec((3, t, 128), lambda hp, i: (0, 0, hp)), btile],
        out_shape=[jax.ShapeDtypeStruct((3, t, bw), BF16),
                   jax.ShapeDtypeStruct((nvar + 1, H_ATT, ATT_QB, ATT_WIN), F32)],
        scratch_shapes=[pltpu.VMEM((t, 128), F32), pltpu.VMEM((t, 128), F32)],
        sem=("parallel", "arbitrary"), exchange=exchange)


SKEW_W = ATT_WIN + ATT_QB
REL_PAD = 384


def _rel_onehot(v):
    r = lax.broadcasted_iota(jnp.int32, (REL_PAD, SKEW_W), 0)
    j = lax.broadcasted_iota(jnp.int32, (REL_PAD, SKEW_W), 1)
    dist = jnp.where(j < ATT_WIN, v * ATT_QB - j, v * ATT_QB + SKEW_W - j)
    col = jnp.clip(dist, -REL_CLIP, REL_CLIP) + REL_CLIP
    return _bf(jnp.where(col == r, 1.0, 0.0))


def _split3(v):
    hi = _bf(v)
    rest = v - hi.astype(F32)
    mid = _bf(rest)
    return hi, mid, _bf(rest - mid.astype(F32))


def _skew_rows(a, forward):
    row = lax.broadcasted_iota(jnp.int32, a.shape, 0)
    for b in range(ATT_QB.bit_length() - 1):
        shift = (1 << b) if forward else SKEW_W - (1 << b)
        a = jnp.where(((row >> b) & 1) == 1, pltpu.roll(a, shift, axis=1), a)
    return a


def _bias_tiles(rel_bias, *, name, exchange=None):
    depth = rel_bias.shape[0]
    nvar = ATT_LOOKBACK // ATT_QB + 1
    rel = jnp.pad(rel_bias, ((0, 0), (0, 0), (0, REL_PAD - N_REL)))

    def body(rel_ref, o_ref, ext_ref):
        onehot = _rel_onehot(pl.program_id(1))
        ext_ref[...] = sum(_dot(part, onehot) for part in _split3(rel_ref[...]))
        for h in range(H_ATT):
            tile = _skew_rows(jnp.broadcast_to(ext_ref[h:h + 1, :], (ATT_QB, SKEW_W)), True)
            o_ref[h] = tile[:, :ATT_WIN]

    outs, moved = _pcall(
        body, (rel,), name=name, grid=(depth, nvar),
        in_specs=[pl.BlockSpec((None, H_ATT, REL_PAD), lambda l, v: (l, 0, 0))],
        out_specs=[pl.BlockSpec((None, None, H_ATT, ATT_QB, ATT_WIN), lambda l, v: (l, v, 0, 0, 0))],
        out_shape=[jax.ShapeDtypeStruct((depth, nvar, H_ATT, ATT_QB, ATT_WIN), F32)],
        scratch_shapes=[pltpu.VMEM((H_ATT, SKEW_W), F32)], sem=("parallel", "parallel"), exchange=exchange)
    return outs[0], moved


def _rel_bias_grad(dbias3, *, name):
    nvar = dbias3.shape[0]

    def body(db_ref, o_ref, diag_ref):
        v = pl.program_id(0)
        for h in range(H_ATT):
            tile = jnp.concatenate([db_ref[h], jnp.zeros((ATT_QB, ATT_QB), F32)], axis=1)
            diag_ref[h:h + 1, :] = jnp.sum(_skew_rows(tile, False), axis=0, keepdims=True)
        onehot = _rel_onehot(v)
        part = sum(_dot_nt(p, onehot) for p in _split3(diag_ref[...]))

        @pl.when(v == 0)
        def _():
            o_ref[...] = part

        @pl.when(v > 0)
        def _():
            o_ref[...] += part

    out = pl.pallas_call(
        body, name=name, grid=(nvar,),
        in_specs=[pl.BlockSpec((None, H_ATT, ATT_QB, ATT_WIN), lambda v: (v, 0, 0, 0))],
        out_specs=pl.BlockSpec((H_ATT, REL_PAD), lambda v: (0, 0)),
        out_shape=jax.ShapeDtypeStruct((H_ATT, REL_PAD), F32),
        scratch_shapes=[pltpu.VMEM((H_ATT, SKEW_W), F32)], compiler_params=_cparams("arbitrary"),
    )(dbias3)
    return out[:, :N_REL]


def _merge_fwd(h, ys, wmg, wb_t, *, name, tm=512):
    t, d = h.shape
    bw = ys[0].shape[1]
    tm = _tile(t, tm)

    def body(h_ref, y0_ref, y1_ref, y2_ref, wg_ref, wb_ref, m_ref, s_ref, p_ref):
        hv = h_ref[...]
        total = jnp.zeros((tm, d), F32)
        for b, y_ref in enumerate((y0_ref, y1_ref, y2_ref)):
            s = _sigmoid(_dot(hv, wg_ref[b]))
            p = _dot_nt(y_ref[...], wb_ref[b])
            s_ref[b] = _bf(s)
            p_ref[b] = _bf(p)
            total = total + s * p
        m_ref[...] = _bf(total)

    row = pl.BlockSpec((tm, d), lambda i: (i, 0))
    yrow = pl.BlockSpec((tm, bw), lambda i: (i, 0))
    three = pl.BlockSpec((3, tm, d), lambda i: (0, i, 0))
    return pl.pallas_call(
        body, name=name, grid=(t // tm,),
        in_specs=[row, yrow, yrow, yrow, pl.BlockSpec((3, d, d), lambda i: (0, 0, 0)),
                  pl.BlockSpec((3, d, bw), lambda i: (0, 0, 0))],
        out_specs=[row, three, three],
        out_shape=[jax.ShapeDtypeStruct((t, d), BF16), jax.ShapeDtypeStruct((3, t, d), BF16),
                   jax.ShapeDtypeStruct((3, t, d), BF16)],
        compiler_params=_cparams("parallel"),
    )(h, *ys, wmg, wb_t)


def _merge_bwd(dm, s3, p3, wmg, wb_t, *, name, tm=512):
    _, t, d = s3.shape
    bw = wb_t.shape[2]
    tm = _tile(t, tm)

    def body(dm_ref, s_ref, p_ref, wg_ref, wb_ref, dgp_ref, dp_ref, dy_ref, dh_ref):
        dmv = dm_ref[...].astype(F32)
        dh = jnp.zeros((tm, d), F32)
        for b in range(3):
            s = s_ref[b].astype(F32)
            dgp = _bf(dmv * p_ref[b].astype(F32) * s * (1.0 - s))
            dp = _bf(dmv * s)
            dgp_ref[b] = dgp
            dp_ref[b] = dp
            dy_ref[b] = _bf(_dot(dp, wb_ref[b]))
            dh = dh + _dot_nt(dgp, wg_ref[b])
        dh_ref[...] = dh

    row = pl.BlockSpec((tm, d), lambda i: (i, 0))
    three = pl.BlockSpec((3, tm, d), lambda i: (0, i, 0))
    return pl.pallas_call(
        body, name=name, grid=(t // tm,),
        in_specs=[row, three, three, pl.BlockSpec((3, d, d), lambda i: (0, 0, 0)),
                  pl.BlockSpec((3, d, bw), lambda i: (0, 0, 0))],
        out_specs=[three, three, pl.BlockSpec((3, tm, bw), lambda i: (0, i, 0)), row],
        out_shape=[jax.ShapeDtypeStruct((3, t, d), BF16), jax.ShapeDtypeStruct((3, t, d), BF16),
                   jax.ShapeDtypeStruct((3, t, bw), BF16), jax.ShapeDtypeStruct((t, d), F32)],
        compiler_params=_cparams("parallel"),
    )(dm, s3, p3, wmg, wb_t)


def _carried(hooks, slot, state=None):
    if not hooks or slot not in hooks:
        return None, lambda buf: None
    make, done = hooks[slot]
    return make(state), done


def _layer_fwd(x, p, aux, tag, hooks=None):
    ex, done = _carried(hooks, "ffn1")
    (x1, h1, g1, u1, a1), buf = _ffn_fwd(x, p["n1"], p["wg1"], p["wu1"], p["wd1"], name=f"ffn1_fwd_{tag}", exchange=ex)
    done(buf)
    h2 = _rmsnorm_fwd(x1, p["nmix"], name=f"mixnorm_fwd_{tag}")
    ex, done = _carried(hooks, "inproj")
    if ex is None:
        cols3 = _mm_to_slices(h2, p["win"], name=f"inproj_fwd_{tag}")
    else:
        cols3, buf = _mm_to_slices(h2, p["win"], name=f"inproj_fwd_{tag}", exchange=ex)
        done(buf)
    bias3 = p["bias3"]
    y_conv = _conv_fwd(cols3, p["conv_w"], name=f"conv_fwd_{tag}")
    y_ret, states = _ret_fwd(cols3, aux["rope"], aux["ret"], name=f"ret_fwd_{tag}")
    ex, done = _carried(hooks, "att")
    (y_att, lse), buf = _att_fwd(cols3, bias3, name=f"att_fwd_{tag}", exchange=ex)
    done(buf)
    merged, s3, p3 = _merge_fwd(h2, (y_conv, y_ret, y_att), p["wmg"], p["wb"], name=f"merge_fwd_{tag}")
    x2 = _mm_rows(merged, p["wout"], dims=NN, out_dtype=F32, res=x1, name=f"outproj_fwd_{tag}")
    ex, done = _carried(hooks, "ffn2")
    (x3, h3, g2, u2, a2), buf = _ffn_fwd(x2, p["n2"], p["wg2"], p["wu2"], p["wd2"], name=f"ffn2_fwd_{tag}", exchange=ex)
    done(buf)
    saved = dict(x0=x, h1=h1, g1=g1, u1=u1, a1=a1, x1=x1, h2=h2, cols3=cols3, bias3=bias3, y_conv=y_conv,
                 y_ret=y_ret, states=states, y_att=y_att, lse=lse, merged=merged, s3=s3, p3=p3, x2=x2, h3=h3,
                 g2=g2, u2=u2, a2=a2)
    return x3, saved


def _ffn_grads(dxo, x, h, g, u, a, p, which, tag, grads, all_grads, hooks):
    n = which
    ex, done = _carried(hooks, f"ffn{n}_bwd", all_grads)
    (dx, dyb, dg, du, grads["n" + n]), buf = _ffn_bwd(dxo, x, p["n" + n], g, u, p["wg" + n], p["wu" + n], p["wd" + n],
                                                     name=f"ffn{n}_{tag}_bwd", exchange=ex)
    done(buf)
    grads["wd" + n] = _mm_tn(a, dyb, name=f"ffn{n}_{tag}_dwd")
    for key, lhs in (("wg" + n, dg), ("wu" + n, du)):
        ex, done = _carried(hooks, f"ffn{n}_d{key[:2]}", all_grads)
        if ex is None:
            grads[key] = _mm_tn(lhs, h, name=f"ffn{n}_{tag}_d{key[:2]}")
        else:
            grads[key], buf = _mm_tn(lhs, h, name=f"ffn{n}_{tag}_d{key[:2]}", exchange=ex)
            done(buf)
    return dx


def _layer_bwd(dx3, p, s, aux, tag, grads, all_grads, hooks=None):
    dx2 = _ffn_grads(dx3, s["x2"], s["h3"], s["g2"], s["u2"], s["a2"], p, "2", tag, grads, all_grads, hooks)

    dm = _mm_rows(dx2, p["wout"], dims=NT, out_dtype=BF16, name=f"outproj_dm_{tag}")
    grads["wout"] = _mm_tn(s["merged"], dx2, name=f"outproj_dw_{tag}")
    dgp3, dp3, dy3, dh2 = _merge_bwd(dm, s["s3"], s["p3"], p["wmg"], p["wb"], name=f"merge_bwd_{tag}")
    grads["wmg"] = _mm_tn_batch(s["h2"], dgp3, a_batched=False, name=f"merge_dwg_{tag}")
    y3 = jnp.stack([s["y_conv"], s["y_ret"], s["y_att"]])
    grads["wb"] = _mm_tn_batch(dp3, y3, a_batched=True, name=f"merge_dwb_{tag}")

    dconv3, grads["conv_w"] = _conv_bwd(s["cols3"], p["conv_w"], dy3[0], name=f"conv_bwd_{tag}")
    ex, done = _carried(hooks, "ret_bwd", all_grads)
    dret4, buf = _ret_bwd(s["cols3"], aux["rope"], aux["ret"], s["states"], dy3[1], name=f"ret_bwd_{tag}", exchange=ex)
    done(buf)
    ex, done = _carried(hooks, "att_bwd", all_grads)
    (datt3, dbias3), buf = _att_bwd(s["cols3"], s["bias3"], s["y_att"], s["lse"], dy3[2], name=f"att_bwd_{tag}",
                                    exchange=ex)
    done(buf)
    grads["rel_bias"] = _rel_bias_grad(dbias3, name=f"bias_grad_{tag}")

    dh2 = _inproj_dh((dconv3, dret4, datt3), p["win"], dh2, name=f"inproj_dh_{tag}")
    grads["win"] = jnp.concatenate([
        _mm_tn_slices(dconv3, s["h2"], name=f"inproj_dw_conv_{tag}"),
        _mm_tn_slices(dret4, s["h2"], name=f"inproj_dw_ret_{tag}"),
        _mm_tn_slices(datt3, s["h2"], name=f"inproj_dw_att_{tag}")], axis=0)
    dx1, grads["nmix"] = _rmsnorm_bwd(dh2, s["x1"], p["nmix"], dx2, name=f"mixnorm_bwd_{tag}")

    return _ffn_grads(dx1, s["x0"], s["h1"], s["g1"], s["u1"], s["a1"], p, "1", tag, grads, all_grads, hooks)


def _device_step(x, target, layers, final_norm, fwd_hooks=None, bwd_hooks=None):
    t = x.shape[0]
    depth = len(layers)
    aux = dict(rope=_rope_tables(t), ret=_ret_consts())
    ex, done = _carried(fwd_hooks[0] if fwd_hooks else None, "start")
    bias_all, buf = _bias_tiles(jnp.stack([p["rel_bias"] for p in layers]), name="bias_tiles", exchange=ex)
    done(buf)
    saved = []
    for l, p in enumerate(layers):
        p["bias3"] = bias_all[l]
        x, s = _layer_fwd(x, p, aux, f"l{l}", fwd_hooks[l] if fwd_hooks else None)
        saved.append(s)
    dx, dfinal, loss_row = _loss_fwd_bwd(x, final_norm, target, name="loss_fwd_bwd")
    grads = [dict() for _ in range(depth)]
    for l in reversed(range(depth)):
        dx = _layer_bwd(dx, layers[l], saved[l], aux, f"l{l}", grads[l], grads, bwd_hooks[l] if bwd_hooks else None)
    return loss_row, dx, grads, dfinal


def _sum_slots(parts, *, name, tr=512):
    n, r, cdim = parts.shape
    tr = _tile(r, tr)

    def body(p_ref, o_ref):
        acc = p_ref[0].astype(F32)
        for s in range(1, n):
            acc = acc + p_ref[s].astype(F32)
        o_ref[...] = acc

    return pl.pallas_call(
        body, name=name, grid=(r // tr,), in_specs=[pl.BlockSpec((n, tr, cdim), lambda i: (0, i, 0))],
        out_specs=pl.BlockSpec((tr, cdim), lambda i: (i, 0)), out_shape=jax.ShapeDtypeStruct((r, cdim), F32),
        compiler_params=_cparams("parallel"),
    )(parts)


def _all_reduce_small(v, *, name):
    r = v.shape[0]

    def body(x_ref, o_ref, slots, send_sems, recv_sems):
        x, y, c, me = _my_position()
        slots[me] = x_ref[...]
        sends = []
        for k in range(1, N_DEV):
            peer, _ = _peer(x, y, c, k)
            cp = pltpu.make_async_remote_copy(src_ref=x_ref, dst_ref=slots.at[me], send_sem=send_sems.at[k - 1],
                                              recv_sem=recv_sems.at[k - 1], device_id=peer, device_id_type=MESH)
            cp.start()
            sends.append(cp)
        for k in range(1, N_DEV):
            peer, peer_id = _peer(x, y, c, k)
            pltpu.make_async_remote_copy(src_ref=x_ref, dst_ref=slots.at[peer_id], send_sem=send_sems.at[k - 1],
                                         recv_sem=recv_sems.at[k - 1], device_id=peer, device_id_type=MESH).wait_recv()
        for cp in sends:
            cp.wait_send()
        acc = slots[0]
        for s in range(1, N_DEV):
            acc = acc + slots[s]
        o_ref[...] = acc

    return pl.pallas_call(
        body, name=name, in_specs=[pl.BlockSpec(memory_space=pltpu.VMEM)],
        out_specs=pl.BlockSpec(memory_space=pltpu.VMEM), out_shape=jax.ShapeDtypeStruct((r, 128), F32),
        scratch_shapes=[pltpu.VMEM((N_DEV, r, 128), F32), pltpu.SemaphoreType.DMA((N_DEV - 1,)),
                        pltpu.SemaphoreType.DMA((N_DEV - 1,))],
    )(v)


def _adamw(w, g, m, v, *, name, tr=256):
    shape = w.shape
    cdim = shape[-1]
    w2, g2, m2, v2 = (a.reshape(-1, cdim) for a in (w, g, m, v))
    r = w2.shape[0]
    tr = _tile(r, tr) if r % 8 == 0 else r
    c1 = 1.0 - ADAM_B1 ** ADAM_STEP
    c2 = 1.0 - ADAM_B2 ** ADAM_STEP

    def body(w_ref, g_ref, m_ref, v_ref, d_ref, mo_ref, vo_ref):
        gv = g_ref[...]
        mn = ADAM_B1 * m_ref[...] + (1.0 - ADAM_B1) * gv
        vn = ADAM_B2 * v_ref[...] + (1.0 - ADAM_B2) * (gv * gv)
        d_ref[...] = -ADAM_LR * ((mn / c1) / (jnp.sqrt(vn / c2) + ADAM_EPS) + ADAM_WD * w_ref[...])
        mo_ref[...] = mn
        vo_ref[...] = vn

    spec = pl.BlockSpec((tr, cdim), lambda i: (i, 0))
    outs = pl.pallas_call(
        body, name=name, grid=(r // tr,), in_specs=[spec] * 4, out_specs=[spec] * 3,
        out_shape=[jax.ShapeDtypeStruct((r, cdim), F32)] * 3, compiler_params=_cparams("parallel"),
    )(w2, g2, m2, v2)
    return tuple(o.reshape(shape) for o in outs)


GROUPS = {"f1": ("wg1", "wu1", "wd1"), "in": ("win",), "mg": ("wb", "wmg", "wout"), "f2": ("wg2", "wu2", "wd2")}
BIG = tuple(nm for members in GROUPS.values() for nm in members)


def _to_rows(name, w, d):
    if name in ("wg1", "wu1", "wg2", "wu2", "win"):
        return w.T
    if name == "wb":
        return w.transpose(0, 2, 1).reshape(-1, d)
    if name == "wmg":
        return w.reshape(-1, d)
    return w


def _from_rows(name, rows, d):
    if name in ("wg1", "wu1", "wg2", "wu2", "win"):
        return rows.T
    if name == "wb":
        return rows.reshape(3, -1, BRANCH_W).transpose(0, 2, 1)
    if name == "wmg":
        return rows.reshape(3, -1, d)
    return rows


def _full_from_gathered(name, g, d):
    if name == "wb":
        return g.reshape(N_DEV, 3, -1, BRANCH_W).transpose(1, 0, 2, 3).reshape(3, d, BRANCH_W)
    if name == "wmg":
        return g.reshape(N_DEV, 3, -1, d).transpose(1, 0, 2, 3).reshape(3, d, d)
    return g.reshape(-1, d)


def _gathered_from_full(name, full, d):
    if name == "wb":
        return full.reshape(3, N_DEV, -1, BRANCH_W).transpose(1, 0, 2, 3).reshape(N_DEV, -1, d)
    if name == "wmg":
        return full.reshape(3, N_DEV, -1, d).transpose(1, 0, 2, 3).reshape(N_DEV, -1, d)
    return full.reshape(N_DEV, -1, d)


def kernel(x, ffn1_norm, ffn1_w_gate, ffn1_w_up, ffn1_w_down, mix_norm, w_in, conv_w, rel_bias, w_branch, w_merge_gate, w_out, ffn2_norm, ffn2_w_gate, ffn2_w_up, ffn2_w_down, final_norm, loss_target, m_ffn1_norm, m_ffn1_w_gate, m_ffn1_w_up, m_ffn1_w_down, m_mix_norm, m_w_in, m_conv_w, m_rel_bias, m_w_branch, m_w_merge_gate, m_w_out, m_ffn2_norm, m_ffn2_w_gate, m_ffn2_w_up, m_ffn2_w_down, m_final_norm, v_ffn1_norm, v_ffn1_w_gate, v_ffn1_w_up, v_ffn1_w_down, v_mix_norm, v_w_in, v_conv_w, v_rel_bias, v_w_branch, v_w_merge_gate, v_w_out, v_ffn2_norm, v_ffn2_w_gate, v_ffn2_w_up, v_ffn2_w_down, v_final_norm):
    names = ["ffn1_norm", "ffn1_w_gate", "ffn1_w_up", "ffn1_w_down", "mix_norm", "w_in", "conv_w", "rel_bias",
             "w_branch", "w_merge_gate", "w_out", "ffn2_norm", "ffn2_w_gate", "ffn2_w_up", "ffn2_w_down", "final_norm"]
    weights = dict(zip(names, (ffn1_norm, ffn1_w_gate, ffn1_w_up, ffn1_w_down, mix_norm, w_in, conv_w, rel_bias,
                               w_branch, w_merge_gate, w_out, ffn2_norm, ffn2_w_gate, ffn2_w_up, ffn2_w_down,
                               final_norm)))
    m_in = dict(zip(names, (m_ffn1_norm, m_ffn1_w_gate, m_ffn1_w_up, m_ffn1_w_down, m_mix_norm, m_w_in, m_conv_w,
                            m_rel_bias, m_w_branch, m_w_merge_gate, m_w_out, m_ffn2_norm, m_ffn2_w_gate,
                            m_ffn2_w_up, m_ffn2_w_down, m_final_norm)))
    v_in = dict(zip(names, (v_ffn1_norm, v_ffn1_w_gate, v_ffn1_w_up, v_ffn1_w_down, v_mix_norm, v_w_in, v_conv_w,
                            v_rel_bias, v_w_branch, v_w_merge_gate, v_w_out, v_ffn2_norm, v_ffn2_w_gate,
                            v_ffn2_w_up, v_ffn2_w_down, v_final_norm)))
    big_of = dict(wg1="ffn1_w_gate", wu1="ffn1_w_up", wd1="ffn1_w_down", win="w_in", wb="w_branch",
                  wmg="w_merge_gate", wout="w_out", wg2="ffn2_w_gate", wu2="ffn2_w_up", wd2="ffn2_w_down")
    depth = ffn1_norm.shape[0]
    d = x.shape[-1]
    xs = x.reshape(-1, d)
    target = loss_target.reshape(-1, d)
    _, _, _, me = _my_position()

    rows_of = {nm: _to_rows(nm, weights[big_of[nm]][0], d).shape[0] for nm in BIG}

    def layout(members):
        starts = [sum(rows_of[m] for m in members[:i]) for i in range(len(members))]
        return [(nm, r0, rows_of[nm]) for nm, r0 in zip(members, starts)]

    def pack_shards(l, members):
        return jnp.concatenate([_to_rows(nm, weights[big_of[nm]][l], d).astype(BF16) for nm in members], axis=0)

    def unpack(members, gathered):
        return {nm: _full_from_gathered(nm, gathered[:, r0:r0 + nr], d) for nm, r0, nr in layout(members)}

    conv_cols = conv_w.shape[-1]
    conv_full = _exchange("gather", conv_w.reshape(depth * 3, conv_cols), name="gather_conv_w")
    conv_full = conv_full.reshape(N_DEV, depth, 3, conv_cols).transpose(1, 2, 0, 3).reshape(depth, 3, -1)
    layers = [dict(n1=ffn1_norm[l][None], nmix=mix_norm[l][None], n2=ffn2_norm[l][None], conv_w=conv_full[l],
                   rel_bias=rel_bias[l]) for l in range(depth)]

    def gather_hook(l, grp):
        members = GROUPS[grp]
        return (lambda _: ("gather", pack_shards(l, members))), (lambda buf: layers[l].update(unpack(members, buf)))

    fwd_hooks = []
    for l in range(depth):
        hooks = {"ffn1": gather_hook(l, "in"), "inproj": gather_hook(l, "mg"), "att": gather_hook(l, "f2")}
        if l == 0:
            hooks["start"] = gather_hook(0, "f1")
        if l + 1 < depth:
            hooks["ffn2"] = gather_hook(l + 1, "f1")
        fwd_hooks.append(hooks)

    received = {}

    def pack_grads(g, members):
        return jnp.concatenate([_gathered_from_full(nm, g[nm], d).astype(BF16) for nm in members], axis=1)

    def scatter_hook(l, members):
        return ((lambda all_grads: ("scatter", pack_grads(all_grads[l], members))),
                (lambda buf: received.update({(l, members): buf})))

    bwd_hooks = []
    for l in range(depth):
        hooks = {"att_bwd": scatter_hook(l, GROUPS["f2"]), "ret_bwd": scatter_hook(l, GROUPS["mg"]),
                 "ffn1_bwd": scatter_hook(l, GROUPS["in"])}
        if l + 1 < depth:
            hooks["ffn2_bwd"] = scatter_hook(l + 1, GROUPS["f1"])
        if l == 0:
            hooks["ffn1_dwg"] = scatter_hook(0, ("wd1",))
            hooks["ffn1_dwu"] = scatter_hook(0, ("wg1",))
        bwd_hooks.append(hooks)

    loss_row, dx, grads, dfinal = _device_step(xs, target, layers, final_norm[None], fwd_hooks, bwd_hooks)
    received[(0, ("wu1",))] = _exchange("scatter", pack_grads(grads[0], ("wu1",)), name="scatter_l0_wu1")

    big_grads = {nm: [None] * depth for nm in BIG}
    for (l, members), buf in received.items():
        shard_grad = _sum_slots(buf, name=f"sum_grads_l{l}_{'_'.join(members)}")
        for nm, r0, nr in layout(members):
            big_grads[nm][l] = _from_rows(nm, shard_grad[r0:r0 + nr], d)

    small = {"ffn1_norm": jnp.stack([grads[l]["n1"][0] for l in range(depth)]),
             "mix_norm": jnp.stack([grads[l]["nmix"][0] for l in range(depth)]),
             "ffn2_norm": jnp.stack([grads[l]["n2"][0] for l in range(depth)]),
             "final_norm": dfinal[0],
             "rel_bias": jnp.stack([grads[l]["rel_bias"] for l in range(depth)]),
             "conv_w": jnp.stack([grads[l]["conv_w"] for l in range(depth)]),
             "loss": loss_row[0, :1]}
    order = list(small)
    flat = jnp.concatenate([small[k].reshape(-1) for k in order])
    pad = (-flat.shape[0]) % 1024
    summed = _all_reduce_small(jnp.pad(flat, (0, pad)).reshape(-1, 128), name="reduce_small").reshape(-1)
    pos = 0
    for k in order:
        n = small[k].size
        small[k] = summed[pos:pos + n].reshape(small[k].shape)
        pos += n
    small["conv_w"] = lax.dynamic_slice_in_dim(small["conv_w"], me * conv_cols, conv_cols, axis=2)

    grad_w = {big_of[nm]: jnp.stack(big_grads[nm]) for nm in BIG}
    grad_w.update({k: small[k] for k in order if k != "loss"})
    delta, new_m, new_v = {}, {}, {}
    for nm in names:
        delta[nm], new_m[nm], new_v[nm] = _adamw(weights[nm], grad_w[nm], m_in[nm], v_in[nm], name=f"adamw_{nm}")
    return (small["loss"].reshape(()), dx.reshape(x.shape), *[grad_w[n] for n in names], *[delta[n] for n in names],
            *[new_m[n] for n in names], *[new_v[n] for n in names])
```

```python
import functools
import math

import jax
import jax.numpy as jnp
from jax import lax
from jax.experimental import pallas as pl
from jax.experimental.pallas import tpu as pltpu

F32 = jnp.float32
BF16 = jnp.bfloat16

N_DEV = 8
EPS = 1e-6
CHUNK = 64
BRANCH_W = 512
N_SLICES = 10
H_RET = 4
DK_RET = 128
H_ATT = 8
DH_ATT = 64
N_PREV_CHUNKS = 8
REL_CLIP = 128
N_REL = 2 * REL_CLIP + 1
NEG_INF = -1e30
ROPE_BASE = 10000.0
ATT_QB = 256
ATT_LOOKBACK = N_PREV_CHUNKS * CHUNK
ATT_WIN = ATT_LOOKBACK + ATT_QB
RET_TB = 256
TOKEN_TK = 2048
CONV_HALO = 16

ADAM_LR = 0.001
ADAM_B1 = 0.9
ADAM_B2 = 0.999
ADAM_EPS = 1e-08
ADAM_WD = 0.01
ADAM_STEP = 10

VMEM_LIMIT_BYTES = 56 * 1024 * 1024
MESH = pl.DeviceIdType.MESH


def _cparams(*sem):
    return pltpu.CompilerParams(dimension_semantics=sem, vmem_limit_bytes=VMEM_LIMIT_BYTES)


def _dot(a, b):
    return lax.dot_general(a, b, (((1,), (0,)), ((), ())), preferred_element_type=F32)


def _dot_nt(a, b):
    return lax.dot_general(a, b, (((1,), (1,)), ((), ())), preferred_element_type=F32)


def _dot_tn(a, b):
    return lax.dot_general(a, b, (((0,), (0,)), ((), ())), preferred_element_type=F32)


def _bf(v):
    return v.astype(BF16)


def _sigmoid(v):
    return 1.0 / (1.0 + jnp.exp(-v))


def _tile(n, want):
    if n <= want:
        return n
    for t in range(want - want % 128, 0, -128):
        if n % t == 0:
            return t
    t = want
    while n % t:
        t //= 2
    return t


def _my_position():
    x, y, c = lax.axis_index("x"), lax.axis_index("y"), lax.axis_index("c")
    return x, y, c, 4 * x + 2 * y + c


def _peer(x, y, c, k):
    px = 1 - x if k & 4 else x
    py = 1 - y if k & 2 else y
    pc = 1 - c if k & 1 else c
    return (px, py, pc), 4 * px + 2 * py + pc


def _exchange_sems():
    return [pltpu.SemaphoreType.DMA((N_DEV - 1,)), pltpu.SemaphoreType.DMA((N_DEV - 1,)), pltpu.SemaphoreType.DMA]


def _exchange_copy(kind, src_ref, dst_ref, sems, k, pos, incoming):
    x, y, c, me = pos
    peer, peer_id = _peer(x, y, c, k)
    if kind == "gather":
        src = src_ref
    else:
        src = src_ref.at[me if incoming else peer_id]
    return pltpu.make_async_remote_copy(src_ref=src, dst_ref=dst_ref.at[peer_id if incoming else me],
                                        send_sem=sems[0].at[k - 1], recv_sem=sems[1].at[k - 1],
                                        device_id=peer, device_id_type=MESH)


def _exchange_local(kind, src_ref, dst_ref, sems, me):
    return pltpu.make_async_copy(src_ref if kind == "gather" else src_ref.at[me], dst_ref.at[me], sems[2])


def _exchange_start(kind, src_ref, dst_ref, sems):
    pos = _my_position()
    _exchange_local(kind, src_ref, dst_ref, sems, pos[3]).start()
    for k in range(1, N_DEV):
        _exchange_copy(kind, src_ref, dst_ref, sems, k, pos, False).start()


def _exchange_wait(kind, src_ref, dst_ref, sems):
    pos = _my_position()
    for k in range(1, N_DEV):
        _exchange_copy(kind, src_ref, dst_ref, sems, k, pos, True).wait_recv()
    for k in range(1, N_DEV):
        _exchange_copy(kind, src_ref, dst_ref, sems, k, pos, False).wait_send()
    _exchange_local(kind, src_ref, dst_ref, sems, pos[3]).wait()


def _exchange_shape(kind, src):
    return jax.ShapeDtypeStruct((N_DEV,) + src.shape if kind == "gather" else src.shape, src.dtype)


def _exchange(kind, src, *, name):
    def body(src_ref, dst_ref, *sems):
        _exchange_start(kind, src_ref, dst_ref, sems)
        _exchange_wait(kind, src_ref, dst_ref, sems)

    return pl.pallas_call(
        body, name=name, in_specs=[pl.BlockSpec(memory_space=pl.ANY)], out_specs=pl.BlockSpec(memory_space=pl.ANY),
        out_shape=_exchange_shape(kind, src), scratch_shapes=_exchange_sems(),
    )(src)


def _pcall(body, args, *, name, grid, in_specs, out_specs, out_shape, scratch_shapes, sem, exchange=None):
    if exchange is None:
        outs = pl.pallas_call(body, name=name, grid=grid, in_specs=in_specs, out_specs=out_specs, out_shape=out_shape,
                              scratch_shapes=scratch_shapes, compiler_params=_cparams(*sem))(*args)
        return outs, None
    kind, src = exchange
    n_in, n_out, n_scr = len(in_specs), len(out_specs), len(scratch_shapes)
    hbm = pl.BlockSpec(memory_space=pl.ANY)

    def carrier(*refs):
        ins, src_ref = refs[:n_in], refs[n_in]
        outs, dst_ref = refs[n_in + 1:n_in + 1 + n_out], refs[n_in + 1 + n_out]
        scr = refs[n_in + 2 + n_out:n_in + 2 + n_out + n_scr]
        sems = refs[n_in + 2 + n_out + n_scr:]
        ids = [pl.program_id(a) for a in range(len(grid))]
        first = functools.reduce(jnp.logical_and, [i == 0 for i in ids])
        last = functools.reduce(jnp.logical_and, [i == g - 1 for i, g in zip(ids, grid)])

        @pl.when(first)
        def _():
            _exchange_start(kind, src_ref, dst_ref, sems)

        body(*ins, *outs, *scr)

        @pl.when(last)
        def _():
            _exchange_wait(kind, src_ref, dst_ref, sems)

    outs = pl.pallas_call(
        carrier, name=name, grid=grid, in_specs=list(in_specs) + [hbm], out_specs=list(out_specs) + [hbm],
        out_shape=list(out_shape) + [_exchange_shape(kind, src)],
        scratch_shapes=list(scratch_shapes) + _exchange_sems(),
        compiler_params=_cparams(*(["arbitrary"] * len(grid))))(*args, src)
    return outs[:-1], outs[-1]


def _matmul(a, b, *, dims, grid, a_spec, b_spec, o_spec, out_shape, acc_shape, name, scale=1.0,
            res=None, res_spec=None, exchange=None):
    nk = grid[3]
    has_res = res is not None

    def body(*refs):
        if has_res:
            a_ref, b_ref, r_ref, o_ref = refs[:4]
        else:
            a_ref, b_ref, o_ref = refs[:3]
            r_ref = None

        def finish(acc):
            if scale != 1.0:
                acc = acc * scale
            if r_ref is not None:
                acc = acc + r_ref[...].astype(F32)
            o_ref[...] = acc.astype(o_ref.dtype)

        part = lax.dot_general(_bf(a_ref[...]), _bf(b_ref[...]), (dims, ((), ())), preferred_element_type=F32)
        if nk == 1:
            finish(part)
        else:
            acc_ref = refs[-1]
            k = pl.program_id(3)

            @pl.when(k == 0)
            def _():
                acc_ref[...] = part

            @pl.when(jnp.logical_and(k > 0, k < nk - 1))
            def _():
                acc_ref[...] += part

            @pl.when(k == nk - 1)
            def _():
                finish(acc_ref[...] + part)

    in_specs = [a_spec, b_spec] + ([res_spec] if has_res else [])
    args = (a, b) + ((res,) if has_res else ())
    outs, moved = _pcall(
        body, args, name=name, grid=grid, in_specs=in_specs, out_specs=[o_spec], out_shape=[out_shape],
        scratch_shapes=[] if nk == 1 else [pltpu.VMEM(acc_shape, F32)],
        sem=("parallel", "parallel", "parallel", "arbitrary"), exchange=exchange)
    return outs[0] if exchange is None else (outs[0], moved)


NT = ((1,), (1,))
NN = ((1,), (0,))
TN = ((0,), (0,))


def _mm_rows(a, b, *, dims, out_dtype, name, res=None, scale=1.0, tm=1024, tn=1024, tk=1024):
    m, kdim = a.shape
    n = b.shape[1] if dims == NN else b.shape[0]
    tm, tn, tk = _tile(m, tm), _tile(n, tn), _tile(kdim, tk)
    grid = (1, m // tm, n // tn, kdim // tk)
    a_spec = pl.BlockSpec((tm, tk), lambda s, i, j, k: (i, k))
    if dims == NN:
        b_spec = pl.BlockSpec((tk, tn), lambda s, i, j, k: (k, j))
    else:
        b_spec = pl.BlockSpec((tn, tk), lambda s, i, j, k: (j, k))
    o_spec = pl.BlockSpec((tm, tn), lambda s, i, j, k: (i, j))
    return _matmul(a, b, dims=dims, grid=grid, a_spec=a_spec, b_spec=b_spec, o_spec=o_spec,
                   out_shape=jax.ShapeDtypeStruct((m, n), out_dtype), acc_shape=(tm, tn), name=name,
                   res=res, res_spec=o_spec if res is not None else None, scale=scale)


def _mm_to_slices(a, b_t, *, name, tm=1024, exchange=None):
    m, kdim = a.shape
    n = b_t.shape[0]
    tm = _tile(m, tm)
    w = BRANCH_W
    grid = (1, m // tm, n // w, 1)
    return _matmul(a, b_t, dims=NT, grid=grid,
                   a_spec=pl.BlockSpec((tm, kdim), lambda s, i, j, k: (i, 0)),
                   b_spec=pl.BlockSpec((w, kdim), lambda s, i, j, k: (j, 0)),
                   o_spec=pl.BlockSpec((None, tm, w), lambda s, i, j, k: (j, i, 0)),
                   out_shape=jax.ShapeDtypeStruct((n // w, m, w), BF16), acc_shape=(tm, w), name=name,
                   exchange=exchange)


def _inproj_dh(parts, w_t, res, *, name, tm=512):
    t, n = res.shape
    w = BRANCH_W
    tm = _tile(t, tm)
    counts = [p.shape[0] for p in parts]
    assert sum(counts) * w == w_t.shape[0]

    def body(*refs):
        w_ref, r_ref, o_ref = refs[len(parts):]
        cols = jnp.concatenate([p_ref[s] for p_ref, cnt in zip(refs, counts) for s in range(cnt)], axis=1)
        o_ref[...] = r_ref[...] + _dot(cols, w_ref[...])

    row = pl.BlockSpec((tm, n), lambda i: (i, 0))
    return pl.pallas_call(
        body, name=name, grid=(t // tm,),
        in_specs=[pl.BlockSpec((cnt, tm, w), lambda i: (0, i, 0)) for cnt in counts]
        + [pl.BlockSpec(w_t.shape, lambda i: (0, 0)), row],
        out_specs=row, out_shape=jax.ShapeDtypeStruct((t, n), F32), compiler_params=_cparams("parallel"),
    )(*parts, w_t, res)


def _mm_tn(a, b, *, name, tm=1408, tn=1024, tk=TOKEN_TK, exchange=None):
    t, m = a.shape
    n = b.shape[1]
    tm, tn, tk = _tile(m, tm), _tile(n, tn), _tile(t, tk)
    grid = (1, m // tm, n // tn, t // tk)
    return _matmul(a, b, dims=TN, grid=grid,
                   a_spec=pl.BlockSpec((tk, tm), lambda s, i, j, k: (k, i)),
                   b_spec=pl.BlockSpec((tk, tn), lambda s, i, j, k: (k, j)),
                   o_spec=pl.BlockSpec((tm, tn), lambda s, i, j, k: (i, j)),
                   out_shape=jax.ShapeDtypeStruct((m, n), F32), acc_shape=(tm, tn), name=name, exchange=exchange)


def _mm_tn_slices(a3, b, *, name, tn=1024, tk=TOKEN_TK):
    s_n, t, w = a3.shape
    n = b.shape[1]
    tn, tk = _tile(n, tn), _tile(t, tk)
    grid = (1, s_n, n // tn, t // tk)
    return _matmul(a3, b, dims=TN, grid=grid,
                   a_spec=pl.BlockSpec((None, tk, w), lambda s, i, j, k: (i, k, 0)),
                   b_spec=pl.BlockSpec((tk, tn), lambda s, i, j, k: (k, j)),
                   o_spec=pl.BlockSpec((w, tn), lambda s, i, j, k: (i, j)),
                   out_shape=jax.ShapeDtypeStruct((s_n * w, n), F32), acc_shape=(w, tn), name=name)


def _mm_tn_batch(a, b3, *, name, a_batched, tm=1024, tn=1024, tk=TOKEN_TK):
    s_n, t, n = b3.shape
    m = a.shape[-1]
    tm, tn, tk = _tile(m, tm), _tile(n, tn), _tile(t, tk)
    grid = (s_n, m // tm, n // tn, t // tk)
    if a_batched:
        a_spec = pl.BlockSpec((None, tk, tm), lambda s, i, j, k: (s, k, i))
    else:
        a_spec = pl.BlockSpec((tk, tm), lambda s, i, j, k: (k, i))
    return _matmul(a, b3, dims=TN, grid=grid, a_spec=a_spec,
                   b_spec=pl.BlockSpec((None, tk, tn), lambda s, i, j, k: (s, k, j)),
                   o_spec=pl.BlockSpec((None, tm, tn), lambda s, i, j, k: (s, i, j)),
                   out_shape=jax.ShapeDtypeStruct((s_n, m, n), F32), acc_shape=(tm, tn), name=name)


def _norm_parts(xf):
    r = lax.rsqrt(jnp.mean(xf * xf, axis=-1, keepdims=True) + EPS)
    return xf * r, r


def _norm_bwd(dh, xhat, r, w):
    dxhat = dh * w
    dx = r * (dxhat - xhat * jnp.mean(dxhat * xhat, axis=-1, keepdims=True))
    return dx, jnp.sum(dh * xhat, axis=0, keepdims=True)


def _rmsnorm_fwd(x, w, *, name, tm=1024):
    t, d = x.shape
    tm = _tile(t, tm)

    def body(x_ref, w_ref, h_ref):
        xhat, _ = _norm_parts(x_ref[...])
        h_ref[...] = _bf(xhat * w_ref[...])

    return pl.pallas_call(
        body, name=name, grid=(t // tm,),
        in_specs=[pl.BlockSpec((tm, d), lambda i: (i, 0)), pl.BlockSpec((1, d), lambda i: (0, 0))],
        out_specs=pl.BlockSpec((tm, d), lambda i: (i, 0)),
        out_shape=jax.ShapeDtypeStruct((t, d), BF16), compiler_params=_cparams("parallel"),
    )(x, w)


def _rmsnorm_bwd(dh, x, w, dres, *, name, tm=1024):
    t, d = x.shape
    tm = _tile(t, tm)

    def body(dh_ref, x_ref, w_ref, dres_ref, dx_ref, dw_ref):
        xhat, r = _norm_parts(x_ref[...])
        dx, dw = _norm_bwd(dh_ref[...], xhat, r, w_ref[...])
        dx_ref[...] = dres_ref[...] + dx

        @pl.when(pl.program_id(0) == 0)
        def _():
            dw_ref[...] = dw

        @pl.when(pl.program_id(0) > 0)
        def _():
            dw_ref[...] += dw

    row = pl.BlockSpec((tm, d), lambda i: (i, 0))
    vec = pl.BlockSpec((1, d), lambda i: (0, 0))
    return pl.pallas_call(
        body, name=name, grid=(t // tm,), in_specs=[row, row, vec, row], out_specs=[row, vec],
        out_shape=[jax.ShapeDtypeStruct((t, d), F32), jax.ShapeDtypeStruct((1, d), F32)],
        compiler_params=_cparams("arbitrary"),
    )(dh, x, w, dres)


def _loss_fwd_bwd(x, w, target, *, name, tm=1024):
    t, d = x.shape
    tm = _tile(t, tm)

    def body(x_ref, w_ref, t_ref, dx_ref, dw_ref, loss_ref):
        xhat, r = _norm_parts(x_ref[...])
        wv = w_ref[...]
        err = xhat * wv - t_ref[...]
        dx, dw = _norm_bwd(err * (1.0 / d), xhat, r, wv)
        dx_ref[...] = dx
        part = jnp.full((1, 128), 0.5 / d, F32) * jnp.sum(err * err)

        @pl.when(pl.program_id(0) == 0)
        def _():
            dw_ref[...] = dw
            loss_ref[...] = part

        @pl.when(pl.program_id(0) > 0)
        def _():
            dw_ref[...] += dw
            loss_ref[...] += part

    row = pl.BlockSpec((tm, d), lambda i: (i, 0))
    vec = pl.BlockSpec((1, d), lambda i: (0, 0))
    return pl.pallas_call(
        body, name=name, grid=(t // tm,), in_specs=[row, vec, row],
        out_specs=[row, vec, pl.BlockSpec((1, 128), lambda i: (0, 0))],
        out_shape=[jax.ShapeDtypeStruct((t, d), F32), jax.ShapeDtypeStruct((1, d), F32),
                   jax.ShapeDtypeStruct((1, 128), F32)],
        compiler_params=_cparams("arbitrary"),
    )(x, w, target)


def _ffn_tiles(t, f):
    tf = f
    for cand in (1408, 1024, 512, 256, 128):
        if f % cand == 0:
            tf = cand
            break
    return _tile(t, 512), tf


def _ffn_fwd(x, nw, wg_t, wu_t, wd, *, name, exchange=None):
    t, d = x.shape
    f = wd.shape[0]
    tm, tf = _ffn_tiles(t, f)
    nf = f // tf

    def body(x_ref, nw_ref, wg_ref, wu_ref, wd_ref, xo_ref, h_ref, g_ref, u_ref, a_ref, hs_ref, acc_ref):
        j = pl.program_id(1)

        @pl.when(j == 0)
        def _():
            xhat, _ = _norm_parts(x_ref[...])
            hb = _bf(xhat * nw_ref[...])
            hs_ref[...] = hb
            h_ref[...] = hb

        hb = hs_ref[...]
        g = _dot_nt(hb, wg_ref[...])
        u = _dot_nt(hb, wu_ref[...])
        a = _bf(g * _sigmoid(g) * u)
        g_ref[...] = _bf(g)
        u_ref[...] = _bf(u)
        a_ref[...] = a
        part = _dot(a, wd_ref[...])

        @pl.when(j == 0)
        def _():
            acc_ref[...] = part

        @pl.when(j > 0)
        def _():
            acc_ref[...] += part

        @pl.when(j == nf - 1)
        def _():
            xo_ref[...] = x_ref[...] + 0.5 * acc_ref[...]

    row = pl.BlockSpec((tm, d), lambda i, j: (i, 0))
    wspec = pl.BlockSpec((tf, d), lambda i, j: (j, 0))
    hid = pl.BlockSpec((tm, tf), lambda i, j: (i, j))
    return _pcall(
        body, (x, nw, wg_t, wu_t, wd), name=name, grid=(t // tm, nf),
        in_specs=[row, pl.BlockSpec((1, d), lambda i, j: (0, 0)), wspec, wspec, wspec],
        out_specs=[row, row, hid, hid, hid],
        out_shape=[jax.ShapeDtypeStruct((t, d), F32), jax.ShapeDtypeStruct((t, d), BF16)]
        + [jax.ShapeDtypeStruct((t, f), BF16)] * 3,
        scratch_shapes=[pltpu.VMEM((tm, d), BF16), pltpu.VMEM((tm, d), F32)],
        sem=("parallel", "arbitrary"), exchange=exchange)


def _ffn_bwd(dxo, x, nw, g, u, wg_t, wu_t, wd, *, name, exchange=None):
    t, d = x.shape
    f = wd.shape[0]
    tm, tf = _ffn_tiles(t, f)
    nf = f // tf

    def body(dxo_ref, x_ref, nw_ref, g_ref, u_ref, wg_ref, wu_ref, wd_ref,
             dx_ref, dy_ref, dg_ref, du_ref, dnw_ref, dys_ref, acc_ref):
        i, j = pl.program_id(0), pl.program_id(1)

        @pl.when(j == 0)
        def _():
            dyb = _bf(0.5 * dxo_ref[...])
            dys_ref[...] = dyb
            dy_ref[...] = dyb

        da = _dot_nt(dys_ref[...], wd_ref[...])
        gv = g_ref[...].astype(F32)
        uv = u_ref[...].astype(F32)
        s = _sigmoid(gv)
        dg = _bf(da * uv * (s * (1.0 + gv * (1.0 - s))))
        du = _bf(da * (gv * s))
        dg_ref[...] = dg
        du_ref[...] = du
        part = _dot(dg, wg_ref[...]) + _dot(du, wu_ref[...])

        @pl.when(j == 0)
        def _():
            acc_ref[...] = part

        @pl.when(j > 0)
        def _():
            acc_ref[...] += part

        @pl.when(j == nf - 1)
        def _():
            xhat, r = _norm_parts(x_ref[...])
            dx, dw = _norm_bwd(acc_ref[...], xhat, r, nw_ref[...])
            dx_ref[...] = dxo_ref[...] + dx

            @pl.when(i == 0)
            def _():
                dnw_ref[...] = dw

            @pl.when(i > 0)
            def _():
                dnw_ref[...] += dw

    row = pl.BlockSpec((tm, d), lambda i, j: (i, 0))
    vec = pl.BlockSpec((1, d), lambda i, j: (0, 0))
    wspec = pl.BlockSpec((tf, d), lambda i, j: (j, 0))
    hid = pl.BlockSpec((tm, tf), lambda i, j: (i, j))
    return _pcall(
        body, (dxo, x, nw, g, u, wg_t, wu_t, wd), name=name, grid=(t // tm, nf),
        in_specs=[row, row, vec, hid, hid, wspec, wspec, wspec],
        out_specs=[row, row, hid, hid, vec],
        out_shape=[jax.ShapeDtypeStruct((t, d), F32), jax.ShapeDtypeStruct((t, d), BF16),
                   jax.ShapeDtypeStruct((t, f), BF16), jax.ShapeDtypeStruct((t, f), BF16),
                   jax.ShapeDtypeStruct((1, d), F32)],
        scratch_shapes=[pltpu.VMEM((tm, d), BF16), pltpu.VMEM((tm, d), F32)],
        sem=("arbitrary", "arbitrary"), exchange=exchange)


def _shift_down(prev, cur, n):
    ext = jnp.concatenate([prev, cur], axis=0)
    return pltpu.roll(ext, n, axis=0)[prev.shape[0]:]


def _shift_up(cur, nxt, n):
    ext = jnp.concatenate([cur, nxt], axis=0)
    return pltpu.roll(ext, ext.shape[0] - n, axis=0)[:cur.shape[0]]


def _conv_specs(t, tb):
    hb = tb // CONV_HALO
    last = t // CONV_HALO - 1

    def tile(s):
        return pl.BlockSpec((None, tb, 128), lambda c, i: (s, i, c))

    def prev(s):
        return pl.BlockSpec((None, CONV_HALO, 128), lambda c, i: (s, jnp.maximum(i * hb - 1, 0), c))

    def nxt(s):
        return pl.BlockSpec((None, CONV_HALO, 128), lambda c, i: (s, jnp.minimum((i + 1) * hb, last), c))

    return tile, prev, nxt


def _conv_fwd(cols3, conv_w, *, name, tb=1024):
    _, t, bw = cols3.shape
    tb = _tile(t, tb)
    tile, prev, _ = _conv_specs(t, tb)

    def body(u_ref, b_ref, c_ref, up_ref, cp_ref, w_ref, y_ref):
        first = pl.program_id(1) == 0
        z = c_ref[...].astype(F32) * u_ref[...].astype(F32)
        zp = jnp.where(first, 0.0, cp_ref[...].astype(F32) * up_ref[...].astype(F32))
        conv = w_ref[0:1, :] * _shift_down(zp, z, 2) + w_ref[1:2, :] * _shift_down(zp, z, 1) + w_ref[2:3, :] * z
        y_ref[...] = _bf(b_ref[...].astype(F32) * conv)

    return pl.pallas_call(
        body, name=name, grid=(bw // 128, t // tb),
        in_specs=[tile(0), tile(1), tile(2), prev(0), prev(2), pl.BlockSpec((3, 128), lambda c, i: (0, c))],
        out_specs=pl.BlockSpec((tb, 128), lambda c, i: (i, c)),
        out_shape=jax.ShapeDtypeStruct((t, bw), BF16), compiler_params=_cparams("parallel", "parallel"),
    )(cols3, cols3, cols3, cols3, cols3, conv_w)


def _conv_bwd(cols3, conv_w, dy, *, name, tb=1024):
    _, t, bw = cols3.shape
    tb = _tile(t, tb)
    nt = t // tb
    tile, prev, nxt = _conv_specs(t, tb)
    hb = tb // CONV_HALO
    last = t // CONV_HALO - 1

    def body(u_ref, b_ref, c_ref, up_ref, cp_ref, bn_ref, dy_ref, dyn_ref, w_ref, d3_ref, dw_ref):
        i = pl.program_id(1)
        uv, bv, cv = u_ref[...].astype(F32), b_ref[...].astype(F32), c_ref[...].astype(F32)
        dyv = dy_ref[...].astype(F32)
        z = cv * uv
        zp = jnp.where(i == 0, 0.0, cp_ref[...].astype(F32) * up_ref[...].astype(F32))
        z1, z2 = _shift_down(zp, z, 1), _shift_down(zp, z, 2)
        w0, w1, w2 = w_ref[0:1, :], w_ref[1:2, :], w_ref[2:3, :]
        conv = w0 * z2 + w1 * z1 + w2 * z
        dconv = dyv * bv
        dconv_n = jnp.where(i == nt - 1, 0.0, dyn_ref[...].astype(F32) * bn_ref[...].astype(F32))
        dz = w2 * dconv + w1 * _shift_up(dconv, dconv_n, 1) + w0 * _shift_up(dconv, dconv_n, 2)
        d3_ref[0] = _bf(dz * cv)
        d3_ref[1] = _bf(dyv * conv)
        d3_ref[2] = _bf(dz * uv)
        dws = [jnp.sum(dconv * zz, axis=0, keepdims=True) for zz in (z2, z1, z)]

        @pl.when(i == 0)
        def _():
            for j in range(3):
                dw_ref[j:j + 1, :] = dws[j]

        @pl.when(i > 0)
        def _():
            for j in range(3):
                dw_ref[j:j + 1, :] += dws[j]

    dy_tile = pl.BlockSpec((tb, 128), lambda c, i: (i, c))
    dy_next = pl.BlockSpec((CONV_HALO, 128), lambda c, i: (jnp.minimum((i + 1) * hb, last), c))
    wspec = pl.BlockSpec((3, 128), lambda c, i: (0, c))
    return pl.pallas_call(
        body, name=name, grid=(bw // 128, nt),
        in_specs=[tile(0), tile(1), tile(2), prev(0), prev(2), nxt(1), dy_tile, dy_next, wspec],
        out_specs=[pl.BlockSpec((3, tb, 128), lambda c, i: (0, i, c)), wspec],
        out_shape=[jax.ShapeDtypeStruct((3, t, bw), BF16), jax.ShapeDtypeStruct((3, bw), F32)],
        compiler_params=_cparams("parallel", "arbitrary"),
    )(cols3, cols3, cols3, cols3, cols3, cols3, dy, dy, conv_w)


def _ret_consts():
    log_gamma = jnp.log1p(-jnp.exp2(-5.0 - jnp.arange(H_RET, dtype=F32)))
    pos = jnp.arange(CHUNK, dtype=F32)
    d_intra = jnp.exp(log_gamma[:, None, None] * jnp.abs(pos[:, None] - pos[None, :]))
    q_decay = jnp.exp(log_gamma[:, None] * (pos + 1.0))
    k_decay = jnp.exp(log_gamma[:, None] * (CHUNK - 1.0 - pos))
    chunk_decay = jnp.exp(log_gamma * CHUNK)
    wide = (H_RET, CHUNK, DK_RET)
    return (d_intra, jnp.broadcast_to(q_decay[:, :, None], wide), jnp.broadcast_to(k_decay[:, :, None], wide),
            jnp.broadcast_to(chunk_decay[:, None, None], (H_RET, 1, DK_RET)))


def _rope_tables(t):
    inv_freq = ROPE_BASE ** (-jnp.linspace(0.0, 1.0, DK_RET // 2, dtype=F32))
    ang = jnp.arange(t, dtype=F32)[:, None] * inv_freq[None, :]
    cos, sin = jnp.cos(ang), jnp.sin(ang)
    return jnp.concatenate([cos, cos], axis=1), jnp.concatenate([-sin, sin], axis=1)


def _rope(v, cc, ss):
    return v * cc + pltpu.roll(v, DK_RET // 2, axis=1) * ss


def _ret_in_specs(tb, blk):
    def col(s):
        return pl.BlockSpec((None, tb, H_RET * DK_RET), lambda i: (s, blk(i), 0))

    tab = pl.BlockSpec((tb, DK_RET), lambda i: (blk(i), 0))
    return ([col(3), col(4), col(5), col(6), tab, tab,
             pl.BlockSpec((H_RET, CHUNK, CHUNK), lambda i: (0, 0, 0)),
             pl.BlockSpec((H_RET, CHUNK, DK_RET), lambda i: (0, 0, 0)),
             pl.BlockSpec((H_RET, CHUNK, DK_RET), lambda i: (0, 0, 0)),
             pl.BlockSpec((H_RET, 1, DK_RET), lambda i: (0, 0, 0))])


def _ret_fwd(cols3, tables, consts, *, name):
    _, t, bw = cols3.shape
    tb = _tile(t, RET_TB)
    ncb = tb // CHUNK
    scale = DK_RET ** -0.5

    def body(q_ref, k_ref, v_ref, g_ref, cc_ref, ss_ref, di_ref, qd_ref, kd_ref, cd_ref, y_ref, st_ref, state):
        @pl.when(pl.program_id(0) == 0)
        def _():
            state[...] = jnp.zeros_like(state)

        for c in range(ncb):
            rows = pl.ds(c * CHUNK, CHUNK)
            cc, ss = cc_ref[rows, :], ss_ref[rows, :]
            for h in range(H_RET):
                lanes = pl.ds(h * DK_RET, DK_RET)
                qs = _rope(q_ref[rows, lanes].astype(F32), cc, ss) * scale
                kr = _rope(k_ref[rows, lanes].astype(F32), cc, ss)
                vb = v_ref[rows, lanes]
                gv = g_ref[rows, lanes].astype(F32)
                s_in = state[h]
                st_ref[h, c] = s_in
                inner = _dot_nt(_bf(qs), _bf(kr)) * di_ref[h]
                o = _dot(_bf(inner), vb) + _dot(_bf(qs * qd_ref[h]), _bf(s_in))
                state[h] = s_in * cd_ref[h] + _dot_tn(_bf(kr * kd_ref[h]), vb)
                on = o * lax.rsqrt(jnp.mean(o * o, axis=-1, keepdims=True) + EPS)
                y_ref[rows, lanes] = _bf(gv * _sigmoid(gv) * on)

    return pl.pallas_call(
        body, name=name, grid=(t // tb,),
        in_specs=_ret_in_specs(tb, lambda i: i),
        out_specs=[pl.BlockSpec((tb, bw), lambda i: (i, 0)),
                   pl.BlockSpec((H_RET, ncb, DK_RET, DK_RET), lambda i: (0, i, 0, 0))],
        out_shape=[jax.ShapeDtypeStruct((t, bw), BF16),
                   jax.ShapeDtypeStruct((H_RET, t // CHUNK, DK_RET, DK_RET), F32)],
        scratch_shapes=[pltpu.VMEM((H_RET, DK_RET, DK_RET), F32)],
        compiler_params=_cparams("arbitrary"),
    )(cols3, cols3, cols3, cols3, tables[0], tables[1], *consts)


def _ret_bwd(cols3, tables, consts, states, dy, *, name, exchange=None):
    _, t, bw = cols3.shape
    tb = _tile(t, RET_TB)
    ncb = tb // CHUNK
    nb = t // tb
    scale = DK_RET ** -0.5

    def body(q_ref, k_ref, v_ref, g_ref, cc_ref, ss_ref, di_ref, qd_ref, kd_ref, cd_ref, st_ref, dy_ref,
             d4_ref, dstate):
        @pl.when(pl.program_id(0) == 0)
        def _():
            dstate[...] = jnp.zeros_like(dstate)

        heads = range(H_RET)
        lanes_of = [pl.ds(h * DK_RET, DK_RET) for h in heads]
        di, qd, kd = [di_ref[h] for h in heads], [qd_ref[h] for h in heads], [kd_ref[h] for h in heads]
        carried = [dstate[h] for h in heads]
        for c in reversed(range(ncb)):
            rows = pl.ds(c * CHUNK, CHUNK)
            cc, ss = cc_ref[rows, :], ss_ref[rows, :]
            vb = [v_ref[rows, ln] for ln in lanes_of]
            gv = [g_ref[rows, ln].astype(F32) for ln in lanes_of]
            dyv = [dy_ref[rows, ln].astype(F32) for ln in lanes_of]
            s_in = [_bf(st_ref[h, c]) for h in heads]
            qs = [_rope(q_ref[rows, ln].astype(F32), cc, ss) * scale for ln in lanes_of]
            kr = [_rope(k_ref[rows, ln].astype(F32), cc, ss) for ln in lanes_of]
            qsb, krb = [_bf(v) for v in qs], [_bf(v) for v in kr]
            qdb = [_bf(qs[h] * qd[h]) for h in heads]
            kdb = [_bf(kr[h] * kd[h]) for h in heads]
            innerb = [_bf(_dot_nt(qsb[h], krb[h]) * di[h]) for h in heads]
            o = [_dot(innerb[h], vb[h]) + _dot(qdb[h], s_in[h]) for h in heads]
            r = [lax.rsqrt(jnp.mean(v * v, axis=-1, keepdims=True) + EPS) for v in o]
            on = [o[h] * r[h] for h in heads]
            sg = [_sigmoid(v) for v in gv]
            dgv = [_bf(dyv[h] * on[h] * (sg[h] * (1.0 + gv[h] * (1.0 - sg[h])))) for h in heads]
            don = [dyv[h] * (gv[h] * sg[h]) for h in heads]
            dob = [_bf(r[h] * (don[h] - on[h] * jnp.mean(don[h] * on[h], axis=-1, keepdims=True))) for h in heads]
            dinner = [_bf(_dot_nt(dob[h], vb[h]) * di[h]) for h in heads]
            dqs = [_dot(dinner[h], krb[h]) + _dot_nt(dob[h], s_in[h]) * qd[h] for h in heads]
            dkr = [_dot_tn(dinner[h], qsb[h]) for h in heads]
            dv = [_dot_tn(innerb[h], dob[h]) for h in heads]
            dnew = [_dot_tn(qdb[h], dob[h]) for h in heads]
            dstb = [_bf(v) for v in carried]
            dv = [dv[h] + _dot(kdb[h], dstb[h]) for h in heads]
            dkr = [dkr[h] + _dot_nt(vb[h], dstb[h]) * kd[h] for h in heads]
            carried = [carried[h] * cd_ref[h] + dnew[h] for h in heads]
            for h, ln in enumerate(lanes_of):
                d4_ref[0, rows, ln] = _bf(_rope_bwd(dqs[h] * scale, cc, ss))
                d4_ref[1, rows, ln] = _bf(_rope_bwd(dkr[h], cc, ss))
                d4_ref[2, rows, ln] = _bf(dv[h])
                d4_ref[3, rows, ln] = dgv[h]
        for h in heads:
            dstate[h] = carried[h]

    rev = lambda i: nb - 1 - i
    outs, moved = _pcall(
        body, (cols3, cols3, cols3, cols3, tables[0], tables[1], *consts, states, dy), name=name, grid=(nb,),
        in_specs=_ret_in_specs(tb, rev)
        + [pl.BlockSpec((H_RET, ncb, DK_RET, DK_RET), lambda i: (0, rev(i), 0, 0)),
           pl.BlockSpec((tb, bw), lambda i: (rev(i), 0))],
        out_specs=[pl.BlockSpec((4, tb, bw), lambda i: (0, rev(i), 0))],
        out_shape=[jax.ShapeDtypeStruct((4, t, bw), BF16)],
        scratch_shapes=[pltpu.VMEM((H_RET, DK_RET, DK_RET), F32)],
        sem=("arbitrary",), exchange=exchange)
    return outs[0], moved


def _rope_bwd(dv, cc, ss):
    return dv * cc + pltpu.roll(dv * ss, DK_RET // 2, axis=1)


def _att_window(i):
    return pl.multiple_of(jnp.maximum(i - ATT_LOOKBACK // ATT_QB, 0) * ATT_QB, ATT_QB)


def _att_mask(v):
    qchunk = (v * ATT_QB + lax.broadcasted_iota(jnp.int32, (ATT_QB, ATT_WIN), 0)) // CHUNK
    kchunk = lax.broadcasted_iota(jnp.int32, (ATT_QB, ATT_WIN), 1) // CHUNK
    return (kchunk <= qchunk) & (kchunk >= qchunk - N_PREV_CHUNKS)


def _att_fwd(cols3, bias3, *, name, exchange=None):
    _, t, bw = cols3.shape
    assert t % ATT_QB == 0 and t >= ATT_WIN
    scale = DH_ATT ** -0.5
    nvar = ATT_LOOKBACK // ATT_QB

    def body(q_ref, k_ref, v_ref, b_ref, y_ref, lse_ref):
        i = pl.program_id(1)
        ws = _att_window(i)
        q = q_ref[...].astype(F32)
        kw = k_ref[pl.ds(ws, ATT_WIN), :]
        vw = v_ref[pl.ds(ws, ATT_WIN), :]
        lane_head = lax.broadcasted_iota(jnp.int32, (ATT_QB, 128), 1) // DH_ATT
        out = jnp.zeros((ATT_QB, 128), F32)
        lse = jnp.zeros((ATT_QB, 128), F32)
        for hh in range(2):
            mine = lane_head == hh
            s = _dot_nt(_bf(jnp.where(mine, q, 0.0)), kw) * scale + b_ref[hh]
            mx = jnp.max(s, axis=-1, keepdims=True)
            p = jnp.exp(s - mx)
            l = jnp.sum(p, axis=-1, keepdims=True)
            out = jnp.where(mine, _dot(_bf(p), vw) / l, out)
            lse = jnp.where(mine, mx + jnp.log(l), lse)
        y_ref[...] = _bf(out)
        lse_ref[...] = lse

    kv = lambda s: pl.BlockSpec((None, t, 128), lambda hp, i: (s, 0, hp))
    return _pcall(
        body, (cols3, cols3, cols3, bias3), name=name, grid=(H_ATT // 2, t // ATT_QB),
        in_specs=[pl.BlockSpec((None, ATT_QB, 128), lambda hp, i: (7, i, hp)), kv(8), kv(9),
                  pl.BlockSpec((None, 2, ATT_QB, ATT_WIN), lambda hp, i: (jnp.minimum(i, nvar), hp, 0, 0))],
        out_specs=[pl.BlockSpec((ATT_QB, 128), lambda hp, i: (i, hp)),
                   pl.BlockSpec((None, ATT_QB, 128), lambda hp, i: (hp, i, 0))],
        out_shape=[jax.ShapeDtypeStruct((t, bw), BF16), jax.ShapeDtypeStruct((H_ATT // 2, t, 128), F32)],
        scratch_shapes=[], sem=("parallel", "arbitrary"), exchange=exchange)


def _att_bwd(cols3, bias3, y, lse, dy, *, name, exchange=None):
    _, t, bw = cols3.shape
    nq = t // ATT_QB
    scale = DH_ATT ** -0.5
    nvar = ATT_LOOKBACK // ATT_QB

    def body(q_ref, k_ref, v_ref, b_ref, y_ref, lse_ref, dy_ref, d3_ref, db_ref, dk_acc, dv_acc):
        i = pl.program_id(1)

        @pl.when(i == 0)
        def _():
            dk_acc[...] = jnp.zeros_like(dk_acc)
            dv_acc[...] = jnp.zeros_like(dv_acc)

        ws = _att_window(i)
        q = q_ref[...].astype(F32)
        kw = k_ref[pl.ds(ws, ATT_WIN), :]
        vw = v_ref[pl.ds(ws, ATT_WIN), :]
        do = dy_ref[...].astype(F32)
        dof = do * y_ref[...].astype(F32)
        lsev = lse_ref[...]
        lane_head = lax.broadcasted_iota(jnp.int32, (ATT_QB, 128), 1) // DH_ATT
        dq = jnp.zeros((ATT_QB, 128), F32)
        dk = jnp.zeros((ATT_WIN, 128), F32)
        dv = jnp.zeros((ATT_WIN, 128), F32)
        first = i <= nvar
        for hh in range(2):
            mine = lane_head == hh
            qh = _bf(jnp.where(mine, q, 0.0))
            doh = _bf(jnp.where(mine, do, 0.0))
            s = _dot_nt(qh, kw) * scale + b_ref[hh]
            lse_h = jnp.max(jnp.where(mine, lsev, NEG_INF), axis=-1, keepdims=True)
            p = jnp.exp(s - lse_h)
            delta = jnp.sum(jnp.where(mine, dof, 0.0), axis=-1, keepdims=True)
            ds = p * (_dot_nt(doh, vw) - delta)

            @pl.when(first)
            def _():
                db_ref[hh] = ds

            @pl.when(jnp.logical_not(first))
            def _():
                db_ref[hh] += ds

            dsb = _bf(ds * scale)
            dq = jnp.where(mine, _dot(dsb, kw), dq)
            dk = dk + _dot_tn(dsb, qh)
            dv = dv + _dot_tn(_bf(p), doh)
        d3_ref[0, pl.ds(pl.multiple_of(i * ATT_QB, ATT_QB), ATT_QB), :] = _bf(dq)
        dk_acc[pl.ds(ws, ATT_WIN), :] += dk
        dv_acc[pl.ds(ws, ATT_WIN), :] += dv

        @pl.when(i == nq - 1)
        def _():
            d3_ref[1] = _bf(dk_acc[...])
            d3_ref[2] = _bf(dv_acc[...])

    kv = lambda s: pl.BlockSpec((None, t, 128), lambda hp, i: (s, 0, hp))
    qrow = pl.BlockSpec((ATT_QB, 128), lambda hp, i: (i, hp))
    btile = pl.BlockSpec((None, 2, ATT_QB, ATT_WIN), lambda hp, i: (jnp.minimum(i, nvar), hp, 0, 0))
    return _pcall(
        body, (cols3, cols3, cols3, bias3, y, lse, dy), name=name, grid=(H_ATT // 2, nq),
        in_specs=[pl.BlockSpec((None, ATT_QB, 128), lambda hp, i: (7, i, hp)), kv(8), kv(9), btile, qrow,
                  pl.BlockSpec((None, ATT_QB, 128), lambda hp, i: (hp, i, 0)), qrow],
        out_specs=[pl.BlockSpec((3, t, 128), lambda hp, i: (0, 0, hp)), btile],
        out_shape=[jax.ShapeDtypeStruct((3, t, bw), BF16),
                   jax.ShapeDtypeStruct((nvar + 1, H_ATT, ATT_QB, ATT_WIN), F32)],
        scratch_shapes=[pltpu.VMEM((t, 128), F32), pltpu.VMEM((t, 128), F32)],
        sem=("parallel", "arbitrary"), exchange=exchange)


SKEW_W = ATT_WIN + ATT_QB
REL_PAD = 384


def _rel_onehot(v):
    r = lax.broadcasted_iota(jnp.int32, (REL_PAD, SKEW_W), 0)
    j = lax.broadcasted_iota(jnp.int32, (REL_PAD, SKEW_W), 1)
    dist = jnp.where(j < ATT_WIN, v * ATT_QB - j, v * ATT_QB + SKEW_W - j)
    col = jnp.clip(dist, -REL_CLIP, REL_CLIP) + REL_CLIP
    return _bf(jnp.where(col == r, 1.0, 0.0))


def _split3(v):
    hi = _bf(v)
    rest = v - hi.astype(F32)
    mid = _bf(rest)
    return hi, mid, _bf(rest - mid.astype(F32))


def _skew8(a, forward):
    row = lax.broadcasted_iota(jnp.int32, a.shape, 0)
    for b in range(3):
        shift = (1 << b) if forward else SKEW_W - (1 << b)
        a = jnp.where(((row >> b) & 1) == 1, pltpu.roll(a, shift, axis=1), a)
    return a


def _toeplitz_rows(ext_row):
    a = _skew8(jnp.broadcast_to(ext_row, (8, SKEW_W)), True)
    while a.shape[0] < ATT_QB:
        a = jnp.concatenate([a, pltpu.roll(a, a.shape[0], axis=1)], axis=0)
    return a


def _diagonal_sums(tile):
    a = tile
    while a.shape[0] > 8:
        half = a.shape[0] // 2
        a = a[:half] + pltpu.roll(a[half:], SKEW_W - half, axis=1)
    return jnp.sum(_skew8(a, False), axis=0, keepdims=True)


def _bias_tiles(rel_bias, *, name, exchange=None):
    depth = rel_bias.shape[0]
    nvar = ATT_LOOKBACK // ATT_QB + 1
    rel = jnp.pad(rel_bias, ((0, 0), (0, 0), (0, REL_PAD - N_REL)))

    def body(rel_ref, o_ref, ext_ref):
        v = pl.program_id(1)
        onehot = _rel_onehot(v)
        ext_ref[...] = sum(_dot(part, onehot) for part in _split3(rel_ref[...]))
        valid = _att_mask(v)
        for h in range(H_ATT):
            o_ref[h] = jnp.where(valid, _toeplitz_rows(ext_ref[h:h + 1, :])[:, :ATT_WIN], NEG_INF)

    outs, moved = _pcall(
        body, (rel,), name=name, grid=(depth, nvar),
        in_specs=[pl.BlockSpec((None, H_ATT, REL_PAD), lambda l, v: (l, 0, 0))],
        out_specs=[pl.BlockSpec((None, None, H_ATT, ATT_QB, ATT_WIN), lambda l, v: (l, v, 0, 0, 0))],
        out_shape=[jax.ShapeDtypeStruct((depth, nvar, H_ATT, ATT_QB, ATT_WIN), F32)],
        scratch_shapes=[pltpu.VMEM((H_ATT, SKEW_W), F32)], sem=("parallel", "parallel"), exchange=exchange)
    return outs[0], moved


def _rel_bias_grad(dbias3, *, name):
    nvar = dbias3.shape[0]

    def body(db_ref, o_ref, diag_ref):
        v = pl.program_id(0)
        for h in range(H_ATT):
            tile = jnp.concatenate([db_ref[h], jnp.zeros((ATT_QB, ATT_QB), F32)], axis=1)
            diag_ref[h:h + 1, :] = _diagonal_sums(tile)
        onehot = _rel_onehot(v)
        part = sum(_dot_nt(p, onehot) for p in _split3(diag_ref[...]))

        @pl.when(v == 0)
        def _():
            o_ref[...] = part

        @pl.when(v > 0)
        def _():
            o_ref[...] += part

    out = pl.pallas_call(
        body, name=name, grid=(nvar,),
        in_specs=[pl.BlockSpec((None, H_ATT, ATT_QB, ATT_WIN), lambda v: (v, 0, 0, 0))],
        out_specs=pl.BlockSpec((H_ATT, REL_PAD), lambda v: (0, 0)),
        out_shape=jax.ShapeDtypeStruct((H_ATT, REL_PAD), F32),
        scratch_shapes=[pltpu.VMEM((H_ATT, SKEW_W), F32)], compiler_params=_cparams("arbitrary"),
    )(dbias3)
    return out[:, :N_REL]


def _merge_fwd(h, ys, wmg, wb_t, *, name, tm=512):
    t, d = h.shape
    bw = ys[0].shape[1]
    tm = _tile(t, tm)

    def body(h_ref, y0_ref, y1_ref, y2_ref, wg_ref, wb_ref, m_ref, s_ref, p_ref):
        hv = h_ref[...]
        total = jnp.zeros((tm, d), F32)
        for b, y_ref in enumerate((y0_ref, y1_ref, y2_ref)):
            s = _sigmoid(_dot(hv, wg_ref[b]))
            p = _dot_nt(y_ref[...], wb_ref[b])
            s_ref[b] = _bf(s)
            p_ref[b] = _bf(p)
            total = total + s * p
        m_ref[...] = _bf(total)

    row = pl.BlockSpec((tm, d), lambda i: (i, 0))
    yrow = pl.BlockSpec((tm, bw), lambda i: (i, 0))
    three = pl.BlockSpec((3, tm, d), lambda i: (0, i, 0))
    return pl.pallas_call(
        body, name=name, grid=(t // tm,),
        in_specs=[row, yrow, yrow, yrow, pl.BlockSpec((3, d, d), lambda i: (0, 0, 0)),
                  pl.BlockSpec((3, d, bw), lambda i: (0, 0, 0))],
        out_specs=[row, three, three],
        out_shape=[jax.ShapeDtypeStruct((t, d), BF16), jax.ShapeDtypeStruct((3, t, d), BF16),
                   jax.ShapeDtypeStruct((3, t, d), BF16)],
        compiler_params=_cparams("parallel"),
    )(h, *ys, wmg, wb_t)


def _merge_bwd(dm, s3, p3, wmg, wb_t, *, name, tm=512):
    _, t, d = s3.shape
    bw = wb_t.shape[2]
    tm = _tile(t, tm)

    def body(dm_ref, s_ref, p_ref, wg_ref, wb_ref, dgp_ref, dp_ref, dy_ref, dh_ref):
        dmv = dm_ref[...].astype(F32)
        dh = jnp.zeros((tm, d), F32)
        for b in range(3):
            s = s_ref[b].astype(F32)
            dgp = _bf(dmv * p_ref[b].astype(F32) * s * (1.0 - s))
            dp = _bf(dmv * s)
            dgp_ref[b] = dgp
            dp_ref[b] = dp
            dy_ref[b] = _bf(_dot(dp, wb_ref[b]))
            dh = dh + _dot_nt(dgp, wg_ref[b])
        dh_ref[...] = dh

    row = pl.BlockSpec((tm, d), lambda i: (i, 0))
    three = pl.BlockSpec((3, tm, d), lambda i: (0, i, 0))
    return pl.pallas_call(
        body, name=name, grid=(t // tm,),
        in_specs=[row, three, three, pl.BlockSpec((3, d, d), lambda i: (0, 0, 0)),
                  pl.BlockSpec((3, d, bw), lambda i: (0, 0, 0))],
        out_specs=[three, three, pl.BlockSpec((3, tm, bw), lambda i: (0, i, 0)), row],
        out_shape=[jax.ShapeDtypeStruct((3, t, d), BF16), jax.ShapeDtypeStruct((3, t, d), BF16),
                   jax.ShapeDtypeStruct((3, t, bw), BF16), jax.ShapeDtypeStruct((t, d), F32)],
        compiler_params=_cparams("parallel"),
    )(dm, s3, p3, wmg, wb_t)


def _carried(hooks, slot, state=None):
    if not hooks or slot not in hooks:
        return None, lambda buf: None
    make, done = hooks[slot]
    return make(state), done


def _layer_fwd(x, p, aux, tag, hooks=None):
    ex, done = _carried(hooks, "ffn1")
    (x1, h1, g1, u1, a1), buf = _ffn_fwd(x, p["n1"], p["wg1"], p["wu1"], p["wd1"], name=f"ffn1_fwd_{tag}", exchange=ex)
    done(buf)
    h2 = _rmsnorm_fwd(x1, p["nmix"], name=f"mixnorm_fwd_{tag}")
    ex, done = _carried(hooks, "inproj")
    if ex is None:
        cols3 = _mm_to_slices(h2, p["win"], name=f"inproj_fwd_{tag}")
    else:
        cols3, buf = _mm_to_slices(h2, p["win"], name=f"inproj_fwd_{tag}", exchange=ex)
        done(buf)
    bias3 = p["bias3"]
    y_conv = _conv_fwd(cols3, p["conv_w"], name=f"conv_fwd_{tag}")
    y_ret, states = _ret_fwd(cols3, aux["rope"], aux["ret"], name=f"ret_fwd_{tag}")
    ex, done = _carried(hooks, "att")
    (y_att, lse), buf = _att_fwd(cols3, bias3, name=f"att_fwd_{tag}", exchange=ex)
    done(buf)
    merged, s3, p3 = _merge_fwd(h2, (y_conv, y_ret, y_att), p["wmg"], p["wb"], name=f"merge_fwd_{tag}")
    x2 = _mm_rows(merged, p["wout"], dims=NN, out_dtype=F32, res=x1, name=f"outproj_fwd_{tag}")
    ex, done = _carried(hooks, "ffn2")
    (x3, h3, g2, u2, a2), buf = _ffn_fwd(x2, p["n2"], p["wg2"], p["wu2"], p["wd2"], name=f"ffn2_fwd_{tag}", exchange=ex)
    done(buf)
    saved = dict(x0=x, h1=h1, g1=g1, u1=u1, a1=a1, x1=x1, h2=h2, cols3=cols3, bias3=bias3, y_conv=y_conv,
                 y_ret=y_ret, states=states, y_att=y_att, lse=lse, merged=merged, s3=s3, p3=p3, x2=x2, h3=h3,
                 g2=g2, u2=u2, a2=a2)
    return x3, saved


def _ffn_grads(dxo, x, h, g, u, a, p, which, tag, grads, all_grads, hooks):
    n = which
    ex, done = _carried(hooks, f"ffn{n}_bwd", all_grads)
    (dx, dyb, dg, du, grads["n" + n]), buf = _ffn_bwd(dxo, x, p["n" + n], g, u, p["wg" + n], p["wu" + n], p["wd" + n],
                                                     name=f"ffn{n}_{tag}_bwd", exchange=ex)
    done(buf)
    grads["wd" + n] = _mm_tn(a, dyb, name=f"ffn{n}_{tag}_dwd")
    for key, lhs in (("wg" + n, dg), ("wu" + n, du)):
        ex, done = _carried(hooks, f"ffn{n}_d{key[:2]}", all_grads)
        if ex is None:
            grads[key] = _mm_tn(lhs, h, name=f"ffn{n}_{tag}_d{key[:2]}")
        else:
            grads[key], buf = _mm_tn(lhs, h, name=f"ffn{n}_{tag}_d{key[:2]}", exchange=ex)
            done(buf)
    return dx


def _layer_bwd(dx3, p, s, aux, tag, grads, all_grads, hooks=None):
    dx2 = _ffn_grads(dx3, s["x2"], s["h3"], s["g2"], s["u2"], s["a2"], p, "2", tag, grads, all_grads, hooks)

    dm = _mm_rows(dx2, p["wout"], dims=NT, out_dtype=BF16, name=f"outproj_dm_{tag}")
    grads["wout"] = _mm_tn(s["merged"], dx2, name=f"outproj_dw_{tag}")
    dgp3, dp3, dy3, dh2 = _merge_bwd(dm, s["s3"], s["p3"], p["wmg"], p["wb"], name=f"merge_bwd_{tag}")
    grads["wmg"] = _mm_tn_batch(s["h2"], dgp3, a_batched=False, name=f"merge_dwg_{tag}")
    y3 = jnp.stack([s["y_conv"], s["y_ret"], s["y_att"]])
    grads["wb"] = _mm_tn_batch(dp3, y3, a_batched=True, name=f"merge_dwb_{tag}")

    dconv3, grads["conv_w"] = _conv_bwd(s["cols3"], p["conv_w"], dy3[0], name=f"conv_bwd_{tag}")
    ex, done = _carried(hooks, "ret_bwd", all_grads)
    dret4, buf = _ret_bwd(s["cols3"], aux["rope"], aux["ret"], s["states"], dy3[1], name=f"ret_bwd_{tag}", exchange=ex)
    done(buf)
    ex, done = _carried(hooks, "att_bwd", all_grads)
    (datt3, dbias3), buf = _att_bwd(s["cols3"], s["bias3"], s["y_att"], s["lse"], dy3[2], name=f"att_bwd_{tag}",
                                    exchange=ex)
    done(buf)
    grads["rel_bias"] = _rel_bias_grad(dbias3, name=f"bias_grad_{tag}")

    dh2 = _inproj_dh((dconv3, dret4, datt3), p["win"], dh2, name=f"inproj_dh_{tag}")
    grads["win"] = jnp.concatenate([
        _mm_tn_slices(dconv3, s["h2"], name=f"inproj_dw_conv_{tag}"),
        _mm_tn_slices(dret4, s["h2"], name=f"inproj_dw_ret_{tag}"),
        _mm_tn_slices(datt3, s["h2"], name=f"inproj_dw_att_{tag}")], axis=0)
    dx1, grads["nmix"] = _rmsnorm_bwd(dh2, s["x1"], p["nmix"], dx2, name=f"mixnorm_bwd_{tag}")

    return _ffn_grads(dx1, s["x0"], s["h1"], s["g1"], s["u1"], s["a1"], p, "1", tag, grads, all_grads, hooks)


def _device_step(x, target, layers, final_norm, fwd_hooks=None, bwd_hooks=None):
    t = x.shape[0]
    depth = len(layers)
    aux = dict(rope=_rope_tables(t), ret=_ret_consts())
    ex, done = _carried(fwd_hooks[0] if fwd_hooks else None, "start")
    bias_all, buf = _bias_tiles(jnp.stack([p["rel_bias"] for p in layers]), name="bias_tiles", exchange=ex)
    done(buf)
    saved = []
    for l, p in enumerate(layers):
        p["bias3"] = bias_all[l]
        x, s = _layer_fwd(x, p, aux, f"l{l}", fwd_hooks[l] if fwd_hooks else None)
        saved.append(s)
    dx, dfinal, loss_row = _loss_fwd_bwd(x, final_norm, target, name="loss_fwd_bwd")
    grads = [dict() for _ in range(depth)]
    for l in reversed(range(depth)):
        dx = _layer_bwd(dx, layers[l], saved[l], aux, f"l{l}", grads[l], grads, bwd_hooks[l] if bwd_hooks else None)
    return loss_row, dx, grads, dfinal


def _sum_slots(parts, *, name, tr=512):
    n, r, cdim = parts.shape
    tr = _tile(r, tr)

    def body(p_ref, o_ref):
        acc = p_ref[0].astype(F32)
        for s in range(1, n):
            acc = acc + p_ref[s].astype(F32)
        o_ref[...] = acc

    return pl.pallas_call(
        body, name=name, grid=(r // tr,), in_specs=[pl.BlockSpec((n, tr, cdim), lambda i: (0, i, 0))],
        out_specs=pl.BlockSpec((tr, cdim), lambda i: (i, 0)), out_shape=jax.ShapeDtypeStruct((r, cdim), F32),
        compiler_params=_cparams("parallel"),
    )(parts)


def _all_reduce_small(v, *, name):
    r = v.shape[0]

    def body(x_ref, o_ref, slots, send_sems, recv_sems):
        x, y, c, me = _my_position()
        slots[me] = x_ref[...]
        sends = []
        for k in range(1, N_DEV):
            peer, _ = _peer(x, y, c, k)
            cp = pltpu.make_async_remote_copy(src_ref=x_ref, dst_ref=slots.at[me], send_sem=send_sems.at[k - 1],
                                              recv_sem=recv_sems.at[k - 1], device_id=peer, device_id_type=MESH)
            cp.start()
            sends.append(cp)
        for k in range(1, N_DEV):
            peer, peer_id = _peer(x, y, c, k)
            pltpu.make_async_remote_copy(src_ref=x_ref, dst_ref=slots.at[peer_id], send_sem=send_sems.at[k - 1],
                                         recv_sem=recv_sems.at[k - 1], device_id=peer, device_id_type=MESH).wait_recv()
        for cp in sends:
            cp.wait_send()
        acc = slots[0]
        for s in range(1, N_DEV):
            acc = acc + slots[s]
        o_ref[...] = acc

    return pl.pallas_call(
        body, name=name, in_specs=[pl.BlockSpec(memory_space=pltpu.VMEM)],
        out_specs=pl.BlockSpec(memory_space=pltpu.VMEM), out_shape=jax.ShapeDtypeStruct((r, 128), F32),
        scratch_shapes=[pltpu.VMEM((N_DEV, r, 128), F32), pltpu.SemaphoreType.DMA((N_DEV - 1,)),
                        pltpu.SemaphoreType.DMA((N_DEV - 1,))],
    )(v)


def _adamw(w, g, m, v, *, name, tr=256):
    shape = w.shape
    cdim = shape[-1]
    w2, g2, m2, v2 = (a.reshape(-1, cdim) for a in (w, g, m, v))
    r = w2.shape[0]
    tr = _tile(r, tr) if r % 8 == 0 else r
    c1 = 1.0 - ADAM_B1 ** ADAM_STEP
    c2 = 1.0 - ADAM_B2 ** ADAM_STEP

    def body(w_ref, g_ref, m_ref, v_ref, d_ref, mo_ref, vo_ref):
        gv = g_ref[...]
        mn = ADAM_B1 * m_ref[...] + (1.0 - ADAM_B1) * gv
        vn = ADAM_B2 * v_ref[...] + (1.0 - ADAM_B2) * (gv * gv)
        d_ref[...] = -ADAM_LR * ((mn / c1) / (jnp.sqrt(vn / c2) + ADAM_EPS) + ADAM_WD * w_ref[...])
        mo_ref[...] = mn
        vo_ref[...] = vn

    spec = pl.BlockSpec((tr, cdim), lambda i: (i, 0))
    outs = pl.pallas_call(
        body, name=name, grid=(r // tr,), in_specs=[spec] * 4, out_specs=[spec] * 3,
        out_shape=[jax.ShapeDtypeStruct((r, cdim), F32)] * 3, compiler_params=_cparams("parallel"),
    )(w2, g2, m2, v2)
    return tuple(o.reshape(shape) for o in outs)


GROUPS = {"f1": ("wg1", "wu1", "wd1"), "in": ("win",), "mg": ("wb", "wmg", "wout"), "f2": ("wg2", "wu2", "wd2")}
BIG = tuple(nm for members in GROUPS.values() for nm in members)


def _to_rows(name, w, d):
    if name in ("wg1", "wu1", "wg2", "wu2", "win"):
        return w.T
    if name == "wb":
        return w.transpose(0, 2, 1).reshape(-1, d)
    if name == "wmg":
        return w.reshape(-1, d)
    return w


def _from_rows(name, rows, d):
    if name in ("wg1", "wu1", "wg2", "wu2", "win"):
        return rows.T
    if name == "wb":
        return rows.reshape(3, -1, BRANCH_W).transpose(0, 2, 1)
    if name == "wmg":
        return rows.reshape(3, -1, d)
    return rows


def _full_from_gathered(name, g, d):
    if name == "wb":
        return g.reshape(N_DEV, 3, -1, BRANCH_W).transpose(1, 0, 2, 3).reshape(3, d, BRANCH_W)
    if name == "wmg":
        return g.reshape(N_DEV, 3, -1, d).transpose(1, 0, 2, 3).reshape(3, d, d)
    return g.reshape(-1, d)


def _gathered_from_full(name, full, d):
    if name == "wb":
        return full.reshape(3, N_DEV, -1, BRANCH_W).transpose(1, 0, 2, 3).reshape(N_DEV, -1, d)
    if name == "wmg":
        return full.reshape(3, N_DEV, -1, d).transpose(1, 0, 2, 3).reshape(N_DEV, -1, d)
    return full.reshape(N_DEV, -1, d)


def kernel(x, ffn1_norm, ffn1_w_gate, ffn1_w_up, ffn1_w_down, mix_norm, w_in, conv_w, rel_bias, w_branch, w_merge_gate, w_out, ffn2_norm, ffn2_w_gate, ffn2_w_up, ffn2_w_down, final_norm, loss_target, m_ffn1_norm, m_ffn1_w_gate, m_ffn1_w_up, m_ffn1_w_down, m_mix_norm, m_w_in, m_conv_w, m_rel_bias, m_w_branch, m_w_merge_gate, m_w_out, m_ffn2_norm, m_ffn2_w_gate, m_ffn2_w_up, m_ffn2_w_down, m_final_norm, v_ffn1_norm, v_ffn1_w_gate, v_ffn1_w_up, v_ffn1_w_down, v_mix_norm, v_w_in, v_conv_w, v_rel_bias, v_w_branch, v_w_merge_gate, v_w_out, v_ffn2_norm, v_ffn2_w_gate, v_ffn2_w_up, v_ffn2_w_down, v_final_norm):
    names = ["ffn1_norm", "ffn1_w_gate", "ffn1_w_up", "ffn1_w_down", "mix_norm", "w_in", "conv_w", "rel_bias",
             "w_branch", "w_merge_gate", "w_out", "ffn2_norm", "ffn2_w_gate", "ffn2_w_up", "ffn2_w_down", "final_norm"]
    weights = dict(zip(names, (ffn1_norm, ffn1_w_gate, ffn1_w_up, ffn1_w_down, mix_norm, w_in, conv_w, rel_bias,
                               w_branch, w_merge_gate, w_out, ffn2_norm, ffn2_w_gate, ffn2_w_up, ffn2_w_down,
                               final_norm)))
    m_in = dict(zip(names, (m_ffn1_norm, m_ffn1_w_gate, m_ffn1_w_up, m_ffn1_w_down, m_mix_norm, m_w_in, m_conv_w,
                            m_rel_bias, m_w_branch, m_w_merge_gate, m_w_out, m_ffn2_norm, m_ffn2_w_gate,
                            m_ffn2_w_up, m_ffn2_w_down, m_final_norm)))
    v_in = dict(zip(names, (v_ffn1_norm, v_ffn1_w_gate, v_ffn1_w_up, v_ffn1_w_down, v_mix_norm, v_w_in, v_conv_w,
                            v_rel_bias, v_w_branch, v_w_merge_gate, v_w_out, v_ffn2_norm, v_ffn2_w_gate,
                            v_ffn2_w_up, v_ffn2_w_down, v_final_norm)))
    big_of = dict(wg1="ffn1_w_gate", wu1="ffn1_w_up", wd1="ffn1_w_down", win="w_in", wb="w_branch",
                  wmg="w_merge_gate", wout="w_out", wg2="ffn2_w_gate", wu2="ffn2_w_up", wd2="ffn2_w_down")
    depth = ffn1_norm.shape[0]
    d = x.shape[-1]
    xs = x.reshape(-1, d)
    target = loss_target.reshape(-1, d)
    _, _, _, me = _my_position()

    rows_of = {nm: _to_rows(nm, weights[big_of[nm]][0], d).shape[0] for nm in BIG}

    def layout(members):
        starts = [sum(rows_of[m] for m in members[:i]) for i in range(len(members))]
        return [(nm, r0, rows_of[nm]) for nm, r0 in zip(members, starts)]

    def pack_shards(l, members):
        return jnp.concatenate([_to_rows(nm, weights[big_of[nm]][l], d).astype(BF16) for nm in members], axis=0)

    def unpack(members, gathered):
        return {nm: _full_from_gathered(nm, gathered[:, r0:r0 + nr], d) for nm, r0, nr in layout(members)}

    conv_cols = conv_w.shape[-1]
    conv_full = _exchange("gather", conv_w.reshape(depth * 3, conv_cols), name="gather_conv_w")
    conv_full = conv_full.reshape(N_DEV, depth, 3, conv_cols).transpose(1, 2, 0, 3).reshape(depth, 3, -1)
    layers = [dict(n1=ffn1_norm[l][None], nmix=mix_norm[l][None], n2=ffn2_norm[l][None], conv_w=conv_full[l],
                   rel_bias=rel_bias[l]) for l in range(depth)]

    def gather_hook(l, grp):
        members = GROUPS[grp]
        return (lambda _: ("gather", pack_shards(l, members))), (lambda buf: layers[l].update(unpack(members, buf)))

    fwd_hooks = []
    for l in range(depth):
        hooks = {"ffn1": gather_hook(l, "in"), "inproj": gather_hook(l, "mg"), "att": gather_hook(l, "f2")}
        if l == 0:
            hooks["start"] = gather_hook(0, "f1")
        if l + 1 < depth:
            hooks["ffn2"] = gather_hook(l + 1, "f1")
        fwd_hooks.append(hooks)

    received = {}

    def pack_grads(g, members):
        return jnp.concatenate([_gathered_from_full(nm, g[nm], d).astype(BF16) for nm in members], axis=1)

    def scatter_hook(l, members):
        return ((lambda all_grads: ("scatter", pack_grads(all_grads[l], members))),
                (lambda buf: received.update({(l, members): buf})))

    bwd_hooks = []
    for l in range(depth):
        hooks = {"att_bwd": scatter_hook(l, GROUPS["f2"]), "ret_bwd": scatter_hook(l, GROUPS["mg"]),
                 "ffn1_bwd": scatter_hook(l, GROUPS["in"])}
        if l + 1 < depth:
            hooks["ffn2_bwd"] = scatter_hook(l + 1, GROUPS["f1"])
        if l == 0:
            hooks["ffn1_dwg"] = scatter_hook(0, ("wd1",))
            hooks["ffn1_dwu"] = scatter_hook(0, ("wg1",))
        bwd_hooks.append(hooks)

    loss_row, dx, grads, dfinal = _device_step(xs, target, layers, final_norm[None], fwd_hooks, bwd_hooks)
    received[(0, ("wu1",))] = _exchange("scatter", pack_grads(grads[0], ("wu1",)), name="scatter_l0_wu1")

    big_grads = {nm: [None] * depth for nm in BIG}
    for (l, members), buf in received.items():
        shard_grad = _sum_slots(buf, name=f"sum_grads_l{l}_{'_'.join(members)}")
        for nm, r0, nr in layout(members):
            big_grads[nm][l] = _from_rows(nm, shard_grad[r0:r0 + nr], d)

    small = {"ffn1_norm": jnp.stack([grads[l]["n1"][0] for l in range(depth)]),
             "mix_norm": jnp.stack([grads[l]["nmix"][0] for l in range(depth)]),
             "ffn2_norm": jnp.stack([grads[l]["n2"][0] for l in range(depth)]),
             "final_norm": dfinal[0],
             "rel_bias": jnp.stack([grads[l]["rel_bias"] for l in range(depth)]),
             "conv_w": jnp.stack([grads[l]["conv_w"] for l in range(depth)]),
             "loss": loss_row[0, :1]}
    order = list(small)
    flat = jnp.concatenate([small[k].reshape(-1) for k in order])
    pad = (-flat.shape[0]) % 1024
    summed = _all_reduce_small(jnp.pad(flat, (0, pad)).reshape(-1, 128), name="reduce_small").reshape(-1)
    pos = 0
    for k in order:
        n = small[k].size
        small[k] = summed[pos:pos + n].reshape(small[k].shape)
        pos += n
    small["conv_w"] = lax.dynamic_slice_in_dim(small["conv_w"], me * conv_cols, conv_cols, axis=2)

    grad_w = {big_of[nm]: jnp.stack(big_grads[nm]) for nm in BIG}
    grad_w.update({k: small[k] for k in order if k != "loss"})
    delta, new_m, new_v = {}, {}, {}
    for nm in names:
        delta[nm], new_m[nm], new_v[nm] = _adamw(weights[nm], grad_w[nm], m_in[nm], v_in[nm], name=f"adamw_{nm}")
    return (small["loss"].reshape(()), dx.reshape(x.shape), *[grad_w[n] for n in names], *[delta[n] for n in names],
            *[new_m[n] for n in names], *[new_v[n] for n in names])
```

```python
import functools
import math

import jax
import jax.numpy as jnp
from jax import lax
from jax.experimental import pallas as pl
from jax.experimental.pallas import tpu as pltpu

F32 = jnp.float32
BF16 = jnp.bfloat16

N_DEV = 8
EPS = 1e-6
CHUNK = 64
BRANCH_W = 512
N_SLICES = 10
H_RET = 4
DK_RET = 128
H_ATT = 8
DH_ATT = 64
N_PREV_CHUNKS = 8
REL_CLIP = 128
N_REL = 2 * REL_CLIP + 1
NEG_INF = -1e30
ROPE_BASE = 10000.0
ATT_QB = 256
ATT_LOOKBACK = N_PREV_CHUNKS * CHUNK
ATT_WIN = ATT_LOOKBACK + ATT_QB
RET_TB = 256
TOKEN_TK = 2048
PARTIAL_DTYPE = jnp.bfloat16
CONV_HALO = 16

ADAM_LR = 0.001
ADAM_B1 = 0.9
ADAM_B2 = 0.999
ADAM_EPS = 1e-08
ADAM_WD = 0.01
ADAM_STEP = 10

VMEM_LIMIT_BYTES = 56 * 1024 * 1024
MESH = pl.DeviceIdType.MESH


def _cparams(*sem):
    return pltpu.CompilerParams(dimension_semantics=sem, vmem_limit_bytes=VMEM_LIMIT_BYTES)


def _dot(a, b):
    return lax.dot_general(a, b, (((1,), (0,)), ((), ())), preferred_element_type=F32)


def _dot_nt(a, b):
    return lax.dot_general(a, b, (((1,), (1,)), ((), ())), preferred_element_type=F32)


def _dot_tn(a, b):
    return lax.dot_general(a, b, (((0,), (0,)), ((), ())), preferred_element_type=F32)


def _bf(v):
    return v.astype(BF16)


def _sigmoid(v):
    return 1.0 / (1.0 + jnp.exp(-v))


def _tile(n, want):
    if n <= want:
        return n
    for t in range(want - want % 128, 0, -128):
        if n % t == 0:
            return t
    t = want
    while n % t:
        t //= 2
    return t


def _my_position():
    x, y, c = lax.axis_index("x"), lax.axis_index("y"), lax.axis_index("c")
    return x, y, c, 4 * x + 2 * y + c


def _peer(x, y, c, k):
    px = 1 - x if k & 4 else x
    py = 1 - y if k & 2 else y
    pc = 1 - c if k & 1 else c
    return (px, py, pc), 4 * px + 2 * py + pc


def _exchange_sems(n):
    return [pltpu.SemaphoreType.DMA((n, N_DEV - 1)), pltpu.SemaphoreType.DMA((n, N_DEV - 1)),
            pltpu.SemaphoreType.DMA((n,))]


def _exchange_src(kind, src_ref, layer, block):
    if kind == "gather":
        return src_ref if layer is None else src_ref.at[layer]
    return src_ref.at[block]


def _exchange_copy(kind, src_ref, layer, dst_ref, sems, a, k, pos, incoming):
    x, y, c, me = pos
    peer, peer_id = _peer(x, y, c, k)
    return pltpu.make_async_remote_copy(src_ref=_exchange_src(kind, src_ref, layer, me if incoming else peer_id),
                                        dst_ref=dst_ref.at[peer_id if incoming else me],
                                        send_sem=sems[0].at[a, k - 1], recv_sem=sems[1].at[a, k - 1],
                                        device_id=peer, device_id_type=MESH)


def _exchange_local(kind, src_ref, layer, dst_ref, sems, a, me):
    return pltpu.make_async_copy(_exchange_src(kind, src_ref, layer, me), dst_ref.at[me], sems[2].at[a])


def _exchange_start(kind, layers, src_refs, dst_refs, sems):
    pos = _my_position()
    for a, (src_ref, layer, dst_ref) in enumerate(zip(src_refs, layers, dst_refs)):
        _exchange_local(kind, src_ref, layer, dst_ref, sems, a, pos[3]).start()
        for k in range(1, N_DEV):
            _exchange_copy(kind, src_ref, layer, dst_ref, sems, a, k, pos, False).start()


def _exchange_wait(kind, layers, src_refs, dst_refs, sems):
    pos = _my_position()
    for a, (src_ref, layer, dst_ref) in enumerate(zip(src_refs, layers, dst_refs)):
        for k in range(1, N_DEV):
            _exchange_copy(kind, src_ref, layer, dst_ref, sems, a, k, pos, True).wait_recv()
        for k in range(1, N_DEV):
            _exchange_copy(kind, src_ref, layer, dst_ref, sems, a, k, pos, False).wait_send()
        _exchange_local(kind, src_ref, layer, dst_ref, sems, a, pos[3]).wait()


def _exchange_shapes(kind, srcs, layers):
    if kind == "gather":
        return [jax.ShapeDtypeStruct((N_DEV,) + (s.shape if l is None else s.shape[1:]), s.dtype)
                for s, l in zip(srcs, layers)]
    return [jax.ShapeDtypeStruct(s.shape, s.dtype) for s in srcs]


def _exchange(kind, srcs, layers=None, *, name):
    n = len(srcs)
    layers = layers or [None] * n
    hbm = pl.BlockSpec(memory_space=pl.ANY)

    def body(*refs):
        src_refs, dst_refs, sems = refs[:n], refs[n:2 * n], refs[2 * n:]
        _exchange_start(kind, layers, src_refs, dst_refs, sems)
        _exchange_wait(kind, layers, src_refs, dst_refs, sems)

    return pl.pallas_call(body, name=name, in_specs=[hbm] * n, out_specs=[hbm] * n,
                          out_shape=_exchange_shapes(kind, srcs, layers), scratch_shapes=_exchange_sems(n))(*srcs)


def _pcall(body, args, *, name, grid, in_specs, out_specs, out_shape, scratch_shapes, sem, exchange=None):
    if exchange is None:
        outs = pl.pallas_call(body, name=name, grid=grid, in_specs=in_specs, out_specs=out_specs, out_shape=out_shape,
                              scratch_shapes=scratch_shapes, compiler_params=_cparams(*sem))(*args)
        return outs, None
    kind, srcs, layers = exchange
    n_in, n_out, n_scr, n_x = len(in_specs), len(out_specs), len(scratch_shapes), len(srcs)
    hbm = pl.BlockSpec(memory_space=pl.ANY)

    def carrier(*refs):
        ins, refs = refs[:n_in], refs[n_in:]
        src_refs, refs = refs[:n_x], refs[n_x:]
        outs, refs = refs[:n_out], refs[n_out:]
        dst_refs, refs = refs[:n_x], refs[n_x:]
        scr, sems = refs[:n_scr], refs[n_scr:]
        ids = [pl.program_id(a) for a in range(len(grid))]
        first = functools.reduce(jnp.logical_and, [i == 0 for i in ids])
        last = functools.reduce(jnp.logical_and, [i == g - 1 for i, g in zip(ids, grid)])

        @pl.when(first)
        def _():
            _exchange_start(kind, layers, src_refs, dst_refs, sems)

        body(*ins, *outs, *scr)

        @pl.when(last)
        def _():
            _exchange_wait(kind, layers, src_refs, dst_refs, sems)

    outs = pl.pallas_call(
        carrier, name=name, grid=grid, in_specs=list(in_specs) + [hbm] * n_x, out_specs=list(out_specs) + [hbm] * n_x,
        out_shape=list(out_shape) + _exchange_shapes(kind, srcs, layers),
        scratch_shapes=list(scratch_shapes) + _exchange_sems(n_x),
        compiler_params=_cparams(*(["arbitrary"] * len(grid))))(*args, *srcs)
    return outs[:n_out], outs[n_out:]


def _matmul(a, b, *, dims, grid, a_spec, b_spec, o_spec, out_shape, acc_shape, name, scale=1.0,
            res=None, res_spec=None, exchange=None):
    nk = grid[3]
    has_res = res is not None

    def body(*refs):
        if has_res:
            a_ref, b_ref, r_ref, o_ref = refs[:4]
        else:
            a_ref, b_ref, o_ref = refs[:3]
            r_ref = None

        def finish(acc):
            if scale != 1.0:
                acc = acc * scale
            if r_ref is not None:
                acc = acc + r_ref[...].astype(F32)
            o_ref[...] = acc.astype(o_ref.dtype)

        part = lax.dot_general(_bf(a_ref[...]), _bf(b_ref[...]), (dims, ((), ())), preferred_element_type=F32)
        if nk == 1:
            finish(part)
        else:
            acc_ref = refs[-1]
            k = pl.program_id(3)

            @pl.when(k == 0)
            def _():
                acc_ref[...] = part

            @pl.when(jnp.logical_and(k > 0, k < nk - 1))
            def _():
                acc_ref[...] += part

            @pl.when(k == nk - 1)
            def _():
                finish(acc_ref[...] + part)

    in_specs = [a_spec, b_spec] + ([res_spec] if has_res else [])
    args = (a, b) + ((res,) if has_res else ())
    outs, moved = _pcall(
        body, args, name=name, grid=grid, in_specs=in_specs, out_specs=[o_spec], out_shape=[out_shape],
        scratch_shapes=[] if nk == 1 else [pltpu.VMEM(acc_shape, F32)],
        sem=("parallel", "parallel", "parallel", "arbitrary"), exchange=exchange)
    return outs[0] if exchange is None else (outs[0], moved)


NT = ((1,), (1,))
NN = ((1,), (0,))
TN = ((0,), (0,))


def _mm_rows(a, b, *, dims, out_dtype, name, res=None, scale=1.0, tm=1024, tn=1024, tk=1024):
    m, kdim = a.shape
    n = b.shape[1] if dims == NN else b.shape[0]
    tm, tn, tk = _tile(m, tm), _tile(n, tn), _tile(kdim, tk)
    grid = (1, m // tm, n // tn, kdim // tk)
    a_spec = pl.BlockSpec((tm, tk), lambda s, i, j, k: (i, k))
    if dims == NN:
        b_spec = pl.BlockSpec((tk, tn), lambda s, i, j, k: (k, j))
    else:
        b_spec = pl.BlockSpec((tn, tk), lambda s, i, j, k: (j, k))
    o_spec = pl.BlockSpec((tm, tn), lambda s, i, j, k: (i, j))
    return _matmul(a, b, dims=dims, grid=grid, a_spec=a_spec, b_spec=b_spec, o_spec=o_spec,
                   out_shape=jax.ShapeDtypeStruct((m, n), out_dtype), acc_shape=(tm, tn), name=name,
                   res=res, res_spec=o_spec if res is not None else None, scale=scale)


def _mm_to_slices(a, b_t, *, name, tm=1024, exchange=None):
    m, kdim = a.shape
    n = b_t.shape[0]
    tm = _tile(m, tm)
    w = BRANCH_W
    grid = (1, m // tm, n // w, 1)
    return _matmul(a, b_t, dims=NT, grid=grid,
                   a_spec=pl.BlockSpec((tm, kdim), lambda s, i, j, k: (i, 0)),
                   b_spec=pl.BlockSpec((w, kdim), lambda s, i, j, k: (j, 0)),
                   o_spec=pl.BlockSpec((None, tm, w), lambda s, i, j, k: (j, i, 0)),
                   out_shape=jax.ShapeDtypeStruct((n // w, m, w), BF16), acc_shape=(tm, w), name=name,
                   exchange=exchange)


def _inproj_dh(parts, w_t, res, *, name, tm=512):
    t, n = res.shape
    w = BRANCH_W
    tm = _tile(t, tm)
    counts = [p.shape[0] for p in parts]
    assert sum(counts) * w == w_t.shape[0]

    def body(*refs):
        w_ref, r_ref, o_ref = refs[len(parts):]
        cols = jnp.concatenate([p_ref[s] for p_ref, cnt in zip(refs, counts) for s in range(cnt)], axis=1)
        o_ref[...] = r_ref[...] + _dot(cols, w_ref[...])

    row = pl.BlockSpec((tm, n), lambda i: (i, 0))
    return pl.pallas_call(
        body, name=name, grid=(t // tm,),
        in_specs=[pl.BlockSpec((cnt, tm, w), lambda i: (0, i, 0)) for cnt in counts]
        + [pl.BlockSpec(w_t.shape, lambda i: (0, 0)), row],
        out_specs=row, out_shape=jax.ShapeDtypeStruct((t, n), F32), compiler_params=_cparams("parallel"),
    )(*parts, w_t, res)


def _mm_tn(a, b, *, name, tm=1408, tn=1024, tk=TOKEN_TK, exchange=None):
    t, m = a.shape
    n = b.shape[1]
    tm, tn, tk = _tile(m, tm), _tile(n, tn), _tile(t, tk)
    grid = (1, m // tm, n // tn, t // tk)
    return _matmul(a, b, dims=TN, grid=grid,
                   a_spec=pl.BlockSpec((tk, tm), lambda s, i, j, k: (k, i)),
                   b_spec=pl.BlockSpec((tk, tn), lambda s, i, j, k: (k, j)),
                   o_spec=pl.BlockSpec((tm, tn), lambda s, i, j, k: (i, j)),
                   out_shape=jax.ShapeDtypeStruct((m, n), PARTIAL_DTYPE), acc_shape=(tm, tn), name=name,
                   exchange=exchange)


def _mm_tn_slices(a3, b, *, name, tn=1024, tk=TOKEN_TK):
    s_n, t, w = a3.shape
    n = b.shape[1]
    tn, tk = _tile(n, tn), _tile(t, tk)
    grid = (1, s_n, n // tn, t // tk)
    return _matmul(a3, b, dims=TN, grid=grid,
                   a_spec=pl.BlockSpec((None, tk, w), lambda s, i, j, k: (i, k, 0)),
                   b_spec=pl.BlockSpec((tk, tn), lambda s, i, j, k: (k, j)),
                   o_spec=pl.BlockSpec((w, tn), lambda s, i, j, k: (i, j)),
                   out_shape=jax.ShapeDtypeStruct((s_n * w, n), PARTIAL_DTYPE), acc_shape=(w, tn), name=name)


def _mm_tn_batch(a, b3, *, name, a_batched, tm=1024, tn=1024, tk=TOKEN_TK):
    s_n, t, n = b3.shape
    m = a.shape[-1]
    tm, tn, tk = _tile(m, tm), _tile(n, tn), _tile(t, tk)
    grid = (s_n, m // tm, n // tn, t // tk)
    if a_batched:
        a_spec = pl.BlockSpec((None, tk, tm), lambda s, i, j, k: (s, k, i))
    else:
        a_spec = pl.BlockSpec((tk, tm), lambda s, i, j, k: (k, i))
    return _matmul(a, b3, dims=TN, grid=grid, a_spec=a_spec,
                   b_spec=pl.BlockSpec((None, tk, tn), lambda s, i, j, k: (s, k, j)),
                   o_spec=pl.BlockSpec((None, tm, tn), lambda s, i, j, k: (s, i, j)),
                   out_shape=jax.ShapeDtypeStruct((s_n, m, n), PARTIAL_DTYPE), acc_shape=(tm, tn), name=name)


def _norm_parts(xf):
    r = lax.rsqrt(jnp.mean(xf * xf, axis=-1, keepdims=True) + EPS)
    return xf * r, r


def _norm_bwd(dh, xhat, r, w):
    dxhat = dh * w
    dx = r * (dxhat - xhat * jnp.mean(dxhat * xhat, axis=-1, keepdims=True))
    return dx, jnp.sum(dh * xhat, axis=0, keepdims=True)


def _rmsnorm_fwd(x, w, *, name, tm=1024):
    t, d = x.shape
    tm = _tile(t, tm)

    def body(x_ref, w_ref, h_ref):
        xhat, _ = _norm_parts(x_ref[...])
        h_ref[...] = _bf(xhat * w_ref[...])

    return pl.pallas_call(
        body, name=name, grid=(t // tm,),
        in_specs=[pl.BlockSpec((tm, d), lambda i: (i, 0)), pl.BlockSpec((1, d), lambda i: (0, 0))],
        out_specs=pl.BlockSpec((tm, d), lambda i: (i, 0)),
        out_shape=jax.ShapeDtypeStruct((t, d), BF16), compiler_params=_cparams("parallel"),
    )(x, w)


def _rmsnorm_bwd(dh, x, w, dres, *, name, tm=1024):
    t, d = x.shape
    tm = _tile(t, tm)

    def body(dh_ref, x_ref, w_ref, dres_ref, dx_ref, dw_ref):
        xhat, r = _norm_parts(x_ref[...])
        dx, dw = _norm_bwd(dh_ref[...], xhat, r, w_ref[...])
        dx_ref[...] = dres_ref[...] + dx

        @pl.when(pl.program_id(0) == 0)
        def _():
            dw_ref[...] = dw

        @pl.when(pl.program_id(0) > 0)
        def _():
            dw_ref[...] += dw

    row = pl.BlockSpec((tm, d), lambda i: (i, 0))
    vec = pl.BlockSpec((1, d), lambda i: (0, 0))
    return pl.pallas_call(
        body, name=name, grid=(t // tm,), in_specs=[row, row, vec, row], out_specs=[row, vec],
        out_shape=[jax.ShapeDtypeStruct((t, d), F32), jax.ShapeDtypeStruct((1, d), F32)],
        compiler_params=_cparams("arbitrary"),
    )(dh, x, w, dres)


def _loss_fwd_bwd(x, w, target, *, name, tm=1024):
    t, d = x.shape
    tm = _tile(t, tm)

    def body(x_ref, w_ref, t_ref, dx_ref, dw_ref, loss_ref):
        xhat, r = _norm_parts(x_ref[...])
        wv = w_ref[...]
        err = xhat * wv - t_ref[...]
        dx, dw = _norm_bwd(err * (1.0 / d), xhat, r, wv)
        dx_ref[...] = dx
        part = jnp.full((1, 128), 0.5 / d, F32) * jnp.sum(err * err)

        @pl.when(pl.program_id(0) == 0)
        def _():
            dw_ref[...] = dw
            loss_ref[...] = part

        @pl.when(pl.program_id(0) > 0)
        def _():
            dw_ref[...] += dw
            loss_ref[...] += part

    row = pl.BlockSpec((tm, d), lambda i: (i, 0))
    vec = pl.BlockSpec((1, d), lambda i: (0, 0))
    return pl.pallas_call(
        body, name=name, grid=(t // tm,), in_specs=[row, vec, row],
        out_specs=[row, vec, pl.BlockSpec((1, 128), lambda i: (0, 0))],
        out_shape=[jax.ShapeDtypeStruct((t, d), F32), jax.ShapeDtypeStruct((1, d), F32),
                   jax.ShapeDtypeStruct((1, 128), F32)],
        compiler_params=_cparams("arbitrary"),
    )(x, w, target)


def _ffn_tiles(t, f):
    tf = f
    for cand in (1408, 1024, 512, 256, 128):
        if f % cand == 0:
            tf = cand
            break
    return _tile(t, 512), tf


def _ffn_fwd(x, nw, wg_t, wu_t, wd, *, name, exchange=None):
    t, d = x.shape
    f = wd.shape[0]
    tm, tf = _ffn_tiles(t, f)
    nf = f // tf

    def body(x_ref, nw_ref, wg_ref, wu_ref, wd_ref, xo_ref, h_ref, g_ref, u_ref, a_ref, hs_ref, acc_ref):
        j = pl.program_id(1)

        @pl.when(j == 0)
        def _():
            xhat, _ = _norm_parts(x_ref[...])
            hb = _bf(xhat * nw_ref[...])
            hs_ref[...] = hb
            h_ref[...] = hb

        hb = hs_ref[...]
        g = _dot_nt(hb, wg_ref[...])
        u = _dot_nt(hb, wu_ref[...])
        a = _bf(g * _sigmoid(g) * u)
        g_ref[...] = _bf(g)
        u_ref[...] = _bf(u)
        a_ref[...] = a
        part = _dot(a, wd_ref[...])

        @pl.when(j == 0)
        def _():
            acc_ref[...] = part

        @pl.when(j > 0)
        def _():
            acc_ref[...] += part

        @pl.when(j == nf - 1)
        def _():
            xo_ref[...] = x_ref[...] + 0.5 * acc_ref[...]

    row = pl.BlockSpec((tm, d), lambda i, j: (i, 0))
    wspec = pl.BlockSpec((tf, d), lambda i, j: (j, 0))
    hid = pl.BlockSpec((tm, tf), lambda i, j: (i, j))
    return _pcall(
        body, (x, nw, wg_t, wu_t, wd), name=name, grid=(t // tm, nf),
        in_specs=[row, pl.BlockSpec((1, d), lambda i, j: (0, 0)), wspec, wspec, wspec],
        out_specs=[row, row, hid, hid, hid],
        out_shape=[jax.ShapeDtypeStruct((t, d), F32), jax.ShapeDtypeStruct((t, d), BF16)]
        + [jax.ShapeDtypeStruct((t, f), BF16)] * 3,
        scratch_shapes=[pltpu.VMEM((tm, d), BF16), pltpu.VMEM((tm, d), F32)],
        sem=("parallel", "arbitrary"), exchange=exchange)


def _ffn_bwd(dxo, x, nw, g, u, wg_t, wu_t, wd, *, name, exchange=None):
    t, d = x.shape
    f = wd.shape[0]
    tm, tf = _ffn_tiles(t, f)
    nf = f // tf

    def body(dxo_ref, x_ref, nw_ref, g_ref, u_ref, wg_ref, wu_ref, wd_ref,
             dx_ref, dy_ref, dg_ref, du_ref, dnw_ref, dys_ref, acc_ref):
        i, j = pl.program_id(0), pl.program_id(1)

        @pl.when(j == 0)
        def _():
            dyb = _bf(0.5 * dxo_ref[...])
            dys_ref[...] = dyb
            dy_ref[...] = dyb

        da = _dot_nt(dys_ref[...], wd_ref[...])
        gv = g_ref[...].astype(F32)
        uv = u_ref[...].astype(F32)
        s = _sigmoid(gv)
        dg = _bf(da * uv * (s * (1.0 + gv * (1.0 - s))))
        du = _bf(da * (gv * s))
        dg_ref[...] = dg
        du_ref[...] = du
        part = _dot(dg, wg_ref[...]) + _dot(du, wu_ref[...])

        @pl.when(j == 0)
        def _():
            acc_ref[...] = part

        @pl.when(j > 0)
        def _():
            acc_ref[...] += part

        @pl.when(j == nf - 1)
        def _():
            xhat, r = _norm_parts(x_ref[...])
            dx, dw = _norm_bwd(acc_ref[...], xhat, r, nw_ref[...])
            dx_ref[...] = dxo_ref[...] + dx

            @pl.when(i == 0)
            def _():
                dnw_ref[...] = dw

            @pl.when(i > 0)
            def _():
                dnw_ref[...] += dw

    row = pl.BlockSpec((tm, d), lambda i, j: (i, 0))
    vec = pl.BlockSpec((1, d), lambda i, j: (0, 0))
    wspec = pl.BlockSpec((tf, d), lambda i, j: (j, 0))
    hid = pl.BlockSpec((tm, tf), lambda i, j: (i, j))
    return _pcall(
        body, (dxo, x, nw, g, u, wg_t, wu_t, wd), name=name, grid=(t // tm, nf),
        in_specs=[row, row, vec, hid, hid, wspec, wspec, wspec],
        out_specs=[row, row, hid, hid, vec],
        out_shape=[jax.ShapeDtypeStruct((t, d), F32), jax.ShapeDtypeStruct((t, d), BF16),
                   jax.ShapeDtypeStruct((t, f), BF16), jax.ShapeDtypeStruct((t, f), BF16),
                   jax.ShapeDtypeStruct((1, d), F32)],
        scratch_shapes=[pltpu.VMEM((tm, d), BF16), pltpu.VMEM((tm, d), F32)],
        sem=("arbitrary", "arbitrary"), exchange=exchange)


def _shift_down(prev, cur, n):
    ext = jnp.concatenate([prev, cur], axis=0)
    return pltpu.roll(ext, n, axis=0)[prev.shape[0]:]


def _shift_up(cur, nxt, n):
    ext = jnp.concatenate([cur, nxt], axis=0)
    return pltpu.roll(ext, ext.shape[0] - n, axis=0)[:cur.shape[0]]


def _conv_specs(t, tb):
    hb = tb // CONV_HALO
    last = t // CONV_HALO - 1

    def tile(s):
        return pl.BlockSpec((None, tb, 128), lambda c, i: (s, i, c))

    def prev(s):
        return pl.BlockSpec((None, CONV_HALO, 128), lambda c, i: (s, jnp.maximum(i * hb - 1, 0), c))

    def nxt(s):
        return pl.BlockSpec((None, CONV_HALO, 128), lambda c, i: (s, jnp.minimum((i + 1) * hb, last), c))

    return tile, prev, nxt


def _conv_fwd(cols3, conv_w, *, name, tb=1024):
    _, t, bw = cols3.shape
    tb = _tile(t, tb)
    tile, prev, _ = _conv_specs(t, tb)

    def body(u_ref, b_ref, c_ref, up_ref, cp_ref, w_ref, y_ref):
        first = pl.program_id(1) == 0
        z = c_ref[...].astype(F32) * u_ref[...].astype(F32)
        zp = jnp.where(first, 0.0, cp_ref[...].astype(F32) * up_ref[...].astype(F32))
        conv = w_ref[0:1, :] * _shift_down(zp, z, 2) + w_ref[1:2, :] * _shift_down(zp, z, 1) + w_ref[2:3, :] * z
        y_ref[...] = _bf(b_ref[...].astype(F32) * conv)

    return pl.pallas_call(
        body, name=name, grid=(bw // 128, t // tb),
        in_specs=[tile(0), tile(1), tile(2), prev(0), prev(2), pl.BlockSpec((3, 128), lambda c, i: (0, c))],
        out_specs=pl.BlockSpec((tb, 128), lambda c, i: (i, c)),
        out_shape=jax.ShapeDtypeStruct((t, bw), BF16), compiler_params=_cparams("parallel", "parallel"),
    )(cols3, cols3, cols3, cols3, cols3, conv_w)


def _conv_bwd(cols3, conv_w, dy, *, name, tb=1024):
    _, t, bw = cols3.shape
    tb = _tile(t, tb)
    nt = t // tb
    tile, prev, nxt = _conv_specs(t, tb)
    hb = tb // CONV_HALO
    last = t // CONV_HALO - 1

    def body(u_ref, b_ref, c_ref, up_ref, cp_ref, bn_ref, dy_ref, dyn_ref, w_ref, d3_ref, dw_ref):
        i = pl.program_id(1)
        uv, bv, cv = u_ref[...].astype(F32), b_ref[...].astype(F32), c_ref[...].astype(F32)
        dyv = dy_ref[...].astype(F32)
        z = cv * uv
        zp = jnp.where(i == 0, 0.0, cp_ref[...].astype(F32) * up_ref[...].astype(F32))
        z1, z2 = _shift_down(zp, z, 1), _shift_down(zp, z, 2)
        w0, w1, w2 = w_ref[0:1, :], w_ref[1:2, :], w_ref[2:3, :]
        conv = w0 * z2 + w1 * z1 + w2 * z
        dconv = dyv * bv
        dconv_n = jnp.where(i == nt - 1, 0.0, dyn_ref[...].astype(F32) * bn_ref[...].astype(F32))
        dz = w2 * dconv + w1 * _shift_up(dconv, dconv_n, 1) + w0 * _shift_up(dconv, dconv_n, 2)
        d3_ref[0] = _bf(dz * cv)
        d3_ref[1] = _bf(dyv * conv)
        d3_ref[2] = _bf(dz * uv)
        dws = [jnp.sum(dconv * zz, axis=0, keepdims=True) for zz in (z2, z1, z)]

        @pl.when(i == 0)
        def _():
            for j in range(3):
                dw_ref[j:j + 1, :] = dws[j]

        @pl.when(i > 0)
        def _():
            for j in range(3):
                dw_ref[j:j + 1, :] += dws[j]

    dy_tile = pl.BlockSpec((tb, 128), lambda c, i: (i, c))
    dy_next = pl.BlockSpec((CONV_HALO, 128), lambda c, i: (jnp.minimum((i + 1) * hb, last), c))
    wspec = pl.BlockSpec((3, 128), lambda c, i: (0, c))
    return pl.pallas_call(
        body, name=name, grid=(bw // 128, nt),
        in_specs=[tile(0), tile(1), tile(2), prev(0), prev(2), nxt(1), dy_tile, dy_next, wspec],
        out_specs=[pl.BlockSpec((3, tb, 128), lambda c, i: (0, i, c)), wspec],
        out_shape=[jax.ShapeDtypeStruct((3, t, bw), BF16), jax.ShapeDtypeStruct((3, bw), F32)],
        compiler_params=_cparams("parallel", "arbitrary"),
    )(cols3, cols3, cols3, cols3, cols3, cols3, dy, dy, conv_w)


def _ret_consts():
    log_gamma = jnp.log1p(-jnp.exp2(-5.0 - jnp.arange(H_RET, dtype=F32)))
    pos = jnp.arange(CHUNK, dtype=F32)
    d_intra = jnp.exp(log_gamma[:, None, None] * jnp.abs(pos[:, None] - pos[None, :]))
    q_decay = jnp.exp(log_gamma[:, None] * (pos + 1.0))
    k_decay = jnp.exp(log_gamma[:, None] * (CHUNK - 1.0 - pos))
    chunk_decay = jnp.exp(log_gamma * CHUNK)
    wide = (H_RET, CHUNK, DK_RET)
    return (d_intra, jnp.broadcast_to(q_decay[:, :, None], wide), jnp.broadcast_to(k_decay[:, :, None], wide),
            jnp.broadcast_to(chunk_decay[:, None, None], (H_RET, 1, DK_RET)))


def _rope_tables(t):
    inv_freq = ROPE_BASE ** (-jnp.linspace(0.0, 1.0, DK_RET // 2, dtype=F32))
    ang = jnp.arange(t, dtype=F32)[:, None] * inv_freq[None, :]
    cos, sin = jnp.cos(ang), jnp.sin(ang)
    return jnp.concatenate([cos, cos], axis=1), jnp.concatenate([-sin, sin], axis=1)


def _rope(v, cc, ss):
    return v * cc + pltpu.roll(v, DK_RET // 2, axis=1) * ss


def _ret_in_specs(tb, blk):
    def col(s):
        return pl.BlockSpec((None, tb, H_RET * DK_RET), lambda i: (s, blk(i), 0))

    tab = pl.BlockSpec((tb, DK_RET), lambda i: (blk(i), 0))
    return ([col(3), col(4), col(5), col(6), tab, tab,
             pl.BlockSpec((H_RET, CHUNK, CHUNK), lambda i: (0, 0, 0)),
             pl.BlockSpec((H_RET, CHUNK, DK_RET), lambda i: (0, 0, 0)),
             pl.BlockSpec((H_RET, CHUNK, DK_RET), lambda i: (0, 0, 0)),
             pl.BlockSpec((H_RET, 1, DK_RET), lambda i: (0, 0, 0))])


def _ret_fwd(cols3, tables, consts, *, name):
    _, t, bw = cols3.shape
    tb = _tile(t, RET_TB)
    ncb = tb // CHUNK
    scale = DK_RET ** -0.5

    def body(q_ref, k_ref, v_ref, g_ref, cc_ref, ss_ref, di_ref, qd_ref, kd_ref, cd_ref, y_ref, st_ref, state):
        @pl.when(pl.program_id(0) == 0)
        def _():
            state[...] = jnp.zeros_like(state)

        for c in range(ncb):
            rows = pl.ds(c * CHUNK, CHUNK)
            cc, ss = cc_ref[rows, :], ss_ref[rows, :]
            for h in range(H_RET):
                lanes = pl.ds(h * DK_RET, DK_RET)
                qs = _rope(q_ref[rows, lanes].astype(F32), cc, ss) * scale
                kr = _rope(k_ref[rows, lanes].astype(F32), cc, ss)
                vb = v_ref[rows, lanes]
                gv = g_ref[rows, lanes].astype(F32)
                s_in = state[h]
                st_ref[h, c] = s_in
                inner = _dot_nt(_bf(qs), _bf(kr)) * di_ref[h]
                o = _dot(_bf(inner), vb) + _dot(_bf(qs * qd_ref[h]), _bf(s_in))
                state[h] = s_in * cd_ref[h] + _dot_tn(_bf(kr * kd_ref[h]), vb)
                on = o * lax.rsqrt(jnp.mean(o * o, axis=-1, keepdims=True) + EPS)
                y_ref[rows, lanes] = _bf(gv * _sigmoid(gv) * on)

    return pl.pallas_call(
        body, name=name, grid=(t // tb,),
        in_specs=_ret_in_specs(tb, lambda i: i),
        out_specs=[pl.BlockSpec((tb, bw), lambda i: (i, 0)),
                   pl.BlockSpec((H_RET, ncb, DK_RET, DK_RET), lambda i: (0, i, 0, 0))],
        out_shape=[jax.ShapeDtypeStruct((t, bw), BF16),
                   jax.ShapeDtypeStruct((H_RET, t // CHUNK, DK_RET, DK_RET), F32)],
        scratch_shapes=[pltpu.VMEM((H_RET, DK_RET, DK_RET), F32)],
        compiler_params=_cparams("arbitrary"),
    )(cols3, cols3, cols3, cols3, tables[0], tables[1], *consts)


def _ret_bwd(cols3, tables, consts, states, dy, *, name, exchange=None):
    _, t, bw = cols3.shape
    tb = _tile(t, RET_TB)
    ncb = tb // CHUNK
    nb = t // tb
    scale = DK_RET ** -0.5

    def body(q_ref, k_ref, v_ref, g_ref, cc_ref, ss_ref, di_ref, qd_ref, kd_ref, cd_ref, st_ref, dy_ref,
             d4_ref, dstate):
        @pl.when(pl.program_id(0) == 0)
        def _():
            dstate[...] = jnp.zeros_like(dstate)

        heads = range(H_RET)
        lanes_of = [pl.ds(h * DK_RET, DK_RET) for h in heads]
        di, qd, kd = [di_ref[h] for h in heads], [qd_ref[h] for h in heads], [kd_ref[h] for h in heads]
        carried = [dstate[h] for h in heads]
        for c in reversed(range(ncb)):
            rows = pl.ds(c * CHUNK, CHUNK)
            cc, ss = cc_ref[rows, :], ss_ref[rows, :]
            vb = [v_ref[rows, ln] for ln in lanes_of]
            gv = [g_ref[rows, ln].astype(F32) for ln in lanes_of]
            dyv = [dy_ref[rows, ln].astype(F32) for ln in lanes_of]
            s_in = [_bf(st_ref[h, c]) for h in heads]
            qs = [_rope(q_ref[rows, ln].astype(F32), cc, ss) * scale for ln in lanes_of]
            kr = [_rope(k_ref[rows, ln].astype(F32), cc, ss) for ln in lanes_of]
            qsb, krb = [_bf(v) for v in qs], [_bf(v) for v in kr]
            qdb = [_bf(qs[h] * qd[h]) for h in heads]
            kdb = [_bf(kr[h] * kd[h]) for h in heads]
            innerb = [_bf(_dot_nt(qsb[h], krb[h]) * di[h]) for h in heads]
            o = [_dot(innerb[h], vb[h]) + _dot(qdb[h], s_in[h]) for h in heads]
            r = [lax.rsqrt(jnp.mean(v * v, axis=-1, keepdims=True) + EPS) for v in o]
            on = [o[h] * r[h] for h in heads]
            sg = [_sigmoid(v) for v in gv]
            dgv = [_bf(dyv[h] * on[h] * (sg[h] * (1.0 + gv[h] * (1.0 - sg[h])))) for h in heads]
            don = [dyv[h] * (gv[h] * sg[h]) for h in heads]
            dob = [_bf(r[h] * (don[h] - on[h] * jnp.mean(don[h] * on[h], axis=-1, keepdims=True))) for h in heads]
            dinner = [_bf(_dot_nt(dob[h], vb[h]) * di[h]) for h in heads]
            dqs = [_dot(dinner[h], krb[h]) + _dot_nt(dob[h], s_in[h]) * qd[h] for h in heads]
            dkr = [_dot_tn(dinner[h], qsb[h]) for h in heads]
            dv = [_dot_tn(innerb[h], dob[h]) for h in heads]
            dnew = [_dot_tn(qdb[h], dob[h]) for h in heads]
            dstb = [_bf(v) for v in carried]
            dv = [dv[h] + _dot(kdb[h], dstb[h]) for h in heads]
            dkr = [dkr[h] + _dot_nt(vb[h], dstb[h]) * kd[h] for h in heads]
            carried = [carried[h] * cd_ref[h] + dnew[h] for h in heads]
            for h, ln in enumerate(lanes_of):
                d4_ref[0, rows, ln] = _bf(_rope_bwd(dqs[h] * scale, cc, ss))
                d4_ref[1, rows, ln] = _bf(_rope_bwd(dkr[h], cc, ss))
                d4_ref[2, rows, ln] = _bf(dv[h])
                d4_ref[3, rows, ln] = dgv[h]
        for h in heads:
            dstate[h] = carried[h]

    rev = lambda i: nb - 1 - i
    outs, moved = _pcall(
        body, (cols3, cols3, cols3, cols3, tables[0], tables[1], *consts, states, dy), name=name, grid=(nb,),
        in_specs=_ret_in_specs(tb, rev)
        + [pl.BlockSpec((H_RET, ncb, DK_RET, DK_RET), lambda i: (0, rev(i), 0, 0)),
           pl.BlockSpec((tb, bw), lambda i: (rev(i), 0))],
        out_specs=[pl.BlockSpec((4, tb, bw), lambda i: (0, rev(i), 0))],
        out_shape=[jax.ShapeDtypeStruct((4, t, bw), BF16)],
        scratch_shapes=[pltpu.VMEM((H_RET, DK_RET, DK_RET), F32)],
        sem=("arbitrary",), exchange=exchange)
    return outs[0], moved


def _rope_bwd(dv, cc, ss):
    return dv * cc + pltpu.roll(dv * ss, DK_RET // 2, axis=1)


def _att_window(i):
    return pl.multiple_of(jnp.maximum(i - ATT_LOOKBACK // ATT_QB, 0) * ATT_QB, ATT_QB)


def _att_mask(v):
    qchunk = (v * ATT_QB + lax.broadcasted_iota(jnp.int32, (ATT_QB, ATT_WIN), 0)) // CHUNK
    kchunk = lax.broadcasted_iota(jnp.int32, (ATT_QB, ATT_WIN), 1) // CHUNK
    return (kchunk <= qchunk) & (kchunk >= qchunk - N_PREV_CHUNKS)


def _att_fwd(cols3, bias3, *, name, exchange=None):
    _, t, bw = cols3.shape
    assert t % ATT_QB == 0 and t >= ATT_WIN
    scale = DH_ATT ** -0.5
    nvar = ATT_LOOKBACK // ATT_QB

    def body(q_ref, k_ref, v_ref, b_ref, y_ref, lse_ref):
        i = pl.program_id(1)
        ws = _att_window(i)
        q = q_ref[...].astype(F32)
        kw = k_ref[pl.ds(ws, ATT_WIN), :]
        vw = v_ref[pl.ds(ws, ATT_WIN), :]
        lane_head = lax.broadcasted_iota(jnp.int32, (ATT_QB, 128), 1) // DH_ATT
        out = jnp.zeros((ATT_QB, 128), F32)
        lse = jnp.zeros((ATT_QB, 128), F32)
        for hh in range(2):
            mine = lane_head == hh
            s = _dot_nt(_bf(jnp.where(mine, q, 0.0)), kw) * scale + b_ref[hh]
            mx = jnp.max(s, axis=-1, keepdims=True)
            p = jnp.exp(s - mx)
            l = jnp.sum(p, axis=-1, keepdims=True)
            out = jnp.where(mine, _dot(_bf(p), vw) / l, out)
            lse = jnp.where(mine, mx + jnp.log(l), lse)
        y_ref[...] = _bf(out)
        lse_ref[...] = lse

    kv = lambda s: pl.BlockSpec((None, t, 128), lambda hp, i: (s, 0, hp))
    return _pcall(
        body, (cols3, cols3, cols3, bias3), name=name, grid=(H_ATT // 2, t // ATT_QB),
        in_specs=[pl.BlockSpec((None, ATT_QB, 128), lambda hp, i: (7, i, hp)), kv(8), kv(9),
                  pl.BlockSpec((None, 2, ATT_QB, ATT_WIN), lambda hp, i: (jnp.minimum(i, nvar), hp, 0, 0))],
        out_specs=[pl.BlockSpec((ATT_QB, 128), lambda hp, i: (i, hp)),
                   pl.BlockSpec((None, ATT_QB, 128), lambda hp, i: (hp, i, 0))],
        out_shape=[jax.ShapeDtypeStruct((t, bw), BF16), jax.ShapeDtypeStruct((H_ATT // 2, t, 128), F32)],
        scratch_shapes=[], sem=("parallel", "arbitrary"), exchange=exchange)


def _att_bwd(cols3, bias3, y, lse, dy, *, name, exchange=None):
    _, t, bw = cols3.shape
    nq = t // ATT_QB
    scale = DH_ATT ** -0.5
    nvar = ATT_LOOKBACK // ATT_QB

    def body(q_ref, k_ref, v_ref, b_ref, y_ref, lse_ref, dy_ref, d3_ref, db_ref, dk_acc, dv_acc):
        i = pl.program_id(1)

        @pl.when(i == 0)
        def _():
            dk_acc[...] = jnp.zeros_like(dk_acc)
            dv_acc[...] = jnp.zeros_like(dv_acc)

        ws = _att_window(i)
        q = q_ref[...].astype(F32)
        kw = k_ref[pl.ds(ws, ATT_WIN), :]
        vw = v_ref[pl.ds(ws, ATT_WIN), :]
        do = dy_ref[...].astype(F32)
        dof = do * y_ref[...].astype(F32)
        lsev = lse_ref[...]
        lane_head = lax.broadcasted_iota(jnp.int32, (ATT_QB, 128), 1) // DH_ATT
        dq = jnp.zeros((ATT_QB, 128), F32)
        dk = jnp.zeros((ATT_WIN, 128), F32)
        dv = jnp.zeros((ATT_WIN, 128), F32)
        first = i <= nvar
        for hh in range(2):
            mine = lane_head == hh
            qh = _bf(jnp.where(mine, q, 0.0))
            doh = _bf(jnp.where(mine, do, 0.0))
            s = _dot_nt(qh, kw) * scale + b_ref[hh]
            lse_h = jnp.max(jnp.where(mine, lsev, NEG_INF), axis=-1, keepdims=True)
            p = jnp.exp(s - lse_h)
            delta = jnp.sum(jnp.where(mine, dof, 0.0), axis=-1, keepdims=True)
            ds = p * (_dot_nt(doh, vw) - delta)

            @pl.when(first)
            def _():
                db_ref[hh] = ds

            @pl.when(jnp.logical_not(first))
            def _():
                db_ref[hh] += ds

            dsb = _bf(ds * scale)
            dq = jnp.where(mine, _dot(dsb, kw), dq)
            dk = dk + _dot_tn(dsb, qh)
            dv = dv + _dot_tn(_bf(p), doh)
        d3_ref[0, pl.ds(pl.multiple_of(i * ATT_QB, ATT_QB), ATT_QB), :] = _bf(dq)
        dk_acc[pl.ds(ws, ATT_WIN), :] += dk
        dv_acc[pl.ds(ws, ATT_WIN), :] += dv

        @pl.when(i == nq - 1)
        def _():
            d3_ref[1] = _bf(dk_acc[...])
            d3_ref[2] = _bf(dv_acc[...])

    kv = lambda s: pl.BlockSpec((None, t, 128), lambda hp, i: (s, 0, hp))
    qrow = pl.BlockSpec((ATT_QB, 128), lambda hp, i: (i, hp))
    btile = pl.BlockSpec((None, 2, ATT_QB, ATT_WIN), lambda hp, i: (jnp.minimum(i, nvar), hp, 0, 0))
    return _pcall(
        body, (cols3, cols3, cols3, bias3, y, lse, dy), name=name, grid=(H_ATT // 2, nq),
        in_specs=[pl.BlockSpec((None, ATT_QB, 128), lambda hp, i: (7, i, hp)), kv(8), kv(9), btile, qrow,
                  pl.BlockSpec((None, ATT_QB, 128), lambda hp, i: (hp, i, 0)), qrow],
        out_specs=[pl.BlockSpec((3, t, 128), lambda hp, i: (0, 0, hp)), btile],
        out_shape=[jax.ShapeDtypeStruct((3, t, bw), BF16),
                   jax.ShapeDtypeStruct((nvar + 1, H_ATT, ATT_QB, ATT_WIN), F32)],
        scratch_shapes=[pltpu.VMEM((t, 128), F32), pltpu.VMEM((t, 128), F32)],
        sem=("parallel", "arbitrary"), exchange=exchange)


SKEW_W = ATT_WIN + ATT_QB
REL_PAD = 384


def _rel_onehot(v):
    r = lax.broadcasted_iota(jnp.int32, (REL_PAD, SKEW_W), 0)
    j = lax.broadcasted_iota(jnp.int32, (REL_PAD, SKEW_W), 1)
    dist = jnp.where(j < ATT_WIN, v * ATT_QB - j, v * ATT_QB + SKEW_W - j)
    col = jnp.clip(dist, -REL_CLIP, REL_CLIP) + REL_CLIP
    return _bf(jnp.where(col == r, 1.0, 0.0))


def _split3(v):
    hi = _bf(v)
    rest = v - hi.astype(F32)
    mid = _bf(rest)
    return hi, mid, _bf(rest - mid.astype(F32))


def _skew8(a, forward):
    row = lax.broadcasted_iota(jnp.int32, a.shape, 0)
    for b in range(3):
        shift = (1 << b) if forward else SKEW_W - (1 << b)
        a = jnp.where(((row >> b) & 1) == 1, pltpu.roll(a, shift, axis=1), a)
    return a


def _toeplitz_rows(ext_row):
    a = _skew8(jnp.broadcast_to(ext_row, (8, SKEW_W)), True)
    while a.shape[0] < ATT_QB:
        a = jnp.concatenate([a, pltpu.roll(a, a.shape[0], axis=1)], axis=0)
    return a


def _diagonal_sums(tile):
    a = tile
    while a.shape[0] > 8:
        half = a.shape[0] // 2
        a = a[:half] + pltpu.roll(a[half:], SKEW_W - half, axis=1)
    return jnp.sum(_skew8(a, False), axis=0, keepdims=True)


def _bias_tiles(rel_bias, *, name, exchange=None):
    depth = rel_bias.shape[0]
    nvar = ATT_LOOKBACK // ATT_QB + 1
    rel = jnp.pad(rel_bias, ((0, 0), (0, 0), (0, REL_PAD - N_REL)))

    def body(rel_ref, o_ref, ext_ref):
        v = pl.program_id(1)
        onehot = _rel_onehot(v)
        ext_ref[...] = sum(_dot(part, onehot) for part in _split3(rel_ref[...]))
        valid = _att_mask(v)
        for h in range(H_ATT):
            o_ref[h] = jnp.where(valid, _toeplitz_rows(ext_ref[h:h + 1, :])[:, :ATT_WIN], NEG_INF)

    outs, moved = _pcall(
        body, (rel,), name=name, grid=(depth, nvar),
        in_specs=[pl.BlockSpec((None, H_ATT, REL_PAD), lambda l, v: (l, 0, 0))],
        out_specs=[pl.BlockSpec((None, None, H_ATT, ATT_QB, ATT_WIN), lambda l, v: (l, v, 0, 0, 0))],
        out_shape=[jax.ShapeDtypeStruct((depth, nvar, H_ATT, ATT_QB, ATT_WIN), F32)],
        scratch_shapes=[pltpu.VMEM((H_ATT, SKEW_W), F32)], sem=("parallel", "parallel"), exchange=exchange)
    return outs[0], moved


def _rel_bias_grad(dbias3, *, name):
    nvar = dbias3.shape[0]

    def body(db_ref, o_ref, diag_ref):
        v = pl.program_id(0)
        for h in range(H_ATT):
            tile = jnp.concatenate([db_ref[h], jnp.zeros((ATT_QB, ATT_QB), F32)], axis=1)
            diag_ref[h:h + 1, :] = _diagonal_sums(tile)
        onehot = _rel_onehot(v)
        part = sum(_dot_nt(p, onehot) for p in _split3(diag_ref[...]))

        @pl.when(v == 0)
        def _():
            o_ref[...] = part

        @pl.when(v > 0)
        def _():
            o_ref[...] += part

    out = pl.pallas_call(
        body, name=name, grid=(nvar,),
        in_specs=[pl.BlockSpec((None, H_ATT, ATT_QB, ATT_WIN), lambda v: (v, 0, 0, 0))],
        out_specs=pl.BlockSpec((H_ATT, REL_PAD), lambda v: (0, 0)),
        out_shape=jax.ShapeDtypeStruct((H_ATT, REL_PAD), F32),
        scratch_shapes=[pltpu.VMEM((H_ATT, SKEW_W), F32)], compiler_params=_cparams("arbitrary"),
    )(dbias3)
    return out[:, :N_REL]


def _merge_fwd(h, ys, wmg, wb_t, *, name, tm=512):
    t, d = h.shape
    bw = ys[0].shape[1]
    tm = _tile(t, tm)

    def body(h_ref, y0_ref, y1_ref, y2_ref, wg_ref, wb_ref, m_ref, s_ref, p_ref):
        hv = h_ref[...]
        total = jnp.zeros((tm, d), F32)
        for b, y_ref in enumerate((y0_ref, y1_ref, y2_ref)):
            s = _sigmoid(_dot(hv, wg_ref[b]))
            p = _dot_nt(y_ref[...], wb_ref[b])
            s_ref[b] = _bf(s)
            p_ref[b] = _bf(p)
            total = total + s * p
        m_ref[...] = _bf(total)

    row = pl.BlockSpec((tm, d), lambda i: (i, 0))
    yrow = pl.BlockSpec((tm, bw), lambda i: (i, 0))
    three = pl.BlockSpec((3, tm, d), lambda i: (0, i, 0))
    return pl.pallas_call(
        body, name=name, grid=(t // tm,),
        in_specs=[row, yrow, yrow, yrow, pl.BlockSpec((3, d, d), lambda i: (0, 0, 0)),
                  pl.BlockSpec((3, d, bw), lambda i: (0, 0, 0))],
        out_specs=[row, three, three],
        out_shape=[jax.ShapeDtypeStruct((t, d), BF16), jax.ShapeDtypeStruct((3, t, d), BF16),
                   jax.ShapeDtypeStruct((3, t, d), BF16)],
        compiler_params=_cparams("parallel"),
    )(h, *ys, wmg, wb_t)


def _merge_bwd(dm, s3, p3, wmg, wb_t, *, name, tm=512):
    _, t, d = s3.shape
    bw = wb_t.shape[2]
    tm = _tile(t, tm)

    def body(dm_ref, s_ref, p_ref, wg_ref, wb_ref, dgp_ref, dp_ref, dy_ref, dh_ref):
        dmv = dm_ref[...].astype(F32)
        dh = jnp.zeros((tm, d), F32)
        for b in range(3):
            s = s_ref[b].astype(F32)
            dgp = _bf(dmv * p_ref[b].astype(F32) * s * (1.0 - s))
            dp = _bf(dmv * s)
            dgp_ref[b] = dgp
            dp_ref[b] = dp
            dy_ref[b] = _bf(_dot(dp, wb_ref[b]))
            dh = dh + _dot_nt(dgp, wg_ref[b])
        dh_ref[...] = dh

    row = pl.BlockSpec((tm, d), lambda i: (i, 0))
    three = pl.BlockSpec((3, tm, d), lambda i: (0, i, 0))
    return pl.pallas_call(
        body, name=name, grid=(t // tm,),
        in_specs=[row, three, three, pl.BlockSpec((3, d, d), lambda i: (0, 0, 0)),
                  pl.BlockSpec((3, d, bw), lambda i: (0, 0, 0))],
        out_specs=[three, three, pl.BlockSpec((3, tm, bw), lambda i: (0, i, 0)), row],
        out_shape=[jax.ShapeDtypeStruct((3, t, d), BF16), jax.ShapeDtypeStruct((3, t, d), BF16),
                   jax.ShapeDtypeStruct((3, t, bw), BF16), jax.ShapeDtypeStruct((t, d), F32)],
        compiler_params=_cparams("parallel"),
    )(dm, s3, p3, wmg, wb_t)


def _carried(hooks, slot, state=None):
    if not hooks or slot not in hooks:
        return None, lambda buf: None
    make, done = hooks[slot]
    return make(state), done


def _layer_fwd(x, p, aux, tag, hooks=None):
    ex, done = _carried(hooks, "ffn1")
    (x1, h1, g1, u1, a1), buf = _ffn_fwd(x, p["n1"], p["wg1"], p["wu1"], p["wd1"], name=f"ffn1_fwd_{tag}", exchange=ex)
    done(buf)
    h2 = _rmsnorm_fwd(x1, p["nmix"], name=f"mixnorm_fwd_{tag}")
    ex, done = _carried(hooks, "inproj")
    if ex is None:
        cols3 = _mm_to_slices(h2, p["win"], name=f"inproj_fwd_{tag}")
    else:
        cols3, buf = _mm_to_slices(h2, p["win"], name=f"inproj_fwd_{tag}", exchange=ex)
        done(buf)
    bias3 = p["bias3"]
    y_conv = _conv_fwd(cols3, p["conv_w"], name=f"conv_fwd_{tag}")
    y_ret, states = _ret_fwd(cols3, aux["rope"], aux["ret"], name=f"ret_fwd_{tag}")
    ex, done = _carried(hooks, "att")
    (y_att, lse), buf = _att_fwd(cols3, bias3, name=f"att_fwd_{tag}", exchange=ex)
    done(buf)
    merged, s3, p3 = _merge_fwd(h2, (y_conv, y_ret, y_att), p["wmg"], p["wb"], name=f"merge_fwd_{tag}")
    x2 = _mm_rows(merged, p["wout"], dims=NN, out_dtype=F32, res=x1, name=f"outproj_fwd_{tag}")
    ex, done = _carried(hooks, "ffn2")
    (x3, h3, g2, u2, a2), buf = _ffn_fwd(x2, p["n2"], p["wg2"], p["wu2"], p["wd2"], name=f"ffn2_fwd_{tag}", exchange=ex)
    done(buf)
    saved = dict(x0=x, h1=h1, g1=g1, u1=u1, a1=a1, x1=x1, h2=h2, cols3=cols3, bias3=bias3, y_conv=y_conv,
                 y_ret=y_ret, states=states, y_att=y_att, lse=lse, merged=merged, s3=s3, p3=p3, x2=x2, h3=h3,
                 g2=g2, u2=u2, a2=a2)
    return x3, saved


def _ffn_grads(dxo, x, h, g, u, a, p, which, tag, grads, all_grads, hooks):
    n = which
    ex, done = _carried(hooks, f"ffn{n}_bwd", all_grads)
    (dx, dyb, dg, du, grads["n" + n]), buf = _ffn_bwd(dxo, x, p["n" + n], g, u, p["wg" + n], p["wu" + n], p["wd" + n],
                                                     name=f"ffn{n}_{tag}_bwd", exchange=ex)
    done(buf)
    grads["wd" + n] = _mm_tn(a, dyb, name=f"ffn{n}_{tag}_dwd")
    for key, lhs in (("wg" + n, dg), ("wu" + n, du)):
        ex, done = _carried(hooks, f"ffn{n}_d{key[:2]}", all_grads)
        if ex is None:
            grads[key] = _mm_tn(lhs, h, name=f"ffn{n}_{tag}_d{key[:2]}")
        else:
            grads[key], buf = _mm_tn(lhs, h, name=f"ffn{n}_{tag}_d{key[:2]}", exchange=ex)
            done(buf)
    return dx


def _layer_bwd(dx3, p, s, aux, tag, grads, all_grads, hooks=None):
    dx2 = _ffn_grads(dx3, s["x2"], s["h3"], s["g2"], s["u2"], s["a2"], p, "2", tag, grads, all_grads, hooks)

    dm = _mm_rows(dx2, p["wout"], dims=NT, out_dtype=BF16, name=f"outproj_dm_{tag}")
    grads["wout"] = _mm_tn(s["merged"], dx2, name=f"outproj_dw_{tag}")
    dgp3, dp3, dy3, dh2 = _merge_bwd(dm, s["s3"], s["p3"], p["wmg"], p["wb"], name=f"merge_bwd_{tag}")
    grads["wmg"] = _mm_tn_batch(s["h2"], dgp3, a_batched=False, name=f"merge_dwg_{tag}")
    y3 = jnp.stack([s["y_conv"], s["y_ret"], s["y_att"]])
    grads["wb"] = _mm_tn_batch(dp3, y3, a_batched=True, name=f"merge_dwb_{tag}")

    dconv3, grads["conv_w"] = _conv_bwd(s["cols3"], p["conv_w"], dy3[0], name=f"conv_bwd_{tag}")
    ex, done = _carried(hooks, "ret_bwd", all_grads)
    dret4, buf = _ret_bwd(s["cols3"], aux["rope"], aux["ret"], s["states"], dy3[1], name=f"ret_bwd_{tag}", exchange=ex)
    done(buf)
    ex, done = _carried(hooks, "att_bwd", all_grads)
    (datt3, dbias3), buf = _att_bwd(s["cols3"], s["bias3"], s["y_att"], s["lse"], dy3[2], name=f"att_bwd_{tag}",
                                    exchange=ex)
    done(buf)
    grads["rel_bias"] = _rel_bias_grad(dbias3, name=f"bias_grad_{tag}")

    dh2 = _inproj_dh((dconv3, dret4, datt3), p["win"], dh2, name=f"inproj_dh_{tag}")
    grads["win"] = jnp.concatenate([
        _mm_tn_slices(dconv3, s["h2"], name=f"inproj_dw_conv_{tag}"),
        _mm_tn_slices(dret4, s["h2"], name=f"inproj_dw_ret_{tag}"),
        _mm_tn_slices(datt3, s["h2"], name=f"inproj_dw_att_{tag}")], axis=0)
    dx1, grads["nmix"] = _rmsnorm_bwd(dh2, s["x1"], p["nmix"], dx2, name=f"mixnorm_bwd_{tag}")

    return _ffn_grads(dx1, s["x0"], s["h1"], s["g1"], s["u1"], s["a1"], p, "1", tag, grads, all_grads, hooks)


def _device_step(x, target, layers, final_norm, fwd_hooks=None, bwd_hooks=None):
    t = x.shape[0]
    depth = len(layers)
    aux = dict(rope=_rope_tables(t), ret=_ret_consts())
    ex, done = _carried(fwd_hooks[0] if fwd_hooks else None, "start")
    bias_all, buf = _bias_tiles(jnp.stack([p["rel_bias"] for p in layers]), name="bias_tiles", exchange=ex)
    done(buf)
    saved = []
    for l, p in enumerate(layers):
        p["bias3"] = bias_all[l]
        x, s = _layer_fwd(x, p, aux, f"l{l}", fwd_hooks[l] if fwd_hooks else None)
        saved.append(s)
    dx, dfinal, loss_row = _loss_fwd_bwd(x, final_norm, target, name="loss_fwd_bwd")
    grads = [dict() for _ in range(depth)]
    for l in reversed(range(depth)):
        dx = _layer_bwd(dx, layers[l], saved[l], aux, f"l{l}", grads[l], grads, bwd_hooks[l] if bwd_hooks else None)
    return loss_row, dx, grads, dfinal


def _sum_slots(bufs, *, name, tr=512):
    n, r, cdim = bufs[0].shape
    depth = len(bufs)
    tr = _tile(r, tr)
    nt = r // tr

    def body(*refs):
        o_ref = refs[depth]
        for k in range(depth):
            @pl.when(pl.program_id(0) == k)
            def _(p_ref=refs[k]):
                acc = p_ref[0].astype(F32)
                for s in range(1, n):
                    acc = acc + p_ref[s].astype(F32)
                o_ref[...] = acc

    def spec(k):
        return pl.BlockSpec((n, tr, cdim), lambda l, i: (0, jnp.where(l == k, i, jnp.where(l < k, 0, nt - 1)), 0))

    return pl.pallas_call(
        body, name=name, grid=(depth, nt), in_specs=[spec(k) for k in range(depth)],
        out_specs=pl.BlockSpec((None, tr, cdim), lambda l, i: (l, i, 0)),
        out_shape=jax.ShapeDtypeStruct((depth, r, cdim), F32), compiler_params=_cparams("arbitrary", "arbitrary"),
    )(*bufs)


def _all_reduce_small(v, *, name):
    r = v.shape[0]

    def body(x_ref, o_ref, slots, send_sems, recv_sems):
        x, y, c, me = _my_position()
        slots[me] = x_ref[...]
        sends = []
        for k in range(1, N_DEV):
            peer, _ = _peer(x, y, c, k)
            cp = pltpu.make_async_remote_copy(src_ref=x_ref, dst_ref=slots.at[me], send_sem=send_sems.at[k - 1],
                                              recv_sem=recv_sems.at[k - 1], device_id=peer, device_id_type=MESH)
            cp.start()
            sends.append(cp)
        for k in range(1, N_DEV):
            peer, peer_id = _peer(x, y, c, k)
            pltpu.make_async_remote_copy(src_ref=x_ref, dst_ref=slots.at[peer_id], send_sem=send_sems.at[k - 1],
                                         recv_sem=recv_sems.at[k - 1], device_id=peer, device_id_type=MESH).wait_recv()
        for cp in sends:
            cp.wait_send()
        acc = slots[0]
        for s in range(1, N_DEV):
            acc = acc + slots[s]
        o_ref[...] = acc

    return pl.pallas_call(
        body, name=name, in_specs=[pl.BlockSpec(memory_space=pltpu.VMEM)],
        out_specs=pl.BlockSpec(memory_space=pltpu.VMEM), out_shape=jax.ShapeDtypeStruct((r, 128), F32),
        scratch_shapes=[pltpu.VMEM((N_DEV, r, 128), F32), pltpu.SemaphoreType.DMA((N_DEV - 1,)),
                        pltpu.SemaphoreType.DMA((N_DEV - 1,))],
    )(v)


def _adamw(w, g, m, v, *, name, tr=256):
    shape = w.shape
    cdim = shape[-1]
    w2, g2, m2, v2 = (a.reshape(-1, cdim) for a in (w, g, m, v))
    r = w2.shape[0]
    tr = _tile(r, tr) if r % 8 == 0 else r
    c1 = 1.0 - ADAM_B1 ** ADAM_STEP
    c2 = 1.0 - ADAM_B2 ** ADAM_STEP

    def body(w_ref, g_ref, m_ref, v_ref, d_ref, mo_ref, vo_ref):
        gv = g_ref[...]
        mn = ADAM_B1 * m_ref[...] + (1.0 - ADAM_B1) * gv
        vn = ADAM_B2 * v_ref[...] + (1.0 - ADAM_B2) * (gv * gv)
        d_ref[...] = -ADAM_LR * ((mn / c1) / (jnp.sqrt(vn / c2) + ADAM_EPS) + ADAM_WD * w_ref[...])
        mo_ref[...] = mn
        vo_ref[...] = vn

    spec = pl.BlockSpec((tr, cdim), lambda i: (i, 0))
    outs = pl.pallas_call(
        body, name=name, grid=(r // tr,), in_specs=[spec] * 4, out_specs=[spec] * 3,
        out_shape=[jax.ShapeDtypeStruct((r, cdim), F32)] * 3, compiler_params=_cparams("parallel"),
    )(w2, g2, m2, v2)
    return tuple(o.reshape(shape) for o in outs)


GROUPS = {"f1": ("wg1", "wu1", "wd1"), "in": ("win",), "mg": ("wb", "wmg", "wout"), "f2": ("wg2", "wu2", "wd2")}
BIG = tuple(nm for members in GROUPS.values() for nm in members)


def _to_rows(name, w, d):
    depth = w.shape[0]
    if name in ("wg1", "wu1", "wg2", "wu2", "win"):
        return w.transpose(0, 2, 1)
    if name == "wb":
        return w.transpose(0, 1, 3, 2).reshape(depth, -1, d)
    if name == "wmg":
        return w.reshape(depth, -1, d)
    return w


def _from_rows(name, rows, d):
    depth = rows.shape[0]
    if name in ("wg1", "wu1", "wg2", "wu2", "win"):
        return rows.transpose(0, 2, 1)
    if name == "wb":
        return rows.reshape(depth, 3, -1, BRANCH_W).transpose(0, 1, 3, 2)
    if name == "wmg":
        return rows.reshape(depth, 3, -1, d)
    return rows


def _full_from_gathered(name, g, d):
    if name == "wb":
        return g.reshape(N_DEV, 3, -1, BRANCH_W).transpose(1, 0, 2, 3).reshape(3, d, BRANCH_W)
    if name == "wmg":
        return g.reshape(N_DEV, 3, -1, d).transpose(1, 0, 2, 3).reshape(3, d, d)
    return g.reshape(-1, d)


def _gathered_from_full(name, full, d):
    if name == "wb":
        return full.reshape(3, N_DEV, -1, BRANCH_W).transpose(1, 0, 2, 3).reshape(N_DEV, -1, d)
    if name == "wmg":
        return full.reshape(3, N_DEV, -1, d).transpose(1, 0, 2, 3).reshape(N_DEV, -1, d)
    return full.reshape(N_DEV, -1, d)


def kernel(x, ffn1_norm, ffn1_w_gate, ffn1_w_up, ffn1_w_down, mix_norm, w_in, conv_w, rel_bias, w_branch, w_merge_gate, w_out, ffn2_norm, ffn2_w_gate, ffn2_w_up, ffn2_w_down, final_norm, loss_target, m_ffn1_norm, m_ffn1_w_gate, m_ffn1_w_up, m_ffn1_w_down, m_mix_norm, m_w_in, m_conv_w, m_rel_bias, m_w_branch, m_w_merge_gate, m_w_out, m_ffn2_norm, m_ffn2_w_gate, m_ffn2_w_up, m_ffn2_w_down, m_final_norm, v_ffn1_norm, v_ffn1_w_gate, v_ffn1_w_up, v_ffn1_w_down, v_mix_norm, v_w_in, v_conv_w, v_rel_bias, v_w_branch, v_w_merge_gate, v_w_out, v_ffn2_norm, v_ffn2_w_gate, v_ffn2_w_up, v_ffn2_w_down, v_final_norm):
    names = ["ffn1_norm", "ffn1_w_gate", "ffn1_w_up", "ffn1_w_down", "mix_norm", "w_in", "conv_w", "rel_bias",
             "w_branch", "w_merge_gate", "w_out", "ffn2_norm", "ffn2_w_gate", "ffn2_w_up", "ffn2_w_down", "final_norm"]
    weights = dict(zip(names, (ffn1_norm, ffn1_w_gate, ffn1_w_up, ffn1_w_down, mix_norm, w_in, conv_w, rel_bias,
                               w_branch, w_merge_gate, w_out, ffn2_norm, ffn2_w_gate, ffn2_w_up, ffn2_w_down,
                               final_norm)))
    m_in = dict(zip(names, (m_ffn1_norm, m_ffn1_w_gate, m_ffn1_w_up, m_ffn1_w_down, m_mix_norm, m_w_in, m_conv_w,
                            m_rel_bias, m_w_branch, m_w_merge_gate, m_w_out, m_ffn2_norm, m_ffn2_w_gate,
                            m_ffn2_w_up, m_ffn2_w_down, m_final_norm)))
    v_in = dict(zip(names, (v_ffn1_norm, v_ffn1_w_gate, v_ffn1_w_up, v_ffn1_w_down, v_mix_norm, v_w_in, v_conv_w,
                            v_rel_bias, v_w_branch, v_w_merge_gate, v_w_out, v_ffn2_norm, v_ffn2_w_gate,
                            v_ffn2_w_up, v_ffn2_w_down, v_final_norm)))
    big_of = dict(wg1="ffn1_w_gate", wu1="ffn1_w_up", wd1="ffn1_w_down", win="w_in", wb="w_branch",
                  wmg="w_merge_gate", wout="w_out", wg2="ffn2_w_gate", wu2="ffn2_w_up", wd2="ffn2_w_down")
    depth = ffn1_norm.shape[0]
    d = x.shape[-1]
    xs = x.reshape(-1, d)
    target = loss_target.reshape(-1, d)
    _, _, _, me = _my_position()

    shard_rows = {nm: _to_rows(nm, weights[big_of[nm]], d).astype(BF16) for nm in BIG}

    conv_cols = conv_w.shape[-1]
    conv_full, = _exchange("gather", [conv_w.reshape(depth * 3, conv_cols)], name="gather_conv_w")
    conv_full = conv_full.reshape(N_DEV, depth, 3, conv_cols).transpose(1, 2, 0, 3).reshape(depth, 3, -1)
    layers = [dict(n1=ffn1_norm[l][None], nmix=mix_norm[l][None], n2=ffn2_norm[l][None], conv_w=conv_full[l],
                   rel_bias=rel_bias[l]) for l in range(depth)]

    def gather_hook(l, grp):
        members = GROUPS[grp]

        def done(bufs):
            layers[l].update({nm: _full_from_gathered(nm, buf, d) for nm, buf in zip(members, bufs)})

        return (lambda _: ("gather", [shard_rows[nm] for nm in members], [l] * len(members))), done

    fwd_hooks = []
    for l in range(depth):
        hooks = {"ffn1": gather_hook(l, "in"), "inproj": gather_hook(l, "mg"), "att": gather_hook(l, "f2")}
        if l == 0:
            hooks["start"] = gather_hook(0, "f1")
        if l + 1 < depth:
            hooks["ffn2"] = gather_hook(l + 1, "f1")
        fwd_hooks.append(hooks)

    received = {}

    def scatter_hook(l, members):
        def make(all_grads):
            return "scatter", [_gathered_from_full(nm, all_grads[l][nm], d) for nm in members], [None] * len(members)

        return make, (lambda bufs: received.update({(l, nm): buf for nm, buf in zip(members, bufs)}))

    bwd_hooks = []
    for l in range(depth):
        hooks = {"att_bwd": scatter_hook(l, GROUPS["f2"]), "ret_bwd": scatter_hook(l, GROUPS["mg"]),
                 "ffn1_bwd": scatter_hook(l, GROUPS["in"])}
        if l + 1 < depth:
            hooks["ffn2_bwd"] = scatter_hook(l + 1, GROUPS["f1"])
        if l == 0:
            hooks["ffn1_dwg"] = scatter_hook(0, ("wd1",))
            hooks["ffn1_dwu"] = scatter_hook(0, ("wg1",))
        bwd_hooks.append(hooks)

    loss_row, dx, grads, dfinal = _device_step(xs, target, layers, final_norm[None], fwd_hooks, bwd_hooks)
    received[(0, "wu1")], = _exchange("scatter", [_gathered_from_full("wu1", grads[0]["wu1"], d)], name="scatter_l0_wu1")

    grad_w = {big_of[nm]: _from_rows(nm, _sum_slots([received[(l, nm)] for l in range(depth)],
                                                    name=f"sum_grads_{nm}"), d) for nm in BIG}

    small = {"ffn1_norm": jnp.stack([grads[l]["n1"][0] for l in range(depth)]),
             "mix_norm": jnp.stack([grads[l]["nmix"][0] for l in range(depth)]),
             "ffn2_norm": jnp.stack([grads[l]["n2"][0] for l in range(depth)]),
             "final_norm": dfinal[0],
             "rel_bias": jnp.stack([grads[l]["rel_bias"] for l in range(depth)]),
             "conv_w": jnp.stack([grads[l]["conv_w"] for l in range(depth)]),
             "loss": loss_row[0, :1]}
    order = list(small)
    flat = jnp.concatenate([small[k].reshape(-1) for k in order])
    pad = (-flat.shape[0]) % 1024
    summed = _all_reduce_small(jnp.pad(flat, (0, pad)).reshape(-1, 128), name="reduce_small").reshape(-1)
    pos = 0
    for k in order:
        n = small[k].size
        small[k] = summed[pos:pos + n].reshape(small[k].shape)
        pos += n
    small["conv_w"] = lax.dynamic_slice_in_dim(small["conv_w"], me * conv_cols, conv_cols, axis=2)

    grad_w.update({k: small[k] for k in order if k != "loss"})
    delta, new_m, new_v = {}, {}, {}
    for nm in names:
        delta[nm], new_m[nm], new_v[nm] = _adamw(weights[nm], grad_w[nm], m_in[nm], v_in[nm], name=f"adamw_{nm}")
    return (small["loss"].reshape(()), dx.reshape(x.shape), *[grad_w[n] for n in names], *[delta[n] for n in names],
            *[new_m[n] for n in names], *[new_v[n] for n in names])
```

```python
import functools
import math

import jax
import jax.numpy as jnp
from jax import lax
from jax.experimental import pallas as pl
from jax.experimental.pallas import tpu as pltpu

F32 = jnp.float32
BF16 = jnp.bfloat16

N_DEV = 8
EPS = 1e-6
CHUNK = 64
BRANCH_W = 512
N_SLICES = 10
H_RET = 4
DK_RET = 128
H_ATT = 8
DH_ATT = 64
N_PREV_CHUNKS = 8
REL_CLIP = 128
N_REL = 2 * REL_CLIP + 1
NEG_INF = -1e30
ROPE_BASE = 10000.0
ATT_QB = 256
ATT_LOOKBACK = N_PREV_CHUNKS * CHUNK
ATT_WIN = ATT_LOOKBACK + ATT_QB
RET_TB = 256
TOKEN_TK = 2048
PARTIAL_DTYPE = jnp.bfloat16
CONV_HALO = 16

ADAM_LR = 0.001
ADAM_B1 = 0.9
ADAM_B2 = 0.999
ADAM_EPS = 1e-08
ADAM_WD = 0.01
ADAM_STEP = 10

VMEM_LIMIT_BYTES = 56 * 1024 * 1024
MESH = pl.DeviceIdType.MESH


def _cparams(*sem):
    return pltpu.CompilerParams(dimension_semantics=sem, vmem_limit_bytes=VMEM_LIMIT_BYTES)


def _dot(a, b):
    return lax.dot_general(a, b, (((1,), (0,)), ((), ())), preferred_element_type=F32)


def _dot_nt(a, b):
    return lax.dot_general(a, b, (((1,), (1,)), ((), ())), preferred_element_type=F32)


def _dot_tn(a, b):
    return lax.dot_general(a, b, (((0,), (0,)), ((), ())), preferred_element_type=F32)


def _bf(v):
    return v.astype(BF16)


def _sigmoid(v):
    return 1.0 / (1.0 + jnp.exp(-v))


def _tile(n, want):
    if n <= want:
        return n
    for t in range(want - want % 128, 0, -128):
        if n % t == 0:
            return t
    t = want
    while n % t:
        t //= 2
    return t


def _my_position():
    x, y, c = lax.axis_index("x"), lax.axis_index("y"), lax.axis_index("c")
    return x, y, c, 4 * x + 2 * y + c


def _peer(x, y, c, k):
    px = 1 - x if k & 4 else x
    py = 1 - y if k & 2 else y
    pc = 1 - c if k & 1 else c
    return (px, py, pc), 4 * px + 2 * py + pc


def _exchange_sems(n):
    return [pltpu.SemaphoreType.DMA((n, N_DEV - 1)), pltpu.SemaphoreType.DMA((n, N_DEV - 1)),
            pltpu.SemaphoreType.DMA((n,))]


def _exchange_src(kind, src_ref, layer, block):
    if kind == "gather":
        return src_ref if layer is None else src_ref.at[layer]
    return src_ref.at[block]


def _exchange_copy(kind, src_ref, layer, dst_ref, sems, a, k, pos, incoming):
    x, y, c, me = pos
    peer, peer_id = _peer(x, y, c, k)
    return pltpu.make_async_remote_copy(src_ref=_exchange_src(kind, src_ref, layer, me if incoming else peer_id),
                                        dst_ref=dst_ref.at[peer_id if incoming else me],
                                        send_sem=sems[0].at[a, k - 1], recv_sem=sems[1].at[a, k - 1],
                                        device_id=peer, device_id_type=MESH)


def _exchange_local(kind, src_ref, layer, dst_ref, sems, a, me):
    return pltpu.make_async_copy(_exchange_src(kind, src_ref, layer, me), dst_ref.at[me], sems[2].at[a])


def _exchange_start(kind, layers, src_refs, dst_refs, sems):
    pos = _my_position()
    for a, (src_ref, layer, dst_ref) in enumerate(zip(src_refs, layers, dst_refs)):
        _exchange_local(kind, src_ref, layer, dst_ref, sems, a, pos[3]).start()
        for k in range(1, N_DEV):
            _exchange_copy(kind, src_ref, layer, dst_ref, sems, a, k, pos, False).start()


def _exchange_wait(kind, layers, src_refs, dst_refs, sems):
    pos = _my_position()
    for a, (src_ref, layer, dst_ref) in enumerate(zip(src_refs, layers, dst_refs)):
        for k in range(1, N_DEV):
            _exchange_copy(kind, src_ref, layer, dst_ref, sems, a, k, pos, True).wait_recv()
        for k in range(1, N_DEV):
            _exchange_copy(kind, src_ref, layer, dst_ref, sems, a, k, pos, False).wait_send()
        _exchange_local(kind, src_ref, layer, dst_ref, sems, a, pos[3]).wait()


def _exchange_shapes(kind, srcs, layers):
    if kind == "gather":
        return [jax.ShapeDtypeStruct((N_DEV,) + (s.shape if l is None else s.shape[1:]), s.dtype)
                for s, l in zip(srcs, layers)]
    return [jax.ShapeDtypeStruct(s.shape, s.dtype) for s in srcs]


def _exchange(kind, srcs, layers=None, *, name):
    n = len(srcs)
    layers = layers or [None] * n
    hbm = pl.BlockSpec(memory_space=pl.ANY)

    def body(*refs):
        src_refs, dst_refs, sems = refs[:n], refs[n:2 * n], refs[2 * n:]
        _exchange_start(kind, layers, src_refs, dst_refs, sems)
        _exchange_wait(kind, layers, src_refs, dst_refs, sems)

    return pl.pallas_call(body, name=name, in_specs=[hbm] * n, out_specs=[hbm] * n,
                          out_shape=_exchange_shapes(kind, srcs, layers), scratch_shapes=_exchange_sems(n))(*srcs)


def _pcall(body, args, *, name, grid, in_specs, out_specs, out_shape, scratch_shapes, sem, exchange=None):
    if exchange is None:
        outs = pl.pallas_call(body, name=name, grid=grid, in_specs=in_specs, out_specs=out_specs, out_shape=out_shape,
                              scratch_shapes=scratch_shapes, compiler_params=_cparams(*sem))(*args)
        return outs, None
    kind, srcs, layers = exchange
    n_in, n_out, n_scr, n_x = len(in_specs), len(out_specs), len(scratch_shapes), len(srcs)
    hbm = pl.BlockSpec(memory_space=pl.ANY)

    def carrier(*refs):
        ins, refs = refs[:n_in], refs[n_in:]
        src_refs, refs = refs[:n_x], refs[n_x:]
        outs, refs = refs[:n_out], refs[n_out:]
        dst_refs, refs = refs[:n_x], refs[n_x:]
        scr, sems = refs[:n_scr], refs[n_scr:]
        ids = [pl.program_id(a) for a in range(len(grid))]
        first = functools.reduce(jnp.logical_and, [i == 0 for i in ids])
        last = functools.reduce(jnp.logical_and, [i == g - 1 for i, g in zip(ids, grid)])

        @pl.when(first)
        def _():
            _exchange_start(kind, layers, src_refs, dst_refs, sems)

        body(*ins, *outs, *scr)

        @pl.when(last)
        def _():
            _exchange_wait(kind, layers, src_refs, dst_refs, sems)

    outs = pl.pallas_call(
        carrier, name=name, grid=grid, in_specs=list(in_specs) + [hbm] * n_x, out_specs=list(out_specs) + [hbm] * n_x,
        out_shape=list(out_shape) + _exchange_shapes(kind, srcs, layers),
        scratch_shapes=list(scratch_shapes) + _exchange_sems(n_x),
        compiler_params=_cparams(*(["arbitrary"] * len(grid))))(*args, *srcs)
    return outs[:n_out], outs[n_out:]


def _matmul(a, b, *, dims, grid, a_spec, b_spec, o_spec, out_shape, acc_shape, name, scale=1.0,
            res=None, res_spec=None, exchange=None):
    nk = grid[3]
    has_res = res is not None

    def body(*refs):
        if has_res:
            a_ref, b_ref, r_ref, o_ref = refs[:4]
        else:
            a_ref, b_ref, o_ref = refs[:3]
            r_ref = None

        def finish(acc):
            if scale != 1.0:
                acc = acc * scale
            if r_ref is not None:
                acc = acc + r_ref[...].astype(F32)
            o_ref[...] = acc.astype(o_ref.dtype)

        part = lax.dot_general(_bf(a_ref[...]), _bf(b_ref[...]), (dims, ((), ())), preferred_element_type=F32)
        if nk == 1:
            finish(part)
        else:
            acc_ref = refs[-1]
            k = pl.program_id(3)

            @pl.when(k == 0)
            def _():
                acc_ref[...] = part

            @pl.when(jnp.logical_and(k > 0, k < nk - 1))
            def _():
                acc_ref[...] += part

            @pl.when(k == nk - 1)
            def _():
                finish(acc_ref[...] + part)

    in_specs = [a_spec, b_spec] + ([res_spec] if has_res else [])
    args = (a, b) + ((res,) if has_res else ())
    outs, moved = _pcall(
        body, args, name=name, grid=grid, in_specs=in_specs, out_specs=[o_spec], out_shape=[out_shape],
        scratch_shapes=[] if nk == 1 else [pltpu.VMEM(acc_shape, F32)],
        sem=("parallel", "parallel", "parallel", "arbitrary"), exchange=exchange)
    return outs[0] if exchange is None else (outs[0], moved)


NT = ((1,), (1,))
NN = ((1,), (0,))
TN = ((0,), (0,))


def _mm_rows(a, b, *, dims, out_dtype, name, res=None, scale=1.0, tm=1024, tn=1024, tk=1024):
    m, kdim = a.shape
    n = b.shape[1] if dims == NN else b.shape[0]
    tm, tn, tk = _tile(m, tm), _tile(n, tn), _tile(kdim, tk)
    grid = (1, m // tm, n // tn, kdim // tk)
    a_spec = pl.BlockSpec((tm, tk), lambda s, i, j, k: (i, k))
    if dims == NN:
        b_spec = pl.BlockSpec((tk, tn), lambda s, i, j, k: (k, j))
    else:
        b_spec = pl.BlockSpec((tn, tk), lambda s, i, j, k: (j, k))
    o_spec = pl.BlockSpec((tm, tn), lambda s, i, j, k: (i, j))
    return _matmul(a, b, dims=dims, grid=grid, a_spec=a_spec, b_spec=b_spec, o_spec=o_spec,
                   out_shape=jax.ShapeDtypeStruct((m, n), out_dtype), acc_shape=(tm, tn), name=name,
                   res=res, res_spec=o_spec if res is not None else None, scale=scale)


def _mm_to_slices(a, b_t, *, name, tm=2048, exchange=None):
    m, kdim = a.shape
    n = b_t.shape[0]
    tm = _tile(m, tm)
    w = BRANCH_W
    grid = (1, m // tm, n // w, 1)
    return _matmul(a, b_t, dims=NT, grid=grid,
                   a_spec=pl.BlockSpec((tm, kdim), lambda s, i, j, k: (i, 0)),
                   b_spec=pl.BlockSpec((w, kdim), lambda s, i, j, k: (j, 0)),
                   o_spec=pl.BlockSpec((None, tm, w), lambda s, i, j, k: (j, i, 0)),
                   out_shape=jax.ShapeDtypeStruct((n // w, m, w), BF16), acc_shape=(tm, w), name=name,
                   exchange=exchange)


def _inproj_dh(parts, w_t, res, *, name, tm=512):
    t, n = res.shape
    w = BRANCH_W
    tm = _tile(t, tm)
    counts = [p.shape[0] for p in parts]
    assert sum(counts) * w == w_t.shape[0]

    def body(*refs):
        w_ref, r_ref, o_ref = refs[len(parts):]
        cols = jnp.concatenate([p_ref[s] for p_ref, cnt in zip(refs, counts) for s in range(cnt)], axis=1)
        o_ref[...] = r_ref[...] + _dot(cols, w_ref[...])

    row = pl.BlockSpec((tm, n), lambda i: (i, 0))
    return pl.pallas_call(
        body, name=name, grid=(t // tm,),
        in_specs=[pl.BlockSpec((cnt, tm, w), lambda i: (0, i, 0)) for cnt in counts]
        + [pl.BlockSpec(w_t.shape, lambda i: (0, 0)), row],
        out_specs=row, out_shape=jax.ShapeDtypeStruct((t, n), F32), compiler_params=_cparams("parallel"),
    )(*parts, w_t, res)


def _mm_tn(a, b, *, name, tm=1408, tn=1024, tk=TOKEN_TK, exchange=None):
    t, m = a.shape
    n = b.shape[1]
    tm, tn, tk = _tile(m, tm), _tile(n, tn), _tile(t, tk)
    grid = (1, m // tm, n // tn, t // tk)
    return _matmul(a, b, dims=TN, grid=grid,
                   a_spec=pl.BlockSpec((tk, tm), lambda s, i, j, k: (k, i)),
                   b_spec=pl.BlockSpec((tk, tn), lambda s, i, j, k: (k, j)),
                   o_spec=pl.BlockSpec((tm, tn), lambda s, i, j, k: (i, j)),
                   out_shape=jax.ShapeDtypeStruct((m, n), PARTIAL_DTYPE), acc_shape=(tm, tn), name=name,
                   exchange=exchange)


def _mm_tn_slices(a3, b, *, name, tn=1024, tk=TOKEN_TK):
    s_n, t, w = a3.shape
    n = b.shape[1]
    tn, tk = _tile(n, tn), _tile(t, tk)
    grid = (1, s_n, n // tn, t // tk)
    return _matmul(a3, b, dims=TN, grid=grid,
                   a_spec=pl.BlockSpec((None, tk, w), lambda s, i, j, k: (i, k, 0)),
                   b_spec=pl.BlockSpec((tk, tn), lambda s, i, j, k: (k, j)),
                   o_spec=pl.BlockSpec((w, tn), lambda s, i, j, k: (i, j)),
                   out_shape=jax.ShapeDtypeStruct((s_n * w, n), PARTIAL_DTYPE), acc_shape=(w, tn), name=name)


def _mm_tn_batch(a, b3, *, name, a_batched, tm=1024, tn=1024, tk=TOKEN_TK):
    s_n, t, n = b3.shape
    m = a.shape[-1]
    tm, tn, tk = _tile(m, tm), _tile(n, tn), _tile(t, tk)
    grid = (s_n, m // tm, n // tn, t // tk)
    if a_batched:
        a_spec = pl.BlockSpec((None, tk, tm), lambda s, i, j, k: (s, k, i))
    else:
        a_spec = pl.BlockSpec((tk, tm), lambda s, i, j, k: (k, i))
    return _matmul(a, b3, dims=TN, grid=grid, a_spec=a_spec,
                   b_spec=pl.BlockSpec((None, tk, tn), lambda s, i, j, k: (s, k, j)),
                   o_spec=pl.BlockSpec((None, tm, tn), lambda s, i, j, k: (s, i, j)),
                   out_shape=jax.ShapeDtypeStruct((s_n, m, n), PARTIAL_DTYPE), acc_shape=(tm, tn), name=name)


def _norm_parts(xf):
    r = lax.rsqrt(jnp.mean(xf * xf, axis=-1, keepdims=True) + EPS)
    return xf * r, r


def _norm_bwd(dh, xhat, r, w):
    dxhat = dh * w
    dx = r * (dxhat - xhat * jnp.mean(dxhat * xhat, axis=-1, keepdims=True))
    return dx, jnp.sum(dh * xhat, axis=0, keepdims=True)


def _rmsnorm_fwd(x, w, *, name, tm=1024):
    t, d = x.shape
    tm = _tile(t, tm)

    def body(x_ref, w_ref, h_ref):
        xhat, _ = _norm_parts(x_ref[...])
        h_ref[...] = _bf(xhat * w_ref[...])

    return pl.pallas_call(
        body, name=name, grid=(t // tm,),
        in_specs=[pl.BlockSpec((tm, d), lambda i: (i, 0)), pl.BlockSpec((1, d), lambda i: (0, 0))],
        out_specs=pl.BlockSpec((tm, d), lambda i: (i, 0)),
        out_shape=jax.ShapeDtypeStruct((t, d), BF16), compiler_params=_cparams("parallel"),
    )(x, w)


def _rmsnorm_bwd(dh, x, w, dres, *, name, tm=1024):
    t, d = x.shape
    tm = _tile(t, tm)

    def body(dh_ref, x_ref, w_ref, dres_ref, dx_ref, dw_ref):
        xhat, r = _norm_parts(x_ref[...])
        dx, dw = _norm_bwd(dh_ref[...], xhat, r, w_ref[...])
        dx_ref[...] = dres_ref[...] + dx

        @pl.when(pl.program_id(0) == 0)
        def _():
            dw_ref[...] = dw

        @pl.when(pl.program_id(0) > 0)
        def _():
            dw_ref[...] += dw

    row = pl.BlockSpec((tm, d), lambda i: (i, 0))
    vec = pl.BlockSpec((1, d), lambda i: (0, 0))
    return pl.pallas_call(
        body, name=name, grid=(t // tm,), in_specs=[row, row, vec, row], out_specs=[row, vec],
        out_shape=[jax.ShapeDtypeStruct((t, d), F32), jax.ShapeDtypeStruct((1, d), F32)],
        compiler_params=_cparams("arbitrary"),
    )(dh, x, w, dres)


def _loss_fwd_bwd(x, w, target, *, name, tm=1024):
    t, d = x.shape
    tm = _tile(t, tm)

    def body(x_ref, w_ref, t_ref, dx_ref, dw_ref, loss_ref):
        xhat, r = _norm_parts(x_ref[...])
        wv = w_ref[...]
        err = xhat * wv - t_ref[...]
        dx, dw = _norm_bwd(err * (1.0 / d), xhat, r, wv)
        dx_ref[...] = dx
        part = jnp.full((1, 128), 0.5 / d, F32) * jnp.sum(err * err)

        @pl.when(pl.program_id(0) == 0)
        def _():
            dw_ref[...] = dw
            loss_ref[...] = part

        @pl.when(pl.program_id(0) > 0)
        def _():
            dw_ref[...] += dw
            loss_ref[...] += part

    row = pl.BlockSpec((tm, d), lambda i: (i, 0))
    vec = pl.BlockSpec((1, d), lambda i: (0, 0))
    return pl.pallas_call(
        body, name=name, grid=(t // tm,), in_specs=[row, vec, row],
        out_specs=[row, vec, pl.BlockSpec((1, 128), lambda i: (0, 0))],
        out_shape=[jax.ShapeDtypeStruct((t, d), F32), jax.ShapeDtypeStruct((1, d), F32),
                   jax.ShapeDtypeStruct((1, 128), F32)],
        compiler_params=_cparams("arbitrary"),
    )(x, w, target)


def _ffn_tiles(t, f):
    tf = f
    for cand in (1408, 1024, 512, 256, 128):
        if f % cand == 0:
            tf = cand
            break
    return _tile(t, 512), tf


def _ffn_fwd(x, nw, wg_t, wu_t, wd, *, name, exchange=None):
    t, d = x.shape
    f = wd.shape[0]
    tm, tf = _ffn_tiles(t, f)
    nf = f // tf

    def body(x_ref, nw_ref, wg_ref, wu_ref, wd_ref, xo_ref, h_ref, g_ref, u_ref, a_ref, hs_ref, acc_ref):
        j = pl.program_id(1)

        @pl.when(j == 0)
        def _():
            xhat, _ = _norm_parts(x_ref[...])
            hb = _bf(xhat * nw_ref[...])
            hs_ref[...] = hb
            h_ref[...] = hb

        hb = hs_ref[...]
        g = _dot_nt(hb, wg_ref[...])
        u = _dot_nt(hb, wu_ref[...])
        a = _bf(g * _sigmoid(g) * u)
        g_ref[...] = _bf(g)
        u_ref[...] = _bf(u)
        a_ref[...] = a
        part = _dot(a, wd_ref[...])

        @pl.when(j == 0)
        def _():
            acc_ref[...] = part

        @pl.when(j > 0)
        def _():
            acc_ref[...] += part

        @pl.when(j == nf - 1)
        def _():
            xo_ref[...] = x_ref[...] + 0.5 * acc_ref[...]

    row = pl.BlockSpec((tm, d), lambda i, j: (i, 0))
    wspec = pl.BlockSpec((tf, d), lambda i, j: (j, 0))
    hid = pl.BlockSpec((tm, tf), lambda i, j: (i, j))
    return _pcall(
        body, (x, nw, wg_t, wu_t, wd), name=name, grid=(t // tm, nf),
        in_specs=[row, pl.BlockSpec((1, d), lambda i, j: (0, 0)), wspec, wspec, wspec],
        out_specs=[row, row, hid, hid, hid],
        out_shape=[jax.ShapeDtypeStruct((t, d), F32), jax.ShapeDtypeStruct((t, d), BF16)]
        + [jax.ShapeDtypeStruct((t, f), BF16)] * 3,
        scratch_shapes=[pltpu.VMEM((tm, d), BF16), pltpu.VMEM((tm, d), F32)],
        sem=("parallel", "arbitrary"), exchange=exchange)


def _ffn_bwd(dxo, x, nw, g, u, wg_t, wu_t, wd, *, name, exchange=None):
    t, d = x.shape
    f = wd.shape[0]
    tm, tf = _ffn_tiles(t, f)
    nf = f // tf

    def body(dxo_ref, x_ref, nw_ref, g_ref, u_ref, wg_ref, wu_ref, wd_ref,
             dx_ref, dy_ref, dg_ref, du_ref, dnw_ref, dys_ref, acc_ref):
        i, j = pl.program_id(0), pl.program_id(1)

        @pl.when(j == 0)
        def _():
            dyb = _bf(0.5 * dxo_ref[...])
            dys_ref[...] = dyb
            dy_ref[...] = dyb

        da = _dot_nt(dys_ref[...], wd_ref[...])
        gv = g_ref[...].astype(F32)
        uv = u_ref[...].astype(F32)
        s = _sigmoid(gv)
        dg = _bf(da * uv * (s * (1.0 + gv * (1.0 - s))))
        du = _bf(da * (gv * s))
        dg_ref[...] = dg
        du_ref[...] = du
        part = _dot(dg, wg_ref[...]) + _dot(du, wu_ref[...])

        @pl.when(j == 0)
        def _():
            acc_ref[...] = part

        @pl.when(j > 0)
        def _():
            acc_ref[...] += part

        @pl.when(j == nf - 1)
        def _():
            xhat, r = _norm_parts(x_ref[...])
            dx, dw = _norm_bwd(acc_ref[...], xhat, r, nw_ref[...])
            dx_ref[...] = dxo_ref[...] + dx

            @pl.when(i == 0)
            def _():
                dnw_ref[...] = dw

            @pl.when(i > 0)
            def _():
                dnw_ref[...] += dw

    row = pl.BlockSpec((tm, d), lambda i, j: (i, 0))
    vec = pl.BlockSpec((1, d), lambda i, j: (0, 0))
    wspec = pl.BlockSpec((tf, d), lambda i, j: (j, 0))
    hid = pl.BlockSpec((tm, tf), lambda i, j: (i, j))
    return _pcall(
        body, (dxo, x, nw, g, u, wg_t, wu_t, wd), name=name, grid=(t // tm, nf),
        in_specs=[row, row, vec, hid, hid, wspec, wspec, wspec],
        out_specs=[row, row, hid, hid, vec],
        out_shape=[jax.ShapeDtypeStruct((t, d), F32), jax.ShapeDtypeStruct((t, d), BF16),
                   jax.ShapeDtypeStruct((t, f), BF16), jax.ShapeDtypeStruct((t, f), BF16),
                   jax.ShapeDtypeStruct((1, d), F32)],
        scratch_shapes=[pltpu.VMEM((tm, d), BF16), pltpu.VMEM((tm, d), F32)],
        sem=("arbitrary", "arbitrary"), exchange=exchange)


def _shift_down(prev, cur, n):
    ext = jnp.concatenate([prev, cur], axis=0)
    return pltpu.roll(ext, n, axis=0)[prev.shape[0]:]


def _shift_up(cur, nxt, n):
    ext = jnp.concatenate([cur, nxt], axis=0)
    return pltpu.roll(ext, ext.shape[0] - n, axis=0)[:cur.shape[0]]


def _conv_specs(t, tb):
    hb = tb // CONV_HALO
    last = t // CONV_HALO - 1

    def tile(s):
        return pl.BlockSpec((None, tb, 128), lambda c, i: (s, i, c))

    def prev(s):
        return pl.BlockSpec((None, CONV_HALO, 128), lambda c, i: (s, jnp.maximum(i * hb - 1, 0), c))

    def nxt(s):
        return pl.BlockSpec((None, CONV_HALO, 128), lambda c, i: (s, jnp.minimum((i + 1) * hb, last), c))

    return tile, prev, nxt


def _conv_fwd(cols3, conv_w, *, name, tb=1024):
    _, t, bw = cols3.shape
    tb = _tile(t, tb)
    tile, prev, _ = _conv_specs(t, tb)

    def body(u_ref, b_ref, c_ref, up_ref, cp_ref, w_ref, y_ref):
        first = pl.program_id(1) == 0
        z = c_ref[...].astype(F32) * u_ref[...].astype(F32)
        zp = jnp.where(first, 0.0, cp_ref[...].astype(F32) * up_ref[...].astype(F32))
        conv = w_ref[0:1, :] * _shift_down(zp, z, 2) + w_ref[1:2, :] * _shift_down(zp, z, 1) + w_ref[2:3, :] * z
        y_ref[...] = _bf(b_ref[...].astype(F32) * conv)

    return pl.pallas_call(
        body, name=name, grid=(bw // 128, t // tb),
        in_specs=[tile(0), tile(1), tile(2), prev(0), prev(2), pl.BlockSpec((3, 128), lambda c, i: (0, c))],
        out_specs=pl.BlockSpec((tb, 128), lambda c, i: (i, c)),
        out_shape=jax.ShapeDtypeStruct((t, bw), BF16), compiler_params=_cparams("parallel", "parallel"),
    )(cols3, cols3, cols3, cols3, cols3, conv_w)


def _conv_bwd(cols3, conv_w, dy, *, name, tb=1024):
    _, t, bw = cols3.shape
    tb = _tile(t, tb)
    nt = t // tb
    tile, prev, nxt = _conv_specs(t, tb)
    hb = tb // CONV_HALO
    last = t // CONV_HALO - 1

    def body(u_ref, b_ref, c_ref, up_ref, cp_ref, bn_ref, dy_ref, dyn_ref, w_ref, d3_ref, dw_ref):
        i = pl.program_id(1)
        uv, bv, cv = u_ref[...].astype(F32), b_ref[...].astype(F32), c_ref[...].astype(F32)
        dyv = dy_ref[...].astype(F32)
        z = cv * uv
        zp = jnp.where(i == 0, 0.0, cp_ref[...].astype(F32) * up_ref[...].astype(F32))
        z1, z2 = _shift_down(zp, z, 1), _shift_down(zp, z, 2)
        w0, w1, w2 = w_ref[0:1, :], w_ref[1:2, :], w_ref[2:3, :]
        conv = w0 * z2 + w1 * z1 + w2 * z
        dconv = dyv * bv
        dconv_n = jnp.where(i == nt - 1, 0.0, dyn_ref[...].astype(F32) * bn_ref[...].astype(F32))
        dz = w2 * dconv + w1 * _shift_up(dconv, dconv_n, 1) + w0 * _shift_up(dconv, dconv_n, 2)
        d3_ref[0] = _bf(dz * cv)
        d3_ref[1] = _bf(dyv * conv)
        d3_ref[2] = _bf(dz * uv)
        dws = [jnp.sum(dconv * zz, axis=0, keepdims=True) for zz in (z2, z1, z)]

        @pl.when(i == 0)
        def _():
            for j in range(3):
                dw_ref[j:j + 1, :] = dws[j]

        @pl.when(i > 0)
        def _():
            for j in range(3):
                dw_ref[j:j + 1, :] += dws[j]

    dy_tile = pl.BlockSpec((None, tb, 128), lambda c, i: (0, i, c))
    dy_next = pl.BlockSpec((None, CONV_HALO, 128), lambda c, i: (0, jnp.minimum((i + 1) * hb, last), c))
    wspec = pl.BlockSpec((3, 128), lambda c, i: (0, c))
    return pl.pallas_call(
        body, name=name, grid=(bw // 128, nt),
        in_specs=[tile(0), tile(1), tile(2), prev(0), prev(2), nxt(1), dy_tile, dy_next, wspec],
        out_specs=[pl.BlockSpec((3, tb, 128), lambda c, i: (0, i, c)), wspec],
        out_shape=[jax.ShapeDtypeStruct((3, t, bw), BF16), jax.ShapeDtypeStruct((3, bw), F32)],
        compiler_params=_cparams("parallel", "arbitrary"),
    )(cols3, cols3, cols3, cols3, cols3, cols3, dy, dy, conv_w)


def _ret_consts():
    log_gamma = jnp.log1p(-jnp.exp2(-5.0 - jnp.arange(H_RET, dtype=F32)))
    pos = jnp.arange(CHUNK, dtype=F32)
    d_intra = jnp.exp(log_gamma[:, None, None] * jnp.abs(pos[:, None] - pos[None, :]))
    q_decay = jnp.exp(log_gamma[:, None] * (pos + 1.0))
    k_decay = jnp.exp(log_gamma[:, None] * (CHUNK - 1.0 - pos))
    chunk_decay = jnp.exp(log_gamma * CHUNK)
    wide = (H_RET, CHUNK, DK_RET)
    return (d_intra, jnp.broadcast_to(q_decay[:, :, None], wide), jnp.broadcast_to(k_decay[:, :, None], wide),
            jnp.broadcast_to(chunk_decay[:, None, None], (H_RET, 1, DK_RET)))


def _rope_tables(t):
    inv_freq = ROPE_BASE ** (-jnp.linspace(0.0, 1.0, DK_RET // 2, dtype=F32))
    ang = jnp.arange(t, dtype=F32)[:, None] * inv_freq[None, :]
    cos, sin = jnp.cos(ang), jnp.sin(ang)
    return jnp.concatenate([cos, cos], axis=1), jnp.concatenate([-sin, sin], axis=1)


def _rope(v, cc, ss):
    return v * cc + pltpu.roll(v, DK_RET // 2, axis=1) * ss


def _ret_in_specs(tb, blk):
    def col(s):
        return pl.BlockSpec((None, tb, H_RET * DK_RET), lambda i: (s, blk(i), 0))

    tab = pl.BlockSpec((tb, DK_RET), lambda i: (blk(i), 0))
    return ([col(3), col(4), col(5), col(6), tab, tab,
             pl.BlockSpec((H_RET, CHUNK, CHUNK), lambda i: (0, 0, 0)),
             pl.BlockSpec((H_RET, CHUNK, DK_RET), lambda i: (0, 0, 0)),
             pl.BlockSpec((H_RET, CHUNK, DK_RET), lambda i: (0, 0, 0)),
             pl.BlockSpec((H_RET, 1, DK_RET), lambda i: (0, 0, 0))])


def _ret_fwd(cols3, tables, consts, *, name):
    _, t, bw = cols3.shape
    tb = _tile(t, RET_TB)
    ncb = tb // CHUNK
    scale = DK_RET ** -0.5

    def body(q_ref, k_ref, v_ref, g_ref, cc_ref, ss_ref, di_ref, qd_ref, kd_ref, cd_ref, y_ref, st_ref, state):
        @pl.when(pl.program_id(0) == 0)
        def _():
            state[...] = jnp.zeros_like(state)

        for c in range(ncb):
            rows = pl.ds(c * CHUNK, CHUNK)
            cc, ss = cc_ref[rows, :], ss_ref[rows, :]
            for h in range(H_RET):
                lanes = pl.ds(h * DK_RET, DK_RET)
                qs = _rope(q_ref[rows, lanes].astype(F32), cc, ss) * scale
                kr = _rope(k_ref[rows, lanes].astype(F32), cc, ss)
                vb = v_ref[rows, lanes]
                gv = g_ref[rows, lanes].astype(F32)
                s_in = state[h]
                st_ref[h, c] = s_in
                inner = _dot_nt(_bf(qs), _bf(kr)) * di_ref[h]
                o = _dot(_bf(inner), vb) + _dot(_bf(qs * qd_ref[h]), _bf(s_in))
                state[h] = s_in * cd_ref[h] + _dot_tn(_bf(kr * kd_ref[h]), vb)
                on = o * lax.rsqrt(jnp.mean(o * o, axis=-1, keepdims=True) + EPS)
                y_ref[rows, lanes] = _bf(gv * _sigmoid(gv) * on)

    return pl.pallas_call(
        body, name=name, grid=(t // tb,),
        in_specs=_ret_in_specs(tb, lambda i: i),
        out_specs=[pl.BlockSpec((tb, bw), lambda i: (i, 0)),
                   pl.BlockSpec((H_RET, ncb, DK_RET, DK_RET), lambda i: (0, i, 0, 0))],
        out_shape=[jax.ShapeDtypeStruct((t, bw), BF16),
                   jax.ShapeDtypeStruct((H_RET, t // CHUNK, DK_RET, DK_RET), F32)],
        scratch_shapes=[pltpu.VMEM((H_RET, DK_RET, DK_RET), F32)],
        compiler_params=_cparams("arbitrary"),
    )(cols3, cols3, cols3, cols3, tables[0], tables[1], *consts)


def _ret_bwd(cols3, tables, consts, states, dy, *, name, exchange=None):
    _, t, bw = cols3.shape
    tb = _tile(t, RET_TB)
    ncb = tb // CHUNK
    nb = t // tb
    scale = DK_RET ** -0.5

    def body(q_ref, k_ref, v_ref, g_ref, cc_ref, ss_ref, di_ref, qd_ref, kd_ref, cd_ref, st_ref, dy_ref,
             d4_ref, dstate):
        @pl.when(pl.program_id(0) == 0)
        def _():
            dstate[...] = jnp.zeros_like(dstate)

        heads = range(H_RET)
        lanes_of = [pl.ds(h * DK_RET, DK_RET) for h in heads]
        di, qd, kd = [di_ref[h] for h in heads], [qd_ref[h] for h in heads], [kd_ref[h] for h in heads]
        carried = [dstate[h] for h in heads]
        for c in reversed(range(ncb)):
            rows = pl.ds(c * CHUNK, CHUNK)
            cc, ss = cc_ref[rows, :], ss_ref[rows, :]
            vb = [v_ref[rows, ln] for ln in lanes_of]
            gv = [g_ref[rows, ln].astype(F32) for ln in lanes_of]
            dyv = [dy_ref[rows, ln].astype(F32) for ln in lanes_of]
            s_in = [_bf(st_ref[h, c]) for h in heads]
            qs = [_rope(q_ref[rows, ln].astype(F32), cc, ss) * scale for ln in lanes_of]
            kr = [_rope(k_ref[rows, ln].astype(F32), cc, ss) for ln in lanes_of]
            qsb, krb = [_bf(v) for v in qs], [_bf(v) for v in kr]
            qdb = [_bf(qs[h] * qd[h]) for h in heads]
            kdb = [_bf(kr[h] * kd[h]) for h in heads]
            innerb = [_bf(_dot_nt(qsb[h], krb[h]) * di[h]) for h in heads]
            o = [_dot(innerb[h], vb[h]) + _dot(qdb[h], s_in[h]) for h in heads]
            r = [lax.rsqrt(jnp.mean(v * v, axis=-1, keepdims=True) + EPS) for v in o]
            on = [o[h] * r[h] for h in heads]
            sg = [_sigmoid(v) for v in gv]
            dgv = [_bf(dyv[h] * on[h] * (sg[h] * (1.0 + gv[h] * (1.0 - sg[h])))) for h in heads]
            don = [dyv[h] * (gv[h] * sg[h]) for h in heads]
            dob = [_bf(r[h] * (don[h] - on[h] * jnp.mean(don[h] * on[h], axis=-1, keepdims=True))) for h in heads]
            dinner = [_bf(_dot_nt(dob[h], vb[h]) * di[h]) for h in heads]
            dqs = [_dot(dinner[h], krb[h]) + _dot_nt(dob[h], s_in[h]) * qd[h] for h in heads]
            dkr = [_dot_tn(dinner[h], qsb[h]) for h in heads]
            dv = [_dot_tn(innerb[h], dob[h]) for h in heads]
            dnew = [_dot_tn(qdb[h], dob[h]) for h in heads]
            dstb = [_bf(v) for v in carried]
            dv = [dv[h] + _dot(kdb[h], dstb[h]) for h in heads]
            dkr = [dkr[h] + _dot_nt(vb[h], dstb[h]) * kd[h] for h in heads]
            carried = [carried[h] * cd_ref[h] + dnew[h] for h in heads]
            for h, ln in enumerate(lanes_of):
                d4_ref[0, rows, ln] = _bf(_rope_bwd(dqs[h] * scale, cc, ss))
                d4_ref[1, rows, ln] = _bf(_rope_bwd(dkr[h], cc, ss))
                d4_ref[2, rows, ln] = _bf(dv[h])
                d4_ref[3, rows, ln] = dgv[h]
        for h in heads:
            dstate[h] = carried[h]

    rev = lambda i: nb - 1 - i
    outs, moved = _pcall(
        body, (cols3, cols3, cols3, cols3, tables[0], tables[1], *consts, states, dy), name=name, grid=(nb,),
        in_specs=_ret_in_specs(tb, rev)
        + [pl.BlockSpec((H_RET, ncb, DK_RET, DK_RET), lambda i: (0, rev(i), 0, 0)),
           pl.BlockSpec((None, tb, bw), lambda i: (1, rev(i), 0))],
        out_specs=[pl.BlockSpec((4, tb, bw), lambda i: (0, rev(i), 0))],
        out_shape=[jax.ShapeDtypeStruct((4, t, bw), BF16)],
        scratch_shapes=[pltpu.VMEM((H_RET, DK_RET, DK_RET), F32)],
        sem=("arbitrary",), exchange=exchange)
    return outs[0], moved


def _rope_bwd(dv, cc, ss):
    return dv * cc + pltpu.roll(dv * ss, DK_RET // 2, axis=1)


def _att_window(i):
    return pl.multiple_of(jnp.maximum(i - ATT_LOOKBACK // ATT_QB, 0) * ATT_QB, ATT_QB)


def _att_mask(v):
    qchunk = (v * ATT_QB + lax.broadcasted_iota(jnp.int32, (ATT_QB, ATT_WIN), 0)) // CHUNK
    kchunk = lax.broadcasted_iota(jnp.int32, (ATT_QB, ATT_WIN), 1) // CHUNK
    return (kchunk <= qchunk) & (kchunk >= qchunk - N_PREV_CHUNKS)


def _att_fwd(cols3, bias3, *, name, exchange=None):
    _, t, bw = cols3.shape
    assert t % ATT_QB == 0 and t >= ATT_WIN
    scale = DH_ATT ** -0.5
    nvar = ATT_LOOKBACK // ATT_QB

    def body(q_ref, k_ref, v_ref, b_ref, y_ref, lse_ref):
        i = pl.program_id(1)
        ws = _att_window(i)
        q = q_ref[...].astype(F32)
        kw = k_ref[pl.ds(ws, ATT_WIN), :]
        vw = v_ref[pl.ds(ws, ATT_WIN), :]
        lane_head = lax.broadcasted_iota(jnp.int32, (ATT_QB, 128), 1) // DH_ATT
        out = jnp.zeros((ATT_QB, 128), F32)
        lse = jnp.zeros((ATT_QB, 128), F32)
        for hh in range(2):
            mine = lane_head == hh
            s = _dot_nt(_bf(jnp.where(mine, q, 0.0)), kw) * scale + b_ref[hh]
            mx = jnp.max(s, axis=-1, keepdims=True)
            p = jnp.exp(s - mx)
            l = jnp.sum(p, axis=-1, keepdims=True)
            out = jnp.where(mine, _dot(_bf(p), vw) / l, out)
            lse = jnp.where(mine, mx + jnp.log(l), lse)
        y_ref[...] = _bf(out)
        lse_ref[...] = lse

    kv = lambda s: pl.BlockSpec((None, t, 128), lambda hp, i: (s, 0, hp))
    return _pcall(
        body, (cols3, cols3, cols3, bias3), name=name, grid=(H_ATT // 2, t // ATT_QB),
        in_specs=[pl.BlockSpec((None, ATT_QB, 128), lambda hp, i: (7, i, hp)), kv(8), kv(9),
                  pl.BlockSpec((None, 2, ATT_QB, ATT_WIN), lambda hp, i: (jnp.minimum(i, nvar), hp, 0, 0))],
        out_specs=[pl.BlockSpec((ATT_QB, 128), lambda hp, i: (i, hp)),
                   pl.BlockSpec((None, ATT_QB, 128), lambda hp, i: (hp, i, 0))],
        out_shape=[jax.ShapeDtypeStruct((t, bw), BF16), jax.ShapeDtypeStruct((H_ATT // 2, t, 128), F32)],
        scratch_shapes=[], sem=("parallel", "arbitrary"), exchange=exchange)


def _att_bwd(cols3, bias3, y, lse, dy, *, name, exchange=None):
    _, t, bw = cols3.shape
    nq = t // ATT_QB
    scale = DH_ATT ** -0.5
    nvar = ATT_LOOKBACK // ATT_QB

    def body(q_ref, k_ref, v_ref, b_ref, y_ref, lse_ref, dy_ref, d3_ref, db_ref, dk_acc, dv_acc):
        i = pl.program_id(1)

        @pl.when(i == 0)
        def _():
            dk_acc[...] = jnp.zeros_like(dk_acc)
            dv_acc[...] = jnp.zeros_like(dv_acc)

        ws = _att_window(i)
        q = q_ref[...].astype(F32)
        kw = k_ref[pl.ds(ws, ATT_WIN), :]
        vw = v_ref[pl.ds(ws, ATT_WIN), :]
        do = dy_ref[...].astype(F32)
        dof = do * y_ref[...].astype(F32)
        lsev = lse_ref[...]
        lane_head = lax.broadcasted_iota(jnp.int32, (ATT_QB, 128), 1) // DH_ATT
        dq = jnp.zeros((ATT_QB, 128), F32)
        dk = jnp.zeros((ATT_WIN, 128), F32)
        dv = jnp.zeros((ATT_WIN, 128), F32)
        first = i <= nvar
        for hh in range(2):
            mine = lane_head == hh
            qh = _bf(jnp.where(mine, q, 0.0))
            doh = _bf(jnp.where(mine, do, 0.0))
            s = _dot_nt(qh, kw) * scale + b_ref[hh]
            lse_h = jnp.max(jnp.where(mine, lsev, NEG_INF), axis=-1, keepdims=True)
            p = jnp.exp(s - lse_h)
            delta = jnp.sum(jnp.where(mine, dof, 0.0), axis=-1, keepdims=True)
            ds = p * (_dot_nt(doh, vw) - delta)

            @pl.when(first)
            def _():
                db_ref[hh] = ds

            @pl.when(jnp.logical_not(first))
            def _():
                db_ref[hh] += ds

            dsb = _bf(ds * scale)
            dq = jnp.where(mine, _dot(dsb, kw), dq)
            dk = dk + _dot_tn(dsb, qh)
            dv = dv + _dot_tn(_bf(p), doh)
        d3_ref[0, pl.ds(pl.multiple_of(i * ATT_QB, ATT_QB), ATT_QB), :] = _bf(dq)
        dk_acc[pl.ds(ws, ATT_WIN), :] += dk
        dv_acc[pl.ds(ws, ATT_WIN), :] += dv

        @pl.when(i == nq - 1)
        def _():
            d3_ref[1] = _bf(dk_acc[...])
            d3_ref[2] = _bf(dv_acc[...])

    kv = lambda s: pl.BlockSpec((None, t, 128), lambda hp, i: (s, 0, hp))
    qrow = pl.BlockSpec((ATT_QB, 128), lambda hp, i: (i, hp))
    btile = pl.BlockSpec((None, 2, ATT_QB, ATT_WIN), lambda hp, i: (jnp.minimum(i, nvar), hp, 0, 0))
    return _pcall(
        body, (cols3, cols3, cols3, bias3, y, lse, dy), name=name, grid=(H_ATT // 2, nq),
        in_specs=[pl.BlockSpec((None, ATT_QB, 128), lambda hp, i: (7, i, hp)), kv(8), kv(9), btile, qrow,
                  pl.BlockSpec((None, ATT_QB, 128), lambda hp, i: (hp, i, 0)),
                  pl.BlockSpec((None, ATT_QB, 128), lambda hp, i: (2, i, hp))],
        out_specs=[pl.BlockSpec((3, t, 128), lambda hp, i: (0, 0, hp)), btile],
        out_shape=[jax.ShapeDtypeStruct((3, t, bw), BF16),
                   jax.ShapeDtypeStruct((nvar + 1, H_ATT, ATT_QB, ATT_WIN), F32)],
        scratch_shapes=[pltpu.VMEM((t, 128), F32), pltpu.VMEM((t, 128), F32)],
        sem=("parallel", "arbitrary"), exchange=exchange)


SKEW_W = ATT_WIN + ATT_QB
REL_PAD = 384


def _rel_onehot(v):
    r = lax.broadcasted_iota(jnp.int32, (REL_PAD, SKEW_W), 0)
    j = lax.broadcasted_iota(jnp.int32, (REL_PAD, SKEW_W), 1)
    dist = jnp.where(j < ATT_WIN, v * ATT_QB - j, v * ATT_QB + SKEW_W - j)
    col = jnp.clip(dist, -REL_CLIP, REL_CLIP) + REL_CLIP
    return _bf(jnp.where(col == r, 1.0, 0.0))


def _split3(v):
    hi = _bf(v)
    rest = v - hi.astype(F32)
    mid = _bf(rest)
    return hi, mid, _bf(rest - mid.astype(F32))


def _skew8(a, forward):
    row = lax.broadcasted_iota(jnp.int32, a.shape, 0)
    for b in range(3):
        shift = (1 << b) if forward else SKEW_W - (1 << b)
        a = jnp.where(((row >> b) & 1) == 1, pltpu.roll(a, shift, axis=1), a)
    return a


def _toeplitz_rows(ext_row):
    a = _skew8(jnp.broadcast_to(ext_row, (8, SKEW_W)), True)
    while a.shape[0] < ATT_QB:
        a = jnp.concatenate([a, pltpu.roll(a, a.shape[0], axis=1)], axis=0)
    return a


def _diagonal_sums(tile):
    a = tile
    while a.shape[0] > 8:
        half = a.shape[0] // 2
        a = a[:half] + pltpu.roll(a[half:], SKEW_W - half, axis=1)
    return jnp.sum(_skew8(a, False), axis=0, keepdims=True)


def _bias_tiles(rel_bias, *, name, exchange=None):
    depth = rel_bias.shape[0]
    nvar = ATT_LOOKBACK // ATT_QB + 1
    rel = jnp.pad(rel_bias, ((0, 0), (0, 0), (0, REL_PAD - N_REL)))

    def body(rel_ref, o_ref, ext_ref):
        v = pl.program_id(1)
        onehot = _rel_onehot(v)
        ext_ref[...] = sum(_dot(part, onehot) for part in _split3(rel_ref[...]))
        valid = _att_mask(v)
        for h in range(H_ATT):
            o_ref[h] = jnp.where(valid, _toeplitz_rows(ext_ref[h:h + 1, :])[:, :ATT_WIN], NEG_INF)

    outs, moved = _pcall(
        body, (rel,), name=name, grid=(depth, nvar),
        in_specs=[pl.BlockSpec((None, H_ATT, REL_PAD), lambda l, v: (l, 0, 0))],
        out_specs=[pl.BlockSpec((None, None, H_ATT, ATT_QB, ATT_WIN), lambda l, v: (l, v, 0, 0, 0))],
        out_shape=[jax.ShapeDtypeStruct((depth, nvar, H_ATT, ATT_QB, ATT_WIN), F32)],
        scratch_shapes=[pltpu.VMEM((H_ATT, SKEW_W), F32)], sem=("parallel", "parallel"), exchange=exchange)
    return outs[0], moved


def _rel_bias_grad(dbias3, *, name):
    nvar = dbias3.shape[0]

    def body(db_ref, o_ref, diag_ref):
        v = pl.program_id(0)
        for h in range(H_ATT):
            tile = jnp.concatenate([db_ref[h], jnp.zeros((ATT_QB, ATT_QB), F32)], axis=1)
            diag_ref[h:h + 1, :] = _diagonal_sums(tile)
        onehot = _rel_onehot(v)
        part = sum(_dot_nt(p, onehot) for p in _split3(diag_ref[...]))

        @pl.when(v == 0)
        def _():
            o_ref[...] = part

        @pl.when(v > 0)
        def _():
            o_ref[...] += part

    out = pl.pallas_call(
        body, name=name, grid=(nvar,),
        in_specs=[pl.BlockSpec((None, H_ATT, ATT_QB, ATT_WIN), lambda v: (v, 0, 0, 0))],
        out_specs=pl.BlockSpec((H_ATT, REL_PAD), lambda v: (0, 0)),
        out_shape=jax.ShapeDtypeStruct((H_ATT, REL_PAD), F32),
        scratch_shapes=[pltpu.VMEM((H_ATT, SKEW_W), F32)], compiler_params=_cparams("arbitrary"),
    )(dbias3)
    return out[:, :N_REL]


def _merge_fwd(h, ys, wmg, wb_t, *, name, tm=512):
    t, d = h.shape
    bw = ys[0].shape[1]
    tm = _tile(t, tm)

    def body(h_ref, y0_ref, y1_ref, y2_ref, wg_ref, wb_ref, m_ref, s_ref, p_ref):
        hv = h_ref[...]
        total = jnp.zeros((tm, d), F32)
        for b, y_ref in enumerate((y0_ref, y1_ref, y2_ref)):
            s = _sigmoid(_dot(hv, wg_ref[b]))
            p = _dot_nt(y_ref[...], wb_ref[b])
            s_ref[b] = _bf(s)
            p_ref[b] = _bf(p)
            total = total + s * p
        m_ref[...] = _bf(total)

    row = pl.BlockSpec((tm, d), lambda i: (i, 0))
    yrow = pl.BlockSpec((tm, bw), lambda i: (i, 0))
    three = pl.BlockSpec((3, tm, d), lambda i: (0, i, 0))
    return pl.pallas_call(
        body, name=name, grid=(t // tm,),
        in_specs=[row, yrow, yrow, yrow, pl.BlockSpec((3, d, d), lambda i: (0, 0, 0)),
                  pl.BlockSpec((3, d, bw), lambda i: (0, 0, 0))],
        out_specs=[row, three, three],
        out_shape=[jax.ShapeDtypeStruct((t, d), BF16), jax.ShapeDtypeStruct((3, t, d), BF16),
                   jax.ShapeDtypeStruct((3, t, d), BF16)],
        compiler_params=_cparams("parallel"),
    )(h, *ys, wmg, wb_t)


def _merge_bwd(dm, s3, p3, wmg, wb_t, *, name, tm=512):
    _, t, d = s3.shape
    bw = wb_t.shape[2]
    tm = _tile(t, tm)

    def body(dm_ref, s_ref, p_ref, wg_ref, wb_ref, dgp_ref, dp_ref, dy_ref, dh_ref):
        dmv = dm_ref[...].astype(F32)
        dh = jnp.zeros((tm, d), F32)
        for b in range(3):
            s = s_ref[b].astype(F32)
            dgp = _bf(dmv * p_ref[b].astype(F32) * s * (1.0 - s))
            dp = _bf(dmv * s)
            dgp_ref[b] = dgp
            dp_ref[b] = dp
            dy_ref[b] = _bf(_dot(dp, wb_ref[b]))
            dh = dh + _dot_nt(dgp, wg_ref[b])
        dh_ref[...] = dh

    row = pl.BlockSpec((tm, d), lambda i: (i, 0))
    three = pl.BlockSpec((3, tm, d), lambda i: (0, i, 0))
    return pl.pallas_call(
        body, name=name, grid=(t // tm,),
        in_specs=[row, three, three, pl.BlockSpec((3, d, d), lambda i: (0, 0, 0)),
                  pl.BlockSpec((3, d, bw), lambda i: (0, 0, 0))],
        out_specs=[three, three, pl.BlockSpec((3, tm, bw), lambda i: (0, i, 0)), row],
        out_shape=[jax.ShapeDtypeStruct((3, t, d), BF16), jax.ShapeDtypeStruct((3, t, d), BF16),
                   jax.ShapeDtypeStruct((3, t, bw), BF16), jax.ShapeDtypeStruct((t, d), F32)],
        compiler_params=_cparams("parallel"),
    )(dm, s3, p3, wmg, wb_t)


def _carried(hooks, slot, state=None):
    if not hooks or slot not in hooks:
        return None, lambda buf: None
    make, done = hooks[slot]
    return make(state), done


def _layer_fwd(x, p, aux, tag, hooks=None):
    ex, done = _carried(hooks, "ffn1")
    (x1, h1, g1, u1, a1), buf = _ffn_fwd(x, p["n1"], p["wg1"], p["wu1"], p["wd1"], name=f"ffn1_fwd_{tag}", exchange=ex)
    done(buf)
    h2 = _rmsnorm_fwd(x1, p["nmix"], name=f"mixnorm_fwd_{tag}")
    ex, done = _carried(hooks, "inproj")
    if ex is None:
        cols3 = _mm_to_slices(h2, p["win"], name=f"inproj_fwd_{tag}")
    else:
        cols3, buf = _mm_to_slices(h2, p["win"], name=f"inproj_fwd_{tag}", exchange=ex)
        done(buf)
    bias3 = p["bias3"]
    y_conv = _conv_fwd(cols3, p["conv_w"], name=f"conv_fwd_{tag}")
    y_ret, states = _ret_fwd(cols3, aux["rope"], aux["ret"], name=f"ret_fwd_{tag}")
    ex, done = _carried(hooks, "att")
    (y_att, lse), buf = _att_fwd(cols3, bias3, name=f"att_fwd_{tag}", exchange=ex)
    done(buf)
    merged, s3, p3 = _merge_fwd(h2, (y_conv, y_ret, y_att), p["wmg"], p["wb"], name=f"merge_fwd_{tag}")
    x2 = _mm_rows(merged, p["wout"], dims=NN, out_dtype=F32, res=x1, name=f"outproj_fwd_{tag}")
    ex, done = _carried(hooks, "ffn2")
    (x3, h3, g2, u2, a2), buf = _ffn_fwd(x2, p["n2"], p["wg2"], p["wu2"], p["wd2"], name=f"ffn2_fwd_{tag}", exchange=ex)
    done(buf)
    saved = dict(x0=x, h1=h1, g1=g1, u1=u1, a1=a1, x1=x1, h2=h2, cols3=cols3, bias3=bias3, y_conv=y_conv,
                 y_ret=y_ret, states=states, y_att=y_att, lse=lse, merged=merged, s3=s3, p3=p3, x2=x2, h3=h3,
                 g2=g2, u2=u2, a2=a2)
    return x3, saved


def _ffn_grads(dxo, x, h, g, u, a, p, which, tag, grads, all_grads, hooks):
    n = which
    ex, done = _carried(hooks, f"ffn{n}_bwd", all_grads)
    (dx, dyb, dg, du, grads["n" + n]), buf = _ffn_bwd(dxo, x, p["n" + n], g, u, p["wg" + n], p["wu" + n], p["wd" + n],
                                                     name=f"ffn{n}_{tag}_bwd", exchange=ex)
    done(buf)
    grads["wd" + n] = _mm_tn(a, dyb, name=f"ffn{n}_{tag}_dwd")
    for key, lhs in (("wg" + n, dg), ("wu" + n, du)):
        ex, done = _carried(hooks, f"ffn{n}_d{key[:2]}", all_grads)
        if ex is None:
            grads[key] = _mm_tn(lhs, h, name=f"ffn{n}_{tag}_d{key[:2]}")
        else:
            grads[key], buf = _mm_tn(lhs, h, name=f"ffn{n}_{tag}_d{key[:2]}", exchange=ex)
            done(buf)
    return dx


def _layer_bwd(dx3, p, s, aux, tag, grads, all_grads, hooks=None):
    dx2 = _ffn_grads(dx3, s["x2"], s["h3"], s["g2"], s["u2"], s["a2"], p, "2", tag, grads, all_grads, hooks)

    dm = _mm_rows(dx2, p["wout"], dims=NT, out_dtype=BF16, name=f"outproj_dm_{tag}")
    grads["wout"] = _mm_tn(s["merged"], dx2, name=f"outproj_dw_{tag}")
    dgp3, dp3, dy3, dh2 = _merge_bwd(dm, s["s3"], s["p3"], p["wmg"], p["wb"], name=f"merge_bwd_{tag}")
    grads["wmg"] = _mm_tn_batch(s["h2"], dgp3, a_batched=False, name=f"merge_dwg_{tag}")
    y3 = jnp.stack([s["y_conv"], s["y_ret"], s["y_att"]])
    grads["wb"] = _mm_tn_batch(dp3, y3, a_batched=True, name=f"merge_dwb_{tag}")

    dconv3, grads["conv_w"] = _conv_bwd(s["cols3"], p["conv_w"], dy3, name=f"conv_bwd_{tag}")
    ex, done = _carried(hooks, "ret_bwd", all_grads)
    dret4, buf = _ret_bwd(s["cols3"], aux["rope"], aux["ret"], s["states"], dy3, name=f"ret_bwd_{tag}", exchange=ex)
    done(buf)
    ex, done = _carried(hooks, "att_bwd", all_grads)
    (datt3, dbias3), buf = _att_bwd(s["cols3"], s["bias3"], s["y_att"], s["lse"], dy3, name=f"att_bwd_{tag}",
                                    exchange=ex)
    done(buf)
    grads["rel_bias"] = _rel_bias_grad(dbias3, name=f"bias_grad_{tag}")

    dh2 = _inproj_dh((dconv3, dret4, datt3), p["win"], dh2, name=f"inproj_dh_{tag}")
    grads["win"] = jnp.concatenate([
        _mm_tn_slices(dconv3, s["h2"], name=f"inproj_dw_conv_{tag}"),
        _mm_tn_slices(dret4, s["h2"], name=f"inproj_dw_ret_{tag}"),
        _mm_tn_slices(datt3, s["h2"], name=f"inproj_dw_att_{tag}")], axis=0)
    dx1, grads["nmix"] = _rmsnorm_bwd(dh2, s["x1"], p["nmix"], dx2, name=f"mixnorm_bwd_{tag}")

    return _ffn_grads(dx1, s["x0"], s["h1"], s["g1"], s["u1"], s["a1"], p, "1", tag, grads, all_grads, hooks)


def _device_step(x, target, layers, final_norm, fwd_hooks=None, bwd_hooks=None):
    t = x.shape[0]
    depth = len(layers)
    aux = dict(rope=_rope_tables(t), ret=_ret_consts())
    ex, done = _carried(fwd_hooks[0] if fwd_hooks else None, "start")
    bias_all, buf = _bias_tiles(jnp.stack([p["rel_bias"] for p in layers]), name="bias_tiles", exchange=ex)
    done(buf)
    saved = []
    for l, p in enumerate(layers):
        p["bias3"] = bias_all[l]
        x, s = _layer_fwd(x, p, aux, f"l{l}", fwd_hooks[l] if fwd_hooks else None)
        saved.append(s)
    dx, dfinal, loss_row = _loss_fwd_bwd(x, final_norm, target, name="loss_fwd_bwd")
    grads = [dict() for _ in range(depth)]
    for l in reversed(range(depth)):
        dx = _layer_bwd(dx, layers[l], saved[l], aux, f"l{l}", grads[l], grads, bwd_hooks[l] if bwd_hooks else None)
    return loss_row, dx, grads, dfinal


def _sum_slots(bufs, *, name, tr=512):
    n, r, cdim = bufs[0].shape
    depth = len(bufs)
    tr = _tile(r, tr)
    nt = r // tr

    def body(*refs):
        o_ref = refs[depth]
        for k in range(depth):
            @pl.when(pl.program_id(0) == k)
            def _(p_ref=refs[k]):
                acc = p_ref[0].astype(F32)
                for s in range(1, n):
                    acc = acc + p_ref[s].astype(F32)
                o_ref[...] = acc

    def spec(k):
        return pl.BlockSpec((n, tr, cdim), lambda l, i: (0, jnp.where(l == k, i, jnp.where(l < k, 0, nt - 1)), 0))

    return pl.pallas_call(
        body, name=name, grid=(depth, nt), in_specs=[spec(k) for k in range(depth)],
        out_specs=pl.BlockSpec((None, tr, cdim), lambda l, i: (l, i, 0)),
        out_shape=jax.ShapeDtypeStruct((depth, r, cdim), F32), compiler_params=_cparams("arbitrary", "arbitrary"),
    )(*bufs)


def _all_reduce_small(v, *, name):
    r = v.shape[0]

    def body(x_ref, o_ref, slots, send_sems, recv_sems):
        x, y, c, me = _my_position()
        slots[me] = x_ref[...]
        sends = []
        for k in range(1, N_DEV):
            peer, _ = _peer(x, y, c, k)
            cp = pltpu.make_async_remote_copy(src_ref=x_ref, dst_ref=slots.at[me], send_sem=send_sems.at[k - 1],
                                              recv_sem=recv_sems.at[k - 1], device_id=peer, device_id_type=MESH)
            cp.start()
            sends.append(cp)
        for k in range(1, N_DEV):
            peer, peer_id = _peer(x, y, c, k)
            pltpu.make_async_remote_copy(src_ref=x_ref, dst_ref=slots.at[peer_id], send_sem=send_sems.at[k - 1],
                                         recv_sem=recv_sems.at[k - 1], device_id=peer, device_id_type=MESH).wait_recv()
        for cp in sends:
            cp.wait_send()
        acc = slots[0]
        for s in range(1, N_DEV):
            acc = acc + slots[s]
        o_ref[...] = acc

    return pl.pallas_call(
        body, name=name, in_specs=[pl.BlockSpec(memory_space=pltpu.VMEM)],
        out_specs=pl.BlockSpec(memory_space=pltpu.VMEM), out_shape=jax.ShapeDtypeStruct((r, 128), F32),
        scratch_shapes=[pltpu.VMEM((N_DEV, r, 128), F32), pltpu.SemaphoreType.DMA((N_DEV - 1,)),
                        pltpu.SemaphoreType.DMA((N_DEV - 1,))],
    )(v)


def _adamw(w, g, m, v, *, name, tr=256):
    shape = w.shape
    cdim = shape[-1]
    w2, g2, m2, v2 = (a.reshape(-1, cdim) for a in (w, g, m, v))
    r = w2.shape[0]
    tr = _tile(r, tr) if r % 8 == 0 else r
    c1 = 1.0 - ADAM_B1 ** ADAM_STEP
    c2 = 1.0 - ADAM_B2 ** ADAM_STEP

    def body(w_ref, g_ref, m_ref, v_ref, d_ref, mo_ref, vo_ref):
        gv = g_ref[...]
        mn = ADAM_B1 * m_ref[...] + (1.0 - ADAM_B1) * gv
        vn = ADAM_B2 * v_ref[...] + (1.0 - ADAM_B2) * (gv * gv)
        d_ref[...] = -ADAM_LR * ((mn / c1) / (jnp.sqrt(vn / c2) + ADAM_EPS) + ADAM_WD * w_ref[...])
        mo_ref[...] = mn
        vo_ref[...] = vn

    spec = pl.BlockSpec((tr, cdim), lambda i: (i, 0))
    outs = pl.pallas_call(
        body, name=name, grid=(r // tr,), in_specs=[spec] * 4, out_specs=[spec] * 3,
        out_shape=[jax.ShapeDtypeStruct((r, cdim), F32)] * 3, compiler_params=_cparams("parallel"),
    )(w2, g2, m2, v2)
    return tuple(o.reshape(shape) for o in outs)


GROUPS = {"f1": ("wg1", "wu1", "wd1"), "in": ("win",), "mg": ("wb", "wmg", "wout"), "f2": ("wg2", "wu2", "wd2")}
BIG = tuple(nm for members in GROUPS.values() for nm in members)
ROWS_DOMAIN = ("ffn1_w_gate", "ffn1_w_up", "ffn2_w_gate", "ffn2_w_up")


def _to_rows(name, w, d):
    depth = w.shape[0]
    if name in ("wg1", "wu1", "wg2", "wu2", "win"):
        return w.transpose(0, 2, 1)
    if name == "wb":
        return w.transpose(0, 1, 3, 2).reshape(depth, -1, d)
    if name == "wmg":
        return w.reshape(depth, -1, d)
    return w


def _from_rows(name, rows, d):
    depth = rows.shape[0]
    if name in ("wg1", "wu1", "wg2", "wu2", "win"):
        return rows.transpose(0, 2, 1)
    if name == "wb":
        return rows.reshape(depth, 3, -1, BRANCH_W).transpose(0, 1, 3, 2)
    if name == "wmg":
        return rows.reshape(depth, 3, -1, d)
    return rows


def _full_from_gathered(name, g, d):
    if name == "wb":
        return g.reshape(N_DEV, 3, -1, BRANCH_W).transpose(1, 0, 2, 3).reshape(3, d, BRANCH_W)
    if name == "wmg":
        return g.reshape(N_DEV, 3, -1, d).transpose(1, 0, 2, 3).reshape(3, d, d)
    return g.reshape(-1, d)


def _gathered_from_full(name, full, d):
    if name == "wb":
        return full.reshape(3, N_DEV, -1, BRANCH_W).transpose(1, 0, 2, 3).reshape(N_DEV, -1, d)
    if name == "wmg":
        return full.reshape(3, N_DEV, -1, d).transpose(1, 0, 2, 3).reshape(N_DEV, -1, d)
    return full.reshape(N_DEV, -1, d)


def kernel(x, ffn1_norm, ffn1_w_gate, ffn1_w_up, ffn1_w_down, mix_norm, w_in, conv_w, rel_bias, w_branch, w_merge_gate, w_out, ffn2_norm, ffn2_w_gate, ffn2_w_up, ffn2_w_down, final_norm, loss_target, m_ffn1_norm, m_ffn1_w_gate, m_ffn1_w_up, m_ffn1_w_down, m_mix_norm, m_w_in, m_conv_w, m_rel_bias, m_w_branch, m_w_merge_gate, m_w_out, m_ffn2_norm, m_ffn2_w_gate, m_ffn2_w_up, m_ffn2_w_down, m_final_norm, v_ffn1_norm, v_ffn1_w_gate, v_ffn1_w_up, v_ffn1_w_down, v_mix_norm, v_w_in, v_conv_w, v_rel_bias, v_w_branch, v_w_merge_gate, v_w_out, v_ffn2_norm, v_ffn2_w_gate, v_ffn2_w_up, v_ffn2_w_down, v_final_norm):
    names = ["ffn1_norm", "ffn1_w_gate", "ffn1_w_up", "ffn1_w_down", "mix_norm", "w_in", "conv_w", "rel_bias",
             "w_branch", "w_merge_gate", "w_out", "ffn2_norm", "ffn2_w_gate", "ffn2_w_up", "ffn2_w_down", "final_norm"]
    weights = dict(zip(names, (ffn1_norm, ffn1_w_gate, ffn1_w_up, ffn1_w_down, mix_norm, w_in, conv_w, rel_bias,
                               w_branch, w_merge_gate, w_out, ffn2_norm, ffn2_w_gate, ffn2_w_up, ffn2_w_down,
                               final_norm)))
    m_in = dict(zip(names, (m_ffn1_norm, m_ffn1_w_gate, m_ffn1_w_up, m_ffn1_w_down, m_mix_norm, m_w_in, m_conv_w,
                            m_rel_bias, m_w_branch, m_w_merge_gate, m_w_out, m_ffn2_norm, m_ffn2_w_gate,
                            m_ffn2_w_up, m_ffn2_w_down, m_final_norm)))
    v_in = dict(zip(names, (v_ffn1_norm, v_ffn1_w_gate, v_ffn1_w_up, v_ffn1_w_down, v_mix_norm, v_w_in, v_conv_w,
                            v_rel_bias, v_w_branch, v_w_merge_gate, v_w_out, v_ffn2_norm, v_ffn2_w_gate,
                            v_ffn2_w_up, v_ffn2_w_down, v_final_norm)))
    big_of = dict(wg1="ffn1_w_gate", wu1="ffn1_w_up", wd1="ffn1_w_down", win="w_in", wb="w_branch",
                  wmg="w_merge_gate", wout="w_out", wg2="ffn2_w_gate", wu2="ffn2_w_up", wd2="ffn2_w_down")
    depth = ffn1_norm.shape[0]
    d = x.shape[-1]
    xs = x.reshape(-1, d)
    target = loss_target.reshape(-1, d)
    _, _, _, me = _my_position()

    shard_rows = {nm: _to_rows(nm, weights[big_of[nm]], d).astype(BF16) for nm in BIG}

    conv_cols = conv_w.shape[-1]
    conv_full, = _exchange("gather", [conv_w.reshape(depth * 3, conv_cols)], name="gather_conv_w")
    conv_full = conv_full.reshape(N_DEV, depth, 3, conv_cols).transpose(1, 2, 0, 3).reshape(depth, 3, -1)
    layers = [dict(n1=ffn1_norm[l][None], nmix=mix_norm[l][None], n2=ffn2_norm[l][None], conv_w=conv_full[l],
                   rel_bias=rel_bias[l]) for l in range(depth)]

    def gather_hook(l, grp):
        members = GROUPS[grp]

        def done(bufs):
            layers[l].update({nm: _full_from_gathered(nm, buf, d) for nm, buf in zip(members, bufs)})

        return (lambda _: ("gather", [shard_rows[nm] for nm in members], [l] * len(members))), done

    fwd_hooks = []
    for l in range(depth):
        hooks = {"ffn1": gather_hook(l, "in"), "inproj": gather_hook(l, "mg"), "att": gather_hook(l, "f2")}
        if l == 0:
            hooks["start"] = gather_hook(0, "f1")
        if l + 1 < depth:
            hooks["ffn2"] = gather_hook(l + 1, "f1")
        fwd_hooks.append(hooks)

    received = {}

    def scatter_hook(l, members):
        def make(all_grads):
            return "scatter", [_gathered_from_full(nm, all_grads[l][nm], d) for nm in members], [None] * len(members)

        return make, (lambda bufs: received.update({(l, nm): buf for nm, buf in zip(members, bufs)}))

    bwd_hooks = []
    for l in range(depth):
        hooks = {"att_bwd": scatter_hook(l, GROUPS["f2"]), "ret_bwd": scatter_hook(l, GROUPS["mg"]),
                 "ffn1_bwd": scatter_hook(l, GROUPS["in"])}
        if l + 1 < depth:
            hooks["ffn2_bwd"] = scatter_hook(l + 1, GROUPS["f1"])
        if l == 0:
            hooks["ffn1_dwg"] = scatter_hook(0, ("wd1",))
            hooks["ffn1_dwu"] = scatter_hook(0, ("wg1",))
        bwd_hooks.append(hooks)

    loss_row, dx, grads, dfinal = _device_step(xs, target, layers, final_norm[None], fwd_hooks, bwd_hooks)
    received[(0, "wu1")], = _exchange("scatter", [_gathered_from_full("wu1", grads[0]["wu1"], d)], name="scatter_l0_wu1")

    grad_rows = {big_of[nm]: _sum_slots([received[(l, nm)] for l in range(depth)], name=f"sum_grads_{nm}")
                 for nm in BIG}
    grad_w = {big_of[nm]: _from_rows(nm, grad_rows[big_of[nm]], d) for nm in BIG}

    small = {"ffn1_norm": jnp.stack([grads[l]["n1"][0] for l in range(depth)]),
             "mix_norm": jnp.stack([grads[l]["nmix"][0] for l in range(depth)]),
             "ffn2_norm": jnp.stack([grads[l]["n2"][0] for l in range(depth)]),
             "final_norm": dfinal[0],
             "rel_bias": jnp.stack([grads[l]["rel_bias"] for l in range(depth)]),
             "conv_w": jnp.stack([grads[l]["conv_w"] for l in range(depth)]),
             "loss": loss_row[0, :1]}
    order = list(small)
    flat = jnp.concatenate([small[k].reshape(-1) for k in order])
    pad = (-flat.shape[0]) % 1024
    summed = _all_reduce_small(jnp.pad(flat, (0, pad)).reshape(-1, 128), name="reduce_small").reshape(-1)
    pos = 0
    for k in order:
        n = small[k].size
        small[k] = summed[pos:pos + n].reshape(small[k].shape)
        pos += n
    small["conv_w"] = lax.dynamic_slice_in_dim(small["conv_w"], me * conv_cols, conv_cols, axis=2)

    grad_w.update({k: small[k] for k in order if k != "loss"})
    delta, new_m, new_v = {}, {}, {}
    for nm in names:
        if nm in ROWS_DOMAIN:
            flip = lambda a: a.transpose(0, 2, 1)
            outs = _adamw(flip(weights[nm]), grad_rows[nm], flip(m_in[nm]), flip(v_in[nm]), name=f"adamw_{nm}")
            delta[nm], new_m[nm], new_v[nm] = (flip(o) for o in outs)
        else:
            delta[nm], new_m[nm], new_v[nm] = _adamw(weights[nm], grad_w[nm], m_in[nm], v_in[nm], name=f"adamw_{nm}")
    return (small["loss"].reshape(()), dx.reshape(x.shape), *[grad_w[n] for n in names], *[delta[n] for n in names],
            *[new_m[n] for n in names], *[new_v[n] for n in names])
```

```python
import functools
import math

import jax
import jax.numpy as jnp
from jax import lax
from jax.experimental import pallas as pl
from jax.experimental.pallas import tpu as pltpu

F32 = jnp.float32
BF16 = jnp.bfloat16

N_DEV = 8
EPS = 1e-6
CHUNK = 64
BRANCH_W = 512
N_SLICES = 10
H_RET = 4
DK_RET = 128
H_ATT = 8
DH_ATT = 64
N_PREV_CHUNKS = 8
REL_CLIP = 128
N_REL = 2 * REL_CLIP + 1
NEG_INF = -1e30
ROPE_BASE = 10000.0
ATT_QB = 256
ATT_LOOKBACK = N_PREV_CHUNKS * CHUNK
ATT_WIN = ATT_LOOKBACK + ATT_QB
RET_TB = 256
TOKEN_TK = 2048
PARTIAL_DTYPE = jnp.bfloat16
CONV_HALO = 16

ADAM_LR = 0.001
ADAM_B1 = 0.9
ADAM_B2 = 0.999
ADAM_EPS = 1e-08
ADAM_WD = 0.01
ADAM_STEP = 10

VMEM_LIMIT_BYTES = 56 * 1024 * 1024
MESH = pl.DeviceIdType.MESH


def _cparams(*sem):
    return pltpu.CompilerParams(dimension_semantics=sem, vmem_limit_bytes=VMEM_LIMIT_BYTES)


def _dot(a, b):
    return lax.dot_general(a, b, (((1,), (0,)), ((), ())), preferred_element_type=F32)


def _dot_nt(a, b):
    return lax.dot_general(a, b, (((1,), (1,)), ((), ())), preferred_element_type=F32)


def _dot_tn(a, b):
    return lax.dot_general(a, b, (((0,), (0,)), ((), ())), preferred_element_type=F32)


def _bf(v):
    return v.astype(BF16)


def _sigmoid(v):
    return 1.0 / (1.0 + jnp.exp(-v))


def _tile(n, want):
    if n <= want:
        return n
    for t in range(want - want % 128, 0, -128):
        if n % t == 0:
            return t
    t = want
    while n % t:
        t //= 2
    return t


def _my_position():
    x, y, c = lax.axis_index("x"), lax.axis_index("y"), lax.axis_index("c")
    return x, y, c, 4 * x + 2 * y + c


def _peer(x, y, c, k):
    px = 1 - x if k & 4 else x
    py = 1 - y if k & 2 else y
    pc = 1 - c if k & 1 else c
    return (px, py, pc), 4 * px + 2 * py + pc


DIRECT2 = (1, 2, 4, 6)
PASSED_ON = (2, 4, 6)
N_PUSHES = N_DEV - 1 + len(PASSED_ON)


def _exchange_sems(n):
    return [pltpu.SemaphoreType.DMA((n, N_PUSHES)), pltpu.SemaphoreType.DMA((n, N_PUSHES)),
            pltpu.SemaphoreType.DMA((n,))]


def _exchange_src(kind, src_ref, layer, block):
    if kind in ("gather", "gather2"):
        return src_ref if layer is None else src_ref.at[layer]
    return src_ref.at[block]


def _passed_on(dst_ref, sems, a, j, pos, incoming):
    x, y, c, _ = pos
    sibling, _ = _peer(x, y, c, 1)
    _, block = _peer(x, y, c, PASSED_ON[j] | 1 if incoming else PASSED_ON[j])
    return pltpu.make_async_remote_copy(src_ref=dst_ref.at[block], dst_ref=dst_ref.at[block],
                                        send_sem=sems[0].at[a, N_DEV - 1 + j], recv_sem=sems[1].at[a, N_DEV - 1 + j],
                                        device_id=sibling, device_id_type=MESH)


def _exchange_copy(kind, src_ref, layer, dst_ref, sems, a, k, pos, incoming):
    x, y, c, me = pos
    peer, peer_id = _peer(x, y, c, k)
    return pltpu.make_async_remote_copy(src_ref=_exchange_src(kind, src_ref, layer, me if incoming else peer_id),
                                        dst_ref=dst_ref.at[peer_id if incoming else me],
                                        send_sem=sems[0].at[a, k - 1], recv_sem=sems[1].at[a, k - 1],
                                        device_id=peer, device_id_type=MESH)


def _exchange_local(kind, src_ref, layer, dst_ref, sems, a, me):
    return pltpu.make_async_copy(_exchange_src(kind, src_ref, layer, me), dst_ref.at[me], sems[2].at[a])


def _exchange_start(kind, layers, src_refs, dst_refs, sems):
    pos = _my_position()
    direct = DIRECT2 if kind == "gather2" else range(1, N_DEV)
    for a, (src_ref, layer, dst_ref) in enumerate(zip(src_refs, layers, dst_refs)):
        _exchange_local(kind, src_ref, layer, dst_ref, sems, a, pos[3]).start()
        for k in direct:
            _exchange_copy(kind, src_ref, layer, dst_ref, sems, a, k, pos, False).start()


def _exchange_wait(kind, layers, src_refs, dst_refs, sems):
    pos = _my_position()
    for a, (src_ref, layer, dst_ref) in enumerate(zip(src_refs, layers, dst_refs)):
        if kind == "gather2":
            for j, k in enumerate(PASSED_ON):
                _exchange_copy(kind, src_ref, layer, dst_ref, sems, a, k, pos, True).wait_recv()
                _passed_on(dst_ref, sems, a, j, pos, False).start()
            _exchange_copy(kind, src_ref, layer, dst_ref, sems, a, 1, pos, True).wait_recv()
            for j in range(len(PASSED_ON)):
                _passed_on(dst_ref, sems, a, j, pos, True).wait_recv()
            for k in DIRECT2:
                _exchange_copy(kind, src_ref, layer, dst_ref, sems, a, k, pos, False).wait_send()
            for j in range(len(PASSED_ON)):
                _passed_on(dst_ref, sems, a, j, pos, False).wait_send()
        else:
            for k in range(1, N_DEV):
                _exchange_copy(kind, src_ref, layer, dst_ref, sems, a, k, pos, True).wait_recv()
            for k in range(1, N_DEV):
                _exchange_copy(kind, src_ref, layer, dst_ref, sems, a, k, pos, False).wait_send()
        _exchange_local(kind, src_ref, layer, dst_ref, sems, a, pos[3]).wait()


def _exchange_shapes(kind, srcs, layers):
    if kind in ("gather", "gather2"):
        return [jax.ShapeDtypeStruct((N_DEV,) + (s.shape if l is None else s.shape[1:]), s.dtype)
                for s, l in zip(srcs, layers)]
    return [jax.ShapeDtypeStruct(s.shape, s.dtype) for s in srcs]


def _exchange(kind, srcs, layers=None, *, name):
    n = len(srcs)
    layers = layers or [None] * n
    hbm = pl.BlockSpec(memory_space=pl.ANY)

    def body(*refs):
        src_refs, dst_refs, sems = refs[:n], refs[n:2 * n], refs[2 * n:]
        _exchange_start(kind, layers, src_refs, dst_refs, sems)
        _exchange_wait(kind, layers, src_refs, dst_refs, sems)

    return pl.pallas_call(body, name=name, in_specs=[hbm] * n, out_specs=[hbm] * n,
                          out_shape=_exchange_shapes(kind, srcs, layers), scratch_shapes=_exchange_sems(n))(*srcs)


def _pcall(body, args, *, name, grid, in_specs, out_specs, out_shape, scratch_shapes, sem, exchange=None):
    if exchange is None:
        outs = pl.pallas_call(body, name=name, grid=grid, in_specs=in_specs, out_specs=out_specs, out_shape=out_shape,
                              scratch_shapes=scratch_shapes, compiler_params=_cparams(*sem))(*args)
        return outs, None
    kind, srcs, layers = exchange
    n_in, n_out, n_scr, n_x = len(in_specs), len(out_specs), len(scratch_shapes), len(srcs)
    hbm = pl.BlockSpec(memory_space=pl.ANY)

    def carrier(*refs):
        ins, refs = refs[:n_in], refs[n_in:]
        src_refs, refs = refs[:n_x], refs[n_x:]
        outs, refs = refs[:n_out], refs[n_out:]
        dst_refs, refs = refs[:n_x], refs[n_x:]
        scr, sems = refs[:n_scr], refs[n_scr:]
        ids = [pl.program_id(a) for a in range(len(grid))]
        first = functools.reduce(jnp.logical_and, [i == 0 for i in ids])
        last = functools.reduce(jnp.logical_and, [i == g - 1 for i, g in zip(ids, grid)])

        @pl.when(first)
        def _():
            _exchange_start(kind, layers, src_refs, dst_refs, sems)

        body(*ins, *outs, *scr)

        @pl.when(last)
        def _():
            _exchange_wait(kind, layers, src_refs, dst_refs, sems)

    outs = pl.pallas_call(
        carrier, name=name, grid=grid, in_specs=list(in_specs) + [hbm] * n_x, out_specs=list(out_specs) + [hbm] * n_x,
        out_shape=list(out_shape) + _exchange_shapes(kind, srcs, layers),
        scratch_shapes=list(scratch_shapes) + _exchange_sems(n_x),
        compiler_params=_cparams(*(["arbitrary"] * len(grid))))(*args, *srcs)
    return outs[:n_out], outs[n_out:]


def _accumulate(acc_ref, part, k, nk, finish):
    if nk == 1:
        finish(part)
        return

    @pl.when(k == 0)
    def _():
        acc_ref[...] = part

    @pl.when(jnp.logical_and(k > 0, k < nk - 1))
    def _():
        acc_ref[...] += part

    @pl.when(k == nk - 1)
    def _():
        finish(acc_ref[...] + part)


def _matmul(a, b, *, dims, grid, a_spec, b_spec, o_spec, out_shape, acc_shape, name, scale=1.0,
            res=None, res_spec=None, exchange=None):
    nk = grid[3]
    has_res = res is not None

    def body(*refs):
        if has_res:
            a_ref, b_ref, r_ref, o_ref = refs[:4]
        else:
            a_ref, b_ref, o_ref = refs[:3]
            r_ref = None

        def finish(acc):
            if scale != 1.0:
                acc = acc * scale
            if r_ref is not None:
                acc = acc + r_ref[...].astype(F32)
            o_ref[...] = acc.astype(o_ref.dtype)

        part = lax.dot_general(_bf(a_ref[...]), _bf(b_ref[...]), (dims, ((), ())), preferred_element_type=F32)
        _accumulate(None if nk == 1 else refs[-1], part, pl.program_id(3), nk, finish)

    in_specs = [a_spec, b_spec] + ([res_spec] if has_res else [])
    args = (a, b) + ((res,) if has_res else ())
    outs, moved = _pcall(
        body, args, name=name, grid=grid, in_specs=in_specs, out_specs=[o_spec], out_shape=[out_shape],
        scratch_shapes=[] if nk == 1 else [pltpu.VMEM(acc_shape, F32)],
        sem=("parallel", "parallel", "parallel", "arbitrary"), exchange=exchange)
    return outs[0] if exchange is None else (outs[0], moved)


NT = ((1,), (1,))
NN = ((1,), (0,))
TN = ((0,), (0,))


def _mm_rows(a, b, *, dims, out_dtype, name, res=None, scale=1.0, tm=1024, tn=1024, tk=1024):
    m, kdim = a.shape
    n = b.shape[1] if dims == NN else b.shape[0]
    tm, tn, tk = _tile(m, tm), _tile(n, tn), _tile(kdim, tk)
    grid = (1, m // tm, n // tn, kdim // tk)
    a_spec = pl.BlockSpec((tm, tk), lambda s, i, j, k: (i, k))
    if dims == NN:
        b_spec = pl.BlockSpec((tk, tn), lambda s, i, j, k: (k, j))
    else:
        b_spec = pl.BlockSpec((tn, tk), lambda s, i, j, k: (j, k))
    o_spec = pl.BlockSpec((tm, tn), lambda s, i, j, k: (i, j))
    return _matmul(a, b, dims=dims, grid=grid, a_spec=a_spec, b_spec=b_spec, o_spec=o_spec,
                   out_shape=jax.ShapeDtypeStruct((m, n), out_dtype), acc_shape=(tm, tn), name=name,
                   res=res, res_spec=o_spec if res is not None else None, scale=scale)


def _mm_to_slices(a, b_t, *, name, tm=2048, exchange=None):
    m, kdim = a.shape
    n = b_t.shape[0]
    tm = _tile(m, tm)
    w = BRANCH_W
    grid = (1, m // tm, n // w, 1)
    return _matmul(a, b_t, dims=NT, grid=grid,
                   a_spec=pl.BlockSpec((tm, kdim), lambda s, i, j, k: (i, 0)),
                   b_spec=pl.BlockSpec((w, kdim), lambda s, i, j, k: (j, 0)),
                   o_spec=pl.BlockSpec((None, tm, w), lambda s, i, j, k: (j, i, 0)),
                   out_shape=jax.ShapeDtypeStruct((n // w, m, w), BF16), acc_shape=(tm, w), name=name,
                   exchange=exchange)


def _inproj_dh(parts, w_t, res, *, name, tm=512):
    t, n = res.shape
    w = BRANCH_W
    tm = _tile(t, tm)
    counts = [p.shape[0] for p in parts]
    assert sum(counts) * w == w_t.shape[0]

    def body(*refs):
        w_ref, r_ref, o_ref = refs[len(parts):]
        cols = jnp.concatenate([p_ref[s] for p_ref, cnt in zip(refs, counts) for s in range(cnt)], axis=1)
        o_ref[...] = r_ref[...] + _dot(cols, w_ref[...])

    row = pl.BlockSpec((tm, n), lambda i: (i, 0))
    return pl.pallas_call(
        body, name=name, grid=(t // tm,),
        in_specs=[pl.BlockSpec((cnt, tm, w), lambda i: (0, i, 0)) for cnt in counts]
        + [pl.BlockSpec(w_t.shape, lambda i: (0, 0)), row],
        out_specs=row, out_shape=jax.ShapeDtypeStruct((t, n), F32), compiler_params=_cparams("parallel"),
    )(*parts, w_t, res)


def _mm_tn(a, b, *, name, tm=1408, tn=1024, tk=TOKEN_TK, exchange=None):
    t, m = a.shape
    n = b.shape[1]
    tm, tn, tk = _tile(m, tm), _tile(n, tn), _tile(t, tk)
    grid = (1, m // tm, n // tn, t // tk)
    return _matmul(a, b, dims=TN, grid=grid,
                   a_spec=pl.BlockSpec((tk, tm), lambda s, i, j, k: (k, i)),
                   b_spec=pl.BlockSpec((tk, tn), lambda s, i, j, k: (k, j)),
                   o_spec=pl.BlockSpec((tm, tn), lambda s, i, j, k: (i, j)),
                   out_shape=jax.ShapeDtypeStruct((m, n), PARTIAL_DTYPE), acc_shape=(tm, tn), name=name,
                   exchange=exchange)


def _mm_tn_slices(a3, b, *, name, tn=1024, tk=TOKEN_TK):
    s_n, t, w = a3.shape
    n = b.shape[1]
    tn, tk = _tile(n, tn), _tile(t, tk)
    grid = (1, s_n, n // tn, t // tk)
    return _matmul(a3, b, dims=TN, grid=grid,
                   a_spec=pl.BlockSpec((None, tk, w), lambda s, i, j, k: (i, k, 0)),
                   b_spec=pl.BlockSpec((tk, tn), lambda s, i, j, k: (k, j)),
                   o_spec=pl.BlockSpec((w, tn), lambda s, i, j, k: (i, j)),
                   out_shape=jax.ShapeDtypeStruct((s_n * w, n), PARTIAL_DTYPE), acc_shape=(w, tn), name=name)


def _mm_tn_batch(a, b3, *, name, a_batched, tm=1024, tn=1024, tk=TOKEN_TK):
    s_n, t, n = b3.shape
    m = a.shape[-1]
    tm, tn, tk = _tile(m, tm), _tile(n, tn), _tile(t, tk)
    grid = (s_n, m // tm, n // tn, t // tk)
    if a_batched:
        a_spec = pl.BlockSpec((None, tk, tm), lambda s, i, j, k: (s, k, i))
    else:
        a_spec = pl.BlockSpec((tk, tm), lambda s, i, j, k: (k, i))
    return _matmul(a, b3, dims=TN, grid=grid, a_spec=a_spec,
                   b_spec=pl.BlockSpec((None, tk, tn), lambda s, i, j, k: (s, k, j)),
                   o_spec=pl.BlockSpec((None, tm, tn), lambda s, i, j, k: (s, i, j)),
                   out_shape=jax.ShapeDtypeStruct((s_n, m, n), PARTIAL_DTYPE), acc_shape=(tm, tn), name=name)


def _norm_parts(xf):
    r = lax.rsqrt(jnp.mean(xf * xf, axis=-1, keepdims=True) + EPS)
    return xf * r, r


def _norm_bwd(dh, xhat, r, w):
    dxhat = dh * w
    dx = r * (dxhat - xhat * jnp.mean(dxhat * xhat, axis=-1, keepdims=True))
    return dx, jnp.sum(dh * xhat, axis=0, keepdims=True)


def _rmsnorm_fwd(x, w, *, name, tm=1024):
    t, d = x.shape
    tm = _tile(t, tm)

    def body(x_ref, w_ref, h_ref):
        xhat, _ = _norm_parts(x_ref[...])
        h_ref[...] = _bf(xhat * w_ref[...])

    return pl.pallas_call(
        body, name=name, grid=(t // tm,),
        in_specs=[pl.BlockSpec((tm, d), lambda i: (i, 0)), pl.BlockSpec((1, d), lambda i: (0, 0))],
        out_specs=pl.BlockSpec((tm, d), lambda i: (i, 0)),
        out_shape=jax.ShapeDtypeStruct((t, d), BF16), compiler_params=_cparams("parallel"),
    )(x, w)


def _rmsnorm_bwd(dh, x, w, dres, *, name, tm=1024):
    t, d = x.shape
    tm = _tile(t, tm)

    def body(dh_ref, x_ref, w_ref, dres_ref, dx_ref, dw_ref):
        xhat, r = _norm_parts(x_ref[...])
        dx, dw = _norm_bwd(dh_ref[...], xhat, r, w_ref[...])
        dx_ref[...] = dres_ref[...] + dx

        @pl.when(pl.program_id(0) == 0)
        def _():
            dw_ref[...] = dw

        @pl.when(pl.program_id(0) > 0)
        def _():
            dw_ref[...] += dw

    row = pl.BlockSpec((tm, d), lambda i: (i, 0))
    vec = pl.BlockSpec((1, d), lambda i: (0, 0))
    return pl.pallas_call(
        body, name=name, grid=(t // tm,), in_specs=[row, row, vec, row], out_specs=[row, vec],
        out_shape=[jax.ShapeDtypeStruct((t, d), F32), jax.ShapeDtypeStruct((1, d), F32)],
        compiler_params=_cparams("arbitrary"),
    )(dh, x, w, dres)


def _loss_fwd_bwd(x, w, target, *, name, tm=1024):
    t, d = x.shape
    tm = _tile(t, tm)

    def body(x_ref, w_ref, t_ref, dx_ref, dw_ref, loss_ref):
        xhat, r = _norm_parts(x_ref[...])
        wv = w_ref[...]
        err = xhat * wv - t_ref[...]
        dx, dw = _norm_bwd(err * (1.0 / d), xhat, r, wv)
        dx_ref[...] = dx
        part = jnp.full((1, 128), 0.5 / d, F32) * jnp.sum(err * err)

        @pl.when(pl.program_id(0) == 0)
        def _():
            dw_ref[...] = dw
            loss_ref[...] = part

        @pl.when(pl.program_id(0) > 0)
        def _():
            dw_ref[...] += dw
            loss_ref[...] += part

    row = pl.BlockSpec((tm, d), lambda i: (i, 0))
    vec = pl.BlockSpec((1, d), lambda i: (0, 0))
    return pl.pallas_call(
        body, name=name, grid=(t // tm,), in_specs=[row, vec, row],
        out_specs=[row, vec, pl.BlockSpec((1, 128), lambda i: (0, 0))],
        out_shape=[jax.ShapeDtypeStruct((t, d), F32), jax.ShapeDtypeStruct((1, d), F32),
                   jax.ShapeDtypeStruct((1, 128), F32)],
        compiler_params=_cparams("arbitrary"),
    )(x, w, target)


def _ffn_tiles(t, f):
    tf = f
    for cand in (1408, 1024, 512, 256, 128):
        if f % cand == 0:
            tf = cand
            break
    return _tile(t, 512), tf


def _ffn_fwd(x, nw, wg_t, wu_t, wd, *, name, exchange=None):
    t, d = x.shape
    f = wd.shape[0]
    tm, tf = _ffn_tiles(t, f)
    nf = f // tf

    def body(x_ref, nw_ref, wg_ref, wu_ref, wd_ref, xo_ref, h_ref, g_ref, u_ref, a_ref, hs_ref, acc_ref):
        j = pl.program_id(1)

        @pl.when(j == 0)
        def _():
            xhat, _ = _norm_parts(x_ref[...])
            hb = _bf(xhat * nw_ref[...])
            hs_ref[...] = hb
            h_ref[...] = hb

        hb = hs_ref[...]
        g = _dot_nt(hb, wg_ref[...])
        u = _dot_nt(hb, wu_ref[...])
        a = _bf(g * _sigmoid(g) * u)
        g_ref[...] = _bf(g)
        u_ref[...] = _bf(u)
        a_ref[...] = a
        part = _dot(a, wd_ref[...])

        def finish(total):
            xo_ref[...] = x_ref[...] + 0.5 * total

        _accumulate(acc_ref, part, j, nf, finish)

    row = pl.BlockSpec((tm, d), lambda i, j: (i, 0))
    wspec = pl.BlockSpec((tf, d), lambda i, j: (j, 0))
    hid = pl.BlockSpec((tm, tf), lambda i, j: (i, j))
    return _pcall(
        body, (x, nw, wg_t, wu_t, wd), name=name, grid=(t // tm, nf),
        in_specs=[row, pl.BlockSpec((1, d), lambda i, j: (0, 0)), wspec, wspec, wspec],
        out_specs=[row, row, hid, hid, hid],
        out_shape=[jax.ShapeDtypeStruct((t, d), F32), jax.ShapeDtypeStruct((t, d), BF16)]
        + [jax.ShapeDtypeStruct((t, f), BF16)] * 3,
        scratch_shapes=[pltpu.VMEM((tm, d), BF16), pltpu.VMEM((tm, d), F32)],
        sem=("parallel", "arbitrary"), exchange=exchange)


def _ffn_bwd(dxo, x, nw, g, u, wg_t, wu_t, wd, *, name, exchange=None):
    t, d = x.shape
    f = wd.shape[0]
    tm, tf = _ffn_tiles(t, f)
    nf = f // tf

    def body(dxo_ref, x_ref, nw_ref, g_ref, u_ref, wg_ref, wu_ref, wd_ref,
             dx_ref, dy_ref, dg_ref, du_ref, dnw_ref, dys_ref, acc_ref):
        i, j = pl.program_id(0), pl.program_id(1)

        @pl.when(j == 0)
        def _():
            dyb = _bf(0.5 * dxo_ref[...])
            dys_ref[...] = dyb
            dy_ref[...] = dyb

        da = _dot_nt(dys_ref[...], wd_ref[...])
        gv = g_ref[...].astype(F32)
        uv = u_ref[...].astype(F32)
        s = _sigmoid(gv)
        dg = _bf(da * uv * (s * (1.0 + gv * (1.0 - s))))
        du = _bf(da * (gv * s))
        dg_ref[...] = dg
        du_ref[...] = du
        part = _dot(dg, wg_ref[...]) + _dot(du, wu_ref[...])

        def finish(dh):
            xhat, r = _norm_parts(x_ref[...])
            dx, dw = _norm_bwd(dh, xhat, r, nw_ref[...])
            dx_ref[...] = dxo_ref[...] + dx

            @pl.when(i == 0)
            def _():
                dnw_ref[...] = dw

            @pl.when(i > 0)
            def _():
                dnw_ref[...] += dw

        _accumulate(acc_ref, part, j, nf, finish)

    row = pl.BlockSpec((tm, d), lambda i, j: (i, 0))
    vec = pl.BlockSpec((1, d), lambda i, j: (0, 0))
    wspec = pl.BlockSpec((tf, d), lambda i, j: (j, 0))
    hid = pl.BlockSpec((tm, tf), lambda i, j: (i, j))
    return _pcall(
        body, (dxo, x, nw, g, u, wg_t, wu_t, wd), name=name, grid=(t // tm, nf),
        in_specs=[row, row, vec, hid, hid, wspec, wspec, wspec],
        out_specs=[row, row, hid, hid, vec],
        out_shape=[jax.ShapeDtypeStruct((t, d), F32), jax.ShapeDtypeStruct((t, d), BF16),
                   jax.ShapeDtypeStruct((t, f), BF16), jax.ShapeDtypeStruct((t, f), BF16),
                   jax.ShapeDtypeStruct((1, d), F32)],
        scratch_shapes=[pltpu.VMEM((tm, d), BF16), pltpu.VMEM((tm, d), F32)],
        sem=("arbitrary", "arbitrary"), exchange=exchange)


def _shift_down(prev, cur, n):
    ext = jnp.concatenate([prev, cur], axis=0)
    return pltpu.roll(ext, n, axis=0)[prev.shape[0]:]


def _shift_up(cur, nxt, n):
    ext = jnp.concatenate([cur, nxt], axis=0)
    return pltpu.roll(ext, ext.shape[0] - n, axis=0)[:cur.shape[0]]


def _conv_specs(t, tb):
    hb = tb // CONV_HALO
    last = t // CONV_HALO - 1

    def tile(s):
        return pl.BlockSpec((None, tb, 128), lambda c, i: (s, i, c))

    def prev(s):
        return pl.BlockSpec((None, CONV_HALO, 128), lambda c, i: (s, jnp.maximum(i * hb - 1, 0), c))

    def nxt(s):
        return pl.BlockSpec((None, CONV_HALO, 128), lambda c, i: (s, jnp.minimum((i + 1) * hb, last), c))

    return tile, prev, nxt


def _conv_fwd(cols3, conv_w, *, name, tb=1024):
    _, t, bw = cols3.shape
    tb = _tile(t, tb)
    tile, prev, _ = _conv_specs(t, tb)

    def body(u_ref, b_ref, c_ref, up_ref, cp_ref, w_ref, y_ref):
        first = pl.program_id(1) == 0
        z = c_ref[...].astype(F32) * u_ref[...].astype(F32)
        zp = jnp.where(first, 0.0, cp_ref[...].astype(F32) * up_ref[...].astype(F32))
        conv = w_ref[0:1, :] * _shift_down(zp, z, 2) + w_ref[1:2, :] * _shift_down(zp, z, 1) + w_ref[2:3, :] * z
        y_ref[...] = _bf(b_ref[...].astype(F32) * conv)

    return pl.pallas_call(
        body, name=name, grid=(bw // 128, t // tb),
        in_specs=[tile(0), tile(1), tile(2), prev(0), prev(2), pl.BlockSpec((3, 128), lambda c, i: (0, c))],
        out_specs=pl.BlockSpec((tb, 128), lambda c, i: (i, c)),
        out_shape=jax.ShapeDtypeStruct((t, bw), BF16), compiler_params=_cparams("parallel", "parallel"),
    )(cols3, cols3, cols3, cols3, cols3, conv_w)


def _conv_bwd(cols3, conv_w, dy, *, name, tb=1024):
    _, t, bw = cols3.shape
    tb = _tile(t, tb)
    nt = t // tb
    tile, prev, nxt = _conv_specs(t, tb)
    hb = tb // CONV_HALO
    last = t // CONV_HALO - 1

    def body(u_ref, b_ref, c_ref, up_ref, cp_ref, bn_ref, dy_ref, dyn_ref, w_ref, d3_ref, dw_ref):
        i = pl.program_id(1)
        uv, bv, cv = u_ref[...].astype(F32), b_ref[...].astype(F32), c_ref[...].astype(F32)
        dyv = dy_ref[...].astype(F32)
        z = cv * uv
        zp = jnp.where(i == 0, 0.0, cp_ref[...].astype(F32) * up_ref[...].astype(F32))
        z1, z2 = _shift_down(zp, z, 1), _shift_down(zp, z, 2)
        w0, w1, w2 = w_ref[0:1, :], w_ref[1:2, :], w_ref[2:3, :]
        conv = w0 * z2 + w1 * z1 + w2 * z
        dconv = dyv * bv
        dconv_n = jnp.where(i == nt - 1, 0.0, dyn_ref[...].astype(F32) * bn_ref[...].astype(F32))
        dz = w2 * dconv + w1 * _shift_up(dconv, dconv_n, 1) + w0 * _shift_up(dconv, dconv_n, 2)
        d3_ref[0] = _bf(dz * cv)
        d3_ref[1] = _bf(dyv * conv)
        d3_ref[2] = _bf(dz * uv)
        dws = [jnp.sum(dconv * zz, axis=0, keepdims=True) for zz in (z2, z1, z)]

        @pl.when(i == 0)
        def _():
            for j in range(3):
                dw_ref[j:j + 1, :] = dws[j]

        @pl.when(i > 0)
        def _():
            for j in range(3):
                dw_ref[j:j + 1, :] += dws[j]

    dy_tile = pl.BlockSpec((None, tb, 128), lambda c, i: (0, i, c))
    dy_next = pl.BlockSpec((None, CONV_HALO, 128), lambda c, i: (0, jnp.minimum((i + 1) * hb, last), c))
    wspec = pl.BlockSpec((3, 128), lambda c, i: (0, c))
    return pl.pallas_call(
        body, name=name, grid=(bw // 128, nt),
        in_specs=[tile(0), tile(1), tile(2), prev(0), prev(2), nxt(1), dy_tile, dy_next, wspec],
        out_specs=[pl.BlockSpec((3, tb, 128), lambda c, i: (0, i, c)), wspec],
        out_shape=[jax.ShapeDtypeStruct((3, t, bw), BF16), jax.ShapeDtypeStruct((3, bw), F32)],
        compiler_params=_cparams("parallel", "arbitrary"),
    )(cols3, cols3, cols3, cols3, cols3, cols3, dy, dy, conv_w)


def _ret_consts():
    log_gamma = jnp.log1p(-jnp.exp2(-5.0 - jnp.arange(H_RET, dtype=F32)))
    pos = jnp.arange(CHUNK, dtype=F32)
    d_intra = jnp.exp(log_gamma[:, None, None] * jnp.abs(pos[:, None] - pos[None, :]))
    q_decay = jnp.exp(log_gamma[:, None] * (pos + 1.0))
    k_decay = jnp.exp(log_gamma[:, None] * (CHUNK - 1.0 - pos))
    chunk_decay = jnp.exp(log_gamma * CHUNK)
    wide = (H_RET, CHUNK, DK_RET)
    return (d_intra, jnp.broadcast_to(q_decay[:, :, None], wide), jnp.broadcast_to(k_decay[:, :, None], wide),
            jnp.broadcast_to(chunk_decay[:, None, None], (H_RET, 1, DK_RET)))


def _rope_tables(t):
    inv_freq = ROPE_BASE ** (-jnp.linspace(0.0, 1.0, DK_RET // 2, dtype=F32))
    ang = jnp.arange(t, dtype=F32)[:, None] * inv_freq[None, :]
    cos, sin = jnp.cos(ang), jnp.sin(ang)
    return jnp.concatenate([cos, cos], axis=1), jnp.concatenate([-sin, sin], axis=1)


def _rope(v, cc, ss):
    return v * cc + pltpu.roll(v, DK_RET // 2, axis=1) * ss


def _ret_in_specs(tb, blk):
    def col(s):
        return pl.BlockSpec((None, tb, H_RET * DK_RET), lambda i: (s, blk(i), 0))

    tab = pl.BlockSpec((tb, DK_RET), lambda i: (blk(i), 0))
    return ([col(3), col(4), col(5), col(6), tab, tab,
             pl.BlockSpec((H_RET, CHUNK, CHUNK), lambda i: (0, 0, 0)),
             pl.BlockSpec((H_RET, CHUNK, DK_RET), lambda i: (0, 0, 0)),
             pl.BlockSpec((H_RET, CHUNK, DK_RET), lambda i: (0, 0, 0)),
             pl.BlockSpec((H_RET, 1, DK_RET), lambda i: (0, 0, 0))])


def _ret_fwd(cols3, tables, consts, *, name):
    _, t, bw = cols3.shape
    tb = _tile(t, RET_TB)
    ncb = tb // CHUNK
    scale = DK_RET ** -0.5

    def body(q_ref, k_ref, v_ref, g_ref, cc_ref, ss_ref, di_ref, qd_ref, kd_ref, cd_ref, y_ref, st_ref, state):
        @pl.when(pl.program_id(0) == 0)
        def _():
            state[...] = jnp.zeros_like(state)

        for c in range(ncb):
            rows = pl.ds(c * CHUNK, CHUNK)
            cc, ss = cc_ref[rows, :], ss_ref[rows, :]
            for h in range(H_RET):
                lanes = pl.ds(h * DK_RET, DK_RET)
                qs = _rope(q_ref[rows, lanes].astype(F32), cc, ss) * scale
                kr = _rope(k_ref[rows, lanes].astype(F32), cc, ss)
                vb = v_ref[rows, lanes]
                gv = g_ref[rows, lanes].astype(F32)
                s_in = state[h]
                st_ref[h, c] = s_in
                inner = _dot_nt(_bf(qs), _bf(kr)) * di_ref[h]
                o = _dot(_bf(inner), vb) + _dot(_bf(qs * qd_ref[h]), _bf(s_in))
                state[h] = s_in * cd_ref[h] + _dot_tn(_bf(kr * kd_ref[h]), vb)
                on = o * lax.rsqrt(jnp.mean(o * o, axis=-1, keepdims=True) + EPS)
                y_ref[rows, lanes] = _bf(gv * _sigmoid(gv) * on)

    return pl.pallas_call(
        body, name=name, grid=(t // tb,),
        in_specs=_ret_in_specs(tb, lambda i: i),
        out_specs=[pl.BlockSpec((tb, bw), lambda i: (i, 0)),
                   pl.BlockSpec((H_RET, ncb, DK_RET, DK_RET), lambda i: (0, i, 0, 0))],
        out_shape=[jax.ShapeDtypeStruct((t, bw), BF16),
                   jax.ShapeDtypeStruct((H_RET, t // CHUNK, DK_RET, DK_RET), F32)],
        scratch_shapes=[pltpu.VMEM((H_RET, DK_RET, DK_RET), F32)],
        compiler_params=_cparams("arbitrary"),
    )(cols3, cols3, cols3, cols3, tables[0], tables[1], *consts)


def _ret_bwd(cols3, tables, consts, states, dy, *, name, exchange=None):
    _, t, bw = cols3.shape
    tb = _tile(t, RET_TB)
    ncb = tb // CHUNK
    nb = t // tb
    scale = DK_RET ** -0.5

    def body(q_ref, k_ref, v_ref, g_ref, cc_ref, ss_ref, di_ref, qd_ref, kd_ref, cd_ref, st_ref, dy_ref,
             d4_ref, dstate):
        @pl.when(pl.program_id(0) == 0)
        def _():
            dstate[...] = jnp.zeros_like(dstate)

        heads = range(H_RET)
        lanes_of = [pl.ds(h * DK_RET, DK_RET) for h in heads]
        di, qd, kd = [di_ref[h] for h in heads], [qd_ref[h] for h in heads], [kd_ref[h] for h in heads]
        carried = [dstate[h] for h in heads]
        for c in reversed(range(ncb)):
            rows = pl.ds(c * CHUNK, CHUNK)
            cc, ss = cc_ref[rows, :], ss_ref[rows, :]
            vb = [v_ref[rows, ln] for ln in lanes_of]
            gv = [g_ref[rows, ln].astype(F32) for ln in lanes_of]
            dyv = [dy_ref[rows, ln].astype(F32) for ln in lanes_of]
            s_in = [_bf(st_ref[h, c]) for h in heads]
            qs = [_rope(q_ref[rows, ln].astype(F32), cc, ss) * scale for ln in lanes_of]
            kr = [_rope(k_ref[rows, ln].astype(F32), cc, ss) for ln in lanes_of]
            qsb, krb = [_bf(v) for v in qs], [_bf(v) for v in kr]
            qdb = [_bf(qs[h] * qd[h]) for h in heads]
            kdb = [_bf(kr[h] * kd[h]) for h in heads]
            innerb = [_bf(_dot_nt(qsb[h], krb[h]) * di[h]) for h in heads]
            o = [_dot(innerb[h], vb[h]) + _dot(qdb[h], s_in[h]) for h in heads]
            r = [lax.rsqrt(jnp.mean(v * v, axis=-1, keepdims=True) + EPS) for v in o]
            on = [o[h] * r[h] for h in heads]
            sg = [_sigmoid(v) for v in gv]
            dgv = [_bf(dyv[h] * on[h] * (sg[h] * (1.0 + gv[h] * (1.0 - sg[h])))) for h in heads]
            don = [dyv[h] * (gv[h] * sg[h]) for h in heads]
            dob = [_bf(r[h] * (don[h] - on[h] * jnp.mean(don[h] * on[h], axis=-1, keepdims=True))) for h in heads]
            dinner = [_bf(_dot_nt(dob[h], vb[h]) * di[h]) for h in heads]
            dqs = [_dot(dinner[h], krb[h]) + _dot_nt(dob[h], s_in[h]) * qd[h] for h in heads]
            dkr = [_dot_tn(dinner[h], qsb[h]) for h in heads]
            dv = [_dot_tn(innerb[h], dob[h]) for h in heads]
            dnew = [_dot_tn(qdb[h], dob[h]) for h in heads]
            dstb = [_bf(v) for v in carried]
            dv = [dv[h] + _dot(kdb[h], dstb[h]) for h in heads]
            dkr = [dkr[h] + _dot_nt(vb[h], dstb[h]) * kd[h] for h in heads]
            carried = [carried[h] * cd_ref[h] + dnew[h] for h in heads]
            for h, ln in enumerate(lanes_of):
                d4_ref[0, rows, ln] = _bf(_rope_bwd(dqs[h] * scale, cc, ss))
                d4_ref[1, rows, ln] = _bf(_rope_bwd(dkr[h], cc, ss))
                d4_ref[2, rows, ln] = _bf(dv[h])
                d4_ref[3, rows, ln] = dgv[h]
        for h in heads:
            dstate[h] = carried[h]

    rev = lambda i: nb - 1 - i
    outs, moved = _pcall(
        body, (cols3, cols3, cols3, cols3, tables[0], tables[1], *consts, states, dy), name=name, grid=(nb,),
        in_specs=_ret_in_specs(tb, rev)
        + [pl.BlockSpec((H_RET, ncb, DK_RET, DK_RET), lambda i: (0, rev(i), 0, 0)),
           pl.BlockSpec((None, tb, bw), lambda i: (1, rev(i), 0))],
        out_specs=[pl.BlockSpec((4, tb, bw), lambda i: (0, rev(i), 0))],
        out_shape=[jax.ShapeDtypeStruct((4, t, bw), BF16)],
        scratch_shapes=[pltpu.VMEM((H_RET, DK_RET, DK_RET), F32)],
        sem=("arbitrary",), exchange=exchange)
    return outs[0], moved


def _rope_bwd(dv, cc, ss):
    return dv * cc + pltpu.roll(dv * ss, DK_RET // 2, axis=1)


def _att_window(i):
    return pl.multiple_of(jnp.maximum(i - ATT_LOOKBACK // ATT_QB, 0) * ATT_QB, ATT_QB)


def _att_mask(v):
    qchunk = (v * ATT_QB + lax.broadcasted_iota(jnp.int32, (ATT_QB, ATT_WIN), 0)) // CHUNK
    kchunk = lax.broadcasted_iota(jnp.int32, (ATT_QB, ATT_WIN), 1) // CHUNK
    return (kchunk <= qchunk) & (kchunk >= qchunk - N_PREV_CHUNKS)


def _att_fwd(cols3, bias3, *, name, exchange=None):
    _, t, bw = cols3.shape
    assert t % ATT_QB == 0 and t >= ATT_WIN
    scale = DH_ATT ** -0.5
    nvar = ATT_LOOKBACK // ATT_QB

    def body(q_ref, k_ref, v_ref, b_ref, y_ref, lse_ref):
        i = pl.program_id(1)
        ws = _att_window(i)
        q = q_ref[...].astype(F32)
        kw = k_ref[pl.ds(ws, ATT_WIN), :]
        vw = v_ref[pl.ds(ws, ATT_WIN), :]
        lane_head = lax.broadcasted_iota(jnp.int32, (ATT_QB, 128), 1) // DH_ATT
        out = jnp.zeros((ATT_QB, 128), F32)
        lse = jnp.zeros((ATT_QB, 128), F32)
        for hh in range(2):
            mine = lane_head == hh
            s = _dot_nt(_bf(jnp.where(mine, q, 0.0)), kw) * scale + b_ref[hh]
            mx = jnp.max(s, axis=-1, keepdims=True)
            p = jnp.exp(s - mx)
            l = jnp.sum(p, axis=-1, keepdims=True)
            out = jnp.where(mine, _dot(_bf(p), vw) / l, out)
            lse = jnp.where(mine, mx + jnp.log(l), lse)
        y_ref[...] = _bf(out)
        lse_ref[...] = lse

    kv = lambda s: pl.BlockSpec((None, t, 128), lambda hp, i: (s, 0, hp))
    return _pcall(
        body, (cols3, cols3, cols3, bias3), name=name, grid=(H_ATT // 2, t // ATT_QB),
        in_specs=[pl.BlockSpec((None, ATT_QB, 128), lambda hp, i: (7, i, hp)), kv(8), kv(9),
                  pl.BlockSpec((None, 2, ATT_QB, ATT_WIN), lambda hp, i: (jnp.minimum(i, nvar), hp, 0, 0))],
        out_specs=[pl.BlockSpec((ATT_QB, 128), lambda hp, i: (i, hp)),
                   pl.BlockSpec((None, ATT_QB, 128), lambda hp, i: (hp, i, 0))],
        out_shape=[jax.ShapeDtypeStruct((t, bw), BF16), jax.ShapeDtypeStruct((H_ATT // 2, t, 128), F32)],
        scratch_shapes=[], sem=("parallel", "arbitrary"), exchange=exchange)


def _att_bwd(cols3, bias3, y, lse, dy, *, name, exchange=None):
    _, t, bw = cols3.shape
    nq = t // ATT_QB
    scale = DH_ATT ** -0.5
    nvar = ATT_LOOKBACK // ATT_QB

    def body(q_ref, k_ref, v_ref, b_ref, y_ref, lse_ref, dy_ref, d3_ref, db_ref, dk_acc, dv_acc):
        i = pl.program_id(1)

        @pl.when(i == 0)
        def _():
            dk_acc[...] = jnp.zeros_like(dk_acc)
            dv_acc[...] = jnp.zeros_like(dv_acc)

        ws = _att_window(i)
        q = q_ref[...].astype(F32)
        kw = k_ref[pl.ds(ws, ATT_WIN), :]
        vw = v_ref[pl.ds(ws, ATT_WIN), :]
        do = dy_ref[...].astype(F32)
        dof = do * y_ref[...].astype(F32)
        lsev = lse_ref[...]
        lane_head = lax.broadcasted_iota(jnp.int32, (ATT_QB, 128), 1) // DH_ATT
        dq = jnp.zeros((ATT_QB, 128), F32)
        dk = jnp.zeros((ATT_WIN, 128), F32)
        dv = jnp.zeros((ATT_WIN, 128), F32)
        first = i <= nvar
        for hh in range(2):
            mine = lane_head == hh
            qh = _bf(jnp.where(mine, q, 0.0))
            doh = _bf(jnp.where(mine, do, 0.0))
            s = _dot_nt(qh, kw) * scale + b_ref[hh]
            lse_h = jnp.max(jnp.where(mine, lsev, NEG_INF), axis=-1, keepdims=True)
            p = jnp.exp(s - lse_h)
            delta = jnp.sum(jnp.where(mine, dof, 0.0), axis=-1, keepdims=True)
            ds = p * (_dot_nt(doh, vw) - delta)

            @pl.when(first)
            def _():
                db_ref[hh] = ds

            @pl.when(jnp.logical_not(first))
            def _():
                db_ref[hh] += ds

            dsb = _bf(ds * scale)
            dq = jnp.where(mine, _dot(dsb, kw), dq)
            dk = dk + _dot_tn(dsb, qh)
            dv = dv + _dot_tn(_bf(p), doh)
        d3_ref[0, pl.ds(pl.multiple_of(i * ATT_QB, ATT_QB), ATT_QB), :] = _bf(dq)
        dk_acc[pl.ds(ws, ATT_WIN), :] += dk
        dv_acc[pl.ds(ws, ATT_WIN), :] += dv

        @pl.when(i == nq - 1)
        def _():
            d3_ref[1] = _bf(dk_acc[...])
            d3_ref[2] = _bf(dv_acc[...])

    kv = lambda s: pl.BlockSpec((None, t, 128), lambda hp, i: (s, 0, hp))
    qrow = pl.BlockSpec((ATT_QB, 128), lambda hp, i: (i, hp))
    btile = pl.BlockSpec((None, 2, ATT_QB, ATT_WIN), lambda hp, i: (jnp.minimum(i, nvar), hp, 0, 0))
    return _pcall(
        body, (cols3, cols3, cols3, bias3, y, lse, dy), name=name, grid=(H_ATT // 2, nq),
        in_specs=[pl.BlockSpec((None, ATT_QB, 128), lambda hp, i: (7, i, hp)), kv(8), kv(9), btile, qrow,
                  pl.BlockSpec((None, ATT_QB, 128), lambda hp, i: (hp, i, 0)),
                  pl.BlockSpec((None, ATT_QB, 128), lambda hp, i: (2, i, hp))],
        out_specs=[pl.BlockSpec((3, t, 128), lambda hp, i: (0, 0, hp)), btile],
        out_shape=[jax.ShapeDtypeStruct((3, t, bw), BF16),
                   jax.ShapeDtypeStruct((nvar + 1, H_ATT, ATT_QB, ATT_WIN), F32)],
        scratch_shapes=[pltpu.VMEM((t, 128), F32), pltpu.VMEM((t, 128), F32)],
        sem=("parallel", "arbitrary"), exchange=exchange)


SKEW_W = ATT_WIN + ATT_QB
REL_PAD = 384


def _rel_onehot(v):
    r = lax.broadcasted_iota(jnp.int32, (REL_PAD, SKEW_W), 0)
    j = lax.broadcasted_iota(jnp.int32, (REL_PAD, SKEW_W), 1)
    dist = jnp.where(j < ATT_WIN, v * ATT_QB - j, v * ATT_QB + SKEW_W - j)
    col = jnp.clip(dist, -REL_CLIP, REL_CLIP) + REL_CLIP
    return _bf(jnp.where(col == r, 1.0, 0.0))


def _split3(v):
    hi = _bf(v)
    rest = v - hi.astype(F32)
    mid = _bf(rest)
    return hi, mid, _bf(rest - mid.astype(F32))


def _skew8(a, forward):
    row = lax.broadcasted_iota(jnp.int32, a.shape, 0)
    for b in range(3):
        shift = (1 << b) if forward else SKEW_W - (1 << b)
        a = jnp.where(((row >> b) & 1) == 1, pltpu.roll(a, shift, axis=1), a)
    return a


def _toeplitz_rows(ext_row):
    a = _skew8(jnp.broadcast_to(ext_row, (8, SKEW_W)), True)
    while a.shape[0] < ATT_QB:
        a = jnp.concatenate([a, pltpu.roll(a, a.shape[0], axis=1)], axis=0)
    return a


def _diagonal_sums(tile):
    a = tile
    while a.shape[0] > 8:
        half = a.shape[0] // 2
        a = a[:half] + pltpu.roll(a[half:], SKEW_W - half, axis=1)
    return jnp.sum(_skew8(a, False), axis=0, keepdims=True)


def _bias_tiles(rel_bias, *, name, exchange=None):
    depth = rel_bias.shape[0]
    nvar = ATT_LOOKBACK // ATT_QB + 1
    rel = jnp.pad(rel_bias, ((0, 0), (0, 0), (0, REL_PAD - N_REL)))

    def body(rel_ref, o_ref, ext_ref):
        v = pl.program_id(1)
        onehot = _rel_onehot(v)
        ext_ref[...] = sum(_dot(part, onehot) for part in _split3(rel_ref[...]))
        valid = _att_mask(v)
        for h in range(H_ATT):
            o_ref[h] = jnp.where(valid, _toeplitz_rows(ext_ref[h:h + 1, :])[:, :ATT_WIN], NEG_INF)

    outs, moved = _pcall(
        body, (rel,), name=name, grid=(depth, nvar),
        in_specs=[pl.BlockSpec((None, H_ATT, REL_PAD), lambda l, v: (l, 0, 0))],
        out_specs=[pl.BlockSpec((None, None, H_ATT, ATT_QB, ATT_WIN), lambda l, v: (l, v, 0, 0, 0))],
        out_shape=[jax.ShapeDtypeStruct((depth, nvar, H_ATT, ATT_QB, ATT_WIN), F32)],
        scratch_shapes=[pltpu.VMEM((H_ATT, SKEW_W), F32)], sem=("parallel", "parallel"), exchange=exchange)
    return outs[0], moved


def _rel_bias_grad(dbias3, *, name):
    nvar = dbias3.shape[0]

    def body(db_ref, o_ref, diag_ref):
        v = pl.program_id(0)
        for h in range(H_ATT):
            tile = jnp.concatenate([db_ref[h], jnp.zeros((ATT_QB, ATT_QB), F32)], axis=1)
            diag_ref[h:h + 1, :] = _diagonal_sums(tile)
        onehot = _rel_onehot(v)
        part = sum(_dot_nt(p, onehot) for p in _split3(diag_ref[...]))

        @pl.when(v == 0)
        def _():
            o_ref[...] = part

        @pl.when(v > 0)
        def _():
            o_ref[...] += part

    out = pl.pallas_call(
        body, name=name, grid=(nvar,),
        in_specs=[pl.BlockSpec((None, H_ATT, ATT_QB, ATT_WIN), lambda v: (v, 0, 0, 0))],
        out_specs=pl.BlockSpec((H_ATT, REL_PAD), lambda v: (0, 0)),
        out_shape=jax.ShapeDtypeStruct((H_ATT, REL_PAD), F32),
        scratch_shapes=[pltpu.VMEM((H_ATT, SKEW_W), F32)], compiler_params=_cparams("arbitrary"),
    )(dbias3)
    return out[:, :N_REL]


def _merge_fwd(h, ys, wmg, wb_t, *, name, tm=512):
    t, d = h.shape
    bw = ys[0].shape[1]
    tm = _tile(t, tm)

    def body(h_ref, y0_ref, y1_ref, y2_ref, wg_ref, wb_ref, m_ref, s_ref, p_ref):
        hv = h_ref[...]
        total = jnp.zeros((tm, d), F32)
        for b, y_ref in enumerate((y0_ref, y1_ref, y2_ref)):
            s = _sigmoid(_dot(hv, wg_ref[b]))
            p = _dot_nt(y_ref[...], wb_ref[b])
            s_ref[b] = _bf(s)
            p_ref[b] = _bf(p)
            total = total + s * p
        m_ref[...] = _bf(total)

    row = pl.BlockSpec((tm, d), lambda i: (i, 0))
    yrow = pl.BlockSpec((tm, bw), lambda i: (i, 0))
    three = pl.BlockSpec((3, tm, d), lambda i: (0, i, 0))
    return pl.pallas_call(
        body, name=name, grid=(t // tm,),
        in_specs=[row, yrow, yrow, yrow, pl.BlockSpec((3, d, d), lambda i: (0, 0, 0)),
                  pl.BlockSpec((3, d, bw), lambda i: (0, 0, 0))],
        out_specs=[row, three, three],
        out_shape=[jax.ShapeDtypeStruct((t, d), BF16), jax.ShapeDtypeStruct((3, t, d), BF16),
                   jax.ShapeDtypeStruct((3, t, d), BF16)],
        compiler_params=_cparams("parallel"),
    )(h, *ys, wmg, wb_t)


def _merge_bwd(dm, s3, p3, wmg, wb_t, *, name, tm=512):
    _, t, d = s3.shape
    bw = wb_t.shape[2]
    tm = _tile(t, tm)

    def body(dm_ref, s_ref, p_ref, wg_ref, wb_ref, dgp_ref, dp_ref, dy_ref, dh_ref):
        dmv = dm_ref[...].astype(F32)
        dh = jnp.zeros((tm, d), F32)
        for b in range(3):
            s = s_ref[b].astype(F32)
            dgp = _bf(dmv * p_ref[b].astype(F32) * s * (1.0 - s))
            dp = _bf(dmv * s)
            dgp_ref[b] = dgp
            dp_ref[b] = dp
            dy_ref[b] = _bf(_dot(dp, wb_ref[b]))
            dh = dh + _dot_nt(dgp, wg_ref[b])
        dh_ref[...] = dh

    row = pl.BlockSpec((tm, d), lambda i: (i, 0))
    three = pl.BlockSpec((3, tm, d), lambda i: (0, i, 0))
    return pl.pallas_call(
        body, name=name, grid=(t // tm,),
        in_specs=[row, three, three, pl.BlockSpec((3, d, d), lambda i: (0, 0, 0)),
                  pl.BlockSpec((3, d, bw), lambda i: (0, 0, 0))],
        out_specs=[three, three, pl.BlockSpec((3, tm, bw), lambda i: (0, i, 0)), row],
        out_shape=[jax.ShapeDtypeStruct((3, t, d), BF16), jax.ShapeDtypeStruct((3, t, d), BF16),
                   jax.ShapeDtypeStruct((3, t, bw), BF16), jax.ShapeDtypeStruct((t, d), F32)],
        compiler_params=_cparams("parallel"),
    )(dm, s3, p3, wmg, wb_t)


def _carried(hooks, slot, state=None):
    if not hooks or slot not in hooks:
        return None, lambda buf: None
    make, done = hooks[slot]
    return make(state), done


def _layer_fwd(x, p, aux, tag, hooks=None):
    ex, done = _carried(hooks, "ffn1")
    (x1, h1, g1, u1, a1), buf = _ffn_fwd(x, p["n1"], p["wg1"], p["wu1"], p["wd1"], name=f"ffn1_fwd_{tag}", exchange=ex)
    done(buf)
    h2 = _rmsnorm_fwd(x1, p["nmix"], name=f"mixnorm_fwd_{tag}")
    ex, done = _carried(hooks, "inproj")
    if ex is None:
        cols3 = _mm_to_slices(h2, p["win"], name=f"inproj_fwd_{tag}")
    else:
        cols3, buf = _mm_to_slices(h2, p["win"], name=f"inproj_fwd_{tag}", exchange=ex)
        done(buf)
    bias3 = p["bias3"]
    y_conv = _conv_fwd(cols3, p["conv_w"], name=f"conv_fwd_{tag}")
    y_ret, states = _ret_fwd(cols3, aux["rope"], aux["ret"], name=f"ret_fwd_{tag}")
    ex, done = _carried(hooks, "att")
    (y_att, lse), buf = _att_fwd(cols3, bias3, name=f"att_fwd_{tag}", exchange=ex)
    done(buf)
    merged, s3, p3 = _merge_fwd(h2, (y_conv, y_ret, y_att), p["wmg"], p["wb"], name=f"merge_fwd_{tag}")
    x2 = _mm_rows(merged, p["wout"], dims=NN, out_dtype=F32, res=x1, name=f"outproj_fwd_{tag}")
    ex, done = _carried(hooks, "ffn2")
    (x3, h3, g2, u2, a2), buf = _ffn_fwd(x2, p["n2"], p["wg2"], p["wu2"], p["wd2"], name=f"ffn2_fwd_{tag}", exchange=ex)
    done(buf)
    saved = dict(x0=x, h1=h1, g1=g1, u1=u1, a1=a1, x1=x1, h2=h2, cols3=cols3, bias3=bias3, y_conv=y_conv,
                 y_ret=y_ret, states=states, y_att=y_att, lse=lse, merged=merged, s3=s3, p3=p3, x2=x2, h3=h3,
                 g2=g2, u2=u2, a2=a2)
    return x3, saved


def _ffn_grads(dxo, x, h, g, u, a, p, which, tag, grads, all_grads, hooks):
    n = which
    ex, done = _carried(hooks, f"ffn{n}_bwd", all_grads)
    (dx, dyb, dg, du, grads["n" + n]), buf = _ffn_bwd(dxo, x, p["n" + n], g, u, p["wg" + n], p["wu" + n], p["wd" + n],
                                                     name=f"ffn{n}_{tag}_bwd", exchange=ex)
    done(buf)
    grads["wd" + n] = _mm_tn(a, dyb, name=f"ffn{n}_{tag}_dwd")
    for key, lhs in (("wg" + n, dg), ("wu" + n, du)):
        ex, done = _carried(hooks, f"ffn{n}_d{key[:2]}", all_grads)
        if ex is None:
            grads[key] = _mm_tn(lhs, h, name=f"ffn{n}_{tag}_d{key[:2]}")
        else:
            grads[key], buf = _mm_tn(lhs, h, name=f"ffn{n}_{tag}_d{key[:2]}", exchange=ex)
            done(buf)
    return dx


def _layer_bwd(dx3, p, s, aux, tag, grads, all_grads, hooks=None):
    dx2 = _ffn_grads(dx3, s["x2"], s["h3"], s["g2"], s["u2"], s["a2"], p, "2", tag, grads, all_grads, hooks)

    dm = _mm_rows(dx2, p["wout"], dims=NT, out_dtype=BF16, name=f"outproj_dm_{tag}")
    grads["wout"] = _mm_tn(s["merged"], dx2, name=f"outproj_dw_{tag}")
    dgp3, dp3, dy3, dh2 = _merge_bwd(dm, s["s3"], s["p3"], p["wmg"], p["wb"], name=f"merge_bwd_{tag}")
    grads["wmg"] = _mm_tn_batch(s["h2"], dgp3, a_batched=False, name=f"merge_dwg_{tag}")
    y3 = jnp.stack([s["y_conv"], s["y_ret"], s["y_att"]])
    grads["wb"] = _mm_tn_batch(dp3, y3, a_batched=True, name=f"merge_dwb_{tag}")

    dconv3, grads["conv_w"] = _conv_bwd(s["cols3"], p["conv_w"], dy3, name=f"conv_bwd_{tag}")
    ex, done = _carried(hooks, "ret_bwd", all_grads)
    dret4, buf = _ret_bwd(s["cols3"], aux["rope"], aux["ret"], s["states"], dy3, name=f"ret_bwd_{tag}", exchange=ex)
    done(buf)
    ex, done = _carried(hooks, "att_bwd", all_grads)
    (datt3, dbias3), buf = _att_bwd(s["cols3"], s["bias3"], s["y_att"], s["lse"], dy3, name=f"att_bwd_{tag}",
                                    exchange=ex)
    done(buf)
    grads["rel_bias"] = _rel_bias_grad(dbias3, name=f"bias_grad_{tag}")

    dh2 = _inproj_dh((dconv3, dret4, datt3), p["win"], dh2, name=f"inproj_dh_{tag}")
    grads["win"] = jnp.concatenate([
        _mm_tn_slices(dconv3, s["h2"], name=f"inproj_dw_conv_{tag}"),
        _mm_tn_slices(dret4, s["h2"], name=f"inproj_dw_ret_{tag}"),
        _mm_tn_slices(datt3, s["h2"], name=f"inproj_dw_att_{tag}")], axis=0)
    dx1, grads["nmix"] = _rmsnorm_bwd(dh2, s["x1"], p["nmix"], dx2, name=f"mixnorm_bwd_{tag}")

    return _ffn_grads(dx1, s["x0"], s["h1"], s["g1"], s["u1"], s["a1"], p, "1", tag, grads, all_grads, hooks)


def _device_step(x, target, layers, final_norm, fwd_hooks=None, bwd_hooks=None):
    t = x.shape[0]
    depth = len(layers)
    aux = dict(rope=_rope_tables(t), ret=_ret_consts())
    ex, done = _carried(fwd_hooks[0] if fwd_hooks else None, "start")
    bias_all, buf = _bias_tiles(jnp.stack([p["rel_bias"] for p in layers]), name="bias_tiles", exchange=ex)
    done(buf)
    saved = []
    for l, p in enumerate(layers):
        p["bias3"] = bias_all[l]
        x, s = _layer_fwd(x, p, aux, f"l{l}", fwd_hooks[l] if fwd_hooks else None)
        saved.append(s)
    dx, dfinal, loss_row = _loss_fwd_bwd(x, final_norm, target, name="loss_fwd_bwd")
    grads = [dict() for _ in range(depth)]
    for l in reversed(range(depth)):
        dx = _layer_bwd(dx, layers[l], saved[l], aux, f"l{l}", grads[l], grads, bwd_hooks[l] if bwd_hooks else None)
    return loss_row, dx, grads, dfinal


def _sum_slots(bufs, *, name, tr=512):
    n, r, cdim = bufs[0].shape
    depth = len(bufs)
    tr = _tile(r, tr)
    nt = r // tr

    def body(*refs):
        o_ref = refs[depth]
        for k in range(depth):
            @pl.when(pl.program_id(0) == k)
            def _(p_ref=refs[k]):
                acc = p_ref[0].astype(F32)
                for s in range(1, n):
                    acc = acc + p_ref[s].astype(F32)
                o_ref[...] = acc

    def spec(k):
        return pl.BlockSpec((n, tr, cdim), lambda l, i: (0, jnp.where(l == k, i, jnp.where(l < k, 0, nt - 1)), 0))

    return pl.pallas_call(
        body, name=name, grid=(depth, nt), in_specs=[spec(k) for k in range(depth)],
        out_specs=pl.BlockSpec((None, tr, cdim), lambda l, i: (l, i, 0)),
        out_shape=jax.ShapeDtypeStruct((depth, r, cdim), F32), compiler_params=_cparams("arbitrary", "arbitrary"),
    )(*bufs)


def _all_reduce_small(v, *, name):
    r = v.shape[0]

    def body(x_ref, o_ref, slots, send_sems, recv_sems):
        x, y, c, me = _my_position()
        slots[me] = x_ref[...]
        sends = []
        for k in range(1, N_DEV):
            peer, _ = _peer(x, y, c, k)
            cp = pltpu.make_async_remote_copy(src_ref=x_ref, dst_ref=slots.at[me], send_sem=send_sems.at[k - 1],
                                              recv_sem=recv_sems.at[k - 1], device_id=peer, device_id_type=MESH)
            cp.start()
            sends.append(cp)
        for k in range(1, N_DEV):
            peer, peer_id = _peer(x, y, c, k)
            pltpu.make_async_remote_copy(src_ref=x_ref, dst_ref=slots.at[peer_id], send_sem=send_sems.at[k - 1],
                                         recv_sem=recv_sems.at[k - 1], device_id=peer, device_id_type=MESH).wait_recv()
        for cp in sends:
            cp.wait_send()
        acc = slots[0]
        for s in range(1, N_DEV):
            acc = acc + slots[s]
        o_ref[...] = acc

    return pl.pallas_call(
        body, name=name, in_specs=[pl.BlockSpec(memory_space=pltpu.VMEM)],
        out_specs=pl.BlockSpec(memory_space=pltpu.VMEM), out_shape=jax.ShapeDtypeStruct((r, 128), F32),
        scratch_shapes=[pltpu.VMEM((N_DEV, r, 128), F32), pltpu.SemaphoreType.DMA((N_DEV - 1,)),
                        pltpu.SemaphoreType.DMA((N_DEV - 1,))],
    )(v)


def _adamw(w, g, m, v, *, name, tr=256):
    shape = w.shape
    cdim = shape[-1]
    w2, g2, m2, v2 = (a.reshape(-1, cdim) for a in (w, g, m, v))
    r = w2.shape[0]
    tr = _tile(r, tr) if r % 8 == 0 else r
    c1 = 1.0 - ADAM_B1 ** ADAM_STEP
    c2 = 1.0 - ADAM_B2 ** ADAM_STEP

    def body(w_ref, g_ref, m_ref, v_ref, d_ref, mo_ref, vo_ref):
        gv = g_ref[...]
        mn = ADAM_B1 * m_ref[...] + (1.0 - ADAM_B1) * gv
        vn = ADAM_B2 * v_ref[...] + (1.0 - ADAM_B2) * (gv * gv)
        d_ref[...] = -ADAM_LR * ((mn / c1) / (jnp.sqrt(vn / c2) + ADAM_EPS) + ADAM_WD * w_ref[...])
        mo_ref[...] = mn
        vo_ref[...] = vn

    spec = pl.BlockSpec((tr, cdim), lambda i: (i, 0))
    outs = pl.pallas_call(
        body, name=name, grid=(r // tr,), in_specs=[spec] * 4, out_specs=[spec] * 3,
        out_shape=[jax.ShapeDtypeStruct((r, cdim), F32)] * 3, compiler_params=_cparams("parallel"),
    )(w2, g2, m2, v2)
    return tuple(o.reshape(shape) for o in outs)


GROUPS = {"f1": ("wg1", "wu1", "wd1"), "in": ("win",), "mg": ("wb", "wmg", "wout"), "f2": ("wg2", "wu2", "wd2")}
BIG = tuple(nm for members in GROUPS.values() for nm in members)
ROWS_DOMAIN = ("ffn1_w_gate", "ffn1_w_up", "ffn2_w_gate", "ffn2_w_up")


def _to_rows(name, w, d):
    depth = w.shape[0]
    if name in ("wg1", "wu1", "wg2", "wu2", "win"):
        return w.transpose(0, 2, 1)
    if name == "wb":
        return w.transpose(0, 1, 3, 2).reshape(depth, -1, d)
    if name == "wmg":
        return w.reshape(depth, -1, d)
    return w


def _from_rows(name, rows, d):
    depth = rows.shape[0]
    if name in ("wg1", "wu1", "wg2", "wu2", "win"):
        return rows.transpose(0, 2, 1)
    if name == "wb":
        return rows.reshape(depth, 3, -1, BRANCH_W).transpose(0, 1, 3, 2)
    if name == "wmg":
        return rows.reshape(depth, 3, -1, d)
    return rows


def _full_from_gathered(name, g, d):
    if name == "wb":
        return g.reshape(N_DEV, 3, -1, BRANCH_W).transpose(1, 0, 2, 3).reshape(3, d, BRANCH_W)
    if name == "wmg":
        return g.reshape(N_DEV, 3, -1, d).transpose(1, 0, 2, 3).reshape(3, d, d)
    return g.reshape(-1, d)


def _gathered_from_full(name, full, d):
    if name == "wb":
        return full.reshape(3, N_DEV, -1, BRANCH_W).transpose(1, 0, 2, 3).reshape(N_DEV, -1, d)
    if name == "wmg":
        return full.reshape(3, N_DEV, -1, d).transpose(1, 0, 2, 3).reshape(N_DEV, -1, d)
    return full.reshape(N_DEV, -1, d)


def kernel(x, ffn1_norm, ffn1_w_gate, ffn1_w_up, ffn1_w_down, mix_norm, w_in, conv_w, rel_bias, w_branch, w_merge_gate, w_out, ffn2_norm, ffn2_w_gate, ffn2_w_up, ffn2_w_down, final_norm, loss_target, m_ffn1_norm, m_ffn1_w_gate, m_ffn1_w_up, m_ffn1_w_down, m_mix_norm, m_w_in, m_conv_w, m_rel_bias, m_w_branch, m_w_merge_gate, m_w_out, m_ffn2_norm, m_ffn2_w_gate, m_ffn2_w_up, m_ffn2_w_down, m_final_norm, v_ffn1_norm, v_ffn1_w_gate, v_ffn1_w_up, v_ffn1_w_down, v_mix_norm, v_w_in, v_conv_w, v_rel_bias, v_w_branch, v_w_merge_gate, v_w_out, v_ffn2_norm, v_ffn2_w_gate, v_ffn2_w_up, v_ffn2_w_down, v_final_norm):
    names = ["ffn1_norm", "ffn1_w_gate", "ffn1_w_up", "ffn1_w_down", "mix_norm", "w_in", "conv_w", "rel_bias",
             "w_branch", "w_merge_gate", "w_out", "ffn2_norm", "ffn2_w_gate", "ffn2_w_up", "ffn2_w_down", "final_norm"]
    weights = dict(zip(names, (ffn1_norm, ffn1_w_gate, ffn1_w_up, ffn1_w_down, mix_norm, w_in, conv_w, rel_bias,
                               w_branch, w_merge_gate, w_out, ffn2_norm, ffn2_w_gate, ffn2_w_up, ffn2_w_down,
                               final_norm)))
    m_in = dict(zip(names, (m_ffn1_norm, m_ffn1_w_gate, m_ffn1_w_up, m_ffn1_w_down, m_mix_norm, m_w_in, m_conv_w,
                            m_rel_bias, m_w_branch, m_w_merge_gate, m_w_out, m_ffn2_norm, m_ffn2_w_gate,
                            m_ffn2_w_up, m_ffn2_w_down, m_final_norm)))
    v_in = dict(zip(names, (v_ffn1_norm, v_ffn1_w_gate, v_ffn1_w_up, v_ffn1_w_down, v_mix_norm, v_w_in, v_conv_w,
                            v_rel_bias, v_w_branch, v_w_merge_gate, v_w_out, v_ffn2_norm, v_ffn2_w_gate,
                            v_ffn2_w_up, v_ffn2_w_down, v_final_norm)))
    big_of = dict(wg1="ffn1_w_gate", wu1="ffn1_w_up", wd1="ffn1_w_down", win="w_in", wb="w_branch",
                  wmg="w_merge_gate", wout="w_out", wg2="ffn2_w_gate", wu2="ffn2_w_up", wd2="ffn2_w_down")
    depth = ffn1_norm.shape[0]
    d = x.shape[-1]
    xs = x.reshape(-1, d)
    target = loss_target.reshape(-1, d)
    _, _, _, me = _my_position()

    shard_rows = {nm: _to_rows(nm, weights[big_of[nm]], d).astype(BF16) for nm in BIG}

    conv_cols = conv_w.shape[-1]
    conv_full, = _exchange("gather", [conv_w.reshape(depth * 3, conv_cols)], name="gather_conv_w")
    conv_full = conv_full.reshape(N_DEV, depth, 3, conv_cols).transpose(1, 2, 0, 3).reshape(depth, 3, -1)
    layers = [dict(n1=ffn1_norm[l][None], nmix=mix_norm[l][None], n2=ffn2_norm[l][None], conv_w=conv_full[l],
                   rel_bias=rel_bias[l]) for l in range(depth)]

    def gather_hook(l, grp, kind="gather"):
        members = GROUPS[grp]

        def done(bufs):
            layers[l].update({nm: _full_from_gathered(nm, buf, d) for nm, buf in zip(members, bufs)})

        return (lambda _: (kind, [shard_rows[nm] for nm in members], [l] * len(members))), done

    fwd_hooks = []
    for l in range(depth):
        hooks = {"ffn1": gather_hook(l, "in"), "inproj": gather_hook(l, "mg"), "att": gather_hook(l, "f2")}
        if l == 0:
            hooks["start"] = gather_hook(0, "f1", "gather2")
        if l + 1 < depth:
            hooks["ffn2"] = gather_hook(l + 1, "f1")
        fwd_hooks.append(hooks)

    received = {}

    def scatter_hook(l, members):
        def make(all_grads):
            return "scatter", [_gathered_from_full(nm, all_grads[l][nm], d) for nm in members], [None] * len(members)

        return make, (lambda bufs: received.update({(l, nm): buf for nm, buf in zip(members, bufs)}))

    bwd_hooks = []
    for l in range(depth):
        hooks = {"att_bwd": scatter_hook(l, GROUPS["f2"]), "ret_bwd": scatter_hook(l, GROUPS["mg"]),
                 "ffn1_bwd": scatter_hook(l, GROUPS["in"])}
        if l + 1 < depth:
            hooks["ffn2_bwd"] = scatter_hook(l + 1, GROUPS["f1"])
        if l == 0:
            hooks["ffn1_dwg"] = scatter_hook(0, ("wd1",))
            hooks["ffn1_dwu"] = scatter_hook(0, ("wg1",))
        bwd_hooks.append(hooks)

    loss_row, dx, grads, dfinal = _device_step(xs, target, layers, final_norm[None], fwd_hooks, bwd_hooks)
    received[(0, "wu1")], = _exchange("scatter", [_gathered_from_full("wu1", grads[0]["wu1"], d)], name="scatter_l0_wu1")

    grad_rows = {big_of[nm]: _sum_slots([received[(l, nm)] for l in range(depth)], name=f"sum_grads_{nm}")
                 for nm in BIG}
    grad_w = {big_of[nm]: _from_rows(nm, grad_rows[big_of[nm]], d) for nm in BIG}

    small = {"ffn1_norm": jnp.stack([grads[l]["n1"][0] for l in range(depth)]),
             "mix_norm": jnp.stack([grads[l]["nmix"][0] for l in range(depth)]),
             "ffn2_norm": jnp.stack([grads[l]["n2"][0] for l in range(depth)]),
             "final_norm": dfinal[0],
             "rel_bias": jnp.stack([grads[l]["rel_bias"] for l in range(depth)]),
             "conv_w": jnp.stack([grads[l]["conv_w"] for l in range(depth)]),
             "loss": loss_row[0, :1]}
    order = list(small)
    flat = jnp.concatenate([small[k].reshape(-1) for k in order])
    pad = (-flat.shape[0]) % 1024
    summed = _all_reduce_small(jnp.pad(flat, (0, pad)).reshape(-1, 128), name="reduce_small").reshape(-1)
    pos = 0
    for k in order:
        n = small[k].size
        small[k] = summed[pos:pos + n].reshape(small[k].shape)
        pos += n
    small["conv_w"] = lax.dynamic_slice_in_dim(small["conv_w"], me * conv_cols, conv_cols, axis=2)

    grad_w.update({k: small[k] for k in order if k != "loss"})
    delta, new_m, new_v = {}, {}, {}
    for nm in names:
        if nm in ROWS_DOMAIN:
            flip = lambda a: a.transpose(0, 2, 1)
            outs = _adamw(flip(weights[nm]), grad_rows[nm], flip(m_in[nm]), flip(v_in[nm]), name=f"adamw_{nm}")
            delta[nm], new_m[nm], new_v[nm] = (flip(o) for o in outs)
        else:
            delta[nm], new_m[nm], new_v[nm] = _adamw(weights[nm], grad_w[nm], m_in[nm], v_in[nm], name=f"adamw_{nm}")
    return (small["loss"].reshape(()), dx.reshape(x.shape), *[grad_w[n] for n in names], *[delta[n] for n in names],
            *[new_m[n] for n in names], *[new_v[n] for n in names])
```

```python
import functools
import math

import jax
import jax.numpy as jnp
from jax import lax
from jax.experimental import pallas as pl
from jax.experimental.pallas import tpu as pltpu

F32 = jnp.float32
BF16 = jnp.bfloat16

N_DEV = 8
EPS = 1e-6
CHUNK = 64
BRANCH_W = 512
N_SLICES = 10
H_RET = 4
DK_RET = 128
H_ATT = 8
DH_ATT = 64
N_PREV_CHUNKS = 8
REL_CLIP = 128
N_REL = 2 * REL_CLIP + 1
NEG_INF = -1e30
ROPE_BASE = 10000.0
ATT_QB = 256
ATT_LOOKBACK = N_PREV_CHUNKS * CHUNK
ATT_WIN = ATT_LOOKBACK + ATT_QB
RET_TB = 256
RET_GROUP = 4
TOKEN_TK = 2048
PARTIAL_DTYPE = jnp.bfloat16
CONV_HALO = 16

ADAM_LR = 0.001
ADAM_B1 = 0.9
ADAM_B2 = 0.999
ADAM_EPS = 1e-08
ADAM_WD = 0.01
ADAM_STEP = 10

VMEM_LIMIT_BYTES = 56 * 1024 * 1024
MESH = pl.DeviceIdType.MESH


def _cparams(*sem):
    return pltpu.CompilerParams(dimension_semantics=sem, vmem_limit_bytes=VMEM_LIMIT_BYTES)


def _dot(a, b):
    return lax.dot_general(a, b, (((1,), (0,)), ((), ())), preferred_element_type=F32)


def _dot_nt(a, b):
    return lax.dot_general(a, b, (((1,), (1,)), ((), ())), preferred_element_type=F32)


def _dot_tn(a, b):
    return lax.dot_general(a, b, (((0,), (0,)), ((), ())), preferred_element_type=F32)


def _bf(v):
    return v.astype(BF16)


def _sigmoid(v):
    return 1.0 / (1.0 + jnp.exp(-v))


def _tile(n, want):
    if n <= want:
        return n
    for t in range(want - want % 128, 0, -128):
        if n % t == 0:
            return t
    t = want
    while n % t:
        t //= 2
    return t


def _my_position():
    x, y, c = lax.axis_index("x"), lax.axis_index("y"), lax.axis_index("c")
    return x, y, c, 4 * x + 2 * y + c


def _peer(x, y, c, k):
    px = 1 - x if k & 4 else x
    py = 1 - y if k & 2 else y
    pc = 1 - c if k & 1 else c
    return (px, py, pc), 4 * px + 2 * py + pc


DIRECT2 = (1, 2, 4, 6)
PASSED_ON = (2, 4, 6)
N_PUSHES = N_DEV - 1 + len(PASSED_ON)


def _exchange_sems(n):
    return [pltpu.SemaphoreType.DMA((n, N_PUSHES)), pltpu.SemaphoreType.DMA((n, N_PUSHES)),
            pltpu.SemaphoreType.DMA((n,))]


def _exchange_src(kind, src_ref, layer, block):
    if kind in ("gather", "gather2"):
        return src_ref if layer is None else src_ref.at[layer]
    return src_ref.at[block]


def _passed_on(dst_ref, sems, a, j, pos, incoming):
    x, y, c, _ = pos
    sibling, _ = _peer(x, y, c, 1)
    _, block = _peer(x, y, c, PASSED_ON[j] | 1 if incoming else PASSED_ON[j])
    return pltpu.make_async_remote_copy(src_ref=dst_ref.at[block], dst_ref=dst_ref.at[block],
                                        send_sem=sems[0].at[a, N_DEV - 1 + j], recv_sem=sems[1].at[a, N_DEV - 1 + j],
                                        device_id=sibling, device_id_type=MESH)


def _exchange_copy(kind, src_ref, layer, dst_ref, sems, a, k, pos, incoming):
    x, y, c, me = pos
    peer, peer_id = _peer(x, y, c, k)
    return pltpu.make_async_remote_copy(src_ref=_exchange_src(kind, src_ref, layer, me if incoming else peer_id),
                                        dst_ref=dst_ref.at[peer_id if incoming else me],
                                        send_sem=sems[0].at[a, k - 1], recv_sem=sems[1].at[a, k - 1],
                                        device_id=peer, device_id_type=MESH)


def _exchange_local(kind, src_ref, layer, dst_ref, sems, a, me):
    return pltpu.make_async_copy(_exchange_src(kind, src_ref, layer, me), dst_ref.at[me], sems[2].at[a])


def _exchange_start(kind, layers, src_refs, dst_refs, sems):
    pos = _my_position()
    direct = DIRECT2 if kind == "gather2" else range(1, N_DEV)
    for a, (src_ref, layer, dst_ref) in enumerate(zip(src_refs, layers, dst_refs)):
        _exchange_local(kind, src_ref, layer, dst_ref, sems, a, pos[3]).start()
        for k in direct:
            _exchange_copy(kind, src_ref, layer, dst_ref, sems, a, k, pos, False).start()


def _exchange_wait(kind, layers, src_refs, dst_refs, sems):
    pos = _my_position()
    for a, (src_ref, layer, dst_ref) in enumerate(zip(src_refs, layers, dst_refs)):
        if kind == "gather2":
            for j, k in enumerate(PASSED_ON):
                _exchange_copy(kind, src_ref, layer, dst_ref, sems, a, k, pos, True).wait_recv()
                _passed_on(dst_ref, sems, a, j, pos, False).start()
            _exchange_copy(kind, src_ref, layer, dst_ref, sems, a, 1, pos, True).wait_recv()
            for j in range(len(PASSED_ON)):
                _passed_on(dst_ref, sems, a, j, pos, True).wait_recv()
            for k in DIRECT2:
                _exchange_copy(kind, src_ref, layer, dst_ref, sems, a, k, pos, False).wait_send()
            for j in range(len(PASSED_ON)):
                _passed_on(dst_ref, sems, a, j, pos, False).wait_send()
        else:
            for k in range(1, N_DEV):
                _exchange_copy(kind, src_ref, layer, dst_ref, sems, a, k, pos, True).wait_recv()
            for k in range(1, N_DEV):
                _exchange_copy(kind, src_ref, layer, dst_ref, sems, a, k, pos, False).wait_send()
        _exchange_local(kind, src_ref, layer, dst_ref, sems, a, pos[3]).wait()


def _exchange_shapes(kind, srcs, layers):
    if kind in ("gather", "gather2"):
        return [jax.ShapeDtypeStruct((N_DEV,) + (s.shape if l is None else s.shape[1:]), s.dtype)
                for s, l in zip(srcs, layers)]
    return [jax.ShapeDtypeStruct(s.shape, s.dtype) for s in srcs]


def _exchange(kind, srcs, layers=None, *, name):
    n = len(srcs)
    layers = layers or [None] * n
    hbm = pl.BlockSpec(memory_space=pl.ANY)

    def body(*refs):
        src_refs, dst_refs, sems = refs[:n], refs[n:2 * n], refs[2 * n:]
        _exchange_start(kind, layers, src_refs, dst_refs, sems)
        _exchange_wait(kind, layers, src_refs, dst_refs, sems)

    return pl.pallas_call(body, name=name, in_specs=[hbm] * n, out_specs=[hbm] * n,
                          out_shape=_exchange_shapes(kind, srcs, layers), scratch_shapes=_exchange_sems(n))(*srcs)


def _pcall(body, args, *, name, grid, in_specs, out_specs, out_shape, scratch_shapes, sem, exchange=None):
    if exchange is None:
        outs = pl.pallas_call(body, name=name, grid=grid, in_specs=in_specs, out_specs=out_specs, out_shape=out_shape,
                              scratch_shapes=scratch_shapes, compiler_params=_cparams(*sem))(*args)
        return outs, None
    kind, srcs, layers = exchange
    n_in, n_out, n_scr, n_x = len(in_specs), len(out_specs), len(scratch_shapes), len(srcs)
    hbm = pl.BlockSpec(memory_space=pl.ANY)

    def carrier(*refs):
        ins, refs = refs[:n_in], refs[n_in:]
        src_refs, refs = refs[:n_x], refs[n_x:]
        outs, refs = refs[:n_out], refs[n_out:]
        dst_refs, refs = refs[:n_x], refs[n_x:]
        scr, sems = refs[:n_scr], refs[n_scr:]
        ids = [pl.program_id(a) for a in range(len(grid))]
        first = functools.reduce(jnp.logical_and, [i == 0 for i in ids])
        last = functools.reduce(jnp.logical_and, [i == g - 1 for i, g in zip(ids, grid)])

        @pl.when(first)
        def _():
            _exchange_start(kind, layers, src_refs, dst_refs, sems)

        body(*ins, *outs, *scr)

        @pl.when(last)
        def _():
            _exchange_wait(kind, layers, src_refs, dst_refs, sems)

    outs = pl.pallas_call(
        carrier, name=name, grid=grid, in_specs=list(in_specs) + [hbm] * n_x, out_specs=list(out_specs) + [hbm] * n_x,
        out_shape=list(out_shape) + _exchange_shapes(kind, srcs, layers),
        scratch_shapes=list(scratch_shapes) + _exchange_sems(n_x),
        compiler_params=_cparams(*(["arbitrary"] * len(grid))))(*args, *srcs)
    return outs[:n_out], outs[n_out:]


def _accumulate(acc_ref, part, k, nk, finish):
    if nk == 1:
        finish(part)
        return

    @pl.when(k == 0)
    def _():
        acc_ref[...] = part

    @pl.when(jnp.logical_and(k > 0, k < nk - 1))
    def _():
        acc_ref[...] += part

    @pl.when(k == nk - 1)
    def _():
        finish(acc_ref[...] + part)


def _matmul(a, b, *, dims, grid, a_spec, b_spec, o_spec, out_shape, acc_shape, name, scale=1.0,
            res=None, res_spec=None, exchange=None):
    nk = grid[3]
    has_res = res is not None

    def body(*refs):
        if has_res:
            a_ref, b_ref, r_ref, o_ref = refs[:4]
        else:
            a_ref, b_ref, o_ref = refs[:3]
            r_ref = None

        def finish(acc):
            if scale != 1.0:
                acc = acc * scale
            if r_ref is not None:
                acc = acc + r_ref[...].astype(F32)
            o_ref[...] = acc.astype(o_ref.dtype)

        part = lax.dot_general(_bf(a_ref[...]), _bf(b_ref[...]), (dims, ((), ())), preferred_element_type=F32)
        _accumulate(None if nk == 1 else refs[-1], part, pl.program_id(3), nk, finish)

    in_specs = [a_spec, b_spec] + ([res_spec] if has_res else [])
    args = (a, b) + ((res,) if has_res else ())
    outs, moved = _pcall(
        body, args, name=name, grid=grid, in_specs=in_specs, out_specs=[o_spec], out_shape=[out_shape],
        scratch_shapes=[] if nk == 1 else [pltpu.VMEM(acc_shape, F32)],
        sem=("parallel", "parallel", "parallel", "arbitrary"), exchange=exchange)
    return outs[0] if exchange is None else (outs[0], moved)


NT = ((1,), (1,))
NN = ((1,), (0,))
TN = ((0,), (0,))


def _mm_rows(a, b, *, dims, out_dtype, name, res=None, scale=1.0, tm=1024, tn=1024, tk=1024):
    m, kdim = a.shape
    n = b.shape[1] if dims == NN else b.shape[0]
    tm, tn, tk = _tile(m, tm), _tile(n, tn), _tile(kdim, tk)
    grid = (1, m // tm, n // tn, kdim // tk)
    a_spec = pl.BlockSpec((tm, tk), lambda s, i, j, k: (i, k))
    if dims == NN:
        b_spec = pl.BlockSpec((tk, tn), lambda s, i, j, k: (k, j))
    else:
        b_spec = pl.BlockSpec((tn, tk), lambda s, i, j, k: (j, k))
    o_spec = pl.BlockSpec((tm, tn), lambda s, i, j, k: (i, j))
    return _matmul(a, b, dims=dims, grid=grid, a_spec=a_spec, b_spec=b_spec, o_spec=o_spec,
                   out_shape=jax.ShapeDtypeStruct((m, n), out_dtype), acc_shape=(tm, tn), name=name,
                   res=res, res_spec=o_spec if res is not None else None, scale=scale)


def _mm_to_slices(a, b_t, *, name, tm=2048, exchange=None):
    m, kdim = a.shape
    n = b_t.shape[0]
    tm = _tile(m, tm)
    w = BRANCH_W
    grid = (1, m // tm, n // w, 1)
    return _matmul(a, b_t, dims=NT, grid=grid,
                   a_spec=pl.BlockSpec((tm, kdim), lambda s, i, j, k: (i, 0)),
                   b_spec=pl.BlockSpec((w, kdim), lambda s, i, j, k: (j, 0)),
                   o_spec=pl.BlockSpec((None, tm, w), lambda s, i, j, k: (j, i, 0)),
                   out_shape=jax.ShapeDtypeStruct((n // w, m, w), BF16), acc_shape=(tm, w), name=name,
                   exchange=exchange)


def _inproj_dh(parts, w_t, res, *, name, tm=512):
    t, n = res.shape
    w = BRANCH_W
    tm = _tile(t, tm)
    counts = [p.shape[0] for p in parts]
    assert sum(counts) * w == w_t.shape[0]

    def body(*refs):
        w_ref, r_ref, o_ref = refs[len(parts):]
        cols = jnp.concatenate([p_ref[s] for p_ref, cnt in zip(refs, counts) for s in range(cnt)], axis=1)
        o_ref[...] = r_ref[...] + _dot(cols, w_ref[...])

    row = pl.BlockSpec((tm, n), lambda i: (i, 0))
    return pl.pallas_call(
        body, name=name, grid=(t // tm,),
        in_specs=[pl.BlockSpec((cnt, tm, w), lambda i: (0, i, 0)) for cnt in counts]
        + [pl.BlockSpec(w_t.shape, lambda i: (0, 0)), row],
        out_specs=row, out_shape=jax.ShapeDtypeStruct((t, n), F32), compiler_params=_cparams("parallel"),
    )(*parts, w_t, res)


def _mm_tn(a, b, *, name, tm=1408, tn=1024, tk=TOKEN_TK, exchange=None):
    t, m = a.shape
    n = b.shape[1]
    tm, tn, tk = _tile(m, tm), _tile(n, tn), _tile(t, tk)
    grid = (1, m // tm, n // tn, t // tk)
    return _matmul(a, b, dims=TN, grid=grid,
                   a_spec=pl.BlockSpec((tk, tm), lambda s, i, j, k: (k, i)),
                   b_spec=pl.BlockSpec((tk, tn), lambda s, i, j, k: (k, j)),
                   o_spec=pl.BlockSpec((tm, tn), lambda s, i, j, k: (i, j)),
                   out_shape=jax.ShapeDtypeStruct((m, n), PARTIAL_DTYPE), acc_shape=(tm, tn), name=name,
                   exchange=exchange)


def _mm_tn_slices(a3, b, *, name, tn=1024, tk=TOKEN_TK):
    s_n, t, w = a3.shape
    n = b.shape[1]
    tn, tk = _tile(n, tn), _tile(t, tk)
    grid = (1, s_n, n // tn, t // tk)
    return _matmul(a3, b, dims=TN, grid=grid,
                   a_spec=pl.BlockSpec((None, tk, w), lambda s, i, j, k: (i, k, 0)),
                   b_spec=pl.BlockSpec((tk, tn), lambda s, i, j, k: (k, j)),
                   o_spec=pl.BlockSpec((w, tn), lambda s, i, j, k: (i, j)),
                   out_shape=jax.ShapeDtypeStruct((s_n * w, n), PARTIAL_DTYPE), acc_shape=(w, tn), name=name)


def _mm_tn_batch(a, b3, *, name, tm=1024, tn=1024, tk=TOKEN_TK):
    s_n, t, n = b3.shape
    m = a.shape[-1]
    tm, tn, tk = _tile(m, tm), _tile(n, tn), _tile(t, tk)
    grid = (s_n, m // tm, n // tn, t // tk)
    return _matmul(a, b3, dims=TN, grid=grid, a_spec=pl.BlockSpec((tk, tm), lambda s, i, j, k: (k, i)),
                   b_spec=pl.BlockSpec((None, tk, tn), lambda s, i, j, k: (s, k, j)),
                   o_spec=pl.BlockSpec((None, tm, tn), lambda s, i, j, k: (s, i, j)),
                   out_shape=jax.ShapeDtypeStruct((s_n, m, n), PARTIAL_DTYPE), acc_shape=(tm, tn), name=name)


def _norm_parts(xf):
    r = lax.rsqrt(jnp.mean(xf * xf, axis=-1, keepdims=True) + EPS)
    return xf * r, r


def _norm_bwd(dh, xhat, r, w):
    dxhat = dh * w
    dx = r * (dxhat - xhat * jnp.mean(dxhat * xhat, axis=-1, keepdims=True))
    return dx, jnp.sum(dh * xhat, axis=0, keepdims=True)


def _rmsnorm_fwd(x, w, *, name, tm=1024):
    t, d = x.shape
    tm = _tile(t, tm)

    def body(x_ref, w_ref, h_ref):
        xhat, _ = _norm_parts(x_ref[...])
        h_ref[...] = _bf(xhat * w_ref[...])

    return pl.pallas_call(
        body, name=name, grid=(t // tm,),
        in_specs=[pl.BlockSpec((tm, d), lambda i: (i, 0)), pl.BlockSpec((1, d), lambda i: (0, 0))],
        out_specs=pl.BlockSpec((tm, d), lambda i: (i, 0)),
        out_shape=jax.ShapeDtypeStruct((t, d), BF16), compiler_params=_cparams("parallel"),
    )(x, w)


def _rmsnorm_bwd(dh, x, w, dres, *, name, tm=1024):
    t, d = x.shape
    tm = _tile(t, tm)

    def body(dh_ref, x_ref, w_ref, dres_ref, dx_ref, dw_ref):
        xhat, r = _norm_parts(x_ref[...])
        dx, dw = _norm_bwd(dh_ref[...], xhat, r, w_ref[...])
        dx_ref[...] = dres_ref[...] + dx

        @pl.when(pl.program_id(0) == 0)
        def _():
            dw_ref[...] = dw

        @pl.when(pl.program_id(0) > 0)
        def _():
            dw_ref[...] += dw

    row = pl.BlockSpec((tm, d), lambda i: (i, 0))
    vec = pl.BlockSpec((1, d), lambda i: (0, 0))
    return pl.pallas_call(
        body, name=name, grid=(t // tm,), in_specs=[row, row, vec, row], out_specs=[row, vec],
        out_shape=[jax.ShapeDtypeStruct((t, d), F32), jax.ShapeDtypeStruct((1, d), F32)],
        compiler_params=_cparams("arbitrary"),
    )(dh, x, w, dres)


def _loss_fwd_bwd(x, w, target, *, name, tm=1024):
    t, d = x.shape
    tm = _tile(t, tm)

    def body(x_ref, w_ref, t_ref, dx_ref, dw_ref, loss_ref):
        xhat, r = _norm_parts(x_ref[...])
        wv = w_ref[...]
        err = xhat * wv - t_ref[...]
        dx, dw = _norm_bwd(err * (1.0 / d), xhat, r, wv)
        dx_ref[...] = dx
        part = jnp.full((1, 128), 0.5 / d, F32) * jnp.sum(err * err)

        @pl.when(pl.program_id(0) == 0)
        def _():
            dw_ref[...] = dw
            loss_ref[...] = part

        @pl.when(pl.program_id(0) > 0)
        def _():
            dw_ref[...] += dw
            loss_ref[...] += part

    row = pl.BlockSpec((tm, d), lambda i: (i, 0))
    vec = pl.BlockSpec((1, d), lambda i: (0, 0))
    return pl.pallas_call(
        body, name=name, grid=(t // tm,), in_specs=[row, vec, row],
        out_specs=[row, vec, pl.BlockSpec((1, 128), lambda i: (0, 0))],
        out_shape=[jax.ShapeDtypeStruct((t, d), F32), jax.ShapeDtypeStruct((1, d), F32),
                   jax.ShapeDtypeStruct((1, 128), F32)],
        compiler_params=_cparams("arbitrary"),
    )(x, w, target)


def _ffn_tiles(t, f):
    tf = f
    for cand in (1408, 1024, 512, 256, 128):
        if f % cand == 0:
            tf = cand
            break
    return _tile(t, 512), tf


def _ffn_fwd(x, nw, wg_t, wu_t, wd, *, name, exchange=None):
    t, d = x.shape
    f = wd.shape[0]
    tm, tf = _ffn_tiles(t, f)
    nf = f // tf

    def body(x_ref, nw_ref, wg_ref, wu_ref, wd_ref, xo_ref, h_ref, g_ref, u_ref, a_ref, hs_ref, acc_ref):
        j = pl.program_id(1)

        @pl.when(j == 0)
        def _():
            xhat, _ = _norm_parts(x_ref[...])
            hb = _bf(xhat * nw_ref[...])
            hs_ref[...] = hb
            h_ref[...] = hb

        hb = hs_ref[...]
        g = _dot_nt(hb, wg_ref[...])
        u = _dot_nt(hb, wu_ref[...])
        a = _bf(g * _sigmoid(g) * u)
        g_ref[...] = _bf(g)
        u_ref[...] = _bf(u)
        a_ref[...] = a
        part = _dot(a, wd_ref[...])

        def finish(total):
            xo_ref[...] = x_ref[...] + 0.5 * total

        _accumulate(acc_ref, part, j, nf, finish)

    row = pl.BlockSpec((tm, d), lambda i, j: (i, 0))
    wspec = pl.BlockSpec((tf, d), lambda i, j: (j, 0))
    hid = pl.BlockSpec((tm, tf), lambda i, j: (i, j))
    return _pcall(
        body, (x, nw, wg_t, wu_t, wd), name=name, grid=(t // tm, nf),
        in_specs=[row, pl.BlockSpec((1, d), lambda i, j: (0, 0)), wspec, wspec, wspec],
        out_specs=[row, row, hid, hid, hid],
        out_shape=[jax.ShapeDtypeStruct((t, d), F32), jax.ShapeDtypeStruct((t, d), BF16)]
        + [jax.ShapeDtypeStruct((t, f), BF16)] * 3,
        scratch_shapes=[pltpu.VMEM((tm, d), BF16), pltpu.VMEM((tm, d), F32)],
        sem=("parallel", "arbitrary"), exchange=exchange)


def _ffn_bwd(dxo, x, nw, g, u, wg_t, wu_t, wd, *, name, exchange=None):
    t, d = x.shape
    f = wd.shape[0]
    tm, tf = _ffn_tiles(t, f)
    nf = f // tf

    def body(dxo_ref, x_ref, nw_ref, g_ref, u_ref, wg_ref, wu_ref, wd_ref,
             dx_ref, dy_ref, dg_ref, du_ref, dnw_ref, dys_ref, acc_ref):
        i, j = pl.program_id(0), pl.program_id(1)

        @pl.when(j == 0)
        def _():
            dyb = _bf(0.5 * dxo_ref[...])
            dys_ref[...] = dyb
            dy_ref[...] = dyb

        da = _dot_nt(dys_ref[...], wd_ref[...])
        gv = g_ref[...].astype(F32)
        uv = u_ref[...].astype(F32)
        s = _sigmoid(gv)
        dg = _bf(da * uv * (s * (1.0 + gv * (1.0 - s))))
        du = _bf(da * (gv * s))
        dg_ref[...] = dg
        du_ref[...] = du
        part = _dot(dg, wg_ref[...]) + _dot(du, wu_ref[...])

        def finish(dh):
            xhat, r = _norm_parts(x_ref[...])
            dx, dw = _norm_bwd(dh, xhat, r, nw_ref[...])
            dx_ref[...] = dxo_ref[...] + dx

            @pl.when(i == 0)
            def _():
                dnw_ref[...] = dw

            @pl.when(i > 0)
            def _():
                dnw_ref[...] += dw

        _accumulate(acc_ref, part, j, nf, finish)

    row = pl.BlockSpec((tm, d), lambda i, j: (i, 0))
    vec = pl.BlockSpec((1, d), lambda i, j: (0, 0))
    wspec = pl.BlockSpec((tf, d), lambda i, j: (j, 0))
    hid = pl.BlockSpec((tm, tf), lambda i, j: (i, j))
    return _pcall(
        body, (dxo, x, nw, g, u, wg_t, wu_t, wd), name=name, grid=(t // tm, nf),
        in_specs=[row, row, vec, hid, hid, wspec, wspec, wspec],
        out_specs=[row, row, hid, hid, vec],
        out_shape=[jax.ShapeDtypeStruct((t, d), F32), jax.ShapeDtypeStruct((t, d), BF16),
                   jax.ShapeDtypeStruct((t, f), BF16), jax.ShapeDtypeStruct((t, f), BF16),
                   jax.ShapeDtypeStruct((1, d), F32)],
        scratch_shapes=[pltpu.VMEM((tm, d), BF16), pltpu.VMEM((tm, d), F32)],
        sem=("arbitrary", "arbitrary"), exchange=exchange)


def _shift_down(prev, cur, n):
    ext = jnp.concatenate([prev, cur], axis=0)
    return pltpu.roll(ext, n, axis=0)[prev.shape[0]:]


def _shift_up(cur, nxt, n):
    ext = jnp.concatenate([cur, nxt], axis=0)
    return pltpu.roll(ext, ext.shape[0] - n, axis=0)[:cur.shape[0]]


def _conv_specs(t, tb):
    hb = tb // CONV_HALO
    last = t // CONV_HALO - 1

    def tile(s):
        return pl.BlockSpec((None, tb, 128), lambda c, i: (s, i, c))

    def prev(s):
        return pl.BlockSpec((None, CONV_HALO, 128), lambda c, i: (s, jnp.maximum(i * hb - 1, 0), c))

    def nxt(s):
        return pl.BlockSpec((None, CONV_HALO, 128), lambda c, i: (s, jnp.minimum((i + 1) * hb, last), c))

    return tile, prev, nxt


def _conv_fwd(cols3, conv_w, *, name, tb=1024):
    _, t, bw = cols3.shape
    tb = _tile(t, tb)
    tile, prev, _ = _conv_specs(t, tb)

    def body(u_ref, b_ref, c_ref, up_ref, cp_ref, w_ref, y_ref):
        first = pl.program_id(1) == 0
        z = c_ref[...].astype(F32) * u_ref[...].astype(F32)
        zp = jnp.where(first, 0.0, cp_ref[...].astype(F32) * up_ref[...].astype(F32))
        conv = w_ref[0:1, :] * _shift_down(zp, z, 2) + w_ref[1:2, :] * _shift_down(zp, z, 1) + w_ref[2:3, :] * z
        y_ref[...] = _bf(b_ref[...].astype(F32) * conv)

    return pl.pallas_call(
        body, name=name, grid=(bw // 128, t // tb),
        in_specs=[tile(0), tile(1), tile(2), prev(0), prev(2), pl.BlockSpec((3, 128), lambda c, i: (0, c))],
        out_specs=pl.BlockSpec((tb, 128), lambda c, i: (i, c)),
        out_shape=jax.ShapeDtypeStruct((t, bw), BF16), compiler_params=_cparams("parallel", "parallel"),
    )(cols3, cols3, cols3, cols3, cols3, conv_w)


def _conv_bwd(cols3, conv_w, dy, *, name, tb=1024):
    _, t, bw = cols3.shape
    tb = _tile(t, tb)
    nt = t // tb
    tile, prev, nxt = _conv_specs(t, tb)
    hb = tb // CONV_HALO
    last = t // CONV_HALO - 1

    def body(u_ref, b_ref, c_ref, up_ref, cp_ref, bn_ref, dy_ref, dyn_ref, w_ref, d3_ref, dw_ref):
        i = pl.program_id(1)
        uv, bv, cv = u_ref[...].astype(F32), b_ref[...].astype(F32), c_ref[...].astype(F32)
        dyv = dy_ref[...].astype(F32)
        z = cv * uv
        zp = jnp.where(i == 0, 0.0, cp_ref[...].astype(F32) * up_ref[...].astype(F32))
        z1, z2 = _shift_down(zp, z, 1), _shift_down(zp, z, 2)
        w0, w1, w2 = w_ref[0:1, :], w_ref[1:2, :], w_ref[2:3, :]
        conv = w0 * z2 + w1 * z1 + w2 * z
        dconv = dyv * bv
        dconv_n = jnp.where(i == nt - 1, 0.0, dyn_ref[...].astype(F32) * bn_ref[...].astype(F32))
        dz = w2 * dconv + w1 * _shift_up(dconv, dconv_n, 1) + w0 * _shift_up(dconv, dconv_n, 2)
        d3_ref[0] = _bf(dz * cv)
        d3_ref[1] = _bf(dyv * conv)
        d3_ref[2] = _bf(dz * uv)
        dws = [jnp.sum(dconv * zz, axis=0, keepdims=True) for zz in (z2, z1, z)]

        @pl.when(i == 0)
        def _():
            for j in range(3):
                dw_ref[j:j + 1, :] = dws[j]

        @pl.when(i > 0)
        def _():
            for j in range(3):
                dw_ref[j:j + 1, :] += dws[j]

    dy_tile = pl.BlockSpec((None, tb, 128), lambda c, i: (0, i, c))
    dy_next = pl.BlockSpec((None, CONV_HALO, 128), lambda c, i: (0, jnp.minimum((i + 1) * hb, last), c))
    wspec = pl.BlockSpec((3, 128), lambda c, i: (0, c))
    return pl.pallas_call(
        body, name=name, grid=(bw // 128, nt),
        in_specs=[tile(0), tile(1), tile(2), prev(0), prev(2), nxt(1), dy_tile, dy_next, wspec],
        out_specs=[pl.BlockSpec((3, tb, 128), lambda c, i: (0, i, c)), wspec],
        out_shape=[jax.ShapeDtypeStruct((3, t, bw), BF16), jax.ShapeDtypeStruct((3, bw), F32)],
        compiler_params=_cparams("parallel", "arbitrary"),
    )(cols3, cols3, cols3, cols3, cols3, cols3, dy, dy, conv_w)


def _ret_consts():
    log_gamma = jnp.log1p(-jnp.exp2(-5.0 - jnp.arange(H_RET, dtype=F32)))
    pos = jnp.arange(CHUNK, dtype=F32)
    d_intra = jnp.exp(log_gamma[:, None, None] * jnp.abs(pos[:, None] - pos[None, :]))
    q_decay = jnp.exp(log_gamma[:, None] * (pos + 1.0))
    k_decay = jnp.exp(log_gamma[:, None] * (CHUNK - 1.0 - pos))
    chunk_decay = jnp.exp(log_gamma * CHUNK)
    wide = (H_RET, CHUNK, DK_RET)
    return (d_intra, jnp.broadcast_to(q_decay[:, :, None], wide), jnp.broadcast_to(k_decay[:, :, None], wide),
            jnp.broadcast_to(chunk_decay[:, None, None], (H_RET, 1, DK_RET)))


def _rope_tables(t):
    inv_freq = ROPE_BASE ** (-jnp.linspace(0.0, 1.0, DK_RET // 2, dtype=F32))
    ang = jnp.arange(t, dtype=F32)[:, None] * inv_freq[None, :]
    cos, sin = jnp.cos(ang), jnp.sin(ang)
    return jnp.concatenate([cos, cos], axis=1), jnp.concatenate([-sin, sin], axis=1)


def _rope(v, cc, ss):
    return v * cc + pltpu.roll(v, DK_RET // 2, axis=1) * ss


def _ret_in_specs(tb, blk):
    def col(s):
        return pl.BlockSpec((None, tb, H_RET * DK_RET), lambda i: (s, blk(i), 0))

    tab = pl.BlockSpec((tb, DK_RET), lambda i: (blk(i), 0))
    return ([col(3), col(4), col(5), col(6), tab, tab,
             pl.BlockSpec((H_RET, CHUNK, CHUNK), lambda i: (0, 0, 0)),
             pl.BlockSpec((H_RET, CHUNK, DK_RET), lambda i: (0, 0, 0)),
             pl.BlockSpec((H_RET, CHUNK, DK_RET), lambda i: (0, 0, 0)),
             pl.BlockSpec((H_RET, 1, DK_RET), lambda i: (0, 0, 0))])


def _ret_fwd(cols3, tables, consts, *, name):
    _, t, bw = cols3.shape
    tb = _tile(t, RET_TB)
    ncb = tb // CHUNK
    scale = DK_RET ** -0.5

    def body(q_ref, k_ref, v_ref, g_ref, cc_ref, ss_ref, di_ref, qd_ref, kd_ref, cd_ref, y_ref, st_ref, state):
        @pl.when(pl.program_id(0) == 0)
        def _():
            state[...] = jnp.zeros_like(state)

        for c in range(ncb):
            rows = pl.ds(c * CHUNK, CHUNK)
            cc, ss = cc_ref[rows, :], ss_ref[rows, :]
            for h in range(H_RET):
                lanes = pl.ds(h * DK_RET, DK_RET)
                qs = _rope(q_ref[rows, lanes].astype(F32), cc, ss) * scale
                kr = _rope(k_ref[rows, lanes].astype(F32), cc, ss)
                vb = v_ref[rows, lanes]
                gv = g_ref[rows, lanes].astype(F32)
                s_in = state[h]
                st_ref[h, c] = s_in
                inner = _dot_nt(_bf(qs), _bf(kr)) * di_ref[h]
                o = _dot(_bf(inner), vb) + _dot(_bf(qs * qd_ref[h]), _bf(s_in))
                state[h] = s_in * cd_ref[h] + _dot_tn(_bf(kr * kd_ref[h]), vb)
                on = o * lax.rsqrt(jnp.mean(o * o, axis=-1, keepdims=True) + EPS)
                y_ref[rows, lanes] = _bf(gv * _sigmoid(gv) * on)

    return pl.pallas_call(
        body, name=name, grid=(t // tb,),
        in_specs=_ret_in_specs(tb, lambda i: i),
        out_specs=[pl.BlockSpec((tb, bw), lambda i: (i, 0)),
                   pl.BlockSpec((H_RET, ncb, DK_RET, DK_RET), lambda i: (0, i, 0, 0))],
        out_shape=[jax.ShapeDtypeStruct((t, bw), BF16),
                   jax.ShapeDtypeStruct((H_RET, t // CHUNK, DK_RET, DK_RET), F32)],
        scratch_shapes=[pltpu.VMEM((H_RET, DK_RET, DK_RET), F32)],
        compiler_params=_cparams("arbitrary"),
    )(cols3, cols3, cols3, cols3, tables[0], tables[1], *consts)


def _ret_bwd(cols3, tables, consts, states, dy, *, name, exchange=None):
    _, t, bw = cols3.shape
    tb = _tile(t, RET_TB)
    ncb = tb // CHUNK
    nb = t // tb
    scale = DK_RET ** -0.5

    def body(q_ref, k_ref, v_ref, g_ref, cc_ref, ss_ref, di_ref, qd_ref, kd_ref, cd_ref, st_ref, dy_ref,
             d4_ref, dstate):
        @pl.when(pl.program_id(0) == 0)
        def _():
            dstate[...] = jnp.zeros_like(dstate)

        heads = range(H_RET)
        lanes_of = [pl.ds(h * DK_RET, DK_RET) for h in heads]
        di, qd, kd = [di_ref[h] for h in heads], [qd_ref[h] for h in heads], [kd_ref[h] for h in heads]
        carried = [dstate[h] for h in heads]
        group = RET_GROUP if ncb % RET_GROUP == 0 else 1
        for c0 in reversed(range(0, ncb, group)):
            units = [(c, h) for c in reversed(range(c0, c0 + group)) for h in heads]
            rows = [pl.ds(c * CHUNK, CHUNK) for c, _ in units]
            cc, ss = [cc_ref[rw, :] for rw in rows], [ss_ref[rw, :] for rw in rows]
            every = range(len(units))
            hd = [h for _, h in units]
            vb = [v_ref[rows[u], lanes_of[hd[u]]] for u in every]
            gv = [g_ref[rows[u], lanes_of[hd[u]]].astype(F32) for u in every]
            dyv = [dy_ref[rows[u], lanes_of[hd[u]]].astype(F32) for u in every]
            s_in = [_bf(st_ref[h, c]) for c, h in units]
            qs = [_rope(q_ref[rows[u], lanes_of[hd[u]]].astype(F32), cc[u], ss[u]) * scale for u in every]
            kr = [_rope(k_ref[rows[u], lanes_of[hd[u]]].astype(F32), cc[u], ss[u]) for u in every]
            qsb, krb = [_bf(v) for v in qs], [_bf(v) for v in kr]
            qdb = [_bf(qs[u] * qd[hd[u]]) for u in every]
            kdb = [_bf(kr[u] * kd[hd[u]]) for u in every]
            innerb = [_bf(_dot_nt(qsb[u], krb[u]) * di[hd[u]]) for u in every]
            o = [_dot(innerb[u], vb[u]) + _dot(qdb[u], s_in[u]) for u in every]
            r = [lax.rsqrt(jnp.mean(v * v, axis=-1, keepdims=True) + EPS) for v in o]
            on = [o[u] * r[u] for u in every]
            sg = [_sigmoid(v) for v in gv]
            dgv = [_bf(dyv[u] * on[u] * (sg[u] * (1.0 + gv[u] * (1.0 - sg[u])))) for u in every]
            don = [dyv[u] * (gv[u] * sg[u]) for u in every]
            dob = [_bf(r[u] * (don[u] - on[u] * jnp.mean(don[u] * on[u], axis=-1, keepdims=True))) for u in every]
            dinner = [_bf(_dot_nt(dob[u], vb[u]) * di[hd[u]]) for u in every]
            dqs = [_dot(dinner[u], krb[u]) + _dot_nt(dob[u], s_in[u]) * qd[hd[u]] for u in every]
            dkr = [_dot_tn(dinner[u], qsb[u]) for u in every]
            dv = [_dot_tn(innerb[u], dob[u]) for u in every]
            dnew = [_dot_tn(qdb[u], dob[u]) for u in every]
            for u in every:
                h = hd[u]
                dstb = _bf(carried[h])
                dv[u] = dv[u] + _dot(kdb[u], dstb)
                dkr[u] = dkr[u] + _dot_nt(vb[u], dstb) * kd[h]
                carried[h] = carried[h] * cd_ref[h] + dnew[u]
            for u in every:
                ln = lanes_of[hd[u]]
                d4_ref[0, rows[u], ln] = _bf(_rope_bwd(dqs[u] * scale, cc[u], ss[u]))
                d4_ref[1, rows[u], ln] = _bf(_rope_bwd(dkr[u], cc[u], ss[u]))
                d4_ref[2, rows[u], ln] = _bf(dv[u])
                d4_ref[3, rows[u], ln] = dgv[u]
        for h in heads:
            dstate[h] = carried[h]

    rev = lambda i: nb - 1 - i
    outs, moved = _pcall(
        body, (cols3, cols3, cols3, cols3, tables[0], tables[1], *consts, states, dy), name=name, grid=(nb,),
        in_specs=_ret_in_specs(tb, rev)
        + [pl.BlockSpec((H_RET, ncb, DK_RET, DK_RET), lambda i: (0, rev(i), 0, 0)),
           pl.BlockSpec((None, tb, bw), lambda i: (1, rev(i), 0))],
        out_specs=[pl.BlockSpec((4, tb, bw), lambda i: (0, rev(i), 0))],
        out_shape=[jax.ShapeDtypeStruct((4, t, bw), BF16)],
        scratch_shapes=[pltpu.VMEM((H_RET, DK_RET, DK_RET), F32)],
        sem=("arbitrary",), exchange=exchange)
    return outs[0], moved


def _rope_bwd(dv, cc, ss):
    return dv * cc + pltpu.roll(dv * ss, DK_RET // 2, axis=1)


def _att_window(i):
    return pl.multiple_of(jnp.maximum(i - ATT_LOOKBACK // ATT_QB, 0) * ATT_QB, ATT_QB)


def _att_mask(v):
    qchunk = (v * ATT_QB + lax.broadcasted_iota(jnp.int32, (ATT_QB, ATT_WIN), 0)) // CHUNK
    kchunk = lax.broadcasted_iota(jnp.int32, (ATT_QB, ATT_WIN), 1) // CHUNK
    return (kchunk <= qchunk) & (kchunk >= qchunk - N_PREV_CHUNKS)


def _bias_spec(layer, nvar):
    return pl.BlockSpec((None, None, 2, ATT_QB, ATT_WIN), lambda hp, i: (layer, jnp.minimum(i, nvar), hp, 0, 0))


def _att_fwd(cols3, bias3, *, name, exchange=None):
    _, t, bw = cols3.shape
    assert t % ATT_QB == 0 and t >= ATT_WIN
    scale = DH_ATT ** -0.5
    nvar = ATT_LOOKBACK // ATT_QB

    def body(q_ref, k_ref, v_ref, b_ref, y_ref, lse_ref):
        i = pl.program_id(1)
        ws = _att_window(i)
        q = q_ref[...].astype(F32)
        kw = k_ref[pl.ds(ws, ATT_WIN), :]
        vw = v_ref[pl.ds(ws, ATT_WIN), :]
        lane_head = lax.broadcasted_iota(jnp.int32, (ATT_QB, 128), 1) // DH_ATT
        out = jnp.zeros((ATT_QB, 128), F32)
        lse = jnp.zeros((ATT_QB, 128), F32)
        for hh in range(2):
            mine = lane_head == hh
            s = _dot_nt(_bf(jnp.where(mine, q, 0.0)), kw) * scale + b_ref[hh]
            mx = jnp.max(s, axis=-1, keepdims=True)
            p = jnp.exp(s - mx)
            l = jnp.sum(p, axis=-1, keepdims=True)
            out = jnp.where(mine, _dot(_bf(p), vw) / l, out)
            lse = jnp.where(mine, mx + jnp.log(l), lse)
        y_ref[...] = _bf(out)
        lse_ref[...] = lse

    kv = lambda s: pl.BlockSpec((None, t, 128), lambda hp, i: (s, 0, hp))
    return _pcall(
        body, (cols3, cols3, cols3, bias3[0]), name=name, grid=(H_ATT // 2, t // ATT_QB),
        in_specs=[pl.BlockSpec((None, ATT_QB, 128), lambda hp, i: (7, i, hp)), kv(8), kv(9),
                  _bias_spec(bias3[1], nvar)],
        out_specs=[pl.BlockSpec((ATT_QB, 128), lambda hp, i: (i, hp)),
                   pl.BlockSpec((None, ATT_QB, 128), lambda hp, i: (hp, i, 0))],
        out_shape=[jax.ShapeDtypeStruct((t, bw), BF16), jax.ShapeDtypeStruct((H_ATT // 2, t, 128), F32)],
        scratch_shapes=[], sem=("parallel", "arbitrary"), exchange=exchange)


def _att_bwd(cols3, bias3, y, lse, dy, *, name, exchange=None):
    _, t, bw = cols3.shape
    nq = t // ATT_QB
    scale = DH_ATT ** -0.5
    nvar = ATT_LOOKBACK // ATT_QB

    def body(q_ref, k_ref, v_ref, b_ref, y_ref, lse_ref, dy_ref, d3_ref, db_ref, dk_acc, dv_acc):
        i = pl.program_id(1)

        @pl.when(i == 0)
        def _():
            dk_acc[...] = jnp.zeros_like(dk_acc)
            dv_acc[...] = jnp.zeros_like(dv_acc)

        ws = _att_window(i)
        q = q_ref[...].astype(F32)
        kw = k_ref[pl.ds(ws, ATT_WIN), :]
        vw = v_ref[pl.ds(ws, ATT_WIN), :]
        do = dy_ref[...].astype(F32)
        dof = do * y_ref[...].astype(F32)
        lsev = lse_ref[...]
        lane_head = lax.broadcasted_iota(jnp.int32, (ATT_QB, 128), 1) // DH_ATT
        dq = jnp.zeros((ATT_QB, 128), F32)
        dk = jnp.zeros((ATT_WIN, 128), F32)
        dv = jnp.zeros((ATT_WIN, 128), F32)
        first = i <= nvar
        for hh in range(2):
            mine = lane_head == hh
            qh = _bf(jnp.where(mine, q, 0.0))
            doh = _bf(jnp.where(mine, do, 0.0))
            s = _dot_nt(qh, kw) * scale + b_ref[hh]
            lse_h = jnp.max(jnp.where(mine, lsev, NEG_INF), axis=-1, keepdims=True)
            p = jnp.exp(s - lse_h)
            delta = jnp.sum(jnp.where(mine, dof, 0.0), axis=-1, keepdims=True)
            ds = p * (_dot_nt(doh, vw) - delta)

            @pl.when(first)
            def _():
                db_ref[hh] = ds

            @pl.when(jnp.logical_not(first))
            def _():
                db_ref[hh] += ds

            dsb = _bf(ds * scale)
            dq = jnp.where(mine, _dot(dsb, kw), dq)
            dk = dk + _dot_tn(dsb, qh)
            dv = dv + _dot_tn(_bf(p), doh)
        d3_ref[0, pl.ds(pl.multiple_of(i * ATT_QB, ATT_QB), ATT_QB), :] = _bf(dq)
        dk_acc[pl.ds(ws, ATT_WIN), :] += dk
        dv_acc[pl.ds(ws, ATT_WIN), :] += dv

        @pl.when(i == nq - 1)
        def _():
            d3_ref[1] = _bf(dk_acc[...])
            d3_ref[2] = _bf(dv_acc[...])

    kv = lambda s: pl.BlockSpec((None, t, 128), lambda hp, i: (s, 0, hp))
    qrow = pl.BlockSpec((ATT_QB, 128), lambda hp, i: (i, hp))
    btile = pl.BlockSpec((None, 2, ATT_QB, ATT_WIN), lambda hp, i: (jnp.minimum(i, nvar), hp, 0, 0))
    return _pcall(
        body, (cols3, cols3, cols3, bias3[0], y, lse, dy), name=name, grid=(H_ATT // 2, nq),
        in_specs=[pl.BlockSpec((None, ATT_QB, 128), lambda hp, i: (7, i, hp)), kv(8), kv(9),
                  _bias_spec(bias3[1], nvar), qrow,
                  pl.BlockSpec((None, ATT_QB, 128), lambda hp, i: (hp, i, 0)),
                  pl.BlockSpec((None, ATT_QB, 128), lambda hp, i: (2, i, hp))],
        out_specs=[pl.BlockSpec((3, t, 128), lambda hp, i: (0, 0, hp)), btile],
        out_shape=[jax.ShapeDtypeStruct((3, t, bw), BF16),
                   jax.ShapeDtypeStruct((nvar + 1, H_ATT, ATT_QB, ATT_WIN), F32)],
        scratch_shapes=[pltpu.VMEM((t, 128), F32), pltpu.VMEM((t, 128), F32)],
        sem=("parallel", "arbitrary"), exchange=exchange)


SKEW_W = ATT_WIN + ATT_QB
REL_PAD = 384


def _rel_onehot(v):
    r = lax.broadcasted_iota(jnp.int32, (REL_PAD, SKEW_W), 0)
    j = lax.broadcasted_iota(jnp.int32, (REL_PAD, SKEW_W), 1)
    dist = jnp.where(j < ATT_WIN, v * ATT_QB - j, v * ATT_QB + SKEW_W - j)
    col = jnp.clip(dist, -REL_CLIP, REL_CLIP) + REL_CLIP
    return _bf(jnp.where(col == r, 1.0, 0.0))


def _split3(v):
    hi = _bf(v)
    rest = v - hi.astype(F32)
    mid = _bf(rest)
    return hi, mid, _bf(rest - mid.astype(F32))


def _skew8(a, forward):
    row = lax.broadcasted_iota(jnp.int32, a.shape, 0)
    for b in range(3):
        shift = (1 << b) if forward else SKEW_W - (1 << b)
        a = jnp.where(((row >> b) & 1) == 1, pltpu.roll(a, shift, axis=1), a)
    return a


def _toeplitz_rows(ext_row):
    a = _skew8(jnp.broadcast_to(ext_row, (8, SKEW_W)), True)
    while a.shape[0] < ATT_QB:
        a = jnp.concatenate([a, pltpu.roll(a, a.shape[0], axis=1)], axis=0)
    return a


def _diagonal_sums(tile):
    a = tile
    while a.shape[0] > 8:
        half = a.shape[0] // 2
        a = a[:half] + pltpu.roll(a[half:], SKEW_W - half, axis=1)
    return jnp.sum(_skew8(a, False), axis=0, keepdims=True)


def _bias_tiles(rel_bias, *, name, exchange=None):
    depth = rel_bias.shape[0]
    nvar = ATT_LOOKBACK // ATT_QB + 1
    rel = jnp.pad(rel_bias, ((0, 0), (0, 0), (0, REL_PAD - N_REL)))

    def body(rel_ref, o_ref, ext_ref):
        v = pl.program_id(1)
        onehot = _rel_onehot(v)
        ext_ref[...] = sum(_dot(part, onehot) for part in _split3(rel_ref[...]))
        valid = _att_mask(v)
        for h in range(H_ATT):
            o_ref[h] = jnp.where(valid, _toeplitz_rows(ext_ref[h:h + 1, :])[:, :ATT_WIN], NEG_INF)

    outs, moved = _pcall(
        body, (rel,), name=name, grid=(depth, nvar),
        in_specs=[pl.BlockSpec((None, H_ATT, REL_PAD), lambda l, v: (l, 0, 0))],
        out_specs=[pl.BlockSpec((None, None, H_ATT, ATT_QB, ATT_WIN), lambda l, v: (l, v, 0, 0, 0))],
        out_shape=[jax.ShapeDtypeStruct((depth, nvar, H_ATT, ATT_QB, ATT_WIN), F32)],
        scratch_shapes=[pltpu.VMEM((H_ATT, SKEW_W), F32)], sem=("parallel", "parallel"), exchange=exchange)
    return outs[0], moved


def _rel_bias_grad(dbias3, *, name):
    nvar = dbias3.shape[0]

    def body(db_ref, o_ref, diag_ref):
        v = pl.program_id(0)
        for h in range(H_ATT):
            tile = jnp.concatenate([db_ref[h], jnp.zeros((ATT_QB, ATT_QB), F32)], axis=1)
            diag_ref[h:h + 1, :] = _diagonal_sums(tile)
        onehot = _rel_onehot(v)
        part = sum(_dot_nt(p, onehot) for p in _split3(diag_ref[...]))

        @pl.when(v == 0)
        def _():
            o_ref[...] = part

        @pl.when(v > 0)
        def _():
            o_ref[...] += part

    out = pl.pallas_call(
        body, name=name, grid=(nvar,),
        in_specs=[pl.BlockSpec((None, H_ATT, ATT_QB, ATT_WIN), lambda v: (v, 0, 0, 0))],
        out_specs=pl.BlockSpec((H_ATT, REL_PAD), lambda v: (0, 0)),
        out_shape=jax.ShapeDtypeStruct((H_ATT, REL_PAD), F32),
        scratch_shapes=[pltpu.VMEM((H_ATT, SKEW_W), F32)], compiler_params=_cparams("arbitrary"),
    )(dbias3)
    return out[:, :N_REL]


def _merge_fwd(h, ys, wmg, wb_t, *, name, tm=512):
    t, d = h.shape
    bw = ys[0].shape[1]
    tm = _tile(t, tm)

    def body(h_ref, y0_ref, y1_ref, y2_ref, wg_ref, wb_ref, m_ref, s_ref, p_ref):
        hv = h_ref[...]
        total = jnp.zeros((tm, d), F32)
        for b, y_ref in enumerate((y0_ref, y1_ref, y2_ref)):
            s = _sigmoid(_dot(hv, wg_ref[b]))
            p = _dot_nt(y_ref[...], wb_ref[b])
            s_ref[b] = _bf(s)
            p_ref[b] = _bf(p)
            total = total + s * p
        m_ref[...] = _bf(total)

    row = pl.BlockSpec((tm, d), lambda i: (i, 0))
    yrow = pl.BlockSpec((tm, bw), lambda i: (i, 0))
    three = pl.BlockSpec((3, tm, d), lambda i: (0, i, 0))
    return pl.pallas_call(
        body, name=name, grid=(t // tm,),
        in_specs=[row, yrow, yrow, yrow, pl.BlockSpec((3, d, d), lambda i: (0, 0, 0)),
                  pl.BlockSpec((3, d, bw), lambda i: (0, 0, 0))],
        out_specs=[row, three, three],
        out_shape=[jax.ShapeDtypeStruct((t, d), BF16), jax.ShapeDtypeStruct((3, t, d), BF16),
                   jax.ShapeDtypeStruct((3, t, d), BF16)],
        compiler_params=_cparams("parallel"),
    )(h, *ys, wmg, wb_t)


def _merge_dwb(dp3, ys, *, name, tm=1024, tk=TOKEN_TK):
    nb, t, m = dp3.shape
    n = ys[0].shape[1]
    tm, tk = _tile(m, tm), _tile(t, tk)
    nk = t // tk

    def body(a_ref, *refs):
        o_ref, acc_ref = refs[nb], refs[nb + 1]
        k = pl.program_id(2)

        def finish(total):
            o_ref[...] = total.astype(o_ref.dtype)

        for b in range(nb):
            @pl.when(pl.program_id(0) == b)
            def _(y_ref=refs[b]):
                _accumulate(acc_ref, _dot_tn(a_ref[...], y_ref[...]), k, nk, finish)

    def y_spec(b):
        return pl.BlockSpec((tk, n), lambda s, i, k: (jnp.where(s == b, k, jnp.where(s < b, 0, nk - 1)), 0))

    return pl.pallas_call(
        body, name=name, grid=(nb, m // tm, nk),
        in_specs=[pl.BlockSpec((None, tk, tm), lambda s, i, k: (s, k, i))] + [y_spec(b) for b in range(nb)],
        out_specs=pl.BlockSpec((None, tm, n), lambda s, i, k: (s, i, 0)),
        out_shape=jax.ShapeDtypeStruct((nb, m, n), PARTIAL_DTYPE), scratch_shapes=[pltpu.VMEM((tm, n), F32)],
        compiler_params=_cparams("arbitrary", "arbitrary", "arbitrary"),
    )(dp3, *ys)


def _merge_bwd(dm, s3, p3, wmg, wb_t, *, name, tm=512):
    _, t, d = s3.shape
    bw = wb_t.shape[2]
    tm = _tile(t, tm)

    def body(dm_ref, s_ref, p_ref, wg_ref, wb_ref, dgp_ref, dp_ref, dy_ref, dh_ref):
        dmv = dm_ref[...].astype(F32)
        dh = jnp.zeros((tm, d), F32)
        for b in range(3):
            s = s_ref[b].astype(F32)
            dgp = _bf(dmv * p_ref[b].astype(F32) * s * (1.0 - s))
            dp = _bf(dmv * s)
            dgp_ref[b] = dgp
            dp_ref[b] = dp
            dy_ref[b] = _bf(_dot(dp, wb_ref[b]))
            dh = dh + _dot_nt(dgp, wg_ref[b])
        dh_ref[...] = dh

    row = pl.BlockSpec((tm, d), lambda i: (i, 0))
    three = pl.BlockSpec((3, tm, d), lambda i: (0, i, 0))
    return pl.pallas_call(
        body, name=name, grid=(t // tm,),
        in_specs=[row, three, three, pl.BlockSpec((3, d, d), lambda i: (0, 0, 0)),
                  pl.BlockSpec((3, d, bw), lambda i: (0, 0, 0))],
        out_specs=[three, three, pl.BlockSpec((3, tm, bw), lambda i: (0, i, 0)), row],
        out_shape=[jax.ShapeDtypeStruct((3, t, d), BF16), jax.ShapeDtypeStruct((3, t, d), BF16),
                   jax.ShapeDtypeStruct((3, t, bw), BF16), jax.ShapeDtypeStruct((t, d), F32)],
        compiler_params=_cparams("parallel"),
    )(dm, s3, p3, wmg, wb_t)


def _carried(hooks, slot, state=None):
    if not hooks or slot not in hooks:
        return None, lambda buf: None
    make, done = hooks[slot]
    return make(state), done


def _layer_fwd(x, p, aux, tag, hooks=None):
    ex, done = _carried(hooks, "ffn1")
    (x1, h1, g1, u1, a1), buf = _ffn_fwd(x, p["n1"], p["wg1"], p["wu1"], p["wd1"], name=f"ffn1_fwd_{tag}", exchange=ex)
    done(buf)
    h2 = _rmsnorm_fwd(x1, p["nmix"], name=f"mixnorm_fwd_{tag}")
    ex, done = _carried(hooks, "inproj")
    if ex is None:
        cols3 = _mm_to_slices(h2, p["win"], name=f"inproj_fwd_{tag}")
    else:
        cols3, buf = _mm_to_slices(h2, p["win"], name=f"inproj_fwd_{tag}", exchange=ex)
        done(buf)
    bias3 = p["bias3"]
    y_conv = _conv_fwd(cols3, p["conv_w"], name=f"conv_fwd_{tag}")
    y_ret, states = _ret_fwd(cols3, aux["rope"], aux["ret"], name=f"ret_fwd_{tag}")
    ex, done = _carried(hooks, "att")
    (y_att, lse), buf = _att_fwd(cols3, bias3, name=f"att_fwd_{tag}", exchange=ex)
    done(buf)
    merged, s3, p3 = _merge_fwd(h2, (y_conv, y_ret, y_att), p["wmg"], p["wb"], name=f"merge_fwd_{tag}")
    x2 = _mm_rows(merged, p["wout"], dims=NN, out_dtype=F32, res=x1, name=f"outproj_fwd_{tag}")
    ex, done = _carried(hooks, "ffn2")
    (x3, h3, g2, u2, a2), buf = _ffn_fwd(x2, p["n2"], p["wg2"], p["wu2"], p["wd2"], name=f"ffn2_fwd_{tag}", exchange=ex)
    done(buf)
    saved = dict(x0=x, h1=h1, g1=g1, u1=u1, a1=a1, x1=x1, h2=h2, cols3=cols3, bias3=bias3, y_conv=y_conv,
                 y_ret=y_ret, states=states, y_att=y_att, lse=lse, merged=merged, s3=s3, p3=p3, x2=x2, h3=h3,
                 g2=g2, u2=u2, a2=a2)
    return x3, saved


def _ffn_grads(dxo, x, h, g, u, a, p, which, tag, grads, all_grads, hooks):
    n = which
    ex, done = _carried(hooks, f"ffn{n}_bwd", all_grads)
    (dx, dyb, dg, du, grads["n" + n]), buf = _ffn_bwd(dxo, x, p["n" + n], g, u, p["wg" + n], p["wu" + n], p["wd" + n],
                                                     name=f"ffn{n}_{tag}_bwd", exchange=ex)
    done(buf)
    grads["wd" + n] = _mm_tn(a, dyb, name=f"ffn{n}_{tag}_dwd")
    for key, lhs in (("wg" + n, dg), ("wu" + n, du)):
        ex, done = _carried(hooks, f"ffn{n}_d{key[:2]}", all_grads)
        if ex is None:
            grads[key] = _mm_tn(lhs, h, name=f"ffn{n}_{tag}_d{key[:2]}")
        else:
            grads[key], buf = _mm_tn(lhs, h, name=f"ffn{n}_{tag}_d{key[:2]}", exchange=ex)
            done(buf)
    return dx


def _layer_bwd(dx3, p, s, aux, tag, grads, all_grads, hooks=None):
    dx2 = _ffn_grads(dx3, s["x2"], s["h3"], s["g2"], s["u2"], s["a2"], p, "2", tag, grads, all_grads, hooks)

    dm = _mm_rows(dx2, p["wout"], dims=NT, out_dtype=BF16, name=f"outproj_dm_{tag}")
    grads["wout"] = _mm_tn(s["merged"], dx2, name=f"outproj_dw_{tag}")
    dgp3, dp3, dy3, dh2 = _merge_bwd(dm, s["s3"], s["p3"], p["wmg"], p["wb"], name=f"merge_bwd_{tag}")
    grads["wmg"] = _mm_tn_batch(s["h2"], dgp3, name=f"merge_dwg_{tag}")
    grads["wb"] = _merge_dwb(dp3, (s["y_conv"], s["y_ret"], s["y_att"]), name=f"merge_dwb_{tag}")

    dconv3, grads["conv_w"] = _conv_bwd(s["cols3"], p["conv_w"], dy3, name=f"conv_bwd_{tag}")
    ex, done = _carried(hooks, "ret_bwd", all_grads)
    dret4, buf = _ret_bwd(s["cols3"], aux["rope"], aux["ret"], s["states"], dy3, name=f"ret_bwd_{tag}", exchange=ex)
    done(buf)
    ex, done = _carried(hooks, "att_bwd", all_grads)
    (datt3, dbias3), buf = _att_bwd(s["cols3"], s["bias3"], s["y_att"], s["lse"], dy3, name=f"att_bwd_{tag}",
                                    exchange=ex)
    done(buf)
    grads["rel_bias"] = _rel_bias_grad(dbias3, name=f"bias_grad_{tag}")

    dh2 = _inproj_dh((dconv3, dret4, datt3), p["win"], dh2, name=f"inproj_dh_{tag}")
    grads["win"] = jnp.concatenate([
        _mm_tn_slices(dconv3, s["h2"], name=f"inproj_dw_conv_{tag}"),
        _mm_tn_slices(dret4, s["h2"], name=f"inproj_dw_ret_{tag}"),
        _mm_tn_slices(datt3, s["h2"], name=f"inproj_dw_att_{tag}")], axis=0)
    dx1, grads["nmix"] = _rmsnorm_bwd(dh2, s["x1"], p["nmix"], dx2, name=f"mixnorm_bwd_{tag}")

    return _ffn_grads(dx1, s["x0"], s["h1"], s["g1"], s["u1"], s["a1"], p, "1", tag, grads, all_grads, hooks)


def _device_step(x, target, layers, final_norm, fwd_hooks=None, bwd_hooks=None):
    t = x.shape[0]
    depth = len(layers)
    aux = dict(rope=_rope_tables(t), ret=_ret_consts())
    ex, done = _carried(fwd_hooks[0] if fwd_hooks else None, "start")
    bias_all, buf = _bias_tiles(jnp.stack([p["rel_bias"] for p in layers]), name="bias_tiles", exchange=ex)
    done(buf)
    saved = []
    for l, p in enumerate(layers):
        p["bias3"] = (bias_all, l)
        x, s = _layer_fwd(x, p, aux, f"l{l}", fwd_hooks[l] if fwd_hooks else None)
        saved.append(s)
    dx, dfinal, loss_row = _loss_fwd_bwd(x, final_norm, target, name="loss_fwd_bwd")
    grads = [dict() for _ in range(depth)]
    for l in reversed(range(depth)):
        dx = _layer_bwd(dx, layers[l], saved[l], aux, f"l{l}", grads[l], grads, bwd_hooks[l] if bwd_hooks else None)
    return loss_row, dx, grads, dfinal


def _sum_slots(bufs, *, name, tr=512):
    n, r, cdim = bufs[0].shape
    depth = len(bufs)
    tr = _tile(r, tr)
    nt = r // tr

    def body(*refs):
        o_ref = refs[depth]
        for k in range(depth):
            @pl.when(pl.program_id(0) == k)
            def _(p_ref=refs[k]):
                acc = p_ref[0].astype(F32)
                for s in range(1, n):
                    acc = acc + p_ref[s].astype(F32)
                o_ref[...] = acc

    def spec(k):
        return pl.BlockSpec((n, tr, cdim), lambda l, i: (0, jnp.where(l == k, i, jnp.where(l < k, 0, nt - 1)), 0))

    return pl.pallas_call(
        body, name=name, grid=(depth, nt), in_specs=[spec(k) for k in range(depth)],
        out_specs=pl.BlockSpec((None, tr, cdim), lambda l, i: (l, i, 0)),
        out_shape=jax.ShapeDtypeStruct((depth, r, cdim), F32), compiler_params=_cparams("arbitrary", "arbitrary"),
    )(*bufs)


def _all_reduce_small(v, *, name):
    r = v.shape[0]

    def body(x_ref, o_ref, slots, send_sems, recv_sems):
        x, y, c, me = _my_position()
        slots[me] = x_ref[...]
        sends = []
        for k in range(1, N_DEV):
            peer, _ = _peer(x, y, c, k)
            cp = pltpu.make_async_remote_copy(src_ref=x_ref, dst_ref=slots.at[me], send_sem=send_sems.at[k - 1],
                                              recv_sem=recv_sems.at[k - 1], device_id=peer, device_id_type=MESH)
            cp.start()
            sends.append(cp)
        for k in range(1, N_DEV):
            peer, peer_id = _peer(x, y, c, k)
            pltpu.make_async_remote_copy(src_ref=x_ref, dst_ref=slots.at[peer_id], send_sem=send_sems.at[k - 1],
                                         recv_sem=recv_sems.at[k - 1], device_id=peer, device_id_type=MESH).wait_recv()
        for cp in sends:
            cp.wait_send()
        acc = slots[0]
        for s in range(1, N_DEV):
            acc = acc + slots[s]
        o_ref[...] = acc

    return pl.pallas_call(
        body, name=name, in_specs=[pl.BlockSpec(memory_space=pltpu.VMEM)],
        out_specs=pl.BlockSpec(memory_space=pltpu.VMEM), out_shape=jax.ShapeDtypeStruct((r, 128), F32),
        scratch_shapes=[pltpu.VMEM((N_DEV, r, 128), F32), pltpu.SemaphoreType.DMA((N_DEV - 1,)),
                        pltpu.SemaphoreType.DMA((N_DEV - 1,))],
    )(v)


def _adamw(w, g, m, v, *, name, tr=256):
    shape = w.shape
    cdim = shape[-1]
    w2, g2, m2, v2 = (a.reshape(-1, cdim) for a in (w, g, m, v))
    r = w2.shape[0]
    tr = _tile(r, tr) if r % 8 == 0 else r
    c1 = 1.0 - ADAM_B1 ** ADAM_STEP
    c2 = 1.0 - ADAM_B2 ** ADAM_STEP

    def body(w_ref, g_ref, m_ref, v_ref, d_ref, mo_ref, vo_ref):
        gv = g_ref[...]
        mn = ADAM_B1 * m_ref[...] + (1.0 - ADAM_B1) * gv
        vn = ADAM_B2 * v_ref[...] + (1.0 - ADAM_B2) * (gv * gv)
        d_ref[...] = -ADAM_LR * ((mn / c1) / (jnp.sqrt(vn / c2) + ADAM_EPS) + ADAM_WD * w_ref[...])
        mo_ref[...] = mn
        vo_ref[...] = vn

    spec = pl.BlockSpec((tr, cdim), lambda i: (i, 0))
    outs = pl.pallas_call(
        body, name=name, grid=(r // tr,), in_specs=[spec] * 4, out_specs=[spec] * 3,
        out_shape=[jax.ShapeDtypeStruct((r, cdim), F32)] * 3, compiler_params=_cparams("parallel"),
    )(w2, g2, m2, v2)
    return tuple(o.reshape(shape) for o in outs)


GROUPS = {"f1": ("wg1", "wu1", "wd1"), "in": ("win",), "mg": ("wb", "wmg", "wout"), "f2": ("wg2", "wu2", "wd2")}
BIG = tuple(nm for members in GROUPS.values() for nm in members)
ROWS_DOMAIN = ("ffn1_w_gate", "ffn1_w_up", "ffn2_w_gate", "ffn2_w_up")


def _to_rows(name, w, d):
    depth = w.shape[0]
    if name in ("wg1", "wu1", "wg2", "wu2", "win"):
        return w.transpose(0, 2, 1)
    if name == "wb":
        return w.transpose(0, 1, 3, 2).reshape(depth, -1, d)
    if name == "wmg":
        return w.reshape(depth, -1, d)
    return w


def _from_rows(name, rows, d):
    depth = rows.shape[0]
    if name in ("wg1", "wu1", "wg2", "wu2", "win"):
        return rows.transpose(0, 2, 1)
    if name == "wb":
        return rows.reshape(depth, 3, -1, BRANCH_W).transpose(0, 1, 3, 2)
    if name == "wmg":
        return rows.reshape(depth, 3, -1, d)
    return rows


def _full_from_gathered(name, g, d):
    if name == "wb":
        return g.reshape(N_DEV, 3, -1, BRANCH_W).transpose(1, 0, 2, 3).reshape(3, d, BRANCH_W)
    if name == "wmg":
        return g.reshape(N_DEV, 3, -1, d).transpose(1, 0, 2, 3).reshape(3, d, d)
    return g.reshape(-1, d)


def _gathered_from_full(name, full, d):
    if name == "wb":
        return full.reshape(3, N_DEV, -1, BRANCH_W).transpose(1, 0, 2, 3).reshape(N_DEV, -1, d)
    if name == "wmg":
        return full.reshape(3, N_DEV, -1, d).transpose(1, 0, 2, 3).reshape(N_DEV, -1, d)
    return full.reshape(N_DEV, -1, d)


def kernel(x, ffn1_norm, ffn1_w_gate, ffn1_w_up, ffn1_w_down, mix_norm, w_in, conv_w, rel_bias, w_branch, w_merge_gate, w_out, ffn2_norm, ffn2_w_gate, ffn2_w_up, ffn2_w_down, final_norm, loss_target, m_ffn1_norm, m_ffn1_w_gate, m_ffn1_w_up, m_ffn1_w_down, m_mix_norm, m_w_in, m_conv_w, m_rel_bias, m_w_branch, m_w_merge_gate, m_w_out, m_ffn2_norm, m_ffn2_w_gate, m_ffn2_w_up, m_ffn2_w_down, m_final_norm, v_ffn1_norm, v_ffn1_w_gate, v_ffn1_w_up, v_ffn1_w_down, v_mix_norm, v_w_in, v_conv_w, v_rel_bias, v_w_branch, v_w_merge_gate, v_w_out, v_ffn2_norm, v_ffn2_w_gate, v_ffn2_w_up, v_ffn2_w_down, v_final_norm):
    names = ["ffn1_norm", "ffn1_w_gate", "ffn1_w_up", "ffn1_w_down", "mix_norm", "w_in", "conv_w", "rel_bias",
             "w_branch", "w_merge_gate", "w_out", "ffn2_norm", "ffn2_w_gate", "ffn2_w_up", "ffn2_w_down", "final_norm"]
    weights = dict(zip(names, (ffn1_norm, ffn1_w_gate, ffn1_w_up, ffn1_w_down, mix_norm, w_in, conv_w, rel_bias,
                               w_branch, w_merge_gate, w_out, ffn2_norm, ffn2_w_gate, ffn2_w_up, ffn2_w_down,
                               final_norm)))
    m_in = dict(zip(names, (m_ffn1_norm, m_ffn1_w_gate, m_ffn1_w_up, m_ffn1_w_down, m_mix_norm, m_w_in, m_conv_w,
                            m_rel_bias, m_w_branch, m_w_merge_gate, m_w_out, m_ffn2_norm, m_ffn2_w_gate,
                            m_ffn2_w_up, m_ffn2_w_down, m_final_norm)))
    v_in = dict(zip(names, (v_ffn1_norm, v_ffn1_w_gate, v_ffn1_w_up, v_ffn1_w_down, v_mix_norm, v_w_in, v_conv_w,
                            v_rel_bias, v_w_branch, v_w_merge_gate, v_w_out, v_ffn2_norm, v_ffn2_w_gate,
                            v_ffn2_w_up, v_ffn2_w_down, v_final_norm)))
    big_of = dict(wg1="ffn1_w_gate", wu1="ffn1_w_up", wd1="ffn1_w_down", win="w_in", wb="w_branch",
                  wmg="w_merge_gate", wout="w_out", wg2="ffn2_w_gate", wu2="ffn2_w_up", wd2="ffn2_w_down")
    depth = ffn1_norm.shape[0]
    d = x.shape[-1]
    xs = x.reshape(-1, d)
    target = loss_target.reshape(-1, d)
    _, _, _, me = _my_position()

    shard_rows = {nm: _to_rows(nm, weights[big_of[nm]], d).astype(BF16) for nm in BIG}

    conv_cols = conv_w.shape[-1]
    conv_full, = _exchange("gather", [conv_w.reshape(depth * 3, conv_cols)], name="gather_conv_w")
    conv_full = conv_full.reshape(N_DEV, depth, 3, conv_cols).transpose(1, 2, 0, 3).reshape(depth, 3, -1)
    layers = [dict(n1=ffn1_norm[l][None], nmix=mix_norm[l][None], n2=ffn2_norm[l][None], conv_w=conv_full[l],
                   rel_bias=rel_bias[l]) for l in range(depth)]

    def gather_hook(l, grp, kind="gather"):
        members = GROUPS[grp]

        def done(bufs):
            layers[l].update({nm: _full_from_gathered(nm, buf, d) for nm, buf in zip(members, bufs)})

        return (lambda _: (kind, [shard_rows[nm] for nm in members], [l] * len(members))), done

    fwd_hooks = []
    for l in range(depth):
        hooks = {"ffn1": gather_hook(l, "in"), "inproj": gather_hook(l, "mg"), "att": gather_hook(l, "f2")}
        if l == 0:
            hooks["start"] = gather_hook(0, "f1", "gather2")
        if l + 1 < depth:
            hooks["ffn2"] = gather_hook(l + 1, "f1")
        fwd_hooks.append(hooks)

    received = {}

    def scatter_hook(l, members):
        def make(all_grads):
            return "scatter", [_gathered_from_full(nm, all_grads[l][nm], d) for nm in members], [None] * len(members)

        return make, (lambda bufs: received.update({(l, nm): buf for nm, buf in zip(members, bufs)}))

    bwd_hooks = []
    for l in range(depth):
        hooks = {"att_bwd": scatter_hook(l, GROUPS["f2"]), "ret_bwd": scatter_hook(l, GROUPS["mg"]),
                 "ffn1_bwd": scatter_hook(l, GROUPS["in"])}
        if l + 1 < depth:
            hooks["ffn2_bwd"] = scatter_hook(l + 1, GROUPS["f1"])
        if l == 0:
            hooks["ffn1_dwg"] = scatter_hook(0, ("wd1",))
            hooks["ffn1_dwu"] = scatter_hook(0, ("wg1",))
        bwd_hooks.append(hooks)

    loss_row, dx, grads, dfinal = _device_step(xs, target, layers, final_norm[None], fwd_hooks, bwd_hooks)
    received[(0, "wu1")], = _exchange("scatter", [_gathered_from_full("wu1", grads[0]["wu1"], d)], name="scatter_l0_wu1")

    grad_rows = {big_of[nm]: _sum_slots([received[(l, nm)] for l in range(depth)], name=f"sum_grads_{nm}")
                 for nm in BIG}
    grad_w = {big_of[nm]: _from_rows(nm, grad_rows[big_of[nm]], d) for nm in BIG}

    small = {"ffn1_norm": jnp.stack([grads[l]["n1"][0] for l in range(depth)]),
             "mix_norm": jnp.stack([grads[l]["nmix"][0] for l in range(depth)]),
             "ffn2_norm": jnp.stack([grads[l]["n2"][0] for l in range(depth)]),
             "final_norm": dfinal[0],
             "rel_bias": jnp.stack([grads[l]["rel_bias"] for l in range(depth)]),
             "conv_w": jnp.stack([grads[l]["conv_w"] for l in range(depth)]),
             "loss": loss_row[0, :1]}
    order = list(small)
    flat = jnp.concatenate([small[k].reshape(-1) for k in order])
    pad = (-flat.shape[0]) % 1024
    summed = _all_reduce_small(jnp.pad(flat, (0, pad)).reshape(-1, 128), name="reduce_small").reshape(-1)
    pos = 0
    for k in order:
        n = small[k].size
        small[k] = summed[pos:pos + n].reshape(small[k].shape)
        pos += n
    small["conv_w"] = lax.dynamic_slice_in_dim(small["conv_w"], me * conv_cols, conv_cols, axis=2)

    grad_w.update({k: small[k] for k in order if k != "loss"})
    delta, new_m, new_v = {}, {}, {}
    for nm in names:
        if nm in ROWS_DOMAIN:
            flip = lambda a: a.transpose(0, 2, 1)
            outs = _adamw(flip(weights[nm]), grad_rows[nm], flip(m_in[nm]), flip(v_in[nm]), name=f"adamw_{nm}")
            delta[nm], new_m[nm], new_v[nm] = (flip(o) for o in outs)
        else:
            delta[nm], new_m[nm], new_v[nm] = _adamw(weights[nm], grad_w[nm], m_in[nm], v_in[nm], name=f"adamw_{nm}")
    return (small["loss"].reshape(()), dx.reshape(x.shape), *[grad_w[n] for n in names], *[delta[n] for n in names],
            *[new_m[n] for n in names], *[new_v[n] for n in names])
```

```python
import functools
import math

import jax
import jax.numpy as jnp
from jax import lax
from jax.experimental import pallas as pl
from jax.experimental.pallas import tpu as pltpu

F32 = jnp.float32
BF16 = jnp.bfloat16

N_DEV = 8
EPS = 1e-6
CHUNK = 64
BRANCH_W = 512
N_SLICES = 10
H_RET = 4
DK_RET = 128
H_ATT = 8
DH_ATT = 64
N_PREV_CHUNKS = 8
REL_CLIP = 128
N_REL = 2 * REL_CLIP + 1
NEG_INF = -1e30
ROPE_BASE = 10000.0
ATT_QB = 256
ATT_LOOKBACK = N_PREV_CHUNKS * CHUNK
ATT_WIN = ATT_LOOKBACK + ATT_QB
RET_TB = 256
RET_GROUP = 4
TOKEN_TK = 2048
PARTIAL_DTYPE = jnp.bfloat16
CONV_HALO = 16

ADAM_LR = 0.001
ADAM_B1 = 0.9
ADAM_B2 = 0.999
ADAM_EPS = 1e-08
ADAM_WD = 0.01
ADAM_STEP = 10

VMEM_LIMIT_BYTES = 56 * 1024 * 1024
MESH = pl.DeviceIdType.MESH


def _cparams(*sem):
    return pltpu.CompilerParams(dimension_semantics=sem, vmem_limit_bytes=VMEM_LIMIT_BYTES)


def _dot(a, b):
    return lax.dot_general(a, b, (((1,), (0,)), ((), ())), preferred_element_type=F32)


def _dot_nt(a, b):
    return lax.dot_general(a, b, (((1,), (1,)), ((), ())), preferred_element_type=F32)


def _dot_tn(a, b):
    return lax.dot_general(a, b, (((0,), (0,)), ((), ())), preferred_element_type=F32)


def _bf(v):
    return v.astype(BF16)


def _sigmoid(v):
    return 1.0 / (1.0 + jnp.exp(-v))


def _tile(n, want):
    if n <= want:
        return n
    for t in range(want - want % 128, 0, -128):
        if n % t == 0:
            return t
    t = want
    while n % t:
        t //= 2
    return t


def _my_position():
    x, y, c = lax.axis_index("x"), lax.axis_index("y"), lax.axis_index("c")
    return x, y, c, 4 * x + 2 * y + c


def _peer(x, y, c, k):
    px = 1 - x if k & 4 else x
    py = 1 - y if k & 2 else y
    pc = 1 - c if k & 1 else c
    return (px, py, pc), 4 * px + 2 * py + pc


DIRECT2 = (1, 2, 4, 6)
PASSED_ON = (2, 4, 6)
N_PUSHES = N_DEV - 1 + len(PASSED_ON)


def _exchange_sems(n):
    return [pltpu.SemaphoreType.DMA((n, N_PUSHES)), pltpu.SemaphoreType.DMA((n, N_PUSHES)),
            pltpu.SemaphoreType.DMA((n,))]


def _exchange_src(kind, src_ref, layer, block):
    if kind in ("gather", "gather2"):
        return src_ref if layer is None else src_ref.at[layer]
    return src_ref.at[block]


def _passed_on(dst_ref, sems, a, j, pos, incoming):
    x, y, c, _ = pos
    sibling, _ = _peer(x, y, c, 1)
    _, block = _peer(x, y, c, PASSED_ON[j] | 1 if incoming else PASSED_ON[j])
    return pltpu.make_async_remote_copy(src_ref=dst_ref.at[block], dst_ref=dst_ref.at[block],
                                        send_sem=sems[0].at[a, N_DEV - 1 + j], recv_sem=sems[1].at[a, N_DEV - 1 + j],
                                        device_id=sibling, device_id_type=MESH)


def _exchange_copy(kind, src_ref, layer, dst_ref, sems, a, k, pos, incoming):
    x, y, c, me = pos
    peer, peer_id = _peer(x, y, c, k)
    return pltpu.make_async_remote_copy(src_ref=_exchange_src(kind, src_ref, layer, me if incoming else peer_id),
                                        dst_ref=dst_ref.at[peer_id if incoming else me],
                                        send_sem=sems[0].at[a, k - 1], recv_sem=sems[1].at[a, k - 1],
                                        device_id=peer, device_id_type=MESH)


def _exchange_local(kind, src_ref, layer, dst_ref, sems, a, me):
    return pltpu.make_async_copy(_exchange_src(kind, src_ref, layer, me), dst_ref.at[me], sems[2].at[a])


def _exchange_start(kind, layers, src_refs, dst_refs, sems):
    pos = _my_position()
    direct = DIRECT2 if kind == "gather2" else range(1, N_DEV)
    for a, (src_ref, layer, dst_ref) in enumerate(zip(src_refs, layers, dst_refs)):
        _exchange_local(kind, src_ref, layer, dst_ref, sems, a, pos[3]).start()
        for k in direct:
            _exchange_copy(kind, src_ref, layer, dst_ref, sems, a, k, pos, False).start()


def _exchange_wait(kind, layers, src_refs, dst_refs, sems):
    pos = _my_position()
    for a, (src_ref, layer, dst_ref) in enumerate(zip(src_refs, layers, dst_refs)):
        if kind == "gather2":
            for j, k in enumerate(PASSED_ON):
                _exchange_copy(kind, src_ref, layer, dst_ref, sems, a, k, pos, True).wait_recv()
                _passed_on(dst_ref, sems, a, j, pos, False).start()
            _exchange_copy(kind, src_ref, layer, dst_ref, sems, a, 1, pos, True).wait_recv()
            for j in range(len(PASSED_ON)):
                _passed_on(dst_ref, sems, a, j, pos, True).wait_recv()
            for k in DIRECT2:
                _exchange_copy(kind, src_ref, layer, dst_ref, sems, a, k, pos, False).wait_send()
            for j in range(len(PASSED_ON)):
                _passed_on(dst_ref, sems, a, j, pos, False).wait_send()
        else:
            for k in range(1, N_DEV):
                _exchange_copy(kind, src_ref, layer, dst_ref, sems, a, k, pos, True).wait_recv()
            for k in range(1, N_DEV):
                _exchange_copy(kind, src_ref, layer, dst_ref, sems, a, k, pos, False).wait_send()
        _exchange_local(kind, src_ref, layer, dst_ref, sems, a, pos[3]).wait()


def _exchange_shapes(kind, srcs, layers):
    if kind in ("gather", "gather2"):
        return [jax.ShapeDtypeStruct((N_DEV,) + (s.shape if l is None else s.shape[1:]), s.dtype)
                for s, l in zip(srcs, layers)]
    return [jax.ShapeDtypeStruct(s.shape, s.dtype) for s in srcs]


def _exchange(kind, srcs, layers=None, *, name):
    n = len(srcs)
    layers = layers or [None] * n
    hbm = pl.BlockSpec(memory_space=pl.ANY)

    def body(*refs):
        src_refs, dst_refs, sems = refs[:n], refs[n:2 * n], refs[2 * n:]
        _exchange_start(kind, layers, src_refs, dst_refs, sems)
        _exchange_wait(kind, layers, src_refs, dst_refs, sems)

    return pl.pallas_call(body, name=name, in_specs=[hbm] * n, out_specs=[hbm] * n,
                          out_shape=_exchange_shapes(kind, srcs, layers), scratch_shapes=_exchange_sems(n))(*srcs)


def _pcall(body, args, *, name, grid, in_specs, out_specs, out_shape, scratch_shapes, sem, exchange=None):
    if exchange is None:
        outs = pl.pallas_call(body, name=name, grid=grid, in_specs=in_specs, out_specs=out_specs, out_shape=out_shape,
                              scratch_shapes=scratch_shapes, compiler_params=_cparams(*sem))(*args)
        return outs, None
    kind, srcs, layers = exchange
    n_in, n_out, n_scr, n_x = len(in_specs), len(out_specs), len(scratch_shapes), len(srcs)
    hbm = pl.BlockSpec(memory_space=pl.ANY)

    def carrier(*refs):
        ins, refs = refs[:n_in], refs[n_in:]
        src_refs, refs = refs[:n_x], refs[n_x:]
        outs, refs = refs[:n_out], refs[n_out:]
        dst_refs, refs = refs[:n_x], refs[n_x:]
        scr, sems = refs[:n_scr], refs[n_scr:]
        ids = [pl.program_id(a) for a in range(len(grid))]
        first = functools.reduce(jnp.logical_and, [i == 0 for i in ids])
        last = functools.reduce(jnp.logical_and, [i == g - 1 for i, g in zip(ids, grid)])

        @pl.when(first)
        def _():
            _exchange_start(kind, layers, src_refs, dst_refs, sems)

        body(*ins, *outs, *scr)

        @pl.when(last)
        def _():
            _exchange_wait(kind, layers, src_refs, dst_refs, sems)

    outs = pl.pallas_call(
        carrier, name=name, grid=grid, in_specs=list(in_specs) + [hbm] * n_x, out_specs=list(out_specs) + [hbm] * n_x,
        out_shape=list(out_shape) + _exchange_shapes(kind, srcs, layers),
        scratch_shapes=list(scratch_shapes) + _exchange_sems(n_x),
        compiler_params=_cparams(*(["arbitrary"] * len(grid))))(*args, *srcs)
    return outs[:n_out], outs[n_out:]


def _accumulate(acc_ref, part, k, nk, finish):
    if nk == 1:
        finish(part)
        return

    @pl.when(k == 0)
    def _():
        acc_ref[...] = part

    @pl.when(jnp.logical_and(k > 0, k < nk - 1))
    def _():
        acc_ref[...] += part

    @pl.when(k == nk - 1)
    def _():
        finish(acc_ref[...] + part)


def _matmul(a, b, *, dims, grid, a_spec, b_spec, o_spec, out_shape, acc_shape, name, scale=1.0,
            res=None, res_spec=None, exchange=None):
    nk = grid[3]
    has_res = res is not None

    def body(*refs):
        if has_res:
            a_ref, b_ref, r_ref, o_ref = refs[:4]
        else:
            a_ref, b_ref, o_ref = refs[:3]
            r_ref = None

        def finish(acc):
            if scale != 1.0:
                acc = acc * scale
            if r_ref is not None:
                acc = acc + r_ref[...].astype(F32)
            o_ref[...] = acc.astype(o_ref.dtype)

        part = lax.dot_general(_bf(a_ref[...]), _bf(b_ref[...]), (dims, ((), ())), preferred_element_type=F32)
        _accumulate(None if nk == 1 else refs[-1], part, pl.program_id(3), nk, finish)

    in_specs = [a_spec, b_spec] + ([res_spec] if has_res else [])
    args = (a, b) + ((res,) if has_res else ())
    outs, moved = _pcall(
        body, args, name=name, grid=grid, in_specs=in_specs, out_specs=[o_spec], out_shape=[out_shape],
        scratch_shapes=[] if nk == 1 else [pltpu.VMEM(acc_shape, F32)],
        sem=("parallel", "parallel", "parallel", "arbitrary"), exchange=exchange)
    return outs[0] if exchange is None else (outs[0], moved)


NT = ((1,), (1,))
NN = ((1,), (0,))
TN = ((0,), (0,))


def _mm_rows(a, b, *, dims, out_dtype, name, res=None, scale=1.0, tm=1024, tn=1024, tk=1024):
    m, kdim = a.shape
    n = b.shape[1] if dims == NN else b.shape[0]
    tm, tn, tk = _tile(m, tm), _tile(n, tn), _tile(kdim, tk)
    grid = (1, m // tm, n // tn, kdim // tk)
    a_spec = pl.BlockSpec((tm, tk), lambda s, i, j, k: (i, k))
    if dims == NN:
        b_spec = pl.BlockSpec((tk, tn), lambda s, i, j, k: (k, j))
    else:
        b_spec = pl.BlockSpec((tn, tk), lambda s, i, j, k: (j, k))
    o_spec = pl.BlockSpec((tm, tn), lambda s, i, j, k: (i, j))
    return _matmul(a, b, dims=dims, grid=grid, a_spec=a_spec, b_spec=b_spec, o_spec=o_spec,
                   out_shape=jax.ShapeDtypeStruct((m, n), out_dtype), acc_shape=(tm, tn), name=name,
                   res=res, res_spec=o_spec if res is not None else None, scale=scale)


def _mm_to_slices(a, b_t, *, name, tm=2048, exchange=None):
    m, kdim = a.shape
    n = b_t.shape[0]
    tm = _tile(m, tm)
    w = BRANCH_W
    grid = (1, m // tm, n // w, 1)
    return _matmul(a, b_t, dims=NT, grid=grid,
                   a_spec=pl.BlockSpec((tm, kdim), lambda s, i, j, k: (i, 0)),
                   b_spec=pl.BlockSpec((w, kdim), lambda s, i, j, k: (j, 0)),
                   o_spec=pl.BlockSpec((None, tm, w), lambda s, i, j, k: (j, i, 0)),
                   out_shape=jax.ShapeDtypeStruct((n // w, m, w), BF16), acc_shape=(tm, w), name=name,
                   exchange=exchange)


def _inproj_dh(parts, w_t, res, *, name, tm=512):
    t, n = res.shape
    w = BRANCH_W
    tm = _tile(t, tm)
    counts = [p.shape[0] for p in parts]
    assert sum(counts) * w == w_t.shape[0]

    def body(*refs):
        w_ref, r_ref, o_ref = refs[len(parts):]
        cols = jnp.concatenate([p_ref[s] for p_ref, cnt in zip(refs, counts) for s in range(cnt)], axis=1)
        o_ref[...] = r_ref[...] + _dot(cols, w_ref[...])

    row = pl.BlockSpec((tm, n), lambda i: (i, 0))
    return pl.pallas_call(
        body, name=name, grid=(t // tm,),
        in_specs=[pl.BlockSpec((cnt, tm, w), lambda i: (0, i, 0)) for cnt in counts]
        + [pl.BlockSpec(w_t.shape, lambda i: (0, 0)), row],
        out_specs=row, out_shape=jax.ShapeDtypeStruct((t, n), F32), compiler_params=_cparams("parallel"),
    )(*parts, w_t, res)


def _mm_tn(a, b, *, name, tm=1408, tn=1024, tk=TOKEN_TK, exchange=None):
    t, m = a.shape
    n = b.shape[1]
    tm, tn, tk = _tile(m, tm), _tile(n, tn), _tile(t, tk)
    grid = (1, m // tm, n // tn, t // tk)
    return _matmul(a, b, dims=TN, grid=grid,
                   a_spec=pl.BlockSpec((tk, tm), lambda s, i, j, k: (k, i)),
                   b_spec=pl.BlockSpec((tk, tn), lambda s, i, j, k: (k, j)),
                   o_spec=pl.BlockSpec((tm, tn), lambda s, i, j, k: (i, j)),
                   out_shape=jax.ShapeDtypeStruct((m, n), PARTIAL_DTYPE), acc_shape=(tm, tn), name=name,
                   exchange=exchange)


def _mm_tn_slices(a3, b, *, name, tn=1024, tk=TOKEN_TK):
    s_n, t, w = a3.shape
    n = b.shape[1]
    tn, tk = _tile(n, tn), _tile(t, tk)
    grid = (1, s_n, n // tn, t // tk)
    return _matmul(a3, b, dims=TN, grid=grid,
                   a_spec=pl.BlockSpec((None, tk, w), lambda s, i, j, k: (i, k, 0)),
                   b_spec=pl.BlockSpec((tk, tn), lambda s, i, j, k: (k, j)),
                   o_spec=pl.BlockSpec((w, tn), lambda s, i, j, k: (i, j)),
                   out_shape=jax.ShapeDtypeStruct((s_n * w, n), PARTIAL_DTYPE), acc_shape=(w, tn), name=name)


def _mm_tn_batch(a, b3, *, name, tm=1024, tn=1024, tk=TOKEN_TK):
    s_n, t, n = b3.shape
    m = a.shape[-1]
    tm, tn, tk = _tile(m, tm), _tile(n, tn), _tile(t, tk)
    grid = (s_n, m // tm, n // tn, t // tk)
    return _matmul(a, b3, dims=TN, grid=grid, a_spec=pl.BlockSpec((tk, tm), lambda s, i, j, k: (k, i)),
                   b_spec=pl.BlockSpec((None, tk, tn), lambda s, i, j, k: (s, k, j)),
                   o_spec=pl.BlockSpec((None, tm, tn), lambda s, i, j, k: (s, i, j)),
                   out_shape=jax.ShapeDtypeStruct((s_n, m, n), PARTIAL_DTYPE), acc_shape=(tm, tn), name=name)


def _norm_parts(xf):
    r = lax.rsqrt(jnp.mean(xf * xf, axis=-1, keepdims=True) + EPS)
    return xf * r, r


def _norm_bwd(dh, xhat, r, w):
    dxhat = dh * w
    dx = r * (dxhat - xhat * jnp.mean(dxhat * xhat, axis=-1, keepdims=True))
    return dx, jnp.sum(dh * xhat, axis=0, keepdims=True)


def _rmsnorm_fwd(x, w, *, name, tm=1024):
    t, d = x.shape
    tm = _tile(t, tm)

    def body(x_ref, w_ref, h_ref):
        xhat, _ = _norm_parts(x_ref[...])
        h_ref[...] = _bf(xhat * w_ref[...])

    return pl.pallas_call(
        body, name=name, grid=(t // tm,),
        in_specs=[pl.BlockSpec((tm, d), lambda i: (i, 0)), pl.BlockSpec((1, d), lambda i: (0, 0))],
        out_specs=pl.BlockSpec((tm, d), lambda i: (i, 0)),
        out_shape=jax.ShapeDtypeStruct((t, d), BF16), compiler_params=_cparams("parallel"),
    )(x, w)


def _rmsnorm_bwd(dh, x, w, dres, *, name, tm=1024):
    t, d = x.shape
    tm = _tile(t, tm)

    def body(dh_ref, x_ref, w_ref, dres_ref, dx_ref, dw_ref):
        xhat, r = _norm_parts(x_ref[...])
        dx, dw = _norm_bwd(dh_ref[...], xhat, r, w_ref[...])
        dx_ref[...] = dres_ref[...] + dx

        @pl.when(pl.program_id(0) == 0)
        def _():
            dw_ref[...] = dw

        @pl.when(pl.program_id(0) > 0)
        def _():
            dw_ref[...] += dw

    row = pl.BlockSpec((tm, d), lambda i: (i, 0))
    vec = pl.BlockSpec((1, d), lambda i: (0, 0))
    return pl.pallas_call(
        body, name=name, grid=(t // tm,), in_specs=[row, row, vec, row], out_specs=[row, vec],
        out_shape=[jax.ShapeDtypeStruct((t, d), F32), jax.ShapeDtypeStruct((1, d), F32)],
        compiler_params=_cparams("arbitrary"),
    )(dh, x, w, dres)


def _loss_fwd_bwd(x, w, target, *, name, tm=1024):
    t, d = x.shape
    tm = _tile(t, tm)

    def body(x_ref, w_ref, t_ref, dx_ref, dw_ref, loss_ref):
        xhat, r = _norm_parts(x_ref[...])
        wv = w_ref[...]
        err = xhat * wv - t_ref[...]
        dx, dw = _norm_bwd(err * (1.0 / d), xhat, r, wv)
        dx_ref[...] = dx
        part = jnp.full((1, 128), 0.5 / d, F32) * jnp.sum(err * err)

        @pl.when(pl.program_id(0) == 0)
        def _():
            dw_ref[...] = dw
            loss_ref[...] = part

        @pl.when(pl.program_id(0) > 0)
        def _():
            dw_ref[...] += dw
            loss_ref[...] += part

    row = pl.BlockSpec((tm, d), lambda i: (i, 0))
    vec = pl.BlockSpec((1, d), lambda i: (0, 0))
    return pl.pallas_call(
        body, name=name, grid=(t // tm,), in_specs=[row, vec, row],
        out_specs=[row, vec, pl.BlockSpec((1, 128), lambda i: (0, 0))],
        out_shape=[jax.ShapeDtypeStruct((t, d), F32), jax.ShapeDtypeStruct((1, d), F32),
                   jax.ShapeDtypeStruct((1, 128), F32)],
        compiler_params=_cparams("arbitrary"),
    )(x, w, target)


def _ffn_tiles(t, f):
    tf = f
    for cand in (1408, 1024, 512, 256, 128):
        if f % cand == 0:
            tf = cand
            break
    return _tile(t, 512), tf


def _ffn_fwd(x, nw, wg_t, wu_t, wd, *, name, exchange=None):
    t, d = x.shape
    f = wd.shape[0]
    tm, tf = _ffn_tiles(t, f)
    nf = f // tf

    def body(x_ref, nw_ref, wg_ref, wu_ref, wd_ref, xo_ref, h_ref, g_ref, u_ref, a_ref, hs_ref, acc_ref):
        j = pl.program_id(1)

        @pl.when(j == 0)
        def _():
            xhat, _ = _norm_parts(x_ref[...])
            hb = _bf(xhat * nw_ref[...])
            hs_ref[...] = hb
            h_ref[...] = hb

        hb = hs_ref[...]
        g = _dot_nt(hb, wg_ref[...])
        u = _dot_nt(hb, wu_ref[...])
        a = _bf(g * _sigmoid(g) * u)
        g_ref[...] = _bf(g)
        u_ref[...] = _bf(u)
        a_ref[...] = a
        part = _dot(a, wd_ref[...])

        def finish(total):
            xo_ref[...] = x_ref[...] + 0.5 * total

        _accumulate(acc_ref, part, j, nf, finish)

    row = pl.BlockSpec((tm, d), lambda i, j: (i, 0))
    wspec = pl.BlockSpec((tf, d), lambda i, j: (j, 0))
    hid = pl.BlockSpec((tm, tf), lambda i, j: (i, j))
    return _pcall(
        body, (x, nw, wg_t, wu_t, wd), name=name, grid=(t // tm, nf),
        in_specs=[row, pl.BlockSpec((1, d), lambda i, j: (0, 0)), wspec, wspec, wspec],
        out_specs=[row, row, hid, hid, hid],
        out_shape=[jax.ShapeDtypeStruct((t, d), F32), jax.ShapeDtypeStruct((t, d), BF16)]
        + [jax.ShapeDtypeStruct((t, f), BF16)] * 3,
        scratch_shapes=[pltpu.VMEM((tm, d), BF16), pltpu.VMEM((tm, d), F32)],
        sem=("parallel", "arbitrary"), exchange=exchange)


def _ffn_bwd(dxo, x, nw, g, u, wg_t, wu_t, wd, *, name, exchange=None):
    t, d = x.shape
    f = wd.shape[0]
    tm, tf = _ffn_tiles(t, f)
    nf = f // tf

    def body(dxo_ref, x_ref, nw_ref, g_ref, u_ref, wg_ref, wu_ref, wd_ref,
             dx_ref, dy_ref, dg_ref, du_ref, dnw_ref, dys_ref, acc_ref):
        i, j = pl.program_id(0), pl.program_id(1)

        @pl.when(j == 0)
        def _():
            dyb = _bf(0.5 * dxo_ref[...])
            dys_ref[...] = dyb
            dy_ref[...] = dyb

        da = _dot_nt(dys_ref[...], wd_ref[...])
        gv = g_ref[...].astype(F32)
        uv = u_ref[...].astype(F32)
        s = _sigmoid(gv)
        dg = _bf(da * uv * (s * (1.0 + gv * (1.0 - s))))
        du = _bf(da * (gv * s))
        dg_ref[...] = dg
        du_ref[...] = du
        part = _dot(dg, wg_ref[...]) + _dot(du, wu_ref[...])

        def finish(dh):
            xhat, r = _norm_parts(x_ref[...])
            dx, dw = _norm_bwd(dh, xhat, r, nw_ref[...])
            dx_ref[...] = dxo_ref[...] + dx

            @pl.when(i == 0)
            def _():
                dnw_ref[...] = dw

            @pl.when(i > 0)
            def _():
                dnw_ref[...] += dw

        _accumulate(acc_ref, part, j, nf, finish)

    row = pl.BlockSpec((tm, d), lambda i, j: (i, 0))
    vec = pl.BlockSpec((1, d), lambda i, j: (0, 0))
    wspec = pl.BlockSpec((tf, d), lambda i, j: (j, 0))
    hid = pl.BlockSpec((tm, tf), lambda i, j: (i, j))
    return _pcall(
        body, (dxo, x, nw, g, u, wg_t, wu_t, wd), name=name, grid=(t // tm, nf),
        in_specs=[row, row, vec, hid, hid, wspec, wspec, wspec],
        out_specs=[row, row, hid, hid, vec],
        out_shape=[jax.ShapeDtypeStruct((t, d), F32), jax.ShapeDtypeStruct((t, d), BF16),
                   jax.ShapeDtypeStruct((t, f), BF16), jax.ShapeDtypeStruct((t, f), BF16),
                   jax.ShapeDtypeStruct((1, d), F32)],
        scratch_shapes=[pltpu.VMEM((tm, d), BF16), pltpu.VMEM((tm, d), F32)],
        sem=("arbitrary", "arbitrary"), exchange=exchange)


def _shift_down(prev, cur, n):
    ext = jnp.concatenate([prev, cur], axis=0)
    return pltpu.roll(ext, n, axis=0)[prev.shape[0]:]


def _shift_up(cur, nxt, n):
    ext = jnp.concatenate([cur, nxt], axis=0)
    return pltpu.roll(ext, ext.shape[0] - n, axis=0)[:cur.shape[0]]


def _conv_specs(t, tb):
    hb = tb // CONV_HALO
    last = t // CONV_HALO - 1

    def tile(s):
        return pl.BlockSpec((None, tb, 128), lambda c, i: (s, i, c))

    def prev(s):
        return pl.BlockSpec((None, CONV_HALO, 128), lambda c, i: (s, jnp.maximum(i * hb - 1, 0), c))

    def nxt(s):
        return pl.BlockSpec((None, CONV_HALO, 128), lambda c, i: (s, jnp.minimum((i + 1) * hb, last), c))

    return tile, prev, nxt


def _conv_fwd(cols3, conv_w, *, name, tb=1024):
    _, t, bw = cols3.shape
    tb = _tile(t, tb)
    tile, prev, _ = _conv_specs(t, tb)

    def body(u_ref, b_ref, c_ref, up_ref, cp_ref, w_ref, y_ref):
        first = pl.program_id(1) == 0
        z = c_ref[...].astype(F32) * u_ref[...].astype(F32)
        zp = jnp.where(first, 0.0, cp_ref[...].astype(F32) * up_ref[...].astype(F32))
        conv = w_ref[0:1, :] * _shift_down(zp, z, 2) + w_ref[1:2, :] * _shift_down(zp, z, 1) + w_ref[2:3, :] * z
        y_ref[...] = _bf(b_ref[...].astype(F32) * conv)

    return pl.pallas_call(
        body, name=name, grid=(bw // 128, t // tb),
        in_specs=[tile(0), tile(1), tile(2), prev(0), prev(2), pl.BlockSpec((3, 128), lambda c, i: (0, c))],
        out_specs=pl.BlockSpec((tb, 128), lambda c, i: (i, c)),
        out_shape=jax.ShapeDtypeStruct((t, bw), BF16), compiler_params=_cparams("parallel", "parallel"),
    )(cols3, cols3, cols3, cols3, cols3, conv_w)


def _conv_bwd(cols3, conv_w, dy, *, name, tb=1024):
    _, t, bw = cols3.shape
    tb = _tile(t, tb)
    nt = t // tb
    tile, prev, nxt = _conv_specs(t, tb)
    hb = tb // CONV_HALO
    last = t // CONV_HALO - 1

    def body(u_ref, b_ref, c_ref, up_ref, cp_ref, bn_ref, dy_ref, dyn_ref, w_ref, d3_ref, dw_ref):
        i = pl.program_id(1)
        uv, bv, cv = u_ref[...].astype(F32), b_ref[...].astype(F32), c_ref[...].astype(F32)
        dyv = dy_ref[...].astype(F32)
        z = cv * uv
        zp = jnp.where(i == 0, 0.0, cp_ref[...].astype(F32) * up_ref[...].astype(F32))
        z1, z2 = _shift_down(zp, z, 1), _shift_down(zp, z, 2)
        w0, w1, w2 = w_ref[0:1, :], w_ref[1:2, :], w_ref[2:3, :]
        conv = w0 * z2 + w1 * z1 + w2 * z
        dconv = dyv * bv
        dconv_n = jnp.where(i == nt - 1, 0.0, dyn_ref[...].astype(F32) * bn_ref[...].astype(F32))
        dz = w2 * dconv + w1 * _shift_up(dconv, dconv_n, 1) + w0 * _shift_up(dconv, dconv_n, 2)
        d3_ref[0] = _bf(dz * cv)
        d3_ref[1] = _bf(dyv * conv)
        d3_ref[2] = _bf(dz * uv)
        dws = [jnp.sum(dconv * zz, axis=0, keepdims=True) for zz in (z2, z1, z)]

        @pl.when(i == 0)
        def _():
            for j in range(3):
                dw_ref[j:j + 1, :] = dws[j]

        @pl.when(i > 0)
        def _():
            for j in range(3):
                dw_ref[j:j + 1, :] += dws[j]

    dy_tile = pl.BlockSpec((None, tb, 128), lambda c, i: (0, i, c))
    dy_next = pl.BlockSpec((None, CONV_HALO, 128), lambda c, i: (0, jnp.minimum((i + 1) * hb, last), c))
    wspec = pl.BlockSpec((3, 128), lambda c, i: (0, c))
    return pl.pallas_call(
        body, name=name, grid=(bw // 128, nt),
        in_specs=[tile(0), tile(1), tile(2), prev(0), prev(2), nxt(1), dy_tile, dy_next, wspec],
        out_specs=[pl.BlockSpec((3, tb, 128), lambda c, i: (0, i, c)), wspec],
        out_shape=[jax.ShapeDtypeStruct((3, t, bw), BF16), jax.ShapeDtypeStruct((3, bw), F32)],
        compiler_params=_cparams("parallel", "arbitrary"),
    )(cols3, cols3, cols3, cols3, cols3, cols3, dy, dy, conv_w)


def _ret_consts():
    log_gamma = jnp.log1p(-jnp.exp2(-5.0 - jnp.arange(H_RET, dtype=F32)))
    pos = jnp.arange(CHUNK, dtype=F32)
    d_intra = jnp.exp(log_gamma[:, None, None] * jnp.abs(pos[:, None] - pos[None, :]))
    q_decay = jnp.exp(log_gamma[:, None] * (pos + 1.0))
    k_decay = jnp.exp(log_gamma[:, None] * (CHUNK - 1.0 - pos))
    chunk_decay = jnp.exp(log_gamma * CHUNK)
    wide = (H_RET, CHUNK, DK_RET)
    return (d_intra, jnp.broadcast_to(q_decay[:, :, None], wide), jnp.broadcast_to(k_decay[:, :, None], wide),
            jnp.broadcast_to(chunk_decay[:, None, None], (H_RET, 1, DK_RET)))


def _rope_tables(t):
    inv_freq = ROPE_BASE ** (-jnp.linspace(0.0, 1.0, DK_RET // 2, dtype=F32))
    ang = jnp.arange(t, dtype=F32)[:, None] * inv_freq[None, :]
    cos, sin = jnp.cos(ang), jnp.sin(ang)
    return jnp.concatenate([cos, cos], axis=1), jnp.concatenate([-sin, sin], axis=1)


def _rope(v, cc, ss):
    return v * cc + pltpu.roll(v, DK_RET // 2, axis=1) * ss


def _ret_in_specs(tb, blk):
    def col(s):
        return pl.BlockSpec((None, tb, H_RET * DK_RET), lambda i: (s, blk(i), 0))

    tab = pl.BlockSpec((tb, DK_RET), lambda i: (blk(i), 0))
    return ([col(3), col(4), col(5), col(6), tab, tab,
             pl.BlockSpec((H_RET, CHUNK, CHUNK), lambda i: (0, 0, 0)),
             pl.BlockSpec((H_RET, CHUNK, DK_RET), lambda i: (0, 0, 0)),
             pl.BlockSpec((H_RET, CHUNK, DK_RET), lambda i: (0, 0, 0)),
             pl.BlockSpec((H_RET, 1, DK_RET), lambda i: (0, 0, 0))])


def _ret_fwd(cols3, tables, consts, *, name):
    _, t, bw = cols3.shape
    tb = _tile(t, RET_TB)
    ncb = tb // CHUNK
    scale = DK_RET ** -0.5

    def body(q_ref, k_ref, v_ref, g_ref, cc_ref, ss_ref, di_ref, qd_ref, kd_ref, cd_ref, y_ref, st_ref, state):
        @pl.when(pl.program_id(0) == 0)
        def _():
            state[...] = jnp.zeros_like(state)

        heads = range(H_RET)
        units = [(c, h) for c in range(ncb) for h in heads]
        every = range(len(units))
        rows = [pl.ds(c * CHUNK, CHUNK) for c, _ in units]
        lanes = [pl.ds(h * DK_RET, DK_RET) for _, h in units]
        hd = [h for _, h in units]
        cc, ss = [cc_ref[rw, :] for rw in rows], [ss_ref[rw, :] for rw in rows]
        qs = [_rope(q_ref[rows[u], lanes[u]].astype(F32), cc[u], ss[u]) * scale for u in every]
        kr = [_rope(k_ref[rows[u], lanes[u]].astype(F32), cc[u], ss[u]) for u in every]
        vb = [v_ref[rows[u], lanes[u]] for u in every]
        gv = [g_ref[rows[u], lanes[u]].astype(F32) for u in every]
        innerb = [_bf(_dot_nt(_bf(qs[u]), _bf(kr[u])) * di_ref[hd[u]]) for u in every]
        update = [_dot_tn(_bf(kr[u] * kd_ref[hd[u]]), vb[u]) for u in every]
        carried = [state[h] for h in heads]
        s_in = []
        for u, (c, h) in enumerate(units):
            s_in.append(carried[h])
            st_ref[h, c] = carried[h]
            carried[h] = carried[h] * cd_ref[h] + update[u]
        for h in heads:
            state[h] = carried[h]
        o = [_dot(innerb[u], vb[u]) + _dot(_bf(qs[u] * qd_ref[hd[u]]), _bf(s_in[u])) for u in every]
        for u in every:
            on = o[u] * lax.rsqrt(jnp.mean(o[u] * o[u], axis=-1, keepdims=True) + EPS)
            y_ref[rows[u], lanes[u]] = _bf(gv[u] * _sigmoid(gv[u]) * on)

    return pl.pallas_call(
        body, name=name, grid=(t // tb,),
        in_specs=_ret_in_specs(tb, lambda i: i),
        out_specs=[pl.BlockSpec((tb, bw), lambda i: (i, 0)),
                   pl.BlockSpec((H_RET, ncb, DK_RET, DK_RET), lambda i: (0, i, 0, 0))],
        out_shape=[jax.ShapeDtypeStruct((t, bw), BF16),
                   jax.ShapeDtypeStruct((H_RET, t // CHUNK, DK_RET, DK_RET), F32)],
        scratch_shapes=[pltpu.VMEM((H_RET, DK_RET, DK_RET), F32)],
        compiler_params=_cparams("arbitrary"),
    )(cols3, cols3, cols3, cols3, tables[0], tables[1], *consts)


def _ret_bwd(cols3, tables, consts, states, dy, *, name, exchange=None):
    _, t, bw = cols3.shape
    tb = _tile(t, RET_TB)
    ncb = tb // CHUNK
    nb = t // tb
    scale = DK_RET ** -0.5

    def body(q_ref, k_ref, v_ref, g_ref, cc_ref, ss_ref, di_ref, qd_ref, kd_ref, cd_ref, st_ref, dy_ref,
             d4_ref, dstate):
        @pl.when(pl.program_id(0) == 0)
        def _():
            dstate[...] = jnp.zeros_like(dstate)

        heads = range(H_RET)
        lanes_of = [pl.ds(h * DK_RET, DK_RET) for h in heads]
        di, qd, kd = [di_ref[h] for h in heads], [qd_ref[h] for h in heads], [kd_ref[h] for h in heads]
        carried = [dstate[h] for h in heads]
        group = RET_GROUP if ncb % RET_GROUP == 0 else 1
        for c0 in reversed(range(0, ncb, group)):
            units = [(c, h) for c in reversed(range(c0, c0 + group)) for h in heads]
            rows = [pl.ds(c * CHUNK, CHUNK) for c, _ in units]
            cc, ss = [cc_ref[rw, :] for rw in rows], [ss_ref[rw, :] for rw in rows]
            every = range(len(units))
            hd = [h for _, h in units]
            vb = [v_ref[rows[u], lanes_of[hd[u]]] for u in every]
            gv = [g_ref[rows[u], lanes_of[hd[u]]].astype(F32) for u in every]
            dyv = [dy_ref[rows[u], lanes_of[hd[u]]].astype(F32) for u in every]
            s_in = [_bf(st_ref[h, c]) for c, h in units]
            qs = [_rope(q_ref[rows[u], lanes_of[hd[u]]].astype(F32), cc[u], ss[u]) * scale for u in every]
            kr = [_rope(k_ref[rows[u], lanes_of[hd[u]]].astype(F32), cc[u], ss[u]) for u in every]
            qsb, krb = [_bf(v) for v in qs], [_bf(v) for v in kr]
            qdb = [_bf(qs[u] * qd[hd[u]]) for u in every]
            kdb = [_bf(kr[u] * kd[hd[u]]) for u in every]
            innerb = [_bf(_dot_nt(qsb[u], krb[u]) * di[hd[u]]) for u in every]
            o = [_dot(innerb[u], vb[u]) + _dot(qdb[u], s_in[u]) for u in every]
            r = [lax.rsqrt(jnp.mean(v * v, axis=-1, keepdims=True) + EPS) for v in o]
            on = [o[u] * r[u] for u in every]
            sg = [_sigmoid(v) for v in gv]
            dgv = [_bf(dyv[u] * on[u] * (sg[u] * (1.0 + gv[u] * (1.0 - sg[u])))) for u in every]
            don = [dyv[u] * (gv[u] * sg[u]) for u in every]
            dob = [_bf(r[u] * (don[u] - on[u] * jnp.mean(don[u] * on[u], axis=-1, keepdims=True))) for u in every]
            dinner = [_bf(_dot_nt(dob[u], vb[u]) * di[hd[u]]) for u in every]
            dqs = [_dot(dinner[u], krb[u]) + _dot_nt(dob[u], s_in[u]) * qd[hd[u]] for u in every]
            dkr = [_dot_tn(dinner[u], qsb[u]) for u in every]
            dv = [_dot_tn(innerb[u], dob[u]) for u in every]
            dnew = [_dot_tn(qdb[u], dob[u]) for u in every]
            for u in every:
                h = hd[u]
                dstb = _bf(carried[h])
                dv[u] = dv[u] + _dot(kdb[u], dstb)
                dkr[u] = dkr[u] + _dot_nt(vb[u], dstb) * kd[h]
                carried[h] = carried[h] * cd_ref[h] + dnew[u]
            for u in every:
                ln = lanes_of[hd[u]]
                d4_ref[0, rows[u], ln] = _bf(_rope_bwd(dqs[u] * scale, cc[u], ss[u]))
                d4_ref[1, rows[u], ln] = _bf(_rope_bwd(dkr[u], cc[u], ss[u]))
                d4_ref[2, rows[u], ln] = _bf(dv[u])
                d4_ref[3, rows[u], ln] = dgv[u]
        for h in heads:
            dstate[h] = carried[h]

    rev = lambda i: nb - 1 - i
    outs, moved = _pcall(
        body, (cols3, cols3, cols3, cols3, tables[0], tables[1], *consts, states, dy), name=name, grid=(nb,),
        in_specs=_ret_in_specs(tb, rev)
        + [pl.BlockSpec((H_RET, ncb, DK_RET, DK_RET), lambda i: (0, rev(i), 0, 0)),
           pl.BlockSpec((None, tb, bw), lambda i: (1, rev(i), 0))],
        out_specs=[pl.BlockSpec((4, tb, bw), lambda i: (0, rev(i), 0))],
        out_shape=[jax.ShapeDtypeStruct((4, t, bw), BF16)],
        scratch_shapes=[pltpu.VMEM((H_RET, DK_RET, DK_RET), F32)],
        sem=("arbitrary",), exchange=exchange)
    return outs[0], moved


def _rope_bwd(dv, cc, ss):
    return dv * cc + pltpu.roll(dv * ss, DK_RET // 2, axis=1)


def _att_window(i):
    return pl.multiple_of(jnp.maximum(i - ATT_LOOKBACK // ATT_QB, 0) * ATT_QB, ATT_QB)


def _att_mask(v):
    qchunk = (v * ATT_QB + lax.broadcasted_iota(jnp.int32, (ATT_QB, ATT_WIN), 0)) // CHUNK
    kchunk = lax.broadcasted_iota(jnp.int32, (ATT_QB, ATT_WIN), 1) // CHUNK
    return (kchunk <= qchunk) & (kchunk >= qchunk - N_PREV_CHUNKS)


def _bias_spec(layer, nvar):
    return pl.BlockSpec((None, None, 2, ATT_QB, ATT_WIN), lambda hp, i: (layer, jnp.minimum(i, nvar), hp, 0, 0))


def _att_fwd(cols3, bias3, *, name, exchange=None):
    _, t, bw = cols3.shape
    assert t % ATT_QB == 0 and t >= ATT_WIN
    scale = DH_ATT ** -0.5
    nvar = ATT_LOOKBACK // ATT_QB

    def body(q_ref, k_ref, v_ref, b_ref, y_ref, lse_ref):
        i = pl.program_id(1)
        ws = _att_window(i)
        q = q_ref[...].astype(F32)
        kw = k_ref[pl.ds(ws, ATT_WIN), :]
        vw = v_ref[pl.ds(ws, ATT_WIN), :]
        lane_head = lax.broadcasted_iota(jnp.int32, (ATT_QB, 128), 1) // DH_ATT
        out = jnp.zeros((ATT_QB, 128), F32)
        lse = jnp.zeros((ATT_QB, 128), F32)
        for hh in range(2):
            mine = lane_head == hh
            s = _dot_nt(_bf(jnp.where(mine, q, 0.0)), kw) * scale + b_ref[hh]
            mx = jnp.max(s, axis=-1, keepdims=True)
            p = jnp.exp(s - mx)
            l = jnp.sum(p, axis=-1, keepdims=True)
            out = jnp.where(mine, _dot(_bf(p), vw) / l, out)
            lse = jnp.where(mine, mx + jnp.log(l), lse)
        y_ref[...] = _bf(out)
        lse_ref[...] = lse

    kv = lambda s: pl.BlockSpec((None, t, 128), lambda hp, i: (s, 0, hp))
    return _pcall(
        body, (cols3, cols3, cols3, bias3[0]), name=name, grid=(H_ATT // 2, t // ATT_QB),
        in_specs=[pl.BlockSpec((None, ATT_QB, 128), lambda hp, i: (7, i, hp)), kv(8), kv(9),
                  _bias_spec(bias3[1], nvar)],
        out_specs=[pl.BlockSpec((ATT_QB, 128), lambda hp, i: (i, hp)),
                   pl.BlockSpec((None, ATT_QB, 128), lambda hp, i: (hp, i, 0))],
        out_shape=[jax.ShapeDtypeStruct((t, bw), BF16), jax.ShapeDtypeStruct((H_ATT // 2, t, 128), F32)],
        scratch_shapes=[], sem=("parallel", "arbitrary"), exchange=exchange)


def _att_bwd(cols3, bias3, y, lse, dy, *, name, exchange=None):
    _, t, bw = cols3.shape
    nq = t // ATT_QB
    scale = DH_ATT ** -0.5
    nvar = ATT_LOOKBACK // ATT_QB

    def body(q_ref, k_ref, v_ref, b_ref, y_ref, lse_ref, dy_ref, d3_ref, db_ref, dk_acc, dv_acc):
        i = pl.program_id(1)

        @pl.when(i == 0)
        def _():
            dk_acc[...] = jnp.zeros_like(dk_acc)
            dv_acc[...] = jnp.zeros_like(dv_acc)

        ws = _att_window(i)
        q = q_ref[...].astype(F32)
        kw = k_ref[pl.ds(ws, ATT_WIN), :]
        vw = v_ref[pl.ds(ws, ATT_WIN), :]
        do = dy_ref[...].astype(F32)
        dof = do * y_ref[...].astype(F32)
        lsev = lse_ref[...]
        lane_head = lax.broadcasted_iota(jnp.int32, (ATT_QB, 128), 1) // DH_ATT
        dq = jnp.zeros((ATT_QB, 128), F32)
        dk = jnp.zeros((ATT_WIN, 128), F32)
        dv = jnp.zeros((ATT_WIN, 128), F32)
        first = i <= nvar
        for hh in range(2):
            mine = lane_head == hh
            qh = _bf(jnp.where(mine, q, 0.0))
            doh = _bf(jnp.where(mine, do, 0.0))
            s = _dot_nt(qh, kw) * scale + b_ref[hh]
            lse_h = jnp.max(jnp.where(mine, lsev, NEG_INF), axis=-1, keepdims=True)
            p = jnp.exp(s - lse_h)
            delta = jnp.sum(jnp.where(mine, dof, 0.0), axis=-1, keepdims=True)
            ds = p * (_dot_nt(doh, vw) - delta)

            @pl.when(first)
            def _():
                db_ref[hh] = ds

            @pl.when(jnp.logical_not(first))
            def _():
                db_ref[hh] += ds

            dsb = _bf(ds * scale)
            dq = jnp.where(mine, _dot(dsb, kw), dq)
            dk = dk + _dot_tn(dsb, qh)
            dv = dv + _dot_tn(_bf(p), doh)
        d3_ref[0, pl.ds(pl.multiple_of(i * ATT_QB, ATT_QB), ATT_QB), :] = _bf(dq)
        dk_acc[pl.ds(ws, ATT_WIN), :] += dk
        dv_acc[pl.ds(ws, ATT_WIN), :] += dv

        @pl.when(i == nq - 1)
        def _():
            d3_ref[1] = _bf(dk_acc[...])
            d3_ref[2] = _bf(dv_acc[...])

    kv = lambda s: pl.BlockSpec((None, t, 128), lambda hp, i: (s, 0, hp))
    qrow = pl.BlockSpec((ATT_QB, 128), lambda hp, i: (i, hp))
    btile = pl.BlockSpec((None, 2, ATT_QB, ATT_WIN), lambda hp, i: (jnp.minimum(i, nvar), hp, 0, 0))
    return _pcall(
        body, (cols3, cols3, cols3, bias3[0], y, lse, dy), name=name, grid=(H_ATT // 2, nq),
        in_specs=[pl.BlockSpec((None, ATT_QB, 128), lambda hp, i: (7, i, hp)), kv(8), kv(9),
                  _bias_spec(bias3[1], nvar), qrow,
                  pl.BlockSpec((None, ATT_QB, 128), lambda hp, i: (hp, i, 0)),
                  pl.BlockSpec((None, ATT_QB, 128), lambda hp, i: (2, i, hp))],
        out_specs=[pl.BlockSpec((3, t, 128), lambda hp, i: (0, 0, hp)), btile],
        out_shape=[jax.ShapeDtypeStruct((3, t, bw), BF16),
                   jax.ShapeDtypeStruct((nvar + 1, H_ATT, ATT_QB, ATT_WIN), F32)],
        scratch_shapes=[pltpu.VMEM((t, 128), F32), pltpu.VMEM((t, 128), F32)],
        sem=("parallel", "arbitrary"), exchange=exchange)


SKEW_W = ATT_WIN + ATT_QB
REL_PAD = 384


def _rel_onehot(v):
    r = lax.broadcasted_iota(jnp.int32, (REL_PAD, SKEW_W), 0)
    j = lax.broadcasted_iota(jnp.int32, (REL_PAD, SKEW_W), 1)
    dist = jnp.where(j < ATT_WIN, v * ATT_QB - j, v * ATT_QB + SKEW_W - j)
    col = jnp.clip(dist, -REL_CLIP, REL_CLIP) + REL_CLIP
    return _bf(jnp.where(col == r, 1.0, 0.0))


def _split3(v):
    hi = _bf(v)
    rest = v - hi.astype(F32)
    mid = _bf(rest)
    return hi, mid, _bf(rest - mid.astype(F32))


def _skew8(a, forward):
    row = lax.broadcasted_iota(jnp.int32, a.shape, 0)
    for b in range(3):
        shift = (1 << b) if forward else SKEW_W - (1 << b)
        a = jnp.where(((row >> b) & 1) == 1, pltpu.roll(a, shift, axis=1), a)
    return a


def _toeplitz_rows(ext_row):
    a = _skew8(jnp.broadcast_to(ext_row, (8, SKEW_W)), True)
    while a.shape[0] < ATT_QB:
        a = jnp.concatenate([a, pltpu.roll(a, a.shape[0], axis=1)], axis=0)
    return a


def _diagonal_sums(tile):
    a = tile
    while a.shape[0] > 8:
        half = a.shape[0] // 2
        a = a[:half] + pltpu.roll(a[half:], SKEW_W - half, axis=1)
    return jnp.sum(_skew8(a, False), axis=0, keepdims=True)


def _bias_tiles(rel_bias, *, name, exchange=None):
    depth = rel_bias.shape[0]
    nvar = ATT_LOOKBACK // ATT_QB + 1
    rel = jnp.pad(rel_bias, ((0, 0), (0, 0), (0, REL_PAD - N_REL)))

    def body(rel_ref, o_ref, ext_ref):
        v = pl.program_id(1)
        onehot = _rel_onehot(v)
        ext_ref[...] = sum(_dot(part, onehot) for part in _split3(rel_ref[...]))
        valid = _att_mask(v)
        for h in range(H_ATT):
            o_ref[h] = jnp.where(valid, _toeplitz_rows(ext_ref[h:h + 1, :])[:, :ATT_WIN], NEG_INF)

    outs, moved = _pcall(
        body, (rel,), name=name, grid=(depth, nvar),
        in_specs=[pl.BlockSpec((None, H_ATT, REL_PAD), lambda l, v: (l, 0, 0))],
        out_specs=[pl.BlockSpec((None, None, H_ATT, ATT_QB, ATT_WIN), lambda l, v: (l, v, 0, 0, 0))],
        out_shape=[jax.ShapeDtypeStruct((depth, nvar, H_ATT, ATT_QB, ATT_WIN), F32)],
        scratch_shapes=[pltpu.VMEM((H_ATT, SKEW_W), F32)], sem=("parallel", "parallel"), exchange=exchange)
    return outs[0], moved


def _rel_bias_grad(dbias3, *, name):
    nvar = dbias3.shape[0]

    def body(db_ref, o_ref, diag_ref):
        v = pl.program_id(0)
        for h in range(H_ATT):
            tile = jnp.concatenate([db_ref[h], jnp.zeros((ATT_QB, ATT_QB), F32)], axis=1)
            diag_ref[h:h + 1, :] = _diagonal_sums(tile)
        onehot = _rel_onehot(v)
        part = sum(_dot_nt(p, onehot) for p in _split3(diag_ref[...]))

        @pl.when(v == 0)
        def _():
            o_ref[...] = part

        @pl.when(v > 0)
        def _():
            o_ref[...] += part

    out = pl.pallas_call(
        body, name=name, grid=(nvar,),
        in_specs=[pl.BlockSpec((None, H_ATT, ATT_QB, ATT_WIN), lambda v: (v, 0, 0, 0))],
        out_specs=pl.BlockSpec((H_ATT, REL_PAD), lambda v: (0, 0)),
        out_shape=jax.ShapeDtypeStruct((H_ATT, REL_PAD), F32),
        scratch_shapes=[pltpu.VMEM((H_ATT, SKEW_W), F32)], compiler_params=_cparams("arbitrary"),
    )(dbias3)
    return out[:, :N_REL]


def _merge_fwd(h, ys, wmg, wb_t, *, name, tm=512):
    t, d = h.shape
    bw = ys[0].shape[1]
    tm = _tile(t, tm)

    def body(h_ref, y0_ref, y1_ref, y2_ref, wg_ref, wb_ref, m_ref, s_ref, p_ref):
        hv = h_ref[...]
        total = jnp.zeros((tm, d), F32)
        for b, y_ref in enumerate((y0_ref, y1_ref, y2_ref)):
            s = _sigmoid(_dot(hv, wg_ref[b]))
            p = _dot_nt(y_ref[...], wb_ref[b])
            s_ref[b] = _bf(s)
            p_ref[b] = _bf(p)
            total = total + s * p
        m_ref[...] = _bf(total)

    row = pl.BlockSpec((tm, d), lambda i: (i, 0))
    yrow = pl.BlockSpec((tm, bw), lambda i: (i, 0))
    three = pl.BlockSpec((3, tm, d), lambda i: (0, i, 0))
    return pl.pallas_call(
        body, name=name, grid=(t // tm,),
        in_specs=[row, yrow, yrow, yrow, pl.BlockSpec((3, d, d), lambda i: (0, 0, 0)),
                  pl.BlockSpec((3, d, bw), lambda i: (0, 0, 0))],
        out_specs=[row, three, three],
        out_shape=[jax.ShapeDtypeStruct((t, d), BF16), jax.ShapeDtypeStruct((3, t, d), BF16),
                   jax.ShapeDtypeStruct((3, t, d), BF16)],
        compiler_params=_cparams("parallel"),
    )(h, *ys, wmg, wb_t)


def _merge_dwb(dp3, ys, *, name, tm=1024, tk=TOKEN_TK):
    nb, t, m = dp3.shape
    n = ys[0].shape[1]
    tm, tk = _tile(m, tm), _tile(t, tk)
    nk = t // tk

    def body(a_ref, *refs):
        o_ref, acc_ref = refs[nb], refs[nb + 1]
        k = pl.program_id(2)

        def finish(total):
            o_ref[...] = total.astype(o_ref.dtype)

        for b in range(nb):
            @pl.when(pl.program_id(0) == b)
            def _(y_ref=refs[b]):
                _accumulate(acc_ref, _dot_tn(a_ref[...], y_ref[...]), k, nk, finish)

    def y_spec(b):
        return pl.BlockSpec((tk, n), lambda s, i, k: (jnp.where(s == b, k, jnp.where(s < b, 0, nk - 1)), 0))

    return pl.pallas_call(
        body, name=name, grid=(nb, m // tm, nk),
        in_specs=[pl.BlockSpec((None, tk, tm), lambda s, i, k: (s, k, i))] + [y_spec(b) for b in range(nb)],
        out_specs=pl.BlockSpec((None, tm, n), lambda s, i, k: (s, i, 0)),
        out_shape=jax.ShapeDtypeStruct((nb, m, n), PARTIAL_DTYPE), scratch_shapes=[pltpu.VMEM((tm, n), F32)],
        compiler_params=_cparams("arbitrary", "arbitrary", "arbitrary"),
    )(dp3, *ys)


def _merge_bwd(dm, s3, p3, wmg, wb_t, *, name, tm=512):
    _, t, d = s3.shape
    bw = wb_t.shape[2]
    tm = _tile(t, tm)

    def body(dm_ref, s_ref, p_ref, wg_ref, wb_ref, dgp_ref, dp_ref, dy_ref, dh_ref):
        dmv = dm_ref[...].astype(F32)
        dh = jnp.zeros((tm, d), F32)
        for b in range(3):
            s = s_ref[b].astype(F32)
            dgp = _bf(dmv * p_ref[b].astype(F32) * s * (1.0 - s))
            dp = _bf(dmv * s)
            dgp_ref[b] = dgp
            dp_ref[b] = dp
            dy_ref[b] = _bf(_dot(dp, wb_ref[b]))
            dh = dh + _dot_nt(dgp, wg_ref[b])
        dh_ref[...] = dh

    row = pl.BlockSpec((tm, d), lambda i: (i, 0))
    three = pl.BlockSpec((3, tm, d), lambda i: (0, i, 0))
    return pl.pallas_call(
        body, name=name, grid=(t // tm,),
        in_specs=[row, three, three, pl.BlockSpec((3, d, d), lambda i: (0, 0, 0)),
                  pl.BlockSpec((3, d, bw), lambda i: (0, 0, 0))],
        out_specs=[three, three, pl.BlockSpec((3, tm, bw), lambda i: (0, i, 0)), row],
        out_shape=[jax.ShapeDtypeStruct((3, t, d), BF16), jax.ShapeDtypeStruct((3, t, d), BF16),
                   jax.ShapeDtypeStruct((3, t, bw), BF16), jax.ShapeDtypeStruct((t, d), F32)],
        compiler_params=_cparams("parallel"),
    )(dm, s3, p3, wmg, wb_t)


def _carried(hooks, slot, state=None):
    if not hooks or slot not in hooks:
        return None, lambda buf: None
    make, done = hooks[slot]
    return make(state), done


def _layer_fwd(x, p, aux, tag, hooks=None):
    ex, done = _carried(hooks, "ffn1")
    (x1, h1, g1, u1, a1), buf = _ffn_fwd(x, p["n1"], p["wg1"], p["wu1"], p["wd1"], name=f"ffn1_fwd_{tag}", exchange=ex)
    done(buf)
    h2 = _rmsnorm_fwd(x1, p["nmix"], name=f"mixnorm_fwd_{tag}")
    ex, done = _carried(hooks, "inproj")
    if ex is None:
        cols3 = _mm_to_slices(h2, p["win"], name=f"inproj_fwd_{tag}")
    else:
        cols3, buf = _mm_to_slices(h2, p["win"], name=f"inproj_fwd_{tag}", exchange=ex)
        done(buf)
    bias3 = p["bias3"]
    y_conv = _conv_fwd(cols3, p["conv_w"], name=f"conv_fwd_{tag}")
    y_ret, states = _ret_fwd(cols3, aux["rope"], aux["ret"], name=f"ret_fwd_{tag}")
    ex, done = _carried(hooks, "att")
    (y_att, lse), buf = _att_fwd(cols3, bias3, name=f"att_fwd_{tag}", exchange=ex)
    done(buf)
    merged, s3, p3 = _merge_fwd(h2, (y_conv, y_ret, y_att), p["wmg"], p["wb"], name=f"merge_fwd_{tag}")
    x2 = _mm_rows(merged, p["wout"], dims=NN, out_dtype=F32, res=x1, name=f"outproj_fwd_{tag}")
    ex, done = _carried(hooks, "ffn2")
    (x3, h3, g2, u2, a2), buf = _ffn_fwd(x2, p["n2"], p["wg2"], p["wu2"], p["wd2"], name=f"ffn2_fwd_{tag}", exchange=ex)
    done(buf)
    saved = dict(x0=x, h1=h1, g1=g1, u1=u1, a1=a1, x1=x1, h2=h2, cols3=cols3, bias3=bias3, y_conv=y_conv,
                 y_ret=y_ret, states=states, y_att=y_att, lse=lse, merged=merged, s3=s3, p3=p3, x2=x2, h3=h3,
                 g2=g2, u2=u2, a2=a2)
    return x3, saved


def _ffn_grads(dxo, x, h, g, u, a, p, which, tag, grads, all_grads, hooks):
    n = which
    ex, done = _carried(hooks, f"ffn{n}_bwd", all_grads)
    (dx, dyb, dg, du, grads["n" + n]), buf = _ffn_bwd(dxo, x, p["n" + n], g, u, p["wg" + n], p["wu" + n], p["wd" + n],
                                                     name=f"ffn{n}_{tag}_bwd", exchange=ex)
    done(buf)
    grads["wd" + n] = _mm_tn(a, dyb, name=f"ffn{n}_{tag}_dwd")
    for key, lhs in (("wg" + n, dg), ("wu" + n, du)):
        ex, done = _carried(hooks, f"ffn{n}_d{key[:2]}", all_grads)
        if ex is None:
            grads[key] = _mm_tn(lhs, h, name=f"ffn{n}_{tag}_d{key[:2]}")
        else:
            grads[key], buf = _mm_tn(lhs, h, name=f"ffn{n}_{tag}_d{key[:2]}", exchange=ex)
            done(buf)
    return dx


def _layer_bwd(dx3, p, s, aux, tag, grads, all_grads, hooks=None):
    dx2 = _ffn_grads(dx3, s["x2"], s["h3"], s["g2"], s["u2"], s["a2"], p, "2", tag, grads, all_grads, hooks)

    dm = _mm_rows(dx2, p["wout"], dims=NT, out_dtype=BF16, name=f"outproj_dm_{tag}")
    grads["wout"] = _mm_tn(s["merged"], dx2, name=f"outproj_dw_{tag}")
    dgp3, dp3, dy3, dh2 = _merge_bwd(dm, s["s3"], s["p3"], p["wmg"], p["wb"], name=f"merge_bwd_{tag}")
    grads["wmg"] = _mm_tn_batch(s["h2"], dgp3, name=f"merge_dwg_{tag}")
    grads["wb"] = _merge_dwb(dp3, (s["y_conv"], s["y_ret"], s["y_att"]), name=f"merge_dwb_{tag}")

    dconv3, grads["conv_w"] = _conv_bwd(s["cols3"], p["conv_w"], dy3, name=f"conv_bwd_{tag}")
    ex, done = _carried(hooks, "ret_bwd", all_grads)
    dret4, buf = _ret_bwd(s["cols3"], aux["rope"], aux["ret"], s["states"], dy3, name=f"ret_bwd_{tag}", exchange=ex)
    done(buf)
    ex, done = _carried(hooks, "att_bwd", all_grads)
    (datt3, dbias3), buf = _att_bwd(s["cols3"], s["bias3"], s["y_att"], s["lse"], dy3, name=f"att_bwd_{tag}",
                                    exchange=ex)
    done(buf)
    grads["rel_bias"] = _rel_bias_grad(dbias3, name=f"bias_grad_{tag}")

    dh2 = _inproj_dh((dconv3, dret4, datt3), p["win"], dh2, name=f"inproj_dh_{tag}")
    grads["win"] = jnp.concatenate([
        _mm_tn_slices(dconv3, s["h2"], name=f"inproj_dw_conv_{tag}"),
        _mm_tn_slices(dret4, s["h2"], name=f"inproj_dw_ret_{tag}"),
        _mm_tn_slices(datt3, s["h2"], name=f"inproj_dw_att_{tag}")], axis=0)
    dx1, grads["nmix"] = _rmsnorm_bwd(dh2, s["x1"], p["nmix"], dx2, name=f"mixnorm_bwd_{tag}")

    return _ffn_grads(dx1, s["x0"], s["h1"], s["g1"], s["u1"], s["a1"], p, "1", tag, grads, all_grads, hooks)


def _device_step(x, target, layers, final_norm, fwd_hooks=None, bwd_hooks=None):
    t = x.shape[0]
    depth = len(layers)
    aux = dict(rope=_rope_tables(t), ret=_ret_consts())
    ex, done = _carried(fwd_hooks[0] if fwd_hooks else None, "start")
    bias_all, buf = _bias_tiles(jnp.stack([p["rel_bias"] for p in layers]), name="bias_tiles", exchange=ex)
    done(buf)
    saved = []
    for l, p in enumerate(layers):
        p["bias3"] = (bias_all, l)
        x, s = _layer_fwd(x, p, aux, f"l{l}", fwd_hooks[l] if fwd_hooks else None)
        saved.append(s)
    dx, dfinal, loss_row = _loss_fwd_bwd(x, final_norm, target, name="loss_fwd_bwd")
    grads = [dict() for _ in range(depth)]
    for l in reversed(range(depth)):
        dx = _layer_bwd(dx, layers[l], saved[l], aux, f"l{l}", grads[l], grads, bwd_hooks[l] if bwd_hooks else None)
    return loss_row, dx, grads, dfinal


def _sum_slots(bufs, *, name, tr=512):
    n, r, cdim = bufs[0].shape
    depth = len(bufs)
    tr = _tile(r, tr)
    nt = r // tr

    def body(*refs):
        o_ref = refs[depth]
        for k in range(depth):
            @pl.when(pl.program_id(0) == k)
            def _(p_ref=refs[k]):
                acc = p_ref[0].astype(F32)
                for s in range(1, n):
                    acc = acc + p_ref[s].astype(F32)
                o_ref[...] = acc

    def spec(k):
        return pl.BlockSpec((n, tr, cdim), lambda l, i: (0, jnp.where(l == k, i, jnp.where(l < k, 0, nt - 1)), 0))

    return pl.pallas_call(
        body, name=name, grid=(depth, nt), in_specs=[spec(k) for k in range(depth)],
        out_specs=pl.BlockSpec((None, tr, cdim), lambda l, i: (l, i, 0)),
        out_shape=jax.ShapeDtypeStruct((depth, r, cdim), F32), compiler_params=_cparams("arbitrary", "arbitrary"),
    )(*bufs)


def _all_reduce_small(v, *, name):
    r = v.shape[0]

    def body(x_ref, o_ref, slots, send_sems, recv_sems):
        x, y, c, me = _my_position()
        slots[me] = x_ref[...]
        sends = []
        for k in range(1, N_DEV):
            peer, _ = _peer(x, y, c, k)
            cp = pltpu.make_async_remote_copy(src_ref=x_ref, dst_ref=slots.at[me], send_sem=send_sems.at[k - 1],
                                              recv_sem=recv_sems.at[k - 1], device_id=peer, device_id_type=MESH)
            cp.start()
            sends.append(cp)
        for k in range(1, N_DEV):
            peer, peer_id = _peer(x, y, c, k)
            pltpu.make_async_remote_copy(src_ref=x_ref, dst_ref=slots.at[peer_id], send_sem=send_sems.at[k - 1],
                                         recv_sem=recv_sems.at[k - 1], device_id=peer, device_id_type=MESH).wait_recv()
        for cp in sends:
            cp.wait_send()
        acc = slots[0]
        for s in range(1, N_DEV):
            acc = acc + slots[s]
        o_ref[...] = acc

    return pl.pallas_call(
        body, name=name, in_specs=[pl.BlockSpec(memory_space=pltpu.VMEM)],
        out_specs=pl.BlockSpec(memory_space=pltpu.VMEM), out_shape=jax.ShapeDtypeStruct((r, 128), F32),
        scratch_shapes=[pltpu.VMEM((N_DEV, r, 128), F32), pltpu.SemaphoreType.DMA((N_DEV - 1,)),
                        pltpu.SemaphoreType.DMA((N_DEV - 1,))],
    )(v)


def _adam_update(wv, gv, mv, vv, d_ref, mo_ref, vo_ref):
    mn = ADAM_B1 * mv + (1.0 - ADAM_B1) * gv
    vn = ADAM_B2 * vv + (1.0 - ADAM_B2) * (gv * gv)
    m_hat = mn / (1.0 - ADAM_B1 ** ADAM_STEP)
    v_hat = vn / (1.0 - ADAM_B2 ** ADAM_STEP)
    d_ref[...] = -ADAM_LR * (m_hat / (jnp.sqrt(v_hat) + ADAM_EPS) + ADAM_WD * wv)
    mo_ref[...] = mn
    vo_ref[...] = vn


def _adamw_of_partials(w, bufs, m, v, *, name, tr=512):
    depth, r, cdim = w.shape
    n = bufs[0].shape[0]
    tr = _tile(r, tr)
    nt = r // tr

    def body(w_ref, m_ref, v_ref, *refs):
        d_ref, mo_ref, vo_ref, g_ref = refs[depth:]
        for k in range(depth):
            @pl.when(pl.program_id(0) == k)
            def _(p_ref=refs[k]):
                gv = p_ref[0].astype(F32)
                for s in range(1, n):
                    gv = gv + p_ref[s].astype(F32)
                g_ref[...] = gv
                _adam_update(w_ref[...], gv, m_ref[...], v_ref[...], d_ref, mo_ref, vo_ref)

    def part_spec(k):
        return pl.BlockSpec((n, tr, cdim), lambda l, i: (0, jnp.where(l == k, i, jnp.where(l < k, 0, nt - 1)), 0))

    spec = pl.BlockSpec((None, tr, cdim), lambda l, i: (l, i, 0))
    return pl.pallas_call(
        body, name=name, grid=(depth, nt), in_specs=[spec] * 3 + [part_spec(k) for k in range(depth)],
        out_specs=[spec] * 4, out_shape=[jax.ShapeDtypeStruct((depth, r, cdim), F32)] * 4,
        compiler_params=_cparams("arbitrary", "arbitrary"),
    )(w, m, v, *bufs)


def _adamw(w, g, m, v, *, name, tr=256):
    shape = w.shape
    cdim = shape[-1]
    w2, g2, m2, v2 = (a.reshape(-1, cdim) for a in (w, g, m, v))
    r = w2.shape[0]
    tr = _tile(r, tr) if r % 8 == 0 else r

    def body(w_ref, g_ref, m_ref, v_ref, d_ref, mo_ref, vo_ref):
        _adam_update(w_ref[...], g_ref[...], m_ref[...], v_ref[...], d_ref, mo_ref, vo_ref)

    spec = pl.BlockSpec((tr, cdim), lambda i: (i, 0))
    outs = pl.pallas_call(
        body, name=name, grid=(r // tr,), in_specs=[spec] * 4, out_specs=[spec] * 3,
        out_shape=[jax.ShapeDtypeStruct((r, cdim), F32)] * 3, compiler_params=_cparams("parallel"),
    )(w2, g2, m2, v2)
    return tuple(o.reshape(shape) for o in outs)


GROUPS = {"f1": ("wg1", "wu1", "wd1"), "in": ("win",), "mg": ("wb", "wmg", "wout"), "f2": ("wg2", "wu2", "wd2")}
BIG = tuple(nm for members in GROUPS.values() for nm in members)
ROWS_DOMAIN = ("wg1", "wu1", "wd1", "wmg", "wout", "wg2", "wu2", "wd2")


def _to_rows(name, w, d):
    depth = w.shape[0]
    if name in ("wg1", "wu1", "wg2", "wu2", "win"):
        return w.transpose(0, 2, 1)
    if name == "wb":
        return w.transpose(0, 1, 3, 2).reshape(depth, -1, d)
    if name == "wmg":
        return w.reshape(depth, -1, d)
    return w


def _from_rows(name, rows, d):
    depth = rows.shape[0]
    if name in ("wg1", "wu1", "wg2", "wu2", "win"):
        return rows.transpose(0, 2, 1)
    if name == "wb":
        return rows.reshape(depth, 3, -1, BRANCH_W).transpose(0, 1, 3, 2)
    if name == "wmg":
        return rows.reshape(depth, 3, -1, d)
    return rows


def _full_from_gathered(name, g, d):
    if name == "wb":
        return g.reshape(N_DEV, 3, -1, BRANCH_W).transpose(1, 0, 2, 3).reshape(3, d, BRANCH_W)
    if name == "wmg":
        return g.reshape(N_DEV, 3, -1, d).transpose(1, 0, 2, 3).reshape(3, d, d)
    return g.reshape(-1, d)


def _gathered_from_full(name, full, d):
    if name == "wb":
        return full.reshape(3, N_DEV, -1, BRANCH_W).transpose(1, 0, 2, 3).reshape(N_DEV, -1, d)
    if name == "wmg":
        return full.reshape(3, N_DEV, -1, d).transpose(1, 0, 2, 3).reshape(N_DEV, -1, d)
    return full.reshape(N_DEV, -1, d)


def kernel(x, ffn1_norm, ffn1_w_gate, ffn1_w_up, ffn1_w_down, mix_norm, w_in, conv_w, rel_bias, w_branch, w_merge_gate, w_out, ffn2_norm, ffn2_w_gate, ffn2_w_up, ffn2_w_down, final_norm, loss_target, m_ffn1_norm, m_ffn1_w_gate, m_ffn1_w_up, m_ffn1_w_down, m_mix_norm, m_w_in, m_conv_w, m_rel_bias, m_w_branch, m_w_merge_gate, m_w_out, m_ffn2_norm, m_ffn2_w_gate, m_ffn2_w_up, m_ffn2_w_down, m_final_norm, v_ffn1_norm, v_ffn1_w_gate, v_ffn1_w_up, v_ffn1_w_down, v_mix_norm, v_w_in, v_conv_w, v_rel_bias, v_w_branch, v_w_merge_gate, v_w_out, v_ffn2_norm, v_ffn2_w_gate, v_ffn2_w_up, v_ffn2_w_down, v_final_norm):
    names = ["ffn1_norm", "ffn1_w_gate", "ffn1_w_up", "ffn1_w_down", "mix_norm", "w_in", "conv_w", "rel_bias",
             "w_branch", "w_merge_gate", "w_out", "ffn2_norm", "ffn2_w_gate", "ffn2_w_up", "ffn2_w_down", "final_norm"]
    weights = dict(zip(names, (ffn1_norm, ffn1_w_gate, ffn1_w_up, ffn1_w_down, mix_norm, w_in, conv_w, rel_bias,
                               w_branch, w_merge_gate, w_out, ffn2_norm, ffn2_w_gate, ffn2_w_up, ffn2_w_down,
                               final_norm)))
    m_in = dict(zip(names, (m_ffn1_norm, m_ffn1_w_gate, m_ffn1_w_up, m_ffn1_w_down, m_mix_norm, m_w_in, m_conv_w,
                            m_rel_bias, m_w_branch, m_w_merge_gate, m_w_out, m_ffn2_norm, m_ffn2_w_gate,
                            m_ffn2_w_up, m_ffn2_w_down, m_final_norm)))
    v_in = dict(zip(names, (v_ffn1_norm, v_ffn1_w_gate, v_ffn1_w_up, v_ffn1_w_down, v_mix_norm, v_w_in, v_conv_w,
                            v_rel_bias, v_w_branch, v_w_merge_gate, v_w_out, v_ffn2_norm, v_ffn2_w_gate,
                            v_ffn2_w_up, v_ffn2_w_down, v_final_norm)))
    big_of = dict(wg1="ffn1_w_gate", wu1="ffn1_w_up", wd1="ffn1_w_down", win="w_in", wb="w_branch",
                  wmg="w_merge_gate", wout="w_out", wg2="ffn2_w_gate", wu2="ffn2_w_up", wd2="ffn2_w_down")
    depth = ffn1_norm.shape[0]
    d = x.shape[-1]
    xs = x.reshape(-1, d)
    target = loss_target.reshape(-1, d)
    _, _, _, me = _my_position()

    shard_rows = {nm: _to_rows(nm, weights[big_of[nm]], d).astype(BF16) for nm in BIG}

    conv_cols = conv_w.shape[-1]
    conv_full, = _exchange("gather", [conv_w.reshape(depth * 3, conv_cols)], name="gather_conv_w")
    conv_full = conv_full.reshape(N_DEV, depth, 3, conv_cols).transpose(1, 2, 0, 3).reshape(depth, 3, -1)
    layers = [dict(n1=ffn1_norm[l][None], nmix=mix_norm[l][None], n2=ffn2_norm[l][None], conv_w=conv_full[l],
                   rel_bias=rel_bias[l]) for l in range(depth)]

    def gather_hook(l, grp, kind="gather"):
        members = GROUPS[grp]

        def done(bufs):
            layers[l].update({nm: _full_from_gathered(nm, buf, d) for nm, buf in zip(members, bufs)})

        return (lambda _: (kind, [shard_rows[nm] for nm in members], [l] * len(members))), done

    fwd_hooks = []
    for l in range(depth):
        hooks = {"ffn1": gather_hook(l, "in"), "inproj": gather_hook(l, "mg"), "att": gather_hook(l, "f2")}
        if l == 0:
            hooks["start"] = gather_hook(0, "f1", "gather2")
        if l + 1 < depth:
            hooks["ffn2"] = gather_hook(l + 1, "f1")
        fwd_hooks.append(hooks)

    received = {}

    def scatter_hook(l, members):
        def make(all_grads):
            return "scatter", [_gathered_from_full(nm, all_grads[l][nm], d) for nm in members], [None] * len(members)

        return make, (lambda bufs: received.update({(l, nm): buf for nm, buf in zip(members, bufs)}))

    bwd_hooks = []
    for l in range(depth):
        hooks = {"att_bwd": scatter_hook(l, GROUPS["f2"]), "ret_bwd": scatter_hook(l, GROUPS["mg"]),
                 "ffn1_bwd": scatter_hook(l, GROUPS["in"])}
        if l + 1 < depth:
            hooks["ffn2_bwd"] = scatter_hook(l + 1, GROUPS["f1"])
        if l == 0:
            hooks["ffn1_dwg"] = scatter_hook(0, ("wd1",))
            hooks["ffn1_dwu"] = scatter_hook(0, ("wg1",))
        bwd_hooks.append(hooks)

    loss_row, dx, grads, dfinal = _device_step(xs, target, layers, final_norm[None], fwd_hooks, bwd_hooks)
    received[(0, "wu1")], = _exchange("scatter", [_gathered_from_full("wu1", grads[0]["wu1"], d)], name="scatter_l0_wu1")

    partials = {nm: [received[(l, nm)] for l in range(depth)] for nm in BIG}
    grad_w = {big_of[nm]: _from_rows(nm, _sum_slots(partials[nm], name=f"sum_grads_{nm}"), d)
              for nm in BIG if nm not in ROWS_DOMAIN}

    small = {"ffn1_norm": jnp.stack([grads[l]["n1"][0] for l in range(depth)]),
             "mix_norm": jnp.stack([grads[l]["nmix"][0] for l in range(depth)]),
             "ffn2_norm": jnp.stack([grads[l]["n2"][0] for l in range(depth)]),
             "final_norm": dfinal[0],
             "rel_bias": jnp.stack([grads[l]["rel_bias"] for l in range(depth)]),
             "conv_w": jnp.stack([grads[l]["conv_w"] for l in range(depth)]),
             "loss": loss_row[0, :1]}
    order = list(small)
    flat = jnp.concatenate([small[k].reshape(-1) for k in order])
    pad = (-flat.shape[0]) % 1024
    summed = _all_reduce_small(jnp.pad(flat, (0, pad)).reshape(-1, 128), name="reduce_small").reshape(-1)
    pos = 0
    for k in order:
        n = small[k].size
        small[k] = summed[pos:pos + n].reshape(small[k].shape)
        pos += n
    small["conv_w"] = lax.dynamic_slice_in_dim(small["conv_w"], me * conv_cols, conv_cols, axis=2)

    grad_w.update({k: small[k] for k in order if k != "loss"})
    delta, new_m, new_v = {}, {}, {}
    short = {v: k for k, v in big_of.items()}
    for nm in names:
        key = short.get(nm)
        if key in ROWS_DOMAIN:
            outs = _adamw_of_partials(_to_rows(key, weights[nm], d), partials[key], _to_rows(key, m_in[nm], d),
                                      _to_rows(key, v_in[nm], d), name=f"adamw_{nm}")
            delta[nm], new_m[nm], new_v[nm], grad_w[nm] = (_from_rows(key, o, d) for o in outs)
        else:
            delta[nm], new_m[nm], new_v[nm] = _adamw(weights[nm], grad_w[nm], m_in[nm], v_in[nm], name=f"adamw_{nm}")
    return (small["loss"].reshape(()), dx.reshape(x.shape), *[grad_w[n] for n in names], *[delta[n] for n in names],
            *[new_m[n] for n in names], *[new_v[n] for n in names])
```

```python
import functools
import math

import jax
import jax.numpy as jnp
import numpy as np
from jax import lax
from jax.experimental import pallas as pl
from jax.experimental.pallas import tpu as pltpu

F32 = jnp.float32
BF16 = jnp.bfloat16

N_DEV = 8
EPS = 1e-6
CHUNK = 64
BRANCH_W = 512
N_SLICES = 10
H_RET = 4
DK_RET = 128
H_ATT = 8
DH_ATT = 64
N_PREV_CHUNKS = 8
REL_CLIP = 128
N_REL = 2 * REL_CLIP + 1
NEG_INF = -1e30
ROPE_BASE = 10000.0
ATT_QB = 256
ATT_LOOKBACK = N_PREV_CHUNKS * CHUNK
ATT_WIN = ATT_LOOKBACK + ATT_QB
RET_TB = 256
RET_GROUP = 4
TOKEN_TK = 2048
PARTIAL_DTYPE = jnp.bfloat16
CONV_HALO = 16

ADAM_LR = 0.001
ADAM_B1 = 0.9
ADAM_B2 = 0.999
ADAM_EPS = 1e-08
ADAM_WD = 0.01
ADAM_STEP = 10

VMEM_LIMIT_BYTES = 56 * 1024 * 1024
MESH = pl.DeviceIdType.MESH


def _cparams(*sem):
    return pltpu.CompilerParams(dimension_semantics=sem, vmem_limit_bytes=VMEM_LIMIT_BYTES)


def _dot(a, b):
    return lax.dot_general(a, b, (((1,), (0,)), ((), ())), preferred_element_type=F32)


def _dot_nt(a, b):
    return lax.dot_general(a, b, (((1,), (1,)), ((), ())), preferred_element_type=F32)


def _dot_tn(a, b):
    return lax.dot_general(a, b, (((0,), (0,)), ((), ())), preferred_element_type=F32)


def _bf(v):
    return v.astype(BF16)


def _sigmoid(v):
    return 1.0 / (1.0 + jnp.exp(-v))


def _tile(n, want):
    if n <= want:
        return n
    for t in range(want - want % 128, 0, -128):
        if n % t == 0:
            return t
    t = want
    while n % t:
        t //= 2
    return t


def _my_position():
    x, y, c = lax.axis_index("x"), lax.axis_index("y"), lax.axis_index("c")
    return x, y, c, 4 * x + 2 * y + c


def _peer(x, y, c, k):
    px = 1 - x if k & 4 else x
    py = 1 - y if k & 2 else y
    pc = 1 - c if k & 1 else c
    return (px, py, pc), 4 * px + 2 * py + pc


DIRECT2 = (1, 2, 4, 6)
PASSED_ON = (2, 4, 6)
N_PUSHES = N_DEV - 1 + len(PASSED_ON)


def _exchange_sems(n):
    return [pltpu.SemaphoreType.DMA((n, N_PUSHES)), pltpu.SemaphoreType.DMA((n, N_PUSHES)),
            pltpu.SemaphoreType.DMA((n,))]


def _exchange_src(kind, src_ref, layer, block):
    if kind in ("gather", "gather2"):
        return src_ref if layer is None else src_ref.at[layer]
    return src_ref.at[block]


def _passed_on(dst_ref, sems, a, j, pos, incoming):
    x, y, c, _ = pos
    sibling, _ = _peer(x, y, c, 1)
    _, block = _peer(x, y, c, PASSED_ON[j] | 1 if incoming else PASSED_ON[j])
    return pltpu.make_async_remote_copy(src_ref=dst_ref.at[block], dst_ref=dst_ref.at[block],
                                        send_sem=sems[0].at[a, N_DEV - 1 + j], recv_sem=sems[1].at[a, N_DEV - 1 + j],
                                        device_id=sibling, device_id_type=MESH)


def _exchange_copy(kind, src_ref, layer, dst_ref, sems, a, k, pos, incoming):
    x, y, c, me = pos
    peer, peer_id = _peer(x, y, c, k)
    return pltpu.make_async_remote_copy(src_ref=_exchange_src(kind, src_ref, layer, me if incoming else peer_id),
                                        dst_ref=dst_ref.at[peer_id if incoming else me],
                                        send_sem=sems[0].at[a, k - 1], recv_sem=sems[1].at[a, k - 1],
                                        device_id=peer, device_id_type=MESH)


def _exchange_local(kind, src_ref, layer, dst_ref, sems, a, me):
    return pltpu.make_async_copy(_exchange_src(kind, src_ref, layer, me), dst_ref.at[me], sems[2].at[a])


def _exchange_start(kind, layers, src_refs, dst_refs, sems):
    pos = _my_position()
    direct = DIRECT2 if kind == "gather2" else range(1, N_DEV)
    for a, (src_ref, layer, dst_ref) in enumerate(zip(src_refs, layers, dst_refs)):
        _exchange_local(kind, src_ref, layer, dst_ref, sems, a, pos[3]).start()
        for k in direct:
            _exchange_copy(kind, src_ref, layer, dst_ref, sems, a, k, pos, False).start()


def _exchange_wait(kind, layers, src_refs, dst_refs, sems):
    pos = _my_position()
    for a, (src_ref, layer, dst_ref) in enumerate(zip(src_refs, layers, dst_refs)):
        if kind == "gather2":
            for j, k in enumerate(PASSED_ON):
                _exchange_copy(kind, src_ref, layer, dst_ref, sems, a, k, pos, True).wait_recv()
                _passed_on(dst_ref, sems, a, j, pos, False).start()
            _exchange_copy(kind, src_ref, layer, dst_ref, sems, a, 1, pos, True).wait_recv()
            for j in range(len(PASSED_ON)):
                _passed_on(dst_ref, sems, a, j, pos, True).wait_recv()
            for k in DIRECT2:
                _exchange_copy(kind, src_ref, layer, dst_ref, sems, a, k, pos, False).wait_send()
            for j in range(len(PASSED_ON)):
                _passed_on(dst_ref, sems, a, j, pos, False).wait_send()
        else:
            for k in range(1, N_DEV):
                _exchange_copy(kind, src_ref, layer, dst_ref, sems, a, k, pos, True).wait_recv()
            for k in range(1, N_DEV):
                _exchange_copy(kind, src_ref, layer, dst_ref, sems, a, k, pos, False).wait_send()
        _exchange_local(kind, src_ref, layer, dst_ref, sems, a, pos[3]).wait()


def _exchange_shapes(kind, srcs, layers):
    if kind in ("gather", "gather2"):
        return [jax.ShapeDtypeStruct((N_DEV,) + (s.shape if l is None else s.shape[1:]), s.dtype)
                for s, l in zip(srcs, layers)]
    return [jax.ShapeDtypeStruct(s.shape, s.dtype) for s in srcs]


def _exchange(kind, srcs, layers=None, *, name):
    n = len(srcs)
    layers = layers or [None] * n
    hbm = pl.BlockSpec(memory_space=pl.ANY)

    def body(*refs):
        src_refs, dst_refs, sems = refs[:n], refs[n:2 * n], refs[2 * n:]
        _exchange_start(kind, layers, src_refs, dst_refs, sems)
        _exchange_wait(kind, layers, src_refs, dst_refs, sems)

    return pl.pallas_call(body, name=name, in_specs=[hbm] * n, out_specs=[hbm] * n,
                          out_shape=_exchange_shapes(kind, srcs, layers), scratch_shapes=_exchange_sems(n))(*srcs)


def _pcall(body, args, *, name, grid, in_specs, out_specs, out_shape, scratch_shapes, sem, exchange=None):
    if exchange is None:
        outs = pl.pallas_call(body, name=name, grid=grid, in_specs=in_specs, out_specs=out_specs, out_shape=out_shape,
                              scratch_shapes=scratch_shapes, compiler_params=_cparams(*sem))(*args)
        return outs, None
    kind, srcs, layers = exchange
    n_in, n_out, n_scr, n_x = len(in_specs), len(out_specs), len(scratch_shapes), len(srcs)
    hbm = pl.BlockSpec(memory_space=pl.ANY)

    def carrier(*refs):
        ins, refs = refs[:n_in], refs[n_in:]
        src_refs, refs = refs[:n_x], refs[n_x:]
        outs, refs = refs[:n_out], refs[n_out:]
        dst_refs, refs = refs[:n_x], refs[n_x:]
        scr, sems = refs[:n_scr], refs[n_scr:]
        ids = [pl.program_id(a) for a in range(len(grid))]
        first = functools.reduce(jnp.logical_and, [i == 0 for i in ids])
        last = functools.reduce(jnp.logical_and, [i == g - 1 for i, g in zip(ids, grid)])

        @pl.when(first)
        def _():
            _exchange_start(kind, layers, src_refs, dst_refs, sems)

        body(*ins, *outs, *scr)

        @pl.when(last)
        def _():
            _exchange_wait(kind, layers, src_refs, dst_refs, sems)

    outs = pl.pallas_call(
        carrier, name=name, grid=grid, in_specs=list(in_specs) + [hbm] * n_x, out_specs=list(out_specs) + [hbm] * n_x,
        out_shape=list(out_shape) + _exchange_shapes(kind, srcs, layers),
        scratch_shapes=list(scratch_shapes) + _exchange_sems(n_x),
        compiler_params=_cparams(*(["arbitrary"] * len(grid))))(*args, *srcs)
    return outs[:n_out], outs[n_out:]


def _accumulate(acc_ref, part, k, nk, finish):
    if nk == 1:
        finish(part)
        return

    @pl.when(k == 0)
    def _():
        acc_ref[...] = part

    @pl.when(jnp.logical_and(k > 0, k < nk - 1))
    def _():
        acc_ref[...] += part

    @pl.when(k == nk - 1)
    def _():
        finish(acc_ref[...] + part)


def _matmul(a, b, *, dims, grid, a_spec, b_spec, o_spec, out_shape, acc_shape, name, scale=1.0,
            res=None, res_spec=None, exchange=None):
    nk = grid[3]
    has_res = res is not None

    def body(*refs):
        if has_res:
            a_ref, b_ref, r_ref, o_ref = refs[:4]
        else:
            a_ref, b_ref, o_ref = refs[:3]
            r_ref = None

        def finish(acc):
            if scale != 1.0:
                acc = acc * scale
            if r_ref is not None:
                acc = acc + r_ref[...].astype(F32)
            o_ref[...] = acc.astype(o_ref.dtype)

        part = lax.dot_general(_bf(a_ref[...]), _bf(b_ref[...]), (dims, ((), ())), preferred_element_type=F32)
        _accumulate(None if nk == 1 else refs[-1], part, pl.program_id(3), nk, finish)

    in_specs = [a_spec, b_spec] + ([res_spec] if has_res else [])
    args = (a, b) + ((res,) if has_res else ())
    outs, moved = _pcall(
        body, args, name=name, grid=grid, in_specs=in_specs, out_specs=[o_spec], out_shape=[out_shape],
        scratch_shapes=[] if nk == 1 else [pltpu.VMEM(acc_shape, F32)],
        sem=("parallel", "parallel", "parallel", "arbitrary"), exchange=exchange)
    return outs[0] if exchange is None else (outs[0], moved)


NT = ((1,), (1,))
NN = ((1,), (0,))
TN = ((0,), (0,))


def _mm_rows(a, b, *, dims, out_dtype, name, res=None, scale=1.0, tm=1024, tn=1024, tk=1024):
    m, kdim = a.shape
    n = b.shape[1] if dims == NN else b.shape[0]
    tm, tn, tk = _tile(m, tm), _tile(n, tn), _tile(kdim, tk)
    grid = (1, m // tm, n // tn, kdim // tk)
    a_spec = pl.BlockSpec((tm, tk), lambda s, i, j, k: (i, k))
    if dims == NN:
        b_spec = pl.BlockSpec((tk, tn), lambda s, i, j, k: (k, j))
    else:
        b_spec = pl.BlockSpec((tn, tk), lambda s, i, j, k: (j, k))
    o_spec = pl.BlockSpec((tm, tn), lambda s, i, j, k: (i, j))
    return _matmul(a, b, dims=dims, grid=grid, a_spec=a_spec, b_spec=b_spec, o_spec=o_spec,
                   out_shape=jax.ShapeDtypeStruct((m, n), out_dtype), acc_shape=(tm, tn), name=name,
                   res=res, res_spec=o_spec if res is not None else None, scale=scale)


def _mm_to_slices(a, b_t, *, name, tm=2048, exchange=None):
    m, kdim = a.shape
    n = b_t.shape[0]
    tm = _tile(m, tm)
    w = BRANCH_W
    grid = (1, m // tm, n // w, 1)
    return _matmul(a, b_t, dims=NT, grid=grid,
                   a_spec=pl.BlockSpec((tm, kdim), lambda s, i, j, k: (i, 0)),
                   b_spec=pl.BlockSpec((w, kdim), lambda s, i, j, k: (j, 0)),
                   o_spec=pl.BlockSpec((None, tm, w), lambda s, i, j, k: (j, i, 0)),
                   out_shape=jax.ShapeDtypeStruct((n // w, m, w), BF16), acc_shape=(tm, w), name=name,
                   exchange=exchange)


def _inproj_dh(parts, w_t, res, *, name, tm=512):
    t, n = res.shape
    w = BRANCH_W
    tm = _tile(t, tm)
    counts = [p.shape[0] for p in parts]
    assert sum(counts) * w == w_t.shape[0]

    def body(*refs):
        w_ref, r_ref, o_ref = refs[len(parts):]
        cols = jnp.concatenate([p_ref[s] for p_ref, cnt in zip(refs, counts) for s in range(cnt)], axis=1)
        o_ref[...] = r_ref[...] + _dot(cols, w_ref[...])

    row = pl.BlockSpec((tm, n), lambda i: (i, 0))
    return pl.pallas_call(
        body, name=name, grid=(t // tm,),
        in_specs=[pl.BlockSpec((cnt, tm, w), lambda i: (0, i, 0)) for cnt in counts]
        + [pl.BlockSpec(w_t.shape, lambda i: (0, 0)), row],
        out_specs=row, out_shape=jax.ShapeDtypeStruct((t, n), F32), compiler_params=_cparams("parallel"),
    )(*parts, w_t, res)


def _mm_tn(a, b, *, name, tm=1408, tn=1024, tk=TOKEN_TK, exchange=None):
    t, m = a.shape
    n = b.shape[1]
    tm, tn, tk = _tile(m, tm), _tile(n, tn), _tile(t, tk)
    grid = (1, m // tm, n // tn, t // tk)
    return _matmul(a, b, dims=TN, grid=grid,
                   a_spec=pl.BlockSpec((tk, tm), lambda s, i, j, k: (k, i)),
                   b_spec=pl.BlockSpec((tk, tn), lambda s, i, j, k: (k, j)),
                   o_spec=pl.BlockSpec((tm, tn), lambda s, i, j, k: (i, j)),
                   out_shape=jax.ShapeDtypeStruct((m, n), PARTIAL_DTYPE), acc_shape=(tm, tn), name=name,
                   exchange=exchange)


def _mm_tn_slices(a3, b, *, name, tn=1024, tk=TOKEN_TK):
    s_n, t, w = a3.shape
    n = b.shape[1]
    tn, tk = _tile(n, tn), _tile(t, tk)
    grid = (1, s_n, n // tn, t // tk)
    return _matmul(a3, b, dims=TN, grid=grid,
                   a_spec=pl.BlockSpec((None, tk, w), lambda s, i, j, k: (i, k, 0)),
                   b_spec=pl.BlockSpec((tk, tn), lambda s, i, j, k: (k, j)),
                   o_spec=pl.BlockSpec((w, tn), lambda s, i, j, k: (i, j)),
                   out_shape=jax.ShapeDtypeStruct((s_n * w, n), PARTIAL_DTYPE), acc_shape=(w, tn), name=name)


def _mm_tn_batch(a, b3, *, name, tm=1024, tn=1024, tk=TOKEN_TK):
    s_n, t, n = b3.shape
    m = a.shape[-1]
    tm, tn, tk = _tile(m, tm), _tile(n, tn), _tile(t, tk)
    grid = (s_n, m // tm, n // tn, t // tk)
    return _matmul(a, b3, dims=TN, grid=grid, a_spec=pl.BlockSpec((tk, tm), lambda s, i, j, k: (k, i)),
                   b_spec=pl.BlockSpec((None, tk, tn), lambda s, i, j, k: (s, k, j)),
                   o_spec=pl.BlockSpec((None, tm, tn), lambda s, i, j, k: (s, i, j)),
                   out_shape=jax.ShapeDtypeStruct((s_n, m, n), PARTIAL_DTYPE), acc_shape=(tm, tn), name=name)


def _norm_parts(xf):
    r = lax.rsqrt(jnp.mean(xf * xf, axis=-1, keepdims=True) + EPS)
    return xf * r, r


def _norm_bwd(dh, xhat, r, w):
    dxhat = dh * w
    dx = r * (dxhat - xhat * jnp.mean(dxhat * xhat, axis=-1, keepdims=True))
    return dx, jnp.sum(dh * xhat, axis=0, keepdims=True)


def _rmsnorm_fwd(x, w, *, name, tm=1024):
    t, d = x.shape
    tm = _tile(t, tm)

    def body(x_ref, w_ref, h_ref):
        xhat, _ = _norm_parts(x_ref[...])
        h_ref[...] = _bf(xhat * w_ref[...])

    return pl.pallas_call(
        body, name=name, grid=(t // tm,),
        in_specs=[pl.BlockSpec((tm, d), lambda i: (i, 0)), pl.BlockSpec((1, d), lambda i: (0, 0))],
        out_specs=pl.BlockSpec((tm, d), lambda i: (i, 0)),
        out_shape=jax.ShapeDtypeStruct((t, d), BF16), compiler_params=_cparams("parallel"),
    )(x, w)


def _rmsnorm_bwd(dh, x, w, dres, *, name, tm=1024):
    t, d = x.shape
    tm = _tile(t, tm)

    def body(dh_ref, x_ref, w_ref, dres_ref, dx_ref, dw_ref):
        xhat, r = _norm_parts(x_ref[...])
        dx, dw = _norm_bwd(dh_ref[...], xhat, r, w_ref[...])
        dx_ref[...] = dres_ref[...] + dx

        @pl.when(pl.program_id(0) == 0)
        def _():
            dw_ref[...] = dw

        @pl.when(pl.program_id(0) > 0)
        def _():
            dw_ref[...] += dw

    row = pl.BlockSpec((tm, d), lambda i: (i, 0))
    vec = pl.BlockSpec((1, d), lambda i: (0, 0))
    return pl.pallas_call(
        body, name=name, grid=(t // tm,), in_specs=[row, row, vec, row], out_specs=[row, vec],
        out_shape=[jax.ShapeDtypeStruct((t, d), F32), jax.ShapeDtypeStruct((1, d), F32)],
        compiler_params=_cparams("arbitrary"),
    )(dh, x, w, dres)


def _loss_fwd_bwd(x, w, target, *, name, tm=1024):
    t, d = x.shape
    tm = _tile(t, tm)

    def body(x_ref, w_ref, t_ref, dx_ref, dw_ref, loss_ref):
        xhat, r = _norm_parts(x_ref[...])
        wv = w_ref[...]
        err = xhat * wv - t_ref[...]
        dx, dw = _norm_bwd(err * (1.0 / d), xhat, r, wv)
        dx_ref[...] = dx
        part = jnp.full((1, 128), 0.5 / d, F32) * jnp.sum(err * err)

        @pl.when(pl.program_id(0) == 0)
        def _():
            dw_ref[...] = dw
            loss_ref[...] = part

        @pl.when(pl.program_id(0) > 0)
        def _():
            dw_ref[...] += dw
            loss_ref[...] += part

    row = pl.BlockSpec((tm, d), lambda i: (i, 0))
    vec = pl.BlockSpec((1, d), lambda i: (0, 0))
    return pl.pallas_call(
        body, name=name, grid=(t // tm,), in_specs=[row, vec, row],
        out_specs=[row, vec, pl.BlockSpec((1, 128), lambda i: (0, 0))],
        out_shape=[jax.ShapeDtypeStruct((t, d), F32), jax.ShapeDtypeStruct((1, d), F32),
                   jax.ShapeDtypeStruct((1, 128), F32)],
        compiler_params=_cparams("arbitrary"),
    )(x, w, target)


def _ffn_tiles(t, f):
    tf = f
    for cand in (1408, 1024, 512, 256, 128):
        if f % cand == 0:
            tf = cand
            break
    return _tile(t, 512), tf


def _ffn_fwd(x, nw, wg_t, wu_t, wd, *, name, exchange=None):
    t, d = x.shape
    f = wd.shape[0]
    tm, tf = _ffn_tiles(t, f)
    nf = f // tf

    def body(x_ref, nw_ref, wg_ref, wu_ref, wd_ref, xo_ref, h_ref, g_ref, u_ref, a_ref, hs_ref, acc_ref):
        j = pl.program_id(1)

        @pl.when(j == 0)
        def _():
            xhat, _ = _norm_parts(x_ref[...])
            hb = _bf(xhat * nw_ref[...])
            hs_ref[...] = hb
            h_ref[...] = hb

        hb = hs_ref[...]
        g = _dot_nt(hb, wg_ref[...])
        u = _dot_nt(hb, wu_ref[...])
        a = _bf(g * _sigmoid(g) * u)
        g_ref[...] = _bf(g)
        u_ref[...] = _bf(u)
        a_ref[...] = a
        part = _dot(a, wd_ref[...])

        def finish(total):
            xo_ref[...] = x_ref[...] + 0.5 * total

        _accumulate(acc_ref, part, j, nf, finish)

    row = pl.BlockSpec((tm, d), lambda i, j: (i, 0))
    wspec = pl.BlockSpec((tf, d), lambda i, j: (j, 0))
    hid = pl.BlockSpec((tm, tf), lambda i, j: (i, j))
    return _pcall(
        body, (x, nw, wg_t, wu_t, wd), name=name, grid=(t // tm, nf),
        in_specs=[row, pl.BlockSpec((1, d), lambda i, j: (0, 0)), wspec, wspec, wspec],
        out_specs=[row, row, hid, hid, hid],
        out_shape=[jax.ShapeDtypeStruct((t, d), F32), jax.ShapeDtypeStruct((t, d), BF16)]
        + [jax.ShapeDtypeStruct((t, f), BF16)] * 3,
        scratch_shapes=[pltpu.VMEM((tm, d), BF16), pltpu.VMEM((tm, d), F32)],
        sem=("parallel", "arbitrary"), exchange=exchange)


def _ffn_bwd(dxo, x, nw, g, u, wg_t, wu_t, wd, *, name, exchange=None):
    t, d = x.shape
    f = wd.shape[0]
    tm, tf = _ffn_tiles(t, f)
    nf = f // tf

    def body(dxo_ref, x_ref, nw_ref, g_ref, u_ref, wg_ref, wu_ref, wd_ref,
             dx_ref, dy_ref, dg_ref, du_ref, dnw_ref, dys_ref, acc_ref):
        i, j = pl.program_id(0), pl.program_id(1)

        @pl.when(j == 0)
        def _():
            dyb = _bf(0.5 * dxo_ref[...])
            dys_ref[...] = dyb
            dy_ref[...] = dyb

        da = _dot_nt(dys_ref[...], wd_ref[...])
        gv = g_ref[...].astype(F32)
        uv = u_ref[...].astype(F32)
        s = _sigmoid(gv)
        dg = _bf(da * uv * (s * (1.0 + gv * (1.0 - s))))
        du = _bf(da * (gv * s))
        dg_ref[...] = dg
        du_ref[...] = du
        part = _dot(dg, wg_ref[...]) + _dot(du, wu_ref[...])

        def finish(dh):
            xhat, r = _norm_parts(x_ref[...])
            dx, dw = _norm_bwd(dh, xhat, r, nw_ref[...])
            dx_ref[...] = dxo_ref[...] + dx

            @pl.when(i == 0)
            def _():
                dnw_ref[...] = dw

            @pl.when(i > 0)
            def _():
                dnw_ref[...] += dw

        _accumulate(acc_ref, part, j, nf, finish)

    row = pl.BlockSpec((tm, d), lambda i, j: (i, 0))
    vec = pl.BlockSpec((1, d), lambda i, j: (0, 0))
    wspec = pl.BlockSpec((tf, d), lambda i, j: (j, 0))
    hid = pl.BlockSpec((tm, tf), lambda i, j: (i, j))
    return _pcall(
        body, (dxo, x, nw, g, u, wg_t, wu_t, wd), name=name, grid=(t // tm, nf),
        in_specs=[row, row, vec, hid, hid, wspec, wspec, wspec],
        out_specs=[row, row, hid, hid, vec],
        out_shape=[jax.ShapeDtypeStruct((t, d), F32), jax.ShapeDtypeStruct((t, d), BF16),
                   jax.ShapeDtypeStruct((t, f), BF16), jax.ShapeDtypeStruct((t, f), BF16),
                   jax.ShapeDtypeStruct((1, d), F32)],
        scratch_shapes=[pltpu.VMEM((tm, d), BF16), pltpu.VMEM((tm, d), F32)],
        sem=("arbitrary", "arbitrary"), exchange=exchange)


def _shift_down(prev, cur, n):
    ext = jnp.concatenate([prev, cur], axis=0)
    return pltpu.roll(ext, n, axis=0)[prev.shape[0]:]


def _shift_up(cur, nxt, n):
    ext = jnp.concatenate([cur, nxt], axis=0)
    return pltpu.roll(ext, ext.shape[0] - n, axis=0)[:cur.shape[0]]


def _conv_specs(t, tb):
    hb = tb // CONV_HALO
    last = t // CONV_HALO - 1

    def tile(s):
        return pl.BlockSpec((None, tb, 128), lambda c, i: (s, i, c))

    def prev(s):
        return pl.BlockSpec((None, CONV_HALO, 128), lambda c, i: (s, jnp.maximum(i * hb - 1, 0), c))

    def nxt(s):
        return pl.BlockSpec((None, CONV_HALO, 128), lambda c, i: (s, jnp.minimum((i + 1) * hb, last), c))

    return tile, prev, nxt


def _conv_fwd(cols3, conv_w, *, name, tb=1024):
    _, t, bw = cols3.shape
    tb = _tile(t, tb)
    tile, prev, _ = _conv_specs(t, tb)

    def body(u_ref, b_ref, c_ref, up_ref, cp_ref, w_ref, y_ref):
        first = pl.program_id(1) == 0
        z = c_ref[...].astype(F32) * u_ref[...].astype(F32)
        zp = jnp.where(first, 0.0, cp_ref[...].astype(F32) * up_ref[...].astype(F32))
        conv = w_ref[0:1, :] * _shift_down(zp, z, 2) + w_ref[1:2, :] * _shift_down(zp, z, 1) + w_ref[2:3, :] * z
        y_ref[...] = _bf(b_ref[...].astype(F32) * conv)

    return pl.pallas_call(
        body, name=name, grid=(bw // 128, t // tb),
        in_specs=[tile(0), tile(1), tile(2), prev(0), prev(2), pl.BlockSpec((3, 128), lambda c, i: (0, c))],
        out_specs=pl.BlockSpec((tb, 128), lambda c, i: (i, c)),
        out_shape=jax.ShapeDtypeStruct((t, bw), BF16), compiler_params=_cparams("parallel", "parallel"),
    )(cols3, cols3, cols3, cols3, cols3, conv_w)


def _conv_bwd(cols3, conv_w, dy, *, name, tb=1024):
    _, t, bw = cols3.shape
    tb = _tile(t, tb)
    nt = t // tb
    tile, prev, nxt = _conv_specs(t, tb)
    hb = tb // CONV_HALO
    last = t // CONV_HALO - 1

    def body(u_ref, b_ref, c_ref, up_ref, cp_ref, bn_ref, dy_ref, dyn_ref, w_ref, d3_ref, dw_ref):
        i = pl.program_id(1)
        uv, bv, cv = u_ref[...].astype(F32), b_ref[...].astype(F32), c_ref[...].astype(F32)
        dyv = dy_ref[...].astype(F32)
        z = cv * uv
        zp = jnp.where(i == 0, 0.0, cp_ref[...].astype(F32) * up_ref[...].astype(F32))
        z1, z2 = _shift_down(zp, z, 1), _shift_down(zp, z, 2)
        w0, w1, w2 = w_ref[0:1, :], w_ref[1:2, :], w_ref[2:3, :]
        conv = w0 * z2 + w1 * z1 + w2 * z
        dconv = dyv * bv
        dconv_n = jnp.where(i == nt - 1, 0.0, dyn_ref[...].astype(F32) * bn_ref[...].astype(F32))
        dz = w2 * dconv + w1 * _shift_up(dconv, dconv_n, 1) + w0 * _shift_up(dconv, dconv_n, 2)
        d3_ref[0] = _bf(dz * cv)
        d3_ref[1] = _bf(dyv * conv)
        d3_ref[2] = _bf(dz * uv)
        dws = [jnp.sum(dconv * zz, axis=0, keepdims=True) for zz in (z2, z1, z)]

        @pl.when(i == 0)
        def _():
            for j in range(3):
                dw_ref[j:j + 1, :] = dws[j]

        @pl.when(i > 0)
        def _():
            for j in range(3):
                dw_ref[j:j + 1, :] += dws[j]

    dy_tile = pl.BlockSpec((None, tb, 128), lambda c, i: (0, i, c))
    dy_next = pl.BlockSpec((None, CONV_HALO, 128), lambda c, i: (0, jnp.minimum((i + 1) * hb, last), c))
    wspec = pl.BlockSpec((3, 128), lambda c, i: (0, c))
    return pl.pallas_call(
        body, name=name, grid=(bw // 128, nt),
        in_specs=[tile(0), tile(1), tile(2), prev(0), prev(2), nxt(1), dy_tile, dy_next, wspec],
        out_specs=[pl.BlockSpec((3, tb, 128), lambda c, i: (0, i, c)), wspec],
        out_shape=[jax.ShapeDtypeStruct((3, t, bw), BF16), jax.ShapeDtypeStruct((3, bw), F32)],
        compiler_params=_cparams("parallel", "arbitrary"),
    )(cols3, cols3, cols3, cols3, cols3, cols3, dy, dy, conv_w)


def _ret_consts():
    log_gamma = jnp.log1p(-jnp.exp2(-5.0 - jnp.arange(H_RET, dtype=F32)))
    pos = jnp.arange(CHUNK, dtype=F32)
    d_intra = jnp.exp(log_gamma[:, None, None] * jnp.abs(pos[:, None] - pos[None, :]))
    q_decay = jnp.exp(log_gamma[:, None] * (pos + 1.0))
    k_decay = jnp.exp(log_gamma[:, None] * (CHUNK - 1.0 - pos))
    chunk_decay = jnp.exp(log_gamma * CHUNK)
    wide = (H_RET, CHUNK, DK_RET)
    return (d_intra, jnp.broadcast_to(q_decay[:, :, None], wide), jnp.broadcast_to(k_decay[:, :, None], wide),
            jnp.broadcast_to(chunk_decay[:, None, None], (H_RET, 1, DK_RET)))


def _rope_tables(t):
    inv_freq = np.float32(ROPE_BASE) ** (-np.linspace(0.0, 1.0, DK_RET // 2, dtype=np.float32))
    ang = np.arange(t, dtype=np.float32)[:, None] * inv_freq[None, :]
    cos, sin = np.cos(ang), np.sin(ang)
    return jnp.asarray(np.concatenate([cos, cos], axis=1)), jnp.asarray(np.concatenate([-sin, sin], axis=1))


def _rope(v, cc, ss):
    return v * cc + pltpu.roll(v, DK_RET // 2, axis=1) * ss


def _ret_in_specs(tb, blk):
    def col(s):
        return pl.BlockSpec((None, tb, H_RET * DK_RET), lambda i: (s, blk(i), 0))

    tab = pl.BlockSpec((tb, DK_RET), lambda i: (blk(i), 0))
    return ([col(3), col(4), col(5), col(6), tab, tab,
             pl.BlockSpec((H_RET, CHUNK, CHUNK), lambda i: (0, 0, 0)),
             pl.BlockSpec((H_RET, CHUNK, DK_RET), lambda i: (0, 0, 0)),
             pl.BlockSpec((H_RET, CHUNK, DK_RET), lambda i: (0, 0, 0)),
             pl.BlockSpec((H_RET, 1, DK_RET), lambda i: (0, 0, 0))])


def _ret_fwd(cols3, tables, consts, *, name):
    _, t, bw = cols3.shape
    tb = _tile(t, RET_TB)
    ncb = tb // CHUNK
    scale = DK_RET ** -0.5

    def body(q_ref, k_ref, v_ref, g_ref, cc_ref, ss_ref, di_ref, qd_ref, kd_ref, cd_ref, y_ref, st_ref, state):
        @pl.when(pl.program_id(0) == 0)
        def _():
            state[...] = jnp.zeros_like(state)

        heads = range(H_RET)
        units = [(c, h) for c in range(ncb) for h in heads]
        every = range(len(units))
        rows = [pl.ds(c * CHUNK, CHUNK) for c, _ in units]
        lanes = [pl.ds(h * DK_RET, DK_RET) for _, h in units]
        hd = [h for _, h in units]
        cc, ss = [cc_ref[rw, :] for rw in rows], [ss_ref[rw, :] for rw in rows]
        qs = [_rope(q_ref[rows[u], lanes[u]].astype(F32), cc[u], ss[u]) * scale for u in every]
        kr = [_rope(k_ref[rows[u], lanes[u]].astype(F32), cc[u], ss[u]) for u in every]
        vb = [v_ref[rows[u], lanes[u]] for u in every]
        gv = [g_ref[rows[u], lanes[u]].astype(F32) for u in every]
        innerb = [_bf(_dot_nt(_bf(qs[u]), _bf(kr[u])) * di_ref[hd[u]]) for u in every]
        update = [_dot_tn(_bf(kr[u] * kd_ref[hd[u]]), vb[u]) for u in every]
        carried = [state[h] for h in heads]
        s_in = []
        for u, (c, h) in enumerate(units):
            s_in.append(carried[h])
            st_ref[h, c] = carried[h]
            carried[h] = carried[h] * cd_ref[h] + update[u]
        for h in heads:
            state[h] = carried[h]
        o = [_dot(innerb[u], vb[u]) + _dot(_bf(qs[u] * qd_ref[hd[u]]), _bf(s_in[u])) for u in every]
        for u in every:
            on = o[u] * lax.rsqrt(jnp.mean(o[u] * o[u], axis=-1, keepdims=True) + EPS)
            y_ref[rows[u], lanes[u]] = _bf(gv[u] * _sigmoid(gv[u]) * on)

    return pl.pallas_call(
        body, name=name, grid=(t // tb,),
        in_specs=_ret_in_specs(tb, lambda i: i),
        out_specs=[pl.BlockSpec((tb, bw), lambda i: (i, 0)),
                   pl.BlockSpec((H_RET, ncb, DK_RET, DK_RET), lambda i: (0, i, 0, 0))],
        out_shape=[jax.ShapeDtypeStruct((t, bw), BF16),
                   jax.ShapeDtypeStruct((H_RET, t // CHUNK, DK_RET, DK_RET), F32)],
        scratch_shapes=[pltpu.VMEM((H_RET, DK_RET, DK_RET), F32)],
        compiler_params=_cparams("arbitrary"),
    )(cols3, cols3, cols3, cols3, tables[0], tables[1], *consts)


def _ret_bwd(cols3, tables, consts, states, dy, *, name, exchange=None):
    _, t, bw = cols3.shape
    tb = _tile(t, RET_TB)
    ncb = tb // CHUNK
    nb = t // tb
    scale = DK_RET ** -0.5

    def body(q_ref, k_ref, v_ref, g_ref, cc_ref, ss_ref, di_ref, qd_ref, kd_ref, cd_ref, st_ref, dy_ref,
             d4_ref, dstate):
        @pl.when(pl.program_id(0) == 0)
        def _():
            dstate[...] = jnp.zeros_like(dstate)

        heads = range(H_RET)
        lanes_of = [pl.ds(h * DK_RET, DK_RET) for h in heads]
        di, qd, kd = [di_ref[h] for h in heads], [qd_ref[h] for h in heads], [kd_ref[h] for h in heads]
        carried = [dstate[h] for h in heads]
        group = RET_GROUP if ncb % RET_GROUP == 0 else 1
        for c0 in reversed(range(0, ncb, group)):
            units = [(c, h) for c in reversed(range(c0, c0 + group)) for h in heads]
            rows = [pl.ds(c * CHUNK, CHUNK) for c, _ in units]
            cc, ss = [cc_ref[rw, :] for rw in rows], [ss_ref[rw, :] for rw in rows]
            every = range(len(units))
            hd = [h for _, h in units]
            vb = [v_ref[rows[u], lanes_of[hd[u]]] for u in every]
            gv = [g_ref[rows[u], lanes_of[hd[u]]].astype(F32) for u in every]
            dyv = [dy_ref[rows[u], lanes_of[hd[u]]].astype(F32) for u in every]
            s_in = [_bf(st_ref[h, c]) for c, h in units]
            qs = [_rope(q_ref[rows[u], lanes_of[hd[u]]].astype(F32), cc[u], ss[u]) * scale for u in every]
            kr = [_rope(k_ref[rows[u], lanes_of[hd[u]]].astype(F32), cc[u], ss[u]) for u in every]
            qsb, krb = [_bf(v) for v in qs], [_bf(v) for v in kr]
            qdb = [_bf(qs[u] * qd[hd[u]]) for u in every]
            kdb = [_bf(kr[u] * kd[hd[u]]) for u in every]
            innerb = [_bf(_dot_nt(qsb[u], krb[u]) * di[hd[u]]) for u in every]
            o = [_dot(innerb[u], vb[u]) + _dot(qdb[u], s_in[u]) for u in every]
            r = [lax.rsqrt(jnp.mean(v * v, axis=-1, keepdims=True) + EPS) for v in o]
            on = [o[u] * r[u] for u in every]
            sg = [_sigmoid(v) for v in gv]
            dgv = [_bf(dyv[u] * on[u] * (sg[u] * (1.0 + gv[u] * (1.0 - sg[u])))) for u in every]
            don = [dyv[u] * (gv[u] * sg[u]) for u in every]
            dob = [_bf(r[u] * (don[u] - on[u] * jnp.mean(don[u] * on[u], axis=-1, keepdims=True))) for u in every]
            dinner = [_bf(_dot_nt(dob[u], vb[u]) * di[hd[u]]) for u in every]
            dqs = [_dot(dinner[u], krb[u]) + _dot_nt(dob[u], s_in[u]) * qd[hd[u]] for u in every]
            dkr = [_dot_tn(dinner[u], qsb[u]) for u in every]
            dv = [_dot_tn(innerb[u], dob[u]) for u in every]
            dnew = [_dot_tn(qdb[u], dob[u]) for u in every]
            for u in every:
                h = hd[u]
                dstb = _bf(carried[h])
                dv[u] = dv[u] + _dot(kdb[u], dstb)
                dkr[u] = dkr[u] + _dot_nt(vb[u], dstb) * kd[h]
                carried[h] = carried[h] * cd_ref[h] + dnew[u]
            for u in every:
                ln = lanes_of[hd[u]]
                d4_ref[0, rows[u], ln] = _bf(_rope_bwd(dqs[u] * scale, cc[u], ss[u]))
                d4_ref[1, rows[u], ln] = _bf(_rope_bwd(dkr[u], cc[u], ss[u]))
                d4_ref[2, rows[u], ln] = _bf(dv[u])
                d4_ref[3, rows[u], ln] = dgv[u]
        for h in heads:
            dstate[h] = carried[h]

    rev = lambda i: nb - 1 - i
    outs, moved = _pcall(
        body, (cols3, cols3, cols3, cols3, tables[0], tables[1], *consts, states, dy), name=name, grid=(nb,),
        in_specs=_ret_in_specs(tb, rev)
        + [pl.BlockSpec((H_RET, ncb, DK_RET, DK_RET), lambda i: (0, rev(i), 0, 0)),
           pl.BlockSpec((None, tb, bw), lambda i: (1, rev(i), 0))],
        out_specs=[pl.BlockSpec((4, tb, bw), lambda i: (0, rev(i), 0))],
        out_shape=[jax.ShapeDtypeStruct((4, t, bw), BF16)],
        scratch_shapes=[pltpu.VMEM((H_RET, DK_RET, DK_RET), F32)],
        sem=("arbitrary",), exchange=exchange)
    return outs[0], moved


def _rope_bwd(dv, cc, ss):
    return dv * cc + pltpu.roll(dv * ss, DK_RET // 2, axis=1)


def _att_window(i):
    return pl.multiple_of(jnp.maximum(i - ATT_LOOKBACK // ATT_QB, 0) * ATT_QB, ATT_QB)


def _att_mask(v):
    qchunk = (v * ATT_QB + lax.broadcasted_iota(jnp.int32, (ATT_QB, ATT_WIN), 0)) // CHUNK
    kchunk = lax.broadcasted_iota(jnp.int32, (ATT_QB, ATT_WIN), 1) // CHUNK
    return (kchunk <= qchunk) & (kchunk >= qchunk - N_PREV_CHUNKS)


def _bias_spec(layer, nvar):
    return pl.BlockSpec((None, None, 2, ATT_QB, ATT_WIN), lambda hp, i: (layer, jnp.minimum(i, nvar), hp, 0, 0))


def _att_fwd(cols3, bias3, *, name, exchange=None):
    _, t, bw = cols3.shape
    assert t % ATT_QB == 0 and t >= ATT_WIN
    scale = DH_ATT ** -0.5
    nvar = ATT_LOOKBACK // ATT_QB

    def body(q_ref, k_ref, v_ref, b_ref, y_ref, lse_ref):
        i = pl.program_id(1)
        ws = _att_window(i)
        q = q_ref[...].astype(F32)
        kw = k_ref[pl.ds(ws, ATT_WIN), :]
        vw = v_ref[pl.ds(ws, ATT_WIN), :]
        lane_head = lax.broadcasted_iota(jnp.int32, (ATT_QB, 128), 1) // DH_ATT
        out = jnp.zeros((ATT_QB, 128), F32)
        lse = jnp.zeros((ATT_QB, 128), F32)
        for hh in range(2):
            mine = lane_head == hh
            s = _dot_nt(_bf(jnp.where(mine, q, 0.0)), kw) * scale + b_ref[hh]
            mx = jnp.max(s, axis=-1, keepdims=True)
            p = jnp.exp(s - mx)
            l = jnp.sum(p, axis=-1, keepdims=True)
            out = jnp.where(mine, _dot(_bf(p), vw) / l, out)
            lse = jnp.where(mine, mx + jnp.log(l), lse)
        y_ref[...] = _bf(out)
        lse_ref[...] = lse

    kv = lambda s: pl.BlockSpec((None, t, 128), lambda hp, i: (s, 0, hp))
    return _pcall(
        body, (cols3, cols3, cols3, bias3[0]), name=name, grid=(H_ATT // 2, t // ATT_QB),
        in_specs=[pl.BlockSpec((None, ATT_QB, 128), lambda hp, i: (7, i, hp)), kv(8), kv(9),
                  _bias_spec(bias3[1], nvar)],
        out_specs=[pl.BlockSpec((ATT_QB, 128), lambda hp, i: (i, hp)),
                   pl.BlockSpec((None, ATT_QB, 128), lambda hp, i: (hp, i, 0))],
        out_shape=[jax.ShapeDtypeStruct((t, bw), BF16), jax.ShapeDtypeStruct((H_ATT // 2, t, 128), F32)],
        scratch_shapes=[], sem=("parallel", "arbitrary"), exchange=exchange)


def _att_bwd(cols3, bias3, y, lse, dy, *, name, exchange=None):
    _, t, bw = cols3.shape
    nq = t // ATT_QB
    scale = DH_ATT ** -0.5
    nvar = ATT_LOOKBACK // ATT_QB

    def body(q_ref, k_ref, v_ref, b_ref, y_ref, lse_ref, dy_ref, d3_ref, db_ref, dk_acc, dv_acc):
        i = pl.program_id(1)

        @pl.when(i == 0)
        def _():
            dk_acc[...] = jnp.zeros_like(dk_acc)
            dv_acc[...] = jnp.zeros_like(dv_acc)

        ws = _att_window(i)
        q = q_ref[...].astype(F32)
        kw = k_ref[pl.ds(ws, ATT_WIN), :]
        vw = v_ref[pl.ds(ws, ATT_WIN), :]
        do = dy_ref[...].astype(F32)
        dof = do * y_ref[...].astype(F32)
        lsev = lse_ref[...]
        lane_head = lax.broadcasted_iota(jnp.int32, (ATT_QB, 128), 1) // DH_ATT
        dq = jnp.zeros((ATT_QB, 128), F32)
        dk = jnp.zeros((ATT_WIN, 128), F32)
        dv = jnp.zeros((ATT_WIN, 128), F32)
        first = i <= nvar
        for hh in range(2):
            mine = lane_head == hh
            qh = _bf(jnp.where(mine, q, 0.0))
            doh = _bf(jnp.where(mine, do, 0.0))
            s = _dot_nt(qh, kw) * scale + b_ref[hh]
            lse_h = jnp.max(jnp.where(mine, lsev, NEG_INF), axis=-1, keepdims=True)
            p = jnp.exp(s - lse_h)
            delta = jnp.sum(jnp.where(mine, dof, 0.0), axis=-1, keepdims=True)
            ds = p * (_dot_nt(doh, vw) - delta)

            @pl.when(first)
            def _():
                db_ref[hh] = ds

            @pl.when(jnp.logical_not(first))
            def _():
                db_ref[hh] += ds

            dsb = _bf(ds * scale)
            dq = jnp.where(mine, _dot(dsb, kw), dq)
            dk = dk + _dot_tn(dsb, qh)
            dv = dv + _dot_tn(_bf(p), doh)
        d3_ref[0, pl.ds(pl.multiple_of(i * ATT_QB, ATT_QB), ATT_QB), :] = _bf(dq)
        dk_acc[pl.ds(ws, ATT_WIN), :] += dk
        dv_acc[pl.ds(ws, ATT_WIN), :] += dv

        @pl.when(i == nq - 1)
        def _():
            d3_ref[1] = _bf(dk_acc[...])
            d3_ref[2] = _bf(dv_acc[...])

    kv = lambda s: pl.BlockSpec((None, t, 128), lambda hp, i: (s, 0, hp))
    qrow = pl.BlockSpec((ATT_QB, 128), lambda hp, i: (i, hp))
    btile = pl.BlockSpec((None, 2, ATT_QB, ATT_WIN), lambda hp, i: (jnp.minimum(i, nvar), hp, 0, 0))
    return _pcall(
        body, (cols3, cols3, cols3, bias3[0], y, lse, dy), name=name, grid=(H_ATT // 2, nq),
        in_specs=[pl.BlockSpec((None, ATT_QB, 128), lambda hp, i: (7, i, hp)), kv(8), kv(9),
                  _bias_spec(bias3[1], nvar), qrow,
                  pl.BlockSpec((None, ATT_QB, 128), lambda hp, i: (hp, i, 0)),
                  pl.BlockSpec((None, ATT_QB, 128), lambda hp, i: (2, i, hp))],
        out_specs=[pl.BlockSpec((3, t, 128), lambda hp, i: (0, 0, hp)), btile],
        out_shape=[jax.ShapeDtypeStruct((3, t, bw), BF16),
                   jax.ShapeDtypeStruct((nvar + 1, H_ATT, ATT_QB, ATT_WIN), F32)],
        scratch_shapes=[pltpu.VMEM((t, 128), F32), pltpu.VMEM((t, 128), F32)],
        sem=("parallel", "arbitrary"), exchange=exchange)


SKEW_W = ATT_WIN + ATT_QB
REL_PAD = 384


def _rel_onehot(v):
    r = lax.broadcasted_iota(jnp.int32, (REL_PAD, SKEW_W), 0)
    j = lax.broadcasted_iota(jnp.int32, (REL_PAD, SKEW_W), 1)
    dist = jnp.where(j < ATT_WIN, v * ATT_QB - j, v * ATT_QB + SKEW_W - j)
    col = jnp.clip(dist, -REL_CLIP, REL_CLIP) + REL_CLIP
    return _bf(jnp.where(col == r, 1.0, 0.0))


def _split3(v):
    hi = _bf(v)
    rest = v - hi.astype(F32)
    mid = _bf(rest)
    return hi, mid, _bf(rest - mid.astype(F32))


def _skew8(a, forward):
    row = lax.broadcasted_iota(jnp.int32, a.shape, 0)
    for b in range(3):
        shift = (1 << b) if forward else SKEW_W - (1 << b)
        a = jnp.where(((row >> b) & 1) == 1, pltpu.roll(a, shift, axis=1), a)
    return a


def _toeplitz_rows(ext_row):
    a = _skew8(jnp.broadcast_to(ext_row, (8, SKEW_W)), True)
    while a.shape[0] < ATT_QB:
        a = jnp.concatenate([a, pltpu.roll(a, a.shape[0], axis=1)], axis=0)
    return a


def _diagonal_sums(tile):
    a = tile
    while a.shape[0] > 8:
        half = a.shape[0] // 2
        a = a[:half] + pltpu.roll(a[half:], SKEW_W - half, axis=1)
    return jnp.sum(_skew8(a, False), axis=0, keepdims=True)


def _bias_tiles(rel_bias, *, name, exchange=None):
    depth = rel_bias.shape[0]
    nvar = ATT_LOOKBACK // ATT_QB + 1
    rel = jnp.pad(rel_bias, ((0, 0), (0, 0), (0, REL_PAD - N_REL)))

    def body(rel_ref, o_ref, ext_ref):
        v = pl.program_id(1)
        onehot = _rel_onehot(v)
        ext_ref[...] = sum(_dot(part, onehot) for part in _split3(rel_ref[...]))
        valid = _att_mask(v)
        for h in range(H_ATT):
            o_ref[h] = jnp.where(valid, _toeplitz_rows(ext_ref[h:h + 1, :])[:, :ATT_WIN], NEG_INF)

    outs, moved = _pcall(
        body, (rel,), name=name, grid=(depth, nvar),
        in_specs=[pl.BlockSpec((None, H_ATT, REL_PAD), lambda l, v: (l, 0, 0))],
        out_specs=[pl.BlockSpec((None, None, H_ATT, ATT_QB, ATT_WIN), lambda l, v: (l, v, 0, 0, 0))],
        out_shape=[jax.ShapeDtypeStruct((depth, nvar, H_ATT, ATT_QB, ATT_WIN), F32)],
        scratch_shapes=[pltpu.VMEM((H_ATT, SKEW_W), F32)], sem=("parallel", "parallel"), exchange=exchange)
    return outs[0], moved


def _rel_bias_grad(dbias3, *, name):
    nvar = dbias3.shape[0]

    def body(db_ref, o_ref, diag_ref):
        v = pl.program_id(0)
        for h in range(H_ATT):
            tile = jnp.concatenate([db_ref[h], jnp.zeros((ATT_QB, ATT_QB), F32)], axis=1)
            diag_ref[h:h + 1, :] = _diagonal_sums(tile)
        onehot = _rel_onehot(v)
        part = sum(_dot_nt(p, onehot) for p in _split3(diag_ref[...]))

        @pl.when(v == 0)
        def _():
            o_ref[...] = part

        @pl.when(v > 0)
        def _():
            o_ref[...] += part

    out = pl.pallas_call(
        body, name=name, grid=(nvar,),
        in_specs=[pl.BlockSpec((None, H_ATT, ATT_QB, ATT_WIN), lambda v: (v, 0, 0, 0))],
        out_specs=pl.BlockSpec((H_ATT, REL_PAD), lambda v: (0, 0)),
        out_shape=jax.ShapeDtypeStruct((H_ATT, REL_PAD), F32),
        scratch_shapes=[pltpu.VMEM((H_ATT, SKEW_W), F32)], compiler_params=_cparams("arbitrary"),
    )(dbias3)
    return out[:, :N_REL]


def _merge_fwd(h, ys, wmg, wb_t, *, name, tm=512):
    t, d = h.shape
    bw = ys[0].shape[1]
    tm = _tile(t, tm)

    def body(h_ref, y0_ref, y1_ref, y2_ref, wg_ref, wb_ref, m_ref, s_ref, p_ref):
        hv = h_ref[...]
        total = jnp.zeros((tm, d), F32)
        for b, y_ref in enumerate((y0_ref, y1_ref, y2_ref)):
            s = _sigmoid(_dot(hv, wg_ref[b]))
            p = _dot_nt(y_ref[...], wb_ref[b])
            s_ref[b] = _bf(s)
            p_ref[b] = _bf(p)
            total = total + s * p
        m_ref[...] = _bf(total)

    row = pl.BlockSpec((tm, d), lambda i: (i, 0))
    yrow = pl.BlockSpec((tm, bw), lambda i: (i, 0))
    three = pl.BlockSpec((3, tm, d), lambda i: (0, i, 0))
    return pl.pallas_call(
        body, name=name, grid=(t // tm,),
        in_specs=[row, yrow, yrow, yrow, pl.BlockSpec((3, d, d), lambda i: (0, 0, 0)),
                  pl.BlockSpec((3, d, bw), lambda i: (0, 0, 0))],
        out_specs=[row, three, three],
        out_shape=[jax.ShapeDtypeStruct((t, d), BF16), jax.ShapeDtypeStruct((3, t, d), BF16),
                   jax.ShapeDtypeStruct((3, t, d), BF16)],
        compiler_params=_cparams("parallel"),
    )(h, *ys, wmg, wb_t)


def _merge_dwb(dp3, ys, *, name, tm=1024, tk=TOKEN_TK):
    nb, t, m = dp3.shape
    n = ys[0].shape[1]
    tm, tk = _tile(m, tm), _tile(t, tk)
    nk = t // tk

    def body(a_ref, *refs):
        o_ref, acc_ref = refs[nb], refs[nb + 1]
        k = pl.program_id(2)

        def finish(total):
            o_ref[...] = total.astype(o_ref.dtype)

        for b in range(nb):
            @pl.when(pl.program_id(0) == b)
            def _(y_ref=refs[b]):
                _accumulate(acc_ref, _dot_tn(a_ref[...], y_ref[...]), k, nk, finish)

    def y_spec(b):
        return pl.BlockSpec((tk, n), lambda s, i, k: (jnp.where(s == b, k, jnp.where(s < b, 0, nk - 1)), 0))

    return pl.pallas_call(
        body, name=name, grid=(nb, m // tm, nk),
        in_specs=[pl.BlockSpec((None, tk, tm), lambda s, i, k: (s, k, i))] + [y_spec(b) for b in range(nb)],
        out_specs=pl.BlockSpec((None, tm, n), lambda s, i, k: (s, i, 0)),
        out_shape=jax.ShapeDtypeStruct((nb, m, n), PARTIAL_DTYPE), scratch_shapes=[pltpu.VMEM((tm, n), F32)],
        compiler_params=_cparams("arbitrary", "arbitrary", "arbitrary"),
    )(dp3, *ys)


def _merge_bwd(dm, s3, p3, wmg, wb_t, *, name, tm=512):
    _, t, d = s3.shape
    bw = wb_t.shape[2]
    tm = _tile(t, tm)

    def body(dm_ref, s_ref, p_ref, wg_ref, wb_ref, dgp_ref, dp_ref, dy_ref, dh_ref):
        dmv = dm_ref[...].astype(F32)
        dh = jnp.zeros((tm, d), F32)
        for b in range(3):
            s = s_ref[b].astype(F32)
            dgp = _bf(dmv * p_ref[b].astype(F32) * s * (1.0 - s))
            dp = _bf(dmv * s)
            dgp_ref[b] = dgp
            dp_ref[b] = dp
            dy_ref[b] = _bf(_dot(dp, wb_ref[b]))
            dh = dh + _dot_nt(dgp, wg_ref[b])
        dh_ref[...] = dh

    row = pl.BlockSpec((tm, d), lambda i: (i, 0))
    three = pl.BlockSpec((3, tm, d), lambda i: (0, i, 0))
    return pl.pallas_call(
        body, name=name, grid=(t // tm,),
        in_specs=[row, three, three, pl.BlockSpec((3, d, d), lambda i: (0, 0, 0)),
                  pl.BlockSpec((3, d, bw), lambda i: (0, 0, 0))],
        out_specs=[three, three, pl.BlockSpec((3, tm, bw), lambda i: (0, i, 0)), row],
        out_shape=[jax.ShapeDtypeStruct((3, t, d), BF16), jax.ShapeDtypeStruct((3, t, d), BF16),
                   jax.ShapeDtypeStruct((3, t, bw), BF16), jax.ShapeDtypeStruct((t, d), F32)],
        compiler_params=_cparams("parallel"),
    )(dm, s3, p3, wmg, wb_t)


def _carried(hooks, slot, state=None):
    if not hooks or slot not in hooks:
        return None, lambda buf: None
    make, done = hooks[slot]
    return make(state), done


def _layer_fwd(x, p, aux, tag, hooks=None):
    ex, done = _carried(hooks, "ffn1")
    (x1, h1, g1, u1, a1), buf = _ffn_fwd(x, p["n1"], p["wg1"], p["wu1"], p["wd1"], name=f"ffn1_fwd_{tag}", exchange=ex)
    done(buf)
    h2 = _rmsnorm_fwd(x1, p["nmix"], name=f"mixnorm_fwd_{tag}")
    ex, done = _carried(hooks, "inproj")
    if ex is None:
        cols3 = _mm_to_slices(h2, p["win"], name=f"inproj_fwd_{tag}")
    else:
        cols3, buf = _mm_to_slices(h2, p["win"], name=f"inproj_fwd_{tag}", exchange=ex)
        done(buf)
    bias3 = p["bias3"]
    y_conv = _conv_fwd(cols3, p["conv_w"], name=f"conv_fwd_{tag}")
    y_ret, states = _ret_fwd(cols3, aux["rope"], aux["ret"], name=f"ret_fwd_{tag}")
    ex, done = _carried(hooks, "att")
    (y_att, lse), buf = _att_fwd(cols3, bias3, name=f"att_fwd_{tag}", exchange=ex)
    done(buf)
    merged, s3, p3 = _merge_fwd(h2, (y_conv, y_ret, y_att), p["wmg"], p["wb"], name=f"merge_fwd_{tag}")
    x2 = _mm_rows(merged, p["wout"], dims=NN, out_dtype=F32, res=x1, name=f"outproj_fwd_{tag}")
    ex, done = _carried(hooks, "ffn2")
    (x3, h3, g2, u2, a2), buf = _ffn_fwd(x2, p["n2"], p["wg2"], p["wu2"], p["wd2"], name=f"ffn2_fwd_{tag}", exchange=ex)
    done(buf)
    saved = dict(x0=x, h1=h1, g1=g1, u1=u1, a1=a1, x1=x1, h2=h2, cols3=cols3, bias3=bias3, y_conv=y_conv,
                 y_ret=y_ret, states=states, y_att=y_att, lse=lse, merged=merged, s3=s3, p3=p3, x2=x2, h3=h3,
                 g2=g2, u2=u2, a2=a2)
    return x3, saved


def _ffn_grads(dxo, x, h, g, u, a, p, which, tag, grads, all_grads, hooks):
    n = which
    ex, done = _carried(hooks, f"ffn{n}_bwd", all_grads)
    (dx, dyb, dg, du, grads["n" + n]), buf = _ffn_bwd(dxo, x, p["n" + n], g, u, p["wg" + n], p["wu" + n], p["wd" + n],
                                                     name=f"ffn{n}_{tag}_bwd", exchange=ex)
    done(buf)
    grads["wd" + n] = _mm_tn(a, dyb, name=f"ffn{n}_{tag}_dwd")
    for key, lhs in (("wg" + n, dg), ("wu" + n, du)):
        ex, done = _carried(hooks, f"ffn{n}_d{key[:2]}", all_grads)
        if ex is None:
            grads[key] = _mm_tn(lhs, h, name=f"ffn{n}_{tag}_d{key[:2]}")
        else:
            grads[key], buf = _mm_tn(lhs, h, name=f"ffn{n}_{tag}_d{key[:2]}", exchange=ex)
            done(buf)
    return dx


def _layer_bwd(dx3, p, s, aux, tag, grads, all_grads, hooks=None):
    dx2 = _ffn_grads(dx3, s["x2"], s["h3"], s["g2"], s["u2"], s["a2"], p, "2", tag, grads, all_grads, hooks)

    dm = _mm_rows(dx2, p["wout"], dims=NT, out_dtype=BF16, name=f"outproj_dm_{tag}")
    grads["wout"] = _mm_tn(s["merged"], dx2, name=f"outproj_dw_{tag}")
    dgp3, dp3, dy3, dh2 = _merge_bwd(dm, s["s3"], s["p3"], p["wmg"], p["wb"], name=f"merge_bwd_{tag}")
    grads["wmg"] = _mm_tn_batch(s["h2"], dgp3, name=f"merge_dwg_{tag}")
    grads["wb"] = _merge_dwb(dp3, (s["y_conv"], s["y_ret"], s["y_att"]), name=f"merge_dwb_{tag}")

    dconv3, grads["conv_w"] = _conv_bwd(s["cols3"], p["conv_w"], dy3, name=f"conv_bwd_{tag}")
    ex, done = _carried(hooks, "ret_bwd", all_grads)
    dret4, buf = _ret_bwd(s["cols3"], aux["rope"], aux["ret"], s["states"], dy3, name=f"ret_bwd_{tag}", exchange=ex)
    done(buf)
    ex, done = _carried(hooks, "att_bwd", all_grads)
    (datt3, dbias3), buf = _att_bwd(s["cols3"], s["bias3"], s["y_att"], s["lse"], dy3, name=f"att_bwd_{tag}",
                                    exchange=ex)
    done(buf)
    grads["rel_bias"] = _rel_bias_grad(dbias3, name=f"bias_grad_{tag}")

    dh2 = _inproj_dh((dconv3, dret4, datt3), p["win"], dh2, name=f"inproj_dh_{tag}")
    grads["win"] = jnp.concatenate([
        _mm_tn_slices(dconv3, s["h2"], name=f"inproj_dw_conv_{tag}"),
        _mm_tn_slices(dret4, s["h2"], name=f"inproj_dw_ret_{tag}"),
        _mm_tn_slices(datt3, s["h2"], name=f"inproj_dw_att_{tag}")], axis=0)
    dx1, grads["nmix"] = _rmsnorm_bwd(dh2, s["x1"], p["nmix"], dx2, name=f"mixnorm_bwd_{tag}")

    return _ffn_grads(dx1, s["x0"], s["h1"], s["g1"], s["u1"], s["a1"], p, "1", tag, grads, all_grads, hooks)


def _device_step(x, target, layers, final_norm, fwd_hooks=None, bwd_hooks=None):
    t = x.shape[0]
    depth = len(layers)
    aux = dict(rope=_rope_tables(t), ret=_ret_consts())
    ex, done = _carried(fwd_hooks[0] if fwd_hooks else None, "start")
    bias_all, buf = _bias_tiles(jnp.stack([p["rel_bias"] for p in layers]), name="bias_tiles", exchange=ex)
    done(buf)
    saved = []
    for l, p in enumerate(layers):
        p["bias3"] = (bias_all, l)
        x, s = _layer_fwd(x, p, aux, f"l{l}", fwd_hooks[l] if fwd_hooks else None)
        saved.append(s)
    dx, dfinal, loss_row = _loss_fwd_bwd(x, final_norm, target, name="loss_fwd_bwd")
    grads = [dict() for _ in range(depth)]
    for l in reversed(range(depth)):
        dx = _layer_bwd(dx, layers[l], saved[l], aux, f"l{l}", grads[l], grads, bwd_hooks[l] if bwd_hooks else None)
    return loss_row, dx, grads, dfinal


def _sum_slots(bufs, *, name, tr=512):
    n, r, cdim = bufs[0].shape
    depth = len(bufs)
    tr = _tile(r, tr)
    nt = r // tr

    def body(*refs):
        o_ref = refs[depth]
        for k in range(depth):
            @pl.when(pl.program_id(0) == k)
            def _(p_ref=refs[k]):
                acc = p_ref[0].astype(F32)
                for s in range(1, n):
                    acc = acc + p_ref[s].astype(F32)
                o_ref[...] = acc

    def spec(k):
        return pl.BlockSpec((n, tr, cdim), lambda l, i: (0, jnp.where(l == k, i, jnp.where(l < k, 0, nt - 1)), 0))

    return pl.pallas_call(
        body, name=name, grid=(depth, nt), in_specs=[spec(k) for k in range(depth)],
        out_specs=pl.BlockSpec((None, tr, cdim), lambda l, i: (l, i, 0)),
        out_shape=jax.ShapeDtypeStruct((depth, r, cdim), F32), compiler_params=_cparams("arbitrary", "arbitrary"),
    )(*bufs)


def _all_reduce_small(v, *, name):
    r = v.shape[0]

    def body(x_ref, o_ref, slots, send_sems, recv_sems):
        x, y, c, me = _my_position()
        slots[me] = x_ref[...]
        sends = []
        for k in range(1, N_DEV):
            peer, _ = _peer(x, y, c, k)
            cp = pltpu.make_async_remote_copy(src_ref=x_ref, dst_ref=slots.at[me], send_sem=send_sems.at[k - 1],
                                              recv_sem=recv_sems.at[k - 1], device_id=peer, device_id_type=MESH)
            cp.start()
            sends.append(cp)
        for k in range(1, N_DEV):
            peer, peer_id = _peer(x, y, c, k)
            pltpu.make_async_remote_copy(src_ref=x_ref, dst_ref=slots.at[peer_id], send_sem=send_sems.at[k - 1],
                                         recv_sem=recv_sems.at[k - 1], device_id=peer, device_id_type=MESH).wait_recv()
        for cp in sends:
            cp.wait_send()
        acc = slots[0]
        for s in range(1, N_DEV):
            acc = acc + slots[s]
        o_ref[...] = acc

    return pl.pallas_call(
        body, name=name, in_specs=[pl.BlockSpec(memory_space=pltpu.VMEM)],
        out_specs=pl.BlockSpec(memory_space=pltpu.VMEM), out_shape=jax.ShapeDtypeStruct((r, 128), F32),
        scratch_shapes=[pltpu.VMEM((N_DEV, r, 128), F32), pltpu.SemaphoreType.DMA((N_DEV - 1,)),
                        pltpu.SemaphoreType.DMA((N_DEV - 1,))],
    )(v)


def _adam_update(wv, gv, mv, vv, d_ref, mo_ref, vo_ref):
    mn = ADAM_B1 * mv + (1.0 - ADAM_B1) * gv
    vn = ADAM_B2 * vv + (1.0 - ADAM_B2) * (gv * gv)
    m_hat = mn / (1.0 - ADAM_B1 ** ADAM_STEP)
    v_hat = vn / (1.0 - ADAM_B2 ** ADAM_STEP)
    d_ref[...] = -ADAM_LR * (m_hat / (jnp.sqrt(v_hat) + ADAM_EPS) + ADAM_WD * wv)
    mo_ref[...] = mn
    vo_ref[...] = vn


def _adamw_of_partials(w, bufs, m, v, *, name, tr=512):
    depth, r, cdim = w.shape
    n = bufs[0].shape[0]
    tr = _tile(r, tr)
    nt = r // tr

    def body(w_ref, m_ref, v_ref, *refs):
        d_ref, mo_ref, vo_ref, g_ref = refs[depth:]
        for k in range(depth):
            @pl.when(pl.program_id(0) == k)
            def _(p_ref=refs[k]):
                gv = p_ref[0].astype(F32)
                for s in range(1, n):
                    gv = gv + p_ref[s].astype(F32)
                g_ref[...] = gv
                _adam_update(w_ref[...], gv, m_ref[...], v_ref[...], d_ref, mo_ref, vo_ref)

    def part_spec(k):
        return pl.BlockSpec((n, tr, cdim), lambda l, i: (0, jnp.where(l == k, i, jnp.where(l < k, 0, nt - 1)), 0))

    spec = pl.BlockSpec((None, tr, cdim), lambda l, i: (l, i, 0))
    return pl.pallas_call(
        body, name=name, grid=(depth, nt), in_specs=[spec] * 3 + [part_spec(k) for k in range(depth)],
        out_specs=[spec] * 4, out_shape=[jax.ShapeDtypeStruct((depth, r, cdim), F32)] * 4,
        compiler_params=_cparams("arbitrary", "arbitrary"),
    )(w, m, v, *bufs)


def _adamw(w, g, m, v, *, name, tr=256):
    shape = w.shape
    cdim = shape[-1]
    w2, g2, m2, v2 = (a.reshape(-1, cdim) for a in (w, g, m, v))
    r = w2.shape[0]
    tr = _tile(r, tr) if r % 8 == 0 else r

    def body(w_ref, g_ref, m_ref, v_ref, d_ref, mo_ref, vo_ref):
        _adam_update(w_ref[...], g_ref[...], m_ref[...], v_ref[...], d_ref, mo_ref, vo_ref)

    spec = pl.BlockSpec((tr, cdim), lambda i: (i, 0))
    outs = pl.pallas_call(
        body, name=name, grid=(r // tr,), in_specs=[spec] * 4, out_specs=[spec] * 3,
        out_shape=[jax.ShapeDtypeStruct((r, cdim), F32)] * 3, compiler_params=_cparams("parallel"),
    )(w2, g2, m2, v2)
    return tuple(o.reshape(shape) for o in outs)


GROUPS = {"f1": ("wg1", "wu1", "wd1"), "in": ("win",), "mg": ("wb", "wmg", "wout"), "f2": ("wg2", "wu2", "wd2")}
BIG = tuple(nm for members in GROUPS.values() for nm in members)
ROWS_DOMAIN = ("wg1", "wu1", "wd1", "wmg", "wout", "wg2", "wu2", "wd2")


def _to_rows(name, w, d):
    depth = w.shape[0]
    if name in ("wg1", "wu1", "wg2", "wu2", "win"):
        return w.transpose(0, 2, 1)
    if name == "wb":
        return w.transpose(0, 1, 3, 2).reshape(depth, -1, d)
    if name == "wmg":
        return w.reshape(depth, -1, d)
    return w


def _from_rows(name, rows, d):
    depth = rows.shape[0]
    if name in ("wg1", "wu1", "wg2", "wu2", "win"):
        return rows.transpose(0, 2, 1)
    if name == "wb":
        return rows.reshape(depth, 3, -1, BRANCH_W).transpose(0, 1, 3, 2)
    if name == "wmg":
        return rows.reshape(depth, 3, -1, d)
    return rows


def _full_from_gathered(name, g, d):
    if name == "wb":
        return g.reshape(N_DEV, 3, -1, BRANCH_W).transpose(1, 0, 2, 3).reshape(3, d, BRANCH_W)
    if name == "wmg":
        return g.reshape(N_DEV, 3, -1, d).transpose(1, 0, 2, 3).reshape(3, d, d)
    return g.reshape(-1, d)


def _gathered_from_full(name, full, d):
    if name == "wb":
        return full.reshape(3, N_DEV, -1, BRANCH_W).transpose(1, 0, 2, 3).reshape(N_DEV, -1, d)
    if name == "wmg":
        return full.reshape(3, N_DEV, -1, d).transpose(1, 0, 2, 3).reshape(N_DEV, -1, d)
    return full.reshape(N_DEV, -1, d)


def kernel(x, ffn1_norm, ffn1_w_gate, ffn1_w_up, ffn1_w_down, mix_norm, w_in, conv_w, rel_bias, w_branch, w_merge_gate, w_out, ffn2_norm, ffn2_w_gate, ffn2_w_up, ffn2_w_down, final_norm, loss_target, m_ffn1_norm, m_ffn1_w_gate, m_ffn1_w_up, m_ffn1_w_down, m_mix_norm, m_w_in, m_conv_w, m_rel_bias, m_w_branch, m_w_merge_gate, m_w_out, m_ffn2_norm, m_ffn2_w_gate, m_ffn2_w_up, m_ffn2_w_down, m_final_norm, v_ffn1_norm, v_ffn1_w_gate, v_ffn1_w_up, v_ffn1_w_down, v_mix_norm, v_w_in, v_conv_w, v_rel_bias, v_w_branch, v_w_merge_gate, v_w_out, v_ffn2_norm, v_ffn2_w_gate, v_ffn2_w_up, v_ffn2_w_down, v_final_norm):
    names = ["ffn1_norm", "ffn1_w_gate", "ffn1_w_up", "ffn1_w_down", "mix_norm", "w_in", "conv_w", "rel_bias",
             "w_branch", "w_merge_gate", "w_out", "ffn2_norm", "ffn2_w_gate", "ffn2_w_up", "ffn2_w_down", "final_norm"]
    weights = dict(zip(names, (ffn1_norm, ffn1_w_gate, ffn1_w_up, ffn1_w_down, mix_norm, w_in, conv_w, rel_bias,
                               w_branch, w_merge_gate, w_out, ffn2_norm, ffn2_w_gate, ffn2_w_up, ffn2_w_down,
                               final_norm)))
    m_in = dict(zip(names, (m_ffn1_norm, m_ffn1_w_gate, m_ffn1_w_up, m_ffn1_w_down, m_mix_norm, m_w_in, m_conv_w,
                            m_rel_bias, m_w_branch, m_w_merge_gate, m_w_out, m_ffn2_norm, m_ffn2_w_gate,
                            m_ffn2_w_up, m_ffn2_w_down, m_final_norm)))
    v_in = dict(zip(names, (v_ffn1_norm, v_ffn1_w_gate, v_ffn1_w_up, v_ffn1_w_down, v_mix_norm, v_w_in, v_conv_w,
                            v_rel_bias, v_w_branch, v_w_merge_gate, v_w_out, v_ffn2_norm, v_ffn2_w_gate,
                            v_ffn2_w_up, v_ffn2_w_down, v_final_norm)))
    big_of = dict(wg1="ffn1_w_gate", wu1="ffn1_w_up", wd1="ffn1_w_down", win="w_in", wb="w_branch",
                  wmg="w_merge_gate", wout="w_out", wg2="ffn2_w_gate", wu2="ffn2_w_up", wd2="ffn2_w_down")
    depth = ffn1_norm.shape[0]
    d = x.shape[-1]
    xs = x.reshape(-1, d)
    target = loss_target.reshape(-1, d)
    _, _, _, me = _my_position()

    shard_rows = {nm: _to_rows(nm, weights[big_of[nm]], d).astype(BF16) for nm in BIG}

    conv_cols = conv_w.shape[-1]
    conv_full, = _exchange("gather", [conv_w.reshape(depth * 3, conv_cols)], name="gather_conv_w")
    conv_full = conv_full.reshape(N_DEV, depth, 3, conv_cols).transpose(1, 2, 0, 3).reshape(depth, 3, -1)
    layers = [dict(n1=ffn1_norm[l][None], nmix=mix_norm[l][None], n2=ffn2_norm[l][None], conv_w=conv_full[l],
                   rel_bias=rel_bias[l]) for l in range(depth)]

    def gather_hook(l, grp, kind="gather"):
        members = GROUPS[grp]

        def done(bufs):
            layers[l].update({nm: _full_from_gathered(nm, buf, d) for nm, buf in zip(members, bufs)})

        return (lambda _: (kind, [shard_rows[nm] for nm in members], [l] * len(members))), done

    fwd_hooks = []
    for l in range(depth):
        hooks = {"ffn1": gather_hook(l, "in"), "inproj": gather_hook(l, "mg"), "att": gather_hook(l, "f2")}
        if l == 0:
            hooks["start"] = gather_hook(0, "f1", "gather2")
        if l + 1 < depth:
            hooks["ffn2"] = gather_hook(l + 1, "f1")
        fwd_hooks.append(hooks)

    received = {}

    def scatter_hook(l, members):
        def make(all_grads):
            return "scatter", [_gathered_from_full(nm, all_grads[l][nm], d) for nm in members], [None] * len(members)

        return make, (lambda bufs: received.update({(l, nm): buf for nm, buf in zip(members, bufs)}))

    bwd_hooks = []
    for l in range(depth):
        hooks = {"att_bwd": scatter_hook(l, GROUPS["f2"]), "ret_bwd": scatter_hook(l, GROUPS["mg"]),
                 "ffn1_bwd": scatter_hook(l, GROUPS["in"])}
        if l + 1 < depth:
            hooks["ffn2_bwd"] = scatter_hook(l + 1, GROUPS["f1"])
        if l == 0:
            hooks["ffn1_dwg"] = scatter_hook(0, ("wd1",))
            hooks["ffn1_dwu"] = scatter_hook(0, ("wg1",))
        bwd_hooks.append(hooks)

    loss_row, dx, grads, dfinal = _device_step(xs, target, layers, final_norm[None], fwd_hooks, bwd_hooks)
    received[(0, "wu1")], = _exchange("scatter", [_gathered_from_full("wu1", grads[0]["wu1"], d)], name="scatter_l0_wu1")

    partials = {nm: [received[(l, nm)] for l in range(depth)] for nm in BIG}
    grad_w = {big_of[nm]: _from_rows(nm, _sum_slots(partials[nm], name=f"sum_grads_{nm}"), d)
              for nm in BIG if nm not in ROWS_DOMAIN}

    small = {"ffn1_norm": jnp.stack([grads[l]["n1"][0] for l in range(depth)]),
             "mix_norm": jnp.stack([grads[l]["nmix"][0] for l in range(depth)]),
             "ffn2_norm": jnp.stack([grads[l]["n2"][0] for l in range(depth)]),
             "final_norm": dfinal[0],
             "rel_bias": jnp.stack([grads[l]["rel_bias"] for l in range(depth)]),
             "conv_w": jnp.stack([grads[l]["conv_w"] for l in range(depth)]),
             "loss": loss_row[0, :1]}
    order = list(small)
    flat = jnp.concatenate([small[k].reshape(-1) for k in order])
    pad = (-flat.shape[0]) % 1024
    summed = _all_reduce_small(jnp.pad(flat, (0, pad)).reshape(-1, 128), name="reduce_small").reshape(-1)
    pos = 0
    for k in order:
        n = small[k].size
        small[k] = summed[pos:pos + n].reshape(small[k].shape)
        pos += n
    small["conv_w"] = lax.dynamic_slice_in_dim(small["conv_w"], me * conv_cols, conv_cols, axis=2)

    grad_w.update({k: small[k] for k in order if k != "loss"})
    delta, new_m, new_v = {}, {}, {}
    short = {v: k for k, v in big_of.items()}
    for nm in names:
        key = short.get(nm)
        if key in ROWS_DOMAIN:
            outs = _adamw_of_partials(_to_rows(key, weights[nm], d), partials[key], _to_rows(key, m_in[nm], d),
                                      _to_rows(key, v_in[nm], d), name=f"adamw_{nm}")
            delta[nm], new_m[nm], new_v[nm], grad_w[nm] = (_from_rows(key, o, d) for o in outs)
        else:
            delta[nm], new_m[nm], new_v[nm] = _adamw(weights[nm], grad_w[nm], m_in[nm], v_in[nm], name=f"adamw_{nm}")
    return (small["loss"].reshape(()), dx.reshape(x.shape), *[grad_w[n] for n in names], *[delta[n] for n in names],
            *[new_m[n] for n in names], *[new_v[n] for n in names])
```

```python
import functools
import math

import jax
import jax.numpy as jnp
import numpy as np
from jax import lax
from jax.experimental import pallas as pl
from jax.experimental.pallas import tpu as pltpu

F32 = jnp.float32
BF16 = jnp.bfloat16

N_DEV = 8
EPS = 1e-6
CHUNK = 64
BRANCH_W = 512
N_SLICES = 10
H_RET = 4
DK_RET = 128
H_ATT = 8
DH_ATT = 64
N_PREV_CHUNKS = 8
REL_CLIP = 128
N_REL = 2 * REL_CLIP + 1
NEG_INF = -1e30
ROPE_BASE = 10000.0
ATT_QB = 256
ATT_LOOKBACK = N_PREV_CHUNKS * CHUNK
ATT_WIN = ATT_LOOKBACK + ATT_QB
RET_TB = 256
RET_GROUP = 4
TOKEN_TK = 2048
PARTIAL_DTYPE = jnp.bfloat16
CONV_HALO = 16

ADAM_LR = 0.001
ADAM_B1 = 0.9
ADAM_B2 = 0.999
ADAM_EPS = 1e-08
ADAM_WD = 0.01
ADAM_STEP = 10

VMEM_LIMIT_BYTES = 56 * 1024 * 1024
MESH = pl.DeviceIdType.MESH


def _cparams(*sem):
    return pltpu.CompilerParams(dimension_semantics=sem, vmem_limit_bytes=VMEM_LIMIT_BYTES)


def _dot(a, b):
    return lax.dot_general(a, b, (((1,), (0,)), ((), ())), preferred_element_type=F32)


def _dot_nt(a, b):
    return lax.dot_general(a, b, (((1,), (1,)), ((), ())), preferred_element_type=F32)


def _dot_tn(a, b):
    return lax.dot_general(a, b, (((0,), (0,)), ((), ())), preferred_element_type=F32)


def _bf(v):
    return v.astype(BF16)


def _sigmoid(v):
    return 1.0 / (1.0 + jnp.exp(-v))


def _tile(n, want):
    if n <= want:
        return n
    for t in range(want - want % 128, 0, -128):
        if n % t == 0:
            return t
    t = want
    while n % t:
        t //= 2
    return t


def _my_position():
    x, y, c = lax.axis_index("x"), lax.axis_index("y"), lax.axis_index("c")
    return x, y, c, 4 * x + 2 * y + c


def _peer(x, y, c, k):
    px = 1 - x if k & 4 else x
    py = 1 - y if k & 2 else y
    pc = 1 - c if k & 1 else c
    return (px, py, pc), 4 * px + 2 * py + pc


DIRECT2 = (1, 2, 4, 6)
PASSED_ON = (2, 4, 6)
N_PUSHES = N_DEV - 1 + len(PASSED_ON)


def _exchange_sems(n):
    return [pltpu.SemaphoreType.DMA((n, N_PUSHES)), pltpu.SemaphoreType.DMA((n, N_PUSHES)),
            pltpu.SemaphoreType.DMA((n,))]


def _exchange_src(kind, src_ref, layer, block):
    if kind in ("gather", "gather2"):
        return src_ref if layer is None else src_ref.at[layer]
    return src_ref.at[block]


def _passed_on(dst_ref, sems, a, j, pos, incoming):
    x, y, c, _ = pos
    sibling, _ = _peer(x, y, c, 1)
    _, block = _peer(x, y, c, PASSED_ON[j] | 1 if incoming else PASSED_ON[j])
    return pltpu.make_async_remote_copy(src_ref=dst_ref.at[block], dst_ref=dst_ref.at[block],
                                        send_sem=sems[0].at[a, N_DEV - 1 + j], recv_sem=sems[1].at[a, N_DEV - 1 + j],
                                        device_id=sibling, device_id_type=MESH)


def _exchange_copy(kind, src_ref, layer, dst_ref, sems, a, k, pos, incoming):
    x, y, c, me = pos
    peer, peer_id = _peer(x, y, c, k)
    return pltpu.make_async_remote_copy(src_ref=_exchange_src(kind, src_ref, layer, me if incoming else peer_id),
                                        dst_ref=dst_ref.at[peer_id if incoming else me],
                                        send_sem=sems[0].at[a, k - 1], recv_sem=sems[1].at[a, k - 1],
                                        device_id=peer, device_id_type=MESH)


def _exchange_local(kind, src_ref, layer, dst_ref, sems, a, me):
    return pltpu.make_async_copy(_exchange_src(kind, src_ref, layer, me), dst_ref.at[me], sems[2].at[a])


def _exchange_start(kind, layers, src_refs, dst_refs, sems):
    pos = _my_position()
    direct = DIRECT2 if kind == "gather2" else range(1, N_DEV)
    for a, (src_ref, layer, dst_ref) in enumerate(zip(src_refs, layers, dst_refs)):
        _exchange_local(kind, src_ref, layer, dst_ref, sems, a, pos[3]).start()
        for k in direct:
            _exchange_copy(kind, src_ref, layer, dst_ref, sems, a, k, pos, False).start()


def _exchange_wait(kind, layers, src_refs, dst_refs, sems):
    pos = _my_position()
    for a, (src_ref, layer, dst_ref) in enumerate(zip(src_refs, layers, dst_refs)):
        if kind == "gather2":
            for j, k in enumerate(PASSED_ON):
                _exchange_copy(kind, src_ref, layer, dst_ref, sems, a, k, pos, True).wait_recv()
                _passed_on(dst_ref, sems, a, j, pos, False).start()
            _exchange_copy(kind, src_ref, layer, dst_ref, sems, a, 1, pos, True).wait_recv()
            for j in range(len(PASSED_ON)):
                _passed_on(dst_ref, sems, a, j, pos, True).wait_recv()
            for k in DIRECT2:
                _exchange_copy(kind, src_ref, layer, dst_ref, sems, a, k, pos, False).wait_send()
            for j in range(len(PASSED_ON)):
                _passed_on(dst_ref, sems, a, j, pos, False).wait_send()
        else:
            for k in range(1, N_DEV):
                _exchange_copy(kind, src_ref, layer, dst_ref, sems, a, k, pos, True).wait_recv()
            for k in range(1, N_DEV):
                _exchange_copy(kind, src_ref, layer, dst_ref, sems, a, k, pos, False).wait_send()
        _exchange_local(kind, src_ref, layer, dst_ref, sems, a, pos[3]).wait()


def _exchange_shapes(kind, srcs, layers):
    if kind in ("gather", "gather2"):
        return [jax.ShapeDtypeStruct((N_DEV,) + (s.shape if l is None else s.shape[1:]), s.dtype)
                for s, l in zip(srcs, layers)]
    return [jax.ShapeDtypeStruct(s.shape, s.dtype) for s in srcs]


def _exchange(kind, srcs, layers=None, *, name):
    n = len(srcs)
    layers = layers or [None] * n
    hbm = pl.BlockSpec(memory_space=pl.ANY)

    def body(*refs):
        src_refs, dst_refs, sems = refs[:n], refs[n:2 * n], refs[2 * n:]
        _exchange_start(kind, layers, src_refs, dst_refs, sems)
        _exchange_wait(kind, layers, src_refs, dst_refs, sems)

    return pl.pallas_call(body, name=name, in_specs=[hbm] * n, out_specs=[hbm] * n,
                          out_shape=_exchange_shapes(kind, srcs, layers), scratch_shapes=_exchange_sems(n))(*srcs)


def _pcall(body, args, *, name, grid, in_specs, out_specs, out_shape, scratch_shapes, sem, exchange=None):
    if exchange is None:
        outs = pl.pallas_call(body, name=name, grid=grid, in_specs=in_specs, out_specs=out_specs, out_shape=out_shape,
                              scratch_shapes=scratch_shapes, compiler_params=_cparams(*sem))(*args)
        return outs, None
    kind, srcs, layers = exchange
    n_in, n_out, n_scr, n_x = len(in_specs), len(out_specs), len(scratch_shapes), len(srcs)
    hbm = pl.BlockSpec(memory_space=pl.ANY)

    def carrier(*refs):
        ins, refs = refs[:n_in], refs[n_in:]
        src_refs, refs = refs[:n_x], refs[n_x:]
        outs, refs = refs[:n_out], refs[n_out:]
        dst_refs, refs = refs[:n_x], refs[n_x:]
        scr, sems = refs[:n_scr], refs[n_scr:]
        ids = [pl.program_id(a) for a in range(len(grid))]
        first = functools.reduce(jnp.logical_and, [i == 0 for i in ids])
        last = functools.reduce(jnp.logical_and, [i == g - 1 for i, g in zip(ids, grid)])

        @pl.when(first)
        def _():
            _exchange_start(kind, layers, src_refs, dst_refs, sems)

        body(*ins, *outs, *scr)

        @pl.when(last)
        def _():
            _exchange_wait(kind, layers, src_refs, dst_refs, sems)

    outs = pl.pallas_call(
        carrier, name=name, grid=grid, in_specs=list(in_specs) + [hbm] * n_x, out_specs=list(out_specs) + [hbm] * n_x,
        out_shape=list(out_shape) + _exchange_shapes(kind, srcs, layers),
        scratch_shapes=list(scratch_shapes) + _exchange_sems(n_x),
        compiler_params=_cparams(*(["arbitrary"] * len(grid))))(*args, *srcs)
    return outs[:n_out], outs[n_out:]


def _accumulate(acc_ref, part, k, nk, finish):
    if nk == 1:
        finish(part)
        return

    @pl.when(k == 0)
    def _():
        acc_ref[...] = part

    @pl.when(jnp.logical_and(k > 0, k < nk - 1))
    def _():
        acc_ref[...] += part

    @pl.when(k == nk - 1)
    def _():
        finish(acc_ref[...] + part)


def _matmul(a, b, *, dims, grid, a_spec, b_spec, o_spec, out_shape, acc_shape, name, scale=1.0,
            res=None, res_spec=None, exchange=None):
    nk = grid[3]
    has_res = res is not None

    def body(*refs):
        if has_res:
            a_ref, b_ref, r_ref, o_ref = refs[:4]
        else:
            a_ref, b_ref, o_ref = refs[:3]
            r_ref = None

        def finish(acc):
            if scale != 1.0:
                acc = acc * scale
            if r_ref is not None:
                acc = acc + r_ref[...].astype(F32)
            o_ref[...] = acc.astype(o_ref.dtype)

        part = lax.dot_general(_bf(a_ref[...]), _bf(b_ref[...]), (dims, ((), ())), preferred_element_type=F32)
        _accumulate(None if nk == 1 else refs[-1], part, pl.program_id(3), nk, finish)

    in_specs = [a_spec, b_spec] + ([res_spec] if has_res else [])
    args = (a, b) + ((res,) if has_res else ())
    outs, moved = _pcall(
        body, args, name=name, grid=grid, in_specs=in_specs, out_specs=[o_spec], out_shape=[out_shape],
        scratch_shapes=[] if nk == 1 else [pltpu.VMEM(acc_shape, F32)],
        sem=("parallel", "parallel", "parallel", "arbitrary"), exchange=exchange)
    return outs[0] if exchange is None else (outs[0], moved)


NT = ((1,), (1,))
NN = ((1,), (0,))
TN = ((0,), (0,))


def _mm_rows(a, b, *, dims, out_dtype, name, res=None, scale=1.0, tm=1024, tn=1024, tk=1024):
    m, kdim = a.shape
    n = b.shape[1] if dims == NN else b.shape[0]
    tm, tn, tk = _tile(m, tm), _tile(n, tn), _tile(kdim, tk)
    grid = (1, m // tm, n // tn, kdim // tk)
    a_spec = pl.BlockSpec((tm, tk), lambda s, i, j, k: (i, k))
    if dims == NN:
        b_spec = pl.BlockSpec((tk, tn), lambda s, i, j, k: (k, j))
    else:
        b_spec = pl.BlockSpec((tn, tk), lambda s, i, j, k: (j, k))
    o_spec = pl.BlockSpec((tm, tn), lambda s, i, j, k: (i, j))
    return _matmul(a, b, dims=dims, grid=grid, a_spec=a_spec, b_spec=b_spec, o_spec=o_spec,
                   out_shape=jax.ShapeDtypeStruct((m, n), out_dtype), acc_shape=(tm, tn), name=name,
                   res=res, res_spec=o_spec if res is not None else None, scale=scale)


def _mm_to_slices(a, b_t, *, name, tm=2048, exchange=None):
    m, kdim = a.shape
    n = b_t.shape[0]
    tm = _tile(m, tm)
    w = BRANCH_W
    grid = (1, m // tm, n // w, 1)
    return _matmul(a, b_t, dims=NT, grid=grid,
                   a_spec=pl.BlockSpec((tm, kdim), lambda s, i, j, k: (i, 0)),
                   b_spec=pl.BlockSpec((w, kdim), lambda s, i, j, k: (j, 0)),
                   o_spec=pl.BlockSpec((None, tm, w), lambda s, i, j, k: (j, i, 0)),
                   out_shape=jax.ShapeDtypeStruct((n // w, m, w), BF16), acc_shape=(tm, w), name=name,
                   exchange=exchange)


def _inproj_dh(parts, w_t, res, *, name, tm=512):
    t, n = res.shape
    w = BRANCH_W
    tm = _tile(t, tm)
    counts = [p.shape[0] for p in parts]
    assert sum(counts) * w == w_t.shape[0]

    def body(*refs):
        w_ref, r_ref, o_ref = refs[len(parts):]
        cols = jnp.concatenate([p_ref[s] for p_ref, cnt in zip(refs, counts) for s in range(cnt)], axis=1)
        o_ref[...] = r_ref[...] + _dot(cols, w_ref[...])

    row = pl.BlockSpec((tm, n), lambda i: (i, 0))
    return pl.pallas_call(
        body, name=name, grid=(t // tm,),
        in_specs=[pl.BlockSpec((cnt, tm, w), lambda i: (0, i, 0)) for cnt in counts]
        + [pl.BlockSpec(w_t.shape, lambda i: (0, 0)), row],
        out_specs=row, out_shape=jax.ShapeDtypeStruct((t, n), F32), compiler_params=_cparams("parallel"),
    )(*parts, w_t, res)


def _mm_tn(a, b, *, name, tm=1408, tn=1024, tk=TOKEN_TK, exchange=None):
    t, m = a.shape
    n = b.shape[1]
    tm, tn, tk = _tile(m, tm), _tile(n, tn), _tile(t, tk)
    grid = (1, m // tm, n // tn, t // tk)
    return _matmul(a, b, dims=TN, grid=grid,
                   a_spec=pl.BlockSpec((tk, tm), lambda s, i, j, k: (k, i)),
                   b_spec=pl.BlockSpec((tk, tn), lambda s, i, j, k: (k, j)),
                   o_spec=pl.BlockSpec((tm, tn), lambda s, i, j, k: (i, j)),
                   out_shape=jax.ShapeDtypeStruct((m, n), PARTIAL_DTYPE), acc_shape=(tm, tn), name=name,
                   exchange=exchange)


def _mm_tn_slices(a3, b, *, name, tn=1024, tk=TOKEN_TK):
    s_n, t, w = a3.shape
    n = b.shape[1]
    tn, tk = _tile(n, tn), _tile(t, tk)
    grid = (1, s_n, n // tn, t // tk)
    return _matmul(a3, b, dims=TN, grid=grid,
                   a_spec=pl.BlockSpec((None, tk, w), lambda s, i, j, k: (i, k, 0)),
                   b_spec=pl.BlockSpec((tk, tn), lambda s, i, j, k: (k, j)),
                   o_spec=pl.BlockSpec((w, tn), lambda s, i, j, k: (i, j)),
                   out_shape=jax.ShapeDtypeStruct((s_n * w, n), PARTIAL_DTYPE), acc_shape=(w, tn), name=name)


def _mm_tn_batch(a, b3, *, name, tm=1024, tn=1024, tk=TOKEN_TK):
    s_n, t, n = b3.shape
    m = a.shape[-1]
    tm, tn, tk = _tile(m, tm), _tile(n, tn), _tile(t, tk)
    grid = (s_n, m // tm, n // tn, t // tk)
    return _matmul(a, b3, dims=TN, grid=grid, a_spec=pl.BlockSpec((tk, tm), lambda s, i, j, k: (k, i)),
                   b_spec=pl.BlockSpec((None, tk, tn), lambda s, i, j, k: (s, k, j)),
                   o_spec=pl.BlockSpec((None, tm, tn), lambda s, i, j, k: (s, i, j)),
                   out_shape=jax.ShapeDtypeStruct((s_n, m, n), PARTIAL_DTYPE), acc_shape=(tm, tn), name=name)


def _norm_parts(xf):
    r = lax.rsqrt(jnp.mean(xf * xf, axis=-1, keepdims=True) + EPS)
    return xf * r, r


def _norm_bwd(dh, xhat, r, w):
    dxhat = dh * w
    dx = r * (dxhat - xhat * jnp.mean(dxhat * xhat, axis=-1, keepdims=True))
    return dx, jnp.sum(dh * xhat, axis=0, keepdims=True)


def _rmsnorm_fwd(x, w, *, name, tm=1024):
    t, d = x.shape
    tm = _tile(t, tm)

    def body(x_ref, w_ref, h_ref):
        xhat, _ = _norm_parts(x_ref[...])
        h_ref[...] = _bf(xhat * w_ref[...])

    return pl.pallas_call(
        body, name=name, grid=(t // tm,),
        in_specs=[pl.BlockSpec((tm, d), lambda i: (i, 0)), pl.BlockSpec((1, d), lambda i: (0, 0))],
        out_specs=pl.BlockSpec((tm, d), lambda i: (i, 0)),
        out_shape=jax.ShapeDtypeStruct((t, d), BF16), compiler_params=_cparams("parallel"),
    )(x, w)


def _rmsnorm_bwd(dh, x, w, dres, *, name, tm=1024):
    t, d = x.shape
    tm = _tile(t, tm)

    def body(dh_ref, x_ref, w_ref, dres_ref, dx_ref, dw_ref):
        xhat, r = _norm_parts(x_ref[...])
        dx, dw = _norm_bwd(dh_ref[...], xhat, r, w_ref[...])
        dx_ref[...] = dres_ref[...] + dx

        @pl.when(pl.program_id(0) == 0)
        def _():
            dw_ref[...] = dw

        @pl.when(pl.program_id(0) > 0)
        def _():
            dw_ref[...] += dw

    row = pl.BlockSpec((tm, d), lambda i: (i, 0))
    vec = pl.BlockSpec((1, d), lambda i: (0, 0))
    return pl.pallas_call(
        body, name=name, grid=(t // tm,), in_specs=[row, row, vec, row], out_specs=[row, vec],
        out_shape=[jax.ShapeDtypeStruct((t, d), F32), jax.ShapeDtypeStruct((1, d), F32)],
        compiler_params=_cparams("arbitrary"),
    )(dh, x, w, dres)


def _loss_fwd_bwd(x, w, target, *, name, tm=1024):
    t, d = x.shape
    tm = _tile(t, tm)

    def body(x_ref, w_ref, t_ref, dx_ref, dw_ref, loss_ref):
        xhat, r = _norm_parts(x_ref[...])
        wv = w_ref[...]
        err = xhat * wv - t_ref[...]
        dx, dw = _norm_bwd(err * (1.0 / d), xhat, r, wv)
        dx_ref[...] = dx
        part = jnp.full((1, 128), 0.5 / d, F32) * jnp.sum(err * err)

        @pl.when(pl.program_id(0) == 0)
        def _():
            dw_ref[...] = dw
            loss_ref[...] = part

        @pl.when(pl.program_id(0) > 0)
        def _():
            dw_ref[...] += dw
            loss_ref[...] += part

    row = pl.BlockSpec((tm, d), lambda i: (i, 0))
    vec = pl.BlockSpec((1, d), lambda i: (0, 0))
    return pl.pallas_call(
        body, name=name, grid=(t // tm,), in_specs=[row, vec, row],
        out_specs=[row, vec, pl.BlockSpec((1, 128), lambda i: (0, 0))],
        out_shape=[jax.ShapeDtypeStruct((t, d), F32), jax.ShapeDtypeStruct((1, d), F32),
                   jax.ShapeDtypeStruct((1, 128), F32)],
        compiler_params=_cparams("arbitrary"),
    )(x, w, target)


def _ffn_tiles(t, f):
    tf = f
    for cand in (1408, 1024, 512, 256, 128):
        if f % cand == 0:
            tf = cand
            break
    return _tile(t, 512), tf


def _ffn_fwd(x, nw, wg_t, wu_t, wd, *, name, exchange=None):
    t, d = x.shape
    f = wd.shape[0]
    tm, tf = _ffn_tiles(t, f)
    nf = f // tf

    def body(x_ref, nw_ref, wg_ref, wu_ref, wd_ref, xo_ref, h_ref, g_ref, u_ref, a_ref, hs_ref, acc_ref):
        j = pl.program_id(1)

        @pl.when(j == 0)
        def _():
            xhat, _ = _norm_parts(x_ref[...])
            hb = _bf(xhat * nw_ref[...])
            hs_ref[...] = hb
            h_ref[...] = hb

        hb = hs_ref[...]
        g = _dot_nt(hb, wg_ref[...])
        u = _dot_nt(hb, wu_ref[...])
        a = _bf(g * _sigmoid(g) * u)
        g_ref[...] = _bf(g)
        u_ref[...] = _bf(u)
        a_ref[...] = a
        part = _dot(a, wd_ref[...])

        def finish(total):
            xo_ref[...] = x_ref[...] + 0.5 * total

        _accumulate(acc_ref, part, j, nf, finish)

    row = pl.BlockSpec((tm, d), lambda i, j: (i, 0))
    wspec = pl.BlockSpec((tf, d), lambda i, j: (j, 0))
    hid = pl.BlockSpec((tm, tf), lambda i, j: (i, j))
    return _pcall(
        body, (x, nw, wg_t, wu_t, wd), name=name, grid=(t // tm, nf),
        in_specs=[row, pl.BlockSpec((1, d), lambda i, j: (0, 0)), wspec, wspec, wspec],
        out_specs=[row, row, hid, hid, hid],
        out_shape=[jax.ShapeDtypeStruct((t, d), F32), jax.ShapeDtypeStruct((t, d), BF16)]
        + [jax.ShapeDtypeStruct((t, f), BF16)] * 3,
        scratch_shapes=[pltpu.VMEM((tm, d), BF16), pltpu.VMEM((tm, d), F32)],
        sem=("parallel", "arbitrary"), exchange=exchange)


def _ffn_bwd(dxo, x, nw, g, u, wg_t, wu_t, wd, *, name, exchange=None):
    t, d = x.shape
    f = wd.shape[0]
    tm, tf = _ffn_tiles(t, f)
    nf = f // tf

    def body(dxo_ref, x_ref, nw_ref, g_ref, u_ref, wg_ref, wu_ref, wd_ref,
             dx_ref, dy_ref, dg_ref, du_ref, dnw_ref, dys_ref, acc_ref):
        i, j = pl.program_id(0), pl.program_id(1)

        @pl.when(j == 0)
        def _():
            dyb = _bf(0.5 * dxo_ref[...])
            dys_ref[...] = dyb
            dy_ref[...] = dyb

        da = _dot_nt(dys_ref[...], wd_ref[...])
        gv = g_ref[...].astype(F32)
        uv = u_ref[...].astype(F32)
        s = _sigmoid(gv)
        dg = _bf(da * uv * (s * (1.0 + gv * (1.0 - s))))
        du = _bf(da * (gv * s))
        dg_ref[...] = dg
        du_ref[...] = du
        part = _dot(dg, wg_ref[...]) + _dot(du, wu_ref[...])

        def finish(dh):
            xhat, r = _norm_parts(x_ref[...])
            dx, dw = _norm_bwd(dh, xhat, r, nw_ref[...])
            dx_ref[...] = dxo_ref[...] + dx

            @pl.when(i == 0)
            def _():
                dnw_ref[...] = dw

            @pl.when(i > 0)
            def _():
                dnw_ref[...] += dw

        _accumulate(acc_ref, part, j, nf, finish)

    row = pl.BlockSpec((tm, d), lambda i, j: (i, 0))
    vec = pl.BlockSpec((1, d), lambda i, j: (0, 0))
    wspec = pl.BlockSpec((tf, d), lambda i, j: (j, 0))
    hid = pl.BlockSpec((tm, tf), lambda i, j: (i, j))
    return _pcall(
        body, (dxo, x, nw, g, u, wg_t, wu_t, wd), name=name, grid=(t // tm, nf),
        in_specs=[row, row, vec, hid, hid, wspec, wspec, wspec],
        out_specs=[row, row, hid, hid, vec],
        out_shape=[jax.ShapeDtypeStruct((t, d), F32), jax.ShapeDtypeStruct((t, d), BF16),
                   jax.ShapeDtypeStruct((t, f), BF16), jax.ShapeDtypeStruct((t, f), BF16),
                   jax.ShapeDtypeStruct((1, d), F32)],
        scratch_shapes=[pltpu.VMEM((tm, d), BF16), pltpu.VMEM((tm, d), F32)],
        sem=("arbitrary", "arbitrary"), exchange=exchange)


def _shift_down(prev, cur, n):
    ext = jnp.concatenate([prev, cur], axis=0)
    return pltpu.roll(ext, n, axis=0)[prev.shape[0]:]


def _shift_up(cur, nxt, n):
    ext = jnp.concatenate([cur, nxt], axis=0)
    return pltpu.roll(ext, ext.shape[0] - n, axis=0)[:cur.shape[0]]


def _conv_specs(t, tb):
    hb = tb // CONV_HALO
    last = t // CONV_HALO - 1

    def tile(s):
        return pl.BlockSpec((None, tb, 128), lambda c, i: (s, i, c))

    def prev(s):
        return pl.BlockSpec((None, CONV_HALO, 128), lambda c, i: (s, jnp.maximum(i * hb - 1, 0), c))

    def nxt(s):
        return pl.BlockSpec((None, CONV_HALO, 128), lambda c, i: (s, jnp.minimum((i + 1) * hb, last), c))

    return tile, prev, nxt


def _conv_fwd(cols3, conv_w, *, name, tb=1024):
    _, t, bw = cols3.shape
    tb = _tile(t, tb)
    tile, prev, _ = _conv_specs(t, tb)

    def body(u_ref, b_ref, c_ref, up_ref, cp_ref, w_ref, y_ref):
        first = pl.program_id(1) == 0
        z = c_ref[...].astype(F32) * u_ref[...].astype(F32)
        zp = jnp.where(first, 0.0, cp_ref[...].astype(F32) * up_ref[...].astype(F32))
        conv = w_ref[0:1, :] * _shift_down(zp, z, 2) + w_ref[1:2, :] * _shift_down(zp, z, 1) + w_ref[2:3, :] * z
        y_ref[...] = _bf(b_ref[...].astype(F32) * conv)

    return pl.pallas_call(
        body, name=name, grid=(bw // 128, t // tb),
        in_specs=[tile(0), tile(1), tile(2), prev(0), prev(2), pl.BlockSpec((3, 128), lambda c, i: (0, c))],
        out_specs=pl.BlockSpec((tb, 128), lambda c, i: (i, c)),
        out_shape=jax.ShapeDtypeStruct((t, bw), BF16), compiler_params=_cparams("parallel", "parallel"),
    )(cols3, cols3, cols3, cols3, cols3, conv_w)


def _conv_bwd(cols3, conv_w, dy, *, name, tb=1024):
    _, t, bw = cols3.shape
    tb = _tile(t, tb)
    nt = t // tb
    tile, prev, nxt = _conv_specs(t, tb)
    hb = tb // CONV_HALO
    last = t // CONV_HALO - 1

    def body(u_ref, b_ref, c_ref, up_ref, cp_ref, bn_ref, dy_ref, dyn_ref, w_ref, d3_ref, dw_ref):
        i = pl.program_id(1)
        uv, bv, cv = u_ref[...].astype(F32), b_ref[...].astype(F32), c_ref[...].astype(F32)
        dyv = dy_ref[...].astype(F32)
        z = cv * uv
        zp = jnp.where(i == 0, 0.0, cp_ref[...].astype(F32) * up_ref[...].astype(F32))
        z1, z2 = _shift_down(zp, z, 1), _shift_down(zp, z, 2)
        w0, w1, w2 = w_ref[0:1, :], w_ref[1:2, :], w_ref[2:3, :]
        conv = w0 * z2 + w1 * z1 + w2 * z
        dconv = dyv * bv
        dconv_n = jnp.where(i == nt - 1, 0.0, dyn_ref[...].astype(F32) * bn_ref[...].astype(F32))
        dz = w2 * dconv + w1 * _shift_up(dconv, dconv_n, 1) + w0 * _shift_up(dconv, dconv_n, 2)
        d3_ref[0] = _bf(dz * cv)
        d3_ref[1] = _bf(dyv * conv)
        d3_ref[2] = _bf(dz * uv)
        dws = [jnp.sum(dconv * zz, axis=0, keepdims=True) for zz in (z2, z1, z)]

        @pl.when(i == 0)
        def _():
            for j in range(3):
                dw_ref[j:j + 1, :] = dws[j]

        @pl.when(i > 0)
        def _():
            for j in range(3):
                dw_ref[j:j + 1, :] += dws[j]

    dy_tile = pl.BlockSpec((None, tb, 128), lambda c, i: (0, i, c))
    dy_next = pl.BlockSpec((None, CONV_HALO, 128), lambda c, i: (0, jnp.minimum((i + 1) * hb, last), c))
    wspec = pl.BlockSpec((3, 128), lambda c, i: (0, c))
    return pl.pallas_call(
        body, name=name, grid=(bw // 128, nt),
        in_specs=[tile(0), tile(1), tile(2), prev(0), prev(2), nxt(1), dy_tile, dy_next, wspec],
        out_specs=[pl.BlockSpec((3, tb, 128), lambda c, i: (0, i, c)), wspec],
        out_shape=[jax.ShapeDtypeStruct((3, t, bw), BF16), jax.ShapeDtypeStruct((3, bw), F32)],
        compiler_params=_cparams("parallel", "arbitrary"),
    )(cols3, cols3, cols3, cols3, cols3, cols3, dy, dy, conv_w)


def _ret_consts():
    log_gamma = jnp.log1p(-jnp.exp2(-5.0 - jnp.arange(H_RET, dtype=F32)))
    pos = jnp.arange(CHUNK, dtype=F32)
    d_intra = jnp.exp(log_gamma[:, None, None] * jnp.abs(pos[:, None] - pos[None, :]))
    q_decay = jnp.exp(log_gamma[:, None] * (pos + 1.0))
    k_decay = jnp.exp(log_gamma[:, None] * (CHUNK - 1.0 - pos))
    chunk_decay = jnp.exp(log_gamma * CHUNK)
    wide = (H_RET, CHUNK, DK_RET)
    return (d_intra, jnp.broadcast_to(q_decay[:, :, None], wide), jnp.broadcast_to(k_decay[:, :, None], wide),
            jnp.broadcast_to(chunk_decay[:, None, None], (H_RET, 1, DK_RET)))


def _rope_tables(t):
    inv_freq = np.float32(ROPE_BASE) ** (-np.linspace(0.0, 1.0, DK_RET // 2, dtype=np.float32))
    ang = np.arange(t, dtype=np.float32)[:, None] * inv_freq[None, :]
    cos, sin = np.cos(ang), np.sin(ang)
    return jnp.asarray(np.concatenate([cos, cos], axis=1)), jnp.asarray(np.concatenate([-sin, sin], axis=1))


def _rope(v, cc, ss):
    return v * cc + pltpu.roll(v, DK_RET // 2, axis=1) * ss


def _ret_in_specs(tb, blk):
    def col(s):
        return pl.BlockSpec((None, tb, H_RET * DK_RET), lambda i: (s, blk(i), 0))

    tab = pl.BlockSpec((tb, DK_RET), lambda i: (blk(i), 0))
    return ([col(3), col(4), col(5), col(6), tab, tab,
             pl.BlockSpec((H_RET, CHUNK, CHUNK), lambda i: (0, 0, 0)),
             pl.BlockSpec((H_RET, CHUNK, DK_RET), lambda i: (0, 0, 0)),
             pl.BlockSpec((H_RET, CHUNK, DK_RET), lambda i: (0, 0, 0)),
             pl.BlockSpec((H_RET, 1, DK_RET), lambda i: (0, 0, 0))])


def _ret_fwd(cols3, tables, consts, *, name):
    _, t, bw = cols3.shape
    tb = _tile(t, RET_TB)
    ncb = tb // CHUNK
    scale = DK_RET ** -0.5

    def body(q_ref, k_ref, v_ref, g_ref, cc_ref, ss_ref, di_ref, qd_ref, kd_ref, cd_ref, y_ref, st_ref, state):
        @pl.when(pl.program_id(0) == 0)
        def _():
            state[...] = jnp.zeros_like(state)

        heads = range(H_RET)
        units = [(c, h) for c in range(ncb) for h in heads]
        every = range(len(units))
        rows = [pl.ds(c * CHUNK, CHUNK) for c, _ in units]
        lanes = [pl.ds(h * DK_RET, DK_RET) for _, h in units]
        hd = [h for _, h in units]
        cc, ss = [cc_ref[rw, :] for rw in rows], [ss_ref[rw, :] for rw in rows]
        qs = [_rope(q_ref[rows[u], lanes[u]].astype(F32), cc[u], ss[u]) * scale for u in every]
        kr = [_rope(k_ref[rows[u], lanes[u]].astype(F32), cc[u], ss[u]) for u in every]
        vb = [v_ref[rows[u], lanes[u]] for u in every]
        gv = [g_ref[rows[u], lanes[u]].astype(F32) for u in every]
        innerb = [_bf(_dot_nt(_bf(qs[u]), _bf(kr[u])) * di_ref[hd[u]]) for u in every]
        update = [_dot_tn(_bf(kr[u] * kd_ref[hd[u]]), vb[u]) for u in every]
        carried = [state[h] for h in heads]
        s_in = []
        for u, (c, h) in enumerate(units):
            s_in.append(carried[h])
            st_ref[h, c] = carried[h]
            carried[h] = carried[h] * cd_ref[h] + update[u]
        for h in heads:
            state[h] = carried[h]
        o = [_dot(innerb[u], vb[u]) + _dot(_bf(qs[u] * qd_ref[hd[u]]), _bf(s_in[u])) for u in every]
        for u in every:
            on = o[u] * lax.rsqrt(jnp.mean(o[u] * o[u], axis=-1, keepdims=True) + EPS)
            y_ref[rows[u], lanes[u]] = _bf(gv[u] * _sigmoid(gv[u]) * on)

    return pl.pallas_call(
        body, name=name, grid=(t // tb,),
        in_specs=_ret_in_specs(tb, lambda i: i),
        out_specs=[pl.BlockSpec((tb, bw), lambda i: (i, 0)),
                   pl.BlockSpec((H_RET, ncb, DK_RET, DK_RET), lambda i: (0, i, 0, 0))],
        out_shape=[jax.ShapeDtypeStruct((t, bw), BF16),
                   jax.ShapeDtypeStruct((H_RET, t // CHUNK, DK_RET, DK_RET), F32)],
        scratch_shapes=[pltpu.VMEM((H_RET, DK_RET, DK_RET), F32)],
        compiler_params=_cparams("arbitrary"),
    )(cols3, cols3, cols3, cols3, tables[0], tables[1], *consts)


def _ret_bwd(cols3, tables, consts, states, dy, *, name, exchange=None):
    _, t, bw = cols3.shape
    tb = _tile(t, RET_TB)
    ncb = tb // CHUNK
    nb = t // tb
    scale = DK_RET ** -0.5

    def body(q_ref, k_ref, v_ref, g_ref, cc_ref, ss_ref, di_ref, qd_ref, kd_ref, cd_ref, st_ref, dy_ref,
             d4_ref, dstate):
        @pl.when(pl.program_id(0) == 0)
        def _():
            dstate[...] = jnp.zeros_like(dstate)

        heads = range(H_RET)
        lanes_of = [pl.ds(h * DK_RET, DK_RET) for h in heads]
        di, qd, kd = [di_ref[h] for h in heads], [qd_ref[h] for h in heads], [kd_ref[h] for h in heads]
        carried = [dstate[h] for h in heads]
        group = RET_GROUP if ncb % RET_GROUP == 0 else 1
        for c0 in reversed(range(0, ncb, group)):
            units = [(c, h) for c in reversed(range(c0, c0 + group)) for h in heads]
            rows = [pl.ds(c * CHUNK, CHUNK) for c, _ in units]
            cc, ss = [cc_ref[rw, :] for rw in rows], [ss_ref[rw, :] for rw in rows]
            every = range(len(units))
            hd = [h for _, h in units]
            vb = [v_ref[rows[u], lanes_of[hd[u]]] for u in every]
            gv = [g_ref[rows[u], lanes_of[hd[u]]].astype(F32) for u in every]
            dyv = [dy_ref[rows[u], lanes_of[hd[u]]].astype(F32) for u in every]
            s_in = [_bf(st_ref[h, c]) for c, h in units]
            qs = [_rope(q_ref[rows[u], lanes_of[hd[u]]].astype(F32), cc[u], ss[u]) * scale for u in every]
            kr = [_rope(k_ref[rows[u], lanes_of[hd[u]]].astype(F32), cc[u], ss[u]) for u in every]
            qsb, krb = [_bf(v) for v in qs], [_bf(v) for v in kr]
            qdb = [_bf(qs[u] * qd[hd[u]]) for u in every]
            kdb = [_bf(kr[u] * kd[hd[u]]) for u in every]
            innerb = [_bf(_dot_nt(qsb[u], krb[u]) * di[hd[u]]) for u in every]
            o = [_dot(innerb[u], vb[u]) + _dot(qdb[u], s_in[u]) for u in every]
            r = [lax.rsqrt(jnp.mean(v * v, axis=-1, keepdims=True) + EPS) for v in o]
            on = [o[u] * r[u] for u in every]
            sg = [_sigmoid(v) for v in gv]
            dgv = [_bf(dyv[u] * on[u] * (sg[u] * (1.0 + gv[u] * (1.0 - sg[u])))) for u in every]
            don = [dyv[u] * (gv[u] * sg[u]) for u in every]
            dob = [_bf(r[u] * (don[u] - on[u] * jnp.mean(don[u] * on[u], axis=-1, keepdims=True))) for u in every]
            dinner = [_bf(_dot_nt(dob[u], vb[u]) * di[hd[u]]) for u in every]
            dqs = [_dot(dinner[u], krb[u]) + _dot_nt(dob[u], s_in[u]) * qd[hd[u]] for u in every]
            dkr = [_dot_tn(dinner[u], qsb[u]) for u in every]
            dv = [_dot_tn(innerb[u], dob[u]) for u in every]
            dnew = [_dot_tn(qdb[u], dob[u]) for u in every]
            for u in every:
                h = hd[u]
                dstb = _bf(carried[h])
                dv[u] = dv[u] + _dot(kdb[u], dstb)
                dkr[u] = dkr[u] + _dot_nt(vb[u], dstb) * kd[h]
                carried[h] = carried[h] * cd_ref[h] + dnew[u]
            for u in every:
                ln = lanes_of[hd[u]]
                d4_ref[0, rows[u], ln] = _bf(_rope_bwd(dqs[u] * scale, cc[u], ss[u]))
                d4_ref[1, rows[u], ln] = _bf(_rope_bwd(dkr[u], cc[u], ss[u]))
                d4_ref[2, rows[u], ln] = _bf(dv[u])
                d4_ref[3, rows[u], ln] = dgv[u]
        for h in heads:
            dstate[h] = carried[h]

    rev = lambda i: nb - 1 - i
    outs, moved = _pcall(
        body, (cols3, cols3, cols3, cols3, tables[0], tables[1], *consts, states, dy), name=name, grid=(nb,),
        in_specs=_ret_in_specs(tb, rev)
        + [pl.BlockSpec((H_RET, ncb, DK_RET, DK_RET), lambda i: (0, rev(i), 0, 0)),
           pl.BlockSpec((None, tb, bw), lambda i: (1, rev(i), 0))],
        out_specs=[pl.BlockSpec((4, tb, bw), lambda i: (0, rev(i), 0))],
        out_shape=[jax.ShapeDtypeStruct((4, t, bw), BF16)],
        scratch_shapes=[pltpu.VMEM((H_RET, DK_RET, DK_RET), F32)],
        sem=("arbitrary",), exchange=exchange)
    return outs[0], moved


def _rope_bwd(dv, cc, ss):
    return dv * cc + pltpu.roll(dv * ss, DK_RET // 2, axis=1)


def _att_window(i):
    return pl.multiple_of(jnp.maximum(i - ATT_LOOKBACK // ATT_QB, 0) * ATT_QB, ATT_QB)


def _att_mask(v):
    qchunk = (v * ATT_QB + lax.broadcasted_iota(jnp.int32, (ATT_QB, ATT_WIN), 0)) // CHUNK
    kchunk = lax.broadcasted_iota(jnp.int32, (ATT_QB, ATT_WIN), 1) // CHUNK
    return (kchunk <= qchunk) & (kchunk >= qchunk - N_PREV_CHUNKS)


def _bias_spec(layer, nvar):
    return pl.BlockSpec((None, None, 2, ATT_QB, ATT_WIN), lambda hp, i: (layer, jnp.minimum(i, nvar), hp, 0, 0))


def _att_fwd(cols3, bias3, *, name, exchange=None):
    _, t, bw = cols3.shape
    assert t % ATT_QB == 0 and t >= ATT_WIN
    scale = DH_ATT ** -0.5
    nvar = ATT_LOOKBACK // ATT_QB

    def body(q_ref, k_ref, v_ref, b_ref, y_ref, lse_ref):
        i = pl.program_id(1)
        ws = _att_window(i)
        q = q_ref[...].astype(F32)
        kw = k_ref[pl.ds(ws, ATT_WIN), :]
        vw = v_ref[pl.ds(ws, ATT_WIN), :]
        lane_head = lax.broadcasted_iota(jnp.int32, (ATT_QB, 128), 1) // DH_ATT
        out = jnp.zeros((ATT_QB, 128), F32)
        lse = jnp.zeros((ATT_QB, 128), F32)
        for hh in range(2):
            mine = lane_head == hh
            s = _dot_nt(_bf(jnp.where(mine, q, 0.0)), kw) * scale + b_ref[hh]
            mx = jnp.max(s, axis=-1, keepdims=True)
            p = jnp.exp(s - mx)
            l = jnp.sum(p, axis=-1, keepdims=True)
            out = jnp.where(mine, _dot(_bf(p), vw) / l, out)
            lse = jnp.where(mine, mx + jnp.log(l), lse)
        y_ref[...] = _bf(out)
        lse_ref[...] = lse

    kv = lambda s: pl.BlockSpec((None, t, 128), lambda hp, i: (s, 0, hp))
    return _pcall(
        body, (cols3, cols3, cols3, bias3[0]), name=name, grid=(H_ATT // 2, t // ATT_QB),
        in_specs=[pl.BlockSpec((None, ATT_QB, 128), lambda hp, i: (7, i, hp)), kv(8), kv(9),
                  _bias_spec(bias3[1], nvar)],
        out_specs=[pl.BlockSpec((ATT_QB, 128), lambda hp, i: (i, hp)),
                   pl.BlockSpec((None, ATT_QB, 128), lambda hp, i: (hp, i, 0))],
        out_shape=[jax.ShapeDtypeStruct((t, bw), BF16), jax.ShapeDtypeStruct((H_ATT // 2, t, 128), F32)],
        scratch_shapes=[], sem=("parallel", "arbitrary"), exchange=exchange)


def _att_bwd(cols3, bias3, y, lse, dy, *, name, exchange=None):
    _, t, bw = cols3.shape
    nq = t // ATT_QB
    scale = DH_ATT ** -0.5
    nvar = ATT_LOOKBACK // ATT_QB

    def body(q_ref, k_ref, v_ref, b_ref, y_ref, lse_ref, dy_ref, d3_ref, db_ref, dk_acc, dv_acc):
        i = pl.program_id(1)

        @pl.when(i == 0)
        def _():
            dk_acc[...] = jnp.zeros_like(dk_acc)
            dv_acc[...] = jnp.zeros_like(dv_acc)

        ws = _att_window(i)
        q = q_ref[...].astype(F32)
        kw = k_ref[pl.ds(ws, ATT_WIN), :]
        vw = v_ref[pl.ds(ws, ATT_WIN), :]
        do = dy_ref[...].astype(F32)
        dof = do * y_ref[...].astype(F32)
        lsev = lse_ref[...]
        lane_head = lax.broadcasted_iota(jnp.int32, (ATT_QB, 128), 1) // DH_ATT
        dq = jnp.zeros((ATT_QB, 128), F32)
        dk = jnp.zeros((ATT_WIN, 128), F32)
        dv = jnp.zeros((ATT_WIN, 128), F32)
        first = i <= nvar
        for hh in range(2):
            mine = lane_head == hh
            qh = _bf(jnp.where(mine, q, 0.0))
            doh = _bf(jnp.where(mine, do, 0.0))
            s = _dot_nt(qh, kw) * scale + b_ref[hh]
            lse_h = jnp.max(jnp.where(mine, lsev, NEG_INF), axis=-1, keepdims=True)
            p = jnp.exp(s - lse_h)
            delta = jnp.sum(jnp.where(mine, dof, 0.0), axis=-1, keepdims=True)
            ds = p * (_dot_nt(doh, vw) - delta)

            @pl.when(first)
            def _():
                db_ref[hh] = ds

            @pl.when(jnp.logical_not(first))
            def _():
                db_ref[hh] += ds

            dsb = _bf(ds * scale)
            dq = jnp.where(mine, _dot(dsb, kw), dq)
            dk = dk + _dot_tn(dsb, qh)
            dv = dv + _dot_tn(_bf(p), doh)
        d3_ref[0, pl.ds(pl.multiple_of(i * ATT_QB, ATT_QB), ATT_QB), :] = _bf(dq)
        dk_acc[pl.ds(ws, ATT_WIN), :] += dk
        dv_acc[pl.ds(ws, ATT_WIN), :] += dv

        @pl.when(i == nq - 1)
        def _():
            d3_ref[1] = _bf(dk_acc[...])
            d3_ref[2] = _bf(dv_acc[...])

    kv = lambda s: pl.BlockSpec((None, t, 128), lambda hp, i: (s, 0, hp))
    qrow = pl.BlockSpec((ATT_QB, 128), lambda hp, i: (i, hp))
    btile = pl.BlockSpec((None, 2, ATT_QB, ATT_WIN), lambda hp, i: (jnp.minimum(i, nvar), hp, 0, 0))
    return _pcall(
        body, (cols3, cols3, cols3, bias3[0], y, lse, dy), name=name, grid=(H_ATT // 2, nq),
        in_specs=[pl.BlockSpec((None, ATT_QB, 128), lambda hp, i: (7, i, hp)), kv(8), kv(9),
                  _bias_spec(bias3[1], nvar), qrow,
                  pl.BlockSpec((None, ATT_QB, 128), lambda hp, i: (hp, i, 0)),
                  pl.BlockSpec((None, ATT_QB, 128), lambda hp, i: (2, i, hp))],
        out_specs=[pl.BlockSpec((3, t, 128), lambda hp, i: (0, 0, hp)), btile],
        out_shape=[jax.ShapeDtypeStruct((3, t, bw), BF16),
                   jax.ShapeDtypeStruct((nvar + 1, H_ATT, ATT_QB, ATT_WIN), F32)],
        scratch_shapes=[pltpu.VMEM((t, 128), F32), pltpu.VMEM((t, 128), F32)],
        sem=("parallel", "arbitrary"), exchange=exchange)


SKEW_W = ATT_WIN + ATT_QB
REL_PAD = 384


def _rel_onehot(v):
    r = lax.broadcasted_iota(jnp.int32, (REL_PAD, SKEW_W), 0)
    j = lax.broadcasted_iota(jnp.int32, (REL_PAD, SKEW_W), 1)
    dist = jnp.where(j < ATT_WIN, v * ATT_QB - j, v * ATT_QB + SKEW_W - j)
    col = jnp.clip(dist, -REL_CLIP, REL_CLIP) + REL_CLIP
    return _bf(jnp.where(col == r, 1.0, 0.0))


def _split3(v):
    hi = _bf(v)
    rest = v - hi.astype(F32)
    mid = _bf(rest)
    return hi, mid, _bf(rest - mid.astype(F32))


def _skew8(a, forward):
    row = lax.broadcasted_iota(jnp.int32, a.shape, 0)
    for b in range(3):
        shift = (1 << b) if forward else SKEW_W - (1 << b)
        a = jnp.where(((row >> b) & 1) == 1, pltpu.roll(a, shift, axis=1), a)
    return a


def _toeplitz_rows(ext_row):
    a = _skew8(jnp.broadcast_to(ext_row, (8, SKEW_W)), True)
    while a.shape[0] < ATT_QB:
        a = jnp.concatenate([a, pltpu.roll(a, a.shape[0], axis=1)], axis=0)
    return a


def _diagonal_sums(tile):
    a = tile
    while a.shape[0] > 8:
        half = a.shape[0] // 2
        a = a[:half] + pltpu.roll(a[half:], SKEW_W - half, axis=1)
    return jnp.sum(_skew8(a, False), axis=0, keepdims=True)


def _bias_tiles(rel_bias, *, name, exchange=None):
    depth = rel_bias.shape[0]
    nvar = ATT_LOOKBACK // ATT_QB + 1
    rel = jnp.pad(rel_bias, ((0, 0), (0, 0), (0, REL_PAD - N_REL)))

    def body(rel_ref, o_ref, ext_ref):
        v = pl.program_id(1)
        onehot = _rel_onehot(v)
        ext_ref[...] = sum(_dot(part, onehot) for part in _split3(rel_ref[...]))
        valid = _att_mask(v)
        for h in range(H_ATT):
            o_ref[h] = jnp.where(valid, _toeplitz_rows(ext_ref[h:h + 1, :])[:, :ATT_WIN], NEG_INF)

    outs, moved = _pcall(
        body, (rel,), name=name, grid=(depth, nvar),
        in_specs=[pl.BlockSpec((None, H_ATT, REL_PAD), lambda l, v: (l, 0, 0))],
        out_specs=[pl.BlockSpec((None, None, H_ATT, ATT_QB, ATT_WIN), lambda l, v: (l, v, 0, 0, 0))],
        out_shape=[jax.ShapeDtypeStruct((depth, nvar, H_ATT, ATT_QB, ATT_WIN), F32)],
        scratch_shapes=[pltpu.VMEM((H_ATT, SKEW_W), F32)], sem=("parallel", "parallel"), exchange=exchange)
    return outs[0], moved


def _rel_bias_grad(dbias3, *, name):
    nvar = dbias3.shape[0]

    def body(db_ref, o_ref, diag_ref):
        v = pl.program_id(0)
        for h in range(H_ATT):
            tile = jnp.concatenate([db_ref[h], jnp.zeros((ATT_QB, ATT_QB), F32)], axis=1)
            diag_ref[h:h + 1, :] = _diagonal_sums(tile)
        onehot = _rel_onehot(v)
        part = sum(_dot_nt(p, onehot) for p in _split3(diag_ref[...]))

        @pl.when(v == 0)
        def _():
            o_ref[...] = part

        @pl.when(v > 0)
        def _():
            o_ref[...] += part

    out = pl.pallas_call(
        body, name=name, grid=(nvar,),
        in_specs=[pl.BlockSpec((None, H_ATT, ATT_QB, ATT_WIN), lambda v: (v, 0, 0, 0))],
        out_specs=pl.BlockSpec((H_ATT, REL_PAD), lambda v: (0, 0)),
        out_shape=jax.ShapeDtypeStruct((H_ATT, REL_PAD), F32),
        scratch_shapes=[pltpu.VMEM((H_ATT, SKEW_W), F32)], compiler_params=_cparams("arbitrary"),
    )(dbias3)
    return out[:, :N_REL]


def _merge_fwd(h, ys, wmg, wb_t, *, name, tm=512):
    t, d = h.shape
    bw = ys[0].shape[1]
    tm = _tile(t, tm)

    def body(h_ref, y0_ref, y1_ref, y2_ref, wg_ref, wb_ref, m_ref, s_ref, p_ref):
        hv = h_ref[...]
        total = jnp.zeros((tm, d), F32)
        for b, y_ref in enumerate((y0_ref, y1_ref, y2_ref)):
            s = _sigmoid(_dot(hv, wg_ref[b]))
            p = _dot_nt(y_ref[...], wb_ref[b])
            s_ref[b] = _bf(s)
            p_ref[b] = _bf(p)
            total = total + s * p
        m_ref[...] = _bf(total)

    row = pl.BlockSpec((tm, d), lambda i: (i, 0))
    yrow = pl.BlockSpec((tm, bw), lambda i: (i, 0))
    three = pl.BlockSpec((3, tm, d), lambda i: (0, i, 0))
    return pl.pallas_call(
        body, name=name, grid=(t // tm,),
        in_specs=[row, yrow, yrow, yrow, pl.BlockSpec((3, d, d), lambda i: (0, 0, 0)),
                  pl.BlockSpec((3, d, bw), lambda i: (0, 0, 0))],
        out_specs=[row, three, three],
        out_shape=[jax.ShapeDtypeStruct((t, d), BF16), jax.ShapeDtypeStruct((3, t, d), BF16),
                   jax.ShapeDtypeStruct((3, t, d), BF16)],
        compiler_params=_cparams("parallel"),
    )(h, *ys, wmg, wb_t)


def _merge_dwb(dp3, ys, *, name, tm=1024, tk=TOKEN_TK):
    nb, t, m = dp3.shape
    n = ys[0].shape[1]
    tm, tk = _tile(m, tm), _tile(t, tk)
    nk = t // tk

    def body(a_ref, *refs):
        o_ref, acc_ref = refs[nb], refs[nb + 1]
        k = pl.program_id(2)

        def finish(total):
            o_ref[...] = total.astype(o_ref.dtype)

        for b in range(nb):
            @pl.when(pl.program_id(0) == b)
            def _(y_ref=refs[b]):
                _accumulate(acc_ref, _dot_tn(a_ref[...], y_ref[...]), k, nk, finish)

    def y_spec(b):
        return pl.BlockSpec((tk, n), lambda s, i, k: (jnp.where(s == b, k, jnp.where(s < b, 0, nk - 1)), 0))

    return pl.pallas_call(
        body, name=name, grid=(nb, m // tm, nk),
        in_specs=[pl.BlockSpec((None, tk, tm), lambda s, i, k: (s, k, i))] + [y_spec(b) for b in range(nb)],
        out_specs=pl.BlockSpec((None, tm, n), lambda s, i, k: (s, i, 0)),
        out_shape=jax.ShapeDtypeStruct((nb, m, n), PARTIAL_DTYPE), scratch_shapes=[pltpu.VMEM((tm, n), F32)],
        compiler_params=_cparams("arbitrary", "arbitrary", "arbitrary"),
    )(dp3, *ys)


def _merge_bwd(dm, s3, p3, wmg, wb_t, *, name, tm=512):
    _, t, d = s3.shape
    bw = wb_t.shape[2]
    tm = _tile(t, tm)

    def body(dm_ref, s_ref, p_ref, wg_ref, wb_ref, dgp_ref, dp_ref, dy_ref, dh_ref):
        dmv = dm_ref[...].astype(F32)
        dh = jnp.zeros((tm, d), F32)
        for b in range(3):
            s = s_ref[b].astype(F32)
            dgp = _bf(dmv * p_ref[b].astype(F32) * s * (1.0 - s))
            dp = _bf(dmv * s)
            dgp_ref[b] = dgp
            dp_ref[b] = dp
            dy_ref[b] = _bf(_dot(dp, wb_ref[b]))
            dh = dh + _dot_nt(dgp, wg_ref[b])
        dh_ref[...] = dh

    row = pl.BlockSpec((tm, d), lambda i: (i, 0))
    three = pl.BlockSpec((3, tm, d), lambda i: (0, i, 0))
    return pl.pallas_call(
        body, name=name, grid=(t // tm,),
        in_specs=[row, three, three, pl.BlockSpec((3, d, d), lambda i: (0, 0, 0)),
                  pl.BlockSpec((3, d, bw), lambda i: (0, 0, 0))],
        out_specs=[three, three, pl.BlockSpec((3, tm, bw), lambda i: (0, i, 0)), row],
        out_shape=[jax.ShapeDtypeStruct((3, t, d), BF16), jax.ShapeDtypeStruct((3, t, d), BF16),
                   jax.ShapeDtypeStruct((3, t, bw), BF16), jax.ShapeDtypeStruct((t, d), F32)],
        compiler_params=_cparams("parallel"),
    )(dm, s3, p3, wmg, wb_t)


def _carried(hooks, slot, state=None):
    if not hooks or slot not in hooks:
        return None, lambda buf: None
    make, done = hooks[slot]
    return make(state), done


def _layer_fwd(x, p, aux, tag, hooks=None):
    ex, done = _carried(hooks, "ffn1")
    (x1, h1, g1, u1, a1), buf = _ffn_fwd(x, p["n1"], p["wg1"], p["wu1"], p["wd1"], name=f"ffn1_fwd_{tag}", exchange=ex)
    done(buf)
    h2 = _rmsnorm_fwd(x1, p["nmix"], name=f"mixnorm_fwd_{tag}")
    ex, done = _carried(hooks, "inproj")
    if ex is None:
        cols3 = _mm_to_slices(h2, p["win"], name=f"inproj_fwd_{tag}")
    else:
        cols3, buf = _mm_to_slices(h2, p["win"], name=f"inproj_fwd_{tag}", exchange=ex)
        done(buf)
    bias3 = p["bias3"]
    y_conv = _conv_fwd(cols3, p["conv_w"], name=f"conv_fwd_{tag}")
    y_ret, states = _ret_fwd(cols3, aux["rope"], aux["ret"], name=f"ret_fwd_{tag}")
    ex, done = _carried(hooks, "att")
    (y_att, lse), buf = _att_fwd(cols3, bias3, name=f"att_fwd_{tag}", exchange=ex)
    done(buf)
    merged, s3, p3 = _merge_fwd(h2, (y_conv, y_ret, y_att), p["wmg"], p["wb"], name=f"merge_fwd_{tag}")
    x2 = _mm_rows(merged, p["wout"], dims=NN, out_dtype=F32, res=x1, name=f"outproj_fwd_{tag}")
    ex, done = _carried(hooks, "ffn2")
    (x3, h3, g2, u2, a2), buf = _ffn_fwd(x2, p["n2"], p["wg2"], p["wu2"], p["wd2"], name=f"ffn2_fwd_{tag}", exchange=ex)
    done(buf)
    saved = dict(x0=x, h1=h1, g1=g1, u1=u1, a1=a1, x1=x1, h2=h2, cols3=cols3, bias3=bias3, y_conv=y_conv,
                 y_ret=y_ret, states=states, y_att=y_att, lse=lse, merged=merged, s3=s3, p3=p3, x2=x2, h3=h3,
                 g2=g2, u2=u2, a2=a2)
    return x3, saved


def _ffn_grads(dxo, x, h, g, u, a, p, which, tag, grads, all_grads, hooks):
    n = which
    ex, done = _carried(hooks, f"ffn{n}_bwd", all_grads)
    (dx, dyb, dg, du, grads["n" + n]), buf = _ffn_bwd(dxo, x, p["n" + n], g, u, p["wg" + n], p["wu" + n], p["wd" + n],
                                                     name=f"ffn{n}_{tag}_bwd", exchange=ex)
    done(buf)
    grads["wd" + n] = _mm_tn(a, dyb, name=f"ffn{n}_{tag}_dwd")
    for key, lhs in (("wg" + n, dg), ("wu" + n, du)):
        ex, done = _carried(hooks, f"ffn{n}_d{key[:2]}", all_grads)
        if ex is None:
            grads[key] = _mm_tn(lhs, h, name=f"ffn{n}_{tag}_d{key[:2]}")
        else:
            grads[key], buf = _mm_tn(lhs, h, name=f"ffn{n}_{tag}_d{key[:2]}", exchange=ex)
            done(buf)
    return dx


def _layer_bwd(dx3, p, s, aux, tag, grads, all_grads, hooks=None):
    dx2 = _ffn_grads(dx3, s["x2"], s["h3"], s["g2"], s["u2"], s["a2"], p, "2", tag, grads, all_grads, hooks)

    dm = _mm_rows(dx2, p["wout"], dims=NT, out_dtype=BF16, name=f"outproj_dm_{tag}")
    grads["wout"] = _mm_tn(s["merged"], dx2, name=f"outproj_dw_{tag}")
    dgp3, dp3, dy3, dh2 = _merge_bwd(dm, s["s3"], s["p3"], p["wmg"], p["wb"], name=f"merge_bwd_{tag}")
    grads["wmg"] = _mm_tn_batch(s["h2"], dgp3, name=f"merge_dwg_{tag}")
    grads["wb"] = _merge_dwb(dp3, (s["y_conv"], s["y_ret"], s["y_att"]), name=f"merge_dwb_{tag}")

    dconv3, grads["conv_w"] = _conv_bwd(s["cols3"], p["conv_w"], dy3, name=f"conv_bwd_{tag}")
    ex, done = _carried(hooks, "ret_bwd", all_grads)
    dret4, buf = _ret_bwd(s["cols3"], aux["rope"], aux["ret"], s["states"], dy3, name=f"ret_bwd_{tag}", exchange=ex)
    done(buf)
    ex, done = _carried(hooks, "att_bwd", all_grads)
    (datt3, dbias3), buf = _att_bwd(s["cols3"], s["bias3"], s["y_att"], s["lse"], dy3, name=f"att_bwd_{tag}",
                                    exchange=ex)
    done(buf)
    grads["rel_bias"] = _rel_bias_grad(dbias3, name=f"bias_grad_{tag}")

    dh2 = _inproj_dh((dconv3, dret4, datt3), p["win"], dh2, name=f"inproj_dh_{tag}")
    grads["win"] = jnp.concatenate([
        _mm_tn_slices(dconv3, s["h2"], name=f"inproj_dw_conv_{tag}"),
        _mm_tn_slices(dret4, s["h2"], name=f"inproj_dw_ret_{tag}"),
        _mm_tn_slices(datt3, s["h2"], name=f"inproj_dw_att_{tag}")], axis=0)
    dx1, grads["nmix"] = _rmsnorm_bwd(dh2, s["x1"], p["nmix"], dx2, name=f"mixnorm_bwd_{tag}")

    return _ffn_grads(dx1, s["x0"], s["h1"], s["g1"], s["u1"], s["a1"], p, "1", tag, grads, all_grads, hooks)


def _device_step(x, target, layers, final_norm, fwd_hooks=None, bwd_hooks=None):
    t = x.shape[0]
    depth = len(layers)
    aux = dict(rope=_rope_tables(t), ret=_ret_consts())
    ex, done = _carried(fwd_hooks[0] if fwd_hooks else None, "start")
    bias_all, buf = _bias_tiles(jnp.stack([p["rel_bias"] for p in layers]), name="bias_tiles", exchange=ex)
    done(buf)
    saved = []
    for l, p in enumerate(layers):
        p["bias3"] = (bias_all, l)
        x, s = _layer_fwd(x, p, aux, f"l{l}", fwd_hooks[l] if fwd_hooks else None)
        saved.append(s)
    dx, dfinal, loss_row = _loss_fwd_bwd(x, final_norm, target, name="loss_fwd_bwd")
    grads = [dict() for _ in range(depth)]
    for l in reversed(range(depth)):
        dx = _layer_bwd(dx, layers[l], saved[l], aux, f"l{l}", grads[l], grads, bwd_hooks[l] if bwd_hooks else None)
    return loss_row, dx, grads, dfinal


def _sum_slots(bufs, *, name, tr=512):
    n, r, cdim = bufs[0].shape
    depth = len(bufs)
    tr = _tile(r, tr)
    nt = r // tr

    def body(*refs):
        o_ref = refs[depth]
        for k in range(depth):
            @pl.when(pl.program_id(0) == k)
            def _(p_ref=refs[k]):
                acc = p_ref[0].astype(F32)
                for s in range(1, n):
                    acc = acc + p_ref[s].astype(F32)
                o_ref[...] = acc

    def spec(k):
        return pl.BlockSpec((n, tr, cdim), lambda l, i: (0, jnp.where(l == k, i, jnp.where(l < k, 0, nt - 1)), 0))

    return pl.pallas_call(
        body, name=name, grid=(depth, nt), in_specs=[spec(k) for k in range(depth)],
        out_specs=pl.BlockSpec((None, tr, cdim), lambda l, i: (l, i, 0)),
        out_shape=jax.ShapeDtypeStruct((depth, r, cdim), F32), compiler_params=_cparams("arbitrary", "arbitrary"),
    )(*bufs)


def _all_reduce_small(v, *, name):
    r = v.shape[0]

    def body(x_ref, o_ref, slots, send_sems, recv_sems):
        x, y, c, me = _my_position()
        slots[me] = x_ref[...]
        sends = []
        for k in range(1, N_DEV):
            peer, _ = _peer(x, y, c, k)
            cp = pltpu.make_async_remote_copy(src_ref=x_ref, dst_ref=slots.at[me], send_sem=send_sems.at[k - 1],
                                              recv_sem=recv_sems.at[k - 1], device_id=peer, device_id_type=MESH)
            cp.start()
            sends.append(cp)
        for k in range(1, N_DEV):
            peer, peer_id = _peer(x, y, c, k)
            pltpu.make_async_remote_copy(src_ref=x_ref, dst_ref=slots.at[peer_id], send_sem=send_sems.at[k - 1],
                                         recv_sem=recv_sems.at[k - 1], device_id=peer, device_id_type=MESH).wait_recv()
        for cp in sends:
            cp.wait_send()
        acc = slots[0]
        for s in range(1, N_DEV):
            acc = acc + slots[s]
        o_ref[...] = acc

    return pl.pallas_call(
        body, name=name, in_specs=[pl.BlockSpec(memory_space=pltpu.VMEM)],
        out_specs=pl.BlockSpec(memory_space=pltpu.VMEM), out_shape=jax.ShapeDtypeStruct((r, 128), F32),
        scratch_shapes=[pltpu.VMEM((N_DEV, r, 128), F32), pltpu.SemaphoreType.DMA((N_DEV - 1,)),
                        pltpu.SemaphoreType.DMA((N_DEV - 1,))],
    )(v)


def _adam_update(wv, gv, mv, vv, d_ref, mo_ref, vo_ref):
    mn = ADAM_B1 * mv + (1.0 - ADAM_B1) * gv
    vn = ADAM_B2 * vv + (1.0 - ADAM_B2) * (gv * gv)
    m_hat = mn / (1.0 - ADAM_B1 ** ADAM_STEP)
    v_hat = vn / (1.0 - ADAM_B2 ** ADAM_STEP)
    d_ref[...] = -ADAM_LR * (m_hat / (jnp.sqrt(v_hat) + ADAM_EPS) + ADAM_WD * wv)
    mo_ref[...] = mn
    vo_ref[...] = vn


def _adamw_of_partials(w, bufs, m, v, *, name, tr=512):
    depth, r, cdim = w.shape
    n = bufs[0].shape[0]
    tr = _tile(r, tr)
    nt = r // tr

    def body(w_ref, m_ref, v_ref, *refs):
        d_ref, mo_ref, vo_ref, g_ref = refs[depth:]
        for k in range(depth):
            @pl.when(pl.program_id(0) == k)
            def _(p_ref=refs[k]):
                gv = p_ref[0].astype(F32)
                for s in range(1, n):
                    gv = gv + p_ref[s].astype(F32)
                g_ref[...] = gv
                _adam_update(w_ref[...], gv, m_ref[...], v_ref[...], d_ref, mo_ref, vo_ref)

    def part_spec(k):
        return pl.BlockSpec((n, tr, cdim), lambda l, i: (0, jnp.where(l == k, i, jnp.where(l < k, 0, nt - 1)), 0))

    spec = pl.BlockSpec((None, tr, cdim), lambda l, i: (l, i, 0))
    return pl.pallas_call(
        body, name=name, grid=(depth, nt), in_specs=[spec] * 3 + [part_spec(k) for k in range(depth)],
        out_specs=[spec] * 4, out_shape=[jax.ShapeDtypeStruct((depth, r, cdim), F32)] * 4,
        compiler_params=_cparams("arbitrary", "arbitrary"),
    )(w, m, v, *bufs)


def _adamw(w, g, m, v, *, name, tr=256):
    shape = w.shape
    cdim = shape[-1]
    w2, g2, m2, v2 = (a.reshape(-1, cdim) for a in (w, g, m, v))
    r = w2.shape[0]
    tr = _tile(r, tr) if r % 8 == 0 else r

    def body(w_ref, g_ref, m_ref, v_ref, d_ref, mo_ref, vo_ref):
        _adam_update(w_ref[...], g_ref[...], m_ref[...], v_ref[...], d_ref, mo_ref, vo_ref)

    spec = pl.BlockSpec((tr, cdim), lambda i: (i, 0))
    outs = pl.pallas_call(
        body, name=name, grid=(r // tr,), in_specs=[spec] * 4, out_specs=[spec] * 3,
        out_shape=[jax.ShapeDtypeStruct((r, cdim), F32)] * 3, compiler_params=_cparams("parallel"),
    )(w2, g2, m2, v2)
    return tuple(o.reshape(shape) for o in outs)


GROUPS = {"f1": ("wg1", "wu1", "wd1"), "in": ("win",), "mg": ("wb", "wmg", "wout"), "f2": ("wg2", "wu2", "wd2")}
BIG = tuple(nm for members in GROUPS.values() for nm in members)
ROWS_DOMAIN = ("wg1", "wu1", "wd1", "wmg", "wout", "wg2", "wu2", "wd2")


def _to_rows(name, w, d):
    depth = w.shape[0]
    if name in ("wg1", "wu1", "wg2", "wu2", "win"):
        return w.transpose(0, 2, 1)
    if name == "wb":
        return w.transpose(0, 1, 3, 2).reshape(depth, -1, d)
    if name == "wmg":
        return w.reshape(depth, -1, d)
    return w


def _from_rows(name, rows, d):
    depth = rows.shape[0]
    if name in ("wg1", "wu1", "wg2", "wu2", "win"):
        return rows.transpose(0, 2, 1)
    if name == "wb":
        return rows.reshape(depth, 3, -1, BRANCH_W).transpose(0, 1, 3, 2)
    if name == "wmg":
        return rows.reshape(depth, 3, -1, d)
    return rows


def _full_from_gathered(name, g, d):
    if name == "wb":
        return g.reshape(N_DEV, 3, -1, BRANCH_W).transpose(1, 0, 2, 3).reshape(3, d, BRANCH_W)
    if name == "wmg":
        return g.reshape(N_DEV, 3, -1, d).transpose(1, 0, 2, 3).reshape(3, d, d)
    return g.reshape(-1, d)


def _gathered_from_full(name, full, d):
    if name == "wb":
        return full.reshape(3, N_DEV, -1, BRANCH_W).transpose(1, 0, 2, 3).reshape(N_DEV, -1, d)
    if name == "wmg":
        return full.reshape(3, N_DEV, -1, d).transpose(1, 0, 2, 3).reshape(N_DEV, -1, d)
    return full.reshape(N_DEV, -1, d)


def kernel(x, ffn1_norm, ffn1_w_gate, ffn1_w_up, ffn1_w_down, mix_norm, w_in, conv_w, rel_bias, w_branch, w_merge_gate, w_out, ffn2_norm, ffn2_w_gate, ffn2_w_up, ffn2_w_down, final_norm, loss_target, m_ffn1_norm, m_ffn1_w_gate, m_ffn1_w_up, m_ffn1_w_down, m_mix_norm, m_w_in, m_conv_w, m_rel_bias, m_w_branch, m_w_merge_gate, m_w_out, m_ffn2_norm, m_ffn2_w_gate, m_ffn2_w_up, m_ffn2_w_down, m_final_norm, v_ffn1_norm, v_ffn1_w_gate, v_ffn1_w_up, v_ffn1_w_down, v_mix_norm, v_w_in, v_conv_w, v_rel_bias, v_w_branch, v_w_merge_gate, v_w_out, v_ffn2_norm, v_ffn2_w_gate, v_ffn2_w_up, v_ffn2_w_down, v_final_norm):
    names = ["ffn1_norm", "ffn1_w_gate", "ffn1_w_up", "ffn1_w_down", "mix_norm", "w_in", "conv_w", "rel_bias",
             "w_branch", "w_merge_gate", "w_out", "ffn2_norm", "ffn2_w_gate", "ffn2_w_up", "ffn2_w_down", "final_norm"]
    weights = dict(zip(names, (ffn1_norm, ffn1_w_gate, ffn1_w_up, ffn1_w_down, mix_norm, w_in, conv_w, rel_bias,
                               w_branch, w_merge_gate, w_out, ffn2_norm, ffn2_w_gate, ffn2_w_up, ffn2_w_down,
                               final_norm)))
    m_in = dict(zip(names, (m_ffn1_norm, m_ffn1_w_gate, m_ffn1_w_up, m_ffn1_w_down, m_mix_norm, m_w_in, m_conv_w,
                            m_rel_bias, m_w_branch, m_w_merge_gate, m_w_out, m_ffn2_norm, m_ffn2_w_gate,
                            m_ffn2_w_up, m_ffn2_w_down, m_final_norm)))
    v_in = dict(zip(names, (v_ffn1_norm, v_ffn1_w_gate, v_ffn1_w_up, v_ffn1_w_down, v_mix_norm, v_w_in, v_conv_w,
                            v_rel_bias, v_w_branch, v_w_merge_gate, v_w_out, v_ffn2_norm, v_ffn2_w_gate,
                            v_ffn2_w_up, v_ffn2_w_down, v_final_norm)))
    big_of = dict(wg1="ffn1_w_gate", wu1="ffn1_w_up", wd1="ffn1_w_down", win="w_in", wb="w_branch",
                  wmg="w_merge_gate", wout="w_out", wg2="ffn2_w_gate", wu2="ffn2_w_up", wd2="ffn2_w_down")
    depth = ffn1_norm.shape[0]
    d = x.shape[-1]
    xs = x.reshape(-1, d)
    target = loss_target.reshape(-1, d)
    _, _, _, me = _my_position()

    shard_rows = {nm: _to_rows(nm, weights[big_of[nm]], d).astype(BF16) for nm in BIG}

    conv_cols = conv_w.shape[-1]
    conv_full, = _exchange("gather", [conv_w.reshape(depth * 3, conv_cols)], name="gather_conv_w")
    conv_full = conv_full.reshape(N_DEV, depth, 3, conv_cols).transpose(1, 2, 0, 3).reshape(depth, 3, -1)
    layers = [dict(n1=ffn1_norm[l][None], nmix=mix_norm[l][None], n2=ffn2_norm[l][None], conv_w=conv_full[l],
                   rel_bias=rel_bias[l]) for l in range(depth)]

    def gather_hook(l, grp, kind="gather"):
        members = GROUPS[grp]

        def done(bufs):
            layers[l].update({nm: _full_from_gathered(nm, buf, d) for nm, buf in zip(members, bufs)})

        return (lambda _: (kind, [shard_rows[nm] for nm in members], [l] * len(members))), done

    fwd_hooks = []
    for l in range(depth):
        hooks = {"ffn1": gather_hook(l, "in"), "inproj": gather_hook(l, "mg"), "att": gather_hook(l, "f2")}
        if l == 0:
            hooks["start"] = gather_hook(0, "f1", "gather2")
        if l + 1 < depth:
            hooks["ffn2"] = gather_hook(l + 1, "f1")
        fwd_hooks.append(hooks)

    received = {}

    def scatter_hook(l, members):
        def make(all_grads):
            return "scatter", [_gathered_from_full(nm, all_grads[l][nm], d) for nm in members], [None] * len(members)

        return make, (lambda bufs: received.update({(l, nm): buf for nm, buf in zip(members, bufs)}))

    bwd_hooks = []
    for l in range(depth):
        hooks = {"att_bwd": scatter_hook(l, GROUPS["f2"]), "ret_bwd": scatter_hook(l, ("wb", "wout")),
                 "ffn1_bwd": scatter_hook(l, ("win", "wmg"))}
        if l + 1 < depth:
            hooks["ffn2_bwd"] = scatter_hook(l + 1, GROUPS["f1"])
        if l == 0:
            hooks["ffn1_dwg"] = scatter_hook(0, ("wd1",))
            hooks["ffn1_dwu"] = scatter_hook(0, ("wg1",))
        bwd_hooks.append(hooks)

    loss_row, dx, grads, dfinal = _device_step(xs, target, layers, final_norm[None], fwd_hooks, bwd_hooks)
    received[(0, "wu1")], = _exchange("scatter", [_gathered_from_full("wu1", grads[0]["wu1"], d)], name="scatter_l0_wu1")

    partials = {nm: [received[(l, nm)] for l in range(depth)] for nm in BIG}
    grad_w = {big_of[nm]: _from_rows(nm, _sum_slots(partials[nm], name=f"sum_grads_{nm}"), d)
              for nm in BIG if nm not in ROWS_DOMAIN}

    small = {"ffn1_norm": jnp.stack([grads[l]["n1"][0] for l in range(depth)]),
             "mix_norm": jnp.stack([grads[l]["nmix"][0] for l in range(depth)]),
             "ffn2_norm": jnp.stack([grads[l]["n2"][0] for l in range(depth)]),
             "final_norm": dfinal[0],
             "rel_bias": jnp.stack([grads[l]["rel_bias"] for l in range(depth)]),
             "conv_w": jnp.stack([grads[l]["conv_w"] for l in range(depth)]),
             "loss": loss_row[0, :1]}
    order = list(small)
    flat = jnp.concatenate([small[k].reshape(-1) for k in order])
    pad = (-flat.shape[0]) % 1024
    summed = _all_reduce_small(jnp.pad(flat, (0, pad)).reshape(-1, 128), name="reduce_small").reshape(-1)
    pos = 0
    for k in order:
        n = small[k].size
        small[k] = summed[pos:pos + n].reshape(small[k].shape)
        pos += n
    small["conv_w"] = lax.dynamic_slice_in_dim(small["conv_w"], me * conv_cols, conv_cols, axis=2)

    grad_w.update({k: small[k] for k in order if k != "loss"})
    delta, new_m, new_v = {}, {}, {}
    short = {v: k for k, v in big_of.items()}
    for nm in names:
        key = short.get(nm)
        if key in ROWS_DOMAIN:
            outs = _adamw_of_partials(_to_rows(key, weights[nm], d), partials[key], _to_rows(key, m_in[nm], d),
                                      _to_rows(key, v_in[nm], d), name=f"adamw_{nm}")
            delta[nm], new_m[nm], new_v[nm], grad_w[nm] = (_from_rows(key, o, d) for o in outs)
        else:
            delta[nm], new_m[nm], new_v[nm] = _adamw(weights[nm], grad_w[nm], m_in[nm], v_in[nm], name=f"adamw_{nm}")
    return (small["loss"].reshape(()), dx.reshape(x.shape), *[grad_w[n] for n in names], *[delta[n] for n in names],
            *[new_m[n] for n in names], *[new_v[n] for n in names])
```

```python
import functools
import math

import jax
import jax.numpy as jnp
import numpy as np
from jax import lax
from jax.experimental import pallas as pl
from jax.experimental.pallas import tpu as pltpu

F32 = jnp.float32
BF16 = jnp.bfloat16

N_DEV = 8
EPS = 1e-6
CHUNK = 64
BRANCH_W = 512
N_SLICES = 10
H_RET = 4
DK_RET = 128
H_ATT = 8
DH_ATT = 64
N_PREV_CHUNKS = 8
REL_CLIP = 128
N_REL = 2 * REL_CLIP + 1
NEG_INF = -1e30
ROPE_BASE = 10000.0
ATT_QB = 256
ATT_LOOKBACK = N_PREV_CHUNKS * CHUNK
ATT_WIN = ATT_LOOKBACK + ATT_QB
RET_TB = 256
RET_GROUP = 4
TOKEN_TK = 2048
PARTIAL_DTYPE = jnp.bfloat16
CONV_HALO = 16

ADAM_LR = 0.001
ADAM_B1 = 0.9
ADAM_B2 = 0.999
ADAM_EPS = 1e-08
ADAM_WD = 0.01
ADAM_STEP = 10

VMEM_LIMIT_BYTES = 56 * 1024 * 1024
MESH = pl.DeviceIdType.MESH


def _cparams(*sem):
    return pltpu.CompilerParams(dimension_semantics=sem, vmem_limit_bytes=VMEM_LIMIT_BYTES)


def _dot(a, b):
    return lax.dot_general(a, b, (((1,), (0,)), ((), ())), preferred_element_type=F32)


def _dot_nt(a, b):
    return lax.dot_general(a, b, (((1,), (1,)), ((), ())), preferred_element_type=F32)


def _dot_tn(a, b):
    return lax.dot_general(a, b, (((0,), (0,)), ((), ())), preferred_element_type=F32)


def _bf(v):
    return v.astype(BF16)


def _sigmoid(v):
    return 1.0 / (1.0 + jnp.exp(-v))


def _tile(n, want):
    if n <= want:
        return n
    for t in range(want - want % 128, 0, -128):
        if n % t == 0:
            return t
    t = want
    while n % t:
        t //= 2
    return t


def _my_position():
    x, y, c = lax.axis_index("x"), lax.axis_index("y"), lax.axis_index("c")
    return x, y, c, 4 * x + 2 * y + c


def _peer(x, y, c, k):
    px = 1 - x if k & 4 else x
    py = 1 - y if k & 2 else y
    pc = 1 - c if k & 1 else c
    return (px, py, pc), 4 * px + 2 * py + pc


DIRECT2 = (1, 2, 4, 6)
PASSED_ON = (2, 4, 6)
N_PUSHES = N_DEV - 1 + len(PASSED_ON)


def _exchange_sems(n):
    return [pltpu.SemaphoreType.DMA((n, N_PUSHES)), pltpu.SemaphoreType.DMA((n, N_PUSHES)),
            pltpu.SemaphoreType.DMA((n,))]


def _exchange_src(kind, src_ref, layer, block):
    if kind in ("gather", "gather2"):
        return src_ref if layer is None else src_ref.at[layer]
    return src_ref.at[block]


def _passed_on(dst_ref, sems, a, j, pos, incoming):
    x, y, c, _ = pos
    sibling, _ = _peer(x, y, c, 1)
    _, block = _peer(x, y, c, PASSED_ON[j] | 1 if incoming else PASSED_ON[j])
    return pltpu.make_async_remote_copy(src_ref=dst_ref.at[block], dst_ref=dst_ref.at[block],
                                        send_sem=sems[0].at[a, N_DEV - 1 + j], recv_sem=sems[1].at[a, N_DEV - 1 + j],
                                        device_id=sibling, device_id_type=MESH)


def _exchange_copy(kind, src_ref, layer, dst_ref, sems, a, k, pos, incoming):
    x, y, c, me = pos
    peer, peer_id = _peer(x, y, c, k)
    return pltpu.make_async_remote_copy(src_ref=_exchange_src(kind, src_ref, layer, me if incoming else peer_id),
                                        dst_ref=dst_ref.at[peer_id if incoming else me],
                                        send_sem=sems[0].at[a, k - 1], recv_sem=sems[1].at[a, k - 1],
                                        device_id=peer, device_id_type=MESH)


def _exchange_local(kind, src_ref, layer, dst_ref, sems, a, me):
    return pltpu.make_async_copy(_exchange_src(kind, src_ref, layer, me), dst_ref.at[me], sems[2].at[a])


def _exchange_start(kind, layers, src_refs, dst_refs, sems):
    pos = _my_position()
    direct = DIRECT2 if kind == "gather2" else range(1, N_DEV)
    for a, (src_ref, layer, dst_ref) in enumerate(zip(src_refs, layers, dst_refs)):
        _exchange_local(kind, src_ref, layer, dst_ref, sems, a, pos[3]).start()
        for k in direct:
            _exchange_copy(kind, src_ref, layer, dst_ref, sems, a, k, pos, False).start()


def _exchange_wait(kind, layers, src_refs, dst_refs, sems):
    pos = _my_position()
    for a, (src_ref, layer, dst_ref) in enumerate(zip(src_refs, layers, dst_refs)):
        if kind == "gather2":
            for j, k in enumerate(PASSED_ON):
                _exchange_copy(kind, src_ref, layer, dst_ref, sems, a, k, pos, True).wait_recv()
                _passed_on(dst_ref, sems, a, j, pos, False).start()
            _exchange_copy(kind, src_ref, layer, dst_ref, sems, a, 1, pos, True).wait_recv()
            for j in range(len(PASSED_ON)):
                _passed_on(dst_ref, sems, a, j, pos, True).wait_recv()
            for k in DIRECT2:
                _exchange_copy(kind, src_ref, layer, dst_ref, sems, a, k, pos, False).wait_send()
            for j in range(len(PASSED_ON)):
                _passed_on(dst_ref, sems, a, j, pos, False).wait_send()
        else:
            for k in range(1, N_DEV):
                _exchange_copy(kind, src_ref, layer, dst_ref, sems, a, k, pos, True).wait_recv()
            for k in range(1, N_DEV):
                _exchange_copy(kind, src_ref, layer, dst_ref, sems, a, k, pos, False).wait_send()
        _exchange_local(kind, src_ref, layer, dst_ref, sems, a, pos[3]).wait()


def _exchange_shapes(kind, srcs, layers):
    if kind in ("gather", "gather2"):
        return [jax.ShapeDtypeStruct((N_DEV,) + (s.shape if l is None else s.shape[1:]), s.dtype)
                for s, l in zip(srcs, layers)]
    return [jax.ShapeDtypeStruct(s.shape, s.dtype) for s in srcs]


def _exchange(kind, srcs, layers=None, *, name):
    n = len(srcs)
    layers = layers or [None] * n
    hbm = pl.BlockSpec(memory_space=pl.ANY)

    def body(*refs):
        src_refs, dst_refs, sems = refs[:n], refs[n:2 * n], refs[2 * n:]
        _exchange_start(kind, layers, src_refs, dst_refs, sems)
        _exchange_wait(kind, layers, src_refs, dst_refs, sems)

    return pl.pallas_call(body, name=name, in_specs=[hbm] * n, out_specs=[hbm] * n,
                          out_shape=_exchange_shapes(kind, srcs, layers), scratch_shapes=_exchange_sems(n))(*srcs)


def _pcall(body, args, *, name, grid, in_specs, out_specs, out_shape, scratch_shapes, sem, exchange=None):
    if exchange is None:
        outs = pl.pallas_call(body, name=name, grid=grid, in_specs=in_specs, out_specs=out_specs, out_shape=out_shape,
                              scratch_shapes=scratch_shapes, compiler_params=_cparams(*sem))(*args)
        return outs, None
    kind, srcs, layers = exchange
    n_in, n_out, n_scr, n_x = len(in_specs), len(out_specs), len(scratch_shapes), len(srcs)
    hbm = pl.BlockSpec(memory_space=pl.ANY)

    def carrier(*refs):
        ins, refs = refs[:n_in], refs[n_in:]
        src_refs, refs = refs[:n_x], refs[n_x:]
        outs, refs = refs[:n_out], refs[n_out:]
        dst_refs, refs = refs[:n_x], refs[n_x:]
        scr, sems = refs[:n_scr], refs[n_scr:]
        ids = [pl.program_id(a) for a in range(len(grid))]
        first = functools.reduce(jnp.logical_and, [i == 0 for i in ids])
        last = functools.reduce(jnp.logical_and, [i == g - 1 for i, g in zip(ids, grid)])

        @pl.when(first)
        def _():
            _exchange_start(kind, layers, src_refs, dst_refs, sems)

        body(*ins, *outs, *scr)

        @pl.when(last)
        def _():
            _exchange_wait(kind, layers, src_refs, dst_refs, sems)

    outs = pl.pallas_call(
        carrier, name=name, grid=grid, in_specs=list(in_specs) + [hbm] * n_x, out_specs=list(out_specs) + [hbm] * n_x,
        out_shape=list(out_shape) + _exchange_shapes(kind, srcs, layers),
        scratch_shapes=list(scratch_shapes) + _exchange_sems(n_x),
        compiler_params=_cparams(*(["arbitrary"] * len(grid))))(*args, *srcs)
    return outs[:n_out], outs[n_out:]


def _accumulate(acc_ref, part, k, nk, finish):
    if nk == 1:
        finish(part)
        return

    @pl.when(k == 0)
    def _():
        acc_ref[...] = part

    @pl.when(jnp.logical_and(k > 0, k < nk - 1))
    def _():
        acc_ref[...] += part

    @pl.when(k == nk - 1)
    def _():
        finish(acc_ref[...] + part)


def _matmul(a, b, *, dims, grid, a_spec, b_spec, o_spec, out_shape, acc_shape, name, scale=1.0,
            res=None, res_spec=None, exchange=None):
    nk = grid[3]
    has_res = res is not None

    def body(*refs):
        if has_res:
            a_ref, b_ref, r_ref, o_ref = refs[:4]
        else:
            a_ref, b_ref, o_ref = refs[:3]
            r_ref = None

        def finish(acc):
            if scale != 1.0:
                acc = acc * scale
            if r_ref is not None:
                acc = acc + r_ref[...].astype(F32)
            o_ref[...] = acc.astype(o_ref.dtype)

        part = lax.dot_general(_bf(a_ref[...]), _bf(b_ref[...]), (dims, ((), ())), preferred_element_type=F32)
        _accumulate(None if nk == 1 else refs[-1], part, pl.program_id(3), nk, finish)

    in_specs = [a_spec, b_spec] + ([res_spec] if has_res else [])
    args = (a, b) + ((res,) if has_res else ())
    outs, moved = _pcall(
        body, args, name=name, grid=grid, in_specs=in_specs, out_specs=[o_spec], out_shape=[out_shape],
        scratch_shapes=[] if nk == 1 else [pltpu.VMEM(acc_shape, F32)],
        sem=("parallel", "parallel", "parallel", "arbitrary"), exchange=exchange)
    return outs[0] if exchange is None else (outs[0], moved)


NT = ((1,), (1,))
NN = ((1,), (0,))
TN = ((0,), (0,))


def _mm_rows(a, b, *, dims, out_dtype, name, res=None, scale=1.0, tm=1024, tn=1024, tk=1024):
    m, kdim = a.shape
    n = b.shape[1] if dims == NN else b.shape[0]
    tm, tn, tk = _tile(m, tm), _tile(n, tn), _tile(kdim, tk)
    grid = (1, m // tm, n // tn, kdim // tk)
    a_spec = pl.BlockSpec((tm, tk), lambda s, i, j, k: (i, k))
    if dims == NN:
        b_spec = pl.BlockSpec((tk, tn), lambda s, i, j, k: (k, j))
    else:
        b_spec = pl.BlockSpec((tn, tk), lambda s, i, j, k: (j, k))
    o_spec = pl.BlockSpec((tm, tn), lambda s, i, j, k: (i, j))
    return _matmul(a, b, dims=dims, grid=grid, a_spec=a_spec, b_spec=b_spec, o_spec=o_spec,
                   out_shape=jax.ShapeDtypeStruct((m, n), out_dtype), acc_shape=(tm, tn), name=name,
                   res=res, res_spec=o_spec if res is not None else None, scale=scale)


def _mm_to_slices(a, b_t, *, name, tm=2048, exchange=None):
    m, kdim = a.shape
    n = b_t.shape[0]
    tm = _tile(m, tm)
    w = BRANCH_W
    grid = (1, m // tm, n // w, 1)
    return _matmul(a, b_t, dims=NT, grid=grid,
                   a_spec=pl.BlockSpec((tm, kdim), lambda s, i, j, k: (i, 0)),
                   b_spec=pl.BlockSpec((w, kdim), lambda s, i, j, k: (j, 0)),
                   o_spec=pl.BlockSpec((None, tm, w), lambda s, i, j, k: (j, i, 0)),
                   out_shape=jax.ShapeDtypeStruct((n // w, m, w), BF16), acc_shape=(tm, w), name=name,
                   exchange=exchange)


def _inproj_dh(parts, w_t, res, *, name, tm=512):
    t, n = res.shape
    w = BRANCH_W
    tm = _tile(t, tm)
    counts = [p.shape[0] for p in parts]
    assert sum(counts) * w == w_t.shape[0]

    def body(*refs):
        w_ref, r_ref, o_ref = refs[len(parts):]
        cols = jnp.concatenate([p_ref[s] for p_ref, cnt in zip(refs, counts) for s in range(cnt)], axis=1)
        o_ref[...] = r_ref[...] + _dot(cols, w_ref[...])

    row = pl.BlockSpec((tm, n), lambda i: (i, 0))
    return pl.pallas_call(
        body, name=name, grid=(t // tm,),
        in_specs=[pl.BlockSpec((cnt, tm, w), lambda i: (0, i, 0)) for cnt in counts]
        + [pl.BlockSpec(w_t.shape, lambda i: (0, 0)), row],
        out_specs=row, out_shape=jax.ShapeDtypeStruct((t, n), F32), compiler_params=_cparams("parallel"),
    )(*parts, w_t, res)


def _mm_tn(a, b, *, name, tm=1408, tn=1024, tk=TOKEN_TK, exchange=None):
    t, m = a.shape
    n = b.shape[1]
    tm, tn, tk = _tile(m, tm), _tile(n, tn), _tile(t, tk)
    grid = (1, m // tm, n // tn, t // tk)
    return _matmul(a, b, dims=TN, grid=grid,
                   a_spec=pl.BlockSpec((tk, tm), lambda s, i, j, k: (k, i)),
                   b_spec=pl.BlockSpec((tk, tn), lambda s, i, j, k: (k, j)),
                   o_spec=pl.BlockSpec((tm, tn), lambda s, i, j, k: (i, j)),
                   out_shape=jax.ShapeDtypeStruct((m, n), PARTIAL_DTYPE), acc_shape=(tm, tn), name=name,
                   exchange=exchange)


def _mm_tn_slices(a3, b, *, name, tn=1024, tk=TOKEN_TK):
    s_n, t, w = a3.shape
    n = b.shape[1]
    tn, tk = _tile(n, tn), _tile(t, tk)
    grid = (1, s_n, n // tn, t // tk)
    return _matmul(a3, b, dims=TN, grid=grid,
                   a_spec=pl.BlockSpec((None, tk, w), lambda s, i, j, k: (i, k, 0)),
                   b_spec=pl.BlockSpec((tk, tn), lambda s, i, j, k: (k, j)),
                   o_spec=pl.BlockSpec((w, tn), lambda s, i, j, k: (i, j)),
                   out_shape=jax.ShapeDtypeStruct((s_n * w, n), PARTIAL_DTYPE), acc_shape=(w, tn), name=name)


def _mm_tn_batch(a, b3, *, name, tm=1024, tn=1024, tk=TOKEN_TK):
    s_n, t, n = b3.shape
    m = a.shape[-1]
    tm, tn, tk = _tile(m, tm), _tile(n, tn), _tile(t, tk)
    grid = (s_n, m // tm, n // tn, t // tk)
    return _matmul(a, b3, dims=TN, grid=grid, a_spec=pl.BlockSpec((tk, tm), lambda s, i, j, k: (k, i)),
                   b_spec=pl.BlockSpec((None, tk, tn), lambda s, i, j, k: (s, k, j)),
                   o_spec=pl.BlockSpec((None, tm, tn), lambda s, i, j, k: (s, i, j)),
                   out_shape=jax.ShapeDtypeStruct((s_n, m, n), PARTIAL_DTYPE), acc_shape=(tm, tn), name=name)


def _norm_parts(xf):
    r = lax.rsqrt(jnp.mean(xf * xf, axis=-1, keepdims=True) + EPS)
    return xf * r, r


def _norm_bwd(dh, xhat, r, w):
    dxhat = dh * w
    dx = r * (dxhat - xhat * jnp.mean(dxhat * xhat, axis=-1, keepdims=True))
    return dx, jnp.sum(dh * xhat, axis=0, keepdims=True)


def _rmsnorm_fwd(x, w, *, name, tm=1024):
    t, d = x.shape
    tm = _tile(t, tm)

    def body(x_ref, w_ref, h_ref):
        xhat, _ = _norm_parts(x_ref[...])
        h_ref[...] = _bf(xhat * w_ref[...])

    return pl.pallas_call(
        body, name=name, grid=(t // tm,),
        in_specs=[pl.BlockSpec((tm, d), lambda i: (i, 0)), pl.BlockSpec((1, d), lambda i: (0, 0))],
        out_specs=pl.BlockSpec((tm, d), lambda i: (i, 0)),
        out_shape=jax.ShapeDtypeStruct((t, d), BF16), compiler_params=_cparams("parallel"),
    )(x, w)


def _rmsnorm_bwd(dh, x, w, dres, *, name, tm=1024):
    t, d = x.shape
    tm = _tile(t, tm)

    def body(dh_ref, x_ref, w_ref, dres_ref, dx_ref, dw_ref):
        xhat, r = _norm_parts(x_ref[...])
        dx, dw = _norm_bwd(dh_ref[...], xhat, r, w_ref[...])
        dx_ref[...] = dres_ref[...] + dx

        @pl.when(pl.program_id(0) == 0)
        def _():
            dw_ref[...] = dw

        @pl.when(pl.program_id(0) > 0)
        def _():
            dw_ref[...] += dw

    row = pl.BlockSpec((tm, d), lambda i: (i, 0))
    vec = pl.BlockSpec((1, d), lambda i: (0, 0))
    return pl.pallas_call(
        body, name=name, grid=(t // tm,), in_specs=[row, row, vec, row], out_specs=[row, vec],
        out_shape=[jax.ShapeDtypeStruct((t, d), F32), jax.ShapeDtypeStruct((1, d), F32)],
        compiler_params=_cparams("arbitrary"),
    )(dh, x, w, dres)


def _loss_fwd_bwd(x, w, target, *, name, tm=1024):
    t, d = x.shape
    tm = _tile(t, tm)

    def body(x_ref, w_ref, t_ref, dx_ref, dw_ref, loss_ref):
        xhat, r = _norm_parts(x_ref[...])
        wv = w_ref[...]
        err = xhat * wv - t_ref[...]
        dx, dw = _norm_bwd(err * (1.0 / d), xhat, r, wv)
        dx_ref[...] = dx
        part = jnp.full((1, 128), 0.5 / d, F32) * jnp.sum(err * err)

        @pl.when(pl.program_id(0) == 0)
        def _():
            dw_ref[...] = dw
            loss_ref[...] = part

        @pl.when(pl.program_id(0) > 0)
        def _():
            dw_ref[...] += dw
            loss_ref[...] += part

    row = pl.BlockSpec((tm, d), lambda i: (i, 0))
    vec = pl.BlockSpec((1, d), lambda i: (0, 0))
    return pl.pallas_call(
        body, name=name, grid=(t // tm,), in_specs=[row, vec, row],
        out_specs=[row, vec, pl.BlockSpec((1, 128), lambda i: (0, 0))],
        out_shape=[jax.ShapeDtypeStruct((t, d), F32), jax.ShapeDtypeStruct((1, d), F32),
                   jax.ShapeDtypeStruct((1, 128), F32)],
        compiler_params=_cparams("arbitrary"),
    )(x, w, target)


def _ffn_tiles(t, f):
    tf = f
    for cand in (1408, 1024, 512, 256, 128):
        if f % cand == 0:
            tf = cand
            break
    return _tile(t, 512), tf


def _ffn_fwd(x, nw, wg_t, wu_t, wd, *, name, exchange=None):
    t, d = x.shape
    f = wd.shape[0]
    tm, tf = _ffn_tiles(t, f)
    nf = f // tf

    def body(x_ref, nw_ref, wg_ref, wu_ref, wd_ref, xo_ref, h_ref, g_ref, u_ref, a_ref, hs_ref, acc_ref):
        j = pl.program_id(1)

        @pl.when(j == 0)
        def _():
            xhat, _ = _norm_parts(x_ref[...])
            hb = _bf(xhat * nw_ref[...])
            hs_ref[...] = hb
            h_ref[...] = hb

        hb = hs_ref[...]
        g = _dot_nt(hb, wg_ref[...])
        u = _dot_nt(hb, wu_ref[...])
        a = _bf(g * _sigmoid(g) * u)
        g_ref[...] = _bf(g)
        u_ref[...] = _bf(u)
        a_ref[...] = a
        part = _dot(a, wd_ref[...])

        def finish(total):
            xo_ref[...] = x_ref[...] + 0.5 * total

        _accumulate(acc_ref, part, j, nf, finish)

    row = pl.BlockSpec((tm, d), lambda i, j: (i, 0))
    wspec = pl.BlockSpec((tf, d), lambda i, j: (j, 0))
    hid = pl.BlockSpec((tm, tf), lambda i, j: (i, j))
    return _pcall(
        body, (x, nw, wg_t, wu_t, wd), name=name, grid=(t // tm, nf),
        in_specs=[row, pl.BlockSpec((1, d), lambda i, j: (0, 0)), wspec, wspec, wspec],
        out_specs=[row, row, hid, hid, hid],
        out_shape=[jax.ShapeDtypeStruct((t, d), F32), jax.ShapeDtypeStruct((t, d), BF16)]
        + [jax.ShapeDtypeStruct((t, f), BF16)] * 3,
        scratch_shapes=[pltpu.VMEM((tm, d), BF16), pltpu.VMEM((tm, d), F32)],
        sem=("parallel", "arbitrary"), exchange=exchange)


def _ffn_bwd(dxo, x, nw, g, u, wg_t, wu_t, wd, *, name, exchange=None):
    t, d = x.shape
    f = wd.shape[0]
    tm, tf = _ffn_tiles(t, f)
    nf = f // tf

    def body(dxo_ref, x_ref, nw_ref, g_ref, u_ref, wg_ref, wu_ref, wd_ref,
             dx_ref, dy_ref, dg_ref, du_ref, dnw_ref, dys_ref, acc_ref):
        i, j = pl.program_id(0), pl.program_id(1)

        @pl.when(j == 0)
        def _():
            dyb = _bf(0.5 * dxo_ref[...])
            dys_ref[...] = dyb
            dy_ref[...] = dyb

        da = _dot_nt(dys_ref[...], wd_ref[...])
        gv = g_ref[...].astype(F32)
        uv = u_ref[...].astype(F32)
        s = _sigmoid(gv)
        dg = _bf(da * uv * (s * (1.0 + gv * (1.0 - s))))
        du = _bf(da * (gv * s))
        dg_ref[...] = dg
        du_ref[...] = du
        part = _dot(dg, wg_ref[...]) + _dot(du, wu_ref[...])

        def finish(dh):
            xhat, r = _norm_parts(x_ref[...])
            dx, dw = _norm_bwd(dh, xhat, r, nw_ref[...])
            dx_ref[...] = dxo_ref[...] + dx

            @pl.when(i == 0)
            def _():
                dnw_ref[...] = dw

            @pl.when(i > 0)
            def _():
                dnw_ref[...] += dw

        _accumulate(acc_ref, part, j, nf, finish)

    row = pl.BlockSpec((tm, d), lambda i, j: (i, 0))
    vec = pl.BlockSpec((1, d), lambda i, j: (0, 0))
    wspec = pl.BlockSpec((tf, d), lambda i, j: (j, 0))
    hid = pl.BlockSpec((tm, tf), lambda i, j: (i, j))
    return _pcall(
        body, (dxo, x, nw, g, u, wg_t, wu_t, wd), name=name, grid=(t // tm, nf),
        in_specs=[row, row, vec, hid, hid, wspec, wspec, wspec],
        out_specs=[row, row, hid, hid, vec],
        out_shape=[jax.ShapeDtypeStruct((t, d), F32), jax.ShapeDtypeStruct((t, d), BF16),
                   jax.ShapeDtypeStruct((t, f), BF16), jax.ShapeDtypeStruct((t, f), BF16),
                   jax.ShapeDtypeStruct((1, d), F32)],
        scratch_shapes=[pltpu.VMEM((tm, d), BF16), pltpu.VMEM((tm, d), F32)],
        sem=("arbitrary", "arbitrary"), exchange=exchange)


def _shift_down(prev, cur, n):
    ext = jnp.concatenate([prev, cur], axis=0)
    return pltpu.roll(ext, n, axis=0)[prev.shape[0]:]


def _shift_up(cur, nxt, n):
    ext = jnp.concatenate([cur, nxt], axis=0)
    return pltpu.roll(ext, ext.shape[0] - n, axis=0)[:cur.shape[0]]


def _conv_specs(t, tb):
    hb = tb // CONV_HALO
    last = t // CONV_HALO - 1

    def tile(s):
        return pl.BlockSpec((None, tb, 128), lambda c, i: (s, i, c))

    def prev(s):
        return pl.BlockSpec((None, CONV_HALO, 128), lambda c, i: (s, jnp.maximum(i * hb - 1, 0), c))

    def nxt(s):
        return pl.BlockSpec((None, CONV_HALO, 128), lambda c, i: (s, jnp.minimum((i + 1) * hb, last), c))

    return tile, prev, nxt


def _conv_fwd(cols3, conv_w, *, name, tb=1024):
    _, t, bw = cols3.shape
    tb = _tile(t, tb)
    tile, prev, _ = _conv_specs(t, tb)

    def body(u_ref, b_ref, c_ref, up_ref, cp_ref, w_ref, y_ref):
        first = pl.program_id(1) == 0
        z = c_ref[...].astype(F32) * u_ref[...].astype(F32)
        zp = jnp.where(first, 0.0, cp_ref[...].astype(F32) * up_ref[...].astype(F32))
        conv = w_ref[0:1, :] * _shift_down(zp, z, 2) + w_ref[1:2, :] * _shift_down(zp, z, 1) + w_ref[2:3, :] * z
        y_ref[...] = _bf(b_ref[...].astype(F32) * conv)

    return pl.pallas_call(
        body, name=name, grid=(bw // 128, t // tb),
        in_specs=[tile(0), tile(1), tile(2), prev(0), prev(2), pl.BlockSpec((3, 128), lambda c, i: (0, c))],
        out_specs=pl.BlockSpec((tb, 128), lambda c, i: (i, c)),
        out_shape=jax.ShapeDtypeStruct((t, bw), BF16), compiler_params=_cparams("parallel", "parallel"),
    )(cols3, cols3, cols3, cols3, cols3, conv_w)


def _conv_bwd(cols3, conv_w, dy, *, name, tb=1024):
    _, t, bw = cols3.shape
    tb = _tile(t, tb)
    nt = t // tb
    tile, prev, nxt = _conv_specs(t, tb)
    hb = tb // CONV_HALO
    last = t // CONV_HALO - 1

    def body(u_ref, b_ref, c_ref, up_ref, cp_ref, bn_ref, dy_ref, dyn_ref, w_ref, d3_ref, dw_ref):
        i = pl.program_id(1)
        uv, bv, cv = u_ref[...].astype(F32), b_ref[...].astype(F32), c_ref[...].astype(F32)
        dyv = dy_ref[...].astype(F32)
        z = cv * uv
        zp = jnp.where(i == 0, 0.0, cp_ref[...].astype(F32) * up_ref[...].astype(F32))
        z1, z2 = _shift_down(zp, z, 1), _shift_down(zp, z, 2)
        w0, w1, w2 = w_ref[0:1, :], w_ref[1:2, :], w_ref[2:3, :]
        conv = w0 * z2 + w1 * z1 + w2 * z
        dconv = dyv * bv
        dconv_n = jnp.where(i == nt - 1, 0.0, dyn_ref[...].astype(F32) * bn_ref[...].astype(F32))
        dz = w2 * dconv + w1 * _shift_up(dconv, dconv_n, 1) + w0 * _shift_up(dconv, dconv_n, 2)
        d3_ref[0] = _bf(dz * cv)
        d3_ref[1] = _bf(dyv * conv)
        d3_ref[2] = _bf(dz * uv)
        dws = [jnp.sum(dconv * zz, axis=0, keepdims=True) for zz in (z2, z1, z)]

        @pl.when(i == 0)
        def _():
            for j in range(3):
                dw_ref[j:j + 1, :] = dws[j]

        @pl.when(i > 0)
        def _():
            for j in range(3):
                dw_ref[j:j + 1, :] += dws[j]

    dy_tile = pl.BlockSpec((None, tb, 128), lambda c, i: (0, i, c))
    dy_next = pl.BlockSpec((None, CONV_HALO, 128), lambda c, i: (0, jnp.minimum((i + 1) * hb, last), c))
    wspec = pl.BlockSpec((3, 128), lambda c, i: (0, c))
    return pl.pallas_call(
        body, name=name, grid=(bw // 128, nt),
        in_specs=[tile(0), tile(1), tile(2), prev(0), prev(2), nxt(1), dy_tile, dy_next, wspec],
        out_specs=[pl.BlockSpec((3, tb, 128), lambda c, i: (0, i, c)), wspec],
        out_shape=[jax.ShapeDtypeStruct((3, t, bw), BF16), jax.ShapeDtypeStruct((3, bw), F32)],
        compiler_params=_cparams("parallel", "arbitrary"),
    )(cols3, cols3, cols3, cols3, cols3, cols3, dy, dy, conv_w)


def _ret_consts():
    log_gamma = jnp.log1p(-jnp.exp2(-5.0 - jnp.arange(H_RET, dtype=F32)))
    pos = jnp.arange(CHUNK, dtype=F32)
    d_intra = jnp.exp(log_gamma[:, None, None] * jnp.abs(pos[:, None] - pos[None, :]))
    q_decay = jnp.exp(log_gamma[:, None] * (pos + 1.0))
    k_decay = jnp.exp(log_gamma[:, None] * (CHUNK - 1.0 - pos))
    chunk_decay = jnp.exp(log_gamma * CHUNK)
    wide = (H_RET, CHUNK, DK_RET)
    return (d_intra, jnp.broadcast_to(q_decay[:, :, None], wide), jnp.broadcast_to(k_decay[:, :, None], wide),
            jnp.broadcast_to(chunk_decay[:, None, None], (H_RET, 1, DK_RET)))


def _rope_tables(t):
    inv_freq = np.float32(ROPE_BASE) ** (-np.linspace(0.0, 1.0, DK_RET // 2, dtype=np.float32))
    ang = np.arange(t, dtype=np.float32)[:, None] * inv_freq[None, :]
    cos, sin = np.cos(ang), np.sin(ang)
    return jnp.asarray(np.concatenate([cos, cos], axis=1)), jnp.asarray(np.concatenate([-sin, sin], axis=1))


def _rope(v, cc, ss):
    return v * cc + pltpu.roll(v, DK_RET // 2, axis=1) * ss


def _ret_in_specs(tb, blk):
    def col(s):
        return pl.BlockSpec((None, tb, H_RET * DK_RET), lambda i: (s, blk(i), 0))

    tab = pl.BlockSpec((tb, DK_RET), lambda i: (blk(i), 0))
    return ([col(3), col(4), col(5), col(6), tab, tab,
             pl.BlockSpec((H_RET, CHUNK, CHUNK), lambda i: (0, 0, 0)),
             pl.BlockSpec((H_RET, CHUNK, DK_RET), lambda i: (0, 0, 0)),
             pl.BlockSpec((H_RET, CHUNK, DK_RET), lambda i: (0, 0, 0)),
             pl.BlockSpec((H_RET, 1, DK_RET), lambda i: (0, 0, 0))])


def _ret_fwd(cols3, tables, consts, *, name):
    _, t, bw = cols3.shape
    tb = _tile(t, RET_TB)
    ncb = tb // CHUNK
    scale = DK_RET ** -0.5

    def body(q_ref, k_ref, v_ref, g_ref, cc_ref, ss_ref, di_ref, qd_ref, kd_ref, cd_ref, y_ref, st_ref, state):
        @pl.when(pl.program_id(0) == 0)
        def _():
            state[...] = jnp.zeros_like(state)

        heads = range(H_RET)
        units = [(c, h) for c in range(ncb) for h in heads]
        every = range(len(units))
        rows = [pl.ds(c * CHUNK, CHUNK) for c, _ in units]
        lanes = [pl.ds(h * DK_RET, DK_RET) for _, h in units]
        hd = [h for _, h in units]
        cc, ss = [cc_ref[rw, :] for rw in rows], [ss_ref[rw, :] for rw in rows]
        qs = [_rope(q_ref[rows[u], lanes[u]].astype(F32), cc[u], ss[u]) * scale for u in every]
        kr = [_rope(k_ref[rows[u], lanes[u]].astype(F32), cc[u], ss[u]) for u in every]
        vb = [v_ref[rows[u], lanes[u]] for u in every]
        gv = [g_ref[rows[u], lanes[u]].astype(F32) for u in every]
        innerb = [_bf(_dot_nt(_bf(qs[u]), _bf(kr[u])) * di_ref[hd[u]]) for u in every]
        update = [_dot_tn(_bf(kr[u] * kd_ref[hd[u]]), vb[u]) for u in every]
        carried = [state[h] for h in heads]
        s_in = []
        for u, (c, h) in enumerate(units):
            s_in.append(carried[h])
            st_ref[h, c] = carried[h]
            carried[h] = carried[h] * cd_ref[h] + update[u]
        for h in heads:
            state[h] = carried[h]
        o = [_dot(innerb[u], vb[u]) + _dot(_bf(qs[u] * qd_ref[hd[u]]), _bf(s_in[u])) for u in every]
        for u in every:
            on = o[u] * lax.rsqrt(jnp.mean(o[u] * o[u], axis=-1, keepdims=True) + EPS)
            y_ref[rows[u], lanes[u]] = _bf(gv[u] * _sigmoid(gv[u]) * on)

    return pl.pallas_call(
        body, name=name, grid=(t // tb,),
        in_specs=_ret_in_specs(tb, lambda i: i),
        out_specs=[pl.BlockSpec((tb, bw), lambda i: (i, 0)),
                   pl.BlockSpec((H_RET, ncb, DK_RET, DK_RET), lambda i: (0, i, 0, 0))],
        out_shape=[jax.ShapeDtypeStruct((t, bw), BF16),
                   jax.ShapeDtypeStruct((H_RET, t // CHUNK, DK_RET, DK_RET), F32)],
        scratch_shapes=[pltpu.VMEM((H_RET, DK_RET, DK_RET), F32)],
        compiler_params=_cparams("arbitrary"),
    )(cols3, cols3, cols3, cols3, tables[0], tables[1], *consts)


def _ret_bwd(cols3, tables, consts, states, dy, *, name, exchange=None):
    _, t, bw = cols3.shape
    tb = _tile(t, RET_TB)
    ncb = tb // CHUNK
    nb = t // tb
    scale = DK_RET ** -0.5

    def body(q_ref, k_ref, v_ref, g_ref, cc_ref, ss_ref, di_ref, qd_ref, kd_ref, cd_ref, st_ref, dy_ref,
             d4_ref, dstate):
        @pl.when(pl.program_id(0) == 0)
        def _():
            dstate[...] = jnp.zeros_like(dstate)

        heads = range(H_RET)
        lanes_of = [pl.ds(h * DK_RET, DK_RET) for h in heads]
        di, qd, kd = [di_ref[h] for h in heads], [qd_ref[h] for h in heads], [kd_ref[h] for h in heads]
        carried = [dstate[h] for h in heads]
        group = RET_GROUP if ncb % RET_GROUP == 0 else 1
        for c0 in reversed(range(0, ncb, group)):
            units = [(c, h) for c in reversed(range(c0, c0 + group)) for h in heads]
            rows = [pl.ds(c * CHUNK, CHUNK) for c, _ in units]
            cc, ss = [cc_ref[rw, :] for rw in rows], [ss_ref[rw, :] for rw in rows]
            every = range(len(units))
            hd = [h for _, h in units]
            vb = [v_ref[rows[u], lanes_of[hd[u]]] for u in every]
            gv = [g_ref[rows[u], lanes_of[hd[u]]].astype(F32) for u in every]
            dyv = [dy_ref[rows[u], lanes_of[hd[u]]].astype(F32) for u in every]
            s_in = [_bf(st_ref[h, c]) for c, h in units]
            qs = [_rope(q_ref[rows[u], lanes_of[hd[u]]].astype(F32), cc[u], ss[u]) * scale for u in every]
            kr = [_rope(k_ref[rows[u], lanes_of[hd[u]]].astype(F32), cc[u], ss[u]) for u in every]
            qsb, krb = [_bf(v) for v in qs], [_bf(v) for v in kr]
            qdb = [_bf(qs[u] * qd[hd[u]]) for u in every]
            kdb = [_bf(kr[u] * kd[hd[u]]) for u in every]
            innerb = [_bf(_dot_nt(qsb[u], krb[u]) * di[hd[u]]) for u in every]
            o = [_dot(innerb[u], vb[u]) + _dot(qdb[u], s_in[u]) for u in every]
            r = [lax.rsqrt(jnp.mean(v * v, axis=-1, keepdims=True) + EPS) for v in o]
            on = [o[u] * r[u] for u in every]
            sg = [_sigmoid(v) for v in gv]
            dgv = [_bf(dyv[u] * on[u] * (sg[u] * (1.0 + gv[u] * (1.0 - sg[u])))) for u in every]
            don = [dyv[u] * (gv[u] * sg[u]) for u in every]
            dob = [_bf(r[u] * (don[u] - on[u] * jnp.mean(don[u] * on[u], axis=-1, keepdims=True))) for u in every]
            dinner = [_bf(_dot_nt(dob[u], vb[u]) * di[hd[u]]) for u in every]
            dqs = [_dot(dinner[u], krb[u]) + _dot_nt(dob[u], s_in[u]) * qd[hd[u]] for u in every]
            dkr = [_dot_tn(dinner[u], qsb[u]) for u in every]
            dv = [_dot_tn(innerb[u], dob[u]) for u in every]
            dnew = [_dot_tn(qdb[u], dob[u]) for u in every]
            for u in every:
                h = hd[u]
                dstb = _bf(carried[h])
                dv[u] = dv[u] + _dot(kdb[u], dstb)
                dkr[u] = dkr[u] + _dot_nt(vb[u], dstb) * kd[h]
                carried[h] = carried[h] * cd_ref[h] + dnew[u]
            for u in every:
                ln = lanes_of[hd[u]]
                d4_ref[0, rows[u], ln] = _bf(_rope_bwd(dqs[u] * scale, cc[u], ss[u]))
                d4_ref[1, rows[u], ln] = _bf(_rope_bwd(dkr[u], cc[u], ss[u]))
                d4_ref[2, rows[u], ln] = _bf(dv[u])
                d4_ref[3, rows[u], ln] = dgv[u]
        for h in heads:
            dstate[h] = carried[h]

    rev = lambda i: nb - 1 - i
    outs, moved = _pcall(
        body, (cols3, cols3, cols3, cols3, tables[0], tables[1], *consts, states, dy), name=name, grid=(nb,),
        in_specs=_ret_in_specs(tb, rev)
        + [pl.BlockSpec((H_RET, ncb, DK_RET, DK_RET), lambda i: (0, rev(i), 0, 0)),
           pl.BlockSpec((None, tb, bw), lambda i: (1, rev(i), 0))],
        out_specs=[pl.BlockSpec((4, tb, bw), lambda i: (0, rev(i), 0))],
        out_shape=[jax.ShapeDtypeStruct((4, t, bw), BF16)],
        scratch_shapes=[pltpu.VMEM((H_RET, DK_RET, DK_RET), F32)],
        sem=("arbitrary",), exchange=exchange)
    return outs[0], moved


def _rope_bwd(dv, cc, ss):
    return dv * cc + pltpu.roll(dv * ss, DK_RET // 2, axis=1)


def _att_window(i):
    return pl.multiple_of(jnp.maximum(i - ATT_LOOKBACK // ATT_QB, 0) * ATT_QB, ATT_QB)


def _att_mask(v):
    qchunk = (v * ATT_QB + lax.broadcasted_iota(jnp.int32, (ATT_QB, ATT_WIN), 0)) // CHUNK
    kchunk = lax.broadcasted_iota(jnp.int32, (ATT_QB, ATT_WIN), 1) // CHUNK
    return (kchunk <= qchunk) & (kchunk >= qchunk - N_PREV_CHUNKS)


def _bias_spec(layer, nvar):
    return pl.BlockSpec((None, None, 2, ATT_QB, ATT_WIN), lambda hp, i: (layer, jnp.minimum(i, nvar), hp, 0, 0))


def _att_fwd(cols3, bias3, *, name, exchange=None):
    _, t, bw = cols3.shape
    assert t % ATT_QB == 0 and t >= ATT_WIN
    scale = DH_ATT ** -0.5
    nvar = ATT_LOOKBACK // ATT_QB

    def body(q_ref, k_ref, v_ref, b_ref, y_ref, lse_ref):
        i = pl.program_id(1)
        ws = _att_window(i)
        q = q_ref[...].astype(F32)
        kw = k_ref[pl.ds(ws, ATT_WIN), :]
        vw = v_ref[pl.ds(ws, ATT_WIN), :]
        lane_head = lax.broadcasted_iota(jnp.int32, (ATT_QB, 128), 1) // DH_ATT
        out = jnp.zeros((ATT_QB, 128), F32)
        lse = jnp.zeros((ATT_QB, 128), F32)
        for hh in range(2):
            mine = lane_head == hh
            s = _dot_nt(_bf(jnp.where(mine, q, 0.0)), kw) * scale + b_ref[hh]
            mx = jnp.max(s, axis=-1, keepdims=True)
            p = jnp.exp(s - mx)
            l = jnp.sum(p, axis=-1, keepdims=True)
            out = jnp.where(mine, _dot(_bf(p), vw) / l, out)
            lse = jnp.where(mine, mx + jnp.log(l), lse)
        y_ref[...] = _bf(out)
        lse_ref[...] = lse

    kv = lambda s: pl.BlockSpec((None, t, 128), lambda hp, i: (s, 0, hp))
    return _pcall(
        body, (cols3, cols3, cols3, bias3[0]), name=name, grid=(H_ATT // 2, t // ATT_QB),
        in_specs=[pl.BlockSpec((None, ATT_QB, 128), lambda hp, i: (7, i, hp)), kv(8), kv(9),
                  _bias_spec(bias3[1], nvar)],
        out_specs=[pl.BlockSpec((ATT_QB, 128), lambda hp, i: (i, hp)),
                   pl.BlockSpec((None, ATT_QB, 128), lambda hp, i: (hp, i, 0))],
        out_shape=[jax.ShapeDtypeStruct((t, bw), BF16), jax.ShapeDtypeStruct((H_ATT // 2, t, 128), F32)],
        scratch_shapes=[], sem=("parallel", "arbitrary"), exchange=exchange)


def _att_bwd(cols3, bias3, y, lse, dy, *, name, exchange=None):
    _, t, bw = cols3.shape
    nq = t // ATT_QB
    scale = DH_ATT ** -0.5
    nvar = ATT_LOOKBACK // ATT_QB

    def body(q_ref, k_ref, v_ref, b_ref, y_ref, lse_ref, dy_ref, d3_ref, db_ref, dk_acc, dv_acc):
        i = pl.program_id(1)

        @pl.when(i == 0)
        def _():
            dk_acc[...] = jnp.zeros_like(dk_acc)
            dv_acc[...] = jnp.zeros_like(dv_acc)

        ws = _att_window(i)
        q = q_ref[...].astype(F32)
        kw = k_ref[pl.ds(ws, ATT_WIN), :]
        vw = v_ref[pl.ds(ws, ATT_WIN), :]
        do = dy_ref[...].astype(F32)
        dof = do * y_ref[...].astype(F32)
        lsev = lse_ref[...]
        lane_head = lax.broadcasted_iota(jnp.int32, (ATT_QB, 128), 1) // DH_ATT
        dq = jnp.zeros((ATT_QB, 128), F32)
        dk = jnp.zeros((ATT_WIN, 128), F32)
        dv = jnp.zeros((ATT_WIN, 128), F32)
        first = i <= nvar
        for hh in range(2):
            mine = lane_head == hh
            qh = _bf(jnp.where(mine, q, 0.0))
            doh = _bf(jnp.where(mine, do, 0.0))
            s = _dot_nt(qh, kw) * scale + b_ref[hh]
            lse_h = jnp.max(jnp.where(mine, lsev, NEG_INF), axis=-1, keepdims=True)
            p = jnp.exp(s - lse_h)
            delta = jnp.sum(jnp.where(mine, dof, 0.0), axis=-1, keepdims=True)
            ds = p * (_dot_nt(doh, vw) - delta)

            @pl.when(first)
            def _():
                db_ref[hh] = ds

            @pl.when(jnp.logical_not(first))
            def _():
                db_ref[hh] += ds

            dsb = _bf(ds * scale)
            dq = jnp.where(mine, _dot(dsb, kw), dq)
            dk = dk + _dot_tn(dsb, qh)
            dv = dv + _dot_tn(_bf(p), doh)
        d3_ref[0, pl.ds(pl.multiple_of(i * ATT_QB, ATT_QB), ATT_QB), :] = _bf(dq)
        dk_acc[pl.ds(ws, ATT_WIN), :] += dk
        dv_acc[pl.ds(ws, ATT_WIN), :] += dv

        @pl.when(i == nq - 1)
        def _():
            d3_ref[1] = _bf(dk_acc[...])
            d3_ref[2] = _bf(dv_acc[...])

    kv = lambda s: pl.BlockSpec((None, t, 128), lambda hp, i: (s, 0, hp))
    qrow = pl.BlockSpec((ATT_QB, 128), lambda hp, i: (i, hp))
    btile = pl.BlockSpec((None, 2, ATT_QB, ATT_WIN), lambda hp, i: (jnp.minimum(i, nvar), hp, 0, 0))
    return _pcall(
        body, (cols3, cols3, cols3, bias3[0], y, lse, dy), name=name, grid=(H_ATT // 2, nq),
        in_specs=[pl.BlockSpec((None, ATT_QB, 128), lambda hp, i: (7, i, hp)), kv(8), kv(9),
                  _bias_spec(bias3[1], nvar), qrow,
                  pl.BlockSpec((None, ATT_QB, 128), lambda hp, i: (hp, i, 0)),
                  pl.BlockSpec((None, ATT_QB, 128), lambda hp, i: (2, i, hp))],
        out_specs=[pl.BlockSpec((3, t, 128), lambda hp, i: (0, 0, hp)), btile],
        out_shape=[jax.ShapeDtypeStruct((3, t, bw), BF16),
                   jax.ShapeDtypeStruct((nvar + 1, H_ATT, ATT_QB, ATT_WIN), F32)],
        scratch_shapes=[pltpu.VMEM((t, 128), F32), pltpu.VMEM((t, 128), F32)],
        sem=("parallel", "arbitrary"), exchange=exchange)


SKEW_W = ATT_WIN + ATT_QB
REL_PAD = 384


def _rel_onehot(v):
    r = lax.broadcasted_iota(jnp.int32, (REL_PAD, SKEW_W), 0)
    j = lax.broadcasted_iota(jnp.int32, (REL_PAD, SKEW_W), 1)
    dist = jnp.where(j < ATT_WIN, v * ATT_QB - j, v * ATT_QB + SKEW_W - j)
    col = jnp.clip(dist, -REL_CLIP, REL_CLIP) + REL_CLIP
    return _bf(jnp.where(col == r, 1.0, 0.0))


def _split3(v):
    hi = _bf(v)
    rest = v - hi.astype(F32)
    mid = _bf(rest)
    return hi, mid, _bf(rest - mid.astype(F32))


def _skew8(a, forward):
    row = lax.broadcasted_iota(jnp.int32, a.shape, 0)
    for b in range(3):
        shift = (1 << b) if forward else SKEW_W - (1 << b)
        a = jnp.where(((row >> b) & 1) == 1, pltpu.roll(a, shift, axis=1), a)
    return a


def _toeplitz_rows(ext_row):
    a = _skew8(jnp.broadcast_to(ext_row, (8, SKEW_W)), True)
    while a.shape[0] < ATT_QB:
        a = jnp.concatenate([a, pltpu.roll(a, a.shape[0], axis=1)], axis=0)
    return a


def _diagonal_sums(tile):
    a = tile
    while a.shape[0] > 8:
        half = a.shape[0] // 2
        a = a[:half] + pltpu.roll(a[half:], SKEW_W - half, axis=1)
    return jnp.sum(_skew8(a, False), axis=0, keepdims=True)


def _bias_tiles(rel_bias, *, name, exchange=None):
    depth = rel_bias.shape[0]
    nvar = ATT_LOOKBACK // ATT_QB + 1
    rel = jnp.pad(rel_bias, ((0, 0), (0, 0), (0, REL_PAD - N_REL)))

    def body(rel_ref, o_ref, ext_ref):
        v = pl.program_id(1)
        onehot = _rel_onehot(v)
        ext_ref[...] = sum(_dot(part, onehot) for part in _split3(rel_ref[...]))
        valid = _att_mask(v)
        for h in range(H_ATT):
            o_ref[h] = jnp.where(valid, _toeplitz_rows(ext_ref[h:h + 1, :])[:, :ATT_WIN], NEG_INF)

    outs, moved = _pcall(
        body, (rel,), name=name, grid=(depth, nvar),
        in_specs=[pl.BlockSpec((None, H_ATT, REL_PAD), lambda l, v: (l, 0, 0))],
        out_specs=[pl.BlockSpec((None, None, H_ATT, ATT_QB, ATT_WIN), lambda l, v: (l, v, 0, 0, 0))],
        out_shape=[jax.ShapeDtypeStruct((depth, nvar, H_ATT, ATT_QB, ATT_WIN), F32)],
        scratch_shapes=[pltpu.VMEM((H_ATT, SKEW_W), F32)], sem=("parallel", "parallel"), exchange=exchange)
    return outs[0], moved


def _rel_bias_grad(dbias3, *, name):
    nvar = dbias3.shape[0]

    def body(db_ref, o_ref, diag_ref):
        v = pl.program_id(0)
        for h in range(H_ATT):
            tile = jnp.concatenate([db_ref[h], jnp.zeros((ATT_QB, ATT_QB), F32)], axis=1)
            diag_ref[h:h + 1, :] = _diagonal_sums(tile)
        onehot = _rel_onehot(v)
        part = sum(_dot_nt(p, onehot) for p in _split3(diag_ref[...]))

        @pl.when(v == 0)
        def _():
            o_ref[...] = part

        @pl.when(v > 0)
        def _():
            o_ref[...] += part

    out = pl.pallas_call(
        body, name=name, grid=(nvar,),
        in_specs=[pl.BlockSpec((None, H_ATT, ATT_QB, ATT_WIN), lambda v: (v, 0, 0, 0))],
        out_specs=pl.BlockSpec((H_ATT, REL_PAD), lambda v: (0, 0)),
        out_shape=jax.ShapeDtypeStruct((H_ATT, REL_PAD), F32),
        scratch_shapes=[pltpu.VMEM((H_ATT, SKEW_W), F32)], compiler_params=_cparams("arbitrary"),
    )(dbias3)
    return out[:, :N_REL]


def _merge_fwd(h, ys, wmg, wb_t, *, name, tm=512):
    t, d = h.shape
    bw = ys[0].shape[1]
    tm = _tile(t, tm)

    def body(h_ref, y0_ref, y1_ref, y2_ref, wg_ref, wb_ref, m_ref, s_ref, p_ref):
        hv = h_ref[...]
        total = jnp.zeros((tm, d), F32)
        for b, y_ref in enumerate((y0_ref, y1_ref, y2_ref)):
            s = _sigmoid(_dot(hv, wg_ref[b]))
            p = _dot_nt(y_ref[...], wb_ref[b])
            s_ref[b] = _bf(s)
            p_ref[b] = _bf(p)
            total = total + s * p
        m_ref[...] = _bf(total)

    row = pl.BlockSpec((tm, d), lambda i: (i, 0))
    yrow = pl.BlockSpec((tm, bw), lambda i: (i, 0))
    three = pl.BlockSpec((3, tm, d), lambda i: (0, i, 0))
    return pl.pallas_call(
        body, name=name, grid=(t // tm,),
        in_specs=[row, yrow, yrow, yrow, pl.BlockSpec((3, d, d), lambda i: (0, 0, 0)),
                  pl.BlockSpec((3, d, bw), lambda i: (0, 0, 0))],
        out_specs=[row, three, three],
        out_shape=[jax.ShapeDtypeStruct((t, d), BF16), jax.ShapeDtypeStruct((3, t, d), BF16),
                   jax.ShapeDtypeStruct((3, t, d), BF16)],
        compiler_params=_cparams("parallel"),
    )(h, *ys, wmg, wb_t)


def _merge_dwb(dp3, ys, *, name, tm=1024, tk=TOKEN_TK):
    nb, t, m = dp3.shape
    n = ys[0].shape[1]
    tm, tk = _tile(m, tm), _tile(t, tk)
    nk = t // tk

    def body(a_ref, *refs):
        o_ref, acc_ref = refs[nb], refs[nb + 1]
        k = pl.program_id(2)

        def finish(total):
            o_ref[...] = total.astype(o_ref.dtype)

        for b in range(nb):
            @pl.when(pl.program_id(0) == b)
            def _(y_ref=refs[b]):
                _accumulate(acc_ref, _dot_tn(a_ref[...], y_ref[...]), k, nk, finish)

    def y_spec(b):
        return pl.BlockSpec((tk, n), lambda s, i, k: (jnp.where(s == b, k, jnp.where(s < b, 0, nk - 1)), 0))

    return pl.pallas_call(
        body, name=name, grid=(nb, m // tm, nk),
        in_specs=[pl.BlockSpec((None, tk, tm), lambda s, i, k: (s, k, i))] + [y_spec(b) for b in range(nb)],
        out_specs=pl.BlockSpec((None, tm, n), lambda s, i, k: (s, i, 0)),
        out_shape=jax.ShapeDtypeStruct((nb, m, n), PARTIAL_DTYPE), scratch_shapes=[pltpu.VMEM((tm, n), F32)],
        compiler_params=_cparams("arbitrary", "arbitrary", "arbitrary"),
    )(dp3, *ys)


def _merge_bwd(dm, s3, p3, wmg, wb_t, *, name, tm=512):
    _, t, d = s3.shape
    bw = wb_t.shape[2]
    tm = _tile(t, tm)

    def body(dm_ref, s_ref, p_ref, wg_ref, wb_ref, dgp_ref, dp_ref, dy_ref, dh_ref):
        dmv = dm_ref[...].astype(F32)
        dh = jnp.zeros((tm, d), F32)
        for b in range(3):
            s = s_ref[b].astype(F32)
            dgp = _bf(dmv * p_ref[b].astype(F32) * s * (1.0 - s))
            dp = _bf(dmv * s)
            dgp_ref[b] = dgp
            dp_ref[b] = dp
            dy_ref[b] = _bf(_dot(dp, wb_ref[b]))
            dh = dh + _dot_nt(dgp, wg_ref[b])
        dh_ref[...] = dh

    row = pl.BlockSpec((tm, d), lambda i: (i, 0))
    three = pl.BlockSpec((3, tm, d), lambda i: (0, i, 0))
    return pl.pallas_call(
        body, name=name, grid=(t // tm,),
        in_specs=[row, three, three, pl.BlockSpec((3, d, d), lambda i: (0, 0, 0)),
                  pl.BlockSpec((3, d, bw), lambda i: (0, 0, 0))],
        out_specs=[three, three, pl.BlockSpec((3, tm, bw), lambda i: (0, i, 0)), row],
        out_shape=[jax.ShapeDtypeStruct((3, t, d), BF16), jax.ShapeDtypeStruct((3, t, d), BF16),
                   jax.ShapeDtypeStruct((3, t, bw), BF16), jax.ShapeDtypeStruct((t, d), F32)],
        compiler_params=_cparams("parallel"),
    )(dm, s3, p3, wmg, wb_t)


def _carried(hooks, slot, state=None):
    if not hooks or slot not in hooks:
        return None, lambda buf: None
    make, done = hooks[slot]
    return make(state), done


def _layer_fwd(x, p, aux, tag, hooks=None):
    ex, done = _carried(hooks, "ffn1")
    (x1, h1, g1, u1, a1), buf = _ffn_fwd(x, p["n1"], p["wg1"], p["wu1"], p["wd1"], name=f"ffn1_fwd_{tag}", exchange=ex)
    done(buf)
    h2 = _rmsnorm_fwd(x1, p["nmix"], name=f"mixnorm_fwd_{tag}")
    ex, done = _carried(hooks, "inproj")
    if ex is None:
        cols3 = _mm_to_slices(h2, p["win"], name=f"inproj_fwd_{tag}")
    else:
        cols3, buf = _mm_to_slices(h2, p["win"], name=f"inproj_fwd_{tag}", exchange=ex)
        done(buf)
    bias3 = p["bias3"]
    y_conv = _conv_fwd(cols3, p["conv_w"], name=f"conv_fwd_{tag}")
    y_ret, states = _ret_fwd(cols3, aux["rope"], aux["ret"], name=f"ret_fwd_{tag}")
    ex, done = _carried(hooks, "att")
    (y_att, lse), buf = _att_fwd(cols3, bias3, name=f"att_fwd_{tag}", exchange=ex)
    done(buf)
    merged, s3, p3 = _merge_fwd(h2, (y_conv, y_ret, y_att), p["wmg"], p["wb"], name=f"merge_fwd_{tag}")
    x2 = _mm_rows(merged, p["wout"], dims=NN, out_dtype=F32, res=x1, name=f"outproj_fwd_{tag}")
    ex, done = _carried(hooks, "ffn2")
    (x3, h3, g2, u2, a2), buf = _ffn_fwd(x2, p["n2"], p["wg2"], p["wu2"], p["wd2"], name=f"ffn2_fwd_{tag}", exchange=ex)
    done(buf)
    saved = dict(x0=x, h1=h1, g1=g1, u1=u1, a1=a1, x1=x1, h2=h2, cols3=cols3, bias3=bias3, y_conv=y_conv,
                 y_ret=y_ret, states=states, y_att=y_att, lse=lse, merged=merged, s3=s3, p3=p3, x2=x2, h3=h3,
                 g2=g2, u2=u2, a2=a2)
    return x3, saved


def _ffn_grads(dxo, x, h, g, u, a, p, which, tag, grads, all_grads, hooks):
    n = which
    ex, done = _carried(hooks, f"ffn{n}_bwd", all_grads)
    (dx, dyb, dg, du, grads["n" + n]), buf = _ffn_bwd(dxo, x, p["n" + n], g, u, p["wg" + n], p["wu" + n], p["wd" + n],
                                                     name=f"ffn{n}_{tag}_bwd", exchange=ex)
    done(buf)
    grads["wd" + n] = _mm_tn(a, dyb, name=f"ffn{n}_{tag}_dwd")
    for key, lhs in (("wg" + n, dg), ("wu" + n, du)):
        ex, done = _carried(hooks, f"ffn{n}_d{key[:2]}", all_grads)
        if ex is None:
            grads[key] = _mm_tn(lhs, h, name=f"ffn{n}_{tag}_d{key[:2]}")
        else:
            grads[key], buf = _mm_tn(lhs, h, name=f"ffn{n}_{tag}_d{key[:2]}", exchange=ex)
            done(buf)
    return dx


def _layer_bwd(dx3, p, s, aux, tag, grads, all_grads, hooks=None):
    dx2 = _ffn_grads(dx3, s["x2"], s["h3"], s["g2"], s["u2"], s["a2"], p, "2", tag, grads, all_grads, hooks)

    dm = _mm_rows(dx2, p["wout"], dims=NT, out_dtype=BF16, name=f"outproj_dm_{tag}")
    grads["wout"] = _mm_tn(s["merged"], dx2, name=f"outproj_dw_{tag}")
    dgp3, dp3, dy3, dh2 = _merge_bwd(dm, s["s3"], s["p3"], p["wmg"], p["wb"], name=f"merge_bwd_{tag}")
    grads["wmg"] = _mm_tn_batch(s["h2"], dgp3, name=f"merge_dwg_{tag}")
    grads["wb"] = _merge_dwb(dp3, (s["y_conv"], s["y_ret"], s["y_att"]), name=f"merge_dwb_{tag}")

    dconv3, grads["conv_w"] = _conv_bwd(s["cols3"], p["conv_w"], dy3, name=f"conv_bwd_{tag}")
    ex, done = _carried(hooks, "ret_bwd", all_grads)
    dret4, buf = _ret_bwd(s["cols3"], aux["rope"], aux["ret"], s["states"], dy3, name=f"ret_bwd_{tag}", exchange=ex)
    done(buf)
    ex, done = _carried(hooks, "att_bwd", all_grads)
    (datt3, dbias3), buf = _att_bwd(s["cols3"], s["bias3"], s["y_att"], s["lse"], dy3, name=f"att_bwd_{tag}",
                                    exchange=ex)
    done(buf)
    grads["rel_bias"] = _rel_bias_grad(dbias3, name=f"bias_grad_{tag}")

    dh2 = _inproj_dh((dconv3, dret4, datt3), p["win"], dh2, name=f"inproj_dh_{tag}")
    grads["win"] = jnp.concatenate([
        _mm_tn_slices(dconv3, s["h2"], name=f"inproj_dw_conv_{tag}"),
        _mm_tn_slices(dret4, s["h2"], name=f"inproj_dw_ret_{tag}"),
        _mm_tn_slices(datt3, s["h2"], name=f"inproj_dw_att_{tag}")], axis=0)
    dx1, grads["nmix"] = _rmsnorm_bwd(dh2, s["x1"], p["nmix"], dx2, name=f"mixnorm_bwd_{tag}")

    return _ffn_grads(dx1, s["x0"], s["h1"], s["g1"], s["u1"], s["a1"], p, "1", tag, grads, all_grads, hooks)


def _device_step(x, target, layers, final_norm, fwd_hooks=None, bwd_hooks=None):
    t = x.shape[0]
    depth = len(layers)
    aux = dict(rope=_rope_tables(t), ret=_ret_consts())
    ex, done = _carried(fwd_hooks[0] if fwd_hooks else None, "start")
    bias_all, buf = _bias_tiles(jnp.stack([p["rel_bias"] for p in layers]), name="bias_tiles", exchange=ex)
    done(buf)
    saved = []
    for l, p in enumerate(layers):
        p["bias3"] = (bias_all, l)
        x, s = _layer_fwd(x, p, aux, f"l{l}", fwd_hooks[l] if fwd_hooks else None)
        saved.append(s)
    dx, dfinal, loss_row = _loss_fwd_bwd(x, final_norm, target, name="loss_fwd_bwd")
    grads = [dict() for _ in range(depth)]
    for l in reversed(range(depth)):
        dx = _layer_bwd(dx, layers[l], saved[l], aux, f"l{l}", grads[l], grads, bwd_hooks[l] if bwd_hooks else None)
    return loss_row, dx, grads, dfinal


def _sum_slots(bufs, *, name, tr=512):
    n, r, cdim = bufs[0].shape
    depth = len(bufs)
    tr = _tile(r, tr)
    nt = r // tr

    def body(*refs):
        o_ref = refs[depth]
        for k in range(depth):
            @pl.when(pl.program_id(0) == k)
            def _(p_ref=refs[k]):
                acc = p_ref[0].astype(F32)
                for s in range(1, n):
                    acc = acc + p_ref[s].astype(F32)
                o_ref[...] = acc

    def spec(k):
        return pl.BlockSpec((n, tr, cdim), lambda l, i: (0, jnp.where(l == k, i, jnp.where(l < k, 0, nt - 1)), 0))

    return pl.pallas_call(
        body, name=name, grid=(depth, nt), in_specs=[spec(k) for k in range(depth)],
        out_specs=pl.BlockSpec((None, tr, cdim), lambda l, i: (l, i, 0)),
        out_shape=jax.ShapeDtypeStruct((depth, r, cdim), F32), compiler_params=_cparams("arbitrary", "arbitrary"),
    )(*bufs)


def _all_reduce_small(v, *, name):
    r = v.shape[0]

    def body(x_ref, o_ref, slots, send_sems, recv_sems):
        x, y, c, me = _my_position()
        slots[me] = x_ref[...]
        sends = []
        for k in range(1, N_DEV):
            peer, _ = _peer(x, y, c, k)
            cp = pltpu.make_async_remote_copy(src_ref=x_ref, dst_ref=slots.at[me], send_sem=send_sems.at[k - 1],
                                              recv_sem=recv_sems.at[k - 1], device_id=peer, device_id_type=MESH)
            cp.start()
            sends.append(cp)
        for k in range(1, N_DEV):
            peer, peer_id = _peer(x, y, c, k)
            pltpu.make_async_remote_copy(src_ref=x_ref, dst_ref=slots.at[peer_id], send_sem=send_sems.at[k - 1],
                                         recv_sem=recv_sems.at[k - 1], device_id=peer, device_id_type=MESH).wait_recv()
        for cp in sends:
            cp.wait_send()
        acc = slots[0]
        for s in range(1, N_DEV):
            acc = acc + slots[s]
        o_ref[...] = acc

    return pl.pallas_call(
        body, name=name, in_specs=[pl.BlockSpec(memory_space=pltpu.VMEM)],
        out_specs=pl.BlockSpec(memory_space=pltpu.VMEM), out_shape=jax.ShapeDtypeStruct((r, 128), F32),
        scratch_shapes=[pltpu.VMEM((N_DEV, r, 128), F32), pltpu.SemaphoreType.DMA((N_DEV - 1,)),
                        pltpu.SemaphoreType.DMA((N_DEV - 1,))],
    )(v)


def _adam_update(wv, gv, mv, vv, d_ref, mo_ref, vo_ref):
    mn = ADAM_B1 * mv + (1.0 - ADAM_B1) * gv
    vn = ADAM_B2 * vv + (1.0 - ADAM_B2) * (gv * gv)
    m_hat = mn / (1.0 - ADAM_B1 ** ADAM_STEP)
    v_hat = vn / (1.0 - ADAM_B2 ** ADAM_STEP)
    d_ref[...] = -ADAM_LR * (m_hat / (jnp.sqrt(v_hat) + ADAM_EPS) + ADAM_WD * wv)
    mo_ref[...] = mn
    vo_ref[...] = vn


def _adamw_of_partials(w, bufs, m, v, *, name, tr=512):
    depth, r, cdim = w.shape
    n = bufs[0].shape[0]
    tr = _tile(r, tr)
    nt = r // tr

    def body(w_ref, m_ref, v_ref, *refs):
        d_ref, mo_ref, vo_ref, g_ref = refs[depth:]
        for k in range(depth):
            @pl.when(pl.program_id(0) == k)
            def _(p_ref=refs[k]):
                gv = p_ref[0].astype(F32)
                for s in range(1, n):
                    gv = gv + p_ref[s].astype(F32)
                g_ref[...] = gv
                _adam_update(w_ref[...], gv, m_ref[...], v_ref[...], d_ref, mo_ref, vo_ref)

    def part_spec(k):
        return pl.BlockSpec((n, tr, cdim), lambda l, i: (0, jnp.where(l == k, i, jnp.where(l < k, 0, nt - 1)), 0))

    spec = pl.BlockSpec((None, tr, cdim), lambda l, i: (l, i, 0))
    return pl.pallas_call(
        body, name=name, grid=(depth, nt), in_specs=[spec] * 3 + [part_spec(k) for k in range(depth)],
        out_specs=[spec] * 4, out_shape=[jax.ShapeDtypeStruct((depth, r, cdim), F32)] * 4,
        compiler_params=_cparams("arbitrary", "arbitrary"),
    )(w, m, v, *bufs)


def _adamw(w, g, m, v, *, name, tr=256):
    shape = w.shape
    cdim = shape[-1]
    w2, g2, m2, v2 = (a.reshape(-1, cdim) for a in (w, g, m, v))
    r = w2.shape[0]
    tr = _tile(r, tr) if r % 8 == 0 else r

    def body(w_ref, g_ref, m_ref, v_ref, d_ref, mo_ref, vo_ref):
        _adam_update(w_ref[...], g_ref[...], m_ref[...], v_ref[...], d_ref, mo_ref, vo_ref)

    spec = pl.BlockSpec((tr, cdim), lambda i: (i, 0))
    outs = pl.pallas_call(
        body, name=name, grid=(r // tr,), in_specs=[spec] * 4, out_specs=[spec] * 3,
        out_shape=[jax.ShapeDtypeStruct((r, cdim), F32)] * 3, compiler_params=_cparams("parallel"),
    )(w2, g2, m2, v2)
    return tuple(o.reshape(shape) for o in outs)


GROUPS = {"f1": ("wg1", "wu1", "wd1"), "in": ("win",), "mg": ("wb", "wmg", "wout"), "f2": ("wg2", "wu2", "wd2")}
BIG = tuple(nm for members in GROUPS.values() for nm in members)
ROWS_DOMAIN = ("wg1", "wu1", "wd1", "wmg", "wout", "wg2", "wu2", "wd2")


def _to_rows(name, w, d):
    depth = w.shape[0]
    if name in ("wg1", "wu1", "wg2", "wu2", "win"):
        return w.transpose(0, 2, 1)
    if name == "wb":
        return w.transpose(0, 1, 3, 2).reshape(depth, -1, d)
    if name == "wmg":
        return w.reshape(depth, -1, d)
    return w


def _from_rows(name, rows, d):
    depth = rows.shape[0]
    if name in ("wg1", "wu1", "wg2", "wu2", "win"):
        return rows.transpose(0, 2, 1)
    if name == "wb":
        return rows.reshape(depth, 3, -1, BRANCH_W).transpose(0, 1, 3, 2)
    if name == "wmg":
        return rows.reshape(depth, 3, -1, d)
    return rows


def _full_from_gathered(name, g, d):
    if name == "wb":
        return g.reshape(N_DEV, 3, -1, BRANCH_W).transpose(1, 0, 2, 3).reshape(3, d, BRANCH_W)
    if name == "wmg":
        return g.reshape(N_DEV, 3, -1, d).transpose(1, 0, 2, 3).reshape(3, d, d)
    return g.reshape(-1, d)


def _gathered_from_full(name, full, d):
    if name == "wb":
        return full.reshape(3, N_DEV, -1, BRANCH_W).transpose(1, 0, 2, 3).reshape(N_DEV, -1, d)
    if name == "wmg":
        return full.reshape(3, N_DEV, -1, d).transpose(1, 0, 2, 3).reshape(N_DEV, -1, d)
    return full.reshape(N_DEV, -1, d)


def kernel(x, ffn1_norm, ffn1_w_gate, ffn1_w_up, ffn1_w_down, mix_norm, w_in, conv_w, rel_bias, w_branch, w_merge_gate, w_out, ffn2_norm, ffn2_w_gate, ffn2_w_up, ffn2_w_down, final_norm, loss_target, m_ffn1_norm, m_ffn1_w_gate, m_ffn1_w_up, m_ffn1_w_down, m_mix_norm, m_w_in, m_conv_w, m_rel_bias, m_w_branch, m_w_merge_gate, m_w_out, m_ffn2_norm, m_ffn2_w_gate, m_ffn2_w_up, m_ffn2_w_down, m_final_norm, v_ffn1_norm, v_ffn1_w_gate, v_ffn1_w_up, v_ffn1_w_down, v_mix_norm, v_w_in, v_conv_w, v_rel_bias, v_w_branch, v_w_merge_gate, v_w_out, v_ffn2_norm, v_ffn2_w_gate, v_ffn2_w_up, v_ffn2_w_down, v_final_norm):
    names = ["ffn1_norm", "ffn1_w_gate", "ffn1_w_up", "ffn1_w_down", "mix_norm", "w_in", "conv_w", "rel_bias",
             "w_branch", "w_merge_gate", "w_out", "ffn2_norm", "ffn2_w_gate", "ffn2_w_up", "ffn2_w_down", "final_norm"]
    weights = dict(zip(names, (ffn1_norm, ffn1_w_gate, ffn1_w_up, ffn1_w_down, mix_norm, w_in, conv_w, rel_bias,
                               w_branch, w_merge_gate, w_out, ffn2_norm, ffn2_w_gate, ffn2_w_up, ffn2_w_down,
                               final_norm)))
    m_in = dict(zip(names, (m_ffn1_norm, m_ffn1_w_gate, m_ffn1_w_up, m_ffn1_w_down, m_mix_norm, m_w_in, m_conv_w,
                            m_rel_bias, m_w_branch, m_w_merge_gate, m_w_out, m_ffn2_norm, m_ffn2_w_gate,
                            m_ffn2_w_up, m_ffn2_w_down, m_final_norm)))
    v_in = dict(zip(names, (v_ffn1_norm, v_ffn1_w_gate, v_ffn1_w_up, v_ffn1_w_down, v_mix_norm, v_w_in, v_conv_w,
                            v_rel_bias, v_w_branch, v_w_merge_gate, v_w_out, v_ffn2_norm, v_ffn2_w_gate,
                            v_ffn2_w_up, v_ffn2_w_down, v_final_norm)))
    big_of = dict(wg1="ffn1_w_gate", wu1="ffn1_w_up", wd1="ffn1_w_down", win="w_in", wb="w_branch",
                  wmg="w_merge_gate", wout="w_out", wg2="ffn2_w_gate", wu2="ffn2_w_up", wd2="ffn2_w_down")
    depth = ffn1_norm.shape[0]
    d = x.shape[-1]
    xs = x.reshape(-1, d)
    target = loss_target.reshape(-1, d)
    _, _, _, me = _my_position()

    shard_rows = {nm: _to_rows(nm, weights[big_of[nm]], d).astype(BF16) for nm in BIG}

    conv_cols = conv_w.shape[-1]
    conv_full, = _exchange("gather", [conv_w.reshape(depth * 3, conv_cols)], name="gather_conv_w")
    conv_full = conv_full.reshape(N_DEV, depth, 3, conv_cols).transpose(1, 2, 0, 3).reshape(depth, 3, -1)
    layers = [dict(n1=ffn1_norm[l][None], nmix=mix_norm[l][None], n2=ffn2_norm[l][None], conv_w=conv_full[l],
                   rel_bias=rel_bias[l]) for l in range(depth)]

    def gather_hook(l, members, kind="gather"):
        def done(bufs):
            layers[l].update({nm: _full_from_gathered(nm, buf, d) for nm, buf in zip(members, bufs)})

        return (lambda _: (kind, [shard_rows[nm] for nm in members], [l] * len(members))), done

    fwd_hooks = []
    for l in range(depth):
        hooks = {"ffn1": gather_hook(l, ("win", "wd2")), "inproj": gather_hook(l, GROUPS["mg"]),
                 "att": gather_hook(l, ("wg2", "wu2"))}
        if l == 0:
            hooks["start"] = gather_hook(0, GROUPS["f1"], "gather2")
        if l + 1 < depth:
            hooks["ffn2"] = gather_hook(l + 1, GROUPS["f1"])
        fwd_hooks.append(hooks)

    received = {}

    def scatter_hook(l, members):
        def make(all_grads):
            return "scatter", [_gathered_from_full(nm, all_grads[l][nm], d) for nm in members], [None] * len(members)

        return make, (lambda bufs: received.update({(l, nm): buf for nm, buf in zip(members, bufs)}))

    bwd_hooks = []
    for l in range(depth):
        hooks = {"att_bwd": scatter_hook(l, GROUPS["f2"]), "ret_bwd": scatter_hook(l, ("wb", "wout")),
                 "ffn1_bwd": scatter_hook(l, ("win", "wmg"))}
        if l + 1 < depth:
            hooks["ffn2_bwd"] = scatter_hook(l + 1, GROUPS["f1"])
        if l == 0:
            hooks["ffn1_dwg"] = scatter_hook(0, ("wd1",))
            hooks["ffn1_dwu"] = scatter_hook(0, ("wg1",))
        bwd_hooks.append(hooks)

    loss_row, dx, grads, dfinal = _device_step(xs, target, layers, final_norm[None], fwd_hooks, bwd_hooks)
    received[(0, "wu1")], = _exchange("scatter", [_gathered_from_full("wu1", grads[0]["wu1"], d)], name="scatter_l0_wu1")

    partials = {nm: [received[(l, nm)] for l in range(depth)] for nm in BIG}
    grad_w = {big_of[nm]: _from_rows(nm, _sum_slots(partials[nm], name=f"sum_grads_{nm}"), d)
              for nm in BIG if nm not in ROWS_DOMAIN}

    small = {"ffn1_norm": jnp.stack([grads[l]["n1"][0] for l in range(depth)]),
             "mix_norm": jnp.stack([grads[l]["nmix"][0] for l in range(depth)]),
             "ffn2_norm": jnp.stack([grads[l]["n2"][0] for l in range(depth)]),
             "final_norm": dfinal[0],
             "rel_bias": jnp.stack([grads[l]["rel_bias"] for l in range(depth)]),
             "conv_w": jnp.stack([grads[l]["conv_w"] for l in range(depth)]),
             "loss": loss_row[0, :1]}
    order = list(small)
    flat = jnp.concatenate([small[k].reshape(-1) for k in order])
    pad = (-flat.shape[0]) % 1024
    summed = _all_reduce_small(jnp.pad(flat, (0, pad)).reshape(-1, 128), name="reduce_small").reshape(-1)
    pos = 0
    for k in order:
        n = small[k].size
        small[k] = summed[pos:pos + n].reshape(small[k].shape)
        pos += n
    small["conv_w"] = lax.dynamic_slice_in_dim(small["conv_w"], me * conv_cols, conv_cols, axis=2)

    grad_w.update({k: small[k] for k in order if k != "loss"})
    delta, new_m, new_v = {}, {}, {}
    short = {v: k for k, v in big_of.items()}
    for nm in names:
        key = short.get(nm)
        if key in ROWS_DOMAIN:
            outs = _adamw_of_partials(_to_rows(key, weights[nm], d), partials[key], _to_rows(key, m_in[nm], d),
                                      _to_rows(key, v_in[nm], d), name=f"adamw_{nm}")
            delta[nm], new_m[nm], new_v[nm], grad_w[nm] = (_from_rows(key, o, d) for o in outs)
        else:
            delta[nm], new_m[nm], new_v[nm] = _adamw(weights[nm], grad_w[nm], m_in[nm], v_in[nm], name=f"adamw_{nm}")
    return (small["loss"].reshape(()), dx.reshape(x.shape), *[grad_w[n] for n in names], *[delta[n] for n in names],
            *[new_m[n] for n in names], *[new_v[n] for n in names])
```

```python
import functools
import math

import jax
import jax.numpy as jnp
import numpy as np
from jax import lax
from jax.experimental import pallas as pl
from jax.experimental.pallas import tpu as pltpu

F32 = jnp.float32
BF16 = jnp.bfloat16

N_DEV = 8
EPS = 1e-6
CHUNK = 64
BRANCH_W = 512
N_SLICES = 10
H_RET = 4
DK_RET = 128
H_ATT = 8
DH_ATT = 64
N_PREV_CHUNKS = 8
REL_CLIP = 128
N_REL = 2 * REL_CLIP + 1
NEG_INF = -1e30
ROPE_BASE = 10000.0
ATT_QB = 256
ATT_LOOKBACK = N_PREV_CHUNKS * CHUNK
ATT_WIN = ATT_LOOKBACK + ATT_QB
RET_TB = 256
RET_GROUP = 4
TOKEN_TK = 2048
PARTIAL_DTYPE = jnp.bfloat16
CONV_HALO = 16

ADAM_LR = 0.001
ADAM_B1 = 0.9
ADAM_B2 = 0.999
ADAM_EPS = 1e-08
ADAM_WD = 0.01
ADAM_STEP = 10

VMEM_LIMIT_BYTES = 56 * 1024 * 1024
MESH = pl.DeviceIdType.MESH


def _cparams(*sem):
    return pltpu.CompilerParams(dimension_semantics=sem, vmem_limit_bytes=VMEM_LIMIT_BYTES)


def _dot(a, b):
    return lax.dot_general(a, b, (((1,), (0,)), ((), ())), preferred_element_type=F32)


def _dot_nt(a, b):
    return lax.dot_general(a, b, (((1,), (1,)), ((), ())), preferred_element_type=F32)


def _dot_tn(a, b):
    return lax.dot_general(a, b, (((0,), (0,)), ((), ())), preferred_element_type=F32)


def _bf(v):
    return v.astype(BF16)


def _sigmoid(v):
    return 1.0 / (1.0 + jnp.exp(-v))


def _tile(n, want):
    if n <= want:
        return n
    for t in range(want - want % 128, 0, -128):
        if n % t == 0:
            return t
    t = want
    while n % t:
        t //= 2
    return t


def _my_position():
    x, y, c = lax.axis_index("x"), lax.axis_index("y"), lax.axis_index("c")
    return x, y, c, 4 * x + 2 * y + c


def _peer(x, y, c, k):
    px = 1 - x if k & 4 else x
    py = 1 - y if k & 2 else y
    pc = 1 - c if k & 1 else c
    return (px, py, pc), 4 * px + 2 * py + pc


DIRECT2 = (1, 2, 4, 6)
PASSED_ON = (2, 4, 6)
N_PUSHES = N_DEV - 1 + len(PASSED_ON)


def _exchange_sems(n):
    return [pltpu.SemaphoreType.DMA((n, N_PUSHES)), pltpu.SemaphoreType.DMA((n, N_PUSHES)),
            pltpu.SemaphoreType.DMA((n,))]


def _exchange_src(kind, src_ref, layer, block):
    if kind in ("gather", "gather2"):
        return src_ref if layer is None else src_ref.at[layer]
    return src_ref.at[block]


def _passed_on(dst_ref, sems, a, j, pos, incoming):
    x, y, c, _ = pos
    sibling, _ = _peer(x, y, c, 1)
    _, block = _peer(x, y, c, PASSED_ON[j] | 1 if incoming else PASSED_ON[j])
    return pltpu.make_async_remote_copy(src_ref=dst_ref.at[block], dst_ref=dst_ref.at[block],
                                        send_sem=sems[0].at[a, N_DEV - 1 + j], recv_sem=sems[1].at[a, N_DEV - 1 + j],
                                        device_id=sibling, device_id_type=MESH)


def _exchange_copy(kind, src_ref, layer, dst_ref, sems, a, k, pos, incoming):
    x, y, c, me = pos
    peer, peer_id = _peer(x, y, c, k)
    return pltpu.make_async_remote_copy(src_ref=_exchange_src(kind, src_ref, layer, me if incoming else peer_id),
                                        dst_ref=dst_ref.at[peer_id if incoming else me],
                                        send_sem=sems[0].at[a, k - 1], recv_sem=sems[1].at[a, k - 1],
                                        device_id=peer, device_id_type=MESH)


def _exchange_local(kind, src_ref, layer, dst_ref, sems, a, me):
    return pltpu.make_async_copy(_exchange_src(kind, src_ref, layer, me), dst_ref.at[me], sems[2].at[a])


def _exchange_start(kind, layers, src_refs, dst_refs, sems):
    pos = _my_position()
    direct = DIRECT2 if kind == "gather2" else range(1, N_DEV)
    for a, (src_ref, layer, dst_ref) in enumerate(zip(src_refs, layers, dst_refs)):
        _exchange_local(kind, src_ref, layer, dst_ref, sems, a, pos[3]).start()
        for k in direct:
            _exchange_copy(kind, src_ref, layer, dst_ref, sems, a, k, pos, False).start()


def _exchange_wait(kind, layers, src_refs, dst_refs, sems):
    pos = _my_position()
    for a, (src_ref, layer, dst_ref) in enumerate(zip(src_refs, layers, dst_refs)):
        if kind == "gather2":
            for j, k in enumerate(PASSED_ON):
                _exchange_copy(kind, src_ref, layer, dst_ref, sems, a, k, pos, True).wait_recv()
                _passed_on(dst_ref, sems, a, j, pos, False).start()
            _exchange_copy(kind, src_ref, layer, dst_ref, sems, a, 1, pos, True).wait_recv()
            for j in range(len(PASSED_ON)):
                _passed_on(dst_ref, sems, a, j, pos, True).wait_recv()
            for k in DIRECT2:
                _exchange_copy(kind, src_ref, layer, dst_ref, sems, a, k, pos, False).wait_send()
            for j in range(len(PASSED_ON)):
                _passed_on(dst_ref, sems, a, j, pos, False).wait_send()
        else:
            for k in range(1, N_DEV):
                _exchange_copy(kind, src_ref, layer, dst_ref, sems, a, k, pos, True).wait_recv()
            for k in range(1, N_DEV):
                _exchange_copy(kind, src_ref, layer, dst_ref, sems, a, k, pos, False).wait_send()
        _exchange_local(kind, src_ref, layer, dst_ref, sems, a, pos[3]).wait()


def _exchange_shapes(kind, srcs, layers):
    if kind in ("gather", "gather2"):
        return [jax.ShapeDtypeStruct((N_DEV,) + (s.shape if l is None else s.shape[1:]), s.dtype)
                for s, l in zip(srcs, layers)]
    return [jax.ShapeDtypeStruct(s.shape, s.dtype) for s in srcs]


def _exchange(kind, srcs, layers=None, *, name):
    n = len(srcs)
    layers = layers or [None] * n
    hbm = pl.BlockSpec(memory_space=pl.ANY)

    def body(*refs):
        src_refs, dst_refs, sems = refs[:n], refs[n:2 * n], refs[2 * n:]
        _exchange_start(kind, layers, src_refs, dst_refs, sems)
        _exchange_wait(kind, layers, src_refs, dst_refs, sems)

    return pl.pallas_call(body, name=name, in_specs=[hbm] * n, out_specs=[hbm] * n,
                          out_shape=_exchange_shapes(kind, srcs, layers), scratch_shapes=_exchange_sems(n))(*srcs)


def _pcall(body, args, *, name, grid, in_specs, out_specs, out_shape, scratch_shapes, sem, exchange=None):
    if exchange is None:
        outs = pl.pallas_call(body, name=name, grid=grid, in_specs=in_specs, out_specs=out_specs, out_shape=out_shape,
                              scratch_shapes=scratch_shapes, compiler_params=_cparams(*sem))(*args)
        return outs, None
    kind, srcs, layers = exchange
    n_in, n_out, n_scr, n_x = len(in_specs), len(out_specs), len(scratch_shapes), len(srcs)
    hbm = pl.BlockSpec(memory_space=pl.ANY)

    def carrier(*refs):
        ins, refs = refs[:n_in], refs[n_in:]
        src_refs, refs = refs[:n_x], refs[n_x:]
        outs, refs = refs[:n_out], refs[n_out:]
        dst_refs, refs = refs[:n_x], refs[n_x:]
        scr, sems = refs[:n_scr], refs[n_scr:]
        ids = [pl.program_id(a) for a in range(len(grid))]
        first = functools.reduce(jnp.logical_and, [i == 0 for i in ids])
        last = functools.reduce(jnp.logical_and, [i == g - 1 for i, g in zip(ids, grid)])

        @pl.when(first)
        def _():
            _exchange_start(kind, layers, src_refs, dst_refs, sems)

        body(*ins, *outs, *scr)

        @pl.when(last)
        def _():
            _exchange_wait(kind, layers, src_refs, dst_refs, sems)

    outs = pl.pallas_call(
        carrier, name=name, grid=grid, in_specs=list(in_specs) + [hbm] * n_x, out_specs=list(out_specs) + [hbm] * n_x,
        out_shape=list(out_shape) + _exchange_shapes(kind, srcs, layers),
        scratch_shapes=list(scratch_shapes) + _exchange_sems(n_x),
        compiler_params=_cparams(*(["arbitrary"] * len(grid))))(*args, *srcs)
    return outs[:n_out], outs[n_out:]


def _accumulate(acc_ref, part, k, nk, finish):
    if nk == 1:
        finish(part)
        return

    @pl.when(k == 0)
    def _():
        acc_ref[...] = part

    @pl.when(jnp.logical_and(k > 0, k < nk - 1))
    def _():
        acc_ref[...] += part

    @pl.when(k == nk - 1)
    def _():
        finish(acc_ref[...] + part)


def _matmul(a, b, *, dims, grid, a_spec, b_spec, o_spec, out_shape, acc_shape, name, scale=1.0,
            res=None, res_spec=None, exchange=None):
    nk = grid[3]
    has_res = res is not None

    def body(*refs):
        if has_res:
            a_ref, b_ref, r_ref, o_ref = refs[:4]
        else:
            a_ref, b_ref, o_ref = refs[:3]
            r_ref = None

        def finish(acc):
            if scale != 1.0:
                acc = acc * scale
            if r_ref is not None:
                acc = acc + r_ref[...].astype(F32)
            o_ref[...] = acc.astype(o_ref.dtype)

        part = lax.dot_general(_bf(a_ref[...]), _bf(b_ref[...]), (dims, ((), ())), preferred_element_type=F32)
        _accumulate(None if nk == 1 else refs[-1], part, pl.program_id(3), nk, finish)

    in_specs = [a_spec, b_spec] + ([res_spec] if has_res else [])
    args = (a, b) + ((res,) if has_res else ())
    outs, moved = _pcall(
        body, args, name=name, grid=grid, in_specs=in_specs, out_specs=[o_spec], out_shape=[out_shape],
        scratch_shapes=[] if nk == 1 else [pltpu.VMEM(acc_shape, F32)],
        sem=("parallel", "parallel", "parallel", "arbitrary"), exchange=exchange)
    return outs[0] if exchange is None else (outs[0], moved)


NT = ((1,), (1,))
NN = ((1,), (0,))
TN = ((0,), (0,))


def _mm_rows(a, b, *, dims, out_dtype, name, res=None, scale=1.0, tm=1024, tn=1024, tk=1024):
    m, kdim = a.shape
    n = b.shape[1] if dims == NN else b.shape[0]
    tm, tn, tk = _tile(m, tm), _tile(n, tn), _tile(kdim, tk)
    grid = (1, m // tm, n // tn, kdim // tk)
    a_spec = pl.BlockSpec((tm, tk), lambda s, i, j, k: (i, k))
    if dims == NN:
        b_spec = pl.BlockSpec((tk, tn), lambda s, i, j, k: (k, j))
    else:
        b_spec = pl.BlockSpec((tn, tk), lambda s, i, j, k: (j, k))
    o_spec = pl.BlockSpec((tm, tn), lambda s, i, j, k: (i, j))
    return _matmul(a, b, dims=dims, grid=grid, a_spec=a_spec, b_spec=b_spec, o_spec=o_spec,
                   out_shape=jax.ShapeDtypeStruct((m, n), out_dtype), acc_shape=(tm, tn), name=name,
                   res=res, res_spec=o_spec if res is not None else None, scale=scale)


def _mm_to_slices(a, b_t, *, name, tm=2048, exchange=None):
    m, kdim = a.shape
    n = b_t.shape[0]
    tm = _tile(m, tm)
    w = BRANCH_W
    grid = (1, m // tm, n // w, 1)
    return _matmul(a, b_t, dims=NT, grid=grid,
                   a_spec=pl.BlockSpec((tm, kdim), lambda s, i, j, k: (i, 0)),
                   b_spec=pl.BlockSpec((w, kdim), lambda s, i, j, k: (j, 0)),
                   o_spec=pl.BlockSpec((None, tm, w), lambda s, i, j, k: (j, i, 0)),
                   out_shape=jax.ShapeDtypeStruct((n // w, m, w), BF16), acc_shape=(tm, w), name=name,
                   exchange=exchange)


def _inproj_dh(parts, w_t, res, *, name, tm=512):
    t, n = res.shape
    w = BRANCH_W
    tm = _tile(t, tm)
    counts = [p.shape[0] for p in parts]
    assert sum(counts) * w == w_t.shape[0]

    def body(*refs):
        w_ref, r_ref, o_ref = refs[len(parts):]
        cols = jnp.concatenate([p_ref[s] for p_ref, cnt in zip(refs, counts) for s in range(cnt)], axis=1)
        o_ref[...] = r_ref[...] + _dot(cols, w_ref[...])

    row = pl.BlockSpec((tm, n), lambda i: (i, 0))
    return pl.pallas_call(
        body, name=name, grid=(t // tm,),
        in_specs=[pl.BlockSpec((cnt, tm, w), lambda i: (0, i, 0)) for cnt in counts]
        + [pl.BlockSpec(w_t.shape, lambda i: (0, 0)), row],
        out_specs=row, out_shape=jax.ShapeDtypeStruct((t, n), F32), compiler_params=_cparams("parallel"),
    )(*parts, w_t, res)


def _mm_tn(a, b, *, name, tm=1408, tn=1024, tk=TOKEN_TK, exchange=None):
    t, m = a.shape
    n = b.shape[1]
    tm, tn, tk = _tile(m, tm), _tile(n, tn), _tile(t, tk)
    grid = (1, m // tm, n // tn, t // tk)
    return _matmul(a, b, dims=TN, grid=grid,
                   a_spec=pl.BlockSpec((tk, tm), lambda s, i, j, k: (k, i)),
                   b_spec=pl.BlockSpec((tk, tn), lambda s, i, j, k: (k, j)),
                   o_spec=pl.BlockSpec((tm, tn), lambda s, i, j, k: (i, j)),
                   out_shape=jax.ShapeDtypeStruct((m, n), PARTIAL_DTYPE), acc_shape=(tm, tn), name=name,
                   exchange=exchange)


def _mm_tn_slices(a3, b, *, name, tn=1024, tk=TOKEN_TK):
    s_n, t, w = a3.shape
    n = b.shape[1]
    tn, tk = _tile(n, tn), _tile(t, tk)
    grid = (1, s_n, n // tn, t // tk)
    return _matmul(a3, b, dims=TN, grid=grid,
                   a_spec=pl.BlockSpec((None, tk, w), lambda s, i, j, k: (i, k, 0)),
                   b_spec=pl.BlockSpec((tk, tn), lambda s, i, j, k: (k, j)),
                   o_spec=pl.BlockSpec((w, tn), lambda s, i, j, k: (i, j)),
                   out_shape=jax.ShapeDtypeStruct((s_n * w, n), PARTIAL_DTYPE), acc_shape=(w, tn), name=name)


def _mm_tn_batch(a, b3, *, name, tm=1024, tn=1024, tk=TOKEN_TK):
    s_n, t, n = b3.shape
    m = a.shape[-1]
    tm, tn, tk = _tile(m, tm), _tile(n, tn), _tile(t, tk)
    grid = (s_n, m // tm, n // tn, t // tk)
    return _matmul(a, b3, dims=TN, grid=grid, a_spec=pl.BlockSpec((tk, tm), lambda s, i, j, k: (k, i)),
                   b_spec=pl.BlockSpec((None, tk, tn), lambda s, i, j, k: (s, k, j)),
                   o_spec=pl.BlockSpec((None, tm, tn), lambda s, i, j, k: (s, i, j)),
                   out_shape=jax.ShapeDtypeStruct((s_n, m, n), PARTIAL_DTYPE), acc_shape=(tm, tn), name=name)


def _norm_parts(xf):
    r = lax.rsqrt(jnp.mean(xf * xf, axis=-1, keepdims=True) + EPS)
    return xf * r, r


def _norm_bwd(dh, xhat, r, w):
    dxhat = dh * w
    dx = r * (dxhat - xhat * jnp.mean(dxhat * xhat, axis=-1, keepdims=True))
    return dx, jnp.sum(dh * xhat, axis=0, keepdims=True)


def _rmsnorm_fwd(x, w, *, name, tm=1024):
    t, d = x.shape
    tm = _tile(t, tm)

    def body(x_ref, w_ref, h_ref):
        xhat, _ = _norm_parts(x_ref[...])
        h_ref[...] = _bf(xhat * w_ref[...])

    return pl.pallas_call(
        body, name=name, grid=(t // tm,),
        in_specs=[pl.BlockSpec((tm, d), lambda i: (i, 0)), pl.BlockSpec((1, d), lambda i: (0, 0))],
        out_specs=pl.BlockSpec((tm, d), lambda i: (i, 0)),
        out_shape=jax.ShapeDtypeStruct((t, d), BF16), compiler_params=_cparams("parallel"),
    )(x, w)


def _rmsnorm_bwd(dh, x, w, dres, *, name, tm=1024):
    t, d = x.shape
    tm = _tile(t, tm)

    def body(dh_ref, x_ref, w_ref, dres_ref, dx_ref, dw_ref):
        xhat, r = _norm_parts(x_ref[...])
        dx, dw = _norm_bwd(dh_ref[...], xhat, r, w_ref[...])
        dx_ref[...] = dres_ref[...] + dx

        @pl.when(pl.program_id(0) == 0)
        def _():
            dw_ref[...] = dw

        @pl.when(pl.program_id(0) > 0)
        def _():
            dw_ref[...] += dw

    row = pl.BlockSpec((tm, d), lambda i: (i, 0))
    vec = pl.BlockSpec((1, d), lambda i: (0, 0))
    return pl.pallas_call(
        body, name=name, grid=(t // tm,), in_specs=[row, row, vec, row], out_specs=[row, vec],
        out_shape=[jax.ShapeDtypeStruct((t, d), F32), jax.ShapeDtypeStruct((1, d), F32)],
        compiler_params=_cparams("arbitrary"),
    )(dh, x, w, dres)


def _loss_fwd_bwd(x, w, target, *, name, tm=1024):
    t, d = x.shape
    tm = _tile(t, tm)

    def body(x_ref, w_ref, t_ref, dx_ref, dw_ref, loss_ref):
        xhat, r = _norm_parts(x_ref[...])
        wv = w_ref[...]
        err = xhat * wv - t_ref[...]
        dx, dw = _norm_bwd(err * (1.0 / d), xhat, r, wv)
        dx_ref[...] = dx
        part = jnp.full((1, 128), 0.5 / d, F32) * jnp.sum(err * err)

        @pl.when(pl.program_id(0) == 0)
        def _():
            dw_ref[...] = dw
            loss_ref[...] = part

        @pl.when(pl.program_id(0) > 0)
        def _():
            dw_ref[...] += dw
            loss_ref[...] += part

    row = pl.BlockSpec((tm, d), lambda i: (i, 0))
    vec = pl.BlockSpec((1, d), lambda i: (0, 0))
    return pl.pallas_call(
        body, name=name, grid=(t // tm,), in_specs=[row, vec, row],
        out_specs=[row, vec, pl.BlockSpec((1, 128), lambda i: (0, 0))],
        out_shape=[jax.ShapeDtypeStruct((t, d), F32), jax.ShapeDtypeStruct((1, d), F32),
                   jax.ShapeDtypeStruct((1, 128), F32)],
        compiler_params=_cparams("arbitrary"),
    )(x, w, target)


def _ffn_tiles(t, f):
    tf = f
    for cand in (1408, 1024, 512, 256, 128):
        if f % cand == 0:
            tf = cand
            break
    return _tile(t, 512), tf


def _ffn_fwd(x, nw, wg_t, wu_t, wd, *, name, exchange=None):
    t, d = x.shape
    f = wd.shape[0]
    tm, tf = _ffn_tiles(t, f)
    nf = f // tf

    def body(x_ref, nw_ref, wg_ref, wu_ref, wd_ref, xo_ref, h_ref, g_ref, u_ref, a_ref, hs_ref, acc_ref):
        j = pl.program_id(1)

        @pl.when(j == 0)
        def _():
            xhat, _ = _norm_parts(x_ref[...])
            hb = _bf(xhat * nw_ref[...])
            hs_ref[...] = hb
            h_ref[...] = hb

        hb = hs_ref[...]
        g = _dot_nt(hb, wg_ref[...])
        u = _dot_nt(hb, wu_ref[...])
        a = _bf(g * _sigmoid(g) * u)
        g_ref[...] = _bf(g)
        u_ref[...] = _bf(u)
        a_ref[...] = a
        part = _dot(a, wd_ref[...])

        def finish(total):
            xo_ref[...] = x_ref[...] + 0.5 * total

        _accumulate(acc_ref, part, j, nf, finish)

    row = pl.BlockSpec((tm, d), lambda i, j: (i, 0))
    wspec = pl.BlockSpec((tf, d), lambda i, j: (j, 0))
    hid = pl.BlockSpec((tm, tf), lambda i, j: (i, j))
    return _pcall(
        body, (x, nw, wg_t, wu_t, wd), name=name, grid=(t // tm, nf),
        in_specs=[row, pl.BlockSpec((1, d), lambda i, j: (0, 0)), wspec, wspec, wspec],
        out_specs=[row, row, hid, hid, hid],
        out_shape=[jax.ShapeDtypeStruct((t, d), F32), jax.ShapeDtypeStruct((t, d), BF16)]
        + [jax.ShapeDtypeStruct((t, f), BF16)] * 3,
        scratch_shapes=[pltpu.VMEM((tm, d), BF16), pltpu.VMEM((tm, d), F32)],
        sem=("parallel", "arbitrary"), exchange=exchange)


def _ffn_bwd(dxo, x, nw, g, u, wg_t, wu_t, wd, *, name, exchange=None):
    t, d = x.shape
    f = wd.shape[0]
    tm, tf = _ffn_tiles(t, f)
    nf = f // tf

    def body(dxo_ref, x_ref, nw_ref, g_ref, u_ref, wg_ref, wu_ref, wd_ref,
             dx_ref, dy_ref, dg_ref, du_ref, dnw_ref, dys_ref, acc_ref):
        i, j = pl.program_id(0), pl.program_id(1)

        @pl.when(j == 0)
        def _():
            dyb = _bf(0.5 * dxo_ref[...])
            dys_ref[...] = dyb
            dy_ref[...] = dyb

        da = _dot_nt(dys_ref[...], wd_ref[...])
        gv = g_ref[...].astype(F32)
        uv = u_ref[...].astype(F32)
        s = _sigmoid(gv)
        dg = _bf(da * uv * (s * (1.0 + gv * (1.0 - s))))
        du = _bf(da * (gv * s))
        dg_ref[...] = dg
        du_ref[...] = du
        part = _dot(dg, wg_ref[...]) + _dot(du, wu_ref[...])

        def finish(dh):
            xhat, r = _norm_parts(x_ref[...])
            dx, dw = _norm_bwd(dh, xhat, r, nw_ref[...])
            dx_ref[...] = dxo_ref[...] + dx

            @pl.when(i == 0)
            def _():
                dnw_ref[...] = dw

            @pl.when(i > 0)
            def _():
                dnw_ref[...] += dw

        _accumulate(acc_ref, part, j, nf, finish)

    row = pl.BlockSpec((tm, d), lambda i, j: (i, 0))
    vec = pl.BlockSpec((1, d), lambda i, j: (0, 0))
    wspec = pl.BlockSpec((tf, d), lambda i, j: (j, 0))
    hid = pl.BlockSpec((tm, tf), lambda i, j: (i, j))
    return _pcall(
        body, (dxo, x, nw, g, u, wg_t, wu_t, wd), name=name, grid=(t // tm, nf),
        in_specs=[row, row, vec, hid, hid, wspec, wspec, wspec],
        out_specs=[row, row, hid, hid, vec],
        out_shape=[jax.ShapeDtypeStruct((t, d), F32), jax.ShapeDtypeStruct((t, d), BF16),
                   jax.ShapeDtypeStruct((t, f), BF16), jax.ShapeDtypeStruct((t, f), BF16),
                   jax.ShapeDtypeStruct((1, d), F32)],
        scratch_shapes=[pltpu.VMEM((tm, d), BF16), pltpu.VMEM((tm, d), F32)],
        sem=("arbitrary", "arbitrary"), exchange=exchange)


def _shift_down(prev, cur, n):
    ext = jnp.concatenate([prev, cur], axis=0)
    return pltpu.roll(ext, n, axis=0)[prev.shape[0]:]


def _shift_up(cur, nxt, n):
    ext = jnp.concatenate([cur, nxt], axis=0)
    return pltpu.roll(ext, ext.shape[0] - n, axis=0)[:cur.shape[0]]


def _conv_specs(t, tb):
    hb = tb // CONV_HALO
    last = t // CONV_HALO - 1

    def tile(s):
        return pl.BlockSpec((None, tb, 128), lambda c, i: (s, i, c))

    def prev(s):
        return pl.BlockSpec((None, CONV_HALO, 128), lambda c, i: (s, jnp.maximum(i * hb - 1, 0), c))

    def nxt(s):
        return pl.BlockSpec((None, CONV_HALO, 128), lambda c, i: (s, jnp.minimum((i + 1) * hb, last), c))

    return tile, prev, nxt


def _conv_fwd(cols3, conv_w, *, name, tb=1024):
    _, t, bw = cols3.shape
    tb = _tile(t, tb)
    tile, prev, _ = _conv_specs(t, tb)

    def body(u_ref, b_ref, c_ref, up_ref, cp_ref, w_ref, y_ref):
        first = pl.program_id(1) == 0
        z = c_ref[...].astype(F32) * u_ref[...].astype(F32)
        zp = jnp.where(first, 0.0, cp_ref[...].astype(F32) * up_ref[...].astype(F32))
        conv = w_ref[0:1, :] * _shift_down(zp, z, 2) + w_ref[1:2, :] * _shift_down(zp, z, 1) + w_ref[2:3, :] * z
        y_ref[...] = _bf(b_ref[...].astype(F32) * conv)

    return pl.pallas_call(
        body, name=name, grid=(bw // 128, t // tb),
        in_specs=[tile(0), tile(1), tile(2), prev(0), prev(2), pl.BlockSpec((3, 128), lambda c, i: (0, c))],
        out_specs=pl.BlockSpec((tb, 128), lambda c, i: (i, c)),
        out_shape=jax.ShapeDtypeStruct((t, bw), BF16), compiler_params=_cparams("parallel", "parallel"),
    )(cols3, cols3, cols3, cols3, cols3, conv_w)


def _conv_bwd(cols3, conv_w, dy, *, name, tb=1024):
    _, t, bw = cols3.shape
    tb = _tile(t, tb)
    nt = t // tb
    tile, prev, nxt = _conv_specs(t, tb)
    hb = tb // CONV_HALO
    last = t // CONV_HALO - 1

    def body(u_ref, b_ref, c_ref, up_ref, cp_ref, bn_ref, dy_ref, dyn_ref, w_ref, d3_ref, dw_ref):
        i = pl.program_id(1)
        uv, bv, cv = u_ref[...].astype(F32), b_ref[...].astype(F32), c_ref[...].astype(F32)
        dyv = dy_ref[...].astype(F32)
        z = cv * uv
        zp = jnp.where(i == 0, 0.0, cp_ref[...].astype(F32) * up_ref[...].astype(F32))
        z1, z2 = _shift_down(zp, z, 1), _shift_down(zp, z, 2)
        w0, w1, w2 = w_ref[0:1, :], w_ref[1:2, :], w_ref[2:3, :]
        conv = w0 * z2 + w1 * z1 + w2 * z
        dconv = dyv * bv
        dconv_n = jnp.where(i == nt - 1, 0.0, dyn_ref[...].astype(F32) * bn_ref[...].astype(F32))
        dz = w2 * dconv + w1 * _shift_up(dconv, dconv_n, 1) + w0 * _shift_up(dconv, dconv_n, 2)
        d3_ref[0] = _bf(dz * cv)
        d3_ref[1] = _bf(dyv * conv)
        d3_ref[2] = _bf(dz * uv)
        dws = [jnp.sum(dconv * zz, axis=0, keepdims=True) for zz in (z2, z1, z)]

        @pl.when(i == 0)
        def _():
            for j in range(3):
                dw_ref[j:j + 1, :] = dws[j]

        @pl.when(i > 0)
        def _():
            for j in range(3):
                dw_ref[j:j + 1, :] += dws[j]

    dy_tile = pl.BlockSpec((None, tb, 128), lambda c, i: (0, i, c))
    dy_next = pl.BlockSpec((None, CONV_HALO, 128), lambda c, i: (0, jnp.minimum((i + 1) * hb, last), c))
    wspec = pl.BlockSpec((3, 128), lambda c, i: (0, c))
    return pl.pallas_call(
        body, name=name, grid=(bw // 128, nt),
        in_specs=[tile(0), tile(1), tile(2), prev(0), prev(2), nxt(1), dy_tile, dy_next, wspec],
        out_specs=[pl.BlockSpec((3, tb, 128), lambda c, i: (0, i, c)), wspec],
        out_shape=[jax.ShapeDtypeStruct((3, t, bw), BF16), jax.ShapeDtypeStruct((3, bw), F32)],
        compiler_params=_cparams("parallel", "arbitrary"),
    )(cols3, cols3, cols3, cols3, cols3, cols3, dy, dy, conv_w)


def _ret_consts():
    log_gamma = jnp.log1p(-jnp.exp2(-5.0 - jnp.arange(H_RET, dtype=F32)))
    pos = jnp.arange(CHUNK, dtype=F32)
    d_intra = jnp.exp(log_gamma[:, None, None] * jnp.abs(pos[:, None] - pos[None, :]))
    q_decay = jnp.exp(log_gamma[:, None] * (pos + 1.0))
    k_decay = jnp.exp(log_gamma[:, None] * (CHUNK - 1.0 - pos))
    chunk_decay = jnp.exp(log_gamma * CHUNK)
    wide = (H_RET, CHUNK, DK_RET)
    return (d_intra, jnp.broadcast_to(q_decay[:, :, None], wide), jnp.broadcast_to(k_decay[:, :, None], wide),
            jnp.broadcast_to(chunk_decay[:, None, None], (H_RET, 1, DK_RET)))


def _rope_tables(t):
    inv_freq = np.float32(ROPE_BASE) ** (-np.linspace(0.0, 1.0, DK_RET // 2, dtype=np.float32))
    ang = np.arange(t, dtype=np.float32)[:, None] * inv_freq[None, :]
    cos, sin = np.cos(ang), np.sin(ang)
    return jnp.asarray(np.concatenate([cos, cos], axis=1)), jnp.asarray(np.concatenate([-sin, sin], axis=1))


def _rope(v, cc, ss):
    return v * cc + pltpu.roll(v, DK_RET // 2, axis=1) * ss


def _ret_in_specs(tb, blk):
    def col(s):
        return pl.BlockSpec((None, tb, H_RET * DK_RET), lambda i: (s, blk(i), 0))

    tab = pl.BlockSpec((tb, DK_RET), lambda i: (blk(i), 0))
    return ([col(3), col(4), col(5), col(6), tab, tab,
             pl.BlockSpec((H_RET, CHUNK, CHUNK), lambda i: (0, 0, 0)),
             pl.BlockSpec((H_RET, CHUNK, DK_RET), lambda i: (0, 0, 0)),
             pl.BlockSpec((H_RET, CHUNK, DK_RET), lambda i: (0, 0, 0)),
             pl.BlockSpec((H_RET, 1, DK_RET), lambda i: (0, 0, 0))])


def _ret_fwd(cols3, tables, consts, *, name):
    _, t, bw = cols3.shape
    tb = _tile(t, RET_TB)
    ncb = tb // CHUNK
    scale = DK_RET ** -0.5

    def body(q_ref, k_ref, v_ref, g_ref, cc_ref, ss_ref, di_ref, qd_ref, kd_ref, cd_ref, y_ref, st_ref, state):
        @pl.when(pl.program_id(0) == 0)
        def _():
            state[...] = jnp.zeros_like(state)

        heads = range(H_RET)
        units = [(c, h) for c in range(ncb) for h in heads]
        every = range(len(units))
        rows = [pl.ds(c * CHUNK, CHUNK) for c, _ in units]
        lanes = [pl.ds(h * DK_RET, DK_RET) for _, h in units]
        hd = [h for _, h in units]
        cc, ss = [cc_ref[rw, :] for rw in rows], [ss_ref[rw, :] for rw in rows]
        qs = [_rope(q_ref[rows[u], lanes[u]].astype(F32), cc[u], ss[u]) * scale for u in every]
        kr = [_rope(k_ref[rows[u], lanes[u]].astype(F32), cc[u], ss[u]) for u in every]
        vb = [v_ref[rows[u], lanes[u]] for u in every]
        gv = [g_ref[rows[u], lanes[u]].astype(F32) for u in every]
        innerb = [_bf(_dot_nt(_bf(qs[u]), _bf(kr[u])) * di_ref[hd[u]]) for u in every]
        update = [_dot_tn(_bf(kr[u] * kd_ref[hd[u]]), vb[u]) for u in every]
        carried = [state[h] for h in heads]
        s_in = []
        for u, (c, h) in enumerate(units):
            s_in.append(carried[h])
            st_ref[h, c] = carried[h]
            carried[h] = carried[h] * cd_ref[h] + update[u]
        for h in heads:
            state[h] = carried[h]
        o = [_dot(innerb[u], vb[u]) + _dot(_bf(qs[u] * qd_ref[hd[u]]), _bf(s_in[u])) for u in every]
        for u in every:
            on = o[u] * lax.rsqrt(jnp.mean(o[u] * o[u], axis=-1, keepdims=True) + EPS)
            y_ref[rows[u], lanes[u]] = _bf(gv[u] * _sigmoid(gv[u]) * on)

    return pl.pallas_call(
        body, name=name, grid=(t // tb,),
        in_specs=_ret_in_specs(tb, lambda i: i),
        out_specs=[pl.BlockSpec((tb, bw), lambda i: (i, 0)),
                   pl.BlockSpec((H_RET, ncb, DK_RET, DK_RET), lambda i: (0, i, 0, 0))],
        out_shape=[jax.ShapeDtypeStruct((t, bw), BF16),
                   jax.ShapeDtypeStruct((H_RET, t // CHUNK, DK_RET, DK_RET), F32)],
        scratch_shapes=[pltpu.VMEM((H_RET, DK_RET, DK_RET), F32)],
        compiler_params=_cparams("arbitrary"),
    )(cols3, cols3, cols3, cols3, tables[0], tables[1], *consts)


def _ret_bwd(cols3, tables, consts, states, dy, *, name, exchange=None):
    _, t, bw = cols3.shape
    tb = _tile(t, RET_TB)
    ncb = tb // CHUNK
    nb = t // tb
    scale = DK_RET ** -0.5

    def body(q_ref, k_ref, v_ref, g_ref, cc_ref, ss_ref, di_ref, qd_ref, kd_ref, cd_ref, st_ref, dy_ref,
             d4_ref, dstate):
        @pl.when(pl.program_id(0) == 0)
        def _():
            dstate[...] = jnp.zeros_like(dstate)

        heads = range(H_RET)
        lanes_of = [pl.ds(h * DK_RET, DK_RET) for h in heads]
        di, qd, kd = [di_ref[h] for h in heads], [qd_ref[h] for h in heads], [kd_ref[h] for h in heads]
        carried = [dstate[h] for h in heads]
        group = RET_GROUP if ncb % RET_GROUP == 0 else 1
        for c0 in reversed(range(0, ncb, group)):
            units = [(c, h) for c in reversed(range(c0, c0 + group)) for h in heads]
            rows = [pl.ds(c * CHUNK, CHUNK) for c, _ in units]
            cc, ss = [cc_ref[rw, :] for rw in rows], [ss_ref[rw, :] for rw in rows]
            every = range(len(units))
            hd = [h for _, h in units]
            vb = [v_ref[rows[u], lanes_of[hd[u]]] for u in every]
            gv = [g_ref[rows[u], lanes_of[hd[u]]].astype(F32) for u in every]
            dyv = [dy_ref[rows[u], lanes_of[hd[u]]].astype(F32) for u in every]
            s_in = [_bf(st_ref[h, c]) for c, h in units]
            qs = [_rope(q_ref[rows[u], lanes_of[hd[u]]].astype(F32), cc[u], ss[u]) * scale for u in every]
            kr = [_rope(k_ref[rows[u], lanes_of[hd[u]]].astype(F32), cc[u], ss[u]) for u in every]
            qsb, krb = [_bf(v) for v in qs], [_bf(v) for v in kr]
            qdb = [_bf(qs[u] * qd[hd[u]]) for u in every]
            kdb = [_bf(kr[u] * kd[hd[u]]) for u in every]
            innerb = [_bf(_dot_nt(qsb[u], krb[u]) * di[hd[u]]) for u in every]
            o = [_dot(innerb[u], vb[u]) + _dot(qdb[u], s_in[u]) for u in every]
            r = [lax.rsqrt(jnp.mean(v * v, axis=-1, keepdims=True) + EPS) for v in o]
            on = [o[u] * r[u] for u in every]
            sg = [_sigmoid(v) for v in gv]
            dgv = [_bf(dyv[u] * on[u] * (sg[u] * (1.0 + gv[u] * (1.0 - sg[u])))) for u in every]
            don = [dyv[u] * (gv[u] * sg[u]) for u in every]
            dob = [_bf(r[u] * (don[u] - on[u] * jnp.mean(don[u] * on[u], axis=-1, keepdims=True))) for u in every]
            dinner = [_bf(_dot_nt(dob[u], vb[u]) * di[hd[u]]) for u in every]
            dqs = [_dot(dinner[u], krb[u]) + _dot_nt(dob[u], s_in[u]) * qd[hd[u]] for u in every]
            dkr = [_dot_tn(dinner[u], qsb[u]) for u in every]
            dv = [_dot_tn(innerb[u], dob[u]) for u in every]
            dnew = [_dot_tn(qdb[u], dob[u]) for u in every]
            for u in every:
                h = hd[u]
                dstb = _bf(carried[h])
                dv[u] = dv[u] + _dot(kdb[u], dstb)
                dkr[u] = dkr[u] + _dot_nt(vb[u], dstb) * kd[h]
                carried[h] = carried[h] * cd_ref[h] + dnew[u]
            for u in every:
                ln = lanes_of[hd[u]]
                d4_ref[0, rows[u], ln] = _bf(_rope_bwd(dqs[u] * scale, cc[u], ss[u]))
                d4_ref[1, rows[u], ln] = _bf(_rope_bwd(dkr[u], cc[u], ss[u]))
                d4_ref[2, rows[u], ln] = _bf(dv[u])
                d4_ref[3, rows[u], ln] = dgv[u]
        for h in heads:
            dstate[h] = carried[h]

    rev = lambda i: nb - 1 - i
    outs, moved = _pcall(
        body, (cols3, cols3, cols3, cols3, tables[0], tables[1], *consts, states, dy), name=name, grid=(nb,),
        in_specs=_ret_in_specs(tb, rev)
        + [pl.BlockSpec((H_RET, ncb, DK_RET, DK_RET), lambda i: (0, rev(i), 0, 0)),
           pl.BlockSpec((None, tb, bw), lambda i: (1, rev(i), 0))],
        out_specs=[pl.BlockSpec((4, tb, bw), lambda i: (0, rev(i), 0))],
        out_shape=[jax.ShapeDtypeStruct((4, t, bw), BF16)],
        scratch_shapes=[pltpu.VMEM((H_RET, DK_RET, DK_RET), F32)],
        sem=("arbitrary",), exchange=exchange)
    return outs[0], moved


def _rope_bwd(dv, cc, ss):
    return dv * cc + pltpu.roll(dv * ss, DK_RET // 2, axis=1)


def _att_window(i):
    return pl.multiple_of(jnp.maximum(i - ATT_LOOKBACK // ATT_QB, 0) * ATT_QB, ATT_QB)


def _att_mask(v):
    qchunk = (v * ATT_QB + lax.broadcasted_iota(jnp.int32, (ATT_QB, ATT_WIN), 0)) // CHUNK
    kchunk = lax.broadcasted_iota(jnp.int32, (ATT_QB, ATT_WIN), 1) // CHUNK
    return (kchunk <= qchunk) & (kchunk >= qchunk - N_PREV_CHUNKS)


def _bias_spec(layer, nvar):
    return pl.BlockSpec((None, None, 2, ATT_QB, ATT_WIN), lambda hp, i: (layer, jnp.minimum(i, nvar), hp, 0, 0))


def _att_fwd(cols3, bias3, *, name, exchange=None):
    _, t, bw = cols3.shape
    assert t % ATT_QB == 0 and t >= ATT_WIN
    scale = DH_ATT ** -0.5
    nvar = ATT_LOOKBACK // ATT_QB

    def body(q_ref, k_ref, v_ref, b_ref, y_ref, lse_ref):
        i = pl.program_id(1)
        ws = _att_window(i)
        q = q_ref[...].astype(F32)
        kw = k_ref[pl.ds(ws, ATT_WIN), :]
        vw = v_ref[pl.ds(ws, ATT_WIN), :]
        lane_head = lax.broadcasted_iota(jnp.int32, (ATT_QB, 128), 1) // DH_ATT
        out = jnp.zeros((ATT_QB, 128), F32)
        lse = jnp.zeros((ATT_QB, 128), F32)
        for hh in range(2):
            mine = lane_head == hh
            s = _dot_nt(_bf(jnp.where(mine, q, 0.0)), kw) * scale + b_ref[hh]
            mx = jnp.max(s, axis=-1, keepdims=True)
            p = jnp.exp(s - mx)
            l = jnp.sum(p, axis=-1, keepdims=True)
            out = jnp.where(mine, _dot(_bf(p), vw) / l, out)
            lse = jnp.where(mine, mx + jnp.log(l), lse)
        y_ref[...] = _bf(out)
        lse_ref[...] = lse

    kv = lambda s: pl.BlockSpec((None, t, 128), lambda hp, i: (s, 0, hp))
    return _pcall(
        body, (cols3, cols3, cols3, bias3[0]), name=name, grid=(H_ATT // 2, t // ATT_QB),
        in_specs=[pl.BlockSpec((None, ATT_QB, 128), lambda hp, i: (7, i, hp)), kv(8), kv(9),
                  _bias_spec(bias3[1], nvar)],
        out_specs=[pl.BlockSpec((ATT_QB, 128), lambda hp, i: (i, hp)),
                   pl.BlockSpec((None, ATT_QB, 128), lambda hp, i: (hp, i, 0))],
        out_shape=[jax.ShapeDtypeStruct((t, bw), BF16), jax.ShapeDtypeStruct((H_ATT // 2, t, 128), F32)],
        scratch_shapes=[], sem=("parallel", "arbitrary"), exchange=exchange)


def _att_bwd(cols3, bias3, y, lse, dy, *, name, exchange=None):
    _, t, bw = cols3.shape
    nq = t // ATT_QB
    scale = DH_ATT ** -0.5
    nvar = ATT_LOOKBACK // ATT_QB

    def body(q_ref, k_ref, v_ref, b_ref, y_ref, lse_ref, dy_ref, d3_ref, db_ref, dk_acc, dv_acc):
        i = pl.program_id(1)

        @pl.when(i == 0)
        def _():
            dk_acc[...] = jnp.zeros_like(dk_acc)
            dv_acc[...] = jnp.zeros_like(dv_acc)

        ws = _att_window(i)
        q = q_ref[...].astype(F32)
        kw = k_ref[pl.ds(ws, ATT_WIN), :]
        vw = v_ref[pl.ds(ws, ATT_WIN), :]
        do = dy_ref[...].astype(F32)
        dof = do * y_ref[...].astype(F32)
        lsev = lse_ref[...]
        lane_head = lax.broadcasted_iota(jnp.int32, (ATT_QB, 128), 1) // DH_ATT
        dq = jnp.zeros((ATT_QB, 128), F32)
        dk = jnp.zeros((ATT_WIN, 128), F32)
        dv = jnp.zeros((ATT_WIN, 128), F32)
        first = i <= nvar
        for hh in range(2):
            mine = lane_head == hh
            qh = _bf(jnp.where(mine, q, 0.0))
            doh = _bf(jnp.where(mine, do, 0.0))
            s = _dot_nt(qh, kw) * scale + b_ref[hh]
            lse_h = jnp.max(jnp.where(mine, lsev, NEG_INF), axis=-1, keepdims=True)
            p = jnp.exp(s - lse_h)
            delta = jnp.sum(jnp.where(mine, dof, 0.0), axis=-1, keepdims=True)
            ds = p * (_dot_nt(doh, vw) - delta)

            @pl.when(first)
            def _():
                db_ref[hh] = ds

            @pl.when(jnp.logical_not(first))
            def _():
                db_ref[hh] += ds

            dsb = _bf(ds * scale)
            dq = jnp.where(mine, _dot(dsb, kw), dq)
            dk = dk + _dot_tn(dsb, qh)
            dv = dv + _dot_tn(_bf(p), doh)
        d3_ref[0, pl.ds(pl.multiple_of(i * ATT_QB, ATT_QB), ATT_QB), :] = _bf(dq)
        dk_acc[pl.ds(ws, ATT_WIN), :] += dk
        dv_acc[pl.ds(ws, ATT_WIN), :] += dv

        @pl.when(i == nq - 1)
        def _():
            d3_ref[1] = _bf(dk_acc[...])
            d3_ref[2] = _bf(dv_acc[...])

    kv = lambda s: pl.BlockSpec((None, t, 128), lambda hp, i: (s, 0, hp))
    qrow = pl.BlockSpec((ATT_QB, 128), lambda hp, i: (i, hp))
    btile = pl.BlockSpec((None, 2, ATT_QB, ATT_WIN), lambda hp, i: (jnp.minimum(i, nvar), hp, 0, 0))
    return _pcall(
        body, (cols3, cols3, cols3, bias3[0], y, lse, dy), name=name, grid=(H_ATT // 2, nq),
        in_specs=[pl.BlockSpec((None, ATT_QB, 128), lambda hp, i: (7, i, hp)), kv(8), kv(9),
                  _bias_spec(bias3[1], nvar), qrow,
                  pl.BlockSpec((None, ATT_QB, 128), lambda hp, i: (hp, i, 0)),
                  pl.BlockSpec((None, ATT_QB, 128), lambda hp, i: (2, i, hp))],
        out_specs=[pl.BlockSpec((3, t, 128), lambda hp, i: (0, 0, hp)), btile],
        out_shape=[jax.ShapeDtypeStruct((3, t, bw), BF16),
                   jax.ShapeDtypeStruct((nvar + 1, H_ATT, ATT_QB, ATT_WIN), F32)],
        scratch_shapes=[pltpu.VMEM((t, 128), F32), pltpu.VMEM((t, 128), F32)],
        sem=("parallel", "arbitrary"), exchange=exchange)


SKEW_W = ATT_WIN + ATT_QB
REL_PAD = 384


def _rel_onehot(v):
    r = lax.broadcasted_iota(jnp.int32, (REL_PAD, SKEW_W), 0)
    j = lax.broadcasted_iota(jnp.int32, (REL_PAD, SKEW_W), 1)
    dist = jnp.where(j < ATT_WIN, v * ATT_QB - j, v * ATT_QB + SKEW_W - j)
    col = jnp.clip(dist, -REL_CLIP, REL_CLIP) + REL_CLIP
    return _bf(jnp.where(col == r, 1.0, 0.0))


def _split3(v):
    hi = _bf(v)
    rest = v - hi.astype(F32)
    mid = _bf(rest)
    return hi, mid, _bf(rest - mid.astype(F32))


def _skew8(a, forward):
    row = lax.broadcasted_iota(jnp.int32, a.shape, 0)
    for b in range(3):
        shift = (1 << b) if forward else SKEW_W - (1 << b)
        a = jnp.where(((row >> b) & 1) == 1, pltpu.roll(a, shift, axis=1), a)
    return a


def _toeplitz_rows(ext_row):
    a = _skew8(jnp.broadcast_to(ext_row, (8, SKEW_W)), True)
    while a.shape[0] < ATT_QB:
        a = jnp.concatenate([a, pltpu.roll(a, a.shape[0], axis=1)], axis=0)
    return a


def _diagonal_sums(tile):
    a = tile
    while a.shape[0] > 8:
        half = a.shape[0] // 2
        a = a[:half] + pltpu.roll(a[half:], SKEW_W - half, axis=1)
    return jnp.sum(_skew8(a, False), axis=0, keepdims=True)


def _bias_tiles(rel_bias, *, name, exchange=None):
    depth = rel_bias.shape[0]
    nvar = ATT_LOOKBACK // ATT_QB + 1
    rel = jnp.pad(rel_bias, ((0, 0), (0, 0), (0, REL_PAD - N_REL)))

    def body(rel_ref, o_ref, ext_ref):
        v = pl.program_id(1)
        onehot = _rel_onehot(v)
        ext_ref[...] = sum(_dot(part, onehot) for part in _split3(rel_ref[...]))
        valid = _att_mask(v)
        for h in range(H_ATT):
            o_ref[h] = jnp.where(valid, _toeplitz_rows(ext_ref[h:h + 1, :])[:, :ATT_WIN], NEG_INF)

    outs, moved = _pcall(
        body, (rel,), name=name, grid=(depth, nvar),
        in_specs=[pl.BlockSpec((None, H_ATT, REL_PAD), lambda l, v: (l, 0, 0))],
        out_specs=[pl.BlockSpec((None, None, H_ATT, ATT_QB, ATT_WIN), lambda l, v: (l, v, 0, 0, 0))],
        out_shape=[jax.ShapeDtypeStruct((depth, nvar, H_ATT, ATT_QB, ATT_WIN), F32)],
        scratch_shapes=[pltpu.VMEM((H_ATT, SKEW_W), F32)], sem=("parallel", "parallel"), exchange=exchange)
    return outs[0], moved


def _rel_bias_grad(dbias3, *, name):
    nvar = dbias3.shape[0]

    def body(db_ref, o_ref, diag_ref):
        v = pl.program_id(0)
        for h in range(H_ATT):
            tile = jnp.concatenate([db_ref[h], jnp.zeros((ATT_QB, ATT_QB), F32)], axis=1)
            diag_ref[h:h + 1, :] = _diagonal_sums(tile)
        onehot = _rel_onehot(v)
        part = sum(_dot_nt(p, onehot) for p in _split3(diag_ref[...]))

        @pl.when(v == 0)
        def _():
            o_ref[...] = part

        @pl.when(v > 0)
        def _():
            o_ref[...] += part

    out = pl.pallas_call(
        body, name=name, grid=(nvar,),
        in_specs=[pl.BlockSpec((None, H_ATT, ATT_QB, ATT_WIN), lambda v: (v, 0, 0, 0))],
        out_specs=pl.BlockSpec((H_ATT, REL_PAD), lambda v: (0, 0)),
        out_shape=jax.ShapeDtypeStruct((H_ATT, REL_PAD), F32),
        scratch_shapes=[pltpu.VMEM((H_ATT, SKEW_W), F32)], compiler_params=_cparams("arbitrary"),
    )(dbias3)
    return out[:, :N_REL]


def _merge_fwd(h, ys, wmg, wb_t, *, name, tm=512):
    t, d = h.shape
    bw = ys[0].shape[1]
    tm = _tile(t, tm)

    def body(h_ref, y0_ref, y1_ref, y2_ref, wg_ref, wb_ref, m_ref, s_ref, p_ref):
        hv = h_ref[...]
        total = jnp.zeros((tm, d), F32)
        for b, y_ref in enumerate((y0_ref, y1_ref, y2_ref)):
            s = _sigmoid(_dot(hv, wg_ref[b]))
            p = _dot_nt(y_ref[...], wb_ref[b])
            s_ref[b] = _bf(s)
            p_ref[b] = _bf(p)
            total = total + s * p
        m_ref[...] = _bf(total)

    row = pl.BlockSpec((tm, d), lambda i: (i, 0))
    yrow = pl.BlockSpec((tm, bw), lambda i: (i, 0))
    three = pl.BlockSpec((3, tm, d), lambda i: (0, i, 0))
    return pl.pallas_call(
        body, name=name, grid=(t // tm,),
        in_specs=[row, yrow, yrow, yrow, pl.BlockSpec((3, d, d), lambda i: (0, 0, 0)),
                  pl.BlockSpec((3, d, bw), lambda i: (0, 0, 0))],
        out_specs=[row, three, three],
        out_shape=[jax.ShapeDtypeStruct((t, d), BF16), jax.ShapeDtypeStruct((3, t, d), BF16),
                   jax.ShapeDtypeStruct((3, t, d), BF16)],
        compiler_params=_cparams("parallel"),
    )(h, *ys, wmg, wb_t)


def _merge_dwb(dp3, ys, *, name, tm=1024, tk=TOKEN_TK):
    nb, t, m = dp3.shape
    n = ys[0].shape[1]
    tm, tk = _tile(m, tm), _tile(t, tk)
    nk = t // tk

    def body(a_ref, *refs):
        o_ref, acc_ref = refs[nb], refs[nb + 1]
        k = pl.program_id(2)

        def finish(total):
            o_ref[...] = total.astype(o_ref.dtype)

        for b in range(nb):
            @pl.when(pl.program_id(0) == b)
            def _(y_ref=refs[b]):
                _accumulate(acc_ref, _dot_tn(a_ref[...], y_ref[...]), k, nk, finish)

    def y_spec(b):
        return pl.BlockSpec((tk, n), lambda s, i, k: (jnp.where(s == b, k, jnp.where(s < b, 0, nk - 1)), 0))

    return pl.pallas_call(
        body, name=name, grid=(nb, m // tm, nk),
        in_specs=[pl.BlockSpec((None, tk, tm), lambda s, i, k: (s, k, i))] + [y_spec(b) for b in range(nb)],
        out_specs=pl.BlockSpec((None, tm, n), lambda s, i, k: (s, i, 0)),
        out_shape=jax.ShapeDtypeStruct((nb, m, n), PARTIAL_DTYPE), scratch_shapes=[pltpu.VMEM((tm, n), F32)],
        compiler_params=_cparams("arbitrary", "arbitrary", "arbitrary"),
    )(dp3, *ys)


def _merge_bwd(dm, s3, p3, wmg, wb_t, *, name, tm=512):
    _, t, d = s3.shape
    bw = wb_t.shape[2]
    tm = _tile(t, tm)

    def body(dm_ref, s_ref, p_ref, wg_ref, wb_ref, dgp_ref, dp_ref, dy_ref, dh_ref):
        dmv = dm_ref[...].astype(F32)
        dh = jnp.zeros((tm, d), F32)
        for b in range(3):
            s = s_ref[b].astype(F32)
            dgp = _bf(dmv * p_ref[b].astype(F32) * s * (1.0 - s))
            dp = _bf(dmv * s)
            dgp_ref[b] = dgp
            dp_ref[b] = dp
            dy_ref[b] = _bf(_dot(dp, wb_ref[b]))
            dh = dh + _dot_nt(dgp, wg_ref[b])
        dh_ref[...] = dh

    row = pl.BlockSpec((tm, d), lambda i: (i, 0))
    three = pl.BlockSpec((3, tm, d), lambda i: (0, i, 0))
    return pl.pallas_call(
        body, name=name, grid=(t // tm,),
        in_specs=[row, three, three, pl.BlockSpec((3, d, d), lambda i: (0, 0, 0)),
                  pl.BlockSpec((3, d, bw), lambda i: (0, 0, 0))],
        out_specs=[three, three, pl.BlockSpec((3, tm, bw), lambda i: (0, i, 0)), row],
        out_shape=[jax.ShapeDtypeStruct((3, t, d), BF16), jax.ShapeDtypeStruct((3, t, d), BF16),
                   jax.ShapeDtypeStruct((3, t, bw), BF16), jax.ShapeDtypeStruct((t, d), F32)],
        compiler_params=_cparams("parallel"),
    )(dm, s3, p3, wmg, wb_t)


def _carried(hooks, slot, state=None):
    if not hooks or slot not in hooks:
        return None, lambda buf: None
    make, done = hooks[slot]
    return make(state), done


def _layer_fwd(x, p, aux, tag, hooks=None):
    ex, done = _carried(hooks, "ffn1")
    (x1, h1, g1, u1, a1), buf = _ffn_fwd(x, p["n1"], p["wg1"], p["wu1"], p["wd1"], name=f"ffn1_fwd_{tag}", exchange=ex)
    done(buf)
    h2 = _rmsnorm_fwd(x1, p["nmix"], name=f"mixnorm_fwd_{tag}")
    ex, done = _carried(hooks, "inproj")
    if ex is None:
        cols3 = _mm_to_slices(h2, p["win"], name=f"inproj_fwd_{tag}")
    else:
        cols3, buf = _mm_to_slices(h2, p["win"], name=f"inproj_fwd_{tag}", exchange=ex)
        done(buf)
    bias3 = p["bias3"]
    y_conv = _conv_fwd(cols3, p["conv_w"], name=f"conv_fwd_{tag}")
    y_ret, states = _ret_fwd(cols3, aux["rope"], aux["ret"], name=f"ret_fwd_{tag}")
    ex, done = _carried(hooks, "att")
    (y_att, lse), buf = _att_fwd(cols3, bias3, name=f"att_fwd_{tag}", exchange=ex)
    done(buf)
    merged, s3, p3 = _merge_fwd(h2, (y_conv, y_ret, y_att), p["wmg"], p["wb"], name=f"merge_fwd_{tag}")
    x2 = _mm_rows(merged, p["wout"], dims=NN, out_dtype=F32, res=x1, name=f"outproj_fwd_{tag}")
    ex, done = _carried(hooks, "ffn2")
    (x3, h3, g2, u2, a2), buf = _ffn_fwd(x2, p["n2"], p["wg2"], p["wu2"], p["wd2"], name=f"ffn2_fwd_{tag}", exchange=ex)
    done(buf)
    saved = dict(x0=x, h1=h1, g1=g1, u1=u1, a1=a1, x1=x1, h2=h2, cols3=cols3, bias3=bias3, y_conv=y_conv,
                 y_ret=y_ret, states=states, y_att=y_att, lse=lse, merged=merged, s3=s3, p3=p3, x2=x2, h3=h3,
                 g2=g2, u2=u2, a2=a2)
    return x3, saved


def _ffn_grads(dxo, x, h, g, u, a, p, which, tag, grads, all_grads, hooks):
    n = which
    ex, done = _carried(hooks, f"ffn{n}_bwd", all_grads)
    (dx, dyb, dg, du, grads["n" + n]), buf = _ffn_bwd(dxo, x, p["n" + n], g, u, p["wg" + n], p["wu" + n], p["wd" + n],
                                                     name=f"ffn{n}_{tag}_bwd", exchange=ex)
    done(buf)
    grads["wd" + n] = _mm_tn(a, dyb, name=f"ffn{n}_{tag}_dwd")
    for key, lhs in (("wg" + n, dg), ("wu" + n, du)):
        ex, done = _carried(hooks, f"ffn{n}_d{key[:2]}", all_grads)
        if ex is None:
            grads[key] = _mm_tn(lhs, h, name=f"ffn{n}_{tag}_d{key[:2]}")
        else:
            grads[key], buf = _mm_tn(lhs, h, name=f"ffn{n}_{tag}_d{key[:2]}", exchange=ex)
            done(buf)
    return dx


def _layer_bwd(dx3, p, s, aux, tag, grads, all_grads, hooks=None):
    dx2 = _ffn_grads(dx3, s["x2"], s["h3"], s["g2"], s["u2"], s["a2"], p, "2", tag, grads, all_grads, hooks)

    dm = _mm_rows(dx2, p["wout"], dims=NT, out_dtype=BF16, name=f"outproj_dm_{tag}")
    grads["wout"] = _mm_tn(s["merged"], dx2, name=f"outproj_dw_{tag}")
    dgp3, dp3, dy3, dh2 = _merge_bwd(dm, s["s3"], s["p3"], p["wmg"], p["wb"], name=f"merge_bwd_{tag}")
    grads["wmg"] = _mm_tn_batch(s["h2"], dgp3, name=f"merge_dwg_{tag}")
    grads["wb"] = _merge_dwb(dp3, (s["y_conv"], s["y_ret"], s["y_att"]), name=f"merge_dwb_{tag}")

    dconv3, grads["conv_w"] = _conv_bwd(s["cols3"], p["conv_w"], dy3, name=f"conv_bwd_{tag}")
    ex, done = _carried(hooks, "ret_bwd", all_grads)
    dret4, buf = _ret_bwd(s["cols3"], aux["rope"], aux["ret"], s["states"], dy3, name=f"ret_bwd_{tag}", exchange=ex)
    done(buf)
    ex, done = _carried(hooks, "att_bwd", all_grads)
    (datt3, dbias3), buf = _att_bwd(s["cols3"], s["bias3"], s["y_att"], s["lse"], dy3, name=f"att_bwd_{tag}",
                                    exchange=ex)
    done(buf)
    grads["rel_bias"] = _rel_bias_grad(dbias3, name=f"bias_grad_{tag}")

    dh2 = _inproj_dh((dconv3, dret4, datt3), p["win"], dh2, name=f"inproj_dh_{tag}")
    grads["win"] = jnp.concatenate([
        _mm_tn_slices(dconv3, s["h2"], name=f"inproj_dw_conv_{tag}"),
        _mm_tn_slices(dret4, s["h2"], name=f"inproj_dw_ret_{tag}"),
        _mm_tn_slices(datt3, s["h2"], name=f"inproj_dw_att_{tag}")], axis=0)
    dx1, grads["nmix"] = _rmsnorm_bwd(dh2, s["x1"], p["nmix"], dx2, name=f"mixnorm_bwd_{tag}")

    return _ffn_grads(dx1, s["x0"], s["h1"], s["g1"], s["u1"], s["a1"], p, "1", tag, grads, all_grads, hooks)


def _device_step(x, target, layers, final_norm, fwd_hooks=None, bwd_hooks=None):
    t = x.shape[0]
    depth = len(layers)
    aux = dict(rope=_rope_tables(t), ret=_ret_consts())
    ex, done = _carried(fwd_hooks[0] if fwd_hooks else None, "start")
    bias_all, buf = _bias_tiles(jnp.stack([p["rel_bias"] for p in layers]), name="bias_tiles", exchange=ex)
    done(buf)
    saved = []
    for l, p in enumerate(layers):
        p["bias3"] = (bias_all, l)
        x, s = _layer_fwd(x, p, aux, f"l{l}", fwd_hooks[l] if fwd_hooks else None)
        saved.append(s)
    dx, dfinal, loss_row = _loss_fwd_bwd(x, final_norm, target, name="loss_fwd_bwd")
    grads = [dict() for _ in range(depth)]
    for l in reversed(range(depth)):
        dx = _layer_bwd(dx, layers[l], saved[l], aux, f"l{l}", grads[l], grads, bwd_hooks[l] if bwd_hooks else None)
    return loss_row, dx, grads, dfinal


def _sum_slots(bufs, *, name, tr=512):
    n, r, cdim = bufs[0].shape
    depth = len(bufs)
    tr = _tile(r, tr)
    nt = r // tr

    def body(*refs):
        o_ref = refs[depth]
        for k in range(depth):
            @pl.when(pl.program_id(0) == k)
            def _(p_ref=refs[k]):
                acc = p_ref[0].astype(F32)
                for s in range(1, n):
                    acc = acc + p_ref[s].astype(F32)
                o_ref[...] = acc

    def spec(k):
        return pl.BlockSpec((n, tr, cdim), lambda l, i: (0, jnp.where(l == k, i, jnp.where(l < k, 0, nt - 1)), 0))

    return pl.pallas_call(
        body, name=name, grid=(depth, nt), in_specs=[spec(k) for k in range(depth)],
        out_specs=pl.BlockSpec((None, tr, cdim), lambda l, i: (l, i, 0)),
        out_shape=jax.ShapeDtypeStruct((depth, r, cdim), F32), compiler_params=_cparams("arbitrary", "arbitrary"),
    )(*bufs)


def _all_reduce_small(v, *, name):
    r = v.shape[0]

    def body(x_ref, o_ref, slots, send_sems, recv_sems):
        x, y, c, me = _my_position()
        slots[me] = x_ref[...]
        sends = []
        for k in range(1, N_DEV):
            peer, _ = _peer(x, y, c, k)
            cp = pltpu.make_async_remote_copy(src_ref=x_ref, dst_ref=slots.at[me], send_sem=send_sems.at[k - 1],
                                              recv_sem=recv_sems.at[k - 1], device_id=peer, device_id_type=MESH)
            cp.start()
            sends.append(cp)
        for k in range(1, N_DEV):
            peer, peer_id = _peer(x, y, c, k)
            pltpu.make_async_remote_copy(src_ref=x_ref, dst_ref=slots.at[peer_id], send_sem=send_sems.at[k - 1],
                                         recv_sem=recv_sems.at[k - 1], device_id=peer, device_id_type=MESH).wait_recv()
        for cp in sends:
            cp.wait_send()
        acc = slots[0]
        for s in range(1, N_DEV):
            acc = acc + slots[s]
        o_ref[...] = acc

    return pl.pallas_call(
        body, name=name, in_specs=[pl.BlockSpec(memory_space=pltpu.VMEM)],
        out_specs=pl.BlockSpec(memory_space=pltpu.VMEM), out_shape=jax.ShapeDtypeStruct((r, 128), F32),
        scratch_shapes=[pltpu.VMEM((N_DEV, r, 128), F32), pltpu.SemaphoreType.DMA((N_DEV - 1,)),
                        pltpu.SemaphoreType.DMA((N_DEV - 1,))],
    )(v)


def _adam_update(wv, gv, mv, vv, d_ref, mo_ref, vo_ref):
    mn = ADAM_B1 * mv + (1.0 - ADAM_B1) * gv
    vn = ADAM_B2 * vv + (1.0 - ADAM_B2) * (gv * gv)
    m_hat = mn / (1.0 - ADAM_B1 ** ADAM_STEP)
    v_hat = vn / (1.0 - ADAM_B2 ** ADAM_STEP)
    d_ref[...] = -ADAM_LR * (m_hat / (jnp.sqrt(v_hat) + ADAM_EPS) + ADAM_WD * wv)
    mo_ref[...] = mn
    vo_ref[...] = vn


def _adamw_of_partials(w, bufs, m, v, *, name, tr=512):
    depth, r, cdim = w.shape
    n = bufs[0].shape[0]
    tr = _tile(r, tr)
    nt = r // tr

    def body(w_ref, m_ref, v_ref, *refs):
        d_ref, mo_ref, vo_ref, g_ref = refs[depth:]
        for k in range(depth):
            @pl.when(pl.program_id(0) == k)
            def _(p_ref=refs[k]):
                gv = p_ref[0].astype(F32)
                for s in range(1, n):
                    gv = gv + p_ref[s].astype(F32)
                g_ref[...] = gv
                _adam_update(w_ref[...], gv, m_ref[...], v_ref[...], d_ref, mo_ref, vo_ref)

    def part_spec(k):
        return pl.BlockSpec((n, tr, cdim), lambda l, i: (0, jnp.where(l == k, i, jnp.where(l < k, 0, nt - 1)), 0))

    spec = pl.BlockSpec((None, tr, cdim), lambda l, i: (l, i, 0))
    return pl.pallas_call(
        body, name=name, grid=(depth, nt), in_specs=[spec] * 3 + [part_spec(k) for k in range(depth)],
        out_specs=[spec] * 4, out_shape=[jax.ShapeDtypeStruct((depth, r, cdim), F32)] * 4,
        compiler_params=_cparams("arbitrary", "arbitrary"),
    )(w, m, v, *bufs)


def _adamw(w, g, m, v, *, name, tr=256):
    shape = w.shape
    cdim = shape[-1]
    w2, g2, m2, v2 = (a.reshape(-1, cdim) for a in (w, g, m, v))
    r = w2.shape[0]
    tr = _tile(r, tr) if r % 8 == 0 else r

    def body(w_ref, g_ref, m_ref, v_ref, d_ref, mo_ref, vo_ref):
        _adam_update(w_ref[...], g_ref[...], m_ref[...], v_ref[...], d_ref, mo_ref, vo_ref)

    spec = pl.BlockSpec((tr, cdim), lambda i: (i, 0))
    outs = pl.pallas_call(
        body, name=name, grid=(r // tr,), in_specs=[spec] * 4, out_specs=[spec] * 3,
        out_shape=[jax.ShapeDtypeStruct((r, cdim), F32)] * 3, compiler_params=_cparams("parallel"),
    )(w2, g2, m2, v2)
    return tuple(o.reshape(shape) for o in outs)


GROUPS = {"f1": ("wg1", "wu1", "wd1"), "in": ("win",), "mg": ("wb", "wmg", "wout"), "f2": ("wg2", "wu2", "wd2")}
BIG = tuple(nm for members in GROUPS.values() for nm in members)
ROWS_DOMAIN = ("wg1", "wu1", "wd1", "wmg", "wout", "wg2", "wu2", "wd2")


def _to_rows(name, w, d):
    depth = w.shape[0]
    if name in ("wg1", "wu1", "wg2", "wu2", "win"):
        return w.transpose(0, 2, 1)
    if name == "wb":
        return w.transpose(0, 1, 3, 2).reshape(depth, -1, d)
    if name == "wmg":
        return w.reshape(depth, -1, d)
    return w


def _from_rows(name, rows, d):
    depth = rows.shape[0]
    if name in ("wg1", "wu1", "wg2", "wu2", "win"):
        return rows.transpose(0, 2, 1)
    if name == "wb":
        return rows.reshape(depth, 3, -1, BRANCH_W).transpose(0, 1, 3, 2)
    if name == "wmg":
        return rows.reshape(depth, 3, -1, d)
    return rows


def _full_from_gathered(name, g, d):
    if name == "wb":
        return g.reshape(N_DEV, 3, -1, BRANCH_W).transpose(1, 0, 2, 3).reshape(3, d, BRANCH_W)
    if name == "wmg":
        return g.reshape(N_DEV, 3, -1, d).transpose(1, 0, 2, 3).reshape(3, d, d)
    return g.reshape(-1, d)


def _gathered_from_full(name, full, d):
    if name == "wb":
        return full.reshape(3, N_DEV, -1, BRANCH_W).transpose(1, 0, 2, 3).reshape(N_DEV, -1, d)
    if name == "wmg":
        return full.reshape(3, N_DEV, -1, d).transpose(1, 0, 2, 3).reshape(N_DEV, -1, d)
    return full.reshape(N_DEV, -1, d)


def kernel(x, ffn1_norm, ffn1_w_gate, ffn1_w_up, ffn1_w_down, mix_norm, w_in, conv_w, rel_bias, w_branch, w_merge_gate, w_out, ffn2_norm, ffn2_w_gate, ffn2_w_up, ffn2_w_down, final_norm, loss_target, m_ffn1_norm, m_ffn1_w_gate, m_ffn1_w_up, m_ffn1_w_down, m_mix_norm, m_w_in, m_conv_w, m_rel_bias, m_w_branch, m_w_merge_gate, m_w_out, m_ffn2_norm, m_ffn2_w_gate, m_ffn2_w_up, m_ffn2_w_down, m_final_norm, v_ffn1_norm, v_ffn1_w_gate, v_ffn1_w_up, v_ffn1_w_down, v_mix_norm, v_w_in, v_conv_w, v_rel_bias, v_w_branch, v_w_merge_gate, v_w_out, v_ffn2_norm, v_ffn2_w_gate, v_ffn2_w_up, v_ffn2_w_down, v_final_norm):
    names = ["ffn1_norm", "ffn1_w_gate", "ffn1_w_up", "ffn1_w_down", "mix_norm", "w_in", "conv_w", "rel_bias",
             "w_branch", "w_merge_gate", "w_out", "ffn2_norm", "ffn2_w_gate", "ffn2_w_up", "ffn2_w_down", "final_norm"]
    weights = dict(zip(names, (ffn1_norm, ffn1_w_gate, ffn1_w_up, ffn1_w_down, mix_norm, w_in, conv_w, rel_bias,
                               w_branch, w_merge_gate, w_out, ffn2_norm, ffn2_w_gate, ffn2_w_up, ffn2_w_down,
                               final_norm)))
    m_in = dict(zip(names, (m_ffn1_norm, m_ffn1_w_gate, m_ffn1_w_up, m_ffn1_w_down, m_mix_norm, m_w_in, m_conv_w,
                            m_rel_bias, m_w_branch, m_w_merge_gate, m_w_out, m_ffn2_norm, m_ffn2_w_gate,
                            m_ffn2_w_up, m_ffn2_w_down, m_final_norm)))
    v_in = dict(zip(names, (v_ffn1_norm, v_ffn1_w_gate, v_ffn1_w_up, v_ffn1_w_down, v_mix_norm, v_w_in, v_conv_w,
                            v_rel_bias, v_w_branch, v_w_merge_gate, v_w_out, v_ffn2_norm, v_ffn2_w_gate,
                            v_ffn2_w_up, v_ffn2_w_down, v_final_norm)))
    big_of = dict(wg1="ffn1_w_gate", wu1="ffn1_w_up", wd1="ffn1_w_down", win="w_in", wb="w_branch",
                  wmg="w_merge_gate", wout="w_out", wg2="ffn2_w_gate", wu2="ffn2_w_up", wd2="ffn2_w_down")
    depth = ffn1_norm.shape[0]
    d = x.shape[-1]
    xs = x.reshape(-1, d)
    target = loss_target.reshape(-1, d)
    _, _, _, me = _my_position()

    shard_rows = {nm: _to_rows(nm, weights[big_of[nm]], d).astype(BF16) for nm in BIG}

    conv_cols = conv_w.shape[-1]
    conv_full, = _exchange("gather", [conv_w.reshape(depth * 3, conv_cols)], name="gather_conv_w")
    conv_full = conv_full.reshape(N_DEV, depth, 3, conv_cols).transpose(1, 2, 0, 3).reshape(depth, 3, -1)
    layers = [dict(n1=ffn1_norm[l][None], nmix=mix_norm[l][None], n2=ffn2_norm[l][None], conv_w=conv_full[l],
                   rel_bias=rel_bias[l]) for l in range(depth)]

    def gather_hook(l, members, kind="gather"):
        def done(bufs):
            layers[l].update({nm: _full_from_gathered(nm, buf, d) for nm, buf in zip(members, bufs)})

        return (lambda _: (kind, [shard_rows[nm] for nm in members], [l] * len(members))), done

    fwd_hooks = []
    for l in range(depth):
        hooks = {"ffn1": gather_hook(l, ("win", "wout", "wd2")), "inproj": gather_hook(l, ("wb", "wmg")),
                 "att": gather_hook(l, ("wg2", "wu2"))}
        if l == 0:
            hooks["start"] = gather_hook(0, GROUPS["f1"], "gather2")
        if l + 1 < depth:
            hooks["ffn2"] = gather_hook(l + 1, GROUPS["f1"])
        fwd_hooks.append(hooks)

    received = {}

    def scatter_hook(l, members):
        def make(all_grads):
            return "scatter", [_gathered_from_full(nm, all_grads[l][nm], d) for nm in members], [None] * len(members)

        return make, (lambda bufs: received.update({(l, nm): buf for nm, buf in zip(members, bufs)}))

    bwd_hooks = []
    for l in range(depth):
        hooks = {"att_bwd": scatter_hook(l, GROUPS["f2"]), "ret_bwd": scatter_hook(l, ("wb", "wout")),
                 "ffn1_bwd": scatter_hook(l, ("win", "wmg"))}
        if l + 1 < depth:
            hooks["ffn2_bwd"] = scatter_hook(l + 1, GROUPS["f1"])
        if l == 0:
            hooks["ffn1_dwg"] = scatter_hook(0, ("wd1",))
            hooks["ffn1_dwu"] = scatter_hook(0, ("wg1",))
        bwd_hooks.append(hooks)

    loss_row, dx, grads, dfinal = _device_step(xs, target, layers, final_norm[None], fwd_hooks, bwd_hooks)
    received[(0, "wu1")], = _exchange("scatter", [_gathered_from_full("wu1", grads[0]["wu1"], d)], name="scatter_l0_wu1")

    partials = {nm: [received[(l, nm)] for l in range(depth)] for nm in BIG}
    grad_w = {big_of[nm]: _from_rows(nm, _sum_slots(partials[nm], name=f"sum_grads_{nm}"), d)
              for nm in BIG if nm not in ROWS_DOMAIN}

    small = {"ffn1_norm": jnp.stack([grads[l]["n1"][0] for l in range(depth)]),
             "mix_norm": jnp.stack([grads[l]["nmix"][0] for l in range(depth)]),
             "ffn2_norm": jnp.stack([grads[l]["n2"][0] for l in range(depth)]),
             "final_norm": dfinal[0],
             "rel_bias": jnp.stack([grads[l]["rel_bias"] for l in range(depth)]),
             "conv_w": jnp.stack([grads[l]["conv_w"] for l in range(depth)]),
             "loss": loss_row[0, :1]}
    order = list(small)
    flat = jnp.concatenate([small[k].reshape(-1) for k in order])
    pad = (-flat.shape[0]) % 1024
    summed = _all_reduce_small(jnp.pad(flat, (0, pad)).reshape(-1, 128), name="reduce_small").reshape(-1)
    pos = 0
    for k in order:
        n = small[k].size
        small[k] = summed[pos:pos + n].reshape(small[k].shape)
        pos += n
    small["conv_w"] = lax.dynamic_slice_in_dim(small["conv_w"], me * conv_cols, conv_cols, axis=2)

    grad_w.update({k: small[k] for k in order if k != "loss"})
    delta, new_m, new_v = {}, {}, {}
    short = {v: k for k, v in big_of.items()}
    for nm in names:
        key = short.get(nm)
        if key in ROWS_DOMAIN:
            outs = _adamw_of_partials(_to_rows(key, weights[nm], d), partials[key], _to_rows(key, m_in[nm], d),
                                      _to_rows(key, v_in[nm], d), name=f"adamw_{nm}")
            delta[nm], new_m[nm], new_v[nm], grad_w[nm] = (_from_rows(key, o, d) for o in outs)
        else:
            delta[nm], new_m[nm], new_v[nm] = _adamw(weights[nm], grad_w[nm], m_in[nm], v_in[nm], name=f"adamw_{nm}")
    return (small["loss"].reshape(()), dx.reshape(x.shape), *[grad_w[n] for n in names], *[delta[n] for n in names],
            *[new_m[n] for n in names], *[new_v[n] for n in names])
```
